```python
import jax, jax.numpy as jnp
from jax import lax
import numpy as np

D_MODEL = 2048
BATCH = 4
SEQ = 2048
DEPTH = 2
DEC_BATCH = 128
DEC_SEQ = 8
PAST_LEN = 16384
PAGE_SIZE = 128

RET_HEADS = 8
RET_DK = D_MODEL // RET_HEADS
RET_DV = D_MODEL // RET_HEADS
RET_QK_WIDTH = RET_HEADS * RET_DK
RET_V_WIDTH = RET_HEADS * RET_DV
RET_CHUNK = 128
ROPE_BASE = 10000.0
LRU_WIDTH = D_MODEL
LRU_BLOCKS = 16
LRU_BLOCK = LRU_WIDTH // LRU_BLOCKS
LRU_C = 8.0
CONV_W = 4
FF_DENSE = 5632
N_EXPERTS = 8
TOP_K = 2
FF_EXPERT = 2816
N_DENSE = (DEPTH + 1) // 2
N_MOE = DEPTH // 2
EPS = 1e-6
IN_COLS = 2 * RET_QK_WIDTH + 2 * RET_V_WIDTH + 2 * LRU_WIDTH + 2 * D_MODEL

kernel_name = "hybrid_retention_rglru_moe_step"


def rms_norm(x, g):
    x32 = x.astype(jnp.float32)
    y = x32 * lax.rsqrt(jnp.mean(x32 * x32, axis=-1, keepdims=True) + EPS)
    return (y * g.astype(jnp.float32)).astype(x.dtype)


def rotary(x, pos):
    half = RET_DK // 2
    inv = ROPE_BASE ** (-jnp.arange(half, dtype=jnp.float32) / half)
    ang = pos[:, None] * inv[None, :]
    cos = jnp.cos(ang)[None, :, None, :]
    sin = jnp.sin(ang)[None, :, None, :]
    x1, x2 = x[..., :half], x[..., half:]
    return jnp.concatenate([x1 * cos - x2 * sin, x1 * sin + x2 * cos], axis=-1)


def retention(q, k, v, s0):
    B, T, H, DK = q.shape
    DV = v.shape[-1]
    C = RET_CHUNK if T % RET_CHUNK == 0 else T
    NC = T // C
    log_g = jnp.log1p(-jnp.exp2(-5.0 - jnp.arange(H, dtype=jnp.float32)))
    idx = jnp.arange(C)
    rel = idx[:, None] - idx[None, :]
    dmask = jnp.where(rel[None] >= 0,
                      jnp.exp(jnp.maximum(rel, 0)[None].astype(jnp.float32) * log_g[:, None, None]),
                      0.0)
    xi = jnp.exp((idx + 1).astype(jnp.float32)[:, None] * log_g[None, :])
    zeta = jnp.exp((C - 1 - idx).astype(jnp.float32)[:, None] * log_g[None, :])
    g_c = jnp.exp(C * log_g)
    k = k * (DK ** -0.5)
    qc = q.reshape(B, NC, C, H, DK).swapaxes(0, 1)
    kc = k.reshape(B, NC, C, H, DK).swapaxes(0, 1)
    vc = v.reshape(B, NC, C, H, DV).swapaxes(0, 1)

    def step(S, inp):
        qq, kk, vv = inp
        scores = jnp.einsum('bnhd,bmhd->bhnm', qq, kk) * dmask[None]
        intra = jnp.einsum('bhnm,bmhe->bnhe', scores, vv)
        cross = jnp.einsum('bnhd,bhde->bnhe', qq, S) * xi[None, :, :, None]
        S = g_c[None, :, None, None] * S + jnp.einsum('bmhd,bmhe->bhde', kk * zeta[None, :, :, None], vv)
        return S, intra + cross

    s_fin, o = lax.scan(step, s0.astype(jnp.float32), (qc, kc, vc))
    return o.swapaxes(0, 1).reshape(B, T, H, DV), s_fin


def group_norm_heads(o, g):
    mu = jnp.mean(o, axis=-1, keepdims=True)
    var = jnp.mean(jnp.square(o - mu), axis=-1, keepdims=True)
    y = (o - mu) * lax.rsqrt(var + EPS)
    B, T = o.shape[:2]
    return y.reshape(B, T, -1) * g.astype(jnp.float32)


def causal_conv(u, buf, w, b):
    T = u.shape[1]
    xp = jnp.concatenate([buf.astype(u.dtype), u], axis=1)
    out = b
    for j in range(CONV_W):
        out = out + xp[:, j:j + T] * w[j]
    return out, xp[:, T:]


def rg_lru(xc, h0, w_r, b_r, w_i, b_i, lam):
    B, T, W = xc.shape
    x32 = xc.astype(jnp.float32)
    xb = x32.reshape(B, T, LRU_BLOCKS, LRU_BLOCK)
    r = jax.nn.sigmoid(jnp.einsum('btnc,ncd->btnd', xb, w_r.astype(jnp.float32)).reshape(B, T, W) + b_r)
    i = jax.nn.sigmoid(jnp.einsum('btnc,ncd->btnd', xb, w_i.astype(jnp.float32)).reshape(B, T, W) + b_i)
    log_a = -LRU_C * r * jax.nn.softplus(-lam.astype(jnp.float32))
    a = jnp.exp(log_a)
    bx = jnp.sqrt(-jnp.expm1(2.0 * log_a)) * (i * x32)
    bx = bx.at[:, 0].add(a[:, 0] * h0.astype(jnp.float32))

    def comb(e1, e2):
        a1, b1 = e1
        a2, b2 = e2
        return a1 * a2, a2 * b1 + b2

    _, h = lax.associative_scan(comb, (a, bx), axis=1)
    return h, h[:, -1]


def swiglu(h, wg, wu, wd):
    return (jax.nn.silu(h @ wg) * (h @ wu)) @ wd


def moe(h, wr, wg, wu, wd):
    logits = (h @ wr).astype(jnp.float32)
    top_v, top_i = lax.top_k(logits, TOP_K)
    top_w = jax.nn.softmax(top_v, axis=-1)
    comb = jnp.sum(jax.nn.one_hot(top_i, N_EXPERTS, dtype=jnp.float32) * top_w[..., None], axis=-2)
    y = jnp.zeros_like(h)
    for e in range(N_EXPERTS):
        y = y + comb[..., e:e + 1].astype(h.dtype) * swiglu(h, wg[e], wu[e], wd[e])
    return y


def trunk(x, ret0, lru0, conv0, pos0, norm_mix, w_in, ret_gn, w_ret_o, conv_w, conv_b,
          w_rgate, b_rgate, w_igate, b_igate, lru_lambda, w_lru_o, w_out, norm_ffn,
          ffn_w_gate, ffn_w_up, ffn_w_down, moe_router, moe_w_gate, moe_w_up, moe_w_down, norm_final):
    B, T, _ = x.shape
    pos = pos0 + jnp.arange(T, dtype=jnp.float32)
    o1 = RET_QK_WIDTH
    o2 = o1 + RET_QK_WIDTH
    o3 = o2 + RET_V_WIDTH
    o4 = o3 + RET_V_WIDTH
    o5 = o4 + LRU_WIDTH
    o6 = o5 + LRU_WIDTH
    o7 = o6 + D_MODEL
    rets, lrus, convs = [], [], []
    for l in range(DEPTH):
        h = rms_norm(x, norm_mix[l])
        proj = h @ w_in[l]
        q, k, v, g, ux, uy, ga, gb = jnp.split(proj, [o1, o2, o3, o4, o5, o6, o7], axis=-1)
        q = rotary(q.astype(jnp.float32).reshape(B, T, RET_HEADS, RET_DK), pos)
        k = rotary(k.astype(jnp.float32).reshape(B, T, RET_HEADS, RET_DK), pos)
        v = v.astype(jnp.float32).reshape(B, T, RET_HEADS, RET_DV)
        o, s_new = retention(q, k, v, ret0[l])
        ya = (jax.nn.silu(g.astype(jnp.float32)) * group_norm_heads(o, ret_gn[l])).astype(x.dtype) @ w_ret_o[l]
        uc, conv_new = causal_conv(ux, conv0[l], conv_w[l], conv_b[l])
        hl, lru_new = rg_lru(uc, lru0[l], w_rgate[l], b_rgate[l], w_igate[l], b_igate[l], lru_lambda[l])
        yb = (hl.astype(x.dtype) * jax.nn.gelu(uy)) @ w_lru_o[l]
        x = x + (jax.nn.sigmoid(ga) * ya + jax.nn.sigmoid(gb) * yb) @ w_out[l]
        h2 = rms_norm(x, norm_ffn[l])
        if l % 2 == 0:
            j = l // 2
            x = x + swiglu(h2, ffn_w_gate[j], ffn_w_up[j], ffn_w_down[j])
        else:
            j = l // 2
            x = x + moe(h2, moe_router[j], moe_w_gate[j], moe_w_up[j], moe_w_down[j])
        rets.append(s_new)
        lrus.append(lru_new)
        convs.append(conv_new)
    y = rms_norm(x, norm_final)
    return y, jnp.stack(rets), jnp.stack(lrus), jnp.stack(convs)


def setup_inputs(seed: int = 0) -> dict:
    key = jax.random.key(seed)
    ks = jax.random.split(key, 32)
    f = jnp.float32

    def nrm(k, shape, scale):
        return jax.random.normal(k, shape, f) * scale

    u = jax.random.uniform(ks[15], (DEPTH, LRU_WIDTH), f, 0.9, 0.999)
    a = u ** (1.0 / LRU_C)
    lam = jnp.log(a) - jnp.log1p(-a)
    return {
        "x_prompt": nrm(ks[0], (BATCH, SEQ, D_MODEL), 1.0),
        "x_sample": nrm(ks[1], (DEC_BATCH, DEC_SEQ, D_MODEL), 1.0),
        "state_ret": nrm(ks[2], (DEPTH, DEC_BATCH, RET_HEADS, RET_DK, RET_DV), 1.0),
        "state_lru": nrm(ks[3], (DEPTH, DEC_BATCH, LRU_WIDTH), 0.5),
        "state_conv": nrm(ks[4], (DEPTH, DEC_BATCH, CONV_W - 1, LRU_WIDTH), 1.0),
        "norm_mix": 1.0 + nrm(ks[5], (DEPTH, D_MODEL), 0.02),
        "w_in": nrm(ks[6], (DEPTH, D_MODEL, IN_COLS), D_MODEL ** -0.5),
        "ret_gn": 1.0 + nrm(ks[7], (DEPTH, RET_V_WIDTH), 0.02),
        "w_ret_o": nrm(ks[8], (DEPTH, RET_V_WIDTH, D_MODEL), RET_V_WIDTH ** -0.5),
        "conv_w": nrm(ks[9], (DEPTH, CONV_W, LRU_WIDTH), CONV_W ** -0.5),
        "conv_b": nrm(ks[10], (DEPTH, LRU_WIDTH), 0.01),
        "w_rgate": nrm(ks[11], (DEPTH, LRU_BLOCKS, LRU_BLOCK, LRU_BLOCK), LRU_BLOCK ** -0.5),
        "b_rgate": nrm(ks[12], (DEPTH, LRU_WIDTH), 0.01),
        "w_igate": nrm(ks[13], (DEPTH, LRU_BLOCKS, LRU_BLOCK, LRU_BLOCK), LRU_BLOCK ** -0.5),
        "b_igate": nrm(ks[14], (DEPTH, LRU_WIDTH), 0.01),
        "lru_lambda": lam,
        "w_lru_o": nrm(ks[16], (DEPTH, LRU_WIDTH, D_MODEL), LRU_WIDTH ** -0.5),
        "w_out": nrm(ks[17], (DEPTH, D_MODEL, D_MODEL), D_MODEL ** -0.5),
        "norm_ffn": 1.0 + nrm(ks[18], (DEPTH, D_MODEL), 0.02),
        "ffn_w_gate": nrm(ks[19], (N_DENSE, D_MODEL, FF_DENSE), D_MODEL ** -0.5),
        "ffn_w_up": nrm(ks[20], (N_DENSE, D_MODEL, FF_DENSE), D_MODEL ** -0.5),
        "ffn_w_down": nrm(ks[21], (N_DENSE, FF_DENSE, D_MODEL), FF_DENSE ** -0.5),
        "moe_router": nrm(ks[22], (N_MOE, D_MODEL, N_EXPERTS), D_MODEL ** -0.5),
        "moe_w_gate": nrm(ks[23], (N_MOE, N_EXPERTS, D_MODEL, FF_EXPERT), D_MODEL ** -0.5),
        "moe_w_up": nrm(ks[24], (N_MOE, N_EXPERTS, D_MODEL, FF_EXPERT), D_MODEL ** -0.5),
        "moe_w_down": nrm(ks[25], (N_MOE, N_EXPERTS, FF_EXPERT, D_MODEL), FF_EXPERT ** -0.5),
        "norm_final": 1.0 + nrm(ks[26], (D_MODEL,), 0.02),
    }


def reference(x_prompt, x_sample, state_ret, state_lru, state_conv, norm_mix, w_in, ret_gn, w_ret_o,
              conv_w, conv_b, w_rgate, b_rgate, w_igate, b_igate, lru_lambda, w_lru_o, w_out, norm_ffn,
              ffn_w_gate, ffn_w_up, ffn_w_down, moe_router, moe_w_gate, moe_w_up, moe_w_down, norm_final):
    bp = x_prompt.shape[0]
    ret0 = jnp.zeros((DEPTH, bp, RET_HEADS, RET_DK, RET_DV), jnp.float32)
    lru0 = jnp.zeros((DEPTH, bp, LRU_WIDTH), jnp.float32)
    conv0 = jnp.zeros((DEPTH, bp, CONV_W - 1, LRU_WIDTH), x_prompt.dtype)
    y_prompt, ret_p, lru_p, conv_p = trunk(
        x_prompt, ret0, lru0, conv0, 0, norm_mix, w_in, ret_gn, w_ret_o, conv_w, conv_b,
        w_rgate, b_rgate, w_igate, b_igate, lru_lambda, w_lru_o, w_out, norm_ffn,
        ffn_w_gate, ffn_w_up, ffn_w_down, moe_router, moe_w_gate, moe_w_up, moe_w_down, norm_final)
    y_sample, ret_s, lru_s, conv_s = trunk(
        x_sample, state_ret, state_lru, state_conv, PAST_LEN, norm_mix, w_in, ret_gn, w_ret_o, conv_w, conv_b,
        w_rgate, b_rgate, w_igate, b_igate, lru_lambda, w_lru_o, w_out, norm_ffn,
        ffn_w_gate, ffn_w_up, ffn_w_down, moe_router, moe_w_gate, moe_w_up, moe_w_down, norm_final)
    return (y_prompt, y_sample, ret_p, lru_p, conv_p, ret_s, lru_s, conv_s)
```

```python
import functools

import jax
import jax.numpy as jnp
from jax import lax
from jax.experimental import pallas as pl
from jax.experimental.pallas import tpu as pltpu

F32 = jnp.float32
BF16 = jnp.bfloat16

ROPE_BASE = 10000.0
LRU_C = 8.0
EPS = 1e-6
RET_CHUNK = 128
PAST_LEN = 16384
SUBLANES = 8
LANES = 128
VMEM_LIMIT = 56 * 1024 * 1024


def _pick(n, prefs):
    for p in prefs:
        if n % p == 0:
            return p
    return n


def _params(*sem):
    return pltpu.CompilerParams(dimension_semantics=sem, vmem_limit_bytes=VMEM_LIMIT)


def _rmsnorm_kernel(x_ref, g_ref, o_ref):
    x = x_ref[...]
    y = x * lax.rsqrt(jnp.mean(x * x, axis=-1, keepdims=True) + EPS)
    o_ref[...] = (y * g_ref[...]).astype(o_ref.dtype)


def _rmsnorm(x, g, out_dtype):
    m, d = x.shape
    tm = _pick(m, (512, 256, 128, 64, 32, 16, 8))
    return pl.pallas_call(
        _rmsnorm_kernel,
        out_shape=jax.ShapeDtypeStruct((m, d), out_dtype),
        grid=(m // tm,),
        in_specs=[pl.BlockSpec((tm, d), lambda i: (i, 0)),
                  pl.BlockSpec((1, d), lambda i: (0, 0))],
        out_specs=pl.BlockSpec((tm, d), lambda i: (i, 0)),
        compiler_params=_params("arbitrary"),
        name="rmsnorm",
    )(x, g.reshape(1, d))


def _cast_w(w_ref, wb_ref):
    k = w_ref.shape[0]
    ck = _pick(k, (256, 128, 64, 32, 16))

    def body(c, carry):
        r = pl.multiple_of(c * ck, ck)
        wb_ref[pl.ds(r, ck), :] = w_ref[pl.ds(r, ck), :].astype(BF16)
        return carry

    lax.fori_loop(0, k // ck, body, 0)


def _mm_plain_kernel(x_ref, w_ref, o_ref, wb_ref):
    @pl.when(pl.program_id(1) == 0)
    def _():
        _cast_w(w_ref, wb_ref)

    o_ref[...] = jnp.dot(x_ref[...], wb_ref[...], preferred_element_type=F32).astype(o_ref.dtype)


def _mm_plain(x, w, tm, tn):
    m, k = x.shape
    n = w.shape[1]
    return pl.pallas_call(
        _mm_plain_kernel,
        out_shape=jax.ShapeDtypeStruct((m, n), BF16),
        grid=(n // tn, m // tm),
        in_specs=[pl.BlockSpec((tm, k), lambda j, i: (i, 0)),
                  pl.BlockSpec((k, tn), lambda j, i: (0, j))],
        out_specs=pl.BlockSpec((tm, tn), lambda j, i: (i, j)),
        scratch_shapes=[pltpu.VMEM((k, tn), BF16)],
        compiler_params=_params("arbitrary", "arbitrary"),
        name="mm_plain",
    )(x, w)


def _mm_resid_kernel(x_ref, w_ref, r_ref, *rest, scaled):
    if scaled:
        s_ref, o_ref, wb_ref = rest
    else:
        o_ref, wb_ref = rest

    @pl.when(pl.program_id(1) == 0)
    def _():
        _cast_w(w_ref, wb_ref)

    y = jnp.dot(x_ref[...], wb_ref[...], preferred_element_type=F32)
    if scaled:
        y = s_ref[...] * y
    o_ref[...] = r_ref[...] + y


def _mm_resid(x, w, r, tm, tn, scale=None):
    m, k = x.shape
    n = w.shape[1]
    in_specs = [pl.BlockSpec((tm, k), lambda j, i: (i, 0)),
                pl.BlockSpec((k, tn), lambda j, i: (0, j)),
                pl.BlockSpec((tm, tn), lambda j, i: (i, j))]
    args = [x, w, r]
    if scale is not None:
        in_specs.append(pl.BlockSpec((tm, 1), lambda j, i: (i, 0)))
        args.append(scale)
    return pl.pallas_call(
        functools.partial(_mm_resid_kernel, scaled=scale is not None),
        out_shape=jax.ShapeDtypeStruct((m, n), F32),
        grid=(n // tn, m // tm),
        in_specs=in_specs,
        out_specs=pl.BlockSpec((tm, tn), lambda j, i: (i, j)),
        scratch_shapes=[pltpu.VMEM((k, tn), BF16)],
        compiler_params=_params("arbitrary", "arbitrary"),
        name="mm_resid",
    )(*args)


def _mm_swiglu_kernel(x_ref, wg_ref, wu_ref, o_ref, wgb_ref, wub_ref):
    @pl.when(pl.program_id(1) == 0)
    def _():
        _cast_w(wg_ref, wgb_ref)
        _cast_w(wu_ref, wub_ref)

    x = x_ref[...]
    g = jnp.dot(x, wgb_ref[...], preferred_element_type=F32)
    u = jnp.dot(x, wub_ref[...], preferred_element_type=F32)
    o_ref[...] = (jax.nn.silu(g) * u).astype(o_ref.dtype)


def _mm_swiglu(x, wg, wu, tm, tn):
    m, k = x.shape
    n = wg.shape[1]
    return pl.pallas_call(
        _mm_swiglu_kernel,
        out_shape=jax.ShapeDtypeStruct((m, n), BF16),
        grid=(n // tn, m // tm),
        in_specs=[pl.BlockSpec((tm, k), lambda j, i: (i, 0)),
                  pl.BlockSpec((k, tn), lambda j, i: (0, j)),
                  pl.BlockSpec((k, tn), lambda j, i: (0, j))],
        out_specs=pl.BlockSpec((tm, tn), lambda j, i: (i, j)),
        scratch_shapes=[pltpu.VMEM((k, tn), BF16), pltpu.VMEM((k, tn), BF16)],
        compiler_params=_params("arbitrary", "arbitrary"),
        name="mm_swiglu",
    )(x, wg, wu)


def _mm_merge_kernel(a_ref, b_ref, wa_ref, wb_ref, ga_ref, gb_ref, o_ref, wab_ref, wbb_ref):
    @pl.when(pl.program_id(1) == 0)
    def _():
        _cast_w(wa_ref, wab_ref)
        _cast_w(wb_ref, wbb_ref)

    ya = jnp.dot(a_ref[...], wab_ref[...], preferred_element_type=F32)
    yb = jnp.dot(b_ref[...], wbb_ref[...], preferred_element_type=F32)
    ga = ga_ref[...].astype(F32)
    gb = gb_ref[...].astype(F32)
    o_ref[...] = (jax.nn.sigmoid(ga) * ya + jax.nn.sigmoid(gb) * yb).astype(o_ref.dtype)


def _mm_merge(a, b, wa, wb, proj, ga_col, gb_col, tm, tn):
    m, k = a.shape
    n = wa.shape[1]
    ga_blk, gb_blk = ga_col // tn, gb_col // tn
    return pl.pallas_call(
        _mm_merge_kernel,
        out_shape=jax.ShapeDtypeStruct((m, n), BF16),
        grid=(n // tn, m // tm),
        in_specs=[pl.BlockSpec((tm, k), lambda j, i: (i, 0)),
                  pl.BlockSpec((tm, k), lambda j, i: (i, 0)),
                  pl.BlockSpec((k, tn), lambda j, i: (0, j)),
                  pl.BlockSpec((k, tn), lambda j, i: (0, j)),
                  pl.BlockSpec((tm, tn), lambda j, i: (i, ga_blk + j)),
                  pl.BlockSpec((tm, tn), lambda j, i: (i, gb_blk + j))],
        out_specs=pl.BlockSpec((tm, tn), lambda j, i: (i, j)),
        scratch_shapes=[pltpu.VMEM((k, tn), BF16), pltpu.VMEM((k, tn), BF16)],
        compiler_params=_params("arbitrary", "arbitrary"),
        name="mm_merge",
    )(a, b, wa, wb, proj, proj)


def _rope_kernel(inv_ref, cos_ref, sin_ref, *, pos0):
    t, half = cos_ref.shape
    pos = lax.broadcasted_iota(jnp.int32, (t, half), 0).astype(F32) + F32(pos0)
    ang = pos * inv_ref[...]
    cos_ref[...] = jnp.cos(ang)
    sin_ref[...] = jnp.sin(ang)


def _rope_tables(t, half, pos0):
    inv = ROPE_BASE ** (-jnp.arange(half, dtype=F32) / half)
    return pl.pallas_call(
        functools.partial(_rope_kernel, pos0=pos0),
        out_shape=(jax.ShapeDtypeStruct((t, half), F32), jax.ShapeDtypeStruct((t, half), F32)),
        name="rope_tables",
    )(inv.reshape(1, half))


def _decay_tables(c, h, dk, dv):
    log_g = jnp.log1p(-jnp.exp2(-5.0 - jnp.arange(h, dtype=F32)))
    idx = jnp.arange(c)
    rel = idx[:, None] - idx[None, :]
    dmask = jnp.where(rel[None] >= 0,
                      jnp.exp(jnp.maximum(rel, 0)[None].astype(F32) * log_g[:, None, None]), 0.0)
    xi = jnp.exp((idx + 1).astype(F32)[None, :] * log_g[:, None])
    zeta = jnp.exp((c - 1 - idx).astype(F32)[None, :] * log_g[:, None])
    g_c = jnp.exp(c * log_g)
    xi_t = jnp.broadcast_to(xi[:, :, None], (h, c, dv))
    zeta_t = jnp.broadcast_to(zeta[:, :, None], (h, c, dk))
    return dmask, xi_t, zeta_t, g_c


def _ret_head(q, k, v, g, cos, sin, dm, xi, zt, gn, s, gc, dk):
    half = dk // 2
    q1, q2 = q[:, :half], q[:, half:]
    k1, k2 = k[:, :half], k[:, half:]
    qr = jnp.concatenate([q1 * cos - q2 * sin, q1 * sin + q2 * cos], axis=-1)
    kr = jnp.concatenate([k1 * cos - k2 * sin, k1 * sin + k2 * cos], axis=-1) * (dk ** -0.5)
    qb = qr.astype(BF16)
    kb = kr.astype(BF16)
    scores = lax.dot_general(qb, kb, (((1,), (1,)), ((), ())), preferred_element_type=F32)
    intra = jnp.dot((scores * dm).astype(BF16), v, preferred_element_type=F32)
    cross = jnp.dot(qb, s.astype(BF16), preferred_element_type=F32) * xi
    kz = (kr * zt).astype(BF16)
    upd = lax.dot_general(kz, v, (((0,), (0,)), ((), ())), preferred_element_type=F32)
    s_new = gc * s + upd
    o = intra + cross
    mu = jnp.mean(o, axis=-1, keepdims=True)
    d = o - mu
    var = jnp.mean(d * d, axis=-1, keepdims=True)
    y = d * lax.rsqrt(var + EPS) * gn
    return (jax.nn.silu(g) * y).astype(BF16), s_new


def _ret_prompt_kernel(gc_ref, q_ref, k_ref, v_ref, g_ref, cos_ref, sin_ref, dm_ref, xi_ref, zt_ref,
                       gn_ref, o_ref, s_ref, *, nh, dk, dv):
    @pl.when(pl.program_id(1) == 0)
    def _():
        s_ref[...] = jnp.zeros_like(s_ref)

    cos = cos_ref[...]
    sin = sin_ref[...]
    for h in range(nh):
        out, s_new = _ret_head(
            q_ref[:, h * dk:(h + 1) * dk].astype(F32), k_ref[:, h * dk:(h + 1) * dk].astype(F32),
            v_ref[:, h * dv:(h + 1) * dv], g_ref[:, h * dv:(h + 1) * dv].astype(F32),
            cos, sin, dm_ref[h], xi_ref[h], zt_ref[h], gn_ref[:, h * dv:(h + 1) * dv],
            s_ref[h], gc_ref[h], dk)
        o_ref[:, h * dv:(h + 1) * dv] = out
        s_ref[h] = s_new


def _ret_prompt(proj, gn, b, t, nh, dk, dv):
    c = RET_CHUNK if t % RET_CHUNK == 0 else t
    nc = t // c
    w = nh * dk
    cos, sin = _rope_tables(t, dk // 2, 0)
    dmask, xi_t, zeta_t, g_c = _decay_tables(c, nh, dk, dv)
    row = lambda bi, ci: bi * nc + ci
    full3 = lambda bi, ci: (0, 0, 0)
    return pl.pallas_call(
        functools.partial(_ret_prompt_kernel, nh=nh, dk=dk, dv=dv),
        out_shape=(jax.ShapeDtypeStruct((b * t, nh * dv), BF16),
                   jax.ShapeDtypeStruct((b, nh, dk, dv), F32)),
        grid=(b, nc),
        in_specs=[pl.BlockSpec(memory_space=pltpu.SMEM),
                  pl.BlockSpec((c, w), lambda bi, ci: (row(bi, ci), 0)),
                  pl.BlockSpec((c, w), lambda bi, ci: (row(bi, ci), 1)),
                  pl.BlockSpec((c, w), lambda bi, ci: (row(bi, ci), 2)),
                  pl.BlockSpec((c, w), lambda bi, ci: (row(bi, ci), 3)),
                  pl.BlockSpec((c, dk // 2), lambda bi, ci: (ci, 0)),
                  pl.BlockSpec((c, dk // 2), lambda bi, ci: (ci, 0)),
                  pl.BlockSpec((nh, c, c), full3),
                  pl.BlockSpec((nh, c, dv), full3),
                  pl.BlockSpec((nh, c, dk), full3),
                  pl.BlockSpec((1, nh * dv), lambda bi, ci: (0, 0))],
        out_specs=(pl.BlockSpec((c, nh * dv), lambda bi, ci: (row(bi, ci), 0)),
                   pl.BlockSpec((None, nh, dk, dv), lambda bi, ci: (bi, 0, 0, 0))),
        compiler_params=_params("arbitrary", "arbitrary"),
        name="ret_prompt",
    )(g_c, proj, proj, proj, proj, cos, sin, dmask, xi_t, zeta_t, gn.reshape(1, nh * dv))


def _ret_sample_kernel(gc_ref, q_ref, k_ref, v_ref, g_ref, cos_ref, sin_ref, dm_ref, xi_ref, zt_ref,
                       gn_ref, s0_ref, o_ref, s_ref, *, nh, dk, dv, bb, ts):
    cos = cos_ref[...]
    sin = sin_ref[...]
    q = q_ref[...].astype(F32)
    k = k_ref[...].astype(F32)
    v = v_ref[...].astype(F32)
    g = g_ref[...].astype(F32)
    for i in range(bb):
        r0, r1 = i * ts, (i + 1) * ts
        outs = []
        for h in range(nh):
            out, s_new = _ret_head(
                q[r0:r1, h * dk:(h + 1) * dk], k[r0:r1, h * dk:(h + 1) * dk],
                v[r0:r1, h * dv:(h + 1) * dv].astype(BF16), g[r0:r1, h * dv:(h + 1) * dv],
                cos, sin, dm_ref[h], xi_ref[h], zt_ref[h], gn_ref[:, h * dv:(h + 1) * dv],
                s0_ref[i, h], gc_ref[h], dk)
            s_ref[i, h] = s_new
            outs.append(out.astype(F32))
        o_ref[r0:r1, :] = jnp.concatenate(outs, axis=-1)


def _ret_sample(proj, gn, s0, row0, bs, ts, nh, dk, dv):
    bb = 2 if bs % 2 == 0 else 1
    rows = bb * ts
    w = nh * dk
    cos, sin = _rope_tables(ts, dk // 2, PAST_LEN)
    dmask, xi_t, zeta_t, g_c = _decay_tables(ts, nh, dk, dv)
    rb0 = row0 // rows
    full3 = lambda i: (0, 0, 0)
    return pl.pallas_call(
        functools.partial(_ret_sample_kernel, nh=nh, dk=dk, dv=dv, bb=bb, ts=ts),
        out_shape=(jax.ShapeDtypeStruct((bs * ts, nh * dv), F32),
                   jax.ShapeDtypeStruct((bs, nh, dk, dv), F32)),
        grid=(bs // bb,),
        in_specs=[pl.BlockSpec(memory_space=pltpu.SMEM),
                  pl.BlockSpec((rows, w), lambda i: (rb0 + i, 0)),
                  pl.BlockSpec((rows, w), lambda i: (rb0 + i, 1)),
                  pl.BlockSpec((rows, w), lambda i: (rb0 + i, 2)),
                  pl.BlockSpec((rows, w), lambda i: (rb0 + i, 3)),
                  pl.BlockSpec((ts, dk // 2), lambda i: (0, 0)),
                  pl.BlockSpec((ts, dk // 2), lambda i: (0, 0)),
                  pl.BlockSpec((nh, ts, ts), full3),
                  pl.BlockSpec((nh, ts, dv), full3),
                  pl.BlockSpec((nh, ts, dk), full3),
                  pl.BlockSpec((1, nh * dv), lambda i: (0, 0)),
                  pl.BlockSpec((bb, nh, dk, dv), lambda i: (i, 0, 0, 0))],
        out_specs=(pl.BlockSpec((rows, nh * dv), lambda i: (i, 0)),
                   pl.BlockSpec((bb, nh, dk, dv), lambda i: (i, 0, 0, 0))),
        compiler_params=_params("arbitrary"),
        name="ret_sample",
    )(g_c, proj, proj, proj, proj, cos, sin, dmask, xi_t, zeta_t, gn.reshape(1, nh * dv), s0)


def _lru_gates(uc, wr_ref, br_ref, wi_ref, bi_ref, lam_ref):
    nb, lb, _ = wr_ref.shape
    ucb = uc.astype(BF16)
    rl, il = [], []
    for n in range(nb):
        xb = ucb[:, n * lb:(n + 1) * lb]
        rl.append(jnp.dot(xb, wr_ref[n].astype(BF16), preferred_element_type=F32))
        il.append(jnp.dot(xb, wi_ref[n].astype(BF16), preferred_element_type=F32))
    r = jax.nn.sigmoid(jnp.concatenate(rl, axis=-1) + br_ref[...])
    i = jax.nn.sigmoid(jnp.concatenate(il, axis=-1) + bi_ref[...])
    log_a = -LRU_C * r * jax.nn.softplus(-lam_ref[...])
    a = jnp.exp(log_a)
    bx = jnp.sqrt(-jnp.tanh(log_a) * (a * a + 1.0)) * (i * uc)
    return a, bx


def _scan8(a, bx):
    rows = lax.broadcasted_iota(jnp.int32, a.shape, 0) & (SUBLANES - 1)
    d = 1
    while d < SUBLANES:
        a_sh = pltpu.roll(a, d, axis=0)
        b_sh = pltpu.roll(bx, d, axis=0)
        m = rows >= d
        bx = jnp.where(m, a * b_sh + bx, bx)
        a = jnp.where(m, a * a_sh, a)
        d *= 2
    return a, bx


def _lru_prompt_kernel(ux_ref, uy_ref, cw_ref, cb_ref, wr_ref, br_ref, wi_ref, bi_ref, lam_ref,
                       o_ref, hl_ref, hc_ref, ut_ref):
    r, cw = ux_ref.shape
    ncw = cw_ref.shape[0]

    @pl.when(pl.program_id(2) == 0)
    def _():
        hc_ref[...] = jnp.zeros_like(hc_ref)
        ut_ref[...] = jnp.zeros_like(ut_ref)

    u = ux_ref[...].astype(F32)
    ext = jnp.concatenate([ut_ref[...], u], axis=0)
    ut_ref[...] = u[r - SUBLANES:, :]
    uc = cb_ref[...]
    for j in range(ncw):
        s = ncw - 1 - j
        term = u if s == 0 else pltpu.roll(ext, s, axis=0)[SUBLANES:, :]
        uc = uc + term * cw_ref[j:j + 1, :]
    a, bx = _lru_gates(uc, wr_ref, br_ref, wi_ref, bi_ref, lam_ref)
    a, bx = _scan8(a, bx)
    carry = hc_ref[...]
    hs = []
    for blk in range(r // SUBLANES):
        lo = blk * SUBLANES
        hb = a[lo:lo + SUBLANES, :] * carry + bx[lo:lo + SUBLANES, :]
        hs.append(hb)
        carry = jnp.broadcast_to(hb[SUBLANES - 1:SUBLANES, :], (SUBLANES, cw))
    hc_ref[...] = carry
    hl_ref[...] = hs[-1]
    h = jnp.concatenate(hs, axis=0)
    o_ref[...] = (h * jax.nn.gelu(uy_ref[...].astype(F32))).astype(o_ref.dtype)


def _lru_sample_kernel(ux_ref, uy_ref, cw_ref, cb_ref, wr_ref, br_ref, wi_ref, bi_ref, lam_ref,
                       h0_ref, buf_ref, o_ref, hl_ref):
    r, cw = ux_ref.shape
    ncw = cw_ref.shape[0]
    rows = lax.broadcasted_iota(jnp.int32, (r, cw), 0) & (SUBLANES - 1)
    u = ux_ref[...].astype(F32)
    buf = buf_ref[...]
    uc = cb_ref[...]
    for j in range(ncw):
        s = ncw - 1 - j
        if s == 0:
            term = u
        else:
            term = jnp.where(rows >= s, pltpu.roll(u, s, axis=0),
                             pltpu.roll(buf, r - SUBLANES + s, axis=0))
        uc = uc + term * cw_ref[j:j + 1, :]
    a, bx = _lru_gates(uc, wr_ref, br_ref, wi_ref, bi_ref, lam_ref)
    a, bx = _scan8(a, bx)
    h = a * h0_ref[...] + bx
    hl_ref[...] = h
    o_ref[...] = (h * jax.nn.gelu(uy_ref[...].astype(F32))).astype(o_ref.dtype)


def _lru_common_specs(cw, ncw, lb, ux_blk, uy_blk, row_map, cb_of):
    nbc = cw // lb
    return [
        pl.BlockSpec(row_map[0], lambda *g: (row_map[1](*g), ux_blk + cb_of(*g))),
        pl.BlockSpec(row_map[0], lambda *g: (row_map[1](*g), uy_blk + cb_of(*g))),
        pl.BlockSpec((ncw, cw), lambda *g: (0, cb_of(*g))),
        pl.BlockSpec((1, cw), lambda *g: (0, cb_of(*g))),
        pl.BlockSpec((nbc, lb, lb), lambda *g: (cb_of(*g), 0, 0)),
        pl.BlockSpec((1, cw), lambda *g: (0, cb_of(*g))),
        pl.BlockSpec((nbc, lb, lb), lambda *g: (cb_of(*g), 0, 0)),
        pl.BlockSpec((1, cw), lambda *g: (0, cb_of(*g))),
        pl.BlockSpec((1, cw), lambda *g: (0, cb_of(*g))),
    ]


def _lru_prompt(proj, ux_col, conv_w, conv_b, w_r, b_r, w_i, b_i, lam, b, t):
    width = lam.shape[0]
    lb = w_r.shape[-1]
    ncw = conv_w.shape[0]
    cw = _pick(width, (512, 256, 128))
    r = _pick(t, (256, 128, 64, 32, 16, 8))
    nt = t // r
    vec = lambda x: x.reshape(1, width)
    specs = _lru_common_specs(cw, ncw, lb, ux_col // cw, (ux_col + width) // cw,
                              ((r, cw), lambda bi, cb, ti: bi * nt + ti), lambda bi, cb, ti: cb)
    return pl.pallas_call(
        _lru_prompt_kernel,
        out_shape=(jax.ShapeDtypeStruct((b * t, width), BF16),
                   jax.ShapeDtypeStruct((b, SUBLANES, width), F32)),
        grid=(b, width // cw, nt),
        in_specs=specs,
        out_specs=(pl.BlockSpec((r, cw), lambda bi, cb, ti: (bi * nt + ti, cb)),
                   pl.BlockSpec((None, SUBLANES, cw), lambda bi, cb, ti: (bi, 0, cb))),
        scratch_shapes=[pltpu.VMEM((SUBLANES, cw), F32), pltpu.VMEM((SUBLANES, cw), F32)],
        compiler_params=_params("arbitrary", "arbitrary", "arbitrary"),
        name="lru_prompt",
    )(proj, proj, conv_w, vec(conv_b), w_r, vec(b_r), w_i, vec(b_i), vec(lam))


def _lru_sample(proj, ux_col, row0, conv_w, conv_b, w_r, b_r, w_i, b_i, lam, h0, buf, bs, ts):
    assert ts == SUBLANES
    width = lam.shape[0]
    lb = w_r.shape[-1]
    ncw = conv_w.shape[0]
    cw = _pick(width, (512, 256, 128))
    m = bs * ts
    r = _pick(m, (256, 128, 64, 32, 16, 8))
    rb0 = row0 // r
    vec = lambda x: x.reshape(1, width)
    h0_rep = jnp.repeat(h0, ts, axis=0)
    buf_fr = jnp.pad(buf, ((0, 0), (ts - (ncw - 1), 0), (0, 0))).reshape(m, width)
    specs = _lru_common_specs(cw, ncw, lb, ux_col // cw, (ux_col + width) // cw,
                              ((r, cw), lambda ri, cb: rb0 + ri), lambda ri, cb: cb)
    specs += [pl.BlockSpec((r, cw), lambda ri, cb: (ri, cb)),
              pl.BlockSpec((r, cw), lambda ri, cb: (ri, cb))]
    return pl.pallas_call(
        _lru_sample_kernel,
        out_shape=(jax.ShapeDtypeStruct((m, width), BF16), jax.ShapeDtypeStruct((m, width), F32)),
        grid=(m // r, width // cw),
        in_specs=specs,
        out_specs=(pl.BlockSpec((r, cw), lambda ri, cb: (ri, cb)),
                   pl.BlockSpec((r, cw), lambda ri, cb: (ri, cb))),
        compiler_params=_params("arbitrary", "arbitrary"),
        name="lru_sample",
    )(proj, proj, conv_w, vec(conv_b), w_r, vec(b_r), w_i, vec(b_i), vec(lam), h0_rep, buf_fr)


def _router_kernel(h_ref, wr_ref, o_ref, *, ne):
    logits = jnp.dot(h_ref[...], wr_ref[...].astype(BF16), preferred_element_type=F32)
    lane = lax.broadcasted_iota(jnp.int32, logits.shape, 1).astype(F32)
    neg = F32(-jnp.inf)
    big = F32(LANES)
    l1 = jnp.where(lane < ne, logits, neg)
    m1 = jnp.max(l1, axis=-1, keepdims=True)
    i1 = jnp.min(jnp.where(l1 == m1, lane, big), axis=-1, keepdims=True)
    l2 = jnp.where(lane == i1, neg, l1)
    m2 = jnp.max(l2, axis=-1, keepdims=True)
    i2 = jnp.min(jnp.where(l2 == m2, lane, big), axis=-1, keepdims=True)
    e2 = jnp.exp(m2 - m1)
    den = 1.0 + e2
    o_ref[...] = jnp.where(lane == i1, 1.0 / den, 0.0) + jnp.where(lane == i2, e2 / den, 0.0)


def _router(h, wr):
    m, d = h.shape
    ne = wr.shape[1]
    assert ne <= LANES
    wr_pad = jnp.pad(wr, ((0, 0), (0, LANES - ne)))
    tm = _pick(m, (512, 256, 128, 64, 32, 16, 8))
    return pl.pallas_call(
        functools.partial(_router_kernel, ne=ne),
        out_shape=jax.ShapeDtypeStruct((m, LANES), F32),
        grid=(m // tm,),
        in_specs=[pl.BlockSpec((tm, d), lambda i: (i, 0)),
                  pl.BlockSpec((d, LANES), lambda i: (0, 0))],
        out_specs=pl.BlockSpec((tm, LANES), lambda i: (i, 0)),
        compiler_params=_params("arbitrary"),
        name="router",
    )(h, wr_pad)


def kernel(x_prompt, x_sample, state_ret, state_lru, state_conv, norm_mix, w_in, ret_gn, w_ret_o, conv_w, conv_b, w_rgate, b_rgate, w_igate, b_igate, lru_lambda, w_lru_o, w_out, norm_ffn, ffn_w_gate, ffn_w_up, ffn_w_down, moe_router, moe_w_gate, moe_w_up, moe_w_down, norm_final):
    bp, tp, d = x_prompt.shape
    bs, ts, _ = x_sample.shape
    depth, _, nh, dk, dv = state_ret.shape
    width = state_lru.shape[-1]
    ncw = conv_w.shape[1]
    ne = moe_router.shape[-1]
    assert dk == dv and nh * dk == width == d
    mp, ms = bp * tp, bs * ts
    m = mp + ms
    ux_col = 2 * nh * dk + 2 * nh * dv
    ga_col = ux_col + 2 * width

    tm = _pick(m, (1024, 512, 256, 128, 64, 32, 16, 8))
    tn_of = lambda n: _pick(n, (1024, 512, 256, 128))

    x = jnp.concatenate([x_prompt.reshape(mp, d), x_sample.reshape(ms, d)], axis=0)
    rets_p, rets_s, lrus_p, lrus_s, convs_p, convs_s = [], [], [], [], [], []
    for l in range(depth):
        h = _rmsnorm(x, norm_mix[l], BF16)
        proj = _mm_plain(h, w_in[l], tm, tn_of(w_in.shape[-1]))

        ya_p, ret_p = _ret_prompt(proj, ret_gn[l], bp, tp, nh, dk, dv)
        ya_s, ret_s = _ret_sample(proj, ret_gn[l], state_ret[l], mp, bs, ts, nh, dk, dv)
        ya = jnp.concatenate([ya_p, ya_s.astype(BF16)], axis=0)

        lru_args = (conv_w[l], conv_b[l], w_rgate[l], b_rgate[l], w_igate[l], b_igate[l], lru_lambda[l])
        yb_p, hl_p = _lru_prompt(proj, ux_col, *lru_args, bp, tp)
        yb_s, hl_s = _lru_sample(proj, ux_col, mp, *lru_args, state_lru[l], state_conv[l], bs, ts)
        yb = jnp.concatenate([yb_p, yb_s], axis=0)

        ux = proj[:, ux_col:ux_col + width]
        convs_p.append(ux[:mp].reshape(bp, tp, width)[:, tp - (ncw - 1):].astype(F32))
        convs_s.append(ux[mp:].reshape(bs, ts, width)[:, ts - (ncw - 1):].astype(F32))
        rets_p.append(ret_p)
        rets_s.append(ret_s)
        lrus_p.append(hl_p[:, -1])
        lrus_s.append(hl_s.reshape(bs, ts, width)[:, -1])

        z = _mm_merge(ya, yb, w_ret_o[l], w_lru_o[l], proj, ga_col, ga_col + d, tm, tn_of(d) // 2)
        x = _mm_resid(z, w_out[l], x, tm, tn_of(d))
        h2 = _rmsnorm(x, norm_ffn[l], BF16)
        j = l // 2
        if l % 2 == 0:
            ff = ffn_w_gate.shape[-1]
            u = _mm_swiglu(h2, ffn_w_gate[j], ffn_w_up[j], tm, _pick(ff, (512, 256, 128)))
            x = _mm_resid(u, ffn_w_down[j], x, _pick(m, (512, 256, 128, 64, 32, 16, 8)), _pick(d, (512, 256, 128)))
        else:
            fe = moe_w_gate.shape[-1]
            comb = _router(h2, moe_router[j])
            for e in range(ne):
                u = _mm_swiglu(h2, moe_w_gate[j, e], moe_w_up[j, e], tm, _pick(fe, (256, 128)))
                x = _mm_resid(u, moe_w_down[j, e], x, tm, _pick(d, (512, 256, 128)), scale=comb[:, e:e + 1])
    y = _rmsnorm(x, norm_final, F32)
    return (y[:mp].reshape(bp, tp, d), y[mp:].reshape(bs, ts, d),
            jnp.stack(rets_p), jnp.stack(lrus_p), jnp.stack(convs_p),
            jnp.stack(rets_s), jnp.stack(lrus_s), jnp.stack(convs_s))
```

```python
import functools

import jax
import jax.numpy as jnp
from jax import lax
from jax.experimental import pallas as pl
from jax.experimental.pallas import tpu as pltpu

F32 = jnp.float32
BF16 = jnp.bfloat16

ROPE_BASE = 10000.0
LRU_C = 8.0
EPS = 1e-6
RET_CHUNK = 128
PAST_LEN = 16384
TOP_K = 2
SUBLANES = 8
LANES = 128
VMEM_LIMIT = 56 * 1024 * 1024
MOE_TILE = 512
GATHER_ROWS = 256
COMBINE_ROWS = 128


def _pick(n, prefs):
    for p in prefs:
        if n % p == 0:
            return p
    return n


def _params(*sem):
    return pltpu.CompilerParams(dimension_semantics=sem, vmem_limit_bytes=VMEM_LIMIT)


def _pcall(body, *, name, grid, in_specs, args, out_specs, out_shape, sem, scratch=(), prev=(),
           num_scalar_prefetch=0):
    n_in = len(args)
    prev = tuple(prev) + (None,) * (len(out_shape) - len(prev))
    extra = [(oi, p) for oi, p in enumerate(prev) if p is not None]
    aliases = {n_in + e: oi for e, (oi, _) in enumerate(extra)}
    if extra:
        inner = body

        def body(*refs):
            return inner(*refs[:n_in], *refs[n_in + len(extra):])

    in_specs = list(in_specs) + [pl.BlockSpec(memory_space=pl.ANY)] * len(extra)
    if num_scalar_prefetch:
        grid_spec = pltpu.PrefetchScalarGridSpec(
            num_scalar_prefetch=num_scalar_prefetch, grid=grid, in_specs=in_specs[num_scalar_prefetch:],
            out_specs=tuple(out_specs), scratch_shapes=list(scratch))
        call = pl.pallas_call(body, grid_spec=grid_spec, out_shape=tuple(out_shape),
                              input_output_aliases=aliases, compiler_params=_params(*sem), name=name)
    else:
        call = pl.pallas_call(body, grid=grid, in_specs=in_specs, out_specs=tuple(out_specs),
                              out_shape=tuple(out_shape), scratch_shapes=list(scratch),
                              input_output_aliases=aliases, compiler_params=_params(*sem), name=name)
    return call(*args, *[p for _, p in extra])


def _wspec(pre, k, tn):
    return pl.BlockSpec((None,) * len(pre) + (k, tn), lambda j, i: pre + (0, j))


def _vspec(l, n):
    return pl.BlockSpec((None, 1, n), lambda *_: (l, 0, 0))


def _rms(x, g):
    return (x * lax.rsqrt(jnp.mean(x * x, axis=-1, keepdims=True) + EPS)) * g


def _rmsnorm_kernel(x_ref, g_ref, o_ref):
    o_ref[...] = _rms(x_ref[...], g_ref[...]).astype(o_ref.dtype)


def _rmsnorm(x, g3, l, out_dtype, row0=0, rows=None):
    d = x.shape[1]
    rows = x.shape[0] if rows is None else rows
    tm = _pick(rows, (512, 256, 128, 64, 32, 16, 8))
    assert row0 % tm == 0
    rb0 = row0 // tm
    return _pcall(
        _rmsnorm_kernel, name="rmsnorm", grid=(rows // tm,),
        in_specs=[pl.BlockSpec((tm, d), lambda i: (rb0 + i, 0)), _vspec(l, d)],
        args=(x, g3),
        out_specs=[pl.BlockSpec((tm, d), lambda i: (i, 0))],
        out_shape=[jax.ShapeDtypeStruct((rows, d), out_dtype)],
        sem=("arbitrary",))[0]


def _cast_w(w_ref, wb_ref):
    k = w_ref.shape[0]
    ck = _pick(k, (256, 128, 64, 32, 16))

    def body(c, carry):
        r = pl.multiple_of(c * ck, ck)
        wb_ref[pl.ds(r, ck), :] = w_ref[pl.ds(r, ck), :].astype(BF16)
        return carry

    lax.fori_loop(0, k // ck, body, 0)


def _mm_plain_kernel(x_ref, w_ref, o_ref, wb_ref):
    @pl.when(pl.program_id(1) == 0)
    def _():
        _cast_w(w_ref, wb_ref)

    o_ref[...] = jnp.dot(x_ref[...], wb_ref[...], preferred_element_type=F32).astype(o_ref.dtype)


def _mm_plain(x, w, pre, tm, tn):
    m, k = x.shape
    n = w.shape[-1]
    return _pcall(
        _mm_plain_kernel, name="mm_plain", grid=(n // tn, m // tm),
        in_specs=[pl.BlockSpec((tm, k), lambda j, i: (i, 0)), _wspec(pre, k, tn)],
        args=(x, w),
        out_specs=[pl.BlockSpec((tm, tn), lambda j, i: (i, j))],
        out_shape=[jax.ShapeDtypeStruct((m, n), BF16)],
        scratch=[pltpu.VMEM((k, tn), BF16)],
        sem=("arbitrary", "arbitrary"))[0]


def _mm_resid_kernel(x_ref, w_ref, r_ref, o_ref, wb_ref):
    @pl.when(pl.program_id(1) == 0)
    def _():
        _cast_w(w_ref, wb_ref)

    o_ref[...] = r_ref[...] + jnp.dot(x_ref[...], wb_ref[...], preferred_element_type=F32)


def _mm_resid(x, w, pre, r, tm, tn):
    m, k = x.shape
    n = w.shape[-1]
    return _pcall(
        _mm_resid_kernel, name="mm_resid", grid=(n // tn, m // tm),
        in_specs=[pl.BlockSpec((tm, k), lambda j, i: (i, 0)), _wspec(pre, k, tn),
                  pl.BlockSpec((tm, tn), lambda j, i: (i, j))],
        args=(x, w, r),
        out_specs=[pl.BlockSpec((tm, tn), lambda j, i: (i, j))],
        out_shape=[jax.ShapeDtypeStruct((m, n), F32)],
        scratch=[pltpu.VMEM((k, tn), BF16)],
        sem=("arbitrary", "arbitrary"))[0]


def _mm_swiglu_kernel(x_ref, wg_ref, wu_ref, o_ref, wgb_ref, wub_ref):
    @pl.when(pl.program_id(1) == 0)
    def _():
        _cast_w(wg_ref, wgb_ref)
        _cast_w(wu_ref, wub_ref)

    x = x_ref[...]
    g = jnp.dot(x, wgb_ref[...], preferred_element_type=F32)
    u = jnp.dot(x, wub_ref[...], preferred_element_type=F32)
    o_ref[...] = (jax.nn.silu(g) * u).astype(o_ref.dtype)


def _mm_swiglu(x, wg, wu, pre, tm, tn):
    m, k = x.shape
    n = wg.shape[-1]
    return _pcall(
        _mm_swiglu_kernel, name="mm_swiglu", grid=(n // tn, m // tm),
        in_specs=[pl.BlockSpec((tm, k), lambda j, i: (i, 0)), _wspec(pre, k, tn), _wspec(pre, k, tn)],
        args=(x, wg, wu),
        out_specs=[pl.BlockSpec((tm, tn), lambda j, i: (i, j))],
        out_shape=[jax.ShapeDtypeStruct((m, n), BF16)],
        scratch=[pltpu.VMEM((k, tn), BF16), pltpu.VMEM((k, tn), BF16)],
        sem=("arbitrary", "arbitrary"))[0]


def _mm_merge_kernel(a_ref, b_ref, wa_ref, wb_ref, ga_ref, gb_ref, o_ref, wab_ref, wbb_ref):
    @pl.when(pl.program_id(1) == 0)
    def _():
        _cast_w(wa_ref, wab_ref)
        _cast_w(wb_ref, wbb_ref)

    ya = jnp.dot(a_ref[...], wab_ref[...], preferred_element_type=F32)
    yb = jnp.dot(b_ref[...], wbb_ref[...], preferred_element_type=F32)
    ga = ga_ref[...].astype(F32)
    gb = gb_ref[...].astype(F32)
    o_ref[...] = (jax.nn.sigmoid(ga) * ya + jax.nn.sigmoid(gb) * yb).astype(o_ref.dtype)


def _mm_merge(a, b, wa, wb, pre, proj, ga_col, gb_col, tm, tn):
    m, k = a.shape
    n = wa.shape[-1]
    ga_blk, gb_blk = ga_col // tn, gb_col // tn
    return _pcall(
        _mm_merge_kernel, name="mm_merge", grid=(n // tn, m // tm),
        in_specs=[pl.BlockSpec((tm, k), lambda j, i: (i, 0)),
                  pl.BlockSpec((tm, k), lambda j, i: (i, 0)),
                  _wspec(pre, k, tn), _wspec(pre, k, tn),
                  pl.BlockSpec((tm, tn), lambda j, i: (i, ga_blk + j)),
                  pl.BlockSpec((tm, tn), lambda j, i: (i, gb_blk + j))],
        args=(a, b, wa, wb, proj, proj),
        out_specs=[pl.BlockSpec((tm, tn), lambda j, i: (i, j))],
        out_shape=[jax.ShapeDtypeStruct((m, n), BF16)],
        scratch=[pltpu.VMEM((k, tn), BF16), pltpu.VMEM((k, tn), BF16)],
        sem=("arbitrary", "arbitrary"))[0]


def _rope_kernel(inv_ref, cos_ref, sin_ref, *, pos0):
    t, half = cos_ref.shape
    pos = lax.broadcasted_iota(jnp.int32, (t, half), 0).astype(F32) + F32(pos0)
    ang = pos * inv_ref[...]
    cos_ref[...] = jnp.cos(ang)
    sin_ref[...] = jnp.sin(ang)


def _rope_tables(t, half, pos0):
    inv = ROPE_BASE ** (-jnp.arange(half, dtype=F32) / half)
    return pl.pallas_call(
        functools.partial(_rope_kernel, pos0=pos0),
        out_shape=(jax.ShapeDtypeStruct((t, half), F32), jax.ShapeDtypeStruct((t, half), F32)),
        name="rope_tables",
    )(inv.reshape(1, half))


def _decay_tables(c, h, dk, dv):
    log_g = jnp.log1p(-jnp.exp2(-5.0 - jnp.arange(h, dtype=F32)))
    idx = jnp.arange(c)
    rel = idx[:, None] - idx[None, :]
    dmask = jnp.where(rel[None] >= 0,
                      jnp.exp(jnp.maximum(rel, 0)[None].astype(F32) * log_g[:, None, None]), 0.0)
    xi = jnp.exp((idx + 1).astype(F32)[None, :] * log_g[:, None])
    zeta = jnp.exp((c - 1 - idx).astype(F32)[None, :] * log_g[:, None])
    g_c = jnp.exp(c * log_g)
    xi_t = jnp.broadcast_to(xi[:, :, None], (h, c, dv))
    zeta_t = jnp.broadcast_to(zeta[:, :, None], (h, c, dk))
    return dmask, xi_t, zeta_t, g_c


def _ret_head(q, k, v, g, cos, sin, dm, xi, zt, gn, s, gc, dk):
    half = dk // 2
    q1, q2 = q[:, :half], q[:, half:]
    k1, k2 = k[:, :half], k[:, half:]
    qr = jnp.concatenate([q1 * cos - q2 * sin, q1 * sin + q2 * cos], axis=-1)
    kr = jnp.concatenate([k1 * cos - k2 * sin, k1 * sin + k2 * cos], axis=-1) * (dk ** -0.5)
    qb = qr.astype(BF16)
    kb = kr.astype(BF16)
    scores = lax.dot_general(qb, kb, (((1,), (1,)), ((), ())), preferred_element_type=F32)
    intra = jnp.dot((scores * dm).astype(BF16), v, preferred_element_type=F32)
    cross = jnp.dot(qb, s.astype(BF16), preferred_element_type=F32) * xi
    kz = (kr * zt).astype(BF16)
    upd = lax.dot_general(kz, v, (((0,), (0,)), ((), ())), preferred_element_type=F32)
    s_new = gc * s + upd
    o = intra + cross
    mu = jnp.mean(o, axis=-1, keepdims=True)
    d = o - mu
    var = jnp.mean(d * d, axis=-1, keepdims=True)
    y = d * lax.rsqrt(var + EPS) * gn
    return (jax.nn.silu(g) * y).astype(BF16), s_new


def _ret_prompt_kernel(gc_ref, q_ref, k_ref, v_ref, g_ref, cos_ref, sin_ref, dm_ref, xi_ref, zt_ref,
                       gn_ref, o_ref, s_ref, *, nh, dk, dv):
    @pl.when(pl.program_id(1) == 0)
    def _():
        s_ref[...] = jnp.zeros_like(s_ref)

    cos = cos_ref[...]
    sin = sin_ref[...]
    for h in range(nh):
        out, s_new = _ret_head(
            q_ref[:, h * dk:(h + 1) * dk].astype(F32), k_ref[:, h * dk:(h + 1) * dk].astype(F32),
            v_ref[:, h * dv:(h + 1) * dv], g_ref[:, h * dv:(h + 1) * dv].astype(F32),
            cos, sin, dm_ref[h], xi_ref[h], zt_ref[h], gn_ref[:, h * dv:(h + 1) * dv],
            s_ref[h], gc_ref[h], dk)
        o_ref[:, h * dv:(h + 1) * dv] = out
        s_ref[h] = s_new


def _ret_prompt(proj, gn3, l, depth, b, t, nh, dk, dv, s_prev):
    c = RET_CHUNK if t % RET_CHUNK == 0 else t
    nc = t // c
    w = nh * dk
    m = proj.shape[0]
    cos, sin = _rope_tables(t, dk // 2, 0)
    dmask, xi_t, zeta_t, g_c = _decay_tables(c, nh, dk, dv)
    row = lambda bi, ci: bi * nc + ci
    full3 = lambda bi, ci: (0, 0, 0)
    return _pcall(
        functools.partial(_ret_prompt_kernel, nh=nh, dk=dk, dv=dv), name="ret_prompt", grid=(b, nc),
        in_specs=[pl.BlockSpec(memory_space=pltpu.SMEM),
                  pl.BlockSpec((c, w), lambda bi, ci: (row(bi, ci), 0)),
                  pl.BlockSpec((c, w), lambda bi, ci: (row(bi, ci), 1)),
                  pl.BlockSpec((c, w), lambda bi, ci: (row(bi, ci), 2)),
                  pl.BlockSpec((c, w), lambda bi, ci: (row(bi, ci), 3)),
                  pl.BlockSpec((c, dk // 2), lambda bi, ci: (ci, 0)),
                  pl.BlockSpec((c, dk // 2), lambda bi, ci: (ci, 0)),
                  pl.BlockSpec((nh, c, c), full3),
                  pl.BlockSpec((nh, c, dv), full3),
                  pl.BlockSpec((nh, c, dk), full3),
                  _vspec(l, nh * dv)],
        args=(g_c, proj, proj, proj, proj, cos, sin, dmask, xi_t, zeta_t, gn3),
        out_specs=[pl.BlockSpec((c, nh * dv), lambda bi, ci: (row(bi, ci), 0)),
                   pl.BlockSpec((None, None, nh, dk, dv), lambda bi, ci: (l, bi, 0, 0, 0))],
        out_shape=[jax.ShapeDtypeStruct((m, nh * dv), BF16),
                   jax.ShapeDtypeStruct((depth, b, nh, dk, dv), F32)],
        prev=(None, s_prev),
        sem=("arbitrary", "arbitrary"))


def _ret_sample_kernel(gc_ref, q_ref, k_ref, v_ref, g_ref, cos_ref, sin_ref, dm_ref, xi_ref, zt_ref,
                       gn_ref, s0_ref, o_ref, s_ref, *, nh, dk, dv, bb, ts):
    cos = cos_ref[...]
    sin = sin_ref[...]
    q = q_ref[...].astype(F32)
    k = k_ref[...].astype(F32)
    v = v_ref[...].astype(F32)
    g = g_ref[...].astype(F32)
    seqs = []
    for i in range(bb):
        r0, r1 = i * ts, (i + 1) * ts
        outs = []
        for h in range(nh):
            out, s_new = _ret_head(
                q[r0:r1, h * dk:(h + 1) * dk], k[r0:r1, h * dk:(h + 1) * dk],
                v[r0:r1, h * dv:(h + 1) * dv].astype(BF16), g[r0:r1, h * dv:(h + 1) * dv],
                cos, sin, dm_ref[h], xi_ref[h], zt_ref[h], gn_ref[:, h * dv:(h + 1) * dv],
                s0_ref[i, h], gc_ref[h], dk)
            s_ref[i, h] = s_new
            outs.append(out.astype(F32))
        seqs.append(jnp.concatenate(outs, axis=-1))
    o_ref[...] = jnp.concatenate(seqs, axis=0).astype(o_ref.dtype)


def _ret_sample(proj, gn3, s0_all, l, row0, bs, ts, nh, dk, dv, ya_prev, s_prev):
    bb = 2 if bs % 2 == 0 else 1
    rows = bb * ts
    w = nh * dk
    depth = s0_all.shape[0]
    cos, sin = _rope_tables(ts, dk // 2, PAST_LEN)
    dmask, xi_t, zeta_t, g_c = _decay_tables(ts, nh, dk, dv)
    assert row0 % rows == 0
    rb0 = row0 // rows
    full3 = lambda i: (0, 0, 0)
    return _pcall(
        functools.partial(_ret_sample_kernel, nh=nh, dk=dk, dv=dv, bb=bb, ts=ts), name="ret_sample",
        grid=(bs // bb,),
        in_specs=[pl.BlockSpec(memory_space=pltpu.SMEM),
                  pl.BlockSpec((rows, w), lambda i: (rb0 + i, 0)),
                  pl.BlockSpec((rows, w), lambda i: (rb0 + i, 1)),
                  pl.BlockSpec((rows, w), lambda i: (rb0 + i, 2)),
                  pl.BlockSpec((rows, w), lambda i: (rb0 + i, 3)),
                  pl.BlockSpec((ts, dk // 2), lambda i: (0, 0)),
                  pl.BlockSpec((ts, dk // 2), lambda i: (0, 0)),
                  pl.BlockSpec((nh, ts, ts), full3),
                  pl.BlockSpec((nh, ts, dv), full3),
                  pl.BlockSpec((nh, ts, dk), full3),
                  _vspec(l, nh * dv),
                  pl.BlockSpec((None, bb, nh, dk, dv), lambda i: (l, i, 0, 0, 0))],
        args=(g_c, proj, proj, proj, proj, cos, sin, dmask, xi_t, zeta_t, gn3, s0_all),
        out_specs=[pl.BlockSpec((rows, nh * dv), lambda i: (rb0 + i, 0)),
                   pl.BlockSpec((None, bb, nh, dk, dv), lambda i: (l, i, 0, 0, 0))],
        out_shape=[jax.ShapeDtypeStruct(ya_prev.shape, BF16),
                   jax.ShapeDtypeStruct((depth, bs, nh, dk, dv), F32)],
        prev=(ya_prev, s_prev),
        sem=("arbitrary",))


def _lru_gates(uc, wr_ref, br_ref, wi_ref, bi_ref, lam_ref):
    nb, lb, _ = wr_ref.shape
    ucb = uc.astype(BF16)
    rl, il = [], []
    for n in range(nb):
        xb = ucb[:, n * lb:(n + 1) * lb]
        rl.append(jnp.dot(xb, wr_ref[n].astype(BF16), preferred_element_type=F32))
        il.append(jnp.dot(xb, wi_ref[n].astype(BF16), preferred_element_type=F32))
    r = jax.nn.sigmoid(jnp.concatenate(rl, axis=-1) + br_ref[...])
    i = jax.nn.sigmoid(jnp.concatenate(il, axis=-1) + bi_ref[...])
    log_a = -LRU_C * r * jax.nn.softplus(-lam_ref[...])
    a = jnp.exp(log_a)
    bx = jnp.sqrt(-jnp.tanh(log_a) * (a * a + 1.0)) * (i * uc)
    return a, bx


def _scan8(a, bx):
    rows = lax.broadcasted_iota(jnp.int32, a.shape, 0) & (SUBLANES - 1)
    d = 1
    while d < SUBLANES:
        a_sh = pltpu.roll(a, d, axis=0)
        b_sh = pltpu.roll(bx, d, axis=0)
        m = rows >= d
        bx = jnp.where(m, a * b_sh + bx, bx)
        a = jnp.where(m, a * a_sh, a)
        d *= 2
    return a, bx


def _lru_prompt_kernel(ux_ref, uy_ref, cw_ref, cb_ref, wr_ref, br_ref, wi_ref, bi_ref, lam_ref,
                       o_ref, hl_ref, ul_ref, hc_ref):
    r, cw = ux_ref.shape
    ncw = cw_ref.shape[0]

    @pl.when(pl.program_id(2) == 0)
    def _():
        hc_ref[...] = jnp.zeros_like(hc_ref)
        ul_ref[...] = jnp.zeros_like(ul_ref)

    u = ux_ref[...].astype(F32)
    ext = jnp.concatenate([ul_ref[...], u], axis=0)
    ul_ref[...] = u[r - SUBLANES:, :]
    uc = cb_ref[...]
    for j in range(ncw):
        s = ncw - 1 - j
        term = u if s == 0 else pltpu.roll(ext, s, axis=0)[SUBLANES:, :]
        uc = uc + term * cw_ref[j:j + 1, :]
    a, bx = _lru_gates(uc, wr_ref, br_ref, wi_ref, bi_ref, lam_ref)
    a, bx = _scan8(a, bx)
    carry = hc_ref[...]
    hs = []
    for blk in range(r // SUBLANES):
        lo = blk * SUBLANES
        hb = a[lo:lo + SUBLANES, :] * carry + bx[lo:lo + SUBLANES, :]
        hs.append(hb)
        carry = jnp.broadcast_to(hb[SUBLANES - 1:SUBLANES, :], (SUBLANES, cw))
    hc_ref[...] = carry
    hl_ref[...] = hs[-1]
    h = jnp.concatenate(hs, axis=0)
    o_ref[...] = (h * jax.nn.gelu(uy_ref[...].astype(F32))).astype(o_ref.dtype)


def _lru_sample_kernel(ux_ref, uy_ref, cw_ref, cb_ref, wr_ref, br_ref, wi_ref, bi_ref, lam_ref,
                       h0_ref, buf_ref, o_ref, hl_ref, us_ref):
    r, cw = ux_ref.shape
    ncw = cw_ref.shape[0]
    rows = lax.broadcasted_iota(jnp.int32, (r, cw), 0) & (SUBLANES - 1)
    u = ux_ref[...].astype(F32)
    us_ref[...] = u
    buf = buf_ref[...]
    uc = cb_ref[...]
    for j in range(ncw):
        s = ncw - 1 - j
        if s == 0:
            term = u
        else:
            term = jnp.where(rows >= s, pltpu.roll(u, s, axis=0),
                             pltpu.roll(buf, r - SUBLANES + s, axis=0))
        uc = uc + term * cw_ref[j:j + 1, :]
    a, bx = _lru_gates(uc, wr_ref, br_ref, wi_ref, bi_ref, lam_ref)
    a, bx = _scan8(a, bx)
    h = a * h0_ref[...] + bx
    hl_ref[...] = h
    o_ref[...] = (h * jax.nn.gelu(uy_ref[...].astype(F32))).astype(o_ref.dtype)


def _lru_specs(l, r, cw, ncw, lb, ux_blk, uy_blk, row_of, cb_of):
    nbc = cw // lb
    vec = pl.BlockSpec((None, 1, cw), lambda *g: (l, 0, cb_of(*g)))
    gate = pl.BlockSpec((None, nbc, lb, lb), lambda *g: (l, cb_of(*g), 0, 0))
    return [pl.BlockSpec((r, cw), lambda *g: (row_of(*g), ux_blk + cb_of(*g))),
            pl.BlockSpec((r, cw), lambda *g: (row_of(*g), uy_blk + cb_of(*g))),
            pl.BlockSpec((None, ncw, cw), lambda *g: (l, 0, cb_of(*g))),
            vec, gate, vec, gate, vec, vec]


def _lru_prompt(proj, ux_col, l, lru_w, b, t):
    conv_w, conv_b3, w_r, b_r3, w_i, b_i3, lam3 = lru_w
    width = lam3.shape[-1]
    lb = w_r.shape[-1]
    ncw = conv_w.shape[1]
    m = proj.shape[0]
    cw = _pick(width, (512, 256, 128))
    r = _pick(t, (256, 128, 64, 32, 16, 8))
    nt = t // r
    specs = _lru_specs(l, r, cw, ncw, lb, ux_col // cw, (ux_col + width) // cw,
                       lambda bi, cb, ti: bi * nt + ti, lambda bi, cb, ti: cb)
    last8 = pl.BlockSpec((None, SUBLANES, cw), lambda bi, cb, ti: (bi, 0, cb))
    return _pcall(
        _lru_prompt_kernel, name="lru_prompt", grid=(b, width // cw, nt),
        in_specs=specs,
        args=(proj, proj, conv_w, conv_b3, w_r, b_r3, w_i, b_i3, lam3),
        out_specs=[pl.BlockSpec((r, cw), lambda bi, cb, ti: (bi * nt + ti, cb)), last8, last8],
        out_shape=[jax.ShapeDtypeStruct((m, width), BF16),
                   jax.ShapeDtypeStruct((b, SUBLANES, width), F32),
                   jax.ShapeDtypeStruct((b, SUBLANES, width), F32)],
        scratch=[pltpu.VMEM((SUBLANES, cw), F32)],
        sem=("arbitrary", "arbitrary", "arbitrary"))


def _lru_sample(proj, ux_col, l, lru_w, row0, h0_rep, buf_fr, yb_prev):
    conv_w, conv_b3, w_r, b_r3, w_i, b_i3, lam3 = lru_w
    width = lam3.shape[-1]
    lb = w_r.shape[-1]
    ncw = conv_w.shape[1]
    ms = h0_rep.shape[1]
    cw = _pick(width, (512, 256, 128))
    r = _pick(ms, (256, 128, 64, 32, 16, 8))
    assert row0 % r == 0
    rb0 = row0 // r
    specs = _lru_specs(l, r, cw, ncw, lb, ux_col // cw, (ux_col + width) // cw,
                       lambda ri, cb: rb0 + ri, lambda ri, cb: cb)
    st = pl.BlockSpec((None, r, cw), lambda ri, cb: (l, ri, cb))
    f32rows = pl.BlockSpec((r, cw), lambda ri, cb: (ri, cb))
    return _pcall(
        _lru_sample_kernel, name="lru_sample", grid=(ms // r, width // cw),
        in_specs=specs + [st, st],
        args=(proj, proj, conv_w, conv_b3, w_r, b_r3, w_i, b_i3, lam3, h0_rep, buf_fr),
        out_specs=[pl.BlockSpec((r, cw), lambda ri, cb: (rb0 + ri, cb)), f32rows, f32rows],
        out_shape=[jax.ShapeDtypeStruct(yb_prev.shape, BF16),
                   jax.ShapeDtypeStruct((ms, width), F32), jax.ShapeDtypeStruct((ms, width), F32)],
        prev=(yb_prev,),
        sem=("arbitrary", "arbitrary"))


def _router_kernel(x_ref, g_ref, wr_ref, o_ref, *, ne):
    h = _rms(x_ref[...], g_ref[...]).astype(BF16)
    logits = jnp.dot(h, wr_ref[...].astype(BF16), preferred_element_type=F32)
    lane = lax.broadcasted_iota(jnp.int32, logits.shape, 1).astype(F32)
    neg = F32(-jnp.inf)
    big = F32(LANES)
    l1 = jnp.where(lane < ne, logits, neg)
    m1 = jnp.max(l1, axis=-1, keepdims=True)
    i1 = jnp.min(jnp.where(l1 == m1, lane, big), axis=-1, keepdims=True)
    l2 = jnp.where(lane == i1, neg, l1)
    m2 = jnp.max(l2, axis=-1, keepdims=True)
    i2 = jnp.min(jnp.where(l2 == m2, lane, big), axis=-1, keepdims=True)
    e2 = jnp.exp(m2 - m1)
    den = 1.0 + e2
    o_ref[...] = (jnp.where(lane == 0.0, i1, 0.0) + jnp.where(lane == 1.0, i2, 0.0)
                  + jnp.where(lane == 2.0, 1.0 / den, 0.0) + jnp.where(lane == 3.0, e2 / den, 0.0))


def _router(x, g3, l, wr):
    m, d = x.shape
    ne = wr.shape[-1]
    assert TOP_K == 2 and ne <= LANES
    wr_pad = jnp.pad(wr, ((0, 0), (0, LANES - ne)))
    tm = _pick(m, (512, 256, 128, 64, 32, 16, 8))
    return _pcall(
        functools.partial(_router_kernel, ne=ne), name="router", grid=(m // tm,),
        in_specs=[pl.BlockSpec((tm, d), lambda i: (i, 0)), _vspec(l, d),
                  pl.BlockSpec((d, LANES), lambda i: (0, 0))],
        args=(x, g3, wr_pad),
        out_specs=[pl.BlockSpec((tm, LANES), lambda i: (i, 0))],
        out_shape=[jax.ShapeDtypeStruct((m, LANES), F32)],
        sem=("arbitrary",))[0]


def _dispatch_tables(rout, ne, tg):
    m = rout.shape[0]
    na = TOP_K * m
    eid = rout[:, :TOP_K].astype(jnp.int32).T.reshape(na)
    onehot = (eid[:, None] == jnp.arange(ne, dtype=jnp.int32)[None, :]).astype(jnp.int32)
    cnt = jnp.sum(onehot, axis=0)
    rank = jnp.sum((jnp.cumsum(onehot, axis=0) - onehot) * onehot, axis=1)
    padded = ((cnt + tg - 1) // tg) * tg
    gend = jnp.cumsum(padded)
    gstart = gend - padded
    pos = gstart[eid] + rank
    p_rows = ((na + tg - 1) // tg + ne) * tg
    n_tiles = p_rows // tg
    tok = jnp.zeros((p_rows,), jnp.int32).at[pos].set(jnp.arange(na, dtype=jnp.int32) % m)
    n_used = (gend[-1] // tg).astype(jnp.int32)
    tstart = jnp.arange(n_tiles, dtype=jnp.int32) * tg
    tstart = jnp.minimum(tstart, (n_used - 1) * tg)
    te = jnp.minimum(jnp.searchsorted(gend, tstart, side="right"), ne - 1).astype(jnp.int32)
    return pos, tok, te, n_used.reshape(1), p_rows


def _moe_gather_kernel(tok_ref, x_hbm, g_ref, o_ref, buf, sem):
    rows = buf.shape[0]

    def row_copy(r, t):
        return pltpu.make_async_copy(x_hbm.at[pl.ds(t, 1), :], buf.at[pl.ds(r, 1), :], sem)

    def issue(r, c):
        row_copy(r, tok_ref[0, r]).start()
        return c

    def wait(r, c):
        row_copy(r, 0).wait()
        return c

    lax.fori_loop(0, rows, issue, 0, unroll=8)
    lax.fori_loop(0, rows, wait, 0, unroll=8)
    o_ref[...] = _rms(buf[...], g_ref[...]).astype(o_ref.dtype)


def _moe_gather(x, g3, l, tok, p_rows):
    m, d = x.shape
    tr = _pick(p_rows, (GATHER_ROWS, 128, 64, 32, 16))
    nt = p_rows // tr
    return _pcall(
        _moe_gather_kernel, name="moe_gather", grid=(nt,),
        in_specs=[pl.BlockSpec((None, 1, tr), lambda i: (i, 0, 0), memory_space=pltpu.SMEM),
                  pl.BlockSpec(memory_space=pl.ANY), _vspec(l, d)],
        args=(tok.reshape(nt, 1, tr), x, g3),
        out_specs=[pl.BlockSpec((tr, d), lambda i: (i, 0))],
        out_shape=[jax.ShapeDtypeStruct((p_rows, d), BF16)],
        scratch=[pltpu.VMEM((tr, d), F32), pltpu.SemaphoreType.DMA(())],
        sem=("arbitrary",))[0]


def _tile_changed(te_ref):
    ti = pl.program_id(1)
    return jnp.logical_or(ti == 0, te_ref[ti] != te_ref[jnp.maximum(ti - 1, 0)])


def _gmm_swiglu_kernel(te_ref, nu_ref, x_ref, wg_ref, wu_ref, o_ref, wgb_ref, wub_ref):
    @pl.when(_tile_changed(te_ref))
    def _():
        _cast_w(wg_ref, wgb_ref)
        _cast_w(wu_ref, wub_ref)

    used = pl.program_id(1) < nu_ref[0]

    @pl.when(used)
    def _():
        x = x_ref[...]
        g = jnp.dot(x, wgb_ref[...], preferred_element_type=F32)
        u = jnp.dot(x, wub_ref[...], preferred_element_type=F32)
        o_ref[...] = (jax.nn.silu(g) * u).astype(o_ref.dtype)

    @pl.when(jnp.logical_not(used))
    def _():
        o_ref[...] = jnp.zeros_like(o_ref)


def _gmm_down_kernel(te_ref, nu_ref, x_ref, w_ref, o_ref, wb_ref):
    @pl.when(_tile_changed(te_ref))
    def _():
        _cast_w(w_ref, wb_ref)

    used = pl.program_id(1) < nu_ref[0]

    @pl.when(used)
    def _():
        o_ref[...] = jnp.dot(x_ref[...], wb_ref[...], preferred_element_type=F32)

    @pl.when(jnp.logical_not(used))
    def _():
        o_ref[...] = jnp.zeros_like(o_ref)


def _gmm_specs(jm, tg, k, tn):
    xrow = lambda j, ti, te, nu: (jnp.minimum(ti, nu[0] - 1), 0)
    wsp = pl.BlockSpec((None, None, k, tn), lambda j, ti, te, nu: (jm, te[ti], 0, j))
    return pl.BlockSpec((tg, k), xrow), wsp, pl.BlockSpec((tg, tn), lambda j, ti, te, nu: (ti, j))


def _gmm_swiglu(xg, wg, wu, jm, te, n_used, tg, tn):
    p, k = xg.shape
    n = wg.shape[-1]
    xs, ws, os_ = _gmm_specs(jm, tg, k, tn)
    return _pcall(
        _gmm_swiglu_kernel, name="gmm_swiglu", grid=(n // tn, p // tg), num_scalar_prefetch=2,
        in_specs=[None, None, xs, ws, ws], args=(te, n_used, xg, wg, wu),
        out_specs=[os_], out_shape=[jax.ShapeDtypeStruct((p, n), BF16)],
        scratch=[pltpu.VMEM((k, tn), BF16), pltpu.VMEM((k, tn), BF16)],
        sem=("arbitrary", "arbitrary"))[0]


def _gmm_down(ug, wd, jm, te, n_used, tg, tn):
    p, k = ug.shape
    n = wd.shape[-1]
    xs, ws, os_ = _gmm_specs(jm, tg, k, tn)
    return _pcall(
        _gmm_down_kernel, name="gmm_down", grid=(n // tn, p // tg), num_scalar_prefetch=2,
        in_specs=[None, None, xs, ws], args=(te, n_used, ug, wd),
        out_specs=[os_], out_shape=[jax.ShapeDtypeStruct((p, n), F32)],
        scratch=[pltpu.VMEM((k, tn), BF16)],
        sem=("arbitrary", "arbitrary"))[0]


def _moe_combine_kernel(pos_ref, x_ref, rw_ref, yg_hbm, o_ref, buf, sem):
    tc = x_ref.shape[0]

    def row_copy(k, r, p):
        return pltpu.make_async_copy(yg_hbm.at[pl.ds(p, 1), :], buf.at[k, pl.ds(r, 1), :], sem)

    for k in range(TOP_K):
        def issue(r, c, k=k):
            row_copy(k, r, pos_ref[0, k * tc + r]).start()
            return c

        lax.fori_loop(0, tc, issue, 0, unroll=8)
    for k in range(TOP_K):
        def wait(r, c, k=k):
            row_copy(k, r, 0).wait()
            return c

        lax.fori_loop(0, tc, wait, 0, unroll=8)
    rw = rw_ref[...]
    y = rw[:, TOP_K:TOP_K + 1] * buf[0]
    for k in range(1, TOP_K):
        y = y + rw[:, TOP_K + k:TOP_K + k + 1] * buf[k]
    o_ref[...] = x_ref[...] + y


def _moe_combine(x, rout, pos, yg):
    m, d = x.shape
    tc = _pick(m, (COMBINE_ROWS, 64, 32, 16, 8))
    nt = m // tc
    pos_t = pos.reshape(TOP_K, nt, tc).transpose(1, 0, 2).reshape(nt, 1, TOP_K * tc)
    return _pcall(
        _moe_combine_kernel, name="moe_combine", grid=(nt,),
        in_specs=[pl.BlockSpec((None, 1, TOP_K * tc), lambda i: (i, 0, 0), memory_space=pltpu.SMEM),
                  pl.BlockSpec((tc, d), lambda i: (i, 0)),
                  pl.BlockSpec((tc, LANES), lambda i: (i, 0)),
                  pl.BlockSpec(memory_space=pl.ANY)],
        args=(pos_t, x, rout, yg),
        out_specs=[pl.BlockSpec((tc, d), lambda i: (i, 0))],
        out_shape=[jax.ShapeDtypeStruct((m, d), F32)],
        scratch=[pltpu.VMEM((TOP_K, tc, d), F32), pltpu.SemaphoreType.DMA(())],
        sem=("arbitrary",))[0]


def kernel(x_prompt, x_sample, state_ret, state_lru, state_conv, norm_mix, w_in, ret_gn, w_ret_o, conv_w, conv_b, w_rgate, b_rgate, w_igate, b_igate, lru_lambda, w_lru_o, w_out, norm_ffn, ffn_w_gate, ffn_w_up, ffn_w_down, moe_router, moe_w_gate, moe_w_up, moe_w_down, norm_final):
    bp, tp, d = x_prompt.shape
    bs, ts, _ = x_sample.shape
    depth, _, nh, dk, dv = state_ret.shape
    width = state_lru.shape[-1]
    ncw = conv_w.shape[1]
    ne = moe_router.shape[-1]
    assert dk == dv and nh * dk == width == d and ts == SUBLANES and ncw - 1 <= SUBLANES
    mp, ms = bp * tp, bs * ts
    m = mp + ms
    ux_col = 2 * nh * dk + 2 * nh * dv
    ga_col = ux_col + 2 * width

    tm = _pick(m, (1024, 512, 256, 128, 64, 32, 16, 8))
    tm_half = _pick(m, (512, 256, 128, 64, 32, 16, 8))
    tn_of = lambda n: _pick(n, (1024, 512, 256, 128))

    vec3 = lambda a: a.reshape(a.shape[0], 1, a.shape[-1])
    norm_mix3, norm_ffn3, ret_gn3 = vec3(norm_mix), vec3(norm_ffn), vec3(ret_gn)
    norm_final3 = norm_final.reshape(1, 1, d)
    lru_w = (conv_w, vec3(conv_b), w_rgate, vec3(b_rgate), w_igate, vec3(b_igate), vec3(lru_lambda))
    h0_rep = jnp.repeat(state_lru, ts, axis=1)
    buf_fr = jnp.pad(state_conv, ((0, 0), (0, 0), (ts - (ncw - 1), 0), (0, 0))).reshape(depth, ms, width)

    x = jnp.concatenate([x_prompt.reshape(mp, d), x_sample.reshape(ms, d)], axis=0)
    ret_p = ret_s = None
    lrus_p, lrus_s, convs_p, convs_s = [], [], [], []
    for l in range(depth):
        h = _rmsnorm(x, norm_mix3, l, BF16)
        proj = _mm_plain(h, w_in, (l,), tm, tn_of(w_in.shape[-1]))

        ya, ret_p = _ret_prompt(proj, ret_gn3, l, depth, bp, tp, nh, dk, dv, ret_p)
        ya, ret_s = _ret_sample(proj, ret_gn3, state_ret, l, mp, bs, ts, nh, dk, dv, ya, ret_s)
        yb, hl_p, ul_p = _lru_prompt(proj, ux_col, l, lru_w, bp, tp)
        yb, hl_s, us_s = _lru_sample(proj, ux_col, l, lru_w, mp, h0_rep, buf_fr, yb)
        lrus_p.append(hl_p[:, -1])
        lrus_s.append(hl_s.reshape(bs, ts, width)[:, -1])
        convs_p.append(ul_p[:, SUBLANES - (ncw - 1):])
        convs_s.append(us_s.reshape(bs, ts, width)[:, ts - (ncw - 1):])

        z = _mm_merge(ya, yb, w_ret_o, w_lru_o, (l,), proj, ga_col, ga_col + d, tm, tn_of(d) // 2)
        x = _mm_resid(z, w_out, (l,), x, tm, tn_of(d))
        j = l // 2
        if l % 2 == 0:
            h2 = _rmsnorm(x, norm_ffn3, l, BF16)
            ff = ffn_w_gate.shape[-1]
            u = _mm_swiglu(h2, ffn_w_gate, ffn_w_up, (j,), tm, _pick(ff, (512, 256, 128)))
            x = _mm_resid(u, ffn_w_down, (j,), x, tm_half, _pick(d, (512, 256, 128)))
        else:
            fe = moe_w_gate.shape[-1]
            tg = _pick(TOP_K * m, (MOE_TILE, 256, 128, 64, 32, 16))
            rout = _router(x, norm_ffn3, l, moe_router[j])
            pos, tok, te, n_used, p_rows = _dispatch_tables(rout, ne, tg)
            xg = _moe_gather(x, norm_ffn3, l, tok, p_rows)
            ug = _gmm_swiglu(xg, moe_w_gate, moe_w_up, j, te, n_used, tg, _pick(fe, (256, 128)))
            yg = _gmm_down(ug, moe_w_down, j, te, n_used, tg, _pick(d, (512, 256, 128)))
            x = _moe_combine(x, rout, pos, yg)
    y_p = _rmsnorm(x, norm_final3, 0, F32, 0, mp)
    y_s = _rmsnorm(x, norm_final3, 0, F32, mp, ms)
    return (y_p.reshape(bp, tp, d), y_s.reshape(bs, ts, d),
            ret_p, jnp.stack(lrus_p), jnp.stack(convs_p),
            ret_s, jnp.stack(lrus_s), jnp.stack(convs_s))
```

```python
import functools

import jax
import jax.numpy as jnp
from jax import lax
from jax.experimental import pallas as pl
from jax.experimental.pallas import tpu as pltpu

F32 = jnp.float32
BF16 = jnp.bfloat16

ROPE_BASE = 10000.0
LRU_C = 8.0
EPS = 1e-6
RET_CHUNK = 128
PAST_LEN = 16384
TOP_K = 2
SUBLANES = 8
LANES = 128
VMEM_LIMIT = 56 * 1024 * 1024
MOE_TILE = 1024
GATHER_ROWS = 256
COMBINE_ROWS = 128


def _pick(n, prefs):
    for p in prefs:
        if n % p == 0:
            return p
    return n


def _params(*sem):
    return pltpu.CompilerParams(dimension_semantics=sem, vmem_limit_bytes=VMEM_LIMIT)


def _pcall(body, *, name, grid, in_specs, args, out_specs, out_shape, sem, scratch=(), prev=(),
           num_scalar_prefetch=0):
    n_in = len(args)
    prev = tuple(prev) + (None,) * (len(out_shape) - len(prev))
    extra = [(oi, p) for oi, p in enumerate(prev) if p is not None]
    aliases = {n_in + e: oi for e, (oi, _) in enumerate(extra)}
    if extra:
        inner = body

        def body(*refs):
            return inner(*refs[:n_in], *refs[n_in + len(extra):])

    in_specs = list(in_specs) + [pl.BlockSpec(memory_space=pl.ANY)] * len(extra)
    if num_scalar_prefetch:
        grid_spec = pltpu.PrefetchScalarGridSpec(
            num_scalar_prefetch=num_scalar_prefetch, grid=grid, in_specs=in_specs[num_scalar_prefetch:],
            out_specs=tuple(out_specs), scratch_shapes=list(scratch))
        call = pl.pallas_call(body, grid_spec=grid_spec, out_shape=tuple(out_shape),
                              input_output_aliases=aliases, compiler_params=_params(*sem), name=name)
    else:
        call = pl.pallas_call(body, grid=grid, in_specs=in_specs, out_specs=tuple(out_specs),
                              out_shape=tuple(out_shape), scratch_shapes=list(scratch),
                              input_output_aliases=aliases, compiler_params=_params(*sem), name=name)
    return call(*args, *[p for _, p in extra])


def _wspec(pre, k, tn):
    return pl.BlockSpec((None,) * len(pre) + (k, tn), lambda j, i: pre + (0, j))


def _vspec(l, n):
    return pl.BlockSpec((None, 1, n), lambda *_: (l, 0, 0))


def _rms(x, g):
    return (x * lax.rsqrt(jnp.mean(x * x, axis=-1, keepdims=True) + EPS)) * g


def _rmsnorm_kernel(x_ref, g_ref, o_ref):
    o_ref[...] = _rms(x_ref[...], g_ref[...]).astype(o_ref.dtype)


def _rmsnorm(x, g3, l, out_dtype, row0=0, rows=None):
    d = x.shape[1]
    rows = x.shape[0] if rows is None else rows
    tm = _pick(rows, (512, 256, 128, 64, 32, 16, 8))
    assert row0 % tm == 0
    rb0 = row0 // tm
    return _pcall(
        _rmsnorm_kernel, name="rmsnorm", grid=(rows // tm,),
        in_specs=[pl.BlockSpec((tm, d), lambda i: (rb0 + i, 0)), _vspec(l, d)],
        args=(x, g3),
        out_specs=[pl.BlockSpec((tm, d), lambda i: (i, 0))],
        out_shape=[jax.ShapeDtypeStruct((rows, d), out_dtype)],
        sem=("arbitrary",))[0]


def _cast_w(w_ref, wb_ref):
    k = w_ref.shape[0]
    ck = _pick(k, (256, 128, 64, 32, 16))

    def body(c, carry):
        r = pl.multiple_of(c * ck, ck)
        wb_ref[pl.ds(r, ck), :] = w_ref[pl.ds(r, ck), :].astype(BF16)
        return carry

    lax.fori_loop(0, k // ck, body, 0)


def _mm_inproj_kernel(x_ref, w_ref, cos_ref, sin_ref, o_ref, wb_ref, *, nrot, dk, silu_tiles, gelu_tiles):
    j = pl.program_id(0)

    @pl.when(pl.program_id(1) == 0)
    def _():
        _cast_w(w_ref, wb_ref)

    acc = jnp.dot(x_ref[...], wb_ref[...], preferred_element_type=F32)
    in_tiles = lambda t: jnp.logical_and(j >= t[0], j < t[1])
    is_silu, is_gelu = in_tiles(silu_tiles), in_tiles(gelu_tiles)

    @pl.when(jnp.logical_and(j >= nrot, jnp.logical_not(jnp.logical_or(is_silu, is_gelu))))
    def _():
        o_ref[...] = acc.astype(o_ref.dtype)

    @pl.when(is_silu)
    def _():
        o_ref[...] = jax.nn.silu(acc).astype(o_ref.dtype)

    @pl.when(is_gelu)
    def _():
        o_ref[...] = jax.nn.gelu(acc).astype(o_ref.dtype)

    @pl.when(j < nrot)
    def _():
        scale = jnp.where(j >= nrot // 2, F32(dk ** -0.5), F32(1.0))
        cos = cos_ref[...] * scale
        sin = sin_ref[...] * scale
        half = dk // 2
        for h in range(acc.shape[1] // dk):
            x1 = acc[:, h * dk:h * dk + half]
            x2 = acc[:, h * dk + half:(h + 1) * dk]
            o_ref[:, h * dk:h * dk + half] = (x1 * cos - x2 * sin).astype(o_ref.dtype)
            o_ref[:, h * dk + half:(h + 1) * dk] = (x1 * sin + x2 * cos).astype(o_ref.dtype)


def _mm_inproj(x, w, pre, cos, sin, qk_cols, dk, silu_cols, gelu_cols, tm, tn):
    m, k = x.shape
    n = w.shape[-1]
    half = cos.shape[1]
    assert qk_cols % (2 * tn) == 0 and tn % dk == 0
    assert all(c % tn == 0 for c in silu_cols + gelu_cols)
    tiles = lambda cols: (cols[0] // tn, cols[1] // tn)
    return _pcall(
        functools.partial(_mm_inproj_kernel, nrot=qk_cols // tn, dk=dk, silu_tiles=tiles(silu_cols),
                          gelu_tiles=tiles(gelu_cols)), name="mm_inproj",
        grid=(n // tn, m // tm),
        in_specs=[pl.BlockSpec((tm, k), lambda j, i: (i, 0)), _wspec(pre, k, tn),
                  pl.BlockSpec((tm, half), lambda j, i: (i, 0)),
                  pl.BlockSpec((tm, half), lambda j, i: (i, 0))],
        args=(x, w, cos, sin),
        out_specs=[pl.BlockSpec((tm, tn), lambda j, i: (i, j))],
        out_shape=[jax.ShapeDtypeStruct((m, n), BF16)],
        scratch=[pltpu.VMEM((k, tn), BF16)],
        sem=("arbitrary", "arbitrary"))[0]


def _mm_resid_kernel(x_ref, w_ref, r_ref, o_ref, wb_ref):
    @pl.when(pl.program_id(1) == 0)
    def _():
        _cast_w(w_ref, wb_ref)

    o_ref[...] = r_ref[...] + jnp.dot(x_ref[...], wb_ref[...], preferred_element_type=F32)


def _mm_resid(x, w, pre, r, tm, tn):
    m, k = x.shape
    n = w.shape[-1]
    return _pcall(
        _mm_resid_kernel, name="mm_resid", grid=(n // tn, m // tm),
        in_specs=[pl.BlockSpec((tm, k), lambda j, i: (i, 0)), _wspec(pre, k, tn),
                  pl.BlockSpec((tm, tn), lambda j, i: (i, j))],
        args=(x, w, r),
        out_specs=[pl.BlockSpec((tm, tn), lambda j, i: (i, j))],
        out_shape=[jax.ShapeDtypeStruct((m, n), F32)],
        scratch=[pltpu.VMEM((k, tn), BF16)],
        sem=("arbitrary", "arbitrary"))[0]


def _mm_swiglu_kernel(x_ref, wg_ref, wu_ref, o_ref, wgb_ref, wub_ref):
    @pl.when(pl.program_id(1) == 0)
    def _():
        _cast_w(wg_ref, wgb_ref)
        _cast_w(wu_ref, wub_ref)

    x = x_ref[...]
    g = jnp.dot(x, wgb_ref[...], preferred_element_type=F32)
    u = jnp.dot(x, wub_ref[...], preferred_element_type=F32)
    o_ref[...] = (jax.nn.silu(g) * u).astype(o_ref.dtype)


def _mm_swiglu(x, wg, wu, pre, tm, tn):
    m, k = x.shape
    n = wg.shape[-1]
    return _pcall(
        _mm_swiglu_kernel, name="mm_swiglu", grid=(n // tn, m // tm),
        in_specs=[pl.BlockSpec((tm, k), lambda j, i: (i, 0)), _wspec(pre, k, tn), _wspec(pre, k, tn)],
        args=(x, wg, wu),
        out_specs=[pl.BlockSpec((tm, tn), lambda j, i: (i, j))],
        out_shape=[jax.ShapeDtypeStruct((m, n), BF16)],
        scratch=[pltpu.VMEM((k, tn), BF16), pltpu.VMEM((k, tn), BF16)],
        sem=("arbitrary", "arbitrary"))[0]


def _mm_merge_kernel(a_ref, b_ref, wa_ref, wb_ref, ga_ref, gb_ref, o_ref, wab_ref, wbb_ref):
    @pl.when(pl.program_id(1) == 0)
    def _():
        _cast_w(wa_ref, wab_ref)
        _cast_w(wb_ref, wbb_ref)

    ya = jnp.dot(a_ref[...], wab_ref[...], preferred_element_type=F32)
    yb = jnp.dot(b_ref[...], wbb_ref[...], preferred_element_type=F32)
    ga = ga_ref[...].astype(F32)
    gb = gb_ref[...].astype(F32)
    o_ref[...] = (jax.nn.sigmoid(ga) * ya + jax.nn.sigmoid(gb) * yb).astype(o_ref.dtype)


def _mm_merge(a, b, wa, wb, pre, proj, ga_col, gb_col, tm, tn):
    m, k = a.shape
    n = wa.shape[-1]
    ga_blk, gb_blk = ga_col // tn, gb_col // tn
    return _pcall(
        _mm_merge_kernel, name="mm_merge", grid=(n // tn, m // tm),
        in_specs=[pl.BlockSpec((tm, k), lambda j, i: (i, 0)),
                  pl.BlockSpec((tm, k), lambda j, i: (i, 0)),
                  _wspec(pre, k, tn), _wspec(pre, k, tn),
                  pl.BlockSpec((tm, tn), lambda j, i: (i, ga_blk + j)),
                  pl.BlockSpec((tm, tn), lambda j, i: (i, gb_blk + j))],
        args=(a, b, wa, wb, proj, proj),
        out_specs=[pl.BlockSpec((tm, tn), lambda j, i: (i, j))],
        out_shape=[jax.ShapeDtypeStruct((m, n), BF16)],
        scratch=[pltpu.VMEM((k, tn), BF16), pltpu.VMEM((k, tn), BF16)],
        sem=("arbitrary", "arbitrary"))[0]


def _rope_kernel(inv_ref, cos_ref, sin_ref, *, pos0):
    t, half = cos_ref.shape
    pos = lax.broadcasted_iota(jnp.int32, (t, half), 0).astype(F32) + F32(pos0)
    ang = pos * inv_ref[...]
    cos_ref[...] = jnp.cos(ang)
    sin_ref[...] = jnp.sin(ang)


def _rope_tables(t, half, pos0):
    inv = ROPE_BASE ** (-jnp.arange(half, dtype=F32) / half)
    return pl.pallas_call(
        functools.partial(_rope_kernel, pos0=pos0),
        out_shape=(jax.ShapeDtypeStruct((t, half), F32), jax.ShapeDtypeStruct((t, half), F32)),
        name="rope_tables",
    )(inv.reshape(1, half))


def _decay_tables(c, h, dk, dv):
    log_g = jnp.log1p(-jnp.exp2(-5.0 - jnp.arange(h, dtype=F32)))
    idx = jnp.arange(c)
    rel = idx[:, None] - idx[None, :]
    dmask = jnp.where(rel[None] >= 0,
                      jnp.exp(jnp.maximum(rel, 0)[None].astype(F32) * log_g[:, None, None]), 0.0)
    xi = jnp.exp((idx + 1).astype(F32)[None, :] * log_g[:, None])
    zeta = jnp.exp((c - 1 - idx).astype(F32)[None, :] * log_g[:, None])
    g_c = jnp.exp(c * log_g)
    xi_t = jnp.broadcast_to(xi[:, :, None], (h, c, dv))
    zeta_t = jnp.broadcast_to(zeta[:, :, None], (h, c, dk))
    return dmask, xi_t, zeta_t, g_c


def _ret_head(qb, kb, v, g, dm, xi, zt, gn, s, gc):
    scores = lax.dot_general(qb, kb, (((1,), (1,)), ((), ())), preferred_element_type=F32)
    intra = jnp.dot((scores * dm).astype(BF16), v, preferred_element_type=F32)
    cross = jnp.dot(qb, s.astype(BF16), preferred_element_type=F32) * xi
    kz = (kb.astype(F32) * zt).astype(BF16)
    upd = lax.dot_general(kz, v, (((0,), (0,)), ((), ())), preferred_element_type=F32)
    s_new = gc * s + upd
    o = intra + cross
    mu = jnp.mean(o, axis=-1, keepdims=True)
    d = o - mu
    var = jnp.mean(d * d, axis=-1, keepdims=True)
    y = d * lax.rsqrt(var + EPS) * gn
    return (g * y).astype(BF16), s_new


def _ret_prompt_kernel(gc_ref, q_ref, k_ref, v_ref, g_ref, dm_ref, xi_ref, zt_ref,
                       gn_ref, o_ref, s_ref, *, nh, dk, dv):
    @pl.when(pl.program_id(1) == 0)
    def _():
        s_ref[...] = jnp.zeros_like(s_ref)

    for h in range(nh):
        out, s_new = _ret_head(
            q_ref[:, h * dk:(h + 1) * dk], k_ref[:, h * dk:(h + 1) * dk],
            v_ref[:, h * dv:(h + 1) * dv], g_ref[:, h * dv:(h + 1) * dv].astype(F32),
            dm_ref[h], xi_ref[h], zt_ref[h], gn_ref[:, h * dv:(h + 1) * dv],
            s_ref[h], gc_ref[h])
        o_ref[:, h * dv:(h + 1) * dv] = out
        s_ref[h] = s_new


def _ret_prompt(proj, gn3, l, depth, b, t, nh, dk, dv, s_prev):
    c = RET_CHUNK if t % RET_CHUNK == 0 else t
    nc = t // c
    w = nh * dk
    m = proj.shape[0]
    dmask, xi_t, zeta_t, g_c = _decay_tables(c, nh, dk, dv)
    row = lambda bi, ci: bi * nc + ci
    full3 = lambda bi, ci: (0, 0, 0)
    return _pcall(
        functools.partial(_ret_prompt_kernel, nh=nh, dk=dk, dv=dv), name="ret_prompt", grid=(b, nc),
        in_specs=[pl.BlockSpec(memory_space=pltpu.SMEM),
                  pl.BlockSpec((c, w), lambda bi, ci: (row(bi, ci), 0)),
                  pl.BlockSpec((c, w), lambda bi, ci: (row(bi, ci), 1)),
                  pl.BlockSpec((c, w), lambda bi, ci: (row(bi, ci), 2)),
                  pl.BlockSpec((c, w), lambda bi, ci: (row(bi, ci), 3)),
                  pl.BlockSpec((nh, c, c), full3),
                  pl.BlockSpec((nh, c, dv), full3),
                  pl.BlockSpec((nh, c, dk), full3),
                  _vspec(l, nh * dv)],
        args=(g_c, proj, proj, proj, proj, dmask, xi_t, zeta_t, gn3),
        out_specs=[pl.BlockSpec((c, nh * dv), lambda bi, ci: (row(bi, ci), 0)),
                   pl.BlockSpec((None, None, nh, dk, dv), lambda bi, ci: (l, bi, 0, 0, 0))],
        out_shape=[jax.ShapeDtypeStruct((m, nh * dv), BF16),
                   jax.ShapeDtypeStruct((depth, b, nh, dk, dv), F32)],
        prev=(None, s_prev),
        sem=("arbitrary", "arbitrary"))


def _ret_sample_kernel(gc_ref, q_ref, k_ref, v_ref, g_ref, dm_ref, xi_ref, zt_ref,
                       gn_ref, s0_ref, o_ref, s_ref, *, nh, dk, dv, bb, ts):
    q = q_ref[...].astype(F32)
    k = k_ref[...].astype(F32)
    v = v_ref[...].astype(F32)
    g = g_ref[...].astype(F32)
    seqs = []
    for i in range(bb):
        r0, r1 = i * ts, (i + 1) * ts
        outs = []
        for h in range(nh):
            out, s_new = _ret_head(
                q[r0:r1, h * dk:(h + 1) * dk].astype(BF16), k[r0:r1, h * dk:(h + 1) * dk].astype(BF16),
                v[r0:r1, h * dv:(h + 1) * dv].astype(BF16), g[r0:r1, h * dv:(h + 1) * dv],
                dm_ref[h], xi_ref[h], zt_ref[h], gn_ref[:, h * dv:(h + 1) * dv],
                s0_ref[i, h], gc_ref[h])
            s_ref[i, h] = s_new
            outs.append(out.astype(F32))
        seqs.append(jnp.concatenate(outs, axis=-1))
    o_ref[...] = jnp.concatenate(seqs, axis=0).astype(o_ref.dtype)


def _ret_sample(proj, gn3, s0_all, l, row0, bs, ts, nh, dk, dv, ya_prev, s_prev):
    bb = 2 if bs % 2 == 0 else 1
    rows = bb * ts
    w = nh * dk
    depth = s0_all.shape[0]
    dmask, xi_t, zeta_t, g_c = _decay_tables(ts, nh, dk, dv)
    assert row0 % rows == 0
    rb0 = row0 // rows
    full3 = lambda i: (0, 0, 0)
    return _pcall(
        functools.partial(_ret_sample_kernel, nh=nh, dk=dk, dv=dv, bb=bb, ts=ts), name="ret_sample",
        grid=(bs // bb,),
        in_specs=[pl.BlockSpec(memory_space=pltpu.SMEM),
                  pl.BlockSpec((rows, w), lambda i: (rb0 + i, 0)),
                  pl.BlockSpec((rows, w), lambda i: (rb0 + i, 1)),
                  pl.BlockSpec((rows, w), lambda i: (rb0 + i, 2)),
                  pl.BlockSpec((rows, w), lambda i: (rb0 + i, 3)),
                  pl.BlockSpec((nh, ts, ts), full3),
                  pl.BlockSpec((nh, ts, dv), full3),
                  pl.BlockSpec((nh, ts, dk), full3),
                  _vspec(l, nh * dv),
                  pl.BlockSpec((None, bb, nh, dk, dv), lambda i: (l, i, 0, 0, 0))],
        args=(g_c, proj, proj, proj, proj, dmask, xi_t, zeta_t, gn3, s0_all),
        out_specs=[pl.BlockSpec((rows, nh * dv), lambda i: (rb0 + i, 0)),
                   pl.BlockSpec((None, bb, nh, dk, dv), lambda i: (l, i, 0, 0, 0))],
        out_shape=[jax.ShapeDtypeStruct(ya_prev.shape, BF16),
                   jax.ShapeDtypeStruct((depth, bs, nh, dk, dv), F32)],
        prev=(ya_prev, s_prev),
        sem=("arbitrary",))


def _lru_gates(uc, wr_ref, br_ref, wi_ref, bi_ref, lam_ref):
    nb, lb, _ = wr_ref.shape
    ucb = uc.astype(BF16)
    rl, il = [], []
    for n in range(nb):
        xb = ucb[:, n * lb:(n + 1) * lb]
        rl.append(jnp.dot(xb, wr_ref[n].astype(BF16), preferred_element_type=F32))
        il.append(jnp.dot(xb, wi_ref[n].astype(BF16), preferred_element_type=F32))
    r = jax.nn.sigmoid(jnp.concatenate(rl, axis=-1) + br_ref[...])
    i = jax.nn.sigmoid(jnp.concatenate(il, axis=-1) + bi_ref[...])
    log_a = -LRU_C * r * jax.nn.softplus(-lam_ref[...])
    a = jnp.exp(log_a)
    bx = jnp.sqrt(-jnp.tanh(log_a) * (a * a + 1.0)) * (i * uc)
    return a, bx


def _blocks(x):
    return [x[lo:lo + SUBLANES, :] for lo in range(0, x.shape[0], SUBLANES)]


def _conv_blocks(u_blocks, prev_blocks, cw_ref, cb_ref, rows):
    ncw = cw_ref.shape[0]
    rolled = {}

    def rot(x, s):
        key = (id(x), s)
        if key not in rolled:
            rolled[key] = pltpu.roll(x, s, axis=0)
        return rolled[key]

    out = []
    for ub, pb in zip(u_blocks, prev_blocks):
        uc = cb_ref[...]
        for j in range(ncw):
            s = ncw - 1 - j
            term = ub if s == 0 else jnp.where(rows >= s, rot(ub, s), rot(pb, s))
            uc = uc + term * cw_ref[j:j + 1, :]
        out.append(uc)
    return jnp.concatenate(out, axis=0)


def _scan_block(a, bx, rows):
    d = 1
    while d < SUBLANES:
        m = rows >= d
        bx = bx + a * jnp.where(m, pltpu.roll(bx, d, axis=0), 0.0)
        a = a * jnp.where(m, pltpu.roll(a, d, axis=0), 1.0)
        d *= 2
    return a, bx


def _lru_prompt_kernel(ux_ref, uy_ref, cw_ref, cb_ref, wr_ref, br_ref, wi_ref, bi_ref, lam_ref,
                       o_ref, hl_ref, ul_ref, hc_ref):
    r, cw = ux_ref.shape

    @pl.when(pl.program_id(2) == 0)
    def _():
        hc_ref[...] = jnp.zeros_like(hc_ref)
        ul_ref[...] = jnp.zeros_like(ul_ref)

    rows = lax.broadcasted_iota(jnp.int32, (SUBLANES, cw), 0)
    ub = _blocks(ux_ref[...].astype(F32))
    uc = _conv_blocks(ub, [ul_ref[...]] + ub[:-1], cw_ref, cb_ref, rows)
    ul_ref[...] = ub[-1]
    a, bx = _lru_gates(uc, wr_ref, br_ref, wi_ref, bi_ref, lam_ref)
    carry = hc_ref[...]
    hs = []
    for ab, bb in zip(_blocks(a), _blocks(bx)):
        ab, bb = _scan_block(ab, bb, rows)
        hb = ab * carry + bb
        hs.append(hb)
        carry = jnp.broadcast_to(hb[SUBLANES - 1:SUBLANES, :], (SUBLANES, cw))
    hc_ref[...] = carry
    hl_ref[...] = hs[-1]
    h = jnp.concatenate(hs, axis=0)
    o_ref[...] = (h * uy_ref[...].astype(F32)).astype(o_ref.dtype)


def _lru_sample_kernel(ux_ref, uy_ref, cw_ref, cb_ref, wr_ref, br_ref, wi_ref, bi_ref, lam_ref,
                       h0_ref, buf_ref, o_ref, hl_ref, us_ref):
    r, cw = ux_ref.shape
    rows = lax.broadcasted_iota(jnp.int32, (SUBLANES, cw), 0)
    u = ux_ref[...].astype(F32)
    us_ref[...] = u
    uc = _conv_blocks(_blocks(u), _blocks(buf_ref[...]), cw_ref, cb_ref, rows)
    a, bx = _lru_gates(uc, wr_ref, br_ref, wi_ref, bi_ref, lam_ref)
    hs = []
    for ab, bb, h0 in zip(_blocks(a), _blocks(bx), _blocks(h0_ref[...])):
        ab, bb = _scan_block(ab, bb, rows)
        hs.append(ab * h0 + bb)
    h = jnp.concatenate(hs, axis=0)
    hl_ref[...] = h
    o_ref[...] = (h * uy_ref[...].astype(F32)).astype(o_ref.dtype)


def _lru_specs(l, r, cw, ncw, lb, ux_blk, uy_blk, row_of, cb_of):
    nbc = cw // lb
    vec = pl.BlockSpec((None, 1, cw), lambda *g: (l, 0, cb_of(*g)))
    gate = pl.BlockSpec((None, nbc, lb, lb), lambda *g: (l, cb_of(*g), 0, 0))
    return [pl.BlockSpec((r, cw), lambda *g: (row_of(*g), ux_blk + cb_of(*g))),
            pl.BlockSpec((r, cw), lambda *g: (row_of(*g), uy_blk + cb_of(*g))),
            pl.BlockSpec((None, ncw, cw), lambda *g: (l, 0, cb_of(*g))),
            vec, gate, vec, gate, vec, vec]


def _lru_prompt(proj, ux_col, l, lru_w, b, t):
    conv_w, conv_b3, w_r, b_r3, w_i, b_i3, lam3 = lru_w
    width = lam3.shape[-1]
    lb = w_r.shape[-1]
    ncw = conv_w.shape[1]
    m = proj.shape[0]
    cw = _pick(width, (512, 256, 128))
    r = _pick(t, (256, 128, 64, 32, 16, 8))
    nt = t // r
    specs = _lru_specs(l, r, cw, ncw, lb, ux_col // cw, (ux_col + width) // cw,
                       lambda bi, cb, ti: bi * nt + ti, lambda bi, cb, ti: cb)
    last8 = pl.BlockSpec((None, SUBLANES, cw), lambda bi, cb, ti: (bi, 0, cb))
    return _pcall(
        _lru_prompt_kernel, name="lru_prompt", grid=(b, width // cw, nt),
        in_specs=specs,
        args=(proj, proj, conv_w, conv_b3, w_r, b_r3, w_i, b_i3, lam3),
        out_specs=[pl.BlockSpec((r, cw), lambda bi, cb, ti: (bi * nt + ti, cb)), last8, last8],
        out_shape=[jax.ShapeDtypeStruct((m, width), BF16),
                   jax.ShapeDtypeStruct((b, SUBLANES, width), F32),
                   jax.ShapeDtypeStruct((b, SUBLANES, width), F32)],
        scratch=[pltpu.VMEM((SUBLANES, cw), F32)],
        sem=("arbitrary", "arbitrary", "arbitrary"))


def _lru_sample(proj, ux_col, l, lru_w, row0, h0_rep, buf_fr, yb_prev):
    conv_w, conv_b3, w_r, b_r3, w_i, b_i3, lam3 = lru_w
    width = lam3.shape[-1]
    lb = w_r.shape[-1]
    ncw = conv_w.shape[1]
    ms = h0_rep.shape[1]
    cw = _pick(width, (512, 256, 128))
    r = _pick(ms, (256, 128, 64, 32, 16, 8))
    assert row0 % r == 0
    rb0 = row0 // r
    specs = _lru_specs(l, r, cw, ncw, lb, ux_col // cw, (ux_col + width) // cw,
                       lambda ri, cb: rb0 + ri, lambda ri, cb: cb)
    st = pl.BlockSpec((None, r, cw), lambda ri, cb: (l, ri, cb))
    f32rows = pl.BlockSpec((r, cw), lambda ri, cb: (ri, cb))
    return _pcall(
        _lru_sample_kernel, name="lru_sample", grid=(ms // r, width // cw),
        in_specs=specs + [st, st],
        args=(proj, proj, conv_w, conv_b3, w_r, b_r3, w_i, b_i3, lam3, h0_rep, buf_fr),
        out_specs=[pl.BlockSpec((r, cw), lambda ri, cb: (rb0 + ri, cb)), f32rows, f32rows],
        out_shape=[jax.ShapeDtypeStruct(yb_prev.shape, BF16),
                   jax.ShapeDtypeStruct((ms, width), F32), jax.ShapeDtypeStruct((ms, width), F32)],
        prev=(yb_prev,),
        sem=("arbitrary", "arbitrary"))


def _pack_pair(hi, lo):
    hb = lax.bitcast_convert_type(hi.astype(jnp.bfloat16).astype(F32), jnp.uint32)
    lb = lax.bitcast_convert_type(lo.astype(jnp.bfloat16).astype(F32), jnp.uint32)
    return hb | (lb >> 16)


def _unpack_pair(p):
    hi = lax.bitcast_convert_type(p & jnp.uint32(0xFFFF0000), F32)
    lo = lax.bitcast_convert_type(p << 16, F32)
    return hi, lo


def _router_kernel(x_ref, g_ref, wr_ref, o_ref, hp_ref, *, ne):
    hf = _rms(x_ref[...], g_ref[...])
    half = hf.shape[1] // 2
    hp_ref[...] = _pack_pair(hf[:, :half], hf[:, half:])
    h = hf.astype(BF16)
    logits = jnp.dot(h, wr_ref[...].astype(BF16), preferred_element_type=F32)
    lane = lax.broadcasted_iota(jnp.int32, logits.shape, 1).astype(F32)
    neg = F32(-jnp.inf)
    big = F32(LANES)
    l1 = jnp.where(lane < ne, logits, neg)
    m1 = jnp.max(l1, axis=-1, keepdims=True)
    i1 = jnp.min(jnp.where(l1 == m1, lane, big), axis=-1, keepdims=True)
    l2 = jnp.where(lane == i1, neg, l1)
    m2 = jnp.max(l2, axis=-1, keepdims=True)
    i2 = jnp.min(jnp.where(l2 == m2, lane, big), axis=-1, keepdims=True)
    e2 = jnp.exp(m2 - m1)
    den = 1.0 + e2
    o_ref[...] = (jnp.where(lane == 0.0, i1, 0.0) + jnp.where(lane == 1.0, i2, 0.0)
                  + jnp.where(lane == 2.0, 1.0 / den, 0.0) + jnp.where(lane == 3.0, e2 / den, 0.0))


def _router(x, g3, l, wr):
    m, d = x.shape
    ne = wr.shape[-1]
    assert TOP_K == 2 and ne <= LANES
    wr_pad = jnp.pad(wr, ((0, 0), (0, LANES - ne)))
    tm = _pick(m, (512, 256, 128, 64, 32, 16, 8))
    return _pcall(
        functools.partial(_router_kernel, ne=ne), name="router", grid=(m // tm,),
        in_specs=[pl.BlockSpec((tm, d), lambda i: (i, 0)), _vspec(l, d),
                  pl.BlockSpec((d, LANES), lambda i: (0, 0))],
        args=(x, g3, wr_pad),
        out_specs=[pl.BlockSpec((tm, LANES), lambda i: (i, 0)),
                   pl.BlockSpec((tm, d // 2), lambda i: (i, 0))],
        out_shape=[jax.ShapeDtypeStruct((m, LANES), F32),
                   jax.ShapeDtypeStruct((m, d // 2), jnp.uint32)],
        sem=("arbitrary",))


def _dispatch_tables(rout, ne, tg):
    m = rout.shape[0]
    na = TOP_K * m
    eid = rout[:, :TOP_K].astype(jnp.int32).T.reshape(na)
    onehot = (eid[:, None] == jnp.arange(ne, dtype=jnp.int32)[None, :]).astype(jnp.int32)
    cnt = jnp.sum(onehot, axis=0)
    rank = jnp.sum((jnp.cumsum(onehot, axis=0) - onehot) * onehot, axis=1)
    padded = ((cnt + tg - 1) // tg) * tg
    gend = jnp.cumsum(padded)
    gstart = gend - padded
    pos = gstart[eid] + rank
    p_rows = ((na + tg - 1) // tg + ne) * tg
    n_tiles = p_rows // tg
    tok = jnp.zeros((p_rows,), jnp.int32).at[pos].set(jnp.arange(na, dtype=jnp.int32) % m)
    n_used = (gend[-1] // tg).astype(jnp.int32)
    tstart = jnp.arange(n_tiles, dtype=jnp.int32) * tg
    tstart = jnp.minimum(tstart, (n_used - 1) * tg)
    te = jnp.minimum(jnp.searchsorted(gend, tstart, side="right"), ne - 1).astype(jnp.int32)
    return pos, tok, te, n_used.reshape(1), p_rows


DMA_UNROLL = 8


def _row_gather(src_hbm, dst, idx_of, sem):
    rows = dst.shape[0]
    assert rows % DMA_UNROLL == 0

    def row_copy(r, t):
        return pltpu.make_async_copy(src_hbm.at[pl.ds(t, 1), :], dst.at[pl.ds(r, 1), :], sem)

    def issue(c, carry):
        for u in range(DMA_UNROLL):
            r = c * DMA_UNROLL + u
            row_copy(r, idx_of(r)).start(priority=u % 2)
        return carry

    def drain(c, carry):
        for u in range(DMA_UNROLL):
            row_copy(c * DMA_UNROLL + u, 0).wait()
        return carry

    start = lambda: lax.fori_loop(0, rows // DMA_UNROLL, issue, 0)
    wait = lambda: lax.fori_loop(0, rows // DMA_UNROLL, drain, 0)
    return start, wait


def _moe_gather_kernel(tok_ref, hp_hbm, o_ref, buf, sem):
    half = buf.shape[1]
    start, wait = _row_gather(hp_hbm, buf, lambda r: tok_ref[0, r], sem)
    start()
    wait()
    hi, lo = _unpack_pair(buf[...])
    o_ref[:, :half] = hi.astype(o_ref.dtype)
    o_ref[:, half:] = lo.astype(o_ref.dtype)


def _moe_gather(hp, tok, p_rows):
    m, half = hp.shape
    tr = _pick(p_rows, (GATHER_ROWS, 128, 64, 32, 16))
    nt = p_rows // tr
    return _pcall(
        _moe_gather_kernel, name="moe_gather", grid=(nt,),
        in_specs=[pl.BlockSpec((None, 1, tr), lambda i: (i, 0, 0), memory_space=pltpu.SMEM),
                  pl.BlockSpec(memory_space=pl.ANY)],
        args=(tok.reshape(nt, 1, tr), hp),
        out_specs=[pl.BlockSpec((tr, 2 * half), lambda i: (i, 0))],
        out_shape=[jax.ShapeDtypeStruct((p_rows, 2 * half), BF16)],
        scratch=[pltpu.VMEM((tr, half), jnp.uint32), pltpu.SemaphoreType.DMA(())],
        sem=("arbitrary",))[0]


def _tile_changed(te_ref):
    ti = pl.program_id(1)
    return jnp.logical_or(ti == 0, te_ref[ti] != te_ref[jnp.maximum(ti - 1, 0)])


def _gmm_swiglu_kernel(te_ref, nu_ref, x_ref, wg_ref, wu_ref, o_ref, wgb_ref, wub_ref):
    @pl.when(_tile_changed(te_ref))
    def _():
        _cast_w(wg_ref, wgb_ref)
        _cast_w(wu_ref, wub_ref)

    used = pl.program_id(1) < nu_ref[0]

    @pl.when(used)
    def _():
        x = x_ref[...]
        g = jnp.dot(x, wgb_ref[...], preferred_element_type=F32)
        u = jnp.dot(x, wub_ref[...], preferred_element_type=F32)
        o_ref[...] = (jax.nn.silu(g) * u).astype(o_ref.dtype)

    @pl.when(jnp.logical_not(used))
    def _():
        o_ref[...] = jnp.zeros_like(o_ref)


def _gmm_down_kernel(te_ref, nu_ref, x_ref, w_ref, o_ref, wb_ref):
    @pl.when(_tile_changed(te_ref))
    def _():
        _cast_w(w_ref, wb_ref)

    used = pl.program_id(1) < nu_ref[0]

    @pl.when(used)
    def _():
        y = jnp.dot(x_ref[...], wb_ref[...], preferred_element_type=F32)
        hw = y.shape[1] // 2
        o_ref[...] = _pack_pair(y[:, :hw], y[:, hw:])

    @pl.when(jnp.logical_not(used))
    def _():
        o_ref[...] = jnp.zeros_like(o_ref)


def _gmm_specs(jm, tg, k, tn):
    xrow = lambda j, ti, te, nu: (jnp.minimum(ti, nu[0] - 1), 0)
    wsp = pl.BlockSpec((None, None, k, tn), lambda j, ti, te, nu: (jm, te[ti], 0, j))
    return pl.BlockSpec((tg, k), xrow), wsp, pl.BlockSpec((tg, tn), lambda j, ti, te, nu: (ti, j))


def _gmm_swiglu(xg, wg, wu, jm, te, n_used, tg, tn):
    p, k = xg.shape
    n = wg.shape[-1]
    xs, ws, os_ = _gmm_specs(jm, tg, k, tn)
    return _pcall(
        _gmm_swiglu_kernel, name="gmm_swiglu", grid=(n // tn, p // tg), num_scalar_prefetch=2,
        in_specs=[None, None, xs, ws, ws], args=(te, n_used, xg, wg, wu),
        out_specs=[os_], out_shape=[jax.ShapeDtypeStruct((p, n), BF16)],
        scratch=[pltpu.VMEM((k, tn), BF16), pltpu.VMEM((k, tn), BF16)],
        sem=("arbitrary", "arbitrary"))[0]


def _gmm_down(ug, wd, jm, te, n_used, tg, tn):
    p, k = ug.shape
    n = wd.shape[-1]
    xs, ws, _ = _gmm_specs(jm, tg, k, tn)
    return _pcall(
        _gmm_down_kernel, name="gmm_down", grid=(n // tn, p // tg), num_scalar_prefetch=2,
        in_specs=[None, None, xs, ws], args=(te, n_used, ug, wd),
        out_specs=[pl.BlockSpec((tg, tn // 2), lambda j, ti, te, nu: (ti, j))],
        out_shape=[jax.ShapeDtypeStruct((p, n // 2), jnp.uint32)],
        scratch=[pltpu.VMEM((k, tn), BF16)],
        sem=("arbitrary", "arbitrary"))[0]


def _moe_combine_kernel(pos_ref, x_ref, rw_ref, g_ref, yg_hbm, *rest, tn, np_tiles):
    outs, (buf, sem) = rest[:-2], rest[-2:]
    tc = x_ref.shape[0]
    gathers = [_row_gather(yg_hbm, buf.at[k], lambda r, k=k: pos_ref[0, k * tc + r], sem)
               for k in range(TOP_K)]
    for start, _ in gathers:
        start()
    for _, wait in gathers:
        wait()

    def expand(p):
        hw = tn // 2
        parts = []
        for jj in range(p.shape[1] // hw):
            parts.extend(_unpack_pair(p[:, jj * hw:(jj + 1) * hw]))
        return jnp.concatenate(parts, axis=-1)

    rw = rw_ref[...]
    y = rw[:, TOP_K:TOP_K + 1] * expand(buf[0])
    for k in range(1, TOP_K):
        y = y + rw[:, TOP_K + k:TOP_K + k + 1] * expand(buf[k])
    xn = x_ref[...] + y
    if np_tiles is None:
        outs[0][...] = xn
    else:
        yn = _rms(xn, g_ref[...])
        i = pl.program_id(0)

        @pl.when(i < np_tiles)
        def _():
            outs[0][...] = yn

        @pl.when(i >= np_tiles)
        def _():
            outs[1][...] = yn


def _moe_combine(x, rout, pos, ygp, tn, g3=None, mp=None):
    m, d = x.shape
    tc = _pick(m, (COMBINE_ROWS, 64, 32, 16, 8))
    nt = m // tc
    pos_t = pos.reshape(TOP_K, nt, tc).transpose(1, 0, 2).reshape(nt, 1, TOP_K * tc)
    if g3 is None:
        g3 = jnp.ones((1, 1, d), F32)
        np_tiles = None
        out_specs = [pl.BlockSpec((tc, d), lambda i: (i, 0))]
        out_shape = [jax.ShapeDtypeStruct((m, d), F32)]
    else:
        assert mp % tc == 0
        np_tiles = mp // tc
        out_specs = [pl.BlockSpec((tc, d), lambda i: (jnp.minimum(i, np_tiles - 1), 0)),
                     pl.BlockSpec((tc, d), lambda i: (jnp.maximum(i - np_tiles, 0), 0))]
        out_shape = [jax.ShapeDtypeStruct((mp, d), F32), jax.ShapeDtypeStruct((m - mp, d), F32)]
    return _pcall(
        functools.partial(_moe_combine_kernel, tn=tn, np_tiles=np_tiles), name="moe_combine", grid=(nt,),
        in_specs=[pl.BlockSpec((None, 1, TOP_K * tc), lambda i: (i, 0, 0), memory_space=pltpu.SMEM),
                  pl.BlockSpec((tc, d), lambda i: (i, 0)),
                  pl.BlockSpec((tc, LANES), lambda i: (i, 0)),
                  _vspec(0, d),
                  pl.BlockSpec(memory_space=pl.ANY)],
        args=(pos_t, x, rout, g3, ygp),
        out_specs=out_specs, out_shape=out_shape,
        scratch=[pltpu.VMEM((TOP_K, tc, d // 2), jnp.uint32), pltpu.SemaphoreType.DMA(())],
        sem=("arbitrary",))


def kernel(x_prompt, x_sample, state_ret, state_lru, state_conv, norm_mix, w_in, ret_gn, w_ret_o, conv_w, conv_b, w_rgate, b_rgate, w_igate, b_igate, lru_lambda, w_lru_o, w_out, norm_ffn, ffn_w_gate, ffn_w_up, ffn_w_down, moe_router, moe_w_gate, moe_w_up, moe_w_down, norm_final):
    bp, tp, d = x_prompt.shape
    bs, ts, _ = x_sample.shape
    depth, _, nh, dk, dv = state_ret.shape
    width = state_lru.shape[-1]
    ncw = conv_w.shape[1]
    ne = moe_router.shape[-1]
    assert dk == dv and nh * dk == width == d and ts == SUBLANES and ncw - 1 <= SUBLANES
    mp, ms = bp * tp, bs * ts
    m = mp + ms
    g_col = 2 * nh * dk + nh * dv
    ux_col = g_col + nh * dv
    ga_col = ux_col + 2 * width

    tm = _pick(m, (1024, 512, 256, 128, 64, 32, 16, 8))
    tm_half = _pick(m, (512, 256, 128, 64, 32, 16, 8))
    tn_of = lambda n: _pick(n, (1024, 512, 256, 128))

    vec3 = lambda a: a.reshape(a.shape[0], 1, a.shape[-1])
    norm_mix3, norm_ffn3, ret_gn3 = vec3(norm_mix), vec3(norm_ffn), vec3(ret_gn)
    norm_final3 = norm_final.reshape(1, 1, d)
    lru_w = (conv_w, vec3(conv_b), w_rgate, vec3(b_rgate), w_igate, vec3(b_igate), vec3(lru_lambda))
    h0_rep = jnp.repeat(state_lru, ts, axis=1)
    buf_fr = jnp.pad(state_conv, ((0, 0), (0, 0), (ts - (ncw - 1), 0), (0, 0))).reshape(depth, ms, width)

    cos_p, sin_p = _rope_tables(tp, dk // 2, 0)
    cos_s, sin_s = _rope_tables(ts, dk // 2, PAST_LEN)
    cos = jnp.concatenate([jnp.tile(cos_p, (bp, 1)), jnp.tile(cos_s, (bs, 1))], axis=0)
    sin = jnp.concatenate([jnp.tile(sin_p, (bp, 1)), jnp.tile(sin_s, (bs, 1))], axis=0)
    tn_in = _pick(nh * dk, (1024, 512, 256))

    x = jnp.concatenate([x_prompt.reshape(mp, d), x_sample.reshape(ms, d)], axis=0)
    ret_p = ret_s = y_p = y_s = None
    lrus_p, lrus_s, convs_p, convs_s = [], [], [], []
    for l in range(depth):
        h = _rmsnorm(x, norm_mix3, l, BF16)
        proj = _mm_inproj(h, w_in, (l,), cos, sin, 2 * nh * dk, dk, (g_col, ux_col),
                          (ux_col + width, ga_col), tm, tn_in)

        ya, ret_p = _ret_prompt(proj, ret_gn3, l, depth, bp, tp, nh, dk, dv, ret_p)
        ya, ret_s = _ret_sample(proj, ret_gn3, state_ret, l, mp, bs, ts, nh, dk, dv, ya, ret_s)
        yb, hl_p, ul_p = _lru_prompt(proj, ux_col, l, lru_w, bp, tp)
        yb, hl_s, us_s = _lru_sample(proj, ux_col, l, lru_w, mp, h0_rep, buf_fr, yb)
        lrus_p.append(hl_p[:, -1])
        lrus_s.append(hl_s.reshape(bs, ts, width)[:, -1])
        convs_p.append(ul_p[:, SUBLANES - (ncw - 1):])
        convs_s.append(us_s.reshape(bs, ts, width)[:, ts - (ncw - 1):])

        z = _mm_merge(ya, yb, w_ret_o, w_lru_o, (l,), proj, ga_col, ga_col + d, tm, tn_of(d) // 2)
        x = _mm_resid(z, w_out, (l,), x, tm, tn_of(d))
        j = l // 2
        if l % 2 == 0:
            h2 = _rmsnorm(x, norm_ffn3, l, BF16)
            ff = ffn_w_gate.shape[-1]
            u = _mm_swiglu(h2, ffn_w_gate, ffn_w_up, (j,), tm, _pick(ff, (512, 256, 128)))
            x = _mm_resid(u, ffn_w_down, (j,), x, tm_half, _pick(d, (512, 256, 128)))
        else:
            fe = moe_w_gate.shape[-1]
            tg = _pick(TOP_K * m, (MOE_TILE, 256, 128, 64, 32, 16))
            tn_d = _pick(d, (512, 256))
            rout, hp = _router(x, norm_ffn3, l, moe_router[j])
            pos, tok, te, n_used, p_rows = _dispatch_tables(rout, ne, tg)
            xg = _moe_gather(hp, tok, p_rows)
            ug = _gmm_swiglu(xg, moe_w_gate, moe_w_up, j, te, n_used, tg, _pick(fe, (256, 128)))
            ygp = _gmm_down(ug, moe_w_down, j, te, n_used, tg, tn_d)
            if l == depth - 1:
                y_p, y_s = _moe_combine(x, rout, pos, ygp, tn_d, norm_final3, mp)
            else:
                x = _moe_combine(x, rout, pos, ygp, tn_d)[0]
    if y_p is None:
        y_p = _rmsnorm(x, norm_final3, 0, F32, 0, mp)
        y_s = _rmsnorm(x, norm_final3, 0, F32, mp, ms)
    return (y_p.reshape(bp, tp, d), y_s.reshape(bs, ts, d),
            ret_p, jnp.stack(lrus_p), jnp.stack(convs_p),
            ret_s, jnp.stack(lrus_s), jnp.stack(convs_s))
```

```python
import functools

import jax
import jax.numpy as jnp
from jax import lax
from jax.experimental import pallas as pl
from jax.experimental.pallas import tpu as pltpu

F32 = jnp.float32
BF16 = jnp.bfloat16

ROPE_BASE = 10000.0
LRU_C = 8.0
EPS = 1e-6
RET_CHUNK = 128
PAST_LEN = 16384
TOP_K = 2
SUBLANES = 8
LANES = 128
VMEM_LIMIT = 56 * 1024 * 1024
MOE_TILE = 1024
GATHER_ROWS = 256
COMBINE_ROWS = 128


def _pick(n, prefs):
    for p in prefs:
        if n % p == 0:
            return p
    return n


def _params(*sem):
    return pltpu.CompilerParams(dimension_semantics=sem, vmem_limit_bytes=VMEM_LIMIT)


def _pcall(body, *, name, grid, in_specs, args, out_specs, out_shape, sem, scratch=(), prev=(),
           num_scalar_prefetch=0):
    n_in = len(args)
    prev = tuple(prev) + (None,) * (len(out_shape) - len(prev))
    extra = [(oi, p) for oi, p in enumerate(prev) if p is not None]
    aliases = {n_in + e: oi for e, (oi, _) in enumerate(extra)}
    if extra:
        inner = body

        def body(*refs):
            return inner(*refs[:n_in], *refs[n_in + len(extra):])

    in_specs = list(in_specs) + [pl.BlockSpec(memory_space=pl.ANY)] * len(extra)
    if num_scalar_prefetch:
        grid_spec = pltpu.PrefetchScalarGridSpec(
            num_scalar_prefetch=num_scalar_prefetch, grid=grid, in_specs=in_specs[num_scalar_prefetch:],
            out_specs=tuple(out_specs), scratch_shapes=list(scratch))
        call = pl.pallas_call(body, grid_spec=grid_spec, out_shape=tuple(out_shape),
                              input_output_aliases=aliases, compiler_params=_params(*sem), name=name)
    else:
        call = pl.pallas_call(body, grid=grid, in_specs=in_specs, out_specs=tuple(out_specs),
                              out_shape=tuple(out_shape), scratch_shapes=list(scratch),
                              input_output_aliases=aliases, compiler_params=_params(*sem), name=name)
    return call(*args, *[p for _, p in extra])


def _wspec(pre, k, tn):
    return pl.BlockSpec((None,) * len(pre) + (k, tn), lambda j, i: pre + (0, j))


def _vspec(l, n):
    return pl.BlockSpec((None, 1, n), lambda *_: (l, 0, 0))


def _rms(x, g):
    return (x * lax.rsqrt(jnp.mean(x * x, axis=-1, keepdims=True) + EPS)) * g


def _rmsnorm_kernel(x_ref, g_ref, o_ref):
    o_ref[...] = _rms(x_ref[...], g_ref[...]).astype(o_ref.dtype)


def _rmsnorm(x, g3, l, out_dtype, row0=0, rows=None):
    d = x.shape[1]
    rows = x.shape[0] if rows is None else rows
    tm = _pick(rows, (512, 256, 128, 64, 32, 16, 8))
    assert row0 % tm == 0
    rb0 = row0 // tm
    return _pcall(
        _rmsnorm_kernel, name="rmsnorm", grid=(rows // tm,),
        in_specs=[pl.BlockSpec((tm, d), lambda i: (rb0 + i, 0)), _vspec(l, d)],
        args=(x, g3),
        out_specs=[pl.BlockSpec((tm, d), lambda i: (i, 0))],
        out_shape=[jax.ShapeDtypeStruct((rows, d), out_dtype)],
        sem=("arbitrary",))[0]


def _cast_w(w_ref, wb_ref):
    k = w_ref.shape[0]
    ck = _pick(k, (256, 128, 64, 32, 16))

    def body(c, carry):
        r = pl.multiple_of(c * ck, ck)
        wb_ref[pl.ds(r, ck), :] = w_ref[pl.ds(r, ck), :].astype(BF16)
        return carry

    lax.fori_loop(0, k // ck, body, 0)


def _mm_inproj_kernel(x_ref, w_ref, cos_ref, sin_ref, o_ref, wb_ref, *, nrot, dk, silu_tiles, gelu_tiles):
    j = pl.program_id(0)

    @pl.when(pl.program_id(1) == 0)
    def _():
        _cast_w(w_ref, wb_ref)

    tm, tn = o_ref.shape
    nchunk = 4 if tm % (4 * 16) == 0 else 1

    def row_chunks(epilogue):
        rc = tm // nchunk
        for c in range(nchunk):
            rs = slice(c * rc, (c + 1) * rc)
            epilogue(jnp.dot(x_ref[rs, :], wb_ref[...], preferred_element_type=F32), rs)

    def store(fn):
        def epilogue(acc, rs):
            o_ref[rs, :] = fn(acc).astype(o_ref.dtype)
        return epilogue

    in_tiles = lambda t: jnp.logical_and(j >= t[0], j < t[1])
    is_silu, is_gelu = in_tiles(silu_tiles), in_tiles(gelu_tiles)

    @pl.when(jnp.logical_and(j >= nrot, jnp.logical_not(jnp.logical_or(is_silu, is_gelu))))
    def _():
        row_chunks(store(lambda acc: acc))

    @pl.when(is_silu)
    def _():
        row_chunks(store(jax.nn.silu))

    @pl.when(is_gelu)
    def _():
        row_chunks(store(jax.nn.gelu))

    @pl.when(j < nrot)
    def _():
        scale = jnp.where(j >= nrot // 2, F32(dk ** -0.5), F32(1.0))
        half = dk // 2

        def rotary(acc, rs):
            cos = cos_ref[rs, :] * scale
            sin = sin_ref[rs, :] * scale
            for h in range(tn // dk):
                x1 = acc[:, h * dk:h * dk + half]
                x2 = acc[:, h * dk + half:(h + 1) * dk]
                o_ref[rs, h * dk:h * dk + half] = (x1 * cos - x2 * sin).astype(o_ref.dtype)
                o_ref[rs, h * dk + half:(h + 1) * dk] = (x1 * sin + x2 * cos).astype(o_ref.dtype)

        row_chunks(rotary)


def _mm_inproj(x, w, pre, cos, sin, qk_cols, dk, silu_cols, gelu_cols, tm, tn):
    m, k = x.shape
    n = w.shape[-1]
    half = cos.shape[1]
    assert qk_cols % (2 * tn) == 0 and tn % dk == 0
    assert all(c % tn == 0 for c in silu_cols + gelu_cols)
    tiles = lambda cols: (cols[0] // tn, cols[1] // tn)
    return _pcall(
        functools.partial(_mm_inproj_kernel, nrot=qk_cols // tn, dk=dk, silu_tiles=tiles(silu_cols),
                          gelu_tiles=tiles(gelu_cols)), name="mm_inproj",
        grid=(n // tn, m // tm),
        in_specs=[pl.BlockSpec((tm, k), lambda j, i: (i, 0)), _wspec(pre, k, tn),
                  pl.BlockSpec((tm, half), lambda j, i: (i, 0)),
                  pl.BlockSpec((tm, half), lambda j, i: (i, 0))],
        args=(x, w, cos, sin),
        out_specs=[pl.BlockSpec((tm, tn), lambda j, i: (i, j))],
        out_shape=[jax.ShapeDtypeStruct((m, n), BF16)],
        scratch=[pltpu.VMEM((k, tn), BF16)],
        sem=("arbitrary", "arbitrary"))[0]


def _mm_resid_kernel(x_ref, w_ref, r_ref, o_ref, wb_ref):
    @pl.when(pl.program_id(1) == 0)
    def _():
        _cast_w(w_ref, wb_ref)

    o_ref[...] = r_ref[...] + jnp.dot(x_ref[...], wb_ref[...], preferred_element_type=F32)


def _mm_resid(x, w, pre, r, tm, tn):
    m, k = x.shape
    n = w.shape[-1]
    return _pcall(
        _mm_resid_kernel, name="mm_resid", grid=(n // tn, m // tm),
        in_specs=[pl.BlockSpec((tm, k), lambda j, i: (i, 0)), _wspec(pre, k, tn),
                  pl.BlockSpec((tm, tn), lambda j, i: (i, j))],
        args=(x, w, r),
        out_specs=[pl.BlockSpec((tm, tn), lambda j, i: (i, j))],
        out_shape=[jax.ShapeDtypeStruct((m, n), F32)],
        scratch=[pltpu.VMEM((k, tn), BF16)],
        sem=("arbitrary", "arbitrary"))[0]


def _mm_swiglu_kernel(x_ref, wg_ref, wu_ref, o_ref, wgb_ref, wub_ref):
    @pl.when(pl.program_id(1) == 0)
    def _():
        _cast_w(wg_ref, wgb_ref)
        _cast_w(wu_ref, wub_ref)

    x = x_ref[...]
    g = jnp.dot(x, wgb_ref[...], preferred_element_type=F32)
    u = jnp.dot(x, wub_ref[...], preferred_element_type=F32)
    o_ref[...] = (jax.nn.silu(g) * u).astype(o_ref.dtype)


def _mm_swiglu(x, wg, wu, pre, tm, tn):
    m, k = x.shape
    n = wg.shape[-1]
    return _pcall(
        _mm_swiglu_kernel, name="mm_swiglu", grid=(n // tn, m // tm),
        in_specs=[pl.BlockSpec((tm, k), lambda j, i: (i, 0)), _wspec(pre, k, tn), _wspec(pre, k, tn)],
        args=(x, wg, wu),
        out_specs=[pl.BlockSpec((tm, tn), lambda j, i: (i, j))],
        out_shape=[jax.ShapeDtypeStruct((m, n), BF16)],
        scratch=[pltpu.VMEM((k, tn), BF16), pltpu.VMEM((k, tn), BF16)],
        sem=("arbitrary", "arbitrary"))[0]


def _mm_merge_kernel(a_ref, b_ref, wa_ref, wb_ref, ga_ref, gb_ref, o_ref, wab_ref, wbb_ref):
    @pl.when(pl.program_id(1) == 0)
    def _():
        _cast_w(wa_ref, wab_ref)
        _cast_w(wb_ref, wbb_ref)

    ya = jnp.dot(a_ref[...], wab_ref[...], preferred_element_type=F32)
    yb = jnp.dot(b_ref[...], wbb_ref[...], preferred_element_type=F32)
    ga = ga_ref[...].astype(F32)
    gb = gb_ref[...].astype(F32)
    o_ref[...] = (jax.nn.sigmoid(ga) * ya + jax.nn.sigmoid(gb) * yb).astype(o_ref.dtype)


def _mm_merge(a, b, wa, wb, pre, proj, ga_col, gb_col, tm, tn):
    m, k = a.shape
    n = wa.shape[-1]
    ga_blk, gb_blk = ga_col // tn, gb_col // tn
    return _pcall(
        _mm_merge_kernel, name="mm_merge", grid=(n // tn, m // tm),
        in_specs=[pl.BlockSpec((tm, k), lambda j, i: (i, 0)),
                  pl.BlockSpec((tm, k), lambda j, i: (i, 0)),
                  _wspec(pre, k, tn), _wspec(pre, k, tn),
                  pl.BlockSpec((tm, tn), lambda j, i: (i, ga_blk + j)),
                  pl.BlockSpec((tm, tn), lambda j, i: (i, gb_blk + j))],
        args=(a, b, wa, wb, proj, proj),
        out_specs=[pl.BlockSpec((tm, tn), lambda j, i: (i, j))],
        out_shape=[jax.ShapeDtypeStruct((m, n), BF16)],
        scratch=[pltpu.VMEM((k, tn), BF16), pltpu.VMEM((k, tn), BF16)],
        sem=("arbitrary", "arbitrary"))[0]


def _rope_kernel(inv_ref, cos_ref, sin_ref, *, pos0):
    t, half = cos_ref.shape
    pos = lax.broadcasted_iota(jnp.int32, (t, half), 0).astype(F32) + F32(pos0)
    ang = pos * inv_ref[...]
    cos_ref[...] = jnp.cos(ang)
    sin_ref[...] = jnp.sin(ang)


def _rope_tables(t, half, pos0):
    inv = ROPE_BASE ** (-jnp.arange(half, dtype=F32) / half)
    return pl.pallas_call(
        functools.partial(_rope_kernel, pos0=pos0),
        out_shape=(jax.ShapeDtypeStruct((t, half), F32), jax.ShapeDtypeStruct((t, half), F32)),
        name="rope_tables",
    )(inv.reshape(1, half))


def _decay_tables(c, h, dk, dv):
    log_g = jnp.log1p(-jnp.exp2(-5.0 - jnp.arange(h, dtype=F32)))
    idx = jnp.arange(c)
    rel = idx[:, None] - idx[None, :]
    dmask = jnp.where(rel[None] >= 0,
                      jnp.exp(jnp.maximum(rel, 0)[None].astype(F32) * log_g[:, None, None]), 0.0)
    xi = jnp.exp((idx + 1).astype(F32)[None, :] * log_g[:, None])
    zeta = jnp.exp((c - 1 - idx).astype(F32)[None, :] * log_g[:, None])
    g_c = jnp.exp(c * log_g)
    xi_t = jnp.broadcast_to(xi[:, :, None], (h, c, dv))
    zeta_t = jnp.broadcast_to(zeta[:, :, None], (h, c, dk))
    return dmask, xi_t, zeta_t, g_c


def _ret_head(qb, kb, v, g, dm, xi, zt, gn, s, gc):
    scores = lax.dot_general(qb, kb, (((1,), (1,)), ((), ())), preferred_element_type=F32)
    intra = jnp.dot((scores * dm).astype(BF16), v, preferred_element_type=F32)
    cross = jnp.dot(qb, s.astype(BF16), preferred_element_type=F32) * xi
    kz = (kb.astype(F32) * zt).astype(BF16)
    upd = lax.dot_general(kz, v, (((0,), (0,)), ((), ())), preferred_element_type=F32)
    s_new = gc * s + upd
    o = intra + cross
    mu = jnp.mean(o, axis=-1, keepdims=True)
    d = o - mu
    var = jnp.mean(d * d, axis=-1, keepdims=True)
    y = d * lax.rsqrt(var + EPS) * gn
    return (g * y).astype(BF16), s_new


def _ret_prompt_kernel(gc_ref, q_ref, k_ref, v_ref, g_ref, dm_ref, xi_ref, zt_ref,
                       gn_ref, o_ref, s_ref, *, nh, dk, dv):
    @pl.when(pl.program_id(1) == 0)
    def _():
        s_ref[...] = jnp.zeros_like(s_ref)

    for h in range(nh):
        out, s_new = _ret_head(
            q_ref[:, h * dk:(h + 1) * dk], k_ref[:, h * dk:(h + 1) * dk],
            v_ref[:, h * dv:(h + 1) * dv], g_ref[:, h * dv:(h + 1) * dv].astype(F32),
            dm_ref[h], xi_ref[h], zt_ref[h], gn_ref[:, h * dv:(h + 1) * dv],
            s_ref[h], gc_ref[h])
        o_ref[:, h * dv:(h + 1) * dv] = out
        s_ref[h] = s_new


def _ret_prompt(proj, gn3, l, depth, b, t, nh, dk, dv, s_prev):
    c = RET_CHUNK if t % RET_CHUNK == 0 else t
    nc = t // c
    w = nh * dk
    m = proj.shape[0]
    dmask, xi_t, zeta_t, g_c = _decay_tables(c, nh, dk, dv)
    row = lambda bi, ci: bi * nc + ci
    full3 = lambda bi, ci: (0, 0, 0)
    return _pcall(
        functools.partial(_ret_prompt_kernel, nh=nh, dk=dk, dv=dv), name="ret_prompt", grid=(b, nc),
        in_specs=[pl.BlockSpec(memory_space=pltpu.SMEM),
                  pl.BlockSpec((c, w), lambda bi, ci: (row(bi, ci), 0)),
                  pl.BlockSpec((c, w), lambda bi, ci: (row(bi, ci), 1)),
                  pl.BlockSpec((c, w), lambda bi, ci: (row(bi, ci), 2)),
                  pl.BlockSpec((c, w), lambda bi, ci: (row(bi, ci), 3)),
                  pl.BlockSpec((nh, c, c), full3),
                  pl.BlockSpec((nh, c, dv), full3),
                  pl.BlockSpec((nh, c, dk), full3),
                  _vspec(l, nh * dv)],
        args=(g_c, proj, proj, proj, proj, dmask, xi_t, zeta_t, gn3),
        out_specs=[pl.BlockSpec((c, nh * dv), lambda bi, ci: (row(bi, ci), 0)),
                   pl.BlockSpec((None, None, nh, dk, dv), lambda bi, ci: (l, bi, 0, 0, 0))],
        out_shape=[jax.ShapeDtypeStruct((m, nh * dv), BF16),
                   jax.ShapeDtypeStruct((depth, b, nh, dk, dv), F32)],
        prev=(None, s_prev),
        sem=("arbitrary", "arbitrary"))


def _ret_sample_kernel(gc_ref, q_ref, k_ref, v_ref, g_ref, dm_ref, xi_ref, zt_ref,
                       gn_ref, s0_ref, o_ref, s_ref, *, nh, dk, dv, bb, ts):
    q = q_ref[...].astype(F32)
    k = k_ref[...].astype(F32)
    v = v_ref[...].astype(F32)
    g = g_ref[...].astype(F32)
    seqs = []
    for i in range(bb):
        r0, r1 = i * ts, (i + 1) * ts
        outs = []
        for h in range(nh):
            out, s_new = _ret_head(
                q[r0:r1, h * dk:(h + 1) * dk].astype(BF16), k[r0:r1, h * dk:(h + 1) * dk].astype(BF16),
                v[r0:r1, h * dv:(h + 1) * dv].astype(BF16), g[r0:r1, h * dv:(h + 1) * dv],
                dm_ref[h], xi_ref[h], zt_ref[h], gn_ref[:, h * dv:(h + 1) * dv],
                s0_ref[i, h], gc_ref[h])
            s_ref[i, h] = s_new
            outs.append(out.astype(F32))
        seqs.append(jnp.concatenate(outs, axis=-1))
    o_ref[...] = jnp.concatenate(seqs, axis=0).astype(o_ref.dtype)


def _ret_sample(proj, gn3, s0_all, l, row0, bs, ts, nh, dk, dv, ya_prev, s_prev):
    bb = 2 if bs % 2 == 0 else 1
    rows = bb * ts
    w = nh * dk
    depth = s0_all.shape[0]
    dmask, xi_t, zeta_t, g_c = _decay_tables(ts, nh, dk, dv)
    assert row0 % rows == 0
    rb0 = row0 // rows
    full3 = lambda i: (0, 0, 0)
    return _pcall(
        functools.partial(_ret_sample_kernel, nh=nh, dk=dk, dv=dv, bb=bb, ts=ts), name="ret_sample",
        grid=(bs // bb,),
        in_specs=[pl.BlockSpec(memory_space=pltpu.SMEM),
                  pl.BlockSpec((rows, w), lambda i: (rb0 + i, 0)),
                  pl.BlockSpec((rows, w), lambda i: (rb0 + i, 1)),
                  pl.BlockSpec((rows, w), lambda i: (rb0 + i, 2)),
                  pl.BlockSpec((rows, w), lambda i: (rb0 + i, 3)),
                  pl.BlockSpec((nh, ts, ts), full3),
                  pl.BlockSpec((nh, ts, dv), full3),
                  pl.BlockSpec((nh, ts, dk), full3),
                  _vspec(l, nh * dv),
                  pl.BlockSpec((None, bb, nh, dk, dv), lambda i: (l, i, 0, 0, 0))],
        args=(g_c, proj, proj, proj, proj, dmask, xi_t, zeta_t, gn3, s0_all),
        out_specs=[pl.BlockSpec((rows, nh * dv), lambda i: (rb0 + i, 0)),
                   pl.BlockSpec((None, bb, nh, dk, dv), lambda i: (l, i, 0, 0, 0))],
        out_shape=[jax.ShapeDtypeStruct(ya_prev.shape, BF16),
                   jax.ShapeDtypeStruct((depth, bs, nh, dk, dv), F32)],
        prev=(ya_prev, s_prev),
        sem=("arbitrary",))


def _lru_gates(uc, wr_ref, br_ref, wi_ref, bi_ref, lam_ref):
    nb, lb, _ = wr_ref.shape
    ucb = uc.astype(BF16)
    rl, il = [], []
    for n in range(nb):
        xb = ucb[:, n * lb:(n + 1) * lb]
        rl.append(jnp.dot(xb, wr_ref[n].astype(BF16), preferred_element_type=F32))
        il.append(jnp.dot(xb, wi_ref[n].astype(BF16), preferred_element_type=F32))
    r = jax.nn.sigmoid(jnp.concatenate(rl, axis=-1) + br_ref[...])
    i = jax.nn.sigmoid(jnp.concatenate(il, axis=-1) + bi_ref[...])
    log_a = -LRU_C * r * jax.nn.softplus(-lam_ref[...])
    a = jnp.exp(log_a)
    bx = jnp.sqrt(-jnp.tanh(log_a) * (a * a + 1.0)) * (i * uc)
    return a, bx


def _blocks(x):
    return [x[lo:lo + SUBLANES, :] for lo in range(0, x.shape[0], SUBLANES)]


def _conv_blocks(u_blocks, prev_blocks, cw_ref, cb_ref, rows):
    ncw = cw_ref.shape[0]
    rolled = {}

    def rot(x, s):
        key = (id(x), s)
        if key not in rolled:
            rolled[key] = pltpu.roll(x, s, axis=0)
        return rolled[key]

    out = []
    for ub, pb in zip(u_blocks, prev_blocks):
        uc = cb_ref[...]
        for j in range(ncw):
            s = ncw - 1 - j
            term = ub if s == 0 else jnp.where(rows >= s, rot(ub, s), rot(pb, s))
            uc = uc + term * cw_ref[j:j + 1, :]
        out.append(uc)
    return jnp.concatenate(out, axis=0)


def _scan_block(a, bx, rows):
    d = 1
    while d < SUBLANES:
        m = rows >= d
        bx = bx + a * jnp.where(m, pltpu.roll(bx, d, axis=0), 0.0)
        a = a * jnp.where(m, pltpu.roll(a, d, axis=0), 1.0)
        d *= 2
    return a, bx


def _lru_prompt_kernel(ux_ref, uy_ref, cw_ref, cb_ref, wr_ref, br_ref, wi_ref, bi_ref, lam_ref,
                       o_ref, hl_ref, ul_ref, hc_ref):
    r, cw = ux_ref.shape

    @pl.when(pl.program_id(2) == 0)
    def _():
        hc_ref[...] = jnp.zeros_like(hc_ref)
        ul_ref[...] = jnp.zeros_like(ul_ref)

    rows = lax.broadcasted_iota(jnp.int32, (SUBLANES, cw), 0)
    ub = _blocks(ux_ref[...].astype(F32))
    uc = _conv_blocks(ub, [ul_ref[...]] + ub[:-1], cw_ref, cb_ref, rows)
    ul_ref[...] = ub[-1]
    a, bx = _lru_gates(uc, wr_ref, br_ref, wi_ref, bi_ref, lam_ref)
    carry = hc_ref[...]
    hs = []
    for ab, bb in zip(_blocks(a), _blocks(bx)):
        ab, bb = _scan_block(ab, bb, rows)
        hb = ab * carry + bb
        hs.append(hb)
        carry = jnp.broadcast_to(hb[SUBLANES - 1:SUBLANES, :], (SUBLANES, cw))
    hc_ref[...] = carry
    hl_ref[...] = hs[-1]
    h = jnp.concatenate(hs, axis=0)
    o_ref[...] = (h * uy_ref[...].astype(F32)).astype(o_ref.dtype)


def _lru_sample_kernel(ux_ref, uy_ref, cw_ref, cb_ref, wr_ref, br_ref, wi_ref, bi_ref, lam_ref,
                       h0_ref, buf_ref, o_ref, hl_ref, us_ref):
    r, cw = ux_ref.shape
    rows = lax.broadcasted_iota(jnp.int32, (SUBLANES, cw), 0)
    u = ux_ref[...].astype(F32)
    us_ref[...] = u
    uc = _conv_blocks(_blocks(u), _blocks(buf_ref[...]), cw_ref, cb_ref, rows)
    a, bx = _lru_gates(uc, wr_ref, br_ref, wi_ref, bi_ref, lam_ref)
    hs = []
    for ab, bb, h0 in zip(_blocks(a), _blocks(bx), _blocks(h0_ref[...])):
        ab, bb = _scan_block(ab, bb, rows)
        hs.append(ab * h0 + bb)
    h = jnp.concatenate(hs, axis=0)
    hl_ref[...] = h
    o_ref[...] = (h * uy_ref[...].astype(F32)).astype(o_ref.dtype)


def _lru_specs(l, r, cw, ncw, lb, ux_blk, uy_blk, row_of, cb_of):
    nbc = cw // lb
    vec = pl.BlockSpec((None, 1, cw), lambda *g: (l, 0, cb_of(*g)))
    gate = pl.BlockSpec((None, nbc, lb, lb), lambda *g: (l, cb_of(*g), 0, 0))
    return [pl.BlockSpec((r, cw), lambda *g: (row_of(*g), ux_blk + cb_of(*g))),
            pl.BlockSpec((r, cw), lambda *g: (row_of(*g), uy_blk + cb_of(*g))),
            pl.BlockSpec((None, ncw, cw), lambda *g: (l, 0, cb_of(*g))),
            vec, gate, vec, gate, vec, vec]


def _lru_prompt(proj, ux_col, l, lru_w, b, t):
    conv_w, conv_b3, w_r, b_r3, w_i, b_i3, lam3 = lru_w
    width = lam3.shape[-1]
    lb = w_r.shape[-1]
    ncw = conv_w.shape[1]
    m = proj.shape[0]
    cw = _pick(width, (512, 256, 128))
    r = _pick(t, (256, 128, 64, 32, 16, 8))
    nt = t // r
    specs = _lru_specs(l, r, cw, ncw, lb, ux_col // cw, (ux_col + width) // cw,
                       lambda bi, cb, ti: bi * nt + ti, lambda bi, cb, ti: cb)
    last8 = pl.BlockSpec((None, SUBLANES, cw), lambda bi, cb, ti: (bi, 0, cb))
    return _pcall(
        _lru_prompt_kernel, name="lru_prompt", grid=(b, width // cw, nt),
        in_specs=specs,
        args=(proj, proj, conv_w, conv_b3, w_r, b_r3, w_i, b_i3, lam3),
        out_specs=[pl.BlockSpec((r, cw), lambda bi, cb, ti: (bi * nt + ti, cb)), last8, last8],
        out_shape=[jax.ShapeDtypeStruct((m, width), BF16),
                   jax.ShapeDtypeStruct((b, SUBLANES, width), F32),
                   jax.ShapeDtypeStruct((b, SUBLANES, width), F32)],
        scratch=[pltpu.VMEM((SUBLANES, cw), F32)],
        sem=("arbitrary", "arbitrary", "arbitrary"))


def _lru_sample(proj, ux_col, l, lru_w, row0, h0_rep, buf_fr, yb_prev):
    conv_w, conv_b3, w_r, b_r3, w_i, b_i3, lam3 = lru_w
    width = lam3.shape[-1]
    lb = w_r.shape[-1]
    ncw = conv_w.shape[1]
    ms = h0_rep.shape[1]
    cw = _pick(width, (512, 256, 128))
    r = _pick(ms, (256, 128, 64, 32, 16, 8))
    assert row0 % r == 0
    rb0 = row0 // r
    specs = _lru_specs(l, r, cw, ncw, lb, ux_col // cw, (ux_col + width) // cw,
                       lambda ri, cb: rb0 + ri, lambda ri, cb: cb)
    st = pl.BlockSpec((None, r, cw), lambda ri, cb: (l, ri, cb))
    f32rows = pl.BlockSpec((r, cw), lambda ri, cb: (ri, cb))
    return _pcall(
        _lru_sample_kernel, name="lru_sample", grid=(ms // r, width // cw),
        in_specs=specs + [st, st],
        args=(proj, proj, conv_w, conv_b3, w_r, b_r3, w_i, b_i3, lam3, h0_rep, buf_fr),
        out_specs=[pl.BlockSpec((r, cw), lambda ri, cb: (rb0 + ri, cb)), f32rows, f32rows],
        out_shape=[jax.ShapeDtypeStruct(yb_prev.shape, BF16),
                   jax.ShapeDtypeStruct((ms, width), F32), jax.ShapeDtypeStruct((ms, width), F32)],
        prev=(yb_prev,),
        sem=("arbitrary", "arbitrary"))


def _pack_pair(hi, lo):
    hb = lax.bitcast_convert_type(hi.astype(jnp.bfloat16).astype(F32), jnp.uint32)
    lb = lax.bitcast_convert_type(lo.astype(jnp.bfloat16).astype(F32), jnp.uint32)
    return hb | (lb >> 16)


def _unpack_pair(p):
    hi = lax.bitcast_convert_type(p & jnp.uint32(0xFFFF0000), F32)
    lo = lax.bitcast_convert_type(p << 16, F32)
    return hi, lo


def _router_kernel(x_ref, g_ref, wr_ref, o_ref, hp_ref, *, ne):
    hf = _rms(x_ref[...], g_ref[...])
    half = hf.shape[1] // 2
    hp_ref[...] = _pack_pair(hf[:, :half], hf[:, half:])
    h = hf.astype(BF16)
    logits = jnp.dot(h, wr_ref[...].astype(BF16), preferred_element_type=F32)
    lane = lax.broadcasted_iota(jnp.int32, logits.shape, 1).astype(F32)
    neg = F32(-jnp.inf)
    big = F32(LANES)
    l1 = jnp.where(lane < ne, logits, neg)
    m1 = jnp.max(l1, axis=-1, keepdims=True)
    i1 = jnp.min(jnp.where(l1 == m1, lane, big), axis=-1, keepdims=True)
    l2 = jnp.where(lane == i1, neg, l1)
    m2 = jnp.max(l2, axis=-1, keepdims=True)
    i2 = jnp.min(jnp.where(l2 == m2, lane, big), axis=-1, keepdims=True)
    e2 = jnp.exp(m2 - m1)
    den = 1.0 + e2
    o_ref[...] = (jnp.where(lane == 0.0, i1, 0.0) + jnp.where(lane == 1.0, i2, 0.0)
                  + jnp.where(lane == 2.0, 1.0 / den, 0.0) + jnp.where(lane == 3.0, e2 / den, 0.0))


def _router(x, g3, l, wr):
    m, d = x.shape
    ne = wr.shape[-1]
    assert TOP_K == 2 and ne <= LANES
    wr_pad = jnp.pad(wr, ((0, 0), (0, LANES - ne)))
    tm = _pick(m, (512, 256, 128, 64, 32, 16, 8))
    return _pcall(
        functools.partial(_router_kernel, ne=ne), name="router", grid=(m // tm,),
        in_specs=[pl.BlockSpec((tm, d), lambda i: (i, 0)), _vspec(l, d),
                  pl.BlockSpec((d, LANES), lambda i: (0, 0))],
        args=(x, g3, wr_pad),
        out_specs=[pl.BlockSpec((tm, LANES), lambda i: (i, 0)),
                   pl.BlockSpec((tm, d // 2), lambda i: (i, 0))],
        out_shape=[jax.ShapeDtypeStruct((m, LANES), F32),
                   jax.ShapeDtypeStruct((m, d // 2), jnp.uint32)],
        sem=("arbitrary",))


def _dispatch_tables(rout, ne, tg):
    m = rout.shape[0]
    na = TOP_K * m
    eid = rout[:, :TOP_K].astype(jnp.int32).T.reshape(na)
    onehot = (eid[:, None] == jnp.arange(ne, dtype=jnp.int32)[None, :]).astype(jnp.int32)
    cnt = jnp.sum(onehot, axis=0)
    rank = jnp.sum((jnp.cumsum(onehot, axis=0) - onehot) * onehot, axis=1)
    padded = ((cnt + tg - 1) // tg) * tg
    gend = jnp.cumsum(padded)
    gstart = gend - padded
    pos = gstart[eid] + rank
    p_rows = ((na + tg - 1) // tg + ne) * tg
    n_tiles = p_rows // tg
    tok = jnp.zeros((p_rows,), jnp.int32).at[pos].set(jnp.arange(na, dtype=jnp.int32) % m)
    n_used = (gend[-1] // tg).astype(jnp.int32)
    tstart = jnp.arange(n_tiles, dtype=jnp.int32) * tg
    tstart = jnp.minimum(tstart, (n_used - 1) * tg)
    te = jnp.minimum(jnp.searchsorted(gend, tstart, side="right"), ne - 1).astype(jnp.int32)
    return pos, tok, te, n_used.reshape(1), p_rows


DMA_UNROLL = 8


def _row_gather(src_hbm, dst, idx_of, sem):
    rows = dst.shape[0]
    assert rows % DMA_UNROLL == 0

    def row_copy(r, t):
        return pltpu.make_async_copy(src_hbm.at[pl.ds(t, 1), :], dst.at[pl.ds(r, 1), :], sem)

    def issue(c, carry):
        for u in range(DMA_UNROLL):
            r = c * DMA_UNROLL + u
            row_copy(r, idx_of(r)).start()
        return carry

    def drain(c, carry):
        for u in range(DMA_UNROLL):
            row_copy(c * DMA_UNROLL + u, 0).wait()
        return carry

    start = lambda: lax.fori_loop(0, rows // DMA_UNROLL, issue, 0)
    wait = lambda: lax.fori_loop(0, rows // DMA_UNROLL, drain, 0)
    return start, wait


def _moe_gather_kernel(tok_ref, hp_ref, o_ref):
    tr, d = o_ref.shape
    half = d // 2

    group = 2 * SUBLANES

    def body(g, carry):
        r0 = pl.multiple_of(g * group, group)
        rows = [hp_ref[pl.ds(tok_ref[0, r0 + u], 1), :] for u in range(group)]
        hi, lo = _unpack_pair(jnp.concatenate(rows, axis=0))
        o_ref[pl.ds(r0, group), pl.ds(0, half)] = hi.astype(o_ref.dtype)
        o_ref[pl.ds(r0, group), pl.ds(half, half)] = lo.astype(o_ref.dtype)
        return carry

    lax.fori_loop(0, tr // group, body, 0)


def _moe_gather(hp, tok, p_rows):
    m, half = hp.shape
    tr = _pick(p_rows, (GATHER_ROWS, 128, 64, 32, 16))
    nt = p_rows // tr
    return _pcall(
        _moe_gather_kernel, name="moe_gather", grid=(nt,),
        in_specs=[pl.BlockSpec((None, 1, tr), lambda i: (i, 0, 0), memory_space=pltpu.SMEM),
                  pl.BlockSpec((m, half), lambda i: (0, 0), pipeline_mode=pl.Buffered(1))],
        args=(tok.reshape(nt, 1, tr), hp),
        out_specs=[pl.BlockSpec((tr, 2 * half), lambda i: (i, 0))],
        out_shape=[jax.ShapeDtypeStruct((p_rows, 2 * half), BF16)],
        sem=("arbitrary",))[0]


def _tile_changed(te_ref):
    ti = pl.program_id(1)
    return jnp.logical_or(ti == 0, te_ref[ti] != te_ref[jnp.maximum(ti - 1, 0)])


def _gmm_swiglu_kernel(te_ref, nu_ref, x_ref, wg_ref, wu_ref, o_ref, wgb_ref, wub_ref):
    @pl.when(_tile_changed(te_ref))
    def _():
        _cast_w(wg_ref, wgb_ref)
        _cast_w(wu_ref, wub_ref)

    used = pl.program_id(1) < nu_ref[0]

    @pl.when(used)
    def _():
        x = x_ref[...]
        g = jnp.dot(x, wgb_ref[...], preferred_element_type=F32)
        u = jnp.dot(x, wub_ref[...], preferred_element_type=F32)
        o_ref[...] = (jax.nn.silu(g) * u).astype(o_ref.dtype)

    @pl.when(jnp.logical_not(used))
    def _():
        o_ref[...] = jnp.zeros_like(o_ref)


def _gmm_down_kernel(te_ref, nu_ref, x_ref, w_ref, o_ref, wb_ref):
    @pl.when(_tile_changed(te_ref))
    def _():
        _cast_w(w_ref, wb_ref)

    used = pl.program_id(1) < nu_ref[0]

    @pl.when(used)
    def _():
        y = jnp.dot(x_ref[...], wb_ref[...], preferred_element_type=F32)
        hw = y.shape[1] // 2
        o_ref[...] = _pack_pair(y[:, :hw], y[:, hw:])

    @pl.when(jnp.logical_not(used))
    def _():
        o_ref[...] = jnp.zeros_like(o_ref)


def _gmm_specs(jm, tg, k, tn):
    xrow = lambda j, ti, te, nu: (jnp.minimum(ti, nu[0] - 1), 0)
    wsp = pl.BlockSpec((None, None, k, tn), lambda j, ti, te, nu: (jm, te[ti], 0, j))
    return pl.BlockSpec((tg, k), xrow), wsp, pl.BlockSpec((tg, tn), lambda j, ti, te, nu: (ti, j))


def _gmm_swiglu(xg, wg, wu, jm, te, n_used, tg, tn):
    p, k = xg.shape
    n = wg.shape[-1]
    xs, ws, os_ = _gmm_specs(jm, tg, k, tn)
    return _pcall(
        _gmm_swiglu_kernel, name="gmm_swiglu", grid=(n // tn, p // tg), num_scalar_prefetch=2,
        in_specs=[None, None, xs, ws, ws], args=(te, n_used, xg, wg, wu),
        out_specs=[os_], out_shape=[jax.ShapeDtypeStruct((p, n), BF16)],
        scratch=[pltpu.VMEM((k, tn), BF16), pltpu.VMEM((k, tn), BF16)],
        sem=("arbitrary", "arbitrary"))[0]


def _gmm_down(ug, wd, jm, te, n_used, tg, tn):
    p, k = ug.shape
    n = wd.shape[-1]
    xs, ws, _ = _gmm_specs(jm, tg, k, tn)
    return _pcall(
        _gmm_down_kernel, name="gmm_down", grid=(n // tn, p // tg), num_scalar_prefetch=2,
        in_specs=[None, None, xs, ws], args=(te, n_used, ug, wd),
        out_specs=[pl.BlockSpec((tg, tn // 2), lambda j, ti, te, nu: (ti, j))],
        out_shape=[jax.ShapeDtypeStruct((p, n // 2), jnp.uint32)],
        scratch=[pltpu.VMEM((k, tn), BF16)],
        sem=("arbitrary", "arbitrary"))[0]


def _moe_combine_kernel(pos_ref, x_ref, rw_ref, g_ref, yg_hbm, *rest, tn, np_tiles):
    outs, (buf, sem) = rest[:-2], rest[-2:]
    tc = x_ref.shape[0]
    gathers = [_row_gather(yg_hbm, buf.at[k], lambda r, k=k: pos_ref[0, k * tc + r], sem)
               for k in range(TOP_K)]
    for start, _ in gathers:
        start()
    for _, wait in gathers:
        wait()

    def expand(p):
        hw = tn // 2
        parts = []
        for jj in range(p.shape[1] // hw):
            parts.extend(_unpack_pair(p[:, jj * hw:(jj + 1) * hw]))
        return jnp.concatenate(parts, axis=-1)

    rw = rw_ref[...]
    y = rw[:, TOP_K:TOP_K + 1] * expand(buf[0])
    for k in range(1, TOP_K):
        y = y + rw[:, TOP_K + k:TOP_K + k + 1] * expand(buf[k])
    xn = x_ref[...] + y
    if np_tiles is None:
        outs[0][...] = xn
    else:
        yn = _rms(xn, g_ref[...])
        i = pl.program_id(0)

        @pl.when(i < np_tiles)
        def _():
            outs[0][...] = yn

        @pl.when(i >= np_tiles)
        def _():
            outs[1][...] = yn


def _moe_combine(x, rout, pos, ygp, tn, g3=None, mp=None):
    m, d = x.shape
    tc = _pick(m, (COMBINE_ROWS, 64, 32, 16, 8))
    nt = m // tc
    pos_t = pos.reshape(TOP_K, nt, tc).transpose(1, 0, 2).reshape(nt, 1, TOP_K * tc)
    if g3 is None:
        g3 = jnp.ones((1, 1, d), F32)
        np_tiles = None
        out_specs = [pl.BlockSpec((tc, d), lambda i: (i, 0))]
        out_shape = [jax.ShapeDtypeStruct((m, d), F32)]
    else:
        assert mp % tc == 0
        np_tiles = mp // tc
        out_specs = [pl.BlockSpec((tc, d), lambda i: (jnp.minimum(i, np_tiles - 1), 0)),
                     pl.BlockSpec((tc, d), lambda i: (jnp.maximum(i - np_tiles, 0), 0))]
        out_shape = [jax.ShapeDtypeStruct((mp, d), F32), jax.ShapeDtypeStruct((m - mp, d), F32)]
    return _pcall(
        functools.partial(_moe_combine_kernel, tn=tn, np_tiles=np_tiles), name="moe_combine", grid=(nt,),
        in_specs=[pl.BlockSpec((None, 1, TOP_K * tc), lambda i: (i, 0, 0), memory_space=pltpu.SMEM),
                  pl.BlockSpec((tc, d), lambda i: (i, 0)),
                  pl.BlockSpec((tc, LANES), lambda i: (i, 0)),
                  _vspec(0, d),
                  pl.BlockSpec(memory_space=pl.ANY)],
        args=(pos_t, x, rout, g3, ygp),
        out_specs=out_specs, out_shape=out_shape,
        scratch=[pltpu.VMEM((TOP_K, tc, d // 2), jnp.uint32), pltpu.SemaphoreType.DMA(())],
        sem=("arbitrary",))


def kernel(x_prompt, x_sample, state_ret, state_lru, state_conv, norm_mix, w_in, ret_gn, w_ret_o, conv_w, conv_b, w_rgate, b_rgate, w_igate, b_igate, lru_lambda, w_lru_o, w_out, norm_ffn, ffn_w_gate, ffn_w_up, ffn_w_down, moe_router, moe_w_gate, moe_w_up, moe_w_down, norm_final):
    bp, tp, d = x_prompt.shape
    bs, ts, _ = x_sample.shape
    depth, _, nh, dk, dv = state_ret.shape
    width = state_lru.shape[-1]
    ncw = conv_w.shape[1]
    ne = moe_router.shape[-1]
    assert dk == dv and nh * dk == width == d and ts == SUBLANES and ncw - 1 <= SUBLANES
    mp, ms = bp * tp, bs * ts
    m = mp + ms
    g_col = 2 * nh * dk + nh * dv
    ux_col = g_col + nh * dv
    ga_col = ux_col + 2 * width

    tm = _pick(m, (1024, 512, 256, 128, 64, 32, 16, 8))
    tm_half = _pick(m, (512, 256, 128, 64, 32, 16, 8))
    tn_of = lambda n: _pick(n, (1024, 512, 256, 128))

    vec3 = lambda a: a.reshape(a.shape[0], 1, a.shape[-1])
    norm_mix3, norm_ffn3, ret_gn3 = vec3(norm_mix), vec3(norm_ffn), vec3(ret_gn)
    norm_final3 = norm_final.reshape(1, 1, d)
    lru_w = (conv_w, vec3(conv_b), w_rgate, vec3(b_rgate), w_igate, vec3(b_igate), vec3(lru_lambda))
    h0_rep = jnp.repeat(state_lru, ts, axis=1)
    buf_fr = jnp.pad(state_conv, ((0, 0), (0, 0), (ts - (ncw - 1), 0), (0, 0))).reshape(depth, ms, width)

    cos_p, sin_p = _rope_tables(tp, dk // 2, 0)
    cos_s, sin_s = _rope_tables(ts, dk // 2, PAST_LEN)
    cos = jnp.concatenate([jnp.tile(cos_p, (bp, 1)), jnp.tile(cos_s, (bs, 1))], axis=0)
    sin = jnp.concatenate([jnp.tile(sin_p, (bp, 1)), jnp.tile(sin_s, (bs, 1))], axis=0)
    tn_in = _pick(nh * dk, (1024, 512, 256))

    x = jnp.concatenate([x_prompt.reshape(mp, d), x_sample.reshape(ms, d)], axis=0)
    ret_p = ret_s = y_p = y_s = None
    lrus_p, lrus_s, convs_p, convs_s = [], [], [], []
    for l in range(depth):
        h = _rmsnorm(x, norm_mix3, l, BF16)
        proj = _mm_inproj(h, w_in, (l,), cos, sin, 2 * nh * dk, dk, (g_col, ux_col),
                          (ux_col + width, ga_col), tm, tn_in)

        ya, ret_p = _ret_prompt(proj, ret_gn3, l, depth, bp, tp, nh, dk, dv, ret_p)
        ya, ret_s = _ret_sample(proj, ret_gn3, state_ret, l, mp, bs, ts, nh, dk, dv, ya, ret_s)
        yb, hl_p, ul_p = _lru_prompt(proj, ux_col, l, lru_w, bp, tp)
        yb, hl_s, us_s = _lru_sample(proj, ux_col, l, lru_w, mp, h0_rep, buf_fr, yb)
        lrus_p.append(hl_p[:, -1])
        lrus_s.append(hl_s.reshape(bs, ts, width)[:, -1])
        convs_p.append(ul_p[:, SUBLANES - (ncw - 1):])
        convs_s.append(us_s.reshape(bs, ts, width)[:, ts - (ncw - 1):])

        z = _mm_merge(ya, yb, w_ret_o, w_lru_o, (l,), proj, ga_col, ga_col + d, tm, tn_of(d) // 2)
        x = _mm_resid(z, w_out, (l,), x, tm, tn_of(d))
        j = l // 2
        if l % 2 == 0:
            h2 = _rmsnorm(x, norm_ffn3, l, BF16)
            ff = ffn_w_gate.shape[-1]
            u = _mm_swiglu(h2, ffn_w_gate, ffn_w_up, (j,), tm, _pick(ff, (512, 256, 128)))
            x = _mm_resid(u, ffn_w_down, (j,), x, tm_half, _pick(d, (512, 256, 128)))
        else:
            fe = moe_w_gate.shape[-1]
            tg = _pick(TOP_K * m, (MOE_TILE, 256, 128, 64, 32, 16))
            tn_d = _pick(d, (512, 256))
            rout, hp = _router(x, norm_ffn3, l, moe_router[j])
            pos, tok, te, n_used, p_rows = _dispatch_tables(rout, ne, tg)
            xg = _moe_gather(hp, tok, p_rows)
            ug = _gmm_swiglu(xg, moe_w_gate, moe_w_up, j, te, n_used, tg, _pick(fe, (256, 128)))
            ygp = _gmm_down(ug, moe_w_down, j, te, n_used, tg, tn_d)
            if l == depth - 1:
                y_p, y_s = _moe_combine(x, rout, pos, ygp, tn_d, norm_final3, mp)
            else:
                x = _moe_combine(x, rout, pos, ygp, tn_d)[0]
    if y_p is None:
        y_p = _rmsnorm(x, norm_final3, 0, F32, 0, mp)
        y_s = _rmsnorm(x, norm_final3, 0, F32, mp, ms)
    return (y_p.reshape(bp, tp, d), y_s.reshape(bs, ts, d),
            ret_p, jnp.stack(lrus_p), jnp.stack(convs_p),
            ret_s, jnp.stack(lrus_s), jnp.stack(convs_s))
```

```python
import functools

import jax
import jax.numpy as jnp
from jax import lax
from jax.experimental import pallas as pl
from jax.experimental.pallas import tpu as pltpu

F32 = jnp.float32
BF16 = jnp.bfloat16

ROPE_BASE = 10000.0
LRU_C = 8.0
EPS = 1e-6
RET_CHUNK = 128
PAST_LEN = 16384
TOP_K = 2
SUBLANES = 8
LANES = 128
VMEM_LIMIT = 56 * 1024 * 1024
MOE_TILE = 1024
GATHER_ROWS = 256
COMBINE_ROWS = 128


def _pick(n, prefs):
    for p in prefs:
        if n % p == 0:
            return p
    return n


def _params(*sem):
    return pltpu.CompilerParams(dimension_semantics=sem, vmem_limit_bytes=VMEM_LIMIT)


def _pcall(body, *, name, grid, in_specs, args, out_specs, out_shape, sem, scratch=(), prev=(),
           num_scalar_prefetch=0):
    n_in = len(args)
    prev = tuple(prev) + (None,) * (len(out_shape) - len(prev))
    extra = [(oi, p) for oi, p in enumerate(prev) if p is not None]
    aliases = {n_in + e: oi for e, (oi, _) in enumerate(extra)}
    if extra:
        inner = body

        def body(*refs):
            return inner(*refs[:n_in], *refs[n_in + len(extra):])

    in_specs = list(in_specs) + [pl.BlockSpec(memory_space=pl.ANY)] * len(extra)
    if num_scalar_prefetch:
        grid_spec = pltpu.PrefetchScalarGridSpec(
            num_scalar_prefetch=num_scalar_prefetch, grid=grid, in_specs=in_specs[num_scalar_prefetch:],
            out_specs=tuple(out_specs), scratch_shapes=list(scratch))
        call = pl.pallas_call(body, grid_spec=grid_spec, out_shape=tuple(out_shape),
                              input_output_aliases=aliases, compiler_params=_params(*sem), name=name)
    else:
        call = pl.pallas_call(body, grid=grid, in_specs=in_specs, out_specs=tuple(out_specs),
                              out_shape=tuple(out_shape), scratch_shapes=list(scratch),
                              input_output_aliases=aliases, compiler_params=_params(*sem), name=name)
    return call(*args, *[p for _, p in extra])


def _wspec(pre, k, tn):
    return pl.BlockSpec((None,) * len(pre) + (k, tn), lambda j, i: pre + (0, j))


def _vspec(l, n):
    return pl.BlockSpec((None, 1, n), lambda *_: (l, 0, 0))


def _rms(x, g):
    return (x * lax.rsqrt(jnp.mean(x * x, axis=-1, keepdims=True) + EPS)) * g


def _rmsnorm_kernel(x_ref, g_ref, o_ref):
    o_ref[...] = _rms(x_ref[...], g_ref[...]).astype(o_ref.dtype)


def _rmsnorm(x, g3, l, out_dtype, row0=0, rows=None):
    d = x.shape[1]
    rows = x.shape[0] if rows is None else rows
    tm = _pick(rows, (512, 256, 128, 64, 32, 16, 8))
    assert row0 % tm == 0
    rb0 = row0 // tm
    return _pcall(
        _rmsnorm_kernel, name="rmsnorm", grid=(rows // tm,),
        in_specs=[pl.BlockSpec((tm, d), lambda i: (rb0 + i, 0)), _vspec(l, d)],
        args=(x, g3),
        out_specs=[pl.BlockSpec((tm, d), lambda i: (i, 0))],
        out_shape=[jax.ShapeDtypeStruct((rows, d), out_dtype)],
        sem=("arbitrary",))[0]


def _cast_w(w_ref, wb_ref):
    k = w_ref.shape[0]
    ck = _pick(k, (256, 128, 64, 32, 16))

    def body(c, carry):
        r = pl.multiple_of(c * ck, ck)
        wb_ref[pl.ds(r, ck), :] = w_ref[pl.ds(r, ck), :].astype(BF16)
        return carry

    lax.fori_loop(0, k // ck, body, 0)


def _mm_inproj_kernel(x_ref, w_ref, cos_ref, sin_ref, o_ref, wb_ref, *, nrot, dk, silu_tiles, gelu_tiles):
    j = pl.program_id(0)

    @pl.when(pl.program_id(1) == 0)
    def _():
        _cast_w(w_ref, wb_ref)

    tm, tn = o_ref.shape
    nchunk = 4 if tm % (4 * 16) == 0 else 1

    def row_chunks(epilogue):
        rc = tm // nchunk
        for c in range(nchunk):
            rs = slice(c * rc, (c + 1) * rc)
            epilogue(jnp.dot(x_ref[rs, :], wb_ref[...], preferred_element_type=F32), rs)

    def store(fn):
        def epilogue(acc, rs):
            o_ref[rs, :] = fn(acc).astype(o_ref.dtype)
        return epilogue

    in_tiles = lambda t: jnp.logical_and(j >= t[0], j < t[1])
    is_silu, is_gelu = in_tiles(silu_tiles), in_tiles(gelu_tiles)

    @pl.when(jnp.logical_and(j >= nrot, jnp.logical_not(jnp.logical_or(is_silu, is_gelu))))
    def _():
        row_chunks(store(lambda acc: acc))

    @pl.when(is_silu)
    def _():
        row_chunks(store(jax.nn.silu))

    @pl.when(is_gelu)
    def _():
        row_chunks(store(jax.nn.gelu))

    @pl.when(j < nrot)
    def _():
        scale = jnp.where(j >= nrot // 2, F32(dk ** -0.5), F32(1.0))
        half = dk // 2

        def rotary(acc, rs):
            cos = cos_ref[rs, :] * scale
            sin = sin_ref[rs, :] * scale
            for h in range(tn // dk):
                x1 = acc[:, h * dk:h * dk + half]
                x2 = acc[:, h * dk + half:(h + 1) * dk]
                o_ref[rs, h * dk:h * dk + half] = (x1 * cos - x2 * sin).astype(o_ref.dtype)
                o_ref[rs, h * dk + half:(h + 1) * dk] = (x1 * sin + x2 * cos).astype(o_ref.dtype)

        row_chunks(rotary)


def _mm_inproj(x, w, pre, cos, sin, qk_cols, dk, silu_cols, gelu_cols, tm, tn):
    m, k = x.shape
    n = w.shape[-1]
    half = cos.shape[1]
    assert qk_cols % (2 * tn) == 0 and tn % dk == 0
    assert all(c % tn == 0 for c in silu_cols + gelu_cols)
    tiles = lambda cols: (cols[0] // tn, cols[1] // tn)
    return _pcall(
        functools.partial(_mm_inproj_kernel, nrot=qk_cols // tn, dk=dk, silu_tiles=tiles(silu_cols),
                          gelu_tiles=tiles(gelu_cols)), name="mm_inproj",
        grid=(n // tn, m // tm),
        in_specs=[pl.BlockSpec((tm, k), lambda j, i: (i, 0)), _wspec(pre, k, tn),
                  pl.BlockSpec((tm, half), lambda j, i: (i, 0)),
                  pl.BlockSpec((tm, half), lambda j, i: (i, 0))],
        args=(x, w, cos, sin),
        out_specs=[pl.BlockSpec((tm, tn), lambda j, i: (i, j))],
        out_shape=[jax.ShapeDtypeStruct((m, n), BF16)],
        scratch=[pltpu.VMEM((k, tn), BF16)],
        sem=("arbitrary", "arbitrary"))[0]


def _mm_resid_kernel(x_ref, w_ref, r_ref, o_ref, wb_ref):
    @pl.when(pl.program_id(1) == 0)
    def _():
        _cast_w(w_ref, wb_ref)

    o_ref[...] = r_ref[...] + jnp.dot(x_ref[...], wb_ref[...], preferred_element_type=F32)


def _mm_resid(x, w, pre, r, tm, tn):
    m, k = x.shape
    n = w.shape[-1]
    return _pcall(
        _mm_resid_kernel, name="mm_resid", grid=(n // tn, m // tm),
        in_specs=[pl.BlockSpec((tm, k), lambda j, i: (i, 0)), _wspec(pre, k, tn),
                  pl.BlockSpec((tm, tn), lambda j, i: (i, j))],
        args=(x, w, r),
        out_specs=[pl.BlockSpec((tm, tn), lambda j, i: (i, j))],
        out_shape=[jax.ShapeDtypeStruct((m, n), F32)],
        scratch=[pltpu.VMEM((k, tn), BF16)],
        sem=("arbitrary", "arbitrary"))[0]


def _mm_swiglu_kernel(x_ref, wg_ref, wu_ref, o_ref, wgb_ref, wub_ref):
    @pl.when(pl.program_id(1) == 0)
    def _():
        _cast_w(wg_ref, wgb_ref)
        _cast_w(wu_ref, wub_ref)

    x = x_ref[...]
    g = jnp.dot(x, wgb_ref[...], preferred_element_type=F32)
    u = jnp.dot(x, wub_ref[...], preferred_element_type=F32)
    o_ref[...] = (jax.nn.silu(g) * u).astype(o_ref.dtype)


def _mm_swiglu(x, wg, wu, pre, tm, tn):
    m, k = x.shape
    n = wg.shape[-1]
    return _pcall(
        _mm_swiglu_kernel, name="mm_swiglu", grid=(n // tn, m // tm),
        in_specs=[pl.BlockSpec((tm, k), lambda j, i: (i, 0)), _wspec(pre, k, tn), _wspec(pre, k, tn)],
        args=(x, wg, wu),
        out_specs=[pl.BlockSpec((tm, tn), lambda j, i: (i, j))],
        out_shape=[jax.ShapeDtypeStruct((m, n), BF16)],
        scratch=[pltpu.VMEM((k, tn), BF16), pltpu.VMEM((k, tn), BF16)],
        sem=("arbitrary", "arbitrary"))[0]


def _mm_merge_kernel(a_ref, b_ref, wa_ref, wb_ref, ga_ref, gb_ref, o_ref, wab_ref, wbb_ref):
    @pl.when(pl.program_id(1) == 0)
    def _():
        _cast_w(wa_ref, wab_ref)
        _cast_w(wb_ref, wbb_ref)

    ya = jnp.dot(a_ref[...], wab_ref[...], preferred_element_type=F32)
    yb = jnp.dot(b_ref[...], wbb_ref[...], preferred_element_type=F32)
    ga = ga_ref[...].astype(F32)
    gb = gb_ref[...].astype(F32)
    o_ref[...] = (jax.nn.sigmoid(ga) * ya + jax.nn.sigmoid(gb) * yb).astype(o_ref.dtype)


def _mm_merge(a, b, wa, wb, pre, proj, ga_col, gb_col, tm, tn):
    m, k = a.shape
    n = wa.shape[-1]
    ga_blk, gb_blk = ga_col // tn, gb_col // tn
    return _pcall(
        _mm_merge_kernel, name="mm_merge", grid=(n // tn, m // tm),
        in_specs=[pl.BlockSpec((tm, k), lambda j, i: (i, 0)),
                  pl.BlockSpec((tm, k), lambda j, i: (i, 0)),
                  _wspec(pre, k, tn), _wspec(pre, k, tn),
                  pl.BlockSpec((tm, tn), lambda j, i: (i, ga_blk + j)),
                  pl.BlockSpec((tm, tn), lambda j, i: (i, gb_blk + j))],
        args=(a, b, wa, wb, proj, proj),
        out_specs=[pl.BlockSpec((tm, tn), lambda j, i: (i, j))],
        out_shape=[jax.ShapeDtypeStruct((m, n), BF16)],
        scratch=[pltpu.VMEM((k, tn), BF16), pltpu.VMEM((k, tn), BF16)],
        sem=("arbitrary", "arbitrary"))[0]


def _rope_kernel(inv_ref, cos_ref, sin_ref, *, pos0):
    t, half = cos_ref.shape
    pos = lax.broadcasted_iota(jnp.int32, (t, half), 0).astype(F32) + F32(pos0)
    ang = pos * inv_ref[...]
    cos_ref[...] = jnp.cos(ang)
    sin_ref[...] = jnp.sin(ang)


def _rope_tables(t, half, pos0):
    inv = ROPE_BASE ** (-jnp.arange(half, dtype=F32) / half)
    return pl.pallas_call(
        functools.partial(_rope_kernel, pos0=pos0),
        out_shape=(jax.ShapeDtypeStruct((t, half), F32), jax.ShapeDtypeStruct((t, half), F32)),
        name="rope_tables",
    )(inv.reshape(1, half))


def _decay_tables(c, h, dk, dv):
    log_g = jnp.log1p(-jnp.exp2(-5.0 - jnp.arange(h, dtype=F32)))
    idx = jnp.arange(c)
    rel = idx[:, None] - idx[None, :]
    dmask = jnp.where(rel[None] >= 0,
                      jnp.exp(jnp.maximum(rel, 0)[None].astype(F32) * log_g[:, None, None]), 0.0)
    xi = jnp.exp((idx + 1).astype(F32)[None, :] * log_g[:, None])
    zeta = jnp.exp((c - 1 - idx).astype(F32)[None, :] * log_g[:, None])
    g_c = jnp.exp(c * log_g)
    xi_t = jnp.broadcast_to(xi[:, :, None], (h, c, dv))
    zeta_t = jnp.broadcast_to(zeta[:, :, None], (h, c, dk))
    return dmask, xi_t, zeta_t, g_c


def _ret_head(qb, kb, v, g, dm, xi, zt, gn, s, gc):
    scores = lax.dot_general(qb, kb, (((1,), (1,)), ((), ())), preferred_element_type=F32)
    intra = jnp.dot((scores * dm).astype(BF16), v, preferred_element_type=F32)
    cross = jnp.dot(qb, s.astype(BF16), preferred_element_type=F32) * xi
    kz = (kb.astype(F32) * zt).astype(BF16)
    upd = lax.dot_general(kz, v, (((0,), (0,)), ((), ())), preferred_element_type=F32)
    s_new = gc * s + upd
    o = intra + cross
    mu = jnp.mean(o, axis=-1, keepdims=True)
    d = o - mu
    var = jnp.mean(d * d, axis=-1, keepdims=True)
    y = d * lax.rsqrt(var + EPS) * gn
    return (g * y).astype(BF16), s_new


def _ret_prompt_kernel(gc_ref, q_ref, k_ref, v_ref, g_ref, dm_ref, xi_ref, zt_ref,
                       gn_ref, o_ref, s_ref, *, nh, dk, dv):
    @pl.when(pl.program_id(1) == 0)
    def _():
        s_ref[...] = jnp.zeros_like(s_ref)

    for h in range(nh):
        out, s_new = _ret_head(
            q_ref[:, h * dk:(h + 1) * dk], k_ref[:, h * dk:(h + 1) * dk],
            v_ref[:, h * dv:(h + 1) * dv], g_ref[:, h * dv:(h + 1) * dv].astype(F32),
            dm_ref[h], xi_ref[h], zt_ref[h], gn_ref[:, h * dv:(h + 1) * dv],
            s_ref[h], gc_ref[h])
        o_ref[:, h * dv:(h + 1) * dv] = out
        s_ref[h] = s_new


def _ret_prompt(proj, gn3, l, depth, b, t, nh, dk, dv, s_prev):
    c = RET_CHUNK if t % RET_CHUNK == 0 else t
    nc = t // c
    w = nh * dk
    m = proj.shape[0]
    dmask, xi_t, zeta_t, g_c = _decay_tables(c, nh, dk, dv)
    row = lambda bi, ci: bi * nc + ci
    full3 = lambda bi, ci: (0, 0, 0)
    return _pcall(
        functools.partial(_ret_prompt_kernel, nh=nh, dk=dk, dv=dv), name="ret_prompt", grid=(b, nc),
        in_specs=[pl.BlockSpec(memory_space=pltpu.SMEM),
                  pl.BlockSpec((c, w), lambda bi, ci: (row(bi, ci), 0)),
                  pl.BlockSpec((c, w), lambda bi, ci: (row(bi, ci), 1)),
                  pl.BlockSpec((c, w), lambda bi, ci: (row(bi, ci), 2)),
                  pl.BlockSpec((c, w), lambda bi, ci: (row(bi, ci), 3)),
                  pl.BlockSpec((nh, c, c), full3),
                  pl.BlockSpec((nh, c, dv), full3),
                  pl.BlockSpec((nh, c, dk), full3),
                  _vspec(l, nh * dv)],
        args=(g_c, proj, proj, proj, proj, dmask, xi_t, zeta_t, gn3),
        out_specs=[pl.BlockSpec((c, nh * dv), lambda bi, ci: (row(bi, ci), 0)),
                   pl.BlockSpec((None, None, nh, dk, dv), lambda bi, ci: (l, bi, 0, 0, 0))],
        out_shape=[jax.ShapeDtypeStruct((m, nh * dv), BF16),
                   jax.ShapeDtypeStruct((depth, b, nh, dk, dv), F32)],
        prev=(None, s_prev),
        sem=("arbitrary", "arbitrary"))


def _ret_sample_kernel(gc_ref, q_ref, k_ref, v_ref, g_ref, dm_ref, xi_ref, zt_ref,
                       gn_ref, s0_ref, o_ref, s_ref, *, nh, dk, dv, bb, ts):
    q = q_ref[...].astype(F32)
    k = k_ref[...].astype(F32)
    v = v_ref[...].astype(F32)
    g = g_ref[...].astype(F32)
    seqs = []
    for i in range(bb):
        r0, r1 = i * ts, (i + 1) * ts
        outs = []
        for h in range(nh):
            out, s_new = _ret_head(
                q[r0:r1, h * dk:(h + 1) * dk].astype(BF16), k[r0:r1, h * dk:(h + 1) * dk].astype(BF16),
                v[r0:r1, h * dv:(h + 1) * dv].astype(BF16), g[r0:r1, h * dv:(h + 1) * dv],
                dm_ref[h], xi_ref[h], zt_ref[h], gn_ref[:, h * dv:(h + 1) * dv],
                s0_ref[i, h], gc_ref[h])
            s_ref[i, h] = s_new
            outs.append(out.astype(F32))
        seqs.append(jnp.concatenate(outs, axis=-1))
    o_ref[...] = jnp.concatenate(seqs, axis=0).astype(o_ref.dtype)


def _ret_sample(proj, gn3, s0_all, l, row0, bs, ts, nh, dk, dv, ya_prev, s_prev):
    bb = _pick(bs, (4, 2, 1))
    rows = bb * ts
    w = nh * dk
    depth = s0_all.shape[0]
    dmask, xi_t, zeta_t, g_c = _decay_tables(ts, nh, dk, dv)
    assert row0 % rows == 0
    rb0 = row0 // rows
    full3 = lambda i: (0, 0, 0)
    return _pcall(
        functools.partial(_ret_sample_kernel, nh=nh, dk=dk, dv=dv, bb=bb, ts=ts), name="ret_sample",
        grid=(bs // bb,),
        in_specs=[pl.BlockSpec(memory_space=pltpu.SMEM),
                  pl.BlockSpec((rows, w), lambda i: (rb0 + i, 0)),
                  pl.BlockSpec((rows, w), lambda i: (rb0 + i, 1)),
                  pl.BlockSpec((rows, w), lambda i: (rb0 + i, 2)),
                  pl.BlockSpec((rows, w), lambda i: (rb0 + i, 3)),
                  pl.BlockSpec((nh, ts, ts), full3),
                  pl.BlockSpec((nh, ts, dv), full3),
                  pl.BlockSpec((nh, ts, dk), full3),
                  _vspec(l, nh * dv),
                  pl.BlockSpec((None, bb, nh, dk, dv), lambda i: (l, i, 0, 0, 0))],
        args=(g_c, proj, proj, proj, proj, dmask, xi_t, zeta_t, gn3, s0_all),
        out_specs=[pl.BlockSpec((rows, nh * dv), lambda i: (rb0 + i, 0)),
                   pl.BlockSpec((None, bb, nh, dk, dv), lambda i: (l, i, 0, 0, 0))],
        out_shape=[jax.ShapeDtypeStruct(ya_prev.shape, BF16),
                   jax.ShapeDtypeStruct((depth, bs, nh, dk, dv), F32)],
        prev=(ya_prev, s_prev),
        sem=("arbitrary",))


def _lru_gates(uc, wr_ref, br_ref, wi_ref, bi_ref, lam_ref):
    nb, lb, _ = wr_ref.shape
    ucb = uc.astype(BF16)
    rl, il = [], []
    for n in range(nb):
        xb = ucb[:, n * lb:(n + 1) * lb]
        rl.append(jnp.dot(xb, wr_ref[n].astype(BF16), preferred_element_type=F32))
        il.append(jnp.dot(xb, wi_ref[n].astype(BF16), preferred_element_type=F32))
    r = jax.nn.sigmoid(jnp.concatenate(rl, axis=-1) + br_ref[...])
    i = jax.nn.sigmoid(jnp.concatenate(il, axis=-1) + bi_ref[...])
    log_a = -LRU_C * r * jax.nn.softplus(-lam_ref[...])
    a = jnp.exp(log_a)
    bx = jnp.sqrt(-jnp.tanh(log_a) * (a * a + 1.0)) * (i * uc)
    return a, bx


def _blocks(x):
    return [x[lo:lo + SUBLANES, :] for lo in range(0, x.shape[0], SUBLANES)]


def _conv_blocks(u_blocks, prev_blocks, cw_ref, cb_ref, rows):
    ncw = cw_ref.shape[0]
    rolled = {}

    def rot(x, s):
        key = (id(x), s)
        if key not in rolled:
            rolled[key] = pltpu.roll(x, s, axis=0)
        return rolled[key]

    out = []
    for ub, pb in zip(u_blocks, prev_blocks):
        uc = cb_ref[...]
        for j in range(ncw):
            s = ncw - 1 - j
            term = ub if s == 0 else jnp.where(rows >= s, rot(ub, s), rot(pb, s))
            uc = uc + term * cw_ref[j:j + 1, :]
        out.append(uc)
    return jnp.concatenate(out, axis=0)


def _scan_block(a, bx, rows):
    d = 1
    while d < SUBLANES:
        m = rows >= d
        bx = bx + a * jnp.where(m, pltpu.roll(bx, d, axis=0), 0.0)
        a = a * jnp.where(m, pltpu.roll(a, d, axis=0), 1.0)
        d *= 2
    return a, bx


def _lru_prompt_kernel(ux_ref, uy_ref, cw_ref, cb_ref, wr_ref, br_ref, wi_ref, bi_ref, lam_ref,
                       o_ref, hl_ref, ul_ref, hc_ref):
    r, cw = ux_ref.shape

    @pl.when(pl.program_id(2) == 0)
    def _():
        hc_ref[...] = jnp.zeros_like(hc_ref)
        ul_ref[...] = jnp.zeros_like(ul_ref)

    rows = lax.broadcasted_iota(jnp.int32, (SUBLANES, cw), 0)
    ub = _blocks(ux_ref[...].astype(F32))
    uc = _conv_blocks(ub, [ul_ref[...]] + ub[:-1], cw_ref, cb_ref, rows)
    ul_ref[...] = ub[-1]
    a, bx = _lru_gates(uc, wr_ref, br_ref, wi_ref, bi_ref, lam_ref)
    carry = hc_ref[...]
    hs = []
    for ab, bb in zip(_blocks(a), _blocks(bx)):
        ab, bb = _scan_block(ab, bb, rows)
        hb = ab * carry + bb
        hs.append(hb)
        carry = jnp.broadcast_to(hb[SUBLANES - 1:SUBLANES, :], (SUBLANES, cw))
    hc_ref[...] = carry
    hl_ref[...] = hs[-1]
    h = jnp.concatenate(hs, axis=0)
    o_ref[...] = (h * uy_ref[...].astype(F32)).astype(o_ref.dtype)


def _lru_sample_kernel(ux_ref, uy_ref, cw_ref, cb_ref, wr_ref, br_ref, wi_ref, bi_ref, lam_ref,
                       h0_ref, buf_ref, o_ref, hl_ref, us_ref):
    r, cw = ux_ref.shape
    rows = lax.broadcasted_iota(jnp.int32, (SUBLANES, cw), 0)
    u = ux_ref[...].astype(F32)
    us_ref[...] = u
    uc = _conv_blocks(_blocks(u), _blocks(buf_ref[...]), cw_ref, cb_ref, rows)
    a, bx = _lru_gates(uc, wr_ref, br_ref, wi_ref, bi_ref, lam_ref)
    hs = []
    for ab, bb, h0 in zip(_blocks(a), _blocks(bx), _blocks(h0_ref[...])):
        ab, bb = _scan_block(ab, bb, rows)
        hs.append(ab * h0 + bb)
    h = jnp.concatenate(hs, axis=0)
    hl_ref[...] = h
    o_ref[...] = (h * uy_ref[...].astype(F32)).astype(o_ref.dtype)


def _lru_specs(l, r, cw, ncw, lb, ux_blk, uy_blk, row_of, cb_of):
    nbc = cw // lb
    vec = pl.BlockSpec((None, 1, cw), lambda *g: (l, 0, cb_of(*g)))
    gate = pl.BlockSpec((None, nbc, lb, lb), lambda *g: (l, cb_of(*g), 0, 0))
    return [pl.BlockSpec((r, cw), lambda *g: (row_of(*g), ux_blk + cb_of(*g))),
            pl.BlockSpec((r, cw), lambda *g: (row_of(*g), uy_blk + cb_of(*g))),
            pl.BlockSpec((None, ncw, cw), lambda *g: (l, 0, cb_of(*g))),
            vec, gate, vec, gate, vec, vec]


def _lru_prompt(proj, ux_col, l, lru_w, b, t):
    conv_w, conv_b3, w_r, b_r3, w_i, b_i3, lam3 = lru_w
    width = lam3.shape[-1]
    lb = w_r.shape[-1]
    ncw = conv_w.shape[1]
    m = proj.shape[0]
    cw = _pick(width, (512, 256, 128))
    r = _pick(t, (256, 128, 64, 32, 16, 8))
    nt = t // r
    specs = _lru_specs(l, r, cw, ncw, lb, ux_col // cw, (ux_col + width) // cw,
                       lambda bi, cb, ti: bi * nt + ti, lambda bi, cb, ti: cb)
    last8 = pl.BlockSpec((None, SUBLANES, cw), lambda bi, cb, ti: (bi, 0, cb))
    return _pcall(
        _lru_prompt_kernel, name="lru_prompt", grid=(b, width // cw, nt),
        in_specs=specs,
        args=(proj, proj, conv_w, conv_b3, w_r, b_r3, w_i, b_i3, lam3),
        out_specs=[pl.BlockSpec((r, cw), lambda bi, cb, ti: (bi * nt + ti, cb)), last8, last8],
        out_shape=[jax.ShapeDtypeStruct((m, width), BF16),
                   jax.ShapeDtypeStruct((b, SUBLANES, width), F32),
                   jax.ShapeDtypeStruct((b, SUBLANES, width), F32)],
        scratch=[pltpu.VMEM((SUBLANES, cw), F32)],
        sem=("arbitrary", "arbitrary", "arbitrary"))


def _lru_sample(proj, ux_col, l, lru_w, row0, h0_rep, buf_fr, yb_prev):
    conv_w, conv_b3, w_r, b_r3, w_i, b_i3, lam3 = lru_w
    width = lam3.shape[-1]
    lb = w_r.shape[-1]
    ncw = conv_w.shape[1]
    ms = h0_rep.shape[1]
    cw = _pick(width, (512, 256, 128))
    r = _pick(ms, (256, 128, 64, 32, 16, 8))
    assert row0 % r == 0
    rb0 = row0 // r
    specs = _lru_specs(l, r, cw, ncw, lb, ux_col // cw, (ux_col + width) // cw,
                       lambda ri, cb: rb0 + ri, lambda ri, cb: cb)
    st = pl.BlockSpec((None, r, cw), lambda ri, cb: (l, ri, cb))
    f32rows = pl.BlockSpec((r, cw), lambda ri, cb: (ri, cb))
    return _pcall(
        _lru_sample_kernel, name="lru_sample", grid=(ms // r, width // cw),
        in_specs=specs + [st, st],
        args=(proj, proj, conv_w, conv_b3, w_r, b_r3, w_i, b_i3, lam3, h0_rep, buf_fr),
        out_specs=[pl.BlockSpec((r, cw), lambda ri, cb: (rb0 + ri, cb)), f32rows, f32rows],
        out_shape=[jax.ShapeDtypeStruct(yb_prev.shape, BF16),
                   jax.ShapeDtypeStruct((ms, width), F32), jax.ShapeDtypeStruct((ms, width), F32)],
        prev=(yb_prev,),
        sem=("arbitrary", "arbitrary"))


def _pack_pair(hi, lo):
    hb = lax.bitcast_convert_type(hi.astype(jnp.bfloat16).astype(F32), jnp.uint32)
    lb = lax.bitcast_convert_type(lo.astype(jnp.bfloat16).astype(F32), jnp.uint32)
    return hb | (lb >> 16)


def _unpack_pair(p):
    hi = lax.bitcast_convert_type(p & jnp.uint32(0xFFFF0000), F32)
    lo = lax.bitcast_convert_type(p << 16, F32)
    return hi, lo


def _router_kernel(x_ref, g_ref, wr_ref, o_ref, hp_ref, *, ne):
    hf = _rms(x_ref[...], g_ref[...])
    half = hf.shape[1] // 2
    hp_ref[...] = _pack_pair(hf[:, :half], hf[:, half:])
    h = hf.astype(BF16)
    logits = jnp.dot(h, wr_ref[...].astype(BF16), preferred_element_type=F32)
    lane = lax.broadcasted_iota(jnp.int32, logits.shape, 1).astype(F32)
    neg = F32(-jnp.inf)
    big = F32(LANES)
    l1 = jnp.where(lane < ne, logits, neg)
    m1 = jnp.max(l1, axis=-1, keepdims=True)
    i1 = jnp.min(jnp.where(l1 == m1, lane, big), axis=-1, keepdims=True)
    l2 = jnp.where(lane == i1, neg, l1)
    m2 = jnp.max(l2, axis=-1, keepdims=True)
    i2 = jnp.min(jnp.where(l2 == m2, lane, big), axis=-1, keepdims=True)
    e2 = jnp.exp(m2 - m1)
    den = 1.0 + e2
    o_ref[...] = (jnp.where(lane == 0.0, i1, 0.0) + jnp.where(lane == 1.0, i2, 0.0)
                  + jnp.where(lane == 2.0, 1.0 / den, 0.0) + jnp.where(lane == 3.0, e2 / den, 0.0))


def _router(x, g3, l, wr):
    m, d = x.shape
    ne = wr.shape[-1]
    assert TOP_K == 2 and ne <= LANES
    wr_pad = jnp.pad(wr, ((0, 0), (0, LANES - ne)))
    tm = _pick(m, (512, 256, 128, 64, 32, 16, 8))
    return _pcall(
        functools.partial(_router_kernel, ne=ne), name="router", grid=(m // tm,),
        in_specs=[pl.BlockSpec((tm, d), lambda i: (i, 0)), _vspec(l, d),
                  pl.BlockSpec((d, LANES), lambda i: (0, 0))],
        args=(x, g3, wr_pad),
        out_specs=[pl.BlockSpec((tm, LANES), lambda i: (i, 0)),
                   pl.BlockSpec((tm, d // 2), lambda i: (i, 0))],
        out_shape=[jax.ShapeDtypeStruct((m, LANES), F32),
                   jax.ShapeDtypeStruct((m, d // 2), jnp.uint32)],
        sem=("arbitrary",))


def _dispatch_tables(rout, ne, tg):
    m = rout.shape[0]
    na = TOP_K * m
    eid = rout[:, :TOP_K].astype(jnp.int32).T.reshape(na)
    onehot = (eid[:, None] == jnp.arange(ne, dtype=jnp.int32)[None, :]).astype(jnp.int32)
    cnt = jnp.sum(onehot, axis=0)
    rank = jnp.sum((jnp.cumsum(onehot, axis=0) - onehot) * onehot, axis=1)
    padded = ((cnt + tg - 1) // tg) * tg
    gend = jnp.cumsum(padded)
    gstart = gend - padded
    pos = gstart[eid] + rank
    p_rows = ((na + tg - 1) // tg + ne) * tg
    n_tiles = p_rows // tg
    tok = jnp.zeros((p_rows,), jnp.int32).at[pos].set(jnp.arange(na, dtype=jnp.int32) % m)
    n_used = (gend[-1] // tg).astype(jnp.int32)
    tstart = jnp.arange(n_tiles, dtype=jnp.int32) * tg
    tstart = jnp.minimum(tstart, (n_used - 1) * tg)
    te = jnp.minimum(jnp.searchsorted(gend, tstart, side="right"), ne - 1).astype(jnp.int32)
    return pos, tok, te, n_used.reshape(1), p_rows


DMA_UNROLL = 8


def _row_gather(src_hbm, dst, idx_of, sem):
    rows = dst.shape[0]
    assert rows % DMA_UNROLL == 0

    def row_copy(r, t):
        return pltpu.make_async_copy(src_hbm.at[pl.ds(t, 1), :], dst.at[pl.ds(r, 1), :], sem)

    def issue(c, carry):
        for u in range(DMA_UNROLL):
            r = c * DMA_UNROLL + u
            row_copy(r, idx_of(r)).start()
        return carry

    def drain(c, carry):
        for u in range(DMA_UNROLL):
            row_copy(c * DMA_UNROLL + u, 0).wait()
        return carry

    start = lambda: lax.fori_loop(0, rows // DMA_UNROLL, issue, 0)
    wait = lambda: lax.fori_loop(0, rows // DMA_UNROLL, drain, 0)
    return start, wait


def _moe_gather_kernel(tok_ref, hp_ref, o_ref):
    tr, d = o_ref.shape
    half = d // 2

    group = 2 * SUBLANES

    def body(g, carry):
        r0 = pl.multiple_of(g * group, group)
        rows = [hp_ref[pl.ds(tok_ref[0, r0 + u], 1), :] for u in range(group)]
        hi, lo = _unpack_pair(jnp.concatenate(rows, axis=0))
        o_ref[pl.ds(r0, group), pl.ds(0, half)] = hi.astype(o_ref.dtype)
        o_ref[pl.ds(r0, group), pl.ds(half, half)] = lo.astype(o_ref.dtype)
        return carry

    lax.fori_loop(0, tr // group, body, 0)


def _moe_gather(hp, tok, p_rows):
    m, half = hp.shape
    tr = _pick(p_rows, (GATHER_ROWS, 128, 64, 32, 16))
    nt = p_rows // tr
    return _pcall(
        _moe_gather_kernel, name="moe_gather", grid=(nt,),
        in_specs=[pl.BlockSpec((None, 1, tr), lambda i: (i, 0, 0), memory_space=pltpu.SMEM),
                  pl.BlockSpec((m, half), lambda i: (0, 0), pipeline_mode=pl.Buffered(1))],
        args=(tok.reshape(nt, 1, tr), hp),
        out_specs=[pl.BlockSpec((tr, 2 * half), lambda i: (i, 0))],
        out_shape=[jax.ShapeDtypeStruct((p_rows, 2 * half), BF16)],
        sem=("arbitrary",))[0]


def _tile_changed(te_ref):
    ti = pl.program_id(1)
    return jnp.logical_or(ti == 0, te_ref[ti] != te_ref[jnp.maximum(ti - 1, 0)])


def _gmm_swiglu_kernel(te_ref, nu_ref, x_ref, wg_ref, wu_ref, o_ref, wgb_ref, wub_ref):
    @pl.when(_tile_changed(te_ref))
    def _():
        _cast_w(wg_ref, wgb_ref)
        _cast_w(wu_ref, wub_ref)

    used = pl.program_id(1) < nu_ref[0]

    @pl.when(used)
    def _():
        x = x_ref[...]
        g = jnp.dot(x, wgb_ref[...], preferred_element_type=F32)
        u = jnp.dot(x, wub_ref[...], preferred_element_type=F32)
        o_ref[...] = (jax.nn.silu(g) * u).astype(o_ref.dtype)

    @pl.when(jnp.logical_not(used))
    def _():
        o_ref[...] = jnp.zeros_like(o_ref)


def _gmm_down_kernel(te_ref, nu_ref, x_ref, w_ref, o_ref, wb_ref):
    @pl.when(_tile_changed(te_ref))
    def _():
        _cast_w(w_ref, wb_ref)

    used = pl.program_id(1) < nu_ref[0]

    @pl.when(used)
    def _():
        y = jnp.dot(x_ref[...], wb_ref[...], preferred_element_type=F32)
        hw = y.shape[1] // 2
        o_ref[...] = _pack_pair(y[:, :hw], y[:, hw:])

    @pl.when(jnp.logical_not(used))
    def _():
        o_ref[...] = jnp.zeros_like(o_ref)


def _gmm_specs(jm, tg, k, tn, j0=0):
    xrow = lambda j, ti, te, nu: (jnp.minimum(ti, nu[0] - 1), 0)
    wsp = pl.BlockSpec((None, None, k, tn), lambda j, ti, te, nu: (jm, te[ti], 0, j0 + j))
    return pl.BlockSpec((tg, k), xrow), wsp, pl.BlockSpec((tg, tn), lambda j, ti, te, nu: (ti, j0 + j))


def _gmm_swiglu(xg, wg, wu, jm, te, n_used, tg, tns):
    p, k = xg.shape
    n = wg.shape[-1]
    assert sum(tn * cnt for tn, cnt in tns) == n
    out, col = None, 0
    for tn, cnt in tns:
        assert col % tn == 0
        xs, ws, os_ = _gmm_specs(jm, tg, k, tn, col // tn)
        out = _pcall(
            _gmm_swiglu_kernel, name="gmm_swiglu", grid=(cnt, p // tg), num_scalar_prefetch=2,
            in_specs=[None, None, xs, ws, ws], args=(te, n_used, xg, wg, wu),
            out_specs=[os_], out_shape=[jax.ShapeDtypeStruct((p, n), BF16)],
            scratch=[pltpu.VMEM((k, tn), BF16), pltpu.VMEM((k, tn), BF16)],
            prev=(out,), sem=("arbitrary", "arbitrary"))[0]
        col += tn * cnt
    return out


def _col_tiling(n, wide, narrow):
    nw = n // wide
    rest = n - nw * wide
    assert rest % narrow == 0 and (nw * wide) % narrow == 0
    return tuple(t for t in ((wide, nw), (narrow, rest // narrow)) if t[1])


def _gmm_down(ug, wd, jm, te, n_used, tg, tn):
    p, k = ug.shape
    n = wd.shape[-1]
    xs, ws, _ = _gmm_specs(jm, tg, k, tn)
    return _pcall(
        _gmm_down_kernel, name="gmm_down", grid=(n // tn, p // tg), num_scalar_prefetch=2,
        in_specs=[None, None, xs, ws], args=(te, n_used, ug, wd),
        out_specs=[pl.BlockSpec((tg, tn // 2), lambda j, ti, te, nu: (ti, j))],
        out_shape=[jax.ShapeDtypeStruct((p, n // 2), jnp.uint32)],
        scratch=[pltpu.VMEM((k, tn), BF16)],
        sem=("arbitrary", "arbitrary"))[0]


def _moe_combine_kernel(pos_ref, nxt_ref, x_ref, rw_ref, g_ref, yg_hbm, *rest, tn, np_tiles):
    outs, (buf, sem) = rest[:-2], rest[-2:]
    tc = x_ref.shape[0]
    i = pl.program_id(0)
    slot = lax.rem(i, 2)

    def gathers(idx_ref, s):
        return [_row_gather(yg_hbm, buf.at[s, k], lambda r, k=k: idx_ref[0, k * tc + r], sem.at[s])
                for k in range(TOP_K)]

    @pl.when(i == 0)
    def _():
        for start, _ in gathers(pos_ref, 0):
            start()

    @pl.when(i + 1 < pl.num_programs(0))
    def _():
        for start, _ in gathers(nxt_ref, 1 - slot):
            start()

    for _, wait in gathers(pos_ref, slot):
        wait()
    rows = buf[slot]

    def expand(p):
        hw = tn // 2
        parts = []
        for jj in range(p.shape[1] // hw):
            parts.extend(_unpack_pair(p[:, jj * hw:(jj + 1) * hw]))
        return jnp.concatenate(parts, axis=-1)

    rw = rw_ref[...]
    y = rw[:, TOP_K:TOP_K + 1] * expand(rows[0])
    for k in range(1, TOP_K):
        y = y + rw[:, TOP_K + k:TOP_K + k + 1] * expand(rows[k])
    xn = x_ref[...] + y
    if np_tiles is None:
        outs[0][...] = xn
    else:
        yn = _rms(xn, g_ref[...])

        @pl.when(i < np_tiles)
        def _():
            outs[0][...] = yn

        @pl.when(i >= np_tiles)
        def _():
            outs[1][...] = yn


def _moe_combine(x, rout, pos, ygp, tn, g3=None, mp=None):
    m, d = x.shape
    tc = _pick(m, (COMBINE_ROWS, 64, 32, 16, 8))
    nt = m // tc
    pos_t = pos.reshape(TOP_K, nt, tc).transpose(1, 0, 2).reshape(nt, 1, TOP_K * tc)
    if g3 is None:
        g3 = jnp.ones((1, 1, d), F32)
        np_tiles = None
        out_specs = [pl.BlockSpec((tc, d), lambda i: (i, 0))]
        out_shape = [jax.ShapeDtypeStruct((m, d), F32)]
    else:
        assert mp % tc == 0
        np_tiles = mp // tc
        out_specs = [pl.BlockSpec((tc, d), lambda i: (jnp.minimum(i, np_tiles - 1), 0)),
                     pl.BlockSpec((tc, d), lambda i: (jnp.maximum(i - np_tiles, 0), 0))]
        out_shape = [jax.ShapeDtypeStruct((mp, d), F32), jax.ShapeDtypeStruct((m - mp, d), F32)]
    return _pcall(
        functools.partial(_moe_combine_kernel, tn=tn, np_tiles=np_tiles), name="moe_combine", grid=(nt,),
        in_specs=[pl.BlockSpec((None, 1, TOP_K * tc), lambda i: (i, 0, 0), memory_space=pltpu.SMEM),
                  pl.BlockSpec((None, 1, TOP_K * tc), lambda i: (jnp.minimum(i + 1, nt - 1), 0, 0),
                               memory_space=pltpu.SMEM),
                  pl.BlockSpec((tc, d), lambda i: (i, 0)),
                  pl.BlockSpec((tc, LANES), lambda i: (i, 0)),
                  _vspec(0, d),
                  pl.BlockSpec(memory_space=pl.ANY)],
        args=(pos_t, pos_t, x, rout, g3, ygp),
        out_specs=out_specs, out_shape=out_shape,
        scratch=[pltpu.VMEM((2, TOP_K, tc, d // 2), jnp.uint32), pltpu.SemaphoreType.DMA((2,))],
        sem=("arbitrary",))


def kernel(x_prompt, x_sample, state_ret, state_lru, state_conv, norm_mix, w_in, ret_gn, w_ret_o, conv_w, conv_b, w_rgate, b_rgate, w_igate, b_igate, lru_lambda, w_lru_o, w_out, norm_ffn, ffn_w_gate, ffn_w_up, ffn_w_down, moe_router, moe_w_gate, moe_w_up, moe_w_down, norm_final):
    bp, tp, d = x_prompt.shape
    bs, ts, _ = x_sample.shape
    depth, _, nh, dk, dv = state_ret.shape
    width = state_lru.shape[-1]
    ncw = conv_w.shape[1]
    ne = moe_router.shape[-1]
    assert dk == dv and nh * dk == width == d and ts == SUBLANES and ncw - 1 <= SUBLANES
    mp, ms = bp * tp, bs * ts
    m = mp + ms
    g_col = 2 * nh * dk + nh * dv
    ux_col = g_col + nh * dv
    ga_col = ux_col + 2 * width

    tm = _pick(m, (1024, 512, 256, 128, 64, 32, 16, 8))
    tm_half = _pick(m, (512, 256, 128, 64, 32, 16, 8))
    tn_of = lambda n: _pick(n, (1024, 512, 256, 128))

    vec3 = lambda a: a.reshape(a.shape[0], 1, a.shape[-1])
    norm_mix3, norm_ffn3, ret_gn3 = vec3(norm_mix), vec3(norm_ffn), vec3(ret_gn)
    norm_final3 = norm_final.reshape(1, 1, d)
    lru_w = (conv_w, vec3(conv_b), w_rgate, vec3(b_rgate), w_igate, vec3(b_igate), vec3(lru_lambda))
    h0_rep = jnp.repeat(state_lru, ts, axis=1)
    buf_fr = jnp.pad(state_conv, ((0, 0), (0, 0), (ts - (ncw - 1), 0), (0, 0))).reshape(depth, ms, width)

    cos_p, sin_p = _rope_tables(tp, dk // 2, 0)
    cos_s, sin_s = _rope_tables(ts, dk // 2, PAST_LEN)
    cos = jnp.concatenate([jnp.tile(cos_p, (bp, 1)), jnp.tile(cos_s, (bs, 1))], axis=0)
    sin = jnp.concatenate([jnp.tile(sin_p, (bp, 1)), jnp.tile(sin_s, (bs, 1))], axis=0)
    tn_in = _pick(nh * dk, (1024, 512, 256))

    x = jnp.concatenate([x_prompt.reshape(mp, d), x_sample.reshape(ms, d)], axis=0)
    ret_p = ret_s = y_p = y_s = None
    lrus_p, lrus_s, convs_p, convs_s = [], [], [], []
    for l in range(depth):
        h = _rmsnorm(x, norm_mix3, l, BF16)
        proj = _mm_inproj(h, w_in, (l,), cos, sin, 2 * nh * dk, dk, (g_col, ux_col),
                          (ux_col + width, ga_col), tm, tn_in)

        ya, ret_p = _ret_prompt(proj, ret_gn3, l, depth, bp, tp, nh, dk, dv, ret_p)
        ya, ret_s = _ret_sample(proj, ret_gn3, state_ret, l, mp, bs, ts, nh, dk, dv, ya, ret_s)
        yb, hl_p, ul_p = _lru_prompt(proj, ux_col, l, lru_w, bp, tp)
        yb, hl_s, us_s = _lru_sample(proj, ux_col, l, lru_w, mp, h0_rep, buf_fr, yb)
        lrus_p.append(hl_p[:, -1])
        lrus_s.append(hl_s.reshape(bs, ts, width)[:, -1])
        convs_p.append(ul_p[:, SUBLANES - (ncw - 1):])
        convs_s.append(us_s.reshape(bs, ts, width)[:, ts - (ncw - 1):])

        z = _mm_merge(ya, yb, w_ret_o, w_lru_o, (l,), proj, ga_col, ga_col + d, tm, tn_of(d) // 2)
        x = _mm_resid(z, w_out, (l,), x, tm, tn_of(d))
        j = l // 2
        if l % 2 == 0:
            h2 = _rmsnorm(x, norm_ffn3, l, BF16)
            ff = ffn_w_gate.shape[-1]
            u = _mm_swiglu(h2, ffn_w_gate, ffn_w_up, (j,), tm, _pick(ff, (512, 256, 128)))
            x = _mm_resid(u, ffn_w_down, (j,), x, tm_half, _pick(d, (512, 256, 128)))
        else:
            fe = moe_w_gate.shape[-1]
            tg = _pick(TOP_K * m, (MOE_TILE, 256, 128, 64, 32, 16))
            tn_d = _pick(d, (512, 256))
            rout, hp = _router(x, norm_ffn3, l, moe_router[j])
            pos, tok, te, n_used, p_rows = _dispatch_tables(rout, ne, tg)
            xg = _moe_gather(hp, tok, p_rows)
            ug = _gmm_swiglu(xg, moe_w_gate, moe_w_up, j, te, n_used, tg,
                             _col_tiling(fe, 512, _pick(fe, (256, 128))))
            ygp = _gmm_down(ug, moe_w_down, j, te, n_used, tg, tn_d)
            if l == depth - 1:
                y_p, y_s = _moe_combine(x, rout, pos, ygp, tn_d, norm_final3, mp)
            else:
                x = _moe_combine(x, rout, pos, ygp, tn_d)[0]
    if y_p is None:
        y_p = _rmsnorm(x, norm_final3, 0, F32, 0, mp)
        y_s = _rmsnorm(x, norm_final3, 0, F32, mp, ms)
    return (y_p.reshape(bp, tp, d), y_s.reshape(bs, ts, d),
            ret_p, jnp.stack(lrus_p), jnp.stack(convs_p),
            ret_s, jnp.stack(lrus_s), jnp.stack(convs_s))
```

```python
import functools

import jax
import jax.numpy as jnp
from jax import lax
from jax.experimental import pallas as pl
from jax.experimental.pallas import tpu as pltpu

F32 = jnp.float32
BF16 = jnp.bfloat16

ROPE_BASE = 10000.0
LRU_C = 8.0
EPS = 1e-6
RET_CHUNK = 128
PAST_LEN = 16384
TOP_K = 2
SUBLANES = 8
LANES = 128
VMEM_LIMIT = 56 * 1024 * 1024
MOE_TILE = 1024
GATHER_ROWS = 256
COMBINE_ROWS = 128


def _pick(n, prefs):
    for p in prefs:
        if n % p == 0:
            return p
    return n


def _params(*sem):
    return pltpu.CompilerParams(dimension_semantics=sem, vmem_limit_bytes=VMEM_LIMIT)


def _pcall(body, *, name, grid, in_specs, args, out_specs, out_shape, sem, scratch=(), prev=(),
           num_scalar_prefetch=0):
    n_in = len(args)
    prev = tuple(prev) + (None,) * (len(out_shape) - len(prev))
    extra = [(oi, p) for oi, p in enumerate(prev) if p is not None]
    aliases = {n_in + e: oi for e, (oi, _) in enumerate(extra)}
    if extra:
        inner = body

        def body(*refs):
            return inner(*refs[:n_in], *refs[n_in + len(extra):])

    in_specs = list(in_specs) + [pl.BlockSpec(memory_space=pl.ANY)] * len(extra)
    if num_scalar_prefetch:
        grid_spec = pltpu.PrefetchScalarGridSpec(
            num_scalar_prefetch=num_scalar_prefetch, grid=grid, in_specs=in_specs[num_scalar_prefetch:],
            out_specs=tuple(out_specs), scratch_shapes=list(scratch))
        call = pl.pallas_call(body, grid_spec=grid_spec, out_shape=tuple(out_shape),
                              input_output_aliases=aliases, compiler_params=_params(*sem), name=name)
    else:
        call = pl.pallas_call(body, grid=grid, in_specs=in_specs, out_specs=tuple(out_specs),
                              out_shape=tuple(out_shape), scratch_shapes=list(scratch),
                              input_output_aliases=aliases, compiler_params=_params(*sem), name=name)
    return call(*args, *[p for _, p in extra])


def _wspec(pre, k, tn):
    return pl.BlockSpec((None,) * len(pre) + (k, tn), lambda j, i: pre + (0, j))


def _vspec(l, n):
    return pl.BlockSpec((None, 1, n), lambda *_: (l, 0, 0))


def _rms(x, g):
    return (x * lax.rsqrt(jnp.mean(x * x, axis=-1, keepdims=True) + EPS)) * g


def _rmsnorm_kernel(x_ref, g_ref, o_ref):
    o_ref[...] = _rms(x_ref[...], g_ref[...]).astype(o_ref.dtype)


def _rmsnorm(x, g3, l, out_dtype, row0=0, rows=None):
    d = x.shape[1]
    rows = x.shape[0] if rows is None else rows
    tm = _pick(rows, (512, 256, 128, 64, 32, 16, 8))
    assert row0 % tm == 0
    rb0 = row0 // tm
    return _pcall(
        _rmsnorm_kernel, name="rmsnorm", grid=(rows // tm,),
        in_specs=[pl.BlockSpec((tm, d), lambda i: (rb0 + i, 0)), _vspec(l, d)],
        args=(x, g3),
        out_specs=[pl.BlockSpec((tm, d), lambda i: (i, 0))],
        out_shape=[jax.ShapeDtypeStruct((rows, d), out_dtype)],
        sem=("arbitrary",))[0]


def _cast_w(w_ref, wb_ref):
    k = w_ref.shape[0]
    ck = _pick(k, (256, 128, 64, 32, 16))

    def body(c, carry):
        r = pl.multiple_of(c * ck, ck)
        wb_ref[pl.ds(r, ck), :] = w_ref[pl.ds(r, ck), :].astype(BF16)
        return carry

    lax.fori_loop(0, k // ck, body, 0)


def _mm_inproj_kernel(x_ref, w_ref, cos_ref, sin_ref, o_ref, wb_ref, *, nrot, dk, silu_tiles, gelu_tiles):
    j = pl.program_id(0)

    @pl.when(pl.program_id(1) == 0)
    def _():
        _cast_w(w_ref, wb_ref)

    tm, tn = o_ref.shape
    nchunk = 4 if tm % (4 * 16) == 0 else 1

    def row_chunks(epilogue):
        rc = tm // nchunk
        for c in range(nchunk):
            rs = slice(c * rc, (c + 1) * rc)
            epilogue(jnp.dot(x_ref[rs, :], wb_ref[...], preferred_element_type=F32), rs)

    def store(fn):
        def epilogue(acc, rs):
            o_ref[rs, :] = fn(acc).astype(o_ref.dtype)
        return epilogue

    in_tiles = lambda t: jnp.logical_and(j >= t[0], j < t[1])
    is_silu, is_gelu = in_tiles(silu_tiles), in_tiles(gelu_tiles)

    @pl.when(jnp.logical_and(j >= nrot, jnp.logical_not(jnp.logical_or(is_silu, is_gelu))))
    def _():
        row_chunks(store(lambda acc: acc))

    @pl.when(is_silu)
    def _():
        row_chunks(store(jax.nn.silu))

    @pl.when(is_gelu)
    def _():
        row_chunks(store(jax.nn.gelu))

    @pl.when(j < nrot)
    def _():
        scale = jnp.where(j >= nrot // 2, F32(dk ** -0.5), F32(1.0))
        half = dk // 2

        def rotary(acc, rs):
            cos = cos_ref[rs, :] * scale
            sin = sin_ref[rs, :] * scale
            for h in range(tn // dk):
                x1 = acc[:, h * dk:h * dk + half]
                x2 = acc[:, h * dk + half:(h + 1) * dk]
                o_ref[rs, h * dk:h * dk + half] = (x1 * cos - x2 * sin).astype(o_ref.dtype)
                o_ref[rs, h * dk + half:(h + 1) * dk] = (x1 * sin + x2 * cos).astype(o_ref.dtype)

        row_chunks(rotary)


def _mm_inproj(x, w, pre, cos, sin, qk_cols, dk, silu_cols, gelu_cols, tm, tn):
    m, k = x.shape
    n = w.shape[-1]
    half = cos.shape[1]
    assert qk_cols % (2 * tn) == 0 and tn % dk == 0
    assert all(c % tn == 0 for c in silu_cols + gelu_cols)
    tiles = lambda cols: (cols[0] // tn, cols[1] // tn)
    return _pcall(
        functools.partial(_mm_inproj_kernel, nrot=qk_cols // tn, dk=dk, silu_tiles=tiles(silu_cols),
                          gelu_tiles=tiles(gelu_cols)), name="mm_inproj",
        grid=(n // tn, m // tm),
        in_specs=[pl.BlockSpec((tm, k), lambda j, i: (i, 0)), _wspec(pre, k, tn),
                  pl.BlockSpec((tm, half), lambda j, i: (i, 0)),
                  pl.BlockSpec((tm, half), lambda j, i: (i, 0))],
        args=(x, w, cos, sin),
        out_specs=[pl.BlockSpec((tm, tn), lambda j, i: (i, j))],
        out_shape=[jax.ShapeDtypeStruct((m, n), BF16)],
        scratch=[pltpu.VMEM((k, tn), BF16)],
        sem=("arbitrary", "arbitrary"))[0]


def _mm_resid_kernel(x_ref, w_ref, r_ref, o_ref, wb_ref):
    @pl.when(pl.program_id(1) == 0)
    def _():
        _cast_w(w_ref, wb_ref)

    o_ref[...] = r_ref[...] + jnp.dot(x_ref[...], wb_ref[...], preferred_element_type=F32)


def _mm_resid(x, w, pre, r, tm, tn):
    m, k = x.shape
    n = w.shape[-1]
    return _pcall(
        _mm_resid_kernel, name="mm_resid", grid=(n // tn, m // tm),
        in_specs=[pl.BlockSpec((tm, k), lambda j, i: (i, 0)), _wspec(pre, k, tn),
                  pl.BlockSpec((tm, tn), lambda j, i: (i, j))],
        args=(x, w, r),
        out_specs=[pl.BlockSpec((tm, tn), lambda j, i: (i, j))],
        out_shape=[jax.ShapeDtypeStruct((m, n), F32)],
        scratch=[pltpu.VMEM((k, tn), BF16)],
        sem=("arbitrary", "arbitrary"))[0]


def _row_chunks(rows, fn, nchunk=4):
    n = nchunk if rows % (nchunk * 16) == 0 else 1
    rc = rows // n
    for c in range(n):
        fn(slice(c * rc, (c + 1) * rc))


def _swiglu_rows(x_ref, wgb_ref, wub_ref, o_ref):
    def chunk(rs):
        x = x_ref[rs, :]
        g = jnp.dot(x, wgb_ref[...], preferred_element_type=F32)
        u = jnp.dot(x, wub_ref[...], preferred_element_type=F32)
        o_ref[rs, :] = (jax.nn.silu(g) * u).astype(o_ref.dtype)

    _row_chunks(o_ref.shape[0], chunk)


def _mm_swiglu_kernel(x_ref, wg_ref, wu_ref, o_ref, wgb_ref, wub_ref):
    @pl.when(pl.program_id(1) == 0)
    def _():
        _cast_w(wg_ref, wgb_ref)
        _cast_w(wu_ref, wub_ref)

    _swiglu_rows(x_ref, wgb_ref, wub_ref, o_ref)


def _mm_swiglu(x, wg, wu, pre, tm, tn):
    m, k = x.shape
    n = wg.shape[-1]
    return _pcall(
        _mm_swiglu_kernel, name="mm_swiglu", grid=(n // tn, m // tm),
        in_specs=[pl.BlockSpec((tm, k), lambda j, i: (i, 0)), _wspec(pre, k, tn), _wspec(pre, k, tn)],
        args=(x, wg, wu),
        out_specs=[pl.BlockSpec((tm, tn), lambda j, i: (i, j))],
        out_shape=[jax.ShapeDtypeStruct((m, n), BF16)],
        scratch=[pltpu.VMEM((k, tn), BF16), pltpu.VMEM((k, tn), BF16)],
        sem=("arbitrary", "arbitrary"))[0]


def _mm_merge_kernel(a_ref, b_ref, wa_ref, wb_ref, ga_ref, gb_ref, o_ref, wab_ref, wbb_ref):
    @pl.when(pl.program_id(1) == 0)
    def _():
        _cast_w(wa_ref, wab_ref)
        _cast_w(wb_ref, wbb_ref)

    def chunk(rs):
        ya = jnp.dot(a_ref[rs, :], wab_ref[...], preferred_element_type=F32)
        yb = jnp.dot(b_ref[rs, :], wbb_ref[...], preferred_element_type=F32)
        ga = ga_ref[rs, :].astype(F32)
        gb = gb_ref[rs, :].astype(F32)
        o_ref[rs, :] = (jax.nn.sigmoid(ga) * ya + jax.nn.sigmoid(gb) * yb).astype(o_ref.dtype)

    _row_chunks(o_ref.shape[0], chunk)


def _mm_merge(a, b, wa, wb, pre, proj, ga_col, gb_col, tm, tn):
    m, k = a.shape
    n = wa.shape[-1]
    ga_blk, gb_blk = ga_col // tn, gb_col // tn
    return _pcall(
        _mm_merge_kernel, name="mm_merge", grid=(n // tn, m // tm),
        in_specs=[pl.BlockSpec((tm, k), lambda j, i: (i, 0)),
                  pl.BlockSpec((tm, k), lambda j, i: (i, 0)),
                  _wspec(pre, k, tn), _wspec(pre, k, tn),
                  pl.BlockSpec((tm, tn), lambda j, i: (i, ga_blk + j)),
                  pl.BlockSpec((tm, tn), lambda j, i: (i, gb_blk + j))],
        args=(a, b, wa, wb, proj, proj),
        out_specs=[pl.BlockSpec((tm, tn), lambda j, i: (i, j))],
        out_shape=[jax.ShapeDtypeStruct((m, n), BF16)],
        scratch=[pltpu.VMEM((k, tn), BF16), pltpu.VMEM((k, tn), BF16)],
        sem=("arbitrary", "arbitrary"))[0]


def _rope_kernel(inv_ref, cos_ref, sin_ref, *, pos0):
    t, half = cos_ref.shape
    pos = lax.broadcasted_iota(jnp.int32, (t, half), 0).astype(F32) + F32(pos0)
    ang = pos * inv_ref[...]
    cos_ref[...] = jnp.cos(ang)
    sin_ref[...] = jnp.sin(ang)


def _rope_tables(t, half, pos0):
    inv = ROPE_BASE ** (-jnp.arange(half, dtype=F32) / half)
    return pl.pallas_call(
        functools.partial(_rope_kernel, pos0=pos0),
        out_shape=(jax.ShapeDtypeStruct((t, half), F32), jax.ShapeDtypeStruct((t, half), F32)),
        name="rope_tables",
    )(inv.reshape(1, half))


def _decay_tables(c, h, dk, dv):
    log_g = jnp.log1p(-jnp.exp2(-5.0 - jnp.arange(h, dtype=F32)))
    idx = jnp.arange(c)
    rel = idx[:, None] - idx[None, :]
    dmask = jnp.where(rel[None] >= 0,
                      jnp.exp(jnp.maximum(rel, 0)[None].astype(F32) * log_g[:, None, None]), 0.0)
    xi = jnp.exp((idx + 1).astype(F32)[None, :] * log_g[:, None])
    zeta = jnp.exp((c - 1 - idx).astype(F32)[None, :] * log_g[:, None])
    g_c = jnp.exp(c * log_g)
    xi_t = jnp.broadcast_to(xi[:, :, None], (h, c, dv))
    zeta_t = jnp.broadcast_to(zeta[:, :, None], (h, c, dk))
    return dmask, xi_t, zeta_t, g_c


def _ret_head(qb, kb, v, g, dm, xi, zt, gn, s, gc):
    scores = lax.dot_general(qb, kb, (((1,), (1,)), ((), ())), preferred_element_type=F32)
    intra = jnp.dot((scores * dm).astype(BF16), v, preferred_element_type=F32)
    cross = jnp.dot(qb, s.astype(BF16), preferred_element_type=F32) * xi
    kz = (kb.astype(F32) * zt).astype(BF16)
    upd = lax.dot_general(kz, v, (((0,), (0,)), ((), ())), preferred_element_type=F32)
    s_new = gc * s + upd
    o = intra + cross
    mu = jnp.mean(o, axis=-1, keepdims=True)
    d = o - mu
    var = jnp.mean(d * d, axis=-1, keepdims=True)
    y = d * lax.rsqrt(var + EPS) * gn
    return (g * y).astype(BF16), s_new


def _ret_prompt_kernel(gc_ref, q_ref, k_ref, v_ref, g_ref, dm_ref, xi_ref, zt_ref,
                       gn_ref, o_ref, s_ref, *, nh, dk, dv):
    @pl.when(pl.program_id(1) == 0)
    def _():
        s_ref[...] = jnp.zeros_like(s_ref)

    for h in range(nh):
        out, s_new = _ret_head(
            q_ref[:, h * dk:(h + 1) * dk], k_ref[:, h * dk:(h + 1) * dk],
            v_ref[:, h * dv:(h + 1) * dv], g_ref[:, h * dv:(h + 1) * dv].astype(F32),
            dm_ref[h], xi_ref[h], zt_ref[h], gn_ref[:, h * dv:(h + 1) * dv],
            s_ref[h], gc_ref[h])
        o_ref[:, h * dv:(h + 1) * dv] = out
        s_ref[h] = s_new


def _ret_prompt(proj, gn3, l, depth, b, t, nh, dk, dv, s_prev):
    c = RET_CHUNK if t % RET_CHUNK == 0 else t
    nc = t // c
    w = nh * dk
    m = proj.shape[0]
    dmask, xi_t, zeta_t, g_c = _decay_tables(c, nh, dk, dv)
    row = lambda bi, ci: bi * nc + ci
    full3 = lambda bi, ci: (0, 0, 0)
    return _pcall(
        functools.partial(_ret_prompt_kernel, nh=nh, dk=dk, dv=dv), name="ret_prompt", grid=(b, nc),
        in_specs=[pl.BlockSpec(memory_space=pltpu.SMEM),
                  pl.BlockSpec((c, w), lambda bi, ci: (row(bi, ci), 0)),
                  pl.BlockSpec((c, w), lambda bi, ci: (row(bi, ci), 1)),
                  pl.BlockSpec((c, w), lambda bi, ci: (row(bi, ci), 2)),
                  pl.BlockSpec((c, w), lambda bi, ci: (row(bi, ci), 3)),
                  pl.BlockSpec((nh, c, c), full3),
                  pl.BlockSpec((nh, c, dv), full3),
                  pl.BlockSpec((nh, c, dk), full3),
                  _vspec(l, nh * dv)],
        args=(g_c, proj, proj, proj, proj, dmask, xi_t, zeta_t, gn3),
        out_specs=[pl.BlockSpec((c, nh * dv), lambda bi, ci: (row(bi, ci), 0)),
                   pl.BlockSpec((None, None, nh, dk, dv), lambda bi, ci: (l, bi, 0, 0, 0))],
        out_shape=[jax.ShapeDtypeStruct((m, nh * dv), BF16),
                   jax.ShapeDtypeStruct((depth, b, nh, dk, dv), F32)],
        prev=(None, s_prev),
        sem=("arbitrary", "arbitrary"))


def _ret_sample_kernel(gc_ref, q_ref, k_ref, v_ref, g_ref, dm_ref, xi_ref, zt_ref,
                       gn_ref, s0_ref, o_ref, s_ref, *, nh, dk, dv, bb, ts):
    q = q_ref[...].astype(F32)
    k = k_ref[...].astype(F32)
    v = v_ref[...].astype(F32)
    g = g_ref[...].astype(F32)
    seqs = []
    for i in range(bb):
        r0, r1 = i * ts, (i + 1) * ts
        outs = []
        for h in range(nh):
            out, s_new = _ret_head(
                q[r0:r1, h * dk:(h + 1) * dk].astype(BF16), k[r0:r1, h * dk:(h + 1) * dk].astype(BF16),
                v[r0:r1, h * dv:(h + 1) * dv].astype(BF16), g[r0:r1, h * dv:(h + 1) * dv],
                dm_ref[h], xi_ref[h], zt_ref[h], gn_ref[:, h * dv:(h + 1) * dv],
                s0_ref[i, h], gc_ref[h])
            s_ref[i, h] = s_new
            outs.append(out.astype(F32))
        seqs.append(jnp.concatenate(outs, axis=-1))
    o_ref[...] = jnp.concatenate(seqs, axis=0).astype(o_ref.dtype)


def _ret_sample(proj, gn3, s0_all, l, row0, bs, ts, nh, dk, dv, ya_prev, s_prev):
    bb = _pick(bs, (4, 2, 1))
    rows = bb * ts
    w = nh * dk
    depth = s0_all.shape[0]
    dmask, xi_t, zeta_t, g_c = _decay_tables(ts, nh, dk, dv)
    assert row0 % rows == 0
    rb0 = row0 // rows
    full3 = lambda i: (0, 0, 0)
    return _pcall(
        functools.partial(_ret_sample_kernel, nh=nh, dk=dk, dv=dv, bb=bb, ts=ts), name="ret_sample",
        grid=(bs // bb,),
        in_specs=[pl.BlockSpec(memory_space=pltpu.SMEM),
                  pl.BlockSpec((rows, w), lambda i: (rb0 + i, 0)),
                  pl.BlockSpec((rows, w), lambda i: (rb0 + i, 1)),
                  pl.BlockSpec((rows, w), lambda i: (rb0 + i, 2)),
                  pl.BlockSpec((rows, w), lambda i: (rb0 + i, 3)),
                  pl.BlockSpec((nh, ts, ts), full3),
                  pl.BlockSpec((nh, ts, dv), full3),
                  pl.BlockSpec((nh, ts, dk), full3),
                  _vspec(l, nh * dv),
                  pl.BlockSpec((None, bb, nh, dk, dv), lambda i: (l, i, 0, 0, 0))],
        args=(g_c, proj, proj, proj, proj, dmask, xi_t, zeta_t, gn3, s0_all),
        out_specs=[pl.BlockSpec((rows, nh * dv), lambda i: (rb0 + i, 0)),
                   pl.BlockSpec((None, bb, nh, dk, dv), lambda i: (l, i, 0, 0, 0))],
        out_shape=[jax.ShapeDtypeStruct(ya_prev.shape, BF16),
                   jax.ShapeDtypeStruct((depth, bs, nh, dk, dv), F32)],
        prev=(ya_prev, s_prev),
        sem=("arbitrary",))


def _lru_gates(uc, wr_ref, br_ref, wi_ref, bi_ref, lam_ref):
    nb, lb, _ = wr_ref.shape
    ucb = uc.astype(BF16)
    rl, il = [], []
    for n in range(nb):
        xb = ucb[:, n * lb:(n + 1) * lb]
        rl.append(jnp.dot(xb, wr_ref[n].astype(BF16), preferred_element_type=F32))
        il.append(jnp.dot(xb, wi_ref[n].astype(BF16), preferred_element_type=F32))
    r = jax.nn.sigmoid(jnp.concatenate(rl, axis=-1) + br_ref[...])
    i = jax.nn.sigmoid(jnp.concatenate(il, axis=-1) + bi_ref[...])
    log_a = -LRU_C * r * jax.nn.softplus(-lam_ref[...])
    a = jnp.exp(log_a)
    z = -jnp.tanh(log_a) * (a * a + 1.0)
    bx = jnp.where(z == 0.0, 0.0, z * lax.rsqrt(z)) * (i * uc)
    return a, bx


def _blocks(x):
    return [x[lo:lo + SUBLANES, :] for lo in range(0, x.shape[0], SUBLANES)]


def _conv_blocks(u_blocks, prev_blocks, cw_ref, cb_ref, rows):
    ncw = cw_ref.shape[0]
    rolled = {}

    def rot(x, s):
        key = (id(x), s)
        if key not in rolled:
            rolled[key] = pltpu.roll(x, s, axis=0)
        return rolled[key]

    out = []
    for ub, pb in zip(u_blocks, prev_blocks):
        uc = cb_ref[...]
        for j in range(ncw):
            s = ncw - 1 - j
            term = ub if s == 0 else jnp.where(rows >= s, rot(ub, s), rot(pb, s))
            uc = uc + term * cw_ref[j:j + 1, :]
        out.append(uc)
    return jnp.concatenate(out, axis=0)


def _scan_block(a, bx, rows):
    d = 1
    while d < SUBLANES:
        m = rows >= d
        bx = bx + a * jnp.where(m, pltpu.roll(bx, d, axis=0), 0.0)
        a = a * jnp.where(m, pltpu.roll(a, d, axis=0), 1.0)
        d *= 2
    return a, bx


def _lru_prompt_kernel(ux_ref, uy_ref, cw_ref, cb_ref, wr_ref, br_ref, wi_ref, bi_ref, lam_ref,
                       o_ref, hl_ref, ul_ref, hc_ref):
    r, cw = ux_ref.shape

    @pl.when(pl.program_id(2) == 0)
    def _():
        hc_ref[...] = jnp.zeros_like(hc_ref)
        ul_ref[...] = jnp.zeros_like(ul_ref)

    rows = lax.broadcasted_iota(jnp.int32, (SUBLANES, cw), 0)
    ub = _blocks(ux_ref[...].astype(F32))
    uc = _conv_blocks(ub, [ul_ref[...]] + ub[:-1], cw_ref, cb_ref, rows)
    ul_ref[...] = ub[-1]
    a, bx = _lru_gates(uc, wr_ref, br_ref, wi_ref, bi_ref, lam_ref)
    carry = hc_ref[...]
    hs = []
    for ab, bb in zip(_blocks(a), _blocks(bx)):
        ab, bb = _scan_block(ab, bb, rows)
        hb = ab * carry + bb
        hs.append(hb)
        carry = jnp.broadcast_to(hb[SUBLANES - 1:SUBLANES, :], (SUBLANES, cw))
    hc_ref[...] = carry
    hl_ref[...] = hs[-1]
    h = jnp.concatenate(hs, axis=0)
    o_ref[...] = (h * uy_ref[...].astype(F32)).astype(o_ref.dtype)


def _lru_sample_kernel(ux_ref, uy_ref, cw_ref, cb_ref, wr_ref, br_ref, wi_ref, bi_ref, lam_ref,
                       h0_ref, buf_ref, o_ref, hl_ref, us_ref):
    r, cw = ux_ref.shape
    rows = lax.broadcasted_iota(jnp.int32, (SUBLANES, cw), 0)
    u = ux_ref[...].astype(F32)
    us_ref[...] = u
    uc = _conv_blocks(_blocks(u), _blocks(buf_ref[...]), cw_ref, cb_ref, rows)
    a, bx = _lru_gates(uc, wr_ref, br_ref, wi_ref, bi_ref, lam_ref)
    hs = []
    for ab, bb, h0 in zip(_blocks(a), _blocks(bx), _blocks(h0_ref[...])):
        ab, bb = _scan_block(ab, bb, rows)
        hs.append(ab * h0 + bb)
    h = jnp.concatenate(hs, axis=0)
    hl_ref[...] = h
    o_ref[...] = (h * uy_ref[...].astype(F32)).astype(o_ref.dtype)


def _lru_specs(l, r, cw, ncw, lb, ux_blk, uy_blk, row_of, cb_of):
    nbc = cw // lb
    vec = pl.BlockSpec((None, 1, cw), lambda *g: (l, 0, cb_of(*g)))
    gate = pl.BlockSpec((None, nbc, lb, lb), lambda *g: (l, cb_of(*g), 0, 0))
    return [pl.BlockSpec((r, cw), lambda *g: (row_of(*g), ux_blk + cb_of(*g))),
            pl.BlockSpec((r, cw), lambda *g: (row_of(*g), uy_blk + cb_of(*g))),
            pl.BlockSpec((None, ncw, cw), lambda *g: (l, 0, cb_of(*g))),
            vec, gate, vec, gate, vec, vec]


def _lru_prompt(proj, ux_col, l, lru_w, b, t):
    conv_w, conv_b3, w_r, b_r3, w_i, b_i3, lam3 = lru_w
    width = lam3.shape[-1]
    lb = w_r.shape[-1]
    ncw = conv_w.shape[1]
    m = proj.shape[0]
    cw = _pick(width, (512, 256, 128))
    r = _pick(t, (256, 128, 64, 32, 16, 8))
    nt = t // r
    specs = _lru_specs(l, r, cw, ncw, lb, ux_col // cw, (ux_col + width) // cw,
                       lambda bi, cb, ti: bi * nt + ti, lambda bi, cb, ti: cb)
    last8 = pl.BlockSpec((None, SUBLANES, cw), lambda bi, cb, ti: (bi, 0, cb))
    return _pcall(
        _lru_prompt_kernel, name="lru_prompt", grid=(b, width // cw, nt),
        in_specs=specs,
        args=(proj, proj, conv_w, conv_b3, w_r, b_r3, w_i, b_i3, lam3),
        out_specs=[pl.BlockSpec((r, cw), lambda bi, cb, ti: (bi * nt + ti, cb)), last8, last8],
        out_shape=[jax.ShapeDtypeStruct((m, width), BF16),
                   jax.ShapeDtypeStruct((b, SUBLANES, width), F32),
                   jax.ShapeDtypeStruct((b, SUBLANES, width), F32)],
        scratch=[pltpu.VMEM((SUBLANES, cw), F32)],
        sem=("arbitrary", "arbitrary", "arbitrary"))


def _lru_sample(proj, ux_col, l, lru_w, row0, h0_rep, buf_fr, yb_prev):
    conv_w, conv_b3, w_r, b_r3, w_i, b_i3, lam3 = lru_w
    width = lam3.shape[-1]
    lb = w_r.shape[-1]
    ncw = conv_w.shape[1]
    ms = h0_rep.shape[1]
    cw = _pick(width, (512, 256, 128))
    r = _pick(ms, (256, 128, 64, 32, 16, 8))
    assert row0 % r == 0
    rb0 = row0 // r
    specs = _lru_specs(l, r, cw, ncw, lb, ux_col // cw, (ux_col + width) // cw,
                       lambda ri, cb: rb0 + ri, lambda ri, cb: cb)
    st = pl.BlockSpec((None, r, cw), lambda ri, cb: (l, ri, cb))
    f32rows = pl.BlockSpec((r, cw), lambda ri, cb: (ri, cb))
    return _pcall(
        _lru_sample_kernel, name="lru_sample", grid=(ms // r, width // cw),
        in_specs=specs + [st, st],
        args=(proj, proj, conv_w, conv_b3, w_r, b_r3, w_i, b_i3, lam3, h0_rep, buf_fr),
        out_specs=[pl.BlockSpec((r, cw), lambda ri, cb: (rb0 + ri, cb)), f32rows, f32rows],
        out_shape=[jax.ShapeDtypeStruct(yb_prev.shape, BF16),
                   jax.ShapeDtypeStruct((ms, width), F32), jax.ShapeDtypeStruct((ms, width), F32)],
        prev=(yb_prev,),
        sem=("arbitrary", "arbitrary"))


def _pack_pair(hi, lo):
    hb = lax.bitcast_convert_type(hi.astype(jnp.bfloat16).astype(F32), jnp.uint32)
    lb = lax.bitcast_convert_type(lo.astype(jnp.bfloat16).astype(F32), jnp.uint32)
    return hb | (lb >> 16)


def _unpack_pair(p):
    hi = lax.bitcast_convert_type(p & jnp.uint32(0xFFFF0000), F32)
    lo = lax.bitcast_convert_type(p << 16, F32)
    return hi, lo


def _router_kernel(x_ref, g_ref, wr_ref, o_ref, hp_ref, *, ne):
    hf = _rms(x_ref[...], g_ref[...])
    half = hf.shape[1] // 2
    hp_ref[...] = _pack_pair(hf[:, :half], hf[:, half:])
    h = hf.astype(BF16)
    logits = jnp.dot(h, wr_ref[...].astype(BF16), preferred_element_type=F32)
    lane = lax.broadcasted_iota(jnp.int32, logits.shape, 1).astype(F32)
    neg = F32(-jnp.inf)
    big = F32(LANES)
    l1 = jnp.where(lane < ne, logits, neg)
    m1 = jnp.max(l1, axis=-1, keepdims=True)
    i1 = jnp.min(jnp.where(l1 == m1, lane, big), axis=-1, keepdims=True)
    l2 = jnp.where(lane == i1, neg, l1)
    m2 = jnp.max(l2, axis=-1, keepdims=True)
    i2 = jnp.min(jnp.where(l2 == m2, lane, big), axis=-1, keepdims=True)
    e2 = jnp.exp(m2 - m1)
    den = 1.0 + e2
    o_ref[...] = (jnp.where(lane == 0.0, i1, 0.0) + jnp.where(lane == 1.0, i2, 0.0)
                  + jnp.where(lane == 2.0, 1.0 / den, 0.0) + jnp.where(lane == 3.0, e2 / den, 0.0))


def _router(x, g3, l, wr):
    m, d = x.shape
    ne = wr.shape[-1]
    assert TOP_K == 2 and ne <= LANES
    wr_pad = jnp.pad(wr, ((0, 0), (0, LANES - ne)))
    tm = _pick(m, (512, 256, 128, 64, 32, 16, 8))
    return _pcall(
        functools.partial(_router_kernel, ne=ne), name="router", grid=(m // tm,),
        in_specs=[pl.BlockSpec((tm, d), lambda i: (i, 0)), _vspec(l, d),
                  pl.BlockSpec((d, LANES), lambda i: (0, 0))],
        args=(x, g3, wr_pad),
        out_specs=[pl.BlockSpec((tm, LANES), lambda i: (i, 0)),
                   pl.BlockSpec((tm, d // 2), lambda i: (i, 0))],
        out_shape=[jax.ShapeDtypeStruct((m, LANES), F32),
                   jax.ShapeDtypeStruct((m, d // 2), jnp.uint32)],
        sem=("arbitrary",))


def _dispatch_tables(rout, ne, tg):
    m = rout.shape[0]
    na = TOP_K * m
    eid = rout[:, :TOP_K].astype(jnp.int32).T.reshape(na)
    onehot = (eid[:, None] == jnp.arange(ne, dtype=jnp.int32)[None, :]).astype(jnp.int32)
    cnt = jnp.sum(onehot, axis=0)
    rank = jnp.sum((jnp.cumsum(onehot, axis=0) - onehot) * onehot, axis=1)
    padded = ((cnt + tg - 1) // tg) * tg
    gend = jnp.cumsum(padded)
    gstart = gend - padded
    pos = gstart[eid] + rank
    p_rows = ((na + tg - 1) // tg + ne) * tg
    n_tiles = p_rows // tg
    tok = jnp.zeros((p_rows,), jnp.int32).at[pos].set(jnp.arange(na, dtype=jnp.int32) % m)
    n_used = (gend[-1] // tg).astype(jnp.int32)
    tstart = jnp.arange(n_tiles, dtype=jnp.int32) * tg
    tstart = jnp.minimum(tstart, (n_used - 1) * tg)
    te = jnp.minimum(jnp.searchsorted(gend, tstart, side="right"), ne - 1).astype(jnp.int32)
    return pos, tok, te, n_used.reshape(1), p_rows


DMA_UNROLL = 8


def _row_gather(src_hbm, dst, idx_of, sem):
    rows = dst.shape[0]
    assert rows % DMA_UNROLL == 0

    def row_copy(r, t):
        return pltpu.make_async_copy(src_hbm.at[pl.ds(t, 1), :], dst.at[pl.ds(r, 1), :], sem)

    def issue(c, carry):
        for u in range(DMA_UNROLL):
            r = c * DMA_UNROLL + u
            row_copy(r, idx_of(r)).start()
        return carry

    def drain(c, carry):
        for u in range(DMA_UNROLL):
            row_copy(c * DMA_UNROLL + u, 0).wait()
        return carry

    start = lambda: lax.fori_loop(0, rows // DMA_UNROLL, issue, 0)
    wait = lambda: lax.fori_loop(0, rows // DMA_UNROLL, drain, 0)
    return start, wait


def _moe_gather_kernel(tok_ref, hp_ref, o_ref):
    tr, d = o_ref.shape
    half = d // 2

    group = 2 * SUBLANES

    def body(g, carry):
        r0 = pl.multiple_of(g * group, group)
        rows = [hp_ref[pl.ds(tok_ref[0, r0 + u], 1), :] for u in range(group)]
        hi, lo = _unpack_pair(jnp.concatenate(rows, axis=0))
        o_ref[pl.ds(r0, group), pl.ds(0, half)] = hi.astype(o_ref.dtype)
        o_ref[pl.ds(r0, group), pl.ds(half, half)] = lo.astype(o_ref.dtype)
        return carry

    lax.fori_loop(0, tr // group, body, 0)


def _moe_gather(hp, tok, p_rows):
    m, half = hp.shape
    tr = _pick(p_rows, (GATHER_ROWS, 128, 64, 32, 16))
    nt = p_rows // tr
    return _pcall(
        _moe_gather_kernel, name="moe_gather", grid=(nt,),
        in_specs=[pl.BlockSpec((None, 1, tr), lambda i: (i, 0, 0), memory_space=pltpu.SMEM),
                  pl.BlockSpec((m, half), lambda i: (0, 0), pipeline_mode=pl.Buffered(1))],
        args=(tok.reshape(nt, 1, tr), hp),
        out_specs=[pl.BlockSpec((tr, 2 * half), lambda i: (i, 0))],
        out_shape=[jax.ShapeDtypeStruct((p_rows, 2 * half), BF16)],
        sem=("arbitrary",))[0]


def _tile_changed(te_ref):
    ti = pl.program_id(1)
    return jnp.logical_or(ti == 0, te_ref[ti] != te_ref[jnp.maximum(ti - 1, 0)])


def _gmm_swiglu_kernel(te_ref, nu_ref, x_ref, wg_ref, wu_ref, o_ref, wgb_ref, wub_ref):
    @pl.when(_tile_changed(te_ref))
    def _():
        _cast_w(wg_ref, wgb_ref)
        _cast_w(wu_ref, wub_ref)

    used = pl.program_id(1) < nu_ref[0]

    @pl.when(used)
    def _():
        _swiglu_rows(x_ref, wgb_ref, wub_ref, o_ref)

    @pl.when(jnp.logical_not(used))
    def _():
        o_ref[...] = jnp.zeros_like(o_ref)


def _gmm_down_kernel(te_ref, nu_ref, x_ref, w_ref, o_ref, wb_ref):
    @pl.when(_tile_changed(te_ref))
    def _():
        _cast_w(w_ref, wb_ref)

    used = pl.program_id(1) < nu_ref[0]

    @pl.when(used)
    def _():
        tg, hw = o_ref.shape
        nchunk = 4 if tg % (4 * 16) == 0 else 1
        rc = tg // nchunk
        for c in range(nchunk):
            rs = slice(c * rc, (c + 1) * rc)
            y = jnp.dot(x_ref[rs, :], wb_ref[...], preferred_element_type=F32)
            o_ref[rs, :] = _pack_pair(y[:, :hw], y[:, hw:])

    @pl.when(jnp.logical_not(used))
    def _():
        o_ref[...] = jnp.zeros_like(o_ref)


def _gmm_specs(jm, tg, k, tn, j0=0):
    xrow = lambda j, ti, te, nu: (jnp.minimum(ti, nu[0] - 1), 0)
    wsp = pl.BlockSpec((None, None, k, tn), lambda j, ti, te, nu: (jm, te[ti], 0, j0 + j))
    return pl.BlockSpec((tg, k), xrow), wsp, pl.BlockSpec((tg, tn), lambda j, ti, te, nu: (ti, j0 + j))


def _gmm_swiglu(xg, wg, wu, jm, te, n_used, tg, tns):
    p, k = xg.shape
    n = wg.shape[-1]
    assert sum(tn * cnt for tn, cnt in tns) == n
    out, col = None, 0
    for tn, cnt in tns:
        assert col % tn == 0
        xs, ws, os_ = _gmm_specs(jm, tg, k, tn, col // tn)
        out = _pcall(
            _gmm_swiglu_kernel, name="gmm_swiglu", grid=(cnt, p // tg), num_scalar_prefetch=2,
            in_specs=[None, None, xs, ws, ws], args=(te, n_used, xg, wg, wu),
            out_specs=[os_], out_shape=[jax.ShapeDtypeStruct((p, n), BF16)],
            scratch=[pltpu.VMEM((k, tn), BF16), pltpu.VMEM((k, tn), BF16)],
            prev=(out,), sem=("arbitrary", "arbitrary"))[0]
        col += tn * cnt
    return out


def _col_tiling(n, wide, narrow):
    nw = n // wide
    rest = n - nw * wide
    assert rest % narrow == 0 and (nw * wide) % narrow == 0
    return tuple(t for t in ((wide, nw), (narrow, rest // narrow)) if t[1])


def _gmm_down(ug, wd, jm, te, n_used, tg, tn):
    p, k = ug.shape
    n = wd.shape[-1]
    xs, ws, _ = _gmm_specs(jm, tg, k, tn)
    return _pcall(
        _gmm_down_kernel, name="gmm_down", grid=(n // tn, p // tg), num_scalar_prefetch=2,
        in_specs=[None, None, xs, ws], args=(te, n_used, ug, wd),
        out_specs=[pl.BlockSpec((tg, tn // 2), lambda j, ti, te, nu: (ti, j))],
        out_shape=[jax.ShapeDtypeStruct((p, n // 2), jnp.uint32)],
        scratch=[pltpu.VMEM((k, tn), BF16)],
        sem=("arbitrary", "arbitrary"))[0]


def _moe_combine_kernel(pos_ref, nxt_ref, x_ref, rw_ref, g_ref, yg_hbm, *rest, tn, np_tiles):
    outs, (buf, sem) = rest[:-2], rest[-2:]
    tc = x_ref.shape[0]
    i = pl.program_id(0)
    slot = lax.rem(i, 2)

    def gathers(idx_ref, s):
        return [_row_gather(yg_hbm, buf.at[s, k], lambda r, k=k: idx_ref[0, k * tc + r], sem.at[s])
                for k in range(TOP_K)]

    @pl.when(i == 0)
    def _():
        for start, _ in gathers(pos_ref, 0):
            start()

    @pl.when(i + 1 < pl.num_programs(0))
    def _():
        for start, _ in gathers(nxt_ref, 1 - slot):
            start()

    for _, wait in gathers(pos_ref, slot):
        wait()
    rows = buf[slot]

    def expand(p):
        hw = tn // 2
        parts = []
        for jj in range(p.shape[1] // hw):
            parts.extend(_unpack_pair(p[:, jj * hw:(jj + 1) * hw]))
        return jnp.concatenate(parts, axis=-1)

    rw = rw_ref[...]
    y = rw[:, TOP_K:TOP_K + 1] * expand(rows[0])
    for k in range(1, TOP_K):
        y = y + rw[:, TOP_K + k:TOP_K + k + 1] * expand(rows[k])
    xn = x_ref[...] + y
    if np_tiles is None:
        outs[0][...] = xn
    else:
        yn = _rms(xn, g_ref[...])

        @pl.when(i < np_tiles)
        def _():
            outs[0][...] = yn

        @pl.when(i >= np_tiles)
        def _():
            outs[1][...] = yn


def _moe_combine(x, rout, pos, ygp, tn, g3=None, mp=None):
    m, d = x.shape
    tc = _pick(m, (COMBINE_ROWS, 64, 32, 16, 8))
    nt = m // tc
    pos_t = pos.reshape(TOP_K, nt, tc).transpose(1, 0, 2).reshape(nt, 1, TOP_K * tc)
    if g3 is None:
        g3 = jnp.ones((1, 1, d), F32)
        np_tiles = None
        out_specs = [pl.BlockSpec((tc, d), lambda i: (i, 0))]
        out_shape = [jax.ShapeDtypeStruct((m, d), F32)]
    else:
        assert mp % tc == 0
        np_tiles = mp // tc
        out_specs = [pl.BlockSpec((tc, d), lambda i: (jnp.minimum(i, np_tiles - 1), 0)),
                     pl.BlockSpec((tc, d), lambda i: (jnp.maximum(i - np_tiles, 0), 0))]
        out_shape = [jax.ShapeDtypeStruct((mp, d), F32), jax.ShapeDtypeStruct((m - mp, d), F32)]
    return _pcall(
        functools.partial(_moe_combine_kernel, tn=tn, np_tiles=np_tiles), name="moe_combine", grid=(nt,),
        in_specs=[pl.BlockSpec((None, 1, TOP_K * tc), lambda i: (i, 0, 0), memory_space=pltpu.SMEM),
                  pl.BlockSpec((None, 1, TOP_K * tc), lambda i: (jnp.minimum(i + 1, nt - 1), 0, 0),
                               memory_space=pltpu.SMEM),
                  pl.BlockSpec((tc, d), lambda i: (i, 0)),
                  pl.BlockSpec((tc, LANES), lambda i: (i, 0)),
                  _vspec(0, d),
                  pl.BlockSpec(memory_space=pl.ANY)],
        args=(pos_t, pos_t, x, rout, g3, ygp),
        out_specs=out_specs, out_shape=out_shape,
        scratch=[pltpu.VMEM((2, TOP_K, tc, d // 2), jnp.uint32), pltpu.SemaphoreType.DMA((2,))],
        sem=("arbitrary",))


def kernel(x_prompt, x_sample, state_ret, state_lru, state_conv, norm_mix, w_in, ret_gn, w_ret_o, conv_w, conv_b, w_rgate, b_rgate, w_igate, b_igate, lru_lambda, w_lru_o, w_out, norm_ffn, ffn_w_gate, ffn_w_up, ffn_w_down, moe_router, moe_w_gate, moe_w_up, moe_w_down, norm_final):
    bp, tp, d = x_prompt.shape
    bs, ts, _ = x_sample.shape
    depth, _, nh, dk, dv = state_ret.shape
    width = state_lru.shape[-1]
    ncw = conv_w.shape[1]
    ne = moe_router.shape[-1]
    assert dk == dv and nh * dk == width == d and ts == SUBLANES and ncw - 1 <= SUBLANES
    mp, ms = bp * tp, bs * ts
    m = mp + ms
    g_col = 2 * nh * dk + nh * dv
    ux_col = g_col + nh * dv
    ga_col = ux_col + 2 * width

    tm = _pick(m, (1024, 512, 256, 128, 64, 32, 16, 8))
    tm_big = _pick(m, (1536, 1024, 512, 256, 128, 64, 32, 16, 8))
    tm_half = _pick(m, (512, 256, 128, 64, 32, 16, 8))
    tn_of = lambda n: _pick(n, (1024, 512, 256, 128))

    vec3 = lambda a: a.reshape(a.shape[0], 1, a.shape[-1])
    norm_mix3, norm_ffn3, ret_gn3 = vec3(norm_mix), vec3(norm_ffn), vec3(ret_gn)
    norm_final3 = norm_final.reshape(1, 1, d)
    lru_w = (conv_w, vec3(conv_b), w_rgate, vec3(b_rgate), w_igate, vec3(b_igate), vec3(lru_lambda))
    h0_rep = jnp.repeat(state_lru, ts, axis=1)
    buf_fr = jnp.pad(state_conv, ((0, 0), (0, 0), (ts - (ncw - 1), 0), (0, 0))).reshape(depth, ms, width)

    cos_p, sin_p = _rope_tables(tp, dk // 2, 0)
    cos_s, sin_s = _rope_tables(ts, dk // 2, PAST_LEN)
    cos = jnp.concatenate([jnp.tile(cos_p, (bp, 1)), jnp.tile(cos_s, (bs, 1))], axis=0)
    sin = jnp.concatenate([jnp.tile(sin_p, (bp, 1)), jnp.tile(sin_s, (bs, 1))], axis=0)
    tn_in = _pick(nh * dk, (1024, 512, 256))

    x = jnp.concatenate([x_prompt.reshape(mp, d), x_sample.reshape(ms, d)], axis=0)
    ret_p = ret_s = y_p = y_s = None
    lrus_p, lrus_s, convs_p, convs_s = [], [], [], []
    for l in range(depth):
        h = _rmsnorm(x, norm_mix3, l, BF16)
        proj = _mm_inproj(h, w_in, (l,), cos, sin, 2 * nh * dk, dk, (g_col, ux_col),
                          (ux_col + width, ga_col), tm_big, tn_in)

        ya, ret_p = _ret_prompt(proj, ret_gn3, l, depth, bp, tp, nh, dk, dv, ret_p)
        ya, ret_s = _ret_sample(proj, ret_gn3, state_ret, l, mp, bs, ts, nh, dk, dv, ya, ret_s)
        yb, hl_p, ul_p = _lru_prompt(proj, ux_col, l, lru_w, bp, tp)
        yb, hl_s, us_s = _lru_sample(proj, ux_col, l, lru_w, mp, h0_rep, buf_fr, yb)
        lrus_p.append(hl_p[:, -1])
        lrus_s.append(hl_s.reshape(bs, ts, width)[:, -1])
        convs_p.append(ul_p[:, SUBLANES - (ncw - 1):])
        convs_s.append(us_s.reshape(bs, ts, width)[:, ts - (ncw - 1):])

        z = _mm_merge(ya, yb, w_ret_o, w_lru_o, (l,), proj, ga_col, ga_col + d, tm, tn_of(d) // 2)
        x = _mm_resid(z, w_out, (l,), x, tm, tn_of(d))
        j = l // 2
        if l % 2 == 0:
            h2 = _rmsnorm(x, norm_ffn3, l, BF16)
            ff = ffn_w_gate.shape[-1]
            u = _mm_swiglu(h2, ffn_w_gate, ffn_w_up, (j,), tm_big, _pick(ff, (512, 256, 128)))
            x = _mm_resid(u, ffn_w_down, (j,), x, tm_half, _pick(d, (512, 256, 128)))
        else:
            fe = moe_w_gate.shape[-1]
            tg = _pick(TOP_K * m, (MOE_TILE, 256, 128, 64, 32, 16))
            tn_d = _pick(d, (512, 256))
            rout, hp = _router(x, norm_ffn3, l, moe_router[j])
            pos, tok, te, n_used, p_rows = _dispatch_tables(rout, ne, tg)
            xg = _moe_gather(hp, tok, p_rows)
            ug = _gmm_swiglu(xg, moe_w_gate, moe_w_up, j, te, n_used, tg,
                             _col_tiling(fe, 512, _pick(fe, (256, 128))))
            ygp = _gmm_down(ug, moe_w_down, j, te, n_used, tg, tn_d)
            if l == depth - 1:
                y_p, y_s = _moe_combine(x, rout, pos, ygp, tn_d, norm_final3, mp)
            else:
                x = _moe_combine(x, rout, pos, ygp, tn_d)[0]
    if y_p is None:
        y_p = _rmsnorm(x, norm_final3, 0, F32, 0, mp)
        y_s = _rmsnorm(x, norm_final3, 0, F32, mp, ms)
    return (y_p.reshape(bp, tp, d), y_s.reshape(bs, ts, d),
            ret_p, jnp.stack(lrus_p), jnp.stack(convs_p),
            ret_s, jnp.stack(lrus_s), jnp.stack(convs_s))
```

```python
import functools

import jax
import jax.numpy as jnp
from jax import lax
from jax.experimental import pallas as pl
from jax.experimental.pallas import tpu as pltpu

F32 = jnp.float32
BF16 = jnp.bfloat16

ROPE_BASE = 10000.0
LRU_C = 8.0
EPS = 1e-6
RET_CHUNK = 128
PAST_LEN = 16384
TOP_K = 2
SUBLANES = 8
LANES = 128
VMEM_LIMIT = 56 * 1024 * 1024
MOE_TILE = 1024
GATHER_ROWS = 256
COMBINE_ROWS = 128


def _pick(n, prefs):
    for p in prefs:
        if n % p == 0:
            return p
    return n


def _params(*sem):
    return pltpu.CompilerParams(dimension_semantics=sem, vmem_limit_bytes=VMEM_LIMIT)


def _pcall(body, *, name, grid, in_specs, args, out_specs, out_shape, sem, scratch=(), prev=(),
           num_scalar_prefetch=0):
    n_in = len(args)
    prev = tuple(prev) + (None,) * (len(out_shape) - len(prev))
    extra = [(oi, p) for oi, p in enumerate(prev) if p is not None]
    aliases = {n_in + e: oi for e, (oi, _) in enumerate(extra)}
    if extra:
        inner = body

        def body(*refs):
            return inner(*refs[:n_in], *refs[n_in + len(extra):])

    in_specs = list(in_specs) + [pl.BlockSpec(memory_space=pl.ANY)] * len(extra)
    if num_scalar_prefetch:
        grid_spec = pltpu.PrefetchScalarGridSpec(
            num_scalar_prefetch=num_scalar_prefetch, grid=grid, in_specs=in_specs[num_scalar_prefetch:],
            out_specs=tuple(out_specs), scratch_shapes=list(scratch))
        call = pl.pallas_call(body, grid_spec=grid_spec, out_shape=tuple(out_shape),
                              input_output_aliases=aliases, compiler_params=_params(*sem), name=name)
    else:
        call = pl.pallas_call(body, grid=grid, in_specs=in_specs, out_specs=tuple(out_specs),
                              out_shape=tuple(out_shape), scratch_shapes=list(scratch),
                              input_output_aliases=aliases, compiler_params=_params(*sem), name=name)
    return call(*args, *[p for _, p in extra])


def _wspec(pre, k, tn):
    return pl.BlockSpec((None,) * len(pre) + (k, tn), lambda j, i: pre + (0, j))


def _vspec(l, n):
    return pl.BlockSpec((None, 1, n), lambda *_: (l, 0, 0))


def _rms(x, g):
    return (x * lax.rsqrt(jnp.mean(x * x, axis=-1, keepdims=True) + EPS)) * g


def _rmsnorm_kernel(x_ref, g_ref, o_ref):
    o_ref[...] = _rms(x_ref[...], g_ref[...]).astype(o_ref.dtype)


def _rmsnorm(x, g3, l, out_dtype, row0=0, rows=None):
    d = x.shape[1]
    rows = x.shape[0] if rows is None else rows
    tm = _pick(rows, (512, 256, 128, 64, 32, 16, 8))
    assert row0 % tm == 0
    rb0 = row0 // tm
    return _pcall(
        _rmsnorm_kernel, name="rmsnorm", grid=(rows // tm,),
        in_specs=[pl.BlockSpec((tm, d), lambda i: (rb0 + i, 0)), _vspec(l, d)],
        args=(x, g3),
        out_specs=[pl.BlockSpec((tm, d), lambda i: (i, 0))],
        out_shape=[jax.ShapeDtypeStruct((rows, d), out_dtype)],
        sem=("arbitrary",))[0]


def _cast_w(w_ref, wb_ref):
    k = w_ref.shape[0]
    ck = _pick(k, (256, 128, 64, 32, 16))

    def body(c, carry):
        r = pl.multiple_of(c * ck, ck)
        wb_ref[pl.ds(r, ck), :] = w_ref[pl.ds(r, ck), :].astype(BF16)
        return carry

    lax.fori_loop(0, k // ck, body, 0)


def _mm_inproj_kernel(x_ref, w_ref, cos_ref, sin_ref, o_ref, wb_ref, *, nrot, dk, silu_tiles, gelu_tiles):
    j = pl.program_id(0)

    @pl.when(pl.program_id(1) == 0)
    def _():
        _cast_w(w_ref, wb_ref)

    tm, tn = o_ref.shape
    nchunk = 4 if tm % (4 * 16) == 0 else 1

    def row_chunks(epilogue):
        rc = tm // nchunk
        for c in range(nchunk):
            rs = slice(c * rc, (c + 1) * rc)
            epilogue(jnp.dot(x_ref[rs, :], wb_ref[...], preferred_element_type=F32), rs)

    def store(fn):
        def epilogue(acc, rs):
            o_ref[rs, :] = fn(acc).astype(o_ref.dtype)
        return epilogue

    in_tiles = lambda t: jnp.logical_and(j >= t[0], j < t[1])
    is_silu, is_gelu = in_tiles(silu_tiles), in_tiles(gelu_tiles)

    @pl.when(jnp.logical_and(j >= nrot, jnp.logical_not(jnp.logical_or(is_silu, is_gelu))))
    def _():
        row_chunks(store(lambda acc: acc))

    @pl.when(is_silu)
    def _():
        row_chunks(store(jax.nn.silu))

    @pl.when(is_gelu)
    def _():
        row_chunks(store(jax.nn.gelu))

    @pl.when(j < nrot)
    def _():
        scale = jnp.where(j >= nrot // 2, F32(dk ** -0.5), F32(1.0))
        half = dk // 2

        def rotary(acc, rs):
            cos = cos_ref[rs, :] * scale
            sin = sin_ref[rs, :] * scale
            for h in range(tn // dk):
                x1 = acc[:, h * dk:h * dk + half]
                x2 = acc[:, h * dk + half:(h + 1) * dk]
                o_ref[rs, h * dk:h * dk + half] = (x1 * cos - x2 * sin).astype(o_ref.dtype)
                o_ref[rs, h * dk + half:(h + 1) * dk] = (x1 * sin + x2 * cos).astype(o_ref.dtype)

        row_chunks(rotary)


def _mm_inproj(x, w, pre, cos, sin, qk_cols, dk, silu_cols, gelu_cols, tm, tn):
    m, k = x.shape
    n = w.shape[-1]
    half = cos.shape[1]
    assert qk_cols % (2 * tn) == 0 and tn % dk == 0
    assert all(c % tn == 0 for c in silu_cols + gelu_cols)
    tiles = lambda cols: (cols[0] // tn, cols[1] // tn)
    return _pcall(
        functools.partial(_mm_inproj_kernel, nrot=qk_cols // tn, dk=dk, silu_tiles=tiles(silu_cols),
                          gelu_tiles=tiles(gelu_cols)), name="mm_inproj",
        grid=(n // tn, m // tm),
        in_specs=[pl.BlockSpec((tm, k), lambda j, i: (i, 0)), _wspec(pre, k, tn),
                  pl.BlockSpec((tm, half), lambda j, i: (i, 0)),
                  pl.BlockSpec((tm, half), lambda j, i: (i, 0))],
        args=(x, w, cos, sin),
        out_specs=[pl.BlockSpec((tm, tn), lambda j, i: (i, j))],
        out_shape=[jax.ShapeDtypeStruct((m, n), BF16)],
        scratch=[pltpu.VMEM((k, tn), BF16)],
        sem=("arbitrary", "arbitrary"))[0]


def _mm_resid_kernel(x_ref, w_ref, r_ref, o_ref, wb_ref):
    @pl.when(pl.program_id(1) == 0)
    def _():
        _cast_w(w_ref, wb_ref)

    o_ref[...] = r_ref[...] + jnp.dot(x_ref[...], wb_ref[...], preferred_element_type=F32)


def _mm_resid(x, w, pre, r, tm, tn):
    m, k = x.shape
    n = w.shape[-1]
    return _pcall(
        _mm_resid_kernel, name="mm_resid", grid=(n // tn, m // tm),
        in_specs=[pl.BlockSpec((tm, k), lambda j, i: (i, 0)), _wspec(pre, k, tn),
                  pl.BlockSpec((tm, tn), lambda j, i: (i, j))],
        args=(x, w, r),
        out_specs=[pl.BlockSpec((tm, tn), lambda j, i: (i, j))],
        out_shape=[jax.ShapeDtypeStruct((m, n), F32)],
        scratch=[pltpu.VMEM((k, tn), BF16)],
        sem=("arbitrary", "arbitrary"))[0]


def _row_chunks(rows, fn, nchunk=4):
    n = nchunk if rows % (nchunk * 16) == 0 else 1
    rc = rows // n
    for c in range(n):
        fn(slice(c * rc, (c + 1) * rc))


def _swiglu_rows(x_ref, wgb_ref, wub_ref, o_ref):
    def chunk(rs):
        x = x_ref[rs, :]
        g = jnp.dot(x, wgb_ref[...], preferred_element_type=F32)
        u = jnp.dot(x, wub_ref[...], preferred_element_type=F32)
        o_ref[rs, :] = (jax.nn.silu(g) * u).astype(o_ref.dtype)

    _row_chunks(o_ref.shape[0], chunk)


def _mm_swiglu_kernel(x_ref, wg_ref, wu_ref, o_ref, wgb_ref, wub_ref):
    @pl.when(pl.program_id(1) == 0)
    def _():
        _cast_w(wg_ref, wgb_ref)
        _cast_w(wu_ref, wub_ref)

    _swiglu_rows(x_ref, wgb_ref, wub_ref, o_ref)


def _mm_swiglu(x, wg, wu, pre, tm, tn):
    m, k = x.shape
    n = wg.shape[-1]
    return _pcall(
        _mm_swiglu_kernel, name="mm_swiglu", grid=(n // tn, m // tm),
        in_specs=[pl.BlockSpec((tm, k), lambda j, i: (i, 0)), _wspec(pre, k, tn), _wspec(pre, k, tn)],
        args=(x, wg, wu),
        out_specs=[pl.BlockSpec((tm, tn), lambda j, i: (i, j))],
        out_shape=[jax.ShapeDtypeStruct((m, n), BF16)],
        scratch=[pltpu.VMEM((k, tn), BF16), pltpu.VMEM((k, tn), BF16)],
        sem=("arbitrary", "arbitrary"))[0]


def _mm_merge_kernel(a_ref, b_ref, wa_ref, wb_ref, ga_ref, gb_ref, o_ref, wab_ref, wbb_ref):
    @pl.when(pl.program_id(1) == 0)
    def _():
        _cast_w(wa_ref, wab_ref)
        _cast_w(wb_ref, wbb_ref)

    def chunk(rs):
        ya = jnp.dot(a_ref[rs, :], wab_ref[...], preferred_element_type=F32)
        yb = jnp.dot(b_ref[rs, :], wbb_ref[...], preferred_element_type=F32)
        ga = ga_ref[rs, :].astype(F32)
        gb = gb_ref[rs, :].astype(F32)
        o_ref[rs, :] = (jax.nn.sigmoid(ga) * ya + jax.nn.sigmoid(gb) * yb).astype(o_ref.dtype)

    _row_chunks(o_ref.shape[0], chunk)


def _mm_merge(a, b, wa, wb, pre, proj, ga_col, gb_col, tm, tn):
    m, k = a.shape
    n = wa.shape[-1]
    ga_blk, gb_blk = ga_col // tn, gb_col // tn
    return _pcall(
        _mm_merge_kernel, name="mm_merge", grid=(n // tn, m // tm),
        in_specs=[pl.BlockSpec((tm, k), lambda j, i: (i, 0)),
                  pl.BlockSpec((tm, k), lambda j, i: (i, 0)),
                  _wspec(pre, k, tn), _wspec(pre, k, tn),
                  pl.BlockSpec((tm, tn), lambda j, i: (i, ga_blk + j)),
                  pl.BlockSpec((tm, tn), lambda j, i: (i, gb_blk + j))],
        args=(a, b, wa, wb, proj, proj),
        out_specs=[pl.BlockSpec((tm, tn), lambda j, i: (i, j))],
        out_shape=[jax.ShapeDtypeStruct((m, n), BF16)],
        scratch=[pltpu.VMEM((k, tn), BF16), pltpu.VMEM((k, tn), BF16)],
        sem=("arbitrary", "arbitrary"))[0]


def _rope_kernel(inv_ref, cos_ref, sin_ref, *, pos0):
    t, half = cos_ref.shape
    pos = lax.broadcasted_iota(jnp.int32, (t, half), 0).astype(F32) + F32(pos0)
    ang = pos * inv_ref[...]
    cos_ref[...] = jnp.cos(ang)
    sin_ref[...] = jnp.sin(ang)


def _rope_tables(t, half, pos0):
    inv = ROPE_BASE ** (-jnp.arange(half, dtype=F32) / half)
    return pl.pallas_call(
        functools.partial(_rope_kernel, pos0=pos0),
        out_shape=(jax.ShapeDtypeStruct((t, half), F32), jax.ShapeDtypeStruct((t, half), F32)),
        name="rope_tables",
    )(inv.reshape(1, half))


def _decay_tables(c, h, dk, dv):
    log_g = jnp.log1p(-jnp.exp2(-5.0 - jnp.arange(h, dtype=F32)))
    idx = jnp.arange(c)
    rel = idx[:, None] - idx[None, :]
    dmask = jnp.where(rel[None] >= 0,
                      jnp.exp(jnp.maximum(rel, 0)[None].astype(F32) * log_g[:, None, None]), 0.0)
    xi = jnp.exp((idx + 1).astype(F32)[None, :] * log_g[:, None])
    zeta = jnp.exp((c - 1 - idx).astype(F32)[None, :] * log_g[:, None])
    g_c = jnp.exp(c * log_g)
    xi_t = jnp.broadcast_to(xi[:, :, None], (h, c, dv))
    zeta_t = jnp.broadcast_to(zeta[:, :, None], (h, c, dk))
    return dmask, xi_t, zeta_t, g_c


def _ret_head(qb, kb, v, g, dm, xi, zt, gn, s, gc):
    scores = lax.dot_general(qb, kb, (((1,), (1,)), ((), ())), preferred_element_type=F32)
    intra = jnp.dot((scores * dm).astype(BF16), v, preferred_element_type=F32)
    cross = jnp.dot(qb, s.astype(BF16), preferred_element_type=F32) * xi
    kz = (kb.astype(F32) * zt).astype(BF16)
    upd = lax.dot_general(kz, v, (((0,), (0,)), ((), ())), preferred_element_type=F32)
    s_new = gc * s + upd
    o = intra + cross
    mu = jnp.mean(o, axis=-1, keepdims=True)
    d = o - mu
    var = jnp.mean(d * d, axis=-1, keepdims=True)
    y = d * lax.rsqrt(var + EPS) * gn
    return (g * y).astype(BF16), s_new


def _ret_prompt_kernel(gc_ref, q_ref, k_ref, v_ref, g_ref, dm_ref, xi_ref, zt_ref,
                       gn_ref, o_ref, s_ref, *, nh, dk, dv):
    @pl.when(pl.program_id(1) == 0)
    def _():
        s_ref[...] = jnp.zeros_like(s_ref)

    c = dm_ref.shape[1]
    for h in range(nh):
        s = s_ref[h]
        for ci in range(q_ref.shape[0] // c):
            rs = slice(ci * c, (ci + 1) * c)
            out, s = _ret_head(
                q_ref[rs, h * dk:(h + 1) * dk], k_ref[rs, h * dk:(h + 1) * dk],
                v_ref[rs, h * dv:(h + 1) * dv], g_ref[rs, h * dv:(h + 1) * dv].astype(F32),
                dm_ref[h], xi_ref[h], zt_ref[h], gn_ref[:, h * dv:(h + 1) * dv], s, gc_ref[h])
            o_ref[rs, h * dv:(h + 1) * dv] = out
        s_ref[h] = s


def _ret_prompt(proj, gn3, l, depth, b, t, nh, dk, dv, s_prev):
    cl = RET_CHUNK if t % RET_CHUNK == 0 else t
    cps = _pick(t // cl, (4, 2, 1))
    c = cl * cps
    nc = t // c
    w = nh * dk
    m = proj.shape[0]
    dmask, xi_t, zeta_t, g_c = _decay_tables(cl, nh, dk, dv)
    row = lambda bi, ci: bi * nc + ci
    full3 = lambda bi, ci: (0, 0, 0)
    return _pcall(
        functools.partial(_ret_prompt_kernel, nh=nh, dk=dk, dv=dv), name="ret_prompt", grid=(b, nc),
        in_specs=[pl.BlockSpec(memory_space=pltpu.SMEM),
                  pl.BlockSpec((c, w), lambda bi, ci: (row(bi, ci), 0)),
                  pl.BlockSpec((c, w), lambda bi, ci: (row(bi, ci), 1)),
                  pl.BlockSpec((c, w), lambda bi, ci: (row(bi, ci), 2)),
                  pl.BlockSpec((c, w), lambda bi, ci: (row(bi, ci), 3)),
                  pl.BlockSpec((nh, cl, cl), full3),
                  pl.BlockSpec((nh, cl, dv), full3),
                  pl.BlockSpec((nh, cl, dk), full3),
                  _vspec(l, nh * dv)],
        args=(g_c, proj, proj, proj, proj, dmask, xi_t, zeta_t, gn3),
        out_specs=[pl.BlockSpec((c, nh * dv), lambda bi, ci: (row(bi, ci), 0)),
                   pl.BlockSpec((None, None, nh, dk, dv), lambda bi, ci: (l, bi, 0, 0, 0))],
        out_shape=[jax.ShapeDtypeStruct((m, nh * dv), BF16),
                   jax.ShapeDtypeStruct((depth, b, nh, dk, dv), F32)],
        prev=(None, s_prev),
        sem=("arbitrary", "arbitrary"))


def _ret_sample_kernel(gc_ref, q_ref, k_ref, v_ref, g_ref, dm_ref, xi_ref, zt_ref,
                       gn_ref, s0_ref, o_ref, s_ref, *, nh, dk, dv, bb, ts):
    q = q_ref[...].astype(F32)
    k = k_ref[...].astype(F32)
    v = v_ref[...].astype(F32)
    g = g_ref[...].astype(F32)
    seqs = []
    for i in range(bb):
        r0, r1 = i * ts, (i + 1) * ts
        outs = []
        for h in range(nh):
            out, s_new = _ret_head(
                q[r0:r1, h * dk:(h + 1) * dk].astype(BF16), k[r0:r1, h * dk:(h + 1) * dk].astype(BF16),
                v[r0:r1, h * dv:(h + 1) * dv].astype(BF16), g[r0:r1, h * dv:(h + 1) * dv],
                dm_ref[h], xi_ref[h], zt_ref[h], gn_ref[:, h * dv:(h + 1) * dv],
                s0_ref[i, h], gc_ref[h])
            s_ref[i, h] = s_new
            outs.append(out.astype(F32))
        seqs.append(jnp.concatenate(outs, axis=-1))
    o_ref[...] = jnp.concatenate(seqs, axis=0).astype(o_ref.dtype)


def _ret_sample(proj, gn3, s0_all, l, row0, bs, ts, nh, dk, dv, ya_prev, s_prev):
    bb = _pick(bs, (4, 2, 1))
    rows = bb * ts
    w = nh * dk
    depth = s0_all.shape[0]
    dmask, xi_t, zeta_t, g_c = _decay_tables(ts, nh, dk, dv)
    assert row0 % rows == 0
    rb0 = row0 // rows
    full3 = lambda i: (0, 0, 0)
    return _pcall(
        functools.partial(_ret_sample_kernel, nh=nh, dk=dk, dv=dv, bb=bb, ts=ts), name="ret_sample",
        grid=(bs // bb,),
        in_specs=[pl.BlockSpec(memory_space=pltpu.SMEM),
                  pl.BlockSpec((rows, w), lambda i: (rb0 + i, 0)),
                  pl.BlockSpec((rows, w), lambda i: (rb0 + i, 1)),
                  pl.BlockSpec((rows, w), lambda i: (rb0 + i, 2)),
                  pl.BlockSpec((rows, w), lambda i: (rb0 + i, 3)),
                  pl.BlockSpec((nh, ts, ts), full3),
                  pl.BlockSpec((nh, ts, dv), full3),
                  pl.BlockSpec((nh, ts, dk), full3),
                  _vspec(l, nh * dv),
                  pl.BlockSpec((None, bb, nh, dk, dv), lambda i: (l, i, 0, 0, 0))],
        args=(g_c, proj, proj, proj, proj, dmask, xi_t, zeta_t, gn3, s0_all),
        out_specs=[pl.BlockSpec((rows, nh * dv), lambda i: (rb0 + i, 0)),
                   pl.BlockSpec((None, bb, nh, dk, dv), lambda i: (l, i, 0, 0, 0))],
        out_shape=[jax.ShapeDtypeStruct(ya_prev.shape, BF16),
                   jax.ShapeDtypeStruct((depth, bs, nh, dk, dv), F32)],
        prev=(ya_prev, s_prev),
        sem=("arbitrary",))


def _lru_gates(uc, wr_ref, br_ref, wi_ref, bi_ref, lam_ref):
    nb, lb, _ = wr_ref.shape
    ucb = uc.astype(BF16)
    rl, il = [], []
    for n in range(nb):
        xb = ucb[:, n * lb:(n + 1) * lb]
        rl.append(jnp.dot(xb, wr_ref[n].astype(BF16), preferred_element_type=F32))
        il.append(jnp.dot(xb, wi_ref[n].astype(BF16), preferred_element_type=F32))
    r = jax.nn.sigmoid(jnp.concatenate(rl, axis=-1) + br_ref[...])
    i = jax.nn.sigmoid(jnp.concatenate(il, axis=-1) + bi_ref[...])
    log_a = -LRU_C * r * jax.nn.softplus(-lam_ref[...])
    a = jnp.exp(log_a)
    z = -jnp.tanh(log_a) * (a * a + 1.0)
    bx = jnp.where(z == 0.0, 0.0, z * lax.rsqrt(z)) * (i * uc)
    return a, bx


def _blocks(x):
    return [x[lo:lo + SUBLANES, :] for lo in range(0, x.shape[0], SUBLANES)]


def _conv_blocks(u_blocks, prev_blocks, cw_ref, cb_ref, rows):
    ncw = cw_ref.shape[0]
    rolled = {}

    def rot(x, s):
        key = (id(x), s)
        if key not in rolled:
            rolled[key] = pltpu.roll(x, s, axis=0)
        return rolled[key]

    out = []
    for ub, pb in zip(u_blocks, prev_blocks):
        uc = cb_ref[...]
        for j in range(ncw):
            s = ncw - 1 - j
            term = ub if s == 0 else jnp.where(rows >= s, rot(ub, s), rot(pb, s))
            uc = uc + term * cw_ref[j:j + 1, :]
        out.append(uc)
    return jnp.concatenate(out, axis=0)


def _scan_block(a, bx, rows):
    d = 1
    while d < SUBLANES:
        m = rows >= d
        bx = bx + a * jnp.where(m, pltpu.roll(bx, d, axis=0), 0.0)
        a = a * jnp.where(m, pltpu.roll(a, d, axis=0), 1.0)
        d *= 2
    return a, bx


def _lru_prompt_kernel(ux_ref, uy_ref, cw_ref, cb_ref, wr_ref, br_ref, wi_ref, bi_ref, lam_ref,
                       o_ref, hl_ref, ul_ref, hc_ref):
    r, cw = ux_ref.shape

    @pl.when(pl.program_id(2) == 0)
    def _():
        hc_ref[...] = jnp.zeros_like(hc_ref)
        ul_ref[...] = jnp.zeros_like(ul_ref)

    rows = lax.broadcasted_iota(jnp.int32, (SUBLANES, cw), 0)
    ub = _blocks(ux_ref[...].astype(F32))
    uc = _conv_blocks(ub, [ul_ref[...]] + ub[:-1], cw_ref, cb_ref, rows)
    ul_ref[...] = ub[-1]
    a, bx = _lru_gates(uc, wr_ref, br_ref, wi_ref, bi_ref, lam_ref)
    carry = hc_ref[...]
    hs = []
    for ab, bb in zip(_blocks(a), _blocks(bx)):
        ab, bb = _scan_block(ab, bb, rows)
        hb = ab * carry + bb
        hs.append(hb)
        carry = jnp.broadcast_to(hb[SUBLANES - 1:SUBLANES, :], (SUBLANES, cw))
    hc_ref[...] = carry
    hl_ref[...] = hs[-1]
    h = jnp.concatenate(hs, axis=0)
    o_ref[...] = (h * uy_ref[...].astype(F32)).astype(o_ref.dtype)


def _lru_sample_kernel(ux_ref, uy_ref, cw_ref, cb_ref, wr_ref, br_ref, wi_ref, bi_ref, lam_ref,
                       h0_ref, buf_ref, o_ref, hl_ref, us_ref):
    r, cw = ux_ref.shape
    rows = lax.broadcasted_iota(jnp.int32, (SUBLANES, cw), 0)
    u = ux_ref[...].astype(F32)
    us_ref[...] = u
    uc = _conv_blocks(_blocks(u), _blocks(buf_ref[...]), cw_ref, cb_ref, rows)
    a, bx = _lru_gates(uc, wr_ref, br_ref, wi_ref, bi_ref, lam_ref)
    hs = []
    for ab, bb, h0 in zip(_blocks(a), _blocks(bx), _blocks(h0_ref[...])):
        ab, bb = _scan_block(ab, bb, rows)
        hs.append(ab * h0 + bb)
    h = jnp.concatenate(hs, axis=0)
    hl_ref[...] = h
    o_ref[...] = (h * uy_ref[...].astype(F32)).astype(o_ref.dtype)


def _lru_specs(l, r, cw, ncw, lb, ux_blk, uy_blk, row_of, cb_of):
    nbc = cw // lb
    vec = pl.BlockSpec((None, 1, cw), lambda *g: (l, 0, cb_of(*g)))
    gate = pl.BlockSpec((None, nbc, lb, lb), lambda *g: (l, cb_of(*g), 0, 0))
    return [pl.BlockSpec((r, cw), lambda *g: (row_of(*g), ux_blk + cb_of(*g))),
            pl.BlockSpec((r, cw), lambda *g: (row_of(*g), uy_blk + cb_of(*g))),
            pl.BlockSpec((None, ncw, cw), lambda *g: (l, 0, cb_of(*g))),
            vec, gate, vec, gate, vec, vec]


def _lru_prompt(proj, ux_col, l, lru_w, b, t):
    conv_w, conv_b3, w_r, b_r3, w_i, b_i3, lam3 = lru_w
    width = lam3.shape[-1]
    lb = w_r.shape[-1]
    ncw = conv_w.shape[1]
    m = proj.shape[0]
    cw = _pick(width, (1024, 512, 256, 128))
    r = _pick(t, (256, 128, 64, 32, 16, 8))
    nt = t // r
    specs = _lru_specs(l, r, cw, ncw, lb, ux_col // cw, (ux_col + width) // cw,
                       lambda bi, cb, ti: bi * nt + ti, lambda bi, cb, ti: cb)
    last8 = pl.BlockSpec((None, SUBLANES, cw), lambda bi, cb, ti: (bi, 0, cb))
    return _pcall(
        _lru_prompt_kernel, name="lru_prompt", grid=(b, width // cw, nt),
        in_specs=specs,
        args=(proj, proj, conv_w, conv_b3, w_r, b_r3, w_i, b_i3, lam3),
        out_specs=[pl.BlockSpec((r, cw), lambda bi, cb, ti: (bi * nt + ti, cb)), last8, last8],
        out_shape=[jax.ShapeDtypeStruct((m, width), BF16),
                   jax.ShapeDtypeStruct((b, SUBLANES, width), F32),
                   jax.ShapeDtypeStruct((b, SUBLANES, width), F32)],
        scratch=[pltpu.VMEM((SUBLANES, cw), F32)],
        sem=("arbitrary", "arbitrary", "arbitrary"))


def _lru_sample(proj, ux_col, l, lru_w, row0, h0_rep, buf_fr, yb_prev):
    conv_w, conv_b3, w_r, b_r3, w_i, b_i3, lam3 = lru_w
    width = lam3.shape[-1]
    lb = w_r.shape[-1]
    ncw = conv_w.shape[1]
    ms = h0_rep.shape[1]
    cw = _pick(width, (1024, 512, 256, 128))
    r = _pick(ms, (256, 128, 64, 32, 16, 8))
    assert row0 % r == 0
    rb0 = row0 // r
    specs = _lru_specs(l, r, cw, ncw, lb, ux_col // cw, (ux_col + width) // cw,
                       lambda ri, cb: rb0 + ri, lambda ri, cb: cb)
    st = pl.BlockSpec((None, r, cw), lambda ri, cb: (l, ri, cb))
    f32rows = pl.BlockSpec((r, cw), lambda ri, cb: (ri, cb))
    return _pcall(
        _lru_sample_kernel, name="lru_sample", grid=(ms // r, width // cw),
        in_specs=specs + [st, st],
        args=(proj, proj, conv_w, conv_b3, w_r, b_r3, w_i, b_i3, lam3, h0_rep, buf_fr),
        out_specs=[pl.BlockSpec((r, cw), lambda ri, cb: (rb0 + ri, cb)), f32rows, f32rows],
        out_shape=[jax.ShapeDtypeStruct(yb_prev.shape, BF16),
                   jax.ShapeDtypeStruct((ms, width), F32), jax.ShapeDtypeStruct((ms, width), F32)],
        prev=(yb_prev,),
        sem=("arbitrary", "arbitrary"))


def _pack_pair(hi, lo):
    hb = lax.bitcast_convert_type(hi.astype(jnp.bfloat16).astype(F32), jnp.uint32)
    lb = lax.bitcast_convert_type(lo.astype(jnp.bfloat16).astype(F32), jnp.uint32)
    return hb | (lb >> 16)


def _unpack_pair(p):
    hi = lax.bitcast_convert_type(p & jnp.uint32(0xFFFF0000), F32)
    lo = lax.bitcast_convert_type(p << 16, F32)
    return hi, lo


def _router_kernel(x_ref, g_ref, wr_ref, o_ref, hp_ref, *, ne):
    hf = _rms(x_ref[...], g_ref[...])
    half = hf.shape[1] // 2
    hp_ref[...] = _pack_pair(hf[:, :half], hf[:, half:])
    h = hf.astype(BF16)
    logits = jnp.dot(h, wr_ref[...].astype(BF16), preferred_element_type=F32)
    lane = lax.broadcasted_iota(jnp.int32, logits.shape, 1).astype(F32)
    neg = F32(-jnp.inf)
    big = F32(LANES)
    l1 = jnp.where(lane < ne, logits, neg)
    m1 = jnp.max(l1, axis=-1, keepdims=True)
    i1 = jnp.min(jnp.where(l1 == m1, lane, big), axis=-1, keepdims=True)
    l2 = jnp.where(lane == i1, neg, l1)
    m2 = jnp.max(l2, axis=-1, keepdims=True)
    i2 = jnp.min(jnp.where(l2 == m2, lane, big), axis=-1, keepdims=True)
    e2 = jnp.exp(m2 - m1)
    den = 1.0 + e2
    o_ref[...] = (jnp.where(lane == 0.0, i1, 0.0) + jnp.where(lane == 1.0, i2, 0.0)
                  + jnp.where(lane == 2.0, 1.0 / den, 0.0) + jnp.where(lane == 3.0, e2 / den, 0.0))


def _router(x, g3, l, wr):
    m, d = x.shape
    ne = wr.shape[-1]
    assert TOP_K == 2 and ne <= LANES
    wr_pad = jnp.pad(wr, ((0, 0), (0, LANES - ne)))
    tm = _pick(m, (512, 256, 128, 64, 32, 16, 8))
    return _pcall(
        functools.partial(_router_kernel, ne=ne), name="router", grid=(m // tm,),
        in_specs=[pl.BlockSpec((tm, d), lambda i: (i, 0)), _vspec(l, d),
                  pl.BlockSpec((d, LANES), lambda i: (0, 0))],
        args=(x, g3, wr_pad),
        out_specs=[pl.BlockSpec((tm, LANES), lambda i: (i, 0)),
                   pl.BlockSpec((tm, d // 2), lambda i: (i, 0))],
        out_shape=[jax.ShapeDtypeStruct((m, LANES), F32),
                   jax.ShapeDtypeStruct((m, d // 2), jnp.uint32)],
        sem=("arbitrary",))


def _dispatch_tables(rout, ne, tg):
    m = rout.shape[0]
    na = TOP_K * m
    eid = rout[:, :TOP_K].astype(jnp.int32).T.reshape(na)
    onehot = (eid[:, None] == jnp.arange(ne, dtype=jnp.int32)[None, :]).astype(jnp.int32)
    cnt = jnp.sum(onehot, axis=0)
    rank = jnp.sum((jnp.cumsum(onehot, axis=0) - onehot) * onehot, axis=1)
    padded = ((cnt + tg - 1) // tg) * tg
    gend = jnp.cumsum(padded)
    gstart = gend - padded
    pos = gstart[eid] + rank
    p_rows = ((na + tg - 1) // tg + ne) * tg
    n_tiles = p_rows // tg
    tok = jnp.zeros((p_rows,), jnp.int32).at[pos].set(jnp.arange(na, dtype=jnp.int32) % m)
    n_used = (gend[-1] // tg).astype(jnp.int32)
    tstart = jnp.arange(n_tiles, dtype=jnp.int32) * tg
    tstart = jnp.minimum(tstart, (n_used - 1) * tg)
    te = jnp.minimum(jnp.searchsorted(gend, tstart, side="right"), ne - 1).astype(jnp.int32)
    return pos, tok, te, n_used.reshape(1), p_rows


DMA_UNROLL = 8


def _row_gather(src_hbm, dst, idx_of, sem):
    rows = dst.shape[0]
    assert rows % DMA_UNROLL == 0

    def row_copy(r, t):
        return pltpu.make_async_copy(src_hbm.at[pl.ds(t, 1), :], dst.at[pl.ds(r, 1), :], sem)

    def issue(c, carry):
        for u in range(DMA_UNROLL):
            r = c * DMA_UNROLL + u
            row_copy(r, idx_of(r)).start()
        return carry

    def drain(c, carry):
        for u in range(DMA_UNROLL):
            row_copy(c * DMA_UNROLL + u, 0).wait()
        return carry

    start = lambda: lax.fori_loop(0, rows // DMA_UNROLL, issue, 0)
    wait = lambda: lax.fori_loop(0, rows // DMA_UNROLL, drain, 0)
    return start, wait


def _moe_gather_kernel(tok_ref, hp_ref, o_ref):
    tr, d = o_ref.shape
    half = d // 2

    group = 2 * SUBLANES

    def body(g, carry):
        r0 = pl.multiple_of(g * group, group)
        rows = [hp_ref[pl.ds(tok_ref[0, r0 + u], 1), :] for u in range(group)]
        hi, lo = _unpack_pair(jnp.concatenate(rows, axis=0))
        o_ref[pl.ds(r0, group), pl.ds(0, half)] = hi.astype(o_ref.dtype)
        o_ref[pl.ds(r0, group), pl.ds(half, half)] = lo.astype(o_ref.dtype)
        return carry

    lax.fori_loop(0, tr // group, body, 0)


def _moe_gather(hp, tok, p_rows):
    m, half = hp.shape
    tr = _pick(p_rows, (GATHER_ROWS, 128, 64, 32, 16))
    nt = p_rows // tr
    return _pcall(
        _moe_gather_kernel, name="moe_gather", grid=(nt,),
        in_specs=[pl.BlockSpec((None, 1, tr), lambda i: (i, 0, 0), memory_space=pltpu.SMEM),
                  pl.BlockSpec((m, half), lambda i: (0, 0), pipeline_mode=pl.Buffered(1))],
        args=(tok.reshape(nt, 1, tr), hp),
        out_specs=[pl.BlockSpec((tr, 2 * half), lambda i: (i, 0))],
        out_shape=[jax.ShapeDtypeStruct((p_rows, 2 * half), BF16)],
        sem=("arbitrary",))[0]


def _tile_changed(te_ref):
    ti = pl.program_id(1)
    return jnp.logical_or(ti == 0, te_ref[ti] != te_ref[jnp.maximum(ti - 1, 0)])


def _gmm_swiglu_kernel(te_ref, nu_ref, x_ref, wg_ref, wu_ref, o_ref, wgb_ref, wub_ref):
    @pl.when(_tile_changed(te_ref))
    def _():
        _cast_w(wg_ref, wgb_ref)
        _cast_w(wu_ref, wub_ref)

    used = pl.program_id(1) < nu_ref[0]

    @pl.when(used)
    def _():
        _swiglu_rows(x_ref, wgb_ref, wub_ref, o_ref)

    @pl.when(jnp.logical_not(used))
    def _():
        o_ref[...] = jnp.zeros_like(o_ref)


def _gmm_down_kernel(te_ref, nu_ref, x_ref, w_ref, o_ref, wb_ref):
    @pl.when(_tile_changed(te_ref))
    def _():
        _cast_w(w_ref, wb_ref)

    used = pl.program_id(1) < nu_ref[0]

    @pl.when(used)
    def _():
        tg, hw = o_ref.shape
        nchunk = 4 if tg % (4 * 16) == 0 else 1
        rc = tg // nchunk
        for c in range(nchunk):
            rs = slice(c * rc, (c + 1) * rc)
            y = jnp.dot(x_ref[rs, :], wb_ref[...], preferred_element_type=F32)
            o_ref[rs, :] = _pack_pair(y[:, :hw], y[:, hw:])

    @pl.when(jnp.logical_not(used))
    def _():
        o_ref[...] = jnp.zeros_like(o_ref)


def _gmm_specs(jm, tg, k, tn, j0=0):
    xrow = lambda j, ti, te, nu: (jnp.minimum(ti, nu[0] - 1), 0)
    wsp = pl.BlockSpec((None, None, k, tn), lambda j, ti, te, nu: (jm, te[ti], 0, j0 + j))
    return pl.BlockSpec((tg, k), xrow), wsp, pl.BlockSpec((tg, tn), lambda j, ti, te, nu: (ti, j0 + j))


def _gmm_swiglu(xg, wg, wu, jm, te, n_used, tg, tns):
    p, k = xg.shape
    n = wg.shape[-1]
    assert sum(tn * cnt for tn, cnt in tns) == n
    out, col = None, 0
    for tn, cnt in tns:
        assert col % tn == 0
        xs, ws, os_ = _gmm_specs(jm, tg, k, tn, col // tn)
        out = _pcall(
            _gmm_swiglu_kernel, name="gmm_swiglu", grid=(cnt, p // tg), num_scalar_prefetch=2,
            in_specs=[None, None, xs, ws, ws], args=(te, n_used, xg, wg, wu),
            out_specs=[os_], out_shape=[jax.ShapeDtypeStruct((p, n), BF16)],
            scratch=[pltpu.VMEM((k, tn), BF16), pltpu.VMEM((k, tn), BF16)],
            prev=(out,), sem=("arbitrary", "arbitrary"))[0]
        col += tn * cnt
    return out


def _col_tiling(n, wide, narrow):
    nw = n // wide
    rest = n - nw * wide
    assert rest % narrow == 0 and (nw * wide) % narrow == 0
    return tuple(t for t in ((wide, nw), (narrow, rest // narrow)) if t[1])


def _gmm_down(ug, wd, jm, te, n_used, tg, tn):
    p, k = ug.shape
    n = wd.shape[-1]
    xs, ws, _ = _gmm_specs(jm, tg, k, tn)
    return _pcall(
        _gmm_down_kernel, name="gmm_down", grid=(n // tn, p // tg), num_scalar_prefetch=2,
        in_specs=[None, None, xs, ws], args=(te, n_used, ug, wd),
        out_specs=[pl.BlockSpec((tg, tn // 2), lambda j, ti, te, nu: (ti, j))],
        out_shape=[jax.ShapeDtypeStruct((p, n // 2), jnp.uint32)],
        scratch=[pltpu.VMEM((k, tn), BF16)],
        sem=("arbitrary", "arbitrary"))[0]


def _moe_combine_kernel(pos_ref, nxt_ref, x_ref, rw_ref, g_ref, yg_hbm, *rest, tn, np_tiles):
    outs, (buf, sem) = rest[:-2], rest[-2:]
    tc = x_ref.shape[0]
    i = pl.program_id(0)
    slot = lax.rem(i, 2)

    def gathers(idx_ref, s):
        return [_row_gather(yg_hbm, buf.at[s, k], lambda r, k=k: idx_ref[0, k * tc + r], sem.at[s])
                for k in range(TOP_K)]

    @pl.when(i == 0)
    def _():
        for start, _ in gathers(pos_ref, 0):
            start()

    @pl.when(i + 1 < pl.num_programs(0))
    def _():
        for start, _ in gathers(nxt_ref, 1 - slot):
            start()

    for _, wait in gathers(pos_ref, slot):
        wait()
    rows = buf[slot]

    def expand(p):
        hw = tn // 2
        parts = []
        for jj in range(p.shape[1] // hw):
            parts.extend(_unpack_pair(p[:, jj * hw:(jj + 1) * hw]))
        return jnp.concatenate(parts, axis=-1)

    rw = rw_ref[...]
    y = rw[:, TOP_K:TOP_K + 1] * expand(rows[0])
    for k in range(1, TOP_K):
        y = y + rw[:, TOP_K + k:TOP_K + k + 1] * expand(rows[k])
    xn = x_ref[...] + y
    if np_tiles is None:
        outs[0][...] = xn
    else:
        yn = _rms(xn, g_ref[...])

        @pl.when(i < np_tiles)
        def _():
            outs[0][...] = yn

        @pl.when(i >= np_tiles)
        def _():
            outs[1][...] = yn


def _moe_combine(x, rout, pos, ygp, tn, g3=None, mp=None):
    m, d = x.shape
    tc = _pick(m, (COMBINE_ROWS, 64, 32, 16, 8))
    nt = m // tc
    pos_t = pos.reshape(TOP_K, nt, tc).transpose(1, 0, 2).reshape(nt, 1, TOP_K * tc)
    if g3 is None:
        g3 = jnp.ones((1, 1, d), F32)
        np_tiles = None
        out_specs = [pl.BlockSpec((tc, d), lambda i: (i, 0))]
        out_shape = [jax.ShapeDtypeStruct((m, d), F32)]
    else:
        assert mp % tc == 0
        np_tiles = mp // tc
        out_specs = [pl.BlockSpec((tc, d), lambda i: (jnp.minimum(i, np_tiles - 1), 0)),
                     pl.BlockSpec((tc, d), lambda i: (jnp.maximum(i - np_tiles, 0), 0))]
        out_shape = [jax.ShapeDtypeStruct((mp, d), F32), jax.ShapeDtypeStruct((m - mp, d), F32)]
    return _pcall(
        functools.partial(_moe_combine_kernel, tn=tn, np_tiles=np_tiles), name="moe_combine", grid=(nt,),
        in_specs=[pl.BlockSpec((None, 1, TOP_K * tc), lambda i: (i, 0, 0), memory_space=pltpu.SMEM),
                  pl.BlockSpec((None, 1, TOP_K * tc), lambda i: (jnp.minimum(i + 1, nt - 1), 0, 0),
                               memory_space=pltpu.SMEM),
                  pl.BlockSpec((tc, d), lambda i: (i, 0)),
                  pl.BlockSpec((tc, LANES), lambda i: (i, 0)),
                  _vspec(0, d),
                  pl.BlockSpec(memory_space=pl.ANY)],
        args=(pos_t, pos_t, x, rout, g3, ygp),
        out_specs=out_specs, out_shape=out_shape,
        scratch=[pltpu.VMEM((2, TOP_K, tc, d // 2), jnp.uint32), pltpu.SemaphoreType.DMA((2,))],
        sem=("arbitrary",))


def kernel(x_prompt, x_sample, state_ret, state_lru, state_conv, norm_mix, w_in, ret_gn, w_ret_o, conv_w, conv_b, w_rgate, b_rgate, w_igate, b_igate, lru_lambda, w_lru_o, w_out, norm_ffn, ffn_w_gate, ffn_w_up, ffn_w_down, moe_router, moe_w_gate, moe_w_up, moe_w_down, norm_final):
    bp, tp, d = x_prompt.shape
    bs, ts, _ = x_sample.shape
    depth, _, nh, dk, dv = state_ret.shape
    width = state_lru.shape[-1]
    ncw = conv_w.shape[1]
    ne = moe_router.shape[-1]
    assert dk == dv and nh * dk == width == d and ts == SUBLANES and ncw - 1 <= SUBLANES
    mp, ms = bp * tp, bs * ts
    m = mp + ms
    g_col = 2 * nh * dk + nh * dv
    ux_col = g_col + nh * dv
    ga_col = ux_col + 2 * width

    tm = _pick(m, (1024, 512, 256, 128, 64, 32, 16, 8))
    tm_big = _pick(m, (1536, 1024, 512, 256, 128, 64, 32, 16, 8))
    tm_half = _pick(m, (512, 256, 128, 64, 32, 16, 8))
    tn_of = lambda n: _pick(n, (1024, 512, 256, 128))

    vec3 = lambda a: a.reshape(a.shape[0], 1, a.shape[-1])
    norm_mix3, norm_ffn3, ret_gn3 = vec3(norm_mix), vec3(norm_ffn), vec3(ret_gn)
    norm_final3 = norm_final.reshape(1, 1, d)
    lru_w = (conv_w, vec3(conv_b), w_rgate, vec3(b_rgate), w_igate, vec3(b_igate), vec3(lru_lambda))
    h0_rep = jnp.repeat(state_lru, ts, axis=1)
    buf_fr = jnp.pad(state_conv, ((0, 0), (0, 0), (ts - (ncw - 1), 0), (0, 0))).reshape(depth, ms, width)

    cos_p, sin_p = _rope_tables(tp, dk // 2, 0)
    cos_s, sin_s = _rope_tables(ts, dk // 2, PAST_LEN)
    cos = jnp.concatenate([jnp.tile(cos_p, (bp, 1)), jnp.tile(cos_s, (bs, 1))], axis=0)
    sin = jnp.concatenate([jnp.tile(sin_p, (bp, 1)), jnp.tile(sin_s, (bs, 1))], axis=0)
    tn_in = _pick(nh * dk, (1024, 512, 256))

    x = jnp.concatenate([x_prompt.reshape(mp, d), x_sample.reshape(ms, d)], axis=0)
    ret_p = ret_s = y_p = y_s = None
    lrus_p, lrus_s, convs_p, convs_s = [], [], [], []
    for l in range(depth):
        h = _rmsnorm(x, norm_mix3, l, BF16)
        proj = _mm_inproj(h, w_in, (l,), cos, sin, 2 * nh * dk, dk, (g_col, ux_col),
                          (ux_col + width, ga_col), tm_big, tn_in)

        ya, ret_p = _ret_prompt(proj, ret_gn3, l, depth, bp, tp, nh, dk, dv, ret_p)
        ya, ret_s = _ret_sample(proj, ret_gn3, state_ret, l, mp, bs, ts, nh, dk, dv, ya, ret_s)
        yb, hl_p, ul_p = _lru_prompt(proj, ux_col, l, lru_w, bp, tp)
        yb, hl_s, us_s = _lru_sample(proj, ux_col, l, lru_w, mp, h0_rep, buf_fr, yb)
        lrus_p.append(hl_p[:, -1])
        lrus_s.append(hl_s.reshape(bs, ts, width)[:, -1])
        convs_p.append(ul_p[:, SUBLANES - (ncw - 1):])
        convs_s.append(us_s.reshape(bs, ts, width)[:, ts - (ncw - 1):])

        z = _mm_merge(ya, yb, w_ret_o, w_lru_o, (l,), proj, ga_col, ga_col + d, tm, tn_of(d) // 2)
        x = _mm_resid(z, w_out, (l,), x, tm, tn_of(d))
        j = l // 2
        if l % 2 == 0:
            h2 = _rmsnorm(x, norm_ffn3, l, BF16)
            ff = ffn_w_gate.shape[-1]
            u = _mm_swiglu(h2, ffn_w_gate, ffn_w_up, (j,), tm_big, _pick(ff, (512, 256, 128)))
            x = _mm_resid(u, ffn_w_down, (j,), x, tm_half, _pick(d, (512, 256, 128)))
        else:
            fe = moe_w_gate.shape[-1]
            tg = _pick(TOP_K * m, (MOE_TILE, 256, 128, 64, 32, 16))
            tn_d = _pick(d, (1024, 512, 256))
            rout, hp = _router(x, norm_ffn3, l, moe_router[j])
            pos, tok, te, n_used, p_rows = _dispatch_tables(rout, ne, tg)
            xg = _moe_gather(hp, tok, p_rows)
            ug = _gmm_swiglu(xg, moe_w_gate, moe_w_up, j, te, n_used, tg,
                             _col_tiling(fe, 512, _pick(fe, (256, 128))))
            ygp = _gmm_down(ug, moe_w_down, j, te, n_used, tg, tn_d)
            if l == depth - 1:
                y_p, y_s = _moe_combine(x, rout, pos, ygp, tn_d, norm_final3, mp)
            else:
                x = _moe_combine(x, rout, pos, ygp, tn_d)[0]
    if y_p is None:
        y_p = _rmsnorm(x, norm_final3, 0, F32, 0, mp)
        y_s = _rmsnorm(x, norm_final3, 0, F32, mp, ms)
    return (y_p.reshape(bp, tp, d), y_s.reshape(bs, ts, d),
            ret_p, jnp.stack(lrus_p), jnp.stack(convs_p),
            ret_s, jnp.stack(lrus_s), jnp.stack(convs_s))
```

```python
import functools

import jax
import jax.numpy as jnp
from jax import lax
from jax.experimental import pallas as pl
from jax.experimental.pallas import tpu as pltpu

F32 = jnp.float32
BF16 = jnp.bfloat16

ROPE_BASE = 10000.0
LRU_C = 8.0
EPS = 1e-6
RET_CHUNK = 128
PAST_LEN = 16384
TOP_K = 2
SUBLANES = 8
LANES = 128
VMEM_LIMIT = 56 * 1024 * 1024
MOE_TILE = 1024
GATHER_ROWS = 256
COMBINE_ROWS = 128


def _pick(n, prefs):
    for p in prefs:
        if n % p == 0:
            return p
    return n


def _params(*sem):
    return pltpu.CompilerParams(dimension_semantics=sem, vmem_limit_bytes=VMEM_LIMIT)


def _pcall(body, *, name, grid, in_specs, args, out_specs, out_shape, sem, scratch=(), prev=(),
           num_scalar_prefetch=0):
    n_in = len(args)
    prev = tuple(prev) + (None,) * (len(out_shape) - len(prev))
    extra = [(oi, p) for oi, p in enumerate(prev) if p is not None]
    aliases = {n_in + e: oi for e, (oi, _) in enumerate(extra)}
    if extra:
        inner = body

        def body(*refs):
            return inner(*refs[:n_in], *refs[n_in + len(extra):])

    in_specs = list(in_specs) + [pl.BlockSpec(memory_space=pl.ANY)] * len(extra)
    if num_scalar_prefetch:
        grid_spec = pltpu.PrefetchScalarGridSpec(
            num_scalar_prefetch=num_scalar_prefetch, grid=grid, in_specs=in_specs[num_scalar_prefetch:],
            out_specs=tuple(out_specs), scratch_shapes=list(scratch))
        call = pl.pallas_call(body, grid_spec=grid_spec, out_shape=tuple(out_shape),
                              input_output_aliases=aliases, compiler_params=_params(*sem), name=name)
    else:
        call = pl.pallas_call(body, grid=grid, in_specs=in_specs, out_specs=tuple(out_specs),
                              out_shape=tuple(out_shape), scratch_shapes=list(scratch),
                              input_output_aliases=aliases, compiler_params=_params(*sem), name=name)
    return call(*args, *[p for _, p in extra])


def _wspec(pre, k, tn):
    return pl.BlockSpec((None,) * len(pre) + (k, tn), lambda j, i: pre + (0, j))


def _vspec(l, n):
    return pl.BlockSpec((None, 1, n), lambda *_: (l, 0, 0))


def _rms(x, g):
    return (x * lax.rsqrt(jnp.mean(x * x, axis=-1, keepdims=True) + EPS)) * g


def _rmsnorm_kernel(x_ref, g_ref, o_ref):
    o_ref[...] = _rms(x_ref[...], g_ref[...]).astype(o_ref.dtype)


def _rmsnorm(x, g3, l, out_dtype, row0=0, rows=None):
    d = x.shape[1]
    rows = x.shape[0] if rows is None else rows
    tm = _pick(rows, (512, 256, 128, 64, 32, 16, 8))
    assert row0 % tm == 0
    rb0 = row0 // tm
    return _pcall(
        _rmsnorm_kernel, name="rmsnorm", grid=(rows // tm,),
        in_specs=[pl.BlockSpec((tm, d), lambda i: (rb0 + i, 0)), _vspec(l, d)],
        args=(x, g3),
        out_specs=[pl.BlockSpec((tm, d), lambda i: (i, 0))],
        out_shape=[jax.ShapeDtypeStruct((rows, d), out_dtype)],
        sem=("arbitrary",))[0]


def _cast_w(w_ref, wb_ref):
    k = w_ref.shape[0]
    ck = _pick(k, (256, 128, 64, 32, 16))

    def body(c, carry):
        r = pl.multiple_of(c * ck, ck)
        wb_ref[pl.ds(r, ck), :] = w_ref[pl.ds(r, ck), :].astype(BF16)
        return carry

    lax.fori_loop(0, k // ck, body, 0)


def _mm_inproj_kernel(x_ref, w_ref, cos_ref, sin_ref, o_ref, wb_ref, *, nrot, dk, silu_tiles, gelu_tiles):
    j = pl.program_id(0)

    @pl.when(pl.program_id(1) == 0)
    def _():
        _cast_w(w_ref, wb_ref)

    tm, tn = o_ref.shape
    nchunk = 4 if tm % (4 * 16) == 0 else 1

    def row_chunks(epilogue):
        rc = tm // nchunk
        for c in range(nchunk):
            rs = slice(c * rc, (c + 1) * rc)
            epilogue(jnp.dot(x_ref[rs, :], wb_ref[...], preferred_element_type=F32), rs)

    def store(fn):
        def epilogue(acc, rs):
            o_ref[rs, :] = fn(acc).astype(o_ref.dtype)
        return epilogue

    in_tiles = lambda t: jnp.logical_and(j >= t[0], j < t[1])
    is_silu, is_gelu = in_tiles(silu_tiles), in_tiles(gelu_tiles)

    @pl.when(jnp.logical_and(j >= nrot, jnp.logical_not(jnp.logical_or(is_silu, is_gelu))))
    def _():
        row_chunks(store(lambda acc: acc))

    @pl.when(is_silu)
    def _():
        row_chunks(store(jax.nn.silu))

    @pl.when(is_gelu)
    def _():
        row_chunks(store(jax.nn.gelu))

    @pl.when(j < nrot)
    def _():
        scale = jnp.where(j >= nrot // 2, F32(dk ** -0.5), F32(1.0))
        half = dk // 2

        def rotary(acc, rs):
            cos = cos_ref[rs, :] * scale
            sin = sin_ref[rs, :] * scale
            for h in range(tn // dk):
                x1 = acc[:, h * dk:h * dk + half]
                x2 = acc[:, h * dk + half:(h + 1) * dk]
                o_ref[rs, h * dk:h * dk + half] = (x1 * cos - x2 * sin).astype(o_ref.dtype)
                o_ref[rs, h * dk + half:(h + 1) * dk] = (x1 * sin + x2 * cos).astype(o_ref.dtype)

        row_chunks(rotary)


def _mm_inproj(x, w, pre, cos, sin, qk_cols, dk, silu_cols, gelu_cols, tm, tn):
    m, k = x.shape
    n = w.shape[-1]
    half = cos.shape[1]
    assert qk_cols % (2 * tn) == 0 and tn % dk == 0
    assert all(c % tn == 0 for c in silu_cols + gelu_cols)
    tiles = lambda cols: (cols[0] // tn, cols[1] // tn)
    return _pcall(
        functools.partial(_mm_inproj_kernel, nrot=qk_cols // tn, dk=dk, silu_tiles=tiles(silu_cols),
                          gelu_tiles=tiles(gelu_cols)), name="mm_inproj",
        grid=(n // tn, m // tm),
        in_specs=[pl.BlockSpec((tm, k), lambda j, i: (i, 0)), _wspec(pre, k, tn),
                  pl.BlockSpec((tm, half), lambda j, i: (i, 0)),
                  pl.BlockSpec((tm, half), lambda j, i: (i, 0))],
        args=(x, w, cos, sin),
        out_specs=[pl.BlockSpec((tm, tn), lambda j, i: (i, j))],
        out_shape=[jax.ShapeDtypeStruct((m, n), BF16)],
        scratch=[pltpu.VMEM((k, tn), BF16)],
        sem=("arbitrary", "arbitrary"))[0]


def _mm_resid_kernel(x_ref, w_ref, r_ref, o_ref, wb_ref):
    @pl.when(pl.program_id(1) == 0)
    def _():
        _cast_w(w_ref, wb_ref)

    o_ref[...] = r_ref[...] + jnp.dot(x_ref[...], wb_ref[...], preferred_element_type=F32)


def _mm_resid(x, w, pre, r, tm, tn):
    m, k = x.shape
    n = w.shape[-1]
    return _pcall(
        _mm_resid_kernel, name="mm_resid", grid=(n // tn, m // tm),
        in_specs=[pl.BlockSpec((tm, k), lambda j, i: (i, 0)), _wspec(pre, k, tn),
                  pl.BlockSpec((tm, tn), lambda j, i: (i, j))],
        args=(x, w, r),
        out_specs=[pl.BlockSpec((tm, tn), lambda j, i: (i, j))],
        out_shape=[jax.ShapeDtypeStruct((m, n), F32)],
        scratch=[pltpu.VMEM((k, tn), BF16)],
        sem=("arbitrary", "arbitrary"))[0]


def _row_chunks(rows, fn, nchunk=4):
    n = nchunk if rows % (nchunk * 16) == 0 else 1
    rc = rows // n
    for c in range(n):
        fn(slice(c * rc, (c + 1) * rc))


def _swiglu_rows(x_ref, wgb_ref, wub_ref, o_ref):
    def chunk(rs):
        x = x_ref[rs, :]
        g = jnp.dot(x, wgb_ref[...], preferred_element_type=F32)
        u = jnp.dot(x, wub_ref[...], preferred_element_type=F32)
        o_ref[rs, :] = (jax.nn.silu(g) * u).astype(o_ref.dtype)

    _row_chunks(o_ref.shape[0], chunk)


def _mm_swiglu_kernel(x_ref, wg_ref, wu_ref, o_ref, wgb_ref, wub_ref):
    @pl.when(pl.program_id(1) == 0)
    def _():
        _cast_w(wg_ref, wgb_ref)
        _cast_w(wu_ref, wub_ref)

    _swiglu_rows(x_ref, wgb_ref, wub_ref, o_ref)


def _mm_swiglu(x, wg, wu, pre, tm, tn):
    m, k = x.shape
    n = wg.shape[-1]
    return _pcall(
        _mm_swiglu_kernel, name="mm_swiglu", grid=(n // tn, m // tm),
        in_specs=[pl.BlockSpec((tm, k), lambda j, i: (i, 0)), _wspec(pre, k, tn), _wspec(pre, k, tn)],
        args=(x, wg, wu),
        out_specs=[pl.BlockSpec((tm, tn), lambda j, i: (i, j))],
        out_shape=[jax.ShapeDtypeStruct((m, n), BF16)],
        scratch=[pltpu.VMEM((k, tn), BF16), pltpu.VMEM((k, tn), BF16)],
        sem=("arbitrary", "arbitrary"))[0]


def _mm_merge_kernel(a_ref, b_ref, wa_ref, wb_ref, ga_ref, gb_ref, o_ref, wab_ref, wbb_ref):
    @pl.when(pl.program_id(1) == 0)
    def _():
        _cast_w(wa_ref, wab_ref)
        _cast_w(wb_ref, wbb_ref)

    def chunk(rs):
        ya = jnp.dot(a_ref[rs, :], wab_ref[...], preferred_element_type=F32)
        yb = jnp.dot(b_ref[rs, :], wbb_ref[...], preferred_element_type=F32)
        ga = ga_ref[rs, :].astype(F32)
        gb = gb_ref[rs, :].astype(F32)
        o_ref[rs, :] = (jax.nn.sigmoid(ga) * ya + jax.nn.sigmoid(gb) * yb).astype(o_ref.dtype)

    _row_chunks(o_ref.shape[0], chunk)


def _mm_merge(a, b, wa, wb, pre, proj, ga_col, gb_col, tm, tn):
    m, k = a.shape
    n = wa.shape[-1]
    ga_blk, gb_blk = ga_col // tn, gb_col // tn
    return _pcall(
        _mm_merge_kernel, name="mm_merge", grid=(n // tn, m // tm),
        in_specs=[pl.BlockSpec((tm, k), lambda j, i: (i, 0)),
                  pl.BlockSpec((tm, k), lambda j, i: (i, 0)),
                  _wspec(pre, k, tn), _wspec(pre, k, tn),
                  pl.BlockSpec((tm, tn), lambda j, i: (i, ga_blk + j)),
                  pl.BlockSpec((tm, tn), lambda j, i: (i, gb_blk + j))],
        args=(a, b, wa, wb, proj, proj),
        out_specs=[pl.BlockSpec((tm, tn), lambda j, i: (i, j))],
        out_shape=[jax.ShapeDtypeStruct((m, n), BF16)],
        scratch=[pltpu.VMEM((k, tn), BF16), pltpu.VMEM((k, tn), BF16)],
        sem=("arbitrary", "arbitrary"))[0]


def _rope_kernel(inv_ref, cos_ref, sin_ref, *, pos0):
    t, half = cos_ref.shape
    pos = lax.broadcasted_iota(jnp.int32, (t, half), 0).astype(F32) + F32(pos0)
    ang = pos * inv_ref[...]
    cos_ref[...] = jnp.cos(ang)
    sin_ref[...] = jnp.sin(ang)


def _rope_tables(t, half, pos0):
    inv = ROPE_BASE ** (-jnp.arange(half, dtype=F32) / half)
    return pl.pallas_call(
        functools.partial(_rope_kernel, pos0=pos0),
        out_shape=(jax.ShapeDtypeStruct((t, half), F32), jax.ShapeDtypeStruct((t, half), F32)),
        name="rope_tables",
    )(inv.reshape(1, half))


def _decay_tables(c, h, dk, dv):
    log_g = jnp.log1p(-jnp.exp2(-5.0 - jnp.arange(h, dtype=F32)))
    idx = jnp.arange(c)
    rel = idx[:, None] - idx[None, :]
    dmask = jnp.where(rel[None] >= 0,
                      jnp.exp(jnp.maximum(rel, 0)[None].astype(F32) * log_g[:, None, None]), 0.0)
    xi = jnp.exp((idx + 1).astype(F32)[None, :] * log_g[:, None])
    zeta = jnp.exp((c - 1 - idx).astype(F32)[None, :] * log_g[:, None])
    g_c = jnp.exp(c * log_g)
    xi_t = jnp.broadcast_to(xi[:, :, None], (h, c, dv))
    zeta_t = jnp.broadcast_to(zeta[:, :, None], (h, c, dk))
    return dmask, xi_t, zeta_t, g_c


def _ret_head(qb, kb, v, g, dm, xi, zt, gn, s, gc):
    scores = lax.dot_general(qb, kb, (((1,), (1,)), ((), ())), preferred_element_type=F32)
    intra = jnp.dot((scores * dm).astype(BF16), v, preferred_element_type=F32)
    cross = jnp.dot(qb, s.astype(BF16), preferred_element_type=F32) * xi
    kz = (kb.astype(F32) * zt).astype(BF16)
    upd = lax.dot_general(kz, v, (((0,), (0,)), ((), ())), preferred_element_type=F32)
    s_new = gc * s + upd
    o = intra + cross
    mu = jnp.mean(o, axis=-1, keepdims=True)
    d = o - mu
    var = jnp.mean(d * d, axis=-1, keepdims=True)
    y = d * lax.rsqrt(var + EPS) * gn
    return (g * y).astype(BF16), s_new


def _ret_prompt_kernel(gc_ref, q_ref, k_ref, v_ref, g_ref, dm_ref, xi_ref, zt_ref,
                       gn_ref, o_ref, s_ref, *, nh, dk, dv):
    @pl.when(pl.program_id(1) == 0)
    def _():
        s_ref[...] = jnp.zeros_like(s_ref)

    c = dm_ref.shape[1]
    for h in range(nh):
        s = s_ref[h]
        for ci in range(q_ref.shape[0] // c):
            rs = slice(ci * c, (ci + 1) * c)
            out, s = _ret_head(
                q_ref[rs, h * dk:(h + 1) * dk], k_ref[rs, h * dk:(h + 1) * dk],
                v_ref[rs, h * dv:(h + 1) * dv], g_ref[rs, h * dv:(h + 1) * dv].astype(F32),
                dm_ref[h], xi_ref[h], zt_ref[h], gn_ref[:, h * dv:(h + 1) * dv], s, gc_ref[h])
            o_ref[rs, h * dv:(h + 1) * dv] = out
        s_ref[h] = s


def _ret_prompt(proj, gn3, l, depth, b, t, nh, dk, dv, s_prev):
    cl = RET_CHUNK if t % RET_CHUNK == 0 else t
    cps = _pick(t // cl, (4, 2, 1))
    c = cl * cps
    nc = t // c
    w = nh * dk
    m = proj.shape[0]
    dmask, xi_t, zeta_t, g_c = _decay_tables(cl, nh, dk, dv)
    row = lambda bi, ci: bi * nc + ci
    full3 = lambda bi, ci: (0, 0, 0)
    return _pcall(
        functools.partial(_ret_prompt_kernel, nh=nh, dk=dk, dv=dv), name="ret_prompt", grid=(b, nc),
        in_specs=[pl.BlockSpec(memory_space=pltpu.SMEM),
                  pl.BlockSpec((c, w), lambda bi, ci: (row(bi, ci), 0)),
                  pl.BlockSpec((c, w), lambda bi, ci: (row(bi, ci), 1)),
                  pl.BlockSpec((c, w), lambda bi, ci: (row(bi, ci), 2)),
                  pl.BlockSpec((c, w), lambda bi, ci: (row(bi, ci), 3)),
                  pl.BlockSpec((nh, cl, cl), full3),
                  pl.BlockSpec((nh, cl, dv), full3),
                  pl.BlockSpec((nh, cl, dk), full3),
                  _vspec(l, nh * dv)],
        args=(g_c, proj, proj, proj, proj, dmask, xi_t, zeta_t, gn3),
        out_specs=[pl.BlockSpec((c, nh * dv), lambda bi, ci: (row(bi, ci), 0)),
                   pl.BlockSpec((None, None, nh, dk, dv), lambda bi, ci: (l, bi, 0, 0, 0))],
        out_shape=[jax.ShapeDtypeStruct((m, nh * dv), BF16),
                   jax.ShapeDtypeStruct((depth, b, nh, dk, dv), F32)],
        prev=(None, s_prev),
        sem=("arbitrary", "arbitrary"))


def _ret_sample_kernel(gc_ref, q_ref, k_ref, v_ref, g_ref, dm_ref, xi_ref, zt_ref,
                       gn_ref, s0_ref, o_ref, s_ref, *, nh, dk, dv, bb, ts):
    q = q_ref[...].astype(F32)
    k = k_ref[...].astype(F32)
    v = v_ref[...].astype(F32)
    g = g_ref[...].astype(F32)
    seqs = []
    for i in range(bb):
        r0, r1 = i * ts, (i + 1) * ts
        outs = []
        for h in range(nh):
            out, s_new = _ret_head(
                q[r0:r1, h * dk:(h + 1) * dk].astype(BF16), k[r0:r1, h * dk:(h + 1) * dk].astype(BF16),
                v[r0:r1, h * dv:(h + 1) * dv].astype(BF16), g[r0:r1, h * dv:(h + 1) * dv],
                dm_ref[h], xi_ref[h], zt_ref[h], gn_ref[:, h * dv:(h + 1) * dv],
                s0_ref[i, h], gc_ref[h])
            s_ref[i, h] = s_new
            outs.append(out.astype(F32))
        seqs.append(jnp.concatenate(outs, axis=-1))
    o_ref[...] = jnp.concatenate(seqs, axis=0).astype(o_ref.dtype)


def _ret_sample(proj, gn3, s0_all, l, row0, bs, ts, nh, dk, dv, ya_prev, s_prev):
    bb = _pick(bs, (4, 2, 1))
    rows = bb * ts
    w = nh * dk
    depth = s0_all.shape[0]
    dmask, xi_t, zeta_t, g_c = _decay_tables(ts, nh, dk, dv)
    assert row0 % rows == 0
    rb0 = row0 // rows
    full3 = lambda i: (0, 0, 0)
    return _pcall(
        functools.partial(_ret_sample_kernel, nh=nh, dk=dk, dv=dv, bb=bb, ts=ts), name="ret_sample",
        grid=(bs // bb,),
        in_specs=[pl.BlockSpec(memory_space=pltpu.SMEM),
                  pl.BlockSpec((rows, w), lambda i: (rb0 + i, 0)),
                  pl.BlockSpec((rows, w), lambda i: (rb0 + i, 1)),
                  pl.BlockSpec((rows, w), lambda i: (rb0 + i, 2)),
                  pl.BlockSpec((rows, w), lambda i: (rb0 + i, 3)),
                  pl.BlockSpec((nh, ts, ts), full3),
                  pl.BlockSpec((nh, ts, dv), full3),
                  pl.BlockSpec((nh, ts, dk), full3),
                  _vspec(l, nh * dv),
                  pl.BlockSpec((None, bb, nh, dk, dv), lambda i: (l, i, 0, 0, 0))],
        args=(g_c, proj, proj, proj, proj, dmask, xi_t, zeta_t, gn3, s0_all),
        out_specs=[pl.BlockSpec((rows, nh * dv), lambda i: (rb0 + i, 0)),
                   pl.BlockSpec((None, bb, nh, dk, dv), lambda i: (l, i, 0, 0, 0))],
        out_shape=[jax.ShapeDtypeStruct(ya_prev.shape, BF16),
                   jax.ShapeDtypeStruct((depth, bs, nh, dk, dv), F32)],
        prev=(ya_prev, s_prev),
        sem=("arbitrary",))


def _lru_gates(uc, wr_ref, br_ref, wi_ref, bi_ref, lam_ref):
    nb, lb, _ = wr_ref.shape
    ucb = uc.astype(BF16)
    rl, il = [], []
    for n in range(nb):
        xb = ucb[:, n * lb:(n + 1) * lb]
        rl.append(jnp.dot(xb, wr_ref[n].astype(BF16), preferred_element_type=F32))
        il.append(jnp.dot(xb, wi_ref[n].astype(BF16), preferred_element_type=F32))
    r = jax.nn.sigmoid(jnp.concatenate(rl, axis=-1) + br_ref[...])
    i = jax.nn.sigmoid(jnp.concatenate(il, axis=-1) + bi_ref[...])
    log_a = -LRU_C * r * jax.nn.softplus(-lam_ref[...])
    a = jnp.exp(log_a)
    z = -jnp.tanh(log_a) * (a * a + 1.0)
    bx = jnp.where(z == 0.0, 0.0, z * lax.rsqrt(z)) * (i * uc)
    return a, bx


def _blocks(x):
    return [x[lo:lo + SUBLANES, :] for lo in range(0, x.shape[0], SUBLANES)]


def _conv_blocks(u_blocks, prev_blocks, cw_ref, cb_ref, rows):
    ncw = cw_ref.shape[0]
    rolled = {}

    def rot(x, s):
        key = (id(x), s)
        if key not in rolled:
            rolled[key] = pltpu.roll(x, s, axis=0)
        return rolled[key]

    out = []
    for ub, pb in zip(u_blocks, prev_blocks):
        uc = cb_ref[...]
        for j in range(ncw):
            s = ncw - 1 - j
            term = ub if s == 0 else jnp.where(rows >= s, rot(ub, s), rot(pb, s))
            uc = uc + term * cw_ref[j:j + 1, :]
        out.append(uc)
    return jnp.concatenate(out, axis=0)


def _scan_block(a, bx, rows):
    d = 1
    while d < SUBLANES:
        m = rows >= d
        bx = bx + a * jnp.where(m, pltpu.roll(bx, d, axis=0), 0.0)
        a = a * jnp.where(m, pltpu.roll(a, d, axis=0), 1.0)
        d *= 2
    return a, bx


def _lru_prompt_kernel(ux_ref, uy_ref, cw_ref, cb_ref, wr_ref, br_ref, wi_ref, bi_ref, lam_ref,
                       o_ref, hl_ref, ul_ref, hc_ref):
    r, cw = ux_ref.shape

    @pl.when(pl.program_id(2) == 0)
    def _():
        hc_ref[...] = jnp.zeros_like(hc_ref)
        ul_ref[...] = jnp.zeros_like(ul_ref)

    rows = lax.broadcasted_iota(jnp.int32, (SUBLANES, cw), 0)
    ub = _blocks(ux_ref[...].astype(F32))
    uc = _conv_blocks(ub, [ul_ref[...]] + ub[:-1], cw_ref, cb_ref, rows)
    ul_ref[...] = ub[-1]
    a, bx = _lru_gates(uc, wr_ref, br_ref, wi_ref, bi_ref, lam_ref)
    carry = hc_ref[...]
    hs = []
    for ab, bb in zip(_blocks(a), _blocks(bx)):
        ab, bb = _scan_block(ab, bb, rows)
        hb = ab * carry + bb
        hs.append(hb)
        carry = jnp.broadcast_to(hb[SUBLANES - 1:SUBLANES, :], (SUBLANES, cw))
    hc_ref[...] = carry
    hl_ref[...] = hs[-1]
    h = jnp.concatenate(hs, axis=0)
    o_ref[...] = (h * uy_ref[...].astype(F32)).astype(o_ref.dtype)


def _lru_sample_kernel(ux_ref, uy_ref, cw_ref, cb_ref, wr_ref, br_ref, wi_ref, bi_ref, lam_ref,
                       h0_ref, buf_ref, o_ref, hl_ref, us_ref):
    r, cw = ux_ref.shape
    rows = lax.broadcasted_iota(jnp.int32, (SUBLANES, cw), 0)
    u = ux_ref[...].astype(F32)
    us_ref[...] = u
    uc = _conv_blocks(_blocks(u), _blocks(buf_ref[...]), cw_ref, cb_ref, rows)
    a, bx = _lru_gates(uc, wr_ref, br_ref, wi_ref, bi_ref, lam_ref)
    hs = []
    for ab, bb, h0 in zip(_blocks(a), _blocks(bx), _blocks(h0_ref[...])):
        ab, bb = _scan_block(ab, bb, rows)
        hs.append(ab * h0 + bb)
    h = jnp.concatenate(hs, axis=0)
    hl_ref[...] = h
    o_ref[...] = (h * uy_ref[...].astype(F32)).astype(o_ref.dtype)


def _lru_specs(l, r, cw, ncw, lb, ux_blk, uy_blk, row_of, cb_of):
    nbc = cw // lb
    vec = pl.BlockSpec((None, 1, cw), lambda *g: (l, 0, cb_of(*g)))
    gate = pl.BlockSpec((None, nbc, lb, lb), lambda *g: (l, cb_of(*g), 0, 0))
    return [pl.BlockSpec((r, cw), lambda *g: (row_of(*g), ux_blk + cb_of(*g))),
            pl.BlockSpec((r, cw), lambda *g: (row_of(*g), uy_blk + cb_of(*g))),
            pl.BlockSpec((None, ncw, cw), lambda *g: (l, 0, cb_of(*g))),
            vec, gate, vec, gate, vec, vec]


def _lru_prompt(proj, ux_col, l, lru_w, b, t):
    conv_w, conv_b3, w_r, b_r3, w_i, b_i3, lam3 = lru_w
    width = lam3.shape[-1]
    lb = w_r.shape[-1]
    ncw = conv_w.shape[1]
    m = proj.shape[0]
    cw = _pick(width, (1024, 512, 256, 128))
    r = _pick(t, (256, 128, 64, 32, 16, 8))
    nt = t // r
    specs = _lru_specs(l, r, cw, ncw, lb, ux_col // cw, (ux_col + width) // cw,
                       lambda bi, cb, ti: bi * nt + ti, lambda bi, cb, ti: cb)
    last8 = pl.BlockSpec((None, SUBLANES, cw), lambda bi, cb, ti: (bi, 0, cb))
    return _pcall(
        _lru_prompt_kernel, name="lru_prompt", grid=(b, width // cw, nt),
        in_specs=specs,
        args=(proj, proj, conv_w, conv_b3, w_r, b_r3, w_i, b_i3, lam3),
        out_specs=[pl.BlockSpec((r, cw), lambda bi, cb, ti: (bi * nt + ti, cb)), last8, last8],
        out_shape=[jax.ShapeDtypeStruct((m, width), BF16),
                   jax.ShapeDtypeStruct((b, SUBLANES, width), F32),
                   jax.ShapeDtypeStruct((b, SUBLANES, width), F32)],
        scratch=[pltpu.VMEM((SUBLANES, cw), F32)],
        sem=("arbitrary", "arbitrary", "arbitrary"))


def _lru_sample(proj, ux_col, l, lru_w, row0, h0_rep, buf_fr, yb_prev):
    conv_w, conv_b3, w_r, b_r3, w_i, b_i3, lam3 = lru_w
    width = lam3.shape[-1]
    lb = w_r.shape[-1]
    ncw = conv_w.shape[1]
    ms = h0_rep.shape[1]
    cw = _pick(width, (1024, 512, 256, 128))
    r = _pick(ms, (256, 128, 64, 32, 16, 8))
    assert row0 % r == 0
    rb0 = row0 // r
    specs = _lru_specs(l, r, cw, ncw, lb, ux_col // cw, (ux_col + width) // cw,
                       lambda ri, cb: rb0 + ri, lambda ri, cb: cb)
    st = pl.BlockSpec((None, r, cw), lambda ri, cb: (l, ri, cb))
    f32rows = pl.BlockSpec((r, cw), lambda ri, cb: (ri, cb))
    return _pcall(
        _lru_sample_kernel, name="lru_sample", grid=(ms // r, width // cw),
        in_specs=specs + [st, st],
        args=(proj, proj, conv_w, conv_b3, w_r, b_r3, w_i, b_i3, lam3, h0_rep, buf_fr),
        out_specs=[pl.BlockSpec((r, cw), lambda ri, cb: (rb0 + ri, cb)), f32rows, f32rows],
        out_shape=[jax.ShapeDtypeStruct(yb_prev.shape, BF16),
                   jax.ShapeDtypeStruct((ms, width), F32), jax.ShapeDtypeStruct((ms, width), F32)],
        prev=(yb_prev,),
        sem=("arbitrary", "arbitrary"))


def _pack_pair(hi, lo):
    hb = lax.bitcast_convert_type(hi.astype(jnp.bfloat16).astype(F32), jnp.uint32)
    lb = lax.bitcast_convert_type(lo.astype(jnp.bfloat16).astype(F32), jnp.uint32)
    return hb | (lb >> 16)


def _unpack_pair(p):
    hi = lax.bitcast_convert_type(p & jnp.uint32(0xFFFF0000), F32)
    lo = lax.bitcast_convert_type(p << 16, F32)
    return hi, lo


def _router_kernel(x_ref, g_ref, wr_ref, o_ref, hp_ref, *, ne):
    hf = _rms(x_ref[...], g_ref[...])
    half = hf.shape[1] // 2
    hp_ref[...] = _pack_pair(hf[:, :half], hf[:, half:])
    h = hf.astype(BF16)
    logits = jnp.dot(h, wr_ref[...].astype(BF16), preferred_element_type=F32)
    lane = lax.broadcasted_iota(jnp.int32, logits.shape, 1).astype(F32)
    neg = F32(-jnp.inf)
    big = F32(LANES)
    l1 = jnp.where(lane < ne, logits, neg)
    m1 = jnp.max(l1, axis=-1, keepdims=True)
    i1 = jnp.min(jnp.where(l1 == m1, lane, big), axis=-1, keepdims=True)
    l2 = jnp.where(lane == i1, neg, l1)
    m2 = jnp.max(l2, axis=-1, keepdims=True)
    i2 = jnp.min(jnp.where(l2 == m2, lane, big), axis=-1, keepdims=True)
    e2 = jnp.exp(m2 - m1)
    den = 1.0 + e2
    o_ref[...] = (jnp.where(lane == 0.0, i1, 0.0) + jnp.where(lane == 1.0, i2, 0.0)
                  + jnp.where(lane == 2.0, 1.0 / den, 0.0) + jnp.where(lane == 3.0, e2 / den, 0.0))


def _router(x, g3, l, wr):
    m, d = x.shape
    ne = wr.shape[-1]
    assert TOP_K == 2 and ne <= LANES
    wr_pad = jnp.pad(wr, ((0, 0), (0, LANES - ne)))
    tm = _pick(m, (512, 256, 128, 64, 32, 16, 8))
    return _pcall(
        functools.partial(_router_kernel, ne=ne), name="router", grid=(m // tm,),
        in_specs=[pl.BlockSpec((tm, d), lambda i: (i, 0)), _vspec(l, d),
                  pl.BlockSpec((d, LANES), lambda i: (0, 0))],
        args=(x, g3, wr_pad),
        out_specs=[pl.BlockSpec((tm, LANES), lambda i: (i, 0)),
                   pl.BlockSpec((tm, d // 2), lambda i: (i, 0))],
        out_shape=[jax.ShapeDtypeStruct((m, LANES), F32),
                   jax.ShapeDtypeStruct((m, d // 2), jnp.uint32)],
        sem=("arbitrary",))


def _dispatch_tables(rout, ne, tg):
    m = rout.shape[0]
    na = TOP_K * m
    eid = rout[:, :TOP_K].astype(jnp.int32).T.reshape(na)
    onehot = (eid[:, None] == jnp.arange(ne, dtype=jnp.int32)[None, :]).astype(jnp.int32)
    cnt = jnp.sum(onehot, axis=0)
    rank = jnp.sum((jnp.cumsum(onehot, axis=0) - onehot) * onehot, axis=1)
    padded = ((cnt + tg - 1) // tg) * tg
    gend = jnp.cumsum(padded)
    gstart = gend - padded
    pos = gstart[eid] + rank
    p_rows = ((na + tg - 1) // tg + ne) * tg
    n_tiles = p_rows // tg
    tok = jnp.zeros((p_rows,), jnp.int32).at[pos].set(jnp.arange(na, dtype=jnp.int32) % m)
    n_used = (gend[-1] // tg).astype(jnp.int32)
    tstart = jnp.arange(n_tiles, dtype=jnp.int32) * tg
    tstart = jnp.minimum(tstart, (n_used - 1) * tg)
    te = jnp.minimum(jnp.searchsorted(gend, tstart, side="right"), ne - 1).astype(jnp.int32)
    tile0 = jnp.arange(n_tiles, dtype=jnp.int32) * tg
    n_valid = jnp.clip(gstart[te] + cnt[te] - tile0, 0, tg).astype(jnp.int32)
    return pos, tok, te, n_used.reshape(1), n_valid, p_rows


DMA_UNROLL = 8


def _row_gather(src_hbm, dst, idx_of, sem):
    rows = dst.shape[0]
    assert rows % DMA_UNROLL == 0

    def row_copy(r, t):
        return pltpu.make_async_copy(src_hbm.at[pl.ds(t, 1), :], dst.at[pl.ds(r, 1), :], sem)

    def issue(c, carry):
        for u in range(DMA_UNROLL):
            r = c * DMA_UNROLL + u
            row_copy(r, idx_of(r)).start()
        return carry

    def drain(c, carry):
        for u in range(DMA_UNROLL):
            row_copy(c * DMA_UNROLL + u, 0).wait()
        return carry

    start = lambda: lax.fori_loop(0, rows // DMA_UNROLL, issue, 0)
    wait = lambda: lax.fori_loop(0, rows // DMA_UNROLL, drain, 0)
    return start, wait


def _moe_gather_kernel(tok_ref, hp_ref, o_ref):
    tr, d = o_ref.shape
    half = d // 2

    group = 2 * SUBLANES

    def body(g, carry):
        r0 = pl.multiple_of(g * group, group)
        rows = [hp_ref[pl.ds(tok_ref[0, r0 + u], 1), :] for u in range(group)]
        hi, lo = _unpack_pair(jnp.concatenate(rows, axis=0))
        o_ref[pl.ds(r0, group), pl.ds(0, half)] = hi.astype(o_ref.dtype)
        o_ref[pl.ds(r0, group), pl.ds(half, half)] = lo.astype(o_ref.dtype)
        return carry

    lax.fori_loop(0, tr // group, body, 0)


def _moe_gather(hp, tok, p_rows):
    m, half = hp.shape
    tr = _pick(p_rows, (GATHER_ROWS, 128, 64, 32, 16))
    nt = p_rows // tr
    return _pcall(
        _moe_gather_kernel, name="moe_gather", grid=(nt,),
        in_specs=[pl.BlockSpec((None, 1, tr), lambda i: (i, 0, 0), memory_space=pltpu.SMEM),
                  pl.BlockSpec((m, half), lambda i: (0, 0), pipeline_mode=pl.Buffered(1))],
        args=(tok.reshape(nt, 1, tr), hp),
        out_specs=[pl.BlockSpec((tr, 2 * half), lambda i: (i, 0))],
        out_shape=[jax.ShapeDtypeStruct((p_rows, 2 * half), BF16)],
        sem=("arbitrary",))[0]


def _tile_changed(te_ref):
    ti = pl.program_id(1)
    return jnp.logical_or(ti == 0, te_ref[ti] != te_ref[jnp.maximum(ti - 1, 0)])


def _valid_row_chunks(nv, o_ref, chunk):
    tg = o_ref.shape[0]

    @pl.when(nv == tg)
    def _():
        _row_chunks(tg, chunk)

    @pl.when(nv < tg)
    def _():
        def maybe(rs):
            @pl.when(rs.start < nv)
            def _():
                chunk(rs)

            @pl.when(rs.start >= nv)
            def _():
                o_ref[rs, :] = jnp.zeros((rs.stop - rs.start, o_ref.shape[1]), o_ref.dtype)

        _row_chunks(tg, maybe)


def _gmm_swiglu_kernel(te_ref, nu_ref, nv_ref, x_ref, wg_ref, wu_ref, o_ref, wgb_ref, wub_ref):
    @pl.when(_tile_changed(te_ref))
    def _():
        _cast_w(wg_ref, wgb_ref)
        _cast_w(wu_ref, wub_ref)

    def chunk(rs):
        x = x_ref[rs, :]
        g = jnp.dot(x, wgb_ref[...], preferred_element_type=F32)
        u = jnp.dot(x, wub_ref[...], preferred_element_type=F32)
        o_ref[rs, :] = (jax.nn.silu(g) * u).astype(o_ref.dtype)

    _valid_row_chunks(nv_ref[pl.program_id(1)], o_ref, chunk)


def _gmm_down_kernel(te_ref, nu_ref, nv_ref, x_ref, w_ref, o_ref, wb_ref):
    @pl.when(_tile_changed(te_ref))
    def _():
        _cast_w(w_ref, wb_ref)

    hw = o_ref.shape[1]

    def chunk(rs):
        y = jnp.dot(x_ref[rs, :], wb_ref[...], preferred_element_type=F32)
        o_ref[rs, :] = _pack_pair(y[:, :hw], y[:, hw:])

    _valid_row_chunks(nv_ref[pl.program_id(1)], o_ref, chunk)


def _gmm_specs(jm, tg, k, tn, j0=0):
    xrow = lambda j, ti, te, nu, nv: (jnp.minimum(ti, nu[0] - 1), 0)
    wsp = pl.BlockSpec((None, None, k, tn), lambda j, ti, te, nu, nv: (jm, te[ti], 0, j0 + j))
    return pl.BlockSpec((tg, k), xrow), wsp, pl.BlockSpec((tg, tn), lambda j, ti, te, nu, nv: (ti, j0 + j))


def _gmm_swiglu(xg, wg, wu, jm, te, n_used, n_valid, tg, tns):
    p, k = xg.shape
    n = wg.shape[-1]
    assert sum(tn * cnt for tn, cnt in tns) == n
    out, col = None, 0
    for tn, cnt in tns:
        assert col % tn == 0
        xs, ws, os_ = _gmm_specs(jm, tg, k, tn, col // tn)
        out = _pcall(
            _gmm_swiglu_kernel, name="gmm_swiglu", grid=(cnt, p // tg), num_scalar_prefetch=3,
            in_specs=[None, None, None, xs, ws, ws], args=(te, n_used, n_valid, xg, wg, wu),
            out_specs=[os_], out_shape=[jax.ShapeDtypeStruct((p, n), BF16)],
            scratch=[pltpu.VMEM((k, tn), BF16), pltpu.VMEM((k, tn), BF16)],
            prev=(out,), sem=("arbitrary", "arbitrary"))[0]
        col += tn * cnt
    return out


def _col_tiling(n, wide, narrow):
    nw = n // wide
    rest = n - nw * wide
    assert rest % narrow == 0 and (nw * wide) % narrow == 0
    return tuple(t for t in ((wide, nw), (narrow, rest // narrow)) if t[1])


def _gmm_down(ug, wd, jm, te, n_used, n_valid, tg, tn):
    p, k = ug.shape
    n = wd.shape[-1]
    xs, ws, _ = _gmm_specs(jm, tg, k, tn)
    return _pcall(
        _gmm_down_kernel, name="gmm_down", grid=(n // tn, p // tg), num_scalar_prefetch=3,
        in_specs=[None, None, None, xs, ws], args=(te, n_used, n_valid, ug, wd),
        out_specs=[pl.BlockSpec((tg, tn // 2), lambda j, ti, te, nu, nv: (ti, j))],
        out_shape=[jax.ShapeDtypeStruct((p, n // 2), jnp.uint32)],
        scratch=[pltpu.VMEM((k, tn), BF16)],
        sem=("arbitrary", "arbitrary"))[0]


def _moe_combine_kernel(pos_ref, nxt_ref, x_ref, rw_ref, g_ref, yg_hbm, *rest, tn, np_tiles):
    outs, (buf, sem) = rest[:-2], rest[-2:]
    tc = x_ref.shape[0]
    i = pl.program_id(0)
    slot = lax.rem(i, 2)

    def gathers(idx_ref, s):
        return [_row_gather(yg_hbm, buf.at[s, k], lambda r, k=k: idx_ref[0, k * tc + r], sem.at[s])
                for k in range(TOP_K)]

    @pl.when(i == 0)
    def _():
        for start, _ in gathers(pos_ref, 0):
            start()

    @pl.when(i + 1 < pl.num_programs(0))
    def _():
        for start, _ in gathers(nxt_ref, 1 - slot):
            start()

    for _, wait in gathers(pos_ref, slot):
        wait()
    rows = buf[slot]

    def expand(p):
        hw = tn // 2
        parts = []
        for jj in range(p.shape[1] // hw):
            parts.extend(_unpack_pair(p[:, jj * hw:(jj + 1) * hw]))
        return jnp.concatenate(parts, axis=-1)

    rw = rw_ref[...]
    y = rw[:, TOP_K:TOP_K + 1] * expand(rows[0])
    for k in range(1, TOP_K):
        y = y + rw[:, TOP_K + k:TOP_K + k + 1] * expand(rows[k])
    xn = x_ref[...] + y
    if np_tiles is None:
        outs[0][...] = xn
    else:
        yn = _rms(xn, g_ref[...])

        @pl.when(i < np_tiles)
        def _():
            outs[0][...] = yn

        @pl.when(i >= np_tiles)
        def _():
            outs[1][...] = yn


def _moe_combine(x, rout, pos, ygp, tn, g3=None, mp=None):
    m, d = x.shape
    tc = _pick(m, (COMBINE_ROWS, 64, 32, 16, 8))
    nt = m // tc
    pos_t = pos.reshape(TOP_K, nt, tc).transpose(1, 0, 2).reshape(nt, 1, TOP_K * tc)
    if g3 is None:
        g3 = jnp.ones((1, 1, d), F32)
        np_tiles = None
        out_specs = [pl.BlockSpec((tc, d), lambda i: (i, 0))]
        out_shape = [jax.ShapeDtypeStruct((m, d), F32)]
    else:
        assert mp % tc == 0
        np_tiles = mp // tc
        out_specs = [pl.BlockSpec((tc, d), lambda i: (jnp.minimum(i, np_tiles - 1), 0)),
                     pl.BlockSpec((tc, d), lambda i: (jnp.maximum(i - np_tiles, 0), 0))]
        out_shape = [jax.ShapeDtypeStruct((mp, d), F32), jax.ShapeDtypeStruct((m - mp, d), F32)]
    return _pcall(
        functools.partial(_moe_combine_kernel, tn=tn, np_tiles=np_tiles), name="moe_combine", grid=(nt,),
        in_specs=[pl.BlockSpec((None, 1, TOP_K * tc), lambda i: (i, 0, 0), memory_space=pltpu.SMEM),
                  pl.BlockSpec((None, 1, TOP_K * tc), lambda i: (jnp.minimum(i + 1, nt - 1), 0, 0),
                               memory_space=pltpu.SMEM),
                  pl.BlockSpec((tc, d), lambda i: (i, 0)),
                  pl.BlockSpec((tc, LANES), lambda i: (i, 0)),
                  _vspec(0, d),
                  pl.BlockSpec(memory_space=pl.ANY)],
        args=(pos_t, pos_t, x, rout, g3, ygp),
        out_specs=out_specs, out_shape=out_shape,
        scratch=[pltpu.VMEM((2, TOP_K, tc, d // 2), jnp.uint32), pltpu.SemaphoreType.DMA((2,))],
        sem=("arbitrary",))


def kernel(x_prompt, x_sample, state_ret, state_lru, state_conv, norm_mix, w_in, ret_gn, w_ret_o, conv_w, conv_b, w_rgate, b_rgate, w_igate, b_igate, lru_lambda, w_lru_o, w_out, norm_ffn, ffn_w_gate, ffn_w_up, ffn_w_down, moe_router, moe_w_gate, moe_w_up, moe_w_down, norm_final):
    bp, tp, d = x_prompt.shape
    bs, ts, _ = x_sample.shape
    depth, _, nh, dk, dv = state_ret.shape
    width = state_lru.shape[-1]
    ncw = conv_w.shape[1]
    ne = moe_router.shape[-1]
    assert dk == dv and nh * dk == width == d and ts == SUBLANES and ncw - 1 <= SUBLANES
    mp, ms = bp * tp, bs * ts
    m = mp + ms
    g_col = 2 * nh * dk + nh * dv
    ux_col = g_col + nh * dv
    ga_col = ux_col + 2 * width

    tm = _pick(m, (1024, 512, 256, 128, 64, 32, 16, 8))
    tm_big = _pick(m, (1536, 1024, 512, 256, 128, 64, 32, 16, 8))
    tm_half = _pick(m, (512, 256, 128, 64, 32, 16, 8))
    tn_of = lambda n: _pick(n, (1024, 512, 256, 128))

    vec3 = lambda a: a.reshape(a.shape[0], 1, a.shape[-1])
    norm_mix3, norm_ffn3, ret_gn3 = vec3(norm_mix), vec3(norm_ffn), vec3(ret_gn)
    norm_final3 = norm_final.reshape(1, 1, d)
    lru_w = (conv_w, vec3(conv_b), w_rgate, vec3(b_rgate), w_igate, vec3(b_igate), vec3(lru_lambda))
    h0_rep = jnp.repeat(state_lru, ts, axis=1)
    buf_fr = jnp.pad(state_conv, ((0, 0), (0, 0), (ts - (ncw - 1), 0), (0, 0))).reshape(depth, ms, width)

    cos_p, sin_p = _rope_tables(tp, dk // 2, 0)
    cos_s, sin_s = _rope_tables(ts, dk // 2, PAST_LEN)
    cos = jnp.concatenate([jnp.tile(cos_p, (bp, 1)), jnp.tile(cos_s, (bs, 1))], axis=0)
    sin = jnp.concatenate([jnp.tile(sin_p, (bp, 1)), jnp.tile(sin_s, (bs, 1))], axis=0)
    tn_in = _pick(nh * dk, (1024, 512, 256))

    x = jnp.concatenate([x_prompt.reshape(mp, d), x_sample.reshape(ms, d)], axis=0)
    ret_p = ret_s = y_p = y_s = None
    lrus_p, lrus_s, convs_p, convs_s = [], [], [], []
    for l in range(depth):
        h = _rmsnorm(x, norm_mix3, l, BF16)
        proj = _mm_inproj(h, w_in, (l,), cos, sin, 2 * nh * dk, dk, (g_col, ux_col),
                          (ux_col + width, ga_col), tm_big, tn_in)

        ya, ret_p = _ret_prompt(proj, ret_gn3, l, depth, bp, tp, nh, dk, dv, ret_p)
        ya, ret_s = _ret_sample(proj, ret_gn3, state_ret, l, mp, bs, ts, nh, dk, dv, ya, ret_s)
        yb, hl_p, ul_p = _lru_prompt(proj, ux_col, l, lru_w, bp, tp)
        yb, hl_s, us_s = _lru_sample(proj, ux_col, l, lru_w, mp, h0_rep, buf_fr, yb)
        lrus_p.append(hl_p[:, -1])
        lrus_s.append(hl_s.reshape(bs, ts, width)[:, -1])
        convs_p.append(ul_p[:, SUBLANES - (ncw - 1):])
        convs_s.append(us_s.reshape(bs, ts, width)[:, ts - (ncw - 1):])

        z = _mm_merge(ya, yb, w_ret_o, w_lru_o, (l,), proj, ga_col, ga_col + d, tm, tn_of(d) // 2)
        x = _mm_resid(z, w_out, (l,), x, tm, tn_of(d))
        j = l // 2
        if l % 2 == 0:
            h2 = _rmsnorm(x, norm_ffn3, l, BF16)
            ff = ffn_w_gate.shape[-1]
            u = _mm_swiglu(h2, ffn_w_gate, ffn_w_up, (j,), tm_big, _pick(ff, (512, 256, 128)))
            x = _mm_resid(u, ffn_w_down, (j,), x, tm_half, _pick(d, (512, 256, 128)))
        else:
            fe = moe_w_gate.shape[-1]
            tg = _pick(TOP_K * m, (MOE_TILE, 256, 128, 64, 32, 16))
            tn_d = _pick(d, (1024, 512, 256))
            rout, hp = _router(x, norm_ffn3, l, moe_router[j])
            pos, tok, te, n_used, n_valid, p_rows = _dispatch_tables(rout, ne, tg)
            xg = _moe_gather(hp, tok, p_rows)
            ug = _gmm_swiglu(xg, moe_w_gate, moe_w_up, j, te, n_used, n_valid, tg,
                             _col_tiling(fe, 512, _pick(fe, (256, 128))))
            ygp = _gmm_down(ug, moe_w_down, j, te, n_used, n_valid, tg, tn_d)
            if l == depth - 1:
                y_p, y_s = _moe_combine(x, rout, pos, ygp, tn_d, norm_final3, mp)
            else:
                x = _moe_combine(x, rout, pos, ygp, tn_d)[0]
    if y_p is None:
        y_p = _rmsnorm(x, norm_final3, 0, F32, 0, mp)
        y_s = _rmsnorm(x, norm_final3, 0, F32, mp, ms)
    return (y_p.reshape(bp, tp, d), y_s.reshape(bs, ts, d),
            ret_p, jnp.stack(lrus_p), jnp.stack(convs_p),
            ret_s, jnp.stack(lrus_s), jnp.stack(convs_s))
```

```python
import functools

import jax
import jax.numpy as jnp
from jax import lax
from jax.experimental import pallas as pl
from jax.experimental.pallas import tpu as pltpu

F32 = jnp.float32
BF16 = jnp.bfloat16

ROPE_BASE = 10000.0
LRU_C = 8.0
EPS = 1e-6
RET_CHUNK = 128
PAST_LEN = 16384
TOP_K = 2
SUBLANES = 8
LANES = 128
VMEM_LIMIT = 56 * 1024 * 1024
MOE_TILE = 1024
GATHER_ROWS = 256
COMBINE_ROWS = 128


def _pick(n, prefs):
    for p in prefs:
        if n % p == 0:
            return p
    return n


def _params(*sem):
    return pltpu.CompilerParams(dimension_semantics=sem, vmem_limit_bytes=VMEM_LIMIT)


def _pcall(body, *, name, grid, in_specs, args, out_specs, out_shape, sem, scratch=(), prev=(),
           num_scalar_prefetch=0):
    n_in = len(args)
    prev = tuple(prev) + (None,) * (len(out_shape) - len(prev))
    extra = [(oi, p) for oi, p in enumerate(prev) if p is not None]
    aliases = {n_in + e: oi for e, (oi, _) in enumerate(extra)}
    if extra:
        inner = body

        def body(*refs):
            return inner(*refs[:n_in], *refs[n_in + len(extra):])

    in_specs = list(in_specs) + [pl.BlockSpec(memory_space=pl.ANY)] * len(extra)
    if num_scalar_prefetch:
        grid_spec = pltpu.PrefetchScalarGridSpec(
            num_scalar_prefetch=num_scalar_prefetch, grid=grid, in_specs=in_specs[num_scalar_prefetch:],
            out_specs=tuple(out_specs), scratch_shapes=list(scratch))
        call = pl.pallas_call(body, grid_spec=grid_spec, out_shape=tuple(out_shape),
                              input_output_aliases=aliases, compiler_params=_params(*sem), name=name)
    else:
        call = pl.pallas_call(body, grid=grid, in_specs=in_specs, out_specs=tuple(out_specs),
                              out_shape=tuple(out_shape), scratch_shapes=list(scratch),
                              input_output_aliases=aliases, compiler_params=_params(*sem), name=name)
    return call(*args, *[p for _, p in extra])


def _wspec(pre, k, tn):
    return pl.BlockSpec((None,) * len(pre) + (k, tn), lambda j, i: pre + (0, j))


def _vspec(l, n):
    return pl.BlockSpec((None, 1, n), lambda *_: (l, 0, 0))


def _rms(x, g):
    return (x * lax.rsqrt(jnp.mean(x * x, axis=-1, keepdims=True) + EPS)) * g


def _rmsnorm_kernel(x_ref, g_ref, o_ref):
    o_ref[...] = _rms(x_ref[...], g_ref[...]).astype(o_ref.dtype)


def _rmsnorm(x, g3, l, out_dtype, row0=0, rows=None):
    d = x.shape[1]
    rows = x.shape[0] if rows is None else rows
    tm = _pick(rows, (512, 256, 128, 64, 32, 16, 8))
    assert row0 % tm == 0
    rb0 = row0 // tm
    return _pcall(
        _rmsnorm_kernel, name="rmsnorm", grid=(rows // tm,),
        in_specs=[pl.BlockSpec((tm, d), lambda i: (rb0 + i, 0)), _vspec(l, d)],
        args=(x, g3),
        out_specs=[pl.BlockSpec((tm, d), lambda i: (i, 0))],
        out_shape=[jax.ShapeDtypeStruct((rows, d), out_dtype)],
        sem=("arbitrary",))[0]


def _cast_w(w_ref, wb_ref):
    k = w_ref.shape[0]
    ck = _pick(k, (256, 128, 64, 32, 16))

    def body(c, carry):
        r = pl.multiple_of(c * ck, ck)
        wb_ref[pl.ds(r, ck), :] = w_ref[pl.ds(r, ck), :].astype(BF16)
        return carry

    lax.fori_loop(0, k // ck, body, 0)


def _mm_inproj_kernel(x_ref, w_ref, cos_ref, sin_ref, o_ref, wb_ref, *, nrot, dk, silu_tiles, gelu_tiles):
    j = pl.program_id(0)

    @pl.when(pl.program_id(1) == 0)
    def _():
        _cast_w(w_ref, wb_ref)

    tm, tn = o_ref.shape
    nchunk = 4 if tm % (4 * 16) == 0 else 1

    def row_chunks(epilogue):
        rc = tm // nchunk
        for c in range(nchunk):
            rs = slice(c * rc, (c + 1) * rc)
            epilogue(jnp.dot(x_ref[rs, :], wb_ref[...], preferred_element_type=F32), rs)

    def store(fn):
        def epilogue(acc, rs):
            o_ref[rs, :] = fn(acc).astype(o_ref.dtype)
        return epilogue

    in_tiles = lambda t: jnp.logical_and(j >= t[0], j < t[1])
    is_silu, is_gelu = in_tiles(silu_tiles), in_tiles(gelu_tiles)

    @pl.when(jnp.logical_and(j >= nrot, jnp.logical_not(jnp.logical_or(is_silu, is_gelu))))
    def _():
        row_chunks(store(lambda acc: acc))

    @pl.when(is_silu)
    def _():
        row_chunks(store(jax.nn.silu))

    @pl.when(is_gelu)
    def _():
        row_chunks(store(jax.nn.gelu))

    @pl.when(j < nrot)
    def _():
        scale = jnp.where(j >= nrot // 2, F32(dk ** -0.5), F32(1.0))
        half = dk // 2

        def rotary(acc, rs):
            cos = cos_ref[rs, :] * scale
            sin = sin_ref[rs, :] * scale
            for h in range(tn // dk):
                x1 = acc[:, h * dk:h * dk + half]
                x2 = acc[:, h * dk + half:(h + 1) * dk]
                o_ref[rs, h * dk:h * dk + half] = (x1 * cos - x2 * sin).astype(o_ref.dtype)
                o_ref[rs, h * dk + half:(h + 1) * dk] = (x1 * sin + x2 * cos).astype(o_ref.dtype)

        row_chunks(rotary)


def _mm_inproj(x, w, pre, cos, sin, qk_cols, dk, silu_cols, gelu_cols, tm, tn):
    m, k = x.shape
    n = w.shape[-1]
    half = cos.shape[1]
    assert qk_cols % (2 * tn) == 0 and tn % dk == 0
    assert all(c % tn == 0 for c in silu_cols + gelu_cols)
    tiles = lambda cols: (cols[0] // tn, cols[1] // tn)
    return _pcall(
        functools.partial(_mm_inproj_kernel, nrot=qk_cols // tn, dk=dk, silu_tiles=tiles(silu_cols),
                          gelu_tiles=tiles(gelu_cols)), name="mm_inproj",
        grid=(n // tn, m // tm),
        in_specs=[pl.BlockSpec((tm, k), lambda j, i: (i, 0)), _wspec(pre, k, tn),
                  pl.BlockSpec((tm, half), lambda j, i: (i, 0)),
                  pl.BlockSpec((tm, half), lambda j, i: (i, 0))],
        args=(x, w, cos, sin),
        out_specs=[pl.BlockSpec((tm, tn), lambda j, i: (i, j))],
        out_shape=[jax.ShapeDtypeStruct((m, n), BF16)],
        scratch=[pltpu.VMEM((k, tn), BF16)],
        sem=("arbitrary", "arbitrary"))[0]


def _mm_resid_kernel(x_ref, w_ref, r_ref, o_ref, wb_ref):
    @pl.when(pl.program_id(1) == 0)
    def _():
        _cast_w(w_ref, wb_ref)

    o_ref[...] = r_ref[...] + jnp.dot(x_ref[...], wb_ref[...], preferred_element_type=F32)


def _mm_resid(x, w, pre, r, tm, tn):
    m, k = x.shape
    n = w.shape[-1]
    return _pcall(
        _mm_resid_kernel, name="mm_resid", grid=(n // tn, m // tm),
        in_specs=[pl.BlockSpec((tm, k), lambda j, i: (i, 0)), _wspec(pre, k, tn),
                  pl.BlockSpec((tm, tn), lambda j, i: (i, j))],
        args=(x, w, r),
        out_specs=[pl.BlockSpec((tm, tn), lambda j, i: (i, j))],
        out_shape=[jax.ShapeDtypeStruct((m, n), F32)],
        scratch=[pltpu.VMEM((k, tn), BF16)],
        sem=("arbitrary", "arbitrary"))[0]


def _row_chunks(rows, fn, nchunk=4):
    n = nchunk if rows % (nchunk * 16) == 0 else 1
    rc = rows // n
    for c in range(n):
        fn(slice(c * rc, (c + 1) * rc))


def _swiglu_rows(x_ref, wgb_ref, wub_ref, o_ref):
    def chunk(rs):
        x = x_ref[rs, :]
        g = jnp.dot(x, wgb_ref[...], preferred_element_type=F32)
        u = jnp.dot(x, wub_ref[...], preferred_element_type=F32)
        o_ref[rs, :] = (jax.nn.silu(g) * u).astype(o_ref.dtype)

    _row_chunks(o_ref.shape[0], chunk)


def _mm_swiglu_kernel(x_ref, wg_ref, wu_ref, o_ref, wgb_ref, wub_ref):
    @pl.when(pl.program_id(1) == 0)
    def _():
        _cast_w(wg_ref, wgb_ref)
        _cast_w(wu_ref, wub_ref)

    _swiglu_rows(x_ref, wgb_ref, wub_ref, o_ref)


def _mm_swiglu(x, wg, wu, pre, tm, tn):
    m, k = x.shape
    n = wg.shape[-1]
    return _pcall(
        _mm_swiglu_kernel, name="mm_swiglu", grid=(n // tn, m // tm),
        in_specs=[pl.BlockSpec((tm, k), lambda j, i: (i, 0)), _wspec(pre, k, tn), _wspec(pre, k, tn)],
        args=(x, wg, wu),
        out_specs=[pl.BlockSpec((tm, tn), lambda j, i: (i, j))],
        out_shape=[jax.ShapeDtypeStruct((m, n), BF16)],
        scratch=[pltpu.VMEM((k, tn), BF16), pltpu.VMEM((k, tn), BF16)],
        sem=("arbitrary", "arbitrary"))[0]


def _mm_merge_kernel(a_ref, b_ref, wa_ref, wb_ref, ga_ref, gb_ref, o_ref, wab_ref, wbb_ref):
    @pl.when(pl.program_id(1) == 0)
    def _():
        _cast_w(wa_ref, wab_ref)
        _cast_w(wb_ref, wbb_ref)

    def chunk(rs):
        ya = jnp.dot(a_ref[rs, :], wab_ref[...], preferred_element_type=F32)
        yb = jnp.dot(b_ref[rs, :], wbb_ref[...], preferred_element_type=F32)
        ga = ga_ref[rs, :].astype(F32)
        gb = gb_ref[rs, :].astype(F32)
        o_ref[rs, :] = (jax.nn.sigmoid(ga) * ya + jax.nn.sigmoid(gb) * yb).astype(o_ref.dtype)

    _row_chunks(o_ref.shape[0], chunk)


def _mm_merge(a, b, wa, wb, pre, proj, ga_col, gb_col, tm, tn):
    m, k = a.shape
    n = wa.shape[-1]
    ga_blk, gb_blk = ga_col // tn, gb_col // tn
    return _pcall(
        _mm_merge_kernel, name="mm_merge", grid=(n // tn, m // tm),
        in_specs=[pl.BlockSpec((tm, k), lambda j, i: (i, 0)),
                  pl.BlockSpec((tm, k), lambda j, i: (i, 0)),
                  _wspec(pre, k, tn), _wspec(pre, k, tn),
                  pl.BlockSpec((tm, tn), lambda j, i: (i, ga_blk + j)),
                  pl.BlockSpec((tm, tn), lambda j, i: (i, gb_blk + j))],
        args=(a, b, wa, wb, proj, proj),
        out_specs=[pl.BlockSpec((tm, tn), lambda j, i: (i, j))],
        out_shape=[jax.ShapeDtypeStruct((m, n), BF16)],
        scratch=[pltpu.VMEM((k, tn), BF16), pltpu.VMEM((k, tn), BF16)],
        sem=("arbitrary", "arbitrary"))[0]


def _rope_kernel(inv_ref, cos_ref, sin_ref, *, pos0):
    t, half = cos_ref.shape
    pos = lax.broadcasted_iota(jnp.int32, (t, half), 0).astype(F32) + F32(pos0)
    ang = pos * inv_ref[...]
    cos_ref[...] = jnp.cos(ang)
    sin_ref[...] = jnp.sin(ang)


def _rope_tables(t, half, pos0):
    inv = ROPE_BASE ** (-jnp.arange(half, dtype=F32) / half)
    return pl.pallas_call(
        functools.partial(_rope_kernel, pos0=pos0),
        out_shape=(jax.ShapeDtypeStruct((t, half), F32), jax.ShapeDtypeStruct((t, half), F32)),
        name="rope_tables",
    )(inv.reshape(1, half))


def _decay_tables(c, h, dk, dv):
    log_g = jnp.log1p(-jnp.exp2(-5.0 - jnp.arange(h, dtype=F32)))
    idx = jnp.arange(c)
    rel = idx[:, None] - idx[None, :]
    dmask = jnp.where(rel[None] >= 0,
                      jnp.exp(jnp.maximum(rel, 0)[None].astype(F32) * log_g[:, None, None]), 0.0)
    xi = jnp.exp((idx + 1).astype(F32)[None, :] * log_g[:, None])
    zeta = jnp.exp((c - 1 - idx).astype(F32)[None, :] * log_g[:, None])
    g_c = jnp.exp(c * log_g)
    xi_t = jnp.broadcast_to(xi[:, :, None], (h, c, dv))
    zeta_t = jnp.broadcast_to(zeta[:, :, None], (h, c, dk))
    return dmask, xi_t, zeta_t, g_c


def _ret_head(qb, kb, v, g, dm, xi, zt, gn, s, gc):
    scores = lax.dot_general(qb, kb, (((1,), (1,)), ((), ())), preferred_element_type=F32)
    intra = jnp.dot((scores * dm).astype(BF16), v, preferred_element_type=F32)
    cross = jnp.dot(qb, s.astype(BF16), preferred_element_type=F32) * xi
    kz = (kb.astype(F32) * zt).astype(BF16)
    upd = lax.dot_general(kz, v, (((0,), (0,)), ((), ())), preferred_element_type=F32)
    s_new = gc * s + upd
    o = intra + cross
    mu = jnp.mean(o, axis=-1, keepdims=True)
    d = o - mu
    var = jnp.mean(d * d, axis=-1, keepdims=True)
    y = d * lax.rsqrt(var + EPS) * gn
    return (g * y).astype(BF16), s_new


def _ret_prompt_kernel(gc_ref, q_ref, k_ref, v_ref, g_ref, dm_ref, xi_ref, zt_ref,
                       gn_ref, o_ref, s_ref, *, nh, dk, dv):
    @pl.when(pl.program_id(1) == 0)
    def _():
        s_ref[...] = jnp.zeros_like(s_ref)

    c = dm_ref.shape[1]
    for h in range(nh):
        s = s_ref[h]
        for ci in range(q_ref.shape[0] // c):
            rs = slice(ci * c, (ci + 1) * c)
            out, s = _ret_head(
                q_ref[rs, h * dk:(h + 1) * dk], k_ref[rs, h * dk:(h + 1) * dk],
                v_ref[rs, h * dv:(h + 1) * dv], g_ref[rs, h * dv:(h + 1) * dv].astype(F32),
                dm_ref[h], xi_ref[h], zt_ref[h], gn_ref[:, h * dv:(h + 1) * dv], s, gc_ref[h])
            o_ref[rs, h * dv:(h + 1) * dv] = out
        s_ref[h] = s


def _ret_prompt(proj, gn3, l, depth, b, t, nh, dk, dv, s_prev):
    cl = RET_CHUNK if t % RET_CHUNK == 0 else t
    cps = _pick(t // cl, (4, 2, 1))
    c = cl * cps
    nc = t // c
    w = nh * dk
    m = proj.shape[0]
    dmask, xi_t, zeta_t, g_c = _decay_tables(cl, nh, dk, dv)
    row = lambda bi, ci: bi * nc + ci
    full3 = lambda bi, ci: (0, 0, 0)
    return _pcall(
        functools.partial(_ret_prompt_kernel, nh=nh, dk=dk, dv=dv), name="ret_prompt", grid=(b, nc),
        in_specs=[pl.BlockSpec(memory_space=pltpu.SMEM),
                  pl.BlockSpec((c, w), lambda bi, ci: (row(bi, ci), 0)),
                  pl.BlockSpec((c, w), lambda bi, ci: (row(bi, ci), 1)),
                  pl.BlockSpec((c, w), lambda bi, ci: (row(bi, ci), 2)),
                  pl.BlockSpec((c, w), lambda bi, ci: (row(bi, ci), 3)),
                  pl.BlockSpec((nh, cl, cl), full3),
                  pl.BlockSpec((nh, cl, dv), full3),
                  pl.BlockSpec((nh, cl, dk), full3),
                  _vspec(l, nh * dv)],
        args=(g_c, proj, proj, proj, proj, dmask, xi_t, zeta_t, gn3),
        out_specs=[pl.BlockSpec((c, nh * dv), lambda bi, ci: (row(bi, ci), 0)),
                   pl.BlockSpec((None, None, nh, dk, dv), lambda bi, ci: (l, bi, 0, 0, 0))],
        out_shape=[jax.ShapeDtypeStruct((m, nh * dv), BF16),
                   jax.ShapeDtypeStruct((depth, b, nh, dk, dv), F32)],
        prev=(None, s_prev),
        sem=("arbitrary", "arbitrary"))


def _ret_sample_kernel(gc_ref, q_ref, k_ref, v_ref, g_ref, dm_ref, xi_ref, zt_ref,
                       gn_ref, s0_ref, o_ref, s_ref, *, nh, dk, dv, bb, ts):
    q = q_ref[...].astype(F32)
    k = k_ref[...].astype(F32)
    v = v_ref[...].astype(F32)
    g = g_ref[...].astype(F32)
    seqs = []
    for i in range(bb):
        r0, r1 = i * ts, (i + 1) * ts
        outs = []
        for h in range(nh):
            out, s_new = _ret_head(
                q[r0:r1, h * dk:(h + 1) * dk].astype(BF16), k[r0:r1, h * dk:(h + 1) * dk].astype(BF16),
                v[r0:r1, h * dv:(h + 1) * dv].astype(BF16), g[r0:r1, h * dv:(h + 1) * dv],
                dm_ref[h], xi_ref[h], zt_ref[h], gn_ref[:, h * dv:(h + 1) * dv],
                s0_ref[i, h], gc_ref[h])
            s_ref[i, h] = s_new
            outs.append(out.astype(F32))
        seqs.append(jnp.concatenate(outs, axis=-1))
    o_ref[...] = jnp.concatenate(seqs, axis=0).astype(o_ref.dtype)


def _ret_sample(proj, gn3, s0_all, l, row0, bs, ts, nh, dk, dv, ya_prev, s_prev):
    bb = _pick(bs, (4, 2, 1))
    rows = bb * ts
    w = nh * dk
    depth = s0_all.shape[0]
    dmask, xi_t, zeta_t, g_c = _decay_tables(ts, nh, dk, dv)
    assert row0 % rows == 0
    rb0 = row0 // rows
    full3 = lambda i: (0, 0, 0)
    return _pcall(
        functools.partial(_ret_sample_kernel, nh=nh, dk=dk, dv=dv, bb=bb, ts=ts), name="ret_sample",
        grid=(bs // bb,),
        in_specs=[pl.BlockSpec(memory_space=pltpu.SMEM),
                  pl.BlockSpec((rows, w), lambda i: (rb0 + i, 0)),
                  pl.BlockSpec((rows, w), lambda i: (rb0 + i, 1)),
                  pl.BlockSpec((rows, w), lambda i: (rb0 + i, 2)),
                  pl.BlockSpec((rows, w), lambda i: (rb0 + i, 3)),
                  pl.BlockSpec((nh, ts, ts), full3),
                  pl.BlockSpec((nh, ts, dv), full3),
                  pl.BlockSpec((nh, ts, dk), full3),
                  _vspec(l, nh * dv),
                  pl.BlockSpec((None, bb, nh, dk, dv), lambda i: (l, i, 0, 0, 0))],
        args=(g_c, proj, proj, proj, proj, dmask, xi_t, zeta_t, gn3, s0_all),
        out_specs=[pl.BlockSpec((rows, nh * dv), lambda i: (rb0 + i, 0)),
                   pl.BlockSpec((None, bb, nh, dk, dv), lambda i: (l, i, 0, 0, 0))],
        out_shape=[jax.ShapeDtypeStruct(ya_prev.shape, BF16),
                   jax.ShapeDtypeStruct((depth, bs, nh, dk, dv), F32)],
        prev=(ya_prev, s_prev),
        sem=("arbitrary",))


def _lru_gates(uc, wr_ref, br_ref, wi_ref, bi_ref, lam_ref):
    nb, lb, _ = wr_ref.shape
    ucb = uc.astype(BF16)
    rl, il = [], []
    for n in range(nb):
        xb = ucb[:, n * lb:(n + 1) * lb]
        rl.append(jnp.dot(xb, wr_ref[n].astype(BF16), preferred_element_type=F32))
        il.append(jnp.dot(xb, wi_ref[n].astype(BF16), preferred_element_type=F32))
    r = jax.nn.sigmoid(jnp.concatenate(rl, axis=-1) + br_ref[...])
    i = jax.nn.sigmoid(jnp.concatenate(il, axis=-1) + bi_ref[...])
    log_a = -LRU_C * r * jax.nn.softplus(-lam_ref[...])
    a = jnp.exp(log_a)
    z = -jnp.tanh(log_a) * (a * a + 1.0)
    bx = jnp.where(z == 0.0, 0.0, z * lax.rsqrt(z)) * (i * uc)
    return a, bx


def _blocks(x):
    return [x[lo:lo + SUBLANES, :] for lo in range(0, x.shape[0], SUBLANES)]


def _conv_blocks(u_blocks, prev_blocks, cw_ref, cb_ref, rows):
    ncw = cw_ref.shape[0]
    rolled = {}

    def rot(x, s):
        key = (id(x), s)
        if key not in rolled:
            rolled[key] = pltpu.roll(x, s, axis=0)
        return rolled[key]

    out = []
    for ub, pb in zip(u_blocks, prev_blocks):
        uc = cb_ref[...]
        for j in range(ncw):
            s = ncw - 1 - j
            term = ub if s == 0 else jnp.where(rows >= s, rot(ub, s), rot(pb, s))
            uc = uc + term * cw_ref[j:j + 1, :]
        out.append(uc)
    return jnp.concatenate(out, axis=0)


def _scan_block(a, bx, rows):
    d = 1
    while d < SUBLANES:
        m = rows >= d
        bx = bx + a * jnp.where(m, pltpu.roll(bx, d, axis=0), 0.0)
        a = a * jnp.where(m, pltpu.roll(a, d, axis=0), 1.0)
        d *= 2
    return a, bx


def _lru_prompt_kernel(ux_ref, uy_ref, cw_ref, cb_ref, wr_ref, br_ref, wi_ref, bi_ref, lam_ref,
                       o_ref, hl_ref, ul_ref, hc_ref):
    r, cw = ux_ref.shape

    @pl.when(pl.program_id(2) == 0)
    def _():
        hc_ref[...] = jnp.zeros_like(hc_ref)
        ul_ref[...] = jnp.zeros_like(ul_ref)

    rows = lax.broadcasted_iota(jnp.int32, (SUBLANES, cw), 0)
    ub = _blocks(ux_ref[...].astype(F32))
    uc = _conv_blocks(ub, [ul_ref[...]] + ub[:-1], cw_ref, cb_ref, rows)
    ul_ref[...] = ub[-1]
    a, bx = _lru_gates(uc, wr_ref, br_ref, wi_ref, bi_ref, lam_ref)
    carry = hc_ref[...]
    hs = []
    for ab, bb in zip(_blocks(a), _blocks(bx)):
        ab, bb = _scan_block(ab, bb, rows)
        hb = ab * carry + bb
        hs.append(hb)
        carry = jnp.broadcast_to(hb[SUBLANES - 1:SUBLANES, :], (SUBLANES, cw))
    hc_ref[...] = carry
    hl_ref[...] = hs[-1]
    h = jnp.concatenate(hs, axis=0)
    o_ref[...] = (h * uy_ref[...].astype(F32)).astype(o_ref.dtype)


def _lru_sample_kernel(ux_ref, uy_ref, cw_ref, cb_ref, wr_ref, br_ref, wi_ref, bi_ref, lam_ref,
                       h0_ref, buf_ref, o_ref, hl_ref, us_ref):
    r, cw = ux_ref.shape
    rows = lax.broadcasted_iota(jnp.int32, (SUBLANES, cw), 0)
    u = ux_ref[...].astype(F32)
    us_ref[...] = u
    uc = _conv_blocks(_blocks(u), _blocks(buf_ref[...]), cw_ref, cb_ref, rows)
    a, bx = _lru_gates(uc, wr_ref, br_ref, wi_ref, bi_ref, lam_ref)
    hs = []
    for ab, bb, h0 in zip(_blocks(a), _blocks(bx), _blocks(h0_ref[...])):
        ab, bb = _scan_block(ab, bb, rows)
        hs.append(ab * h0 + bb)
    h = jnp.concatenate(hs, axis=0)
    hl_ref[...] = h
    o_ref[...] = (h * uy_ref[...].astype(F32)).astype(o_ref.dtype)


def _lru_specs(l, r, cw, ncw, lb, ux_blk, uy_blk, row_of, cb_of):
    nbc = cw // lb
    vec = pl.BlockSpec((None, 1, cw), lambda *g: (l, 0, cb_of(*g)))
    gate = pl.BlockSpec((None, nbc, lb, lb), lambda *g: (l, cb_of(*g), 0, 0))
    return [pl.BlockSpec((r, cw), lambda *g: (row_of(*g), ux_blk + cb_of(*g))),
            pl.BlockSpec((r, cw), lambda *g: (row_of(*g), uy_blk + cb_of(*g))),
            pl.BlockSpec((None, ncw, cw), lambda *g: (l, 0, cb_of(*g))),
            vec, gate, vec, gate, vec, vec]


def _lru_prompt(proj, ux_col, l, lru_w, b, t):
    conv_w, conv_b3, w_r, b_r3, w_i, b_i3, lam3 = lru_w
    width = lam3.shape[-1]
    lb = w_r.shape[-1]
    ncw = conv_w.shape[1]
    m = proj.shape[0]
    cw = _pick(width, (1024, 512, 256, 128))
    r = _pick(t, (256, 128, 64, 32, 16, 8))
    nt = t // r
    specs = _lru_specs(l, r, cw, ncw, lb, ux_col // cw, (ux_col + width) // cw,
                       lambda bi, cb, ti: bi * nt + ti, lambda bi, cb, ti: cb)
    last8 = pl.BlockSpec((None, SUBLANES, cw), lambda bi, cb, ti: (bi, 0, cb))
    return _pcall(
        _lru_prompt_kernel, name="lru_prompt", grid=(b, width // cw, nt),
        in_specs=specs,
        args=(proj, proj, conv_w, conv_b3, w_r, b_r3, w_i, b_i3, lam3),
        out_specs=[pl.BlockSpec((r, cw), lambda bi, cb, ti: (bi * nt + ti, cb)), last8, last8],
        out_shape=[jax.ShapeDtypeStruct((m, width), BF16),
                   jax.ShapeDtypeStruct((b, SUBLANES, width), F32),
                   jax.ShapeDtypeStruct((b, SUBLANES, width), F32)],
        scratch=[pltpu.VMEM((SUBLANES, cw), F32)],
        sem=("arbitrary", "arbitrary", "arbitrary"))


def _lru_sample(proj, ux_col, l, lru_w, row0, h0_rep, buf_fr, yb_prev):
    conv_w, conv_b3, w_r, b_r3, w_i, b_i3, lam3 = lru_w
    width = lam3.shape[-1]
    lb = w_r.shape[-1]
    ncw = conv_w.shape[1]
    ms = h0_rep.shape[1]
    cw = _pick(width, (1024, 512, 256, 128))
    r = _pick(ms, (256, 128, 64, 32, 16, 8))
    assert row0 % r == 0
    rb0 = row0 // r
    specs = _lru_specs(l, r, cw, ncw, lb, ux_col // cw, (ux_col + width) // cw,
                       lambda ri, cb: rb0 + ri, lambda ri, cb: cb)
    st = pl.BlockSpec((None, r, cw), lambda ri, cb: (l, ri, cb))
    f32rows = pl.BlockSpec((r, cw), lambda ri, cb: (ri, cb))
    return _pcall(
        _lru_sample_kernel, name="lru_sample", grid=(ms // r, width // cw),
        in_specs=specs + [st, st],
        args=(proj, proj, conv_w, conv_b3, w_r, b_r3, w_i, b_i3, lam3, h0_rep, buf_fr),
        out_specs=[pl.BlockSpec((r, cw), lambda ri, cb: (rb0 + ri, cb)), f32rows, f32rows],
        out_shape=[jax.ShapeDtypeStruct(yb_prev.shape, BF16),
                   jax.ShapeDtypeStruct((ms, width), F32), jax.ShapeDtypeStruct((ms, width), F32)],
        prev=(yb_prev,),
        sem=("arbitrary", "arbitrary"))


def _pack_pair(hi, lo):
    hb = lax.bitcast_convert_type(hi.astype(jnp.bfloat16).astype(F32), jnp.uint32)
    lb = lax.bitcast_convert_type(lo.astype(jnp.bfloat16).astype(F32), jnp.uint32)
    return hb | (lb >> 16)


def _unpack_pair(p):
    hi = lax.bitcast_convert_type(p & jnp.uint32(0xFFFF0000), F32)
    lo = lax.bitcast_convert_type(p << 16, F32)
    return hi, lo


def _router_kernel(x_ref, g_ref, wr_ref, o_ref, hp_ref, *, ne):
    hf = _rms(x_ref[...], g_ref[...])
    half = hf.shape[1] // 2
    hp_ref[...] = _pack_pair(hf[:, :half], hf[:, half:])
    h = hf.astype(BF16)
    logits = jnp.dot(h, wr_ref[...].astype(BF16), preferred_element_type=F32)
    lane = lax.broadcasted_iota(jnp.int32, logits.shape, 1).astype(F32)
    neg = F32(-jnp.inf)
    big = F32(LANES)
    l1 = jnp.where(lane < ne, logits, neg)
    m1 = jnp.max(l1, axis=-1, keepdims=True)
    i1 = jnp.min(jnp.where(l1 == m1, lane, big), axis=-1, keepdims=True)
    l2 = jnp.where(lane == i1, neg, l1)
    m2 = jnp.max(l2, axis=-1, keepdims=True)
    i2 = jnp.min(jnp.where(l2 == m2, lane, big), axis=-1, keepdims=True)
    e2 = jnp.exp(m2 - m1)
    den = 1.0 + e2
    o_ref[...] = (jnp.where(lane == 0.0, i1, 0.0) + jnp.where(lane == 1.0, i2, 0.0)
                  + jnp.where(lane == 2.0, 1.0 / den, 0.0) + jnp.where(lane == 3.0, e2 / den, 0.0))


def _router(x, g3, l, wr):
    m, d = x.shape
    ne = wr.shape[-1]
    assert TOP_K == 2 and ne <= LANES
    wr_pad = jnp.pad(wr, ((0, 0), (0, LANES - ne)))
    tm = _pick(m, (512, 256, 128, 64, 32, 16, 8))
    return _pcall(
        functools.partial(_router_kernel, ne=ne), name="router", grid=(m // tm,),
        in_specs=[pl.BlockSpec((tm, d), lambda i: (i, 0)), _vspec(l, d),
                  pl.BlockSpec((d, LANES), lambda i: (0, 0))],
        args=(x, g3, wr_pad),
        out_specs=[pl.BlockSpec((tm, LANES), lambda i: (i, 0)),
                   pl.BlockSpec((tm, d // 2), lambda i: (i, 0))],
        out_shape=[jax.ShapeDtypeStruct((m, LANES), F32),
                   jax.ShapeDtypeStruct((m, d // 2), jnp.uint32)],
        sem=("arbitrary",))


def _dispatch_tables(rout, ne, tg):
    m = rout.shape[0]
    na = TOP_K * m
    eid = rout[:, :TOP_K].astype(jnp.int32).T.reshape(na)
    onehot = (eid[:, None] == jnp.arange(ne, dtype=jnp.int32)[None, :]).astype(jnp.int32)
    cnt = jnp.sum(onehot, axis=0)
    rank = jnp.sum((jnp.cumsum(onehot, axis=0) - onehot) * onehot, axis=1)
    padded = ((cnt + tg - 1) // tg) * tg
    gend = jnp.cumsum(padded)
    gstart = gend - padded
    rem = cnt % tg
    first = jnp.where(rem > 0, rem, tg)
    local = jnp.where(rank < first[eid], rank, rank - first[eid] + tg)
    pos = gstart[eid] + local
    p_rows = ((na + tg - 1) // tg + ne) * tg
    n_tiles = p_rows // tg
    tok = jnp.zeros((p_rows,), jnp.int32).at[pos].set(jnp.arange(na, dtype=jnp.int32) % m)
    n_used = (gend[-1] // tg).astype(jnp.int32)
    tstart = jnp.arange(n_tiles, dtype=jnp.int32) * tg
    tstart = jnp.minimum(tstart, (n_used - 1) * tg)
    te = jnp.minimum(jnp.searchsorted(gend, tstart, side="right"), ne - 1).astype(jnp.int32)
    tidx = jnp.arange(n_tiles, dtype=jnp.int32)
    n_valid = jnp.where(tidx * tg == gstart[te], first[te], tg)
    n_valid = jnp.where(tidx < n_used, n_valid, 0).astype(jnp.int32)
    return pos, tok, te, n_used.reshape(1), n_valid, p_rows


DMA_UNROLL = 8


def _row_gather(src_hbm, dst, idx_of, sem):
    rows = dst.shape[0]
    assert rows % DMA_UNROLL == 0

    def row_copy(r, t):
        return pltpu.make_async_copy(src_hbm.at[pl.ds(t, 1), :], dst.at[pl.ds(r, 1), :], sem)

    def issue(c, carry):
        for u in range(DMA_UNROLL):
            r = c * DMA_UNROLL + u
            row_copy(r, idx_of(r)).start()
        return carry

    def drain(c, carry):
        for u in range(DMA_UNROLL):
            row_copy(c * DMA_UNROLL + u, 0).wait()
        return carry

    start = lambda: lax.fori_loop(0, rows // DMA_UNROLL, issue, 0)
    wait = lambda: lax.fori_loop(0, rows // DMA_UNROLL, drain, 0)
    return start, wait


def _moe_gather_kernel(tok_ref, hp_ref, o_ref):
    tr, d = o_ref.shape
    half = d // 2

    group = 2 * SUBLANES

    def body(g, carry):
        r0 = pl.multiple_of(g * group, group)
        rows = [hp_ref[pl.ds(tok_ref[0, r0 + u], 1), :] for u in range(group)]
        hi, lo = _unpack_pair(jnp.concatenate(rows, axis=0))
        o_ref[pl.ds(r0, group), pl.ds(0, half)] = hi.astype(o_ref.dtype)
        o_ref[pl.ds(r0, group), pl.ds(half, half)] = lo.astype(o_ref.dtype)
        return carry

    lax.fori_loop(0, tr // group, body, 0)


def _moe_gather(hp, tok, p_rows):
    m, half = hp.shape
    tr = _pick(p_rows, (GATHER_ROWS, 128, 64, 32, 16))
    nt = p_rows // tr
    return _pcall(
        _moe_gather_kernel, name="moe_gather", grid=(nt,),
        in_specs=[pl.BlockSpec((None, 1, tr), lambda i: (i, 0, 0), memory_space=pltpu.SMEM),
                  pl.BlockSpec((m, half), lambda i: (0, 0), pipeline_mode=pl.Buffered(1))],
        args=(tok.reshape(nt, 1, tr), hp),
        out_specs=[pl.BlockSpec((tr, 2 * half), lambda i: (i, 0))],
        out_shape=[jax.ShapeDtypeStruct((p_rows, 2 * half), BF16)],
        sem=("arbitrary",))[0]


def _tile_changed(te_ref):
    ti = pl.program_id(1)
    return jnp.logical_or(ti == 0, te_ref[ti] != te_ref[jnp.maximum(ti - 1, 0)])


def _valid_row_chunks(nv, o_ref, chunk):
    tg = o_ref.shape[0]

    @pl.when(nv == tg)
    def _():
        _row_chunks(tg, chunk)

    @pl.when(nv < tg)
    def _():
        def maybe(rs):
            @pl.when(rs.start < nv)
            def _():
                chunk(rs)

            @pl.when(rs.start >= nv)
            def _():
                o_ref[rs, :] = jnp.zeros((rs.stop - rs.start, o_ref.shape[1]), o_ref.dtype)

        _row_chunks(tg, maybe)


def _gmm_swiglu_kernel(te_ref, nu_ref, nv_ref, x_ref, wg_ref, wu_ref, o_ref, wgb_ref, wub_ref):
    @pl.when(_tile_changed(te_ref))
    def _():
        _cast_w(wg_ref, wgb_ref)
        _cast_w(wu_ref, wub_ref)

    def chunk(rs):
        x = x_ref[rs, :]
        g = jnp.dot(x, wgb_ref[...], preferred_element_type=F32)
        u = jnp.dot(x, wub_ref[...], preferred_element_type=F32)
        o_ref[rs, :] = (jax.nn.silu(g) * u).astype(o_ref.dtype)

    _valid_row_chunks(nv_ref[pl.program_id(1)], o_ref, chunk)


def _gmm_down_kernel(te_ref, nu_ref, nv_ref, x_ref, w_ref, o_ref, wb_ref):
    @pl.when(_tile_changed(te_ref))
    def _():
        _cast_w(w_ref, wb_ref)

    hw = o_ref.shape[1]

    def chunk(rs):
        y = jnp.dot(x_ref[rs, :], wb_ref[...], preferred_element_type=F32)
        o_ref[rs, :] = _pack_pair(y[:, :hw], y[:, hw:])

    _valid_row_chunks(nv_ref[pl.program_id(1)], o_ref, chunk)


def _gmm_specs(jm, tg, k, tn, j0=0):
    xrow = lambda j, ti, te, nu, nv: (jnp.minimum(ti, nu[0] - 1), 0)
    wsp = pl.BlockSpec((None, None, k, tn), lambda j, ti, te, nu, nv: (jm, te[ti], 0, j0 + j))
    return pl.BlockSpec((tg, k), xrow), wsp, pl.BlockSpec((tg, tn), lambda j, ti, te, nu, nv: (ti, j0 + j))


def _gmm_swiglu(xg, wg, wu, jm, te, n_used, n_valid, tg, tns):
    p, k = xg.shape
    n = wg.shape[-1]
    assert sum(tn * cnt for tn, cnt in tns) == n
    out, col = None, 0
    for tn, cnt in tns:
        assert col % tn == 0
        xs, ws, os_ = _gmm_specs(jm, tg, k, tn, col // tn)
        out = _pcall(
            _gmm_swiglu_kernel, name="gmm_swiglu", grid=(cnt, p // tg), num_scalar_prefetch=3,
            in_specs=[None, None, None, xs, ws, ws], args=(te, n_used, n_valid, xg, wg, wu),
            out_specs=[os_], out_shape=[jax.ShapeDtypeStruct((p, n), BF16)],
            scratch=[pltpu.VMEM((k, tn), BF16), pltpu.VMEM((k, tn), BF16)],
            prev=(out,), sem=("arbitrary", "arbitrary"))[0]
        col += tn * cnt
    return out


def _col_tiling(n, wide, narrow):
    nw = n // wide
    rest = n - nw * wide
    assert rest % narrow == 0 and (nw * wide) % narrow == 0
    return tuple(t for t in ((wide, nw), (narrow, rest // narrow)) if t[1])


def _gmm_down(ug, wd, jm, te, n_used, n_valid, tg, tn):
    p, k = ug.shape
    n = wd.shape[-1]
    xs, ws, _ = _gmm_specs(jm, tg, k, tn)
    return _pcall(
        _gmm_down_kernel, name="gmm_down", grid=(n // tn, p // tg), num_scalar_prefetch=3,
        in_specs=[None, None, None, xs, ws], args=(te, n_used, n_valid, ug, wd),
        out_specs=[pl.BlockSpec((tg, tn // 2), lambda j, ti, te, nu, nv: (ti, j))],
        out_shape=[jax.ShapeDtypeStruct((p, n // 2), jnp.uint32)],
        scratch=[pltpu.VMEM((k, tn), BF16)],
        sem=("arbitrary", "arbitrary"))[0]


def _moe_combine_kernel(pos_ref, nxt_ref, x_ref, rw_ref, g_ref, yg_hbm, *rest, tn, np_tiles):
    outs, (buf, sem) = rest[:-2], rest[-2:]
    tc = x_ref.shape[0]
    i = pl.program_id(0)
    slot = lax.rem(i, 2)

    def gathers(idx_ref, s):
        return [_row_gather(yg_hbm, buf.at[s, k], lambda r, k=k: idx_ref[0, k * tc + r], sem.at[s])
                for k in range(TOP_K)]

    @pl.when(i == 0)
    def _():
        for start, _ in gathers(pos_ref, 0):
            start()

    @pl.when(i + 1 < pl.num_programs(0))
    def _():
        for start, _ in gathers(nxt_ref, 1 - slot):
            start()

    for _, wait in gathers(pos_ref, slot):
        wait()
    rows = buf[slot]

    def expand(p):
        hw = tn // 2
        parts = []
        for jj in range(p.shape[1] // hw):
            parts.extend(_unpack_pair(p[:, jj * hw:(jj + 1) * hw]))
        return jnp.concatenate(parts, axis=-1)

    rw = rw_ref[...]
    y = rw[:, TOP_K:TOP_K + 1] * expand(rows[0])
    for k in range(1, TOP_K):
        y = y + rw[:, TOP_K + k:TOP_K + k + 1] * expand(rows[k])
    xn = x_ref[...] + y
    if np_tiles is None:
        outs[0][...] = xn
    else:
        yn = _rms(xn, g_ref[...])

        @pl.when(i < np_tiles)
        def _():
            outs[0][...] = yn

        @pl.when(i >= np_tiles)
        def _():
            outs[1][...] = yn


def _moe_combine(x, rout, pos, ygp, tn, g3=None, mp=None):
    m, d = x.shape
    tc = _pick(m, (COMBINE_ROWS, 64, 32, 16, 8))
    nt = m // tc
    pos_t = pos.reshape(TOP_K, nt, tc).transpose(1, 0, 2).reshape(nt, 1, TOP_K * tc)
    if g3 is None:
        g3 = jnp.ones((1, 1, d), F32)
        np_tiles = None
        out_specs = [pl.BlockSpec((tc, d), lambda i: (i, 0))]
        out_shape = [jax.ShapeDtypeStruct((m, d), F32)]
    else:
        assert mp % tc == 0
        np_tiles = mp // tc
        out_specs = [pl.BlockSpec((tc, d), lambda i: (jnp.minimum(i, np_tiles - 1), 0)),
                     pl.BlockSpec((tc, d), lambda i: (jnp.maximum(i - np_tiles, 0), 0))]
        out_shape = [jax.ShapeDtypeStruct((mp, d), F32), jax.ShapeDtypeStruct((m - mp, d), F32)]
    return _pcall(
        functools.partial(_moe_combine_kernel, tn=tn, np_tiles=np_tiles), name="moe_combine", grid=(nt,),
        in_specs=[pl.BlockSpec((None, 1, TOP_K * tc), lambda i: (i, 0, 0), memory_space=pltpu.SMEM),
                  pl.BlockSpec((None, 1, TOP_K * tc), lambda i: (jnp.minimum(i + 1, nt - 1), 0, 0),
                               memory_space=pltpu.SMEM),
                  pl.BlockSpec((tc, d), lambda i: (i, 0)),
                  pl.BlockSpec((tc, LANES), lambda i: (i, 0)),
                  _vspec(0, d),
                  pl.BlockSpec(memory_space=pl.ANY)],
        args=(pos_t, pos_t, x, rout, g3, ygp),
        out_specs=out_specs, out_shape=out_shape,
        scratch=[pltpu.VMEM((2, TOP_K, tc, d // 2), jnp.uint32), pltpu.SemaphoreType.DMA((2,))],
        sem=("arbitrary",))


def kernel(x_prompt, x_sample, state_ret, state_lru, state_conv, norm_mix, w_in, ret_gn, w_ret_o, conv_w, conv_b, w_rgate, b_rgate, w_igate, b_igate, lru_lambda, w_lru_o, w_out, norm_ffn, ffn_w_gate, ffn_w_up, ffn_w_down, moe_router, moe_w_gate, moe_w_up, moe_w_down, norm_final):
    bp, tp, d = x_prompt.shape
    bs, ts, _ = x_sample.shape
    depth, _, nh, dk, dv = state_ret.shape
    width = state_lru.shape[-1]
    ncw = conv_w.shape[1]
    ne = moe_router.shape[-1]
    assert dk == dv and nh * dk == width == d and ts == SUBLANES and ncw - 1 <= SUBLANES
    mp, ms = bp * tp, bs * ts
    m = mp + ms
    g_col = 2 * nh * dk + nh * dv
    ux_col = g_col + nh * dv
    ga_col = ux_col + 2 * width

    tm = _pick(m, (1024, 512, 256, 128, 64, 32, 16, 8))
    tm_big = _pick(m, (1536, 1024, 512, 256, 128, 64, 32, 16, 8))
    tm_half = _pick(m, (512, 256, 128, 64, 32, 16, 8))
    tn_of = lambda n: _pick(n, (1024, 512, 256, 128))

    vec3 = lambda a: a.reshape(a.shape[0], 1, a.shape[-1])
    norm_mix3, norm_ffn3, ret_gn3 = vec3(norm_mix), vec3(norm_ffn), vec3(ret_gn)
    norm_final3 = norm_final.reshape(1, 1, d)
    lru_w = (conv_w, vec3(conv_b), w_rgate, vec3(b_rgate), w_igate, vec3(b_igate), vec3(lru_lambda))
    h0_rep = jnp.repeat(state_lru, ts, axis=1)
    buf_fr = jnp.pad(state_conv, ((0, 0), (0, 0), (ts - (ncw - 1), 0), (0, 0))).reshape(depth, ms, width)

    cos_p, sin_p = _rope_tables(tp, dk // 2, 0)
    cos_s, sin_s = _rope_tables(ts, dk // 2, PAST_LEN)
    cos = jnp.concatenate([jnp.tile(cos_p, (bp, 1)), jnp.tile(cos_s, (bs, 1))], axis=0)
    sin = jnp.concatenate([jnp.tile(sin_p, (bp, 1)), jnp.tile(sin_s, (bs, 1))], axis=0)
    tn_in = _pick(nh * dk, (1024, 512, 256))

    x = jnp.concatenate([x_prompt.reshape(mp, d), x_sample.reshape(ms, d)], axis=0)
    ret_p = ret_s = y_p = y_s = None
    lrus_p, lrus_s, convs_p, convs_s = [], [], [], []
    for l in range(depth):
        h = _rmsnorm(x, norm_mix3, l, BF16)
        proj = _mm_inproj(h, w_in, (l,), cos, sin, 2 * nh * dk, dk, (g_col, ux_col),
                          (ux_col + width, ga_col), tm_big, tn_in)

        ya, ret_p = _ret_prompt(proj, ret_gn3, l, depth, bp, tp, nh, dk, dv, ret_p)
        ya, ret_s = _ret_sample(proj, ret_gn3, state_ret, l, mp, bs, ts, nh, dk, dv, ya, ret_s)
        yb, hl_p, ul_p = _lru_prompt(proj, ux_col, l, lru_w, bp, tp)
        yb, hl_s, us_s = _lru_sample(proj, ux_col, l, lru_w, mp, h0_rep, buf_fr, yb)
        lrus_p.append(hl_p[:, -1])
        lrus_s.append(hl_s.reshape(bs, ts, width)[:, -1])
        convs_p.append(ul_p[:, SUBLANES - (ncw - 1):])
        convs_s.append(us_s.reshape(bs, ts, width)[:, ts - (ncw - 1):])

        z = _mm_merge(ya, yb, w_ret_o, w_lru_o, (l,), proj, ga_col, ga_col + d, tm, tn_of(d) // 2)
        x = _mm_resid(z, w_out, (l,), x, tm, tn_of(d))
        j = l // 2
        if l % 2 == 0:
            h2 = _rmsnorm(x, norm_ffn3, l, BF16)
            ff = ffn_w_gate.shape[-1]
            u = _mm_swiglu(h2, ffn_w_gate, ffn_w_up, (j,), tm_big, _pick(ff, (512, 256, 128)))
            x = _mm_resid(u, ffn_w_down, (j,), x, tm_half, _pick(d, (512, 256, 128)))
        else:
            fe = moe_w_gate.shape[-1]
            tg = _pick(TOP_K * m, (MOE_TILE, 256, 128, 64, 32, 16))
            tn_d = _pick(d, (1024, 512, 256))
            rout, hp = _router(x, norm_ffn3, l, moe_router[j])
            pos, tok, te, n_used, n_valid, p_rows = _dispatch_tables(rout, ne, tg)
            xg = _moe_gather(hp, tok, p_rows)
            ug = _gmm_swiglu(xg, moe_w_gate, moe_w_up, j, te, n_used, n_valid, tg,
                             _col_tiling(fe, 512, _pick(fe, (256, 128))))
            ygp = _gmm_down(ug, moe_w_down, j, te, n_used, n_valid, tg, tn_d)
            if l == depth - 1:
                y_p, y_s = _moe_combine(x, rout, pos, ygp, tn_d, norm_final3, mp)
            else:
                x = _moe_combine(x, rout, pos, ygp, tn_d)[0]
    if y_p is None:
        y_p = _rmsnorm(x, norm_final3, 0, F32, 0, mp)
        y_s = _rmsnorm(x, norm_final3, 0, F32, mp, ms)
    return (y_p.reshape(bp, tp, d), y_s.reshape(bs, ts, d),
            ret_p, jnp.stack(lrus_p), jnp.stack(convs_p),
            ret_s, jnp.stack(lrus_s), jnp.stack(convs_s))
```

```python
import functools

import jax
import jax.numpy as jnp
from jax import lax
from jax.experimental import pallas as pl
from jax.experimental.pallas import tpu as pltpu

F32 = jnp.float32
BF16 = jnp.bfloat16

ROPE_BASE = 10000.0
LRU_C = 8.0
EPS = 1e-6
RET_CHUNK = 128
PAST_LEN = 16384
TOP_K = 2
SUBLANES = 8
LANES = 128
VMEM_LIMIT = 56 * 1024 * 1024
MOE_TILE = 1024
GATHER_ROWS = 256
COMBINE_ROWS = 128


def _pick(n, prefs):
    for p in prefs:
        if n % p == 0:
            return p
    return n


def _params(*sem):
    return pltpu.CompilerParams(dimension_semantics=sem, vmem_limit_bytes=VMEM_LIMIT)


def _pcall(body, *, name, grid, in_specs, args, out_specs, out_shape, sem, scratch=(), prev=(),
           num_scalar_prefetch=0):
    n_in = len(args)
    prev = tuple(prev) + (None,) * (len(out_shape) - len(prev))
    extra = [(oi, p) for oi, p in enumerate(prev) if p is not None]
    aliases = {n_in + e: oi for e, (oi, _) in enumerate(extra)}
    if extra:
        inner = body

        def body(*refs):
            return inner(*refs[:n_in], *refs[n_in + len(extra):])

    in_specs = list(in_specs) + [pl.BlockSpec(memory_space=pl.ANY)] * len(extra)
    if num_scalar_prefetch:
        grid_spec = pltpu.PrefetchScalarGridSpec(
            num_scalar_prefetch=num_scalar_prefetch, grid=grid, in_specs=in_specs[num_scalar_prefetch:],
            out_specs=tuple(out_specs), scratch_shapes=list(scratch))
        call = pl.pallas_call(body, grid_spec=grid_spec, out_shape=tuple(out_shape),
                              input_output_aliases=aliases, compiler_params=_params(*sem), name=name)
    else:
        call = pl.pallas_call(body, grid=grid, in_specs=in_specs, out_specs=tuple(out_specs),
                              out_shape=tuple(out_shape), scratch_shapes=list(scratch),
                              input_output_aliases=aliases, compiler_params=_params(*sem), name=name)
    return call(*args, *[p for _, p in extra])


def _wspec(pre, k, tn):
    return pl.BlockSpec((None,) * len(pre) + (k, tn), lambda j, i: pre + (0, j))


def _vspec(l, n):
    return pl.BlockSpec((None, 1, n), lambda *_: (l, 0, 0))


def _rms(x, g):
    return (x * lax.rsqrt(jnp.mean(x * x, axis=-1, keepdims=True) + EPS)) * g


def _rmsnorm_kernel(x_ref, g_ref, o_ref):
    o_ref[...] = _rms(x_ref[...], g_ref[...]).astype(o_ref.dtype)


def _rmsnorm(x, g3, l, out_dtype, row0=0, rows=None):
    d = x.shape[1]
    rows = x.shape[0] if rows is None else rows
    tm = _pick(rows, (512, 256, 128, 64, 32, 16, 8))
    assert row0 % tm == 0
    rb0 = row0 // tm
    return _pcall(
        _rmsnorm_kernel, name="rmsnorm", grid=(rows // tm,),
        in_specs=[pl.BlockSpec((tm, d), lambda i: (rb0 + i, 0)), _vspec(l, d)],
        args=(x, g3),
        out_specs=[pl.BlockSpec((tm, d), lambda i: (i, 0))],
        out_shape=[jax.ShapeDtypeStruct((rows, d), out_dtype)],
        sem=("arbitrary",))[0]


def _cast_w(w_ref, wb_ref):
    k = w_ref.shape[0]
    ck = _pick(k, (256, 128, 64, 32, 16))

    def body(c, carry):
        r = pl.multiple_of(c * ck, ck)
        wb_ref[pl.ds(r, ck), :] = w_ref[pl.ds(r, ck), :].astype(BF16)
        return carry

    lax.fori_loop(0, k // ck, body, 0)


def _mm_inproj_kernel(x_ref, w_ref, cos_ref, sin_ref, o_ref, wb_ref, *, nrot, dk, silu_tiles, gelu_tiles):
    j = pl.program_id(0)

    @pl.when(pl.program_id(1) == 0)
    def _():
        _cast_w(w_ref, wb_ref)

    tm, tn = o_ref.shape
    nchunk = 4 if tm % (4 * 16) == 0 else 1

    def row_chunks(epilogue):
        rc = tm // nchunk
        for c in range(nchunk):
            rs = slice(c * rc, (c + 1) * rc)
            epilogue(jnp.dot(x_ref[rs, :], wb_ref[...], preferred_element_type=F32), rs)

    def store(fn):
        def epilogue(acc, rs):
            o_ref[rs, :] = fn(acc).astype(o_ref.dtype)
        return epilogue

    in_tiles = lambda t: jnp.logical_and(j >= t[0], j < t[1])
    is_silu, is_gelu = in_tiles(silu_tiles), in_tiles(gelu_tiles)

    @pl.when(jnp.logical_and(j >= nrot, jnp.logical_not(jnp.logical_or(is_silu, is_gelu))))
    def _():
        row_chunks(store(lambda acc: acc))

    @pl.when(is_silu)
    def _():
        row_chunks(store(jax.nn.silu))

    @pl.when(is_gelu)
    def _():
        row_chunks(store(jax.nn.gelu))

    @pl.when(j < nrot)
    def _():
        scale = jnp.where(j >= nrot // 2, F32(dk ** -0.5), F32(1.0))
        half = dk // 2

        def rotary(acc, rs):
            cos = cos_ref[rs, :] * scale
            sin = sin_ref[rs, :] * scale
            for h in range(tn // dk):
                x1 = acc[:, h * dk:h * dk + half]
                x2 = acc[:, h * dk + half:(h + 1) * dk]
                o_ref[rs, h * dk:h * dk + half] = (x1 * cos - x2 * sin).astype(o_ref.dtype)
                o_ref[rs, h * dk + half:(h + 1) * dk] = (x1 * sin + x2 * cos).astype(o_ref.dtype)

        row_chunks(rotary)


def _mm_inproj(x, w, pre, cos, sin, qk_cols, dk, silu_cols, gelu_cols, tm, tn):
    m, k = x.shape
    n = w.shape[-1]
    half = cos.shape[1]
    assert qk_cols % (2 * tn) == 0 and tn % dk == 0
    assert all(c % tn == 0 for c in silu_cols + gelu_cols)
    tiles = lambda cols: (cols[0] // tn, cols[1] // tn)
    return _pcall(
        functools.partial(_mm_inproj_kernel, nrot=qk_cols // tn, dk=dk, silu_tiles=tiles(silu_cols),
                          gelu_tiles=tiles(gelu_cols)), name="mm_inproj",
        grid=(n // tn, m // tm),
        in_specs=[pl.BlockSpec((tm, k), lambda j, i: (i, 0)), _wspec(pre, k, tn),
                  pl.BlockSpec((tm, half), lambda j, i: (i, 0)),
                  pl.BlockSpec((tm, half), lambda j, i: (i, 0))],
        args=(x, w, cos, sin),
        out_specs=[pl.BlockSpec((tm, tn), lambda j, i: (i, j))],
        out_shape=[jax.ShapeDtypeStruct((m, n), BF16)],
        scratch=[pltpu.VMEM((k, tn), BF16)],
        sem=("arbitrary", "arbitrary"))[0]


def _mm_resid_kernel(x_ref, w_ref, r_ref, o_ref, wb_ref):
    @pl.when(pl.program_id(1) == 0)
    def _():
        _cast_w(w_ref, wb_ref)

    o_ref[...] = r_ref[...] + jnp.dot(x_ref[...], wb_ref[...], preferred_element_type=F32)


def _mm_resid(x, w, pre, r, tm, tn):
    m, k = x.shape
    n = w.shape[-1]
    return _pcall(
        _mm_resid_kernel, name="mm_resid", grid=(n // tn, m // tm),
        in_specs=[pl.BlockSpec((tm, k), lambda j, i: (i, 0)), _wspec(pre, k, tn),
                  pl.BlockSpec((tm, tn), lambda j, i: (i, j))],
        args=(x, w, r),
        out_specs=[pl.BlockSpec((tm, tn), lambda j, i: (i, j))],
        out_shape=[jax.ShapeDtypeStruct((m, n), F32)],
        scratch=[pltpu.VMEM((k, tn), BF16)],
        sem=("arbitrary", "arbitrary"))[0]


def _row_chunks(rows, fn, nchunk=4):
    n = nchunk if rows % (nchunk * 16) == 0 else 1
    rc = rows // n
    for c in range(n):
        fn(slice(c * rc, (c + 1) * rc))


def _swiglu_rows(x_ref, wgb_ref, wub_ref, o_ref):
    def chunk(rs):
        x = x_ref[rs, :]
        g = jnp.dot(x, wgb_ref[...], preferred_element_type=F32)
        u = jnp.dot(x, wub_ref[...], preferred_element_type=F32)
        o_ref[rs, :] = (jax.nn.silu(g) * u).astype(o_ref.dtype)

    _row_chunks(o_ref.shape[0], chunk)


def _mm_swiglu_kernel(x_ref, wg_ref, wu_ref, o_ref, wgb_ref, wub_ref):
    @pl.when(pl.program_id(1) == 0)
    def _():
        _cast_w(wg_ref, wgb_ref)
        _cast_w(wu_ref, wub_ref)

    _swiglu_rows(x_ref, wgb_ref, wub_ref, o_ref)


def _mm_swiglu(x, wg, wu, pre, tm, tn):
    m, k = x.shape
    n = wg.shape[-1]
    return _pcall(
        _mm_swiglu_kernel, name="mm_swiglu", grid=(n // tn, m // tm),
        in_specs=[pl.BlockSpec((tm, k), lambda j, i: (i, 0)), _wspec(pre, k, tn), _wspec(pre, k, tn)],
        args=(x, wg, wu),
        out_specs=[pl.BlockSpec((tm, tn), lambda j, i: (i, j))],
        out_shape=[jax.ShapeDtypeStruct((m, n), BF16)],
        scratch=[pltpu.VMEM((k, tn), BF16), pltpu.VMEM((k, tn), BF16)],
        sem=("arbitrary", "arbitrary"))[0]


def _mm_merge_kernel(a_ref, b_ref, wa_ref, wb_ref, ga_ref, gb_ref, o_ref, wab_ref, wbb_ref):
    @pl.when(pl.program_id(1) == 0)
    def _():
        _cast_w(wa_ref, wab_ref)
        _cast_w(wb_ref, wbb_ref)

    def chunk(rs):
        ya = jnp.dot(a_ref[rs, :], wab_ref[...], preferred_element_type=F32)
        yb = jnp.dot(b_ref[rs, :], wbb_ref[...], preferred_element_type=F32)
        ga = ga_ref[rs, :].astype(F32)
        gb = gb_ref[rs, :].astype(F32)
        o_ref[rs, :] = (jax.nn.sigmoid(ga) * ya + jax.nn.sigmoid(gb) * yb).astype(o_ref.dtype)

    _row_chunks(o_ref.shape[0], chunk)


def _mm_merge(a, b, wa, wb, pre, proj, ga_col, gb_col, tm, tn):
    m, k = a.shape
    n = wa.shape[-1]
    ga_blk, gb_blk = ga_col // tn, gb_col // tn
    return _pcall(
        _mm_merge_kernel, name="mm_merge", grid=(n // tn, m // tm),
        in_specs=[pl.BlockSpec((tm, k), lambda j, i: (i, 0)),
                  pl.BlockSpec((tm, k), lambda j, i: (i, 0)),
                  _wspec(pre, k, tn), _wspec(pre, k, tn),
                  pl.BlockSpec((tm, tn), lambda j, i: (i, ga_blk + j)),
                  pl.BlockSpec((tm, tn), lambda j, i: (i, gb_blk + j))],
        args=(a, b, wa, wb, proj, proj),
        out_specs=[pl.BlockSpec((tm, tn), lambda j, i: (i, j))],
        out_shape=[jax.ShapeDtypeStruct((m, n), BF16)],
        scratch=[pltpu.VMEM((k, tn), BF16), pltpu.VMEM((k, tn), BF16)],
        sem=("arbitrary", "arbitrary"))[0]


def _rope_kernel(inv_ref, cos_ref, sin_ref, *, pos0):
    t, half = cos_ref.shape
    pos = lax.broadcasted_iota(jnp.int32, (t, half), 0).astype(F32) + F32(pos0)
    ang = pos * inv_ref[...]
    cos_ref[...] = jnp.cos(ang)
    sin_ref[...] = jnp.sin(ang)


def _rope_tables(t, half, pos0):
    inv = ROPE_BASE ** (-jnp.arange(half, dtype=F32) / half)
    return pl.pallas_call(
        functools.partial(_rope_kernel, pos0=pos0),
        out_shape=(jax.ShapeDtypeStruct((t, half), F32), jax.ShapeDtypeStruct((t, half), F32)),
        name="rope_tables",
    )(inv.reshape(1, half))


def _decay_tables(c, h, dk, dv):
    log_g = jnp.log1p(-jnp.exp2(-5.0 - jnp.arange(h, dtype=F32)))
    idx = jnp.arange(c)
    rel = idx[:, None] - idx[None, :]
    dmask = jnp.where(rel[None] >= 0,
                      jnp.exp(jnp.maximum(rel, 0)[None].astype(F32) * log_g[:, None, None]), 0.0)
    xi = jnp.exp((idx + 1).astype(F32)[None, :] * log_g[:, None])
    zeta = jnp.exp((c - 1 - idx).astype(F32)[None, :] * log_g[:, None])
    g_c = jnp.exp(c * log_g)
    xi_t = jnp.broadcast_to(xi[:, :, None], (h, c, dv))
    zeta_t = jnp.broadcast_to(zeta[:, :, None], (h, c, dk))
    return dmask, xi_t, zeta_t, g_c


def _ret_head(qb, kb, v, g, dm, xi, zt, gn, s, gc):
    scores = lax.dot_general(qb, kb, (((1,), (1,)), ((), ())), preferred_element_type=F32)
    intra = jnp.dot((scores * dm).astype(BF16), v, preferred_element_type=F32)
    cross = jnp.dot(qb, s.astype(BF16), preferred_element_type=F32) * xi
    kz = (kb.astype(F32) * zt).astype(BF16)
    upd = lax.dot_general(kz, v, (((0,), (0,)), ((), ())), preferred_element_type=F32)
    s_new = gc * s + upd
    o = intra + cross
    mu = jnp.mean(o, axis=-1, keepdims=True)
    d = o - mu
    var = jnp.mean(d * d, axis=-1, keepdims=True)
    y = d * lax.rsqrt(var + EPS) * gn
    return (g * y).astype(BF16), s_new


def _ret_prompt_kernel(gc_ref, q_ref, k_ref, v_ref, g_ref, dm_ref, xi_ref, zt_ref,
                       gn_ref, o_ref, s_ref, *, nh, dk, dv):
    @pl.when(pl.program_id(1) == 0)
    def _():
        s_ref[...] = jnp.zeros_like(s_ref)

    c = dm_ref.shape[1]
    for h in range(nh):
        s = s_ref[h]
        for ci in range(q_ref.shape[0] // c):
            rs = slice(ci * c, (ci + 1) * c)
            out, s = _ret_head(
                q_ref[rs, h * dk:(h + 1) * dk], k_ref[rs, h * dk:(h + 1) * dk],
                v_ref[rs, h * dv:(h + 1) * dv], g_ref[rs, h * dv:(h + 1) * dv].astype(F32),
                dm_ref[h], xi_ref[h], zt_ref[h], gn_ref[:, h * dv:(h + 1) * dv], s, gc_ref[h])
            o_ref[rs, h * dv:(h + 1) * dv] = out
        s_ref[h] = s


def _ret_prompt(proj, gn3, l, depth, b, t, nh, dk, dv, s_prev):
    cl = RET_CHUNK if t % RET_CHUNK == 0 else t
    cps = _pick(t // cl, (4, 2, 1))
    c = cl * cps
    nc = t // c
    w = nh * dk
    m = proj.shape[0]
    dmask, xi_t, zeta_t, g_c = _decay_tables(cl, nh, dk, dv)
    row = lambda bi, ci: bi * nc + ci
    full3 = lambda bi, ci: (0, 0, 0)
    return _pcall(
        functools.partial(_ret_prompt_kernel, nh=nh, dk=dk, dv=dv), name="ret_prompt", grid=(b, nc),
        in_specs=[pl.BlockSpec(memory_space=pltpu.SMEM),
                  pl.BlockSpec((c, w), lambda bi, ci: (row(bi, ci), 0)),
                  pl.BlockSpec((c, w), lambda bi, ci: (row(bi, ci), 1)),
                  pl.BlockSpec((c, w), lambda bi, ci: (row(bi, ci), 2)),
                  pl.BlockSpec((c, w), lambda bi, ci: (row(bi, ci), 3)),
                  pl.BlockSpec((nh, cl, cl), full3),
                  pl.BlockSpec((nh, cl, dv), full3),
                  pl.BlockSpec((nh, cl, dk), full3),
                  _vspec(l, nh * dv)],
        args=(g_c, proj, proj, proj, proj, dmask, xi_t, zeta_t, gn3),
        out_specs=[pl.BlockSpec((c, nh * dv), lambda bi, ci: (row(bi, ci), 0)),
                   pl.BlockSpec((None, None, nh, dk, dv), lambda bi, ci: (l, bi, 0, 0, 0))],
        out_shape=[jax.ShapeDtypeStruct((m, nh * dv), BF16),
                   jax.ShapeDtypeStruct((depth, b, nh, dk, dv), F32)],
        prev=(None, s_prev),
        sem=("arbitrary", "arbitrary"))


def _ret_sample_kernel(gc_ref, q_ref, k_ref, v_ref, g_ref, dm_ref, xi_ref, zt_ref,
                       gn_ref, s0_ref, o_ref, s_ref, *, nh, dk, dv, bb, ts):
    q = q_ref[...].astype(F32)
    k = k_ref[...].astype(F32)
    v = v_ref[...].astype(F32)
    g = g_ref[...].astype(F32)
    seqs = []
    for i in range(bb):
        r0, r1 = i * ts, (i + 1) * ts
        outs = []
        for h in range(nh):
            out, s_new = _ret_head(
                q[r0:r1, h * dk:(h + 1) * dk].astype(BF16), k[r0:r1, h * dk:(h + 1) * dk].astype(BF16),
                v[r0:r1, h * dv:(h + 1) * dv].astype(BF16), g[r0:r1, h * dv:(h + 1) * dv],
                dm_ref[h], xi_ref[h], zt_ref[h], gn_ref[:, h * dv:(h + 1) * dv],
                s0_ref[i, h], gc_ref[h])
            s_ref[i, h] = s_new
            outs.append(out.astype(F32))
        seqs.append(jnp.concatenate(outs, axis=-1))
    o_ref[...] = jnp.concatenate(seqs, axis=0).astype(o_ref.dtype)


def _ret_sample(proj, gn3, s0_all, l, row0, bs, ts, nh, dk, dv, ya_prev, s_prev):
    bb = _pick(bs, (4, 2, 1))
    rows = bb * ts
    w = nh * dk
    depth = s0_all.shape[0]
    dmask, xi_t, zeta_t, g_c = _decay_tables(ts, nh, dk, dv)
    assert row0 % rows == 0
    rb0 = row0 // rows
    full3 = lambda i: (0, 0, 0)
    return _pcall(
        functools.partial(_ret_sample_kernel, nh=nh, dk=dk, dv=dv, bb=bb, ts=ts), name="ret_sample",
        grid=(bs // bb,),
        in_specs=[pl.BlockSpec(memory_space=pltpu.SMEM),
                  pl.BlockSpec((rows, w), lambda i: (rb0 + i, 0)),
                  pl.BlockSpec((rows, w), lambda i: (rb0 + i, 1)),
                  pl.BlockSpec((rows, w), lambda i: (rb0 + i, 2)),
                  pl.BlockSpec((rows, w), lambda i: (rb0 + i, 3)),
                  pl.BlockSpec((nh, ts, ts), full3),
                  pl.BlockSpec((nh, ts, dv), full3),
                  pl.BlockSpec((nh, ts, dk), full3),
                  _vspec(l, nh * dv),
                  pl.BlockSpec((None, bb, nh, dk, dv), lambda i: (l, i, 0, 0, 0))],
        args=(g_c, proj, proj, proj, proj, dmask, xi_t, zeta_t, gn3, s0_all),
        out_specs=[pl.BlockSpec((rows, nh * dv), lambda i: (rb0 + i, 0)),
                   pl.BlockSpec((None, bb, nh, dk, dv), lambda i: (l, i, 0, 0, 0))],
        out_shape=[jax.ShapeDtypeStruct(ya_prev.shape, BF16),
                   jax.ShapeDtypeStruct((depth, bs, nh, dk, dv), F32)],
        prev=(ya_prev, s_prev),
        sem=("arbitrary",))


def _lru_gates(uc, wr_ref, br_ref, wi_ref, bi_ref, lam_ref):
    nb, lb, _ = wr_ref.shape
    ucb = uc.astype(BF16)
    rl, il = [], []
    for n in range(nb):
        xb = ucb[:, n * lb:(n + 1) * lb]
        rl.append(jnp.dot(xb, wr_ref[n].astype(BF16), preferred_element_type=F32))
        il.append(jnp.dot(xb, wi_ref[n].astype(BF16), preferred_element_type=F32))
    r = jax.nn.sigmoid(jnp.concatenate(rl, axis=-1) + br_ref[...])
    i = jax.nn.sigmoid(jnp.concatenate(il, axis=-1) + bi_ref[...])
    log_a = -LRU_C * r * jax.nn.softplus(-lam_ref[...])
    a = jnp.exp(log_a)
    z = -jnp.tanh(log_a) * (a * a + 1.0)
    bx = jnp.where(z == 0.0, 0.0, z * lax.rsqrt(z)) * (i * uc)
    return a, bx


def _blocks(x):
    return [x[lo:lo + SUBLANES, :] for lo in range(0, x.shape[0], SUBLANES)]


def _conv_blocks(u_blocks, prev_blocks, cw_ref, cb_ref, rows):
    ncw = cw_ref.shape[0]
    rolled = {}

    def rot(x, s):
        key = (id(x), s)
        if key not in rolled:
            rolled[key] = pltpu.roll(x, s, axis=0)
        return rolled[key]

    out = []
    for ub, pb in zip(u_blocks, prev_blocks):
        uc = cb_ref[...]
        for j in range(ncw):
            s = ncw - 1 - j
            term = ub if s == 0 else jnp.where(rows >= s, rot(ub, s), rot(pb, s))
            uc = uc + term * cw_ref[j:j + 1, :]
        out.append(uc)
    return jnp.concatenate(out, axis=0)


def _scan_block(a, bx, rows):
    d = 1
    while d < SUBLANES:
        m = rows >= d
        bx = bx + a * jnp.where(m, pltpu.roll(bx, d, axis=0), 0.0)
        a = a * jnp.where(m, pltpu.roll(a, d, axis=0), 1.0)
        d *= 2
    return a, bx


def _lru_prompt_kernel(ux_ref, uy_ref, cw_ref, cb_ref, wr_ref, br_ref, wi_ref, bi_ref, lam_ref,
                       o_ref, hl_ref, ul_ref, hc_ref):
    r, cw = ux_ref.shape

    @pl.when(pl.program_id(2) == 0)
    def _():
        hc_ref[...] = jnp.zeros_like(hc_ref)
        ul_ref[...] = jnp.zeros_like(ul_ref)

    rows = lax.broadcasted_iota(jnp.int32, (SUBLANES, cw), 0)
    ub = _blocks(ux_ref[...].astype(F32))
    uc = _conv_blocks(ub, [ul_ref[...]] + ub[:-1], cw_ref, cb_ref, rows)
    ul_ref[...] = ub[-1]
    a, bx = _lru_gates(uc, wr_ref, br_ref, wi_ref, bi_ref, lam_ref)
    carry = hc_ref[...]
    hs = []
    for ab, bb in zip(_blocks(a), _blocks(bx)):
        ab, bb = _scan_block(ab, bb, rows)
        hb = ab * carry + bb
        hs.append(hb)
        carry = jnp.broadcast_to(hb[SUBLANES - 1:SUBLANES, :], (SUBLANES, cw))
    hc_ref[...] = carry
    hl_ref[...] = hs[-1]
    h = jnp.concatenate(hs, axis=0)
    o_ref[...] = (h * uy_ref[...].astype(F32)).astype(o_ref.dtype)


def _lru_sample_kernel(ux_ref, uy_ref, cw_ref, cb_ref, wr_ref, br_ref, wi_ref, bi_ref, lam_ref,
                       h0_ref, buf_ref, o_ref, hl_ref, us_ref):
    r, cw = ux_ref.shape
    rows = lax.broadcasted_iota(jnp.int32, (SUBLANES, cw), 0)
    u = ux_ref[...].astype(F32)
    us_ref[...] = u
    uc = _conv_blocks(_blocks(u), _blocks(buf_ref[...]), cw_ref, cb_ref, rows)
    a, bx = _lru_gates(uc, wr_ref, br_ref, wi_ref, bi_ref, lam_ref)
    hs = []
    for ab, bb, h0 in zip(_blocks(a), _blocks(bx), _blocks(h0_ref[...])):
        ab, bb = _scan_block(ab, bb, rows)
        hs.append(ab * h0 + bb)
    h = jnp.concatenate(hs, axis=0)
    hl_ref[...] = h
    o_ref[...] = (h * uy_ref[...].astype(F32)).astype(o_ref.dtype)


def _lru_specs(l, r, cw, ncw, lb, ux_blk, uy_blk, row_of, cb_of):
    nbc = cw // lb
    vec = pl.BlockSpec((None, 1, cw), lambda *g: (l, 0, cb_of(*g)))
    gate = pl.BlockSpec((None, nbc, lb, lb), lambda *g: (l, cb_of(*g), 0, 0))
    return [pl.BlockSpec((r, cw), lambda *g: (row_of(*g), ux_blk + cb_of(*g))),
            pl.BlockSpec((r, cw), lambda *g: (row_of(*g), uy_blk + cb_of(*g))),
            pl.BlockSpec((None, ncw, cw), lambda *g: (l, 0, cb_of(*g))),
            vec, gate, vec, gate, vec, vec]


def _lru_prompt(proj, ux_col, l, lru_w, b, t):
    conv_w, conv_b3, w_r, b_r3, w_i, b_i3, lam3 = lru_w
    width = lam3.shape[-1]
    lb = w_r.shape[-1]
    ncw = conv_w.shape[1]
    m = proj.shape[0]
    cw = _pick(width, (1024, 512, 256, 128))
    r = _pick(t, (256, 128, 64, 32, 16, 8))
    nt = t // r
    specs = _lru_specs(l, r, cw, ncw, lb, ux_col // cw, (ux_col + width) // cw,
                       lambda bi, cb, ti: bi * nt + ti, lambda bi, cb, ti: cb)
    last8 = pl.BlockSpec((None, SUBLANES, cw), lambda bi, cb, ti: (bi, 0, cb))
    return _pcall(
        _lru_prompt_kernel, name="lru_prompt", grid=(b, width // cw, nt),
        in_specs=specs,
        args=(proj, proj, conv_w, conv_b3, w_r, b_r3, w_i, b_i3, lam3),
        out_specs=[pl.BlockSpec((r, cw), lambda bi, cb, ti: (bi * nt + ti, cb)), last8, last8],
        out_shape=[jax.ShapeDtypeStruct((m, width), BF16),
                   jax.ShapeDtypeStruct((b, SUBLANES, width), F32),
                   jax.ShapeDtypeStruct((b, SUBLANES, width), F32)],
        scratch=[pltpu.VMEM((SUBLANES, cw), F32)],
        sem=("arbitrary", "arbitrary", "arbitrary"))


def _lru_ret_kernel(*refs, nh, dk, dv, bb, ts):
    lru_in, ret_in = refs[:9], refs[9:19]
    yb_ref, hl_ref, ul_ref, ya_ref, s_ref, hc_ref = refs[19:]
    _lru_prompt_kernel(*lru_in, yb_ref, hl_ref, ul_ref, hc_ref)
    _ret_sample_kernel(*ret_in, ya_ref, s_ref, nh=nh, dk=dk, dv=dv, bb=bb, ts=ts)


def _fused_seqs_per_step(width, t, b, bs, ts):
    cw = _pick(width, (1024, 512, 256, 128))
    r = _pick(t, (256, 128, 64, 32, 16, 8))
    steps = b * (width // cw) * (t // r)
    if bs % steps:
        return None
    bb = bs // steps
    return bb if bb <= 4 and (bb * ts) % 16 == 0 else None


def _lru_prompt_ret_sample(proj, ux_col, l, lru_w, b, t, gn3, s0_all, row0, bs, ts, nh, dk, dv,
                           ya_prev, s_prev):
    conv_w, conv_b3, w_r, b_r3, w_i, b_i3, lam3 = lru_w
    width = lam3.shape[-1]
    lb = w_r.shape[-1]
    ncw = conv_w.shape[1]
    m = proj.shape[0]
    depth = s0_all.shape[0]
    cw = _pick(width, (1024, 512, 256, 128))
    r = _pick(t, (256, 128, 64, 32, 16, 8))
    nt, ncb = t // r, width // cw
    bb = _fused_seqs_per_step(width, t, b, bs, ts)
    rows = bb * ts
    w = nh * dk
    assert row0 % rows == 0
    rb0 = row0 // rows
    lin = lambda bi, cb, ti: (bi * ncb + cb) * nt + ti
    dmask, xi_t, zeta_t, g_c = _decay_tables(ts, nh, dk, dv)
    lru_specs = _lru_specs(l, r, cw, ncw, lb, ux_col // cw, (ux_col + width) // cw,
                           lambda bi, cb, ti: bi * nt + ti, lambda bi, cb, ti: cb)
    full3 = lambda *g: (0, 0, 0)
    qkvg = [pl.BlockSpec((rows, w), lambda *g, c=c: (rb0 + lin(*g), c)) for c in range(4)]
    state = pl.BlockSpec((None, bb, nh, dk, dv), lambda *g: (l, lin(*g), 0, 0, 0))
    ret_specs = ([pl.BlockSpec(memory_space=pltpu.SMEM)] + qkvg +
                 [pl.BlockSpec((nh, ts, ts), full3), pl.BlockSpec((nh, ts, dv), full3),
                  pl.BlockSpec((nh, ts, dk), full3), _vspec(l, nh * dv), state])
    last8 = pl.BlockSpec((None, SUBLANES, cw), lambda bi, cb, ti: (bi, 0, cb))
    return _pcall(
        functools.partial(_lru_ret_kernel, nh=nh, dk=dk, dv=dv, bb=bb, ts=ts), name="lru_prompt_ret_sample",
        grid=(b, ncb, nt),
        in_specs=lru_specs + ret_specs,
        args=(proj, proj, conv_w, conv_b3, w_r, b_r3, w_i, b_i3, lam3,
              g_c, proj, proj, proj, proj, dmask, xi_t, zeta_t, gn3, s0_all),
        out_specs=[pl.BlockSpec((r, cw), lambda bi, cb, ti: (bi * nt + ti, cb)), last8, last8,
                   pl.BlockSpec((rows, nh * dv), lambda *g: (rb0 + lin(*g), 0)), state],
        out_shape=[jax.ShapeDtypeStruct((m, width), BF16),
                   jax.ShapeDtypeStruct((b, SUBLANES, width), F32),
                   jax.ShapeDtypeStruct((b, SUBLANES, width), F32),
                   jax.ShapeDtypeStruct(ya_prev.shape, BF16),
                   jax.ShapeDtypeStruct((depth, bs, nh, dk, dv), F32)],
        prev=(None, None, None, ya_prev, s_prev),
        scratch=[pltpu.VMEM((SUBLANES, cw), F32)],
        sem=("arbitrary", "arbitrary", "arbitrary"))


def _lru_sample(proj, ux_col, l, lru_w, row0, h0_rep, buf_fr, yb_prev):
    conv_w, conv_b3, w_r, b_r3, w_i, b_i3, lam3 = lru_w
    width = lam3.shape[-1]
    lb = w_r.shape[-1]
    ncw = conv_w.shape[1]
    ms = h0_rep.shape[1]
    cw = _pick(width, (1024, 512, 256, 128))
    r = _pick(ms, (256, 128, 64, 32, 16, 8))
    assert row0 % r == 0
    rb0 = row0 // r
    specs = _lru_specs(l, r, cw, ncw, lb, ux_col // cw, (ux_col + width) // cw,
                       lambda ri, cb: rb0 + ri, lambda ri, cb: cb)
    st = pl.BlockSpec((None, r, cw), lambda ri, cb: (l, ri, cb))
    f32rows = pl.BlockSpec((r, cw), lambda ri, cb: (ri, cb))
    return _pcall(
        _lru_sample_kernel, name="lru_sample", grid=(ms // r, width // cw),
        in_specs=specs + [st, st],
        args=(proj, proj, conv_w, conv_b3, w_r, b_r3, w_i, b_i3, lam3, h0_rep, buf_fr),
        out_specs=[pl.BlockSpec((r, cw), lambda ri, cb: (rb0 + ri, cb)), f32rows, f32rows],
        out_shape=[jax.ShapeDtypeStruct(yb_prev.shape, BF16),
                   jax.ShapeDtypeStruct((ms, width), F32), jax.ShapeDtypeStruct((ms, width), F32)],
        prev=(yb_prev,),
        sem=("arbitrary", "arbitrary"))


def _pack_pair(hi, lo):
    hb = lax.bitcast_convert_type(hi.astype(jnp.bfloat16).astype(F32), jnp.uint32)
    lb = lax.bitcast_convert_type(lo.astype(jnp.bfloat16).astype(F32), jnp.uint32)
    return hb | (lb >> 16)


def _unpack_pair(p):
    hi = lax.bitcast_convert_type(p & jnp.uint32(0xFFFF0000), F32)
    lo = lax.bitcast_convert_type(p << 16, F32)
    return hi, lo


def _router_kernel(x_ref, g_ref, wr_ref, o_ref, hp_ref, *, ne):
    hf = _rms(x_ref[...], g_ref[...])
    half = hf.shape[1] // 2
    hp_ref[...] = _pack_pair(hf[:, :half], hf[:, half:])
    h = hf.astype(BF16)
    logits = jnp.dot(h, wr_ref[...].astype(BF16), preferred_element_type=F32)
    lane = lax.broadcasted_iota(jnp.int32, logits.shape, 1).astype(F32)
    neg = F32(-jnp.inf)
    big = F32(LANES)
    l1 = jnp.where(lane < ne, logits, neg)
    m1 = jnp.max(l1, axis=-1, keepdims=True)
    i1 = jnp.min(jnp.where(l1 == m1, lane, big), axis=-1, keepdims=True)
    l2 = jnp.where(lane == i1, neg, l1)
    m2 = jnp.max(l2, axis=-1, keepdims=True)
    i2 = jnp.min(jnp.where(l2 == m2, lane, big), axis=-1, keepdims=True)
    e2 = jnp.exp(m2 - m1)
    den = 1.0 + e2
    o_ref[...] = (jnp.where(lane == 0.0, i1, 0.0) + jnp.where(lane == 1.0, i2, 0.0)
                  + jnp.where(lane == 2.0, 1.0 / den, 0.0) + jnp.where(lane == 3.0, e2 / den, 0.0))


def _router(x, g3, l, wr):
    m, d = x.shape
    ne = wr.shape[-1]
    assert TOP_K == 2 and ne <= LANES
    wr_pad = jnp.pad(wr, ((0, 0), (0, LANES - ne)))
    tm = _pick(m, (512, 256, 128, 64, 32, 16, 8))
    return _pcall(
        functools.partial(_router_kernel, ne=ne), name="router", grid=(m // tm,),
        in_specs=[pl.BlockSpec((tm, d), lambda i: (i, 0)), _vspec(l, d),
                  pl.BlockSpec((d, LANES), lambda i: (0, 0))],
        args=(x, g3, wr_pad),
        out_specs=[pl.BlockSpec((tm, LANES), lambda i: (i, 0)),
                   pl.BlockSpec((tm, d // 2), lambda i: (i, 0))],
        out_shape=[jax.ShapeDtypeStruct((m, LANES), F32),
                   jax.ShapeDtypeStruct((m, d // 2), jnp.uint32)],
        sem=("arbitrary",))


def _dispatch_tables(rout, ne, tg):
    m = rout.shape[0]
    na = TOP_K * m
    eid = rout[:, :TOP_K].astype(jnp.int32).T.reshape(na)
    onehot = (eid[:, None] == jnp.arange(ne, dtype=jnp.int32)[None, :]).astype(jnp.int32)
    cnt = jnp.sum(onehot, axis=0)
    rank = jnp.sum((jnp.cumsum(onehot, axis=0) - onehot) * onehot, axis=1)
    padded = ((cnt + tg - 1) // tg) * tg
    gend = jnp.cumsum(padded)
    gstart = gend - padded
    rem = cnt % tg
    first = jnp.where(rem > 0, rem, tg)
    local = jnp.where(rank < first[eid], rank, rank - first[eid] + tg)
    pos = gstart[eid] + local
    p_rows = ((na + tg - 1) // tg + ne) * tg
    n_tiles = p_rows // tg
    tok = jnp.zeros((p_rows,), jnp.int32).at[pos].set(jnp.arange(na, dtype=jnp.int32) % m)
    n_used = (gend[-1] // tg).astype(jnp.int32)
    tstart = jnp.arange(n_tiles, dtype=jnp.int32) * tg
    tstart = jnp.minimum(tstart, (n_used - 1) * tg)
    te = jnp.minimum(jnp.searchsorted(gend, tstart, side="right"), ne - 1).astype(jnp.int32)
    tidx = jnp.arange(n_tiles, dtype=jnp.int32)
    n_valid = jnp.where(tidx * tg == gstart[te], first[te], tg)
    n_valid = jnp.where(tidx < n_used, n_valid, 0).astype(jnp.int32)
    return pos, tok, te, n_used.reshape(1), n_valid, p_rows


DMA_UNROLL = 8


def _row_gather(src_hbm, dst, idx_of, sem):
    rows = dst.shape[0]
    assert rows % DMA_UNROLL == 0

    def row_copy(r, t):
        return pltpu.make_async_copy(src_hbm.at[pl.ds(t, 1), :], dst.at[pl.ds(r, 1), :], sem)

    def issue(c, carry):
        for u in range(DMA_UNROLL):
            r = c * DMA_UNROLL + u
            row_copy(r, idx_of(r)).start()
        return carry

    def drain(c, carry):
        for u in range(DMA_UNROLL):
            row_copy(c * DMA_UNROLL + u, 0).wait()
        return carry

    start = lambda: lax.fori_loop(0, rows // DMA_UNROLL, issue, 0)
    wait = lambda: lax.fori_loop(0, rows // DMA_UNROLL, drain, 0)
    return start, wait


def _moe_gather_kernel(tok_ref, hp_ref, o_ref):
    tr, d = o_ref.shape
    half = d // 2

    group = 2 * SUBLANES

    def body(g, carry):
        r0 = pl.multiple_of(g * group, group)
        rows = [hp_ref[pl.ds(tok_ref[0, r0 + u], 1), :] for u in range(group)]
        hi, lo = _unpack_pair(jnp.concatenate(rows, axis=0))
        o_ref[pl.ds(r0, group), pl.ds(0, half)] = hi.astype(o_ref.dtype)
        o_ref[pl.ds(r0, group), pl.ds(half, half)] = lo.astype(o_ref.dtype)
        return carry

    lax.fori_loop(0, tr // group, body, 0)


def _moe_gather(hp, tok, p_rows):
    m, half = hp.shape
    tr = _pick(p_rows, (GATHER_ROWS, 128, 64, 32, 16))
    nt = p_rows // tr
    return _pcall(
        _moe_gather_kernel, name="moe_gather", grid=(nt,),
        in_specs=[pl.BlockSpec((None, 1, tr), lambda i: (i, 0, 0), memory_space=pltpu.SMEM),
                  pl.BlockSpec((m, half), lambda i: (0, 0), pipeline_mode=pl.Buffered(1))],
        args=(tok.reshape(nt, 1, tr), hp),
        out_specs=[pl.BlockSpec((tr, 2 * half), lambda i: (i, 0))],
        out_shape=[jax.ShapeDtypeStruct((p_rows, 2 * half), BF16)],
        sem=("arbitrary",))[0]


def _tile_changed(te_ref):
    ti = pl.program_id(1)
    return jnp.logical_or(ti == 0, te_ref[ti] != te_ref[jnp.maximum(ti - 1, 0)])


def _valid_row_chunks(nv, o_ref, chunk):
    tg = o_ref.shape[0]

    @pl.when(nv == tg)
    def _():
        _row_chunks(tg, chunk)

    @pl.when(nv < tg)
    def _():
        def maybe(rs):
            @pl.when(rs.start < nv)
            def _():
                chunk(rs)

            @pl.when(rs.start >= nv)
            def _():
                o_ref[rs, :] = jnp.zeros((rs.stop - rs.start, o_ref.shape[1]), o_ref.dtype)

        _row_chunks(tg, maybe)


def _gmm_swiglu_kernel(te_ref, nu_ref, nv_ref, x_ref, wg_ref, wu_ref, o_ref, wgb_ref, wub_ref):
    @pl.when(_tile_changed(te_ref))
    def _():
        _cast_w(wg_ref, wgb_ref)
        _cast_w(wu_ref, wub_ref)

    def chunk(rs):
        x = x_ref[rs, :]
        g = jnp.dot(x, wgb_ref[...], preferred_element_type=F32)
        u = jnp.dot(x, wub_ref[...], preferred_element_type=F32)
        o_ref[rs, :] = (jax.nn.silu(g) * u).astype(o_ref.dtype)

    _valid_row_chunks(nv_ref[pl.program_id(1)], o_ref, chunk)


def _gmm_down_kernel(te_ref, nu_ref, nv_ref, x_ref, w_ref, o_ref, wb_ref):
    @pl.when(_tile_changed(te_ref))
    def _():
        _cast_w(w_ref, wb_ref)

    hw = o_ref.shape[1]

    def chunk(rs):
        y = jnp.dot(x_ref[rs, :], wb_ref[...], preferred_element_type=F32)
        o_ref[rs, :] = _pack_pair(y[:, :hw], y[:, hw:])

    _valid_row_chunks(nv_ref[pl.program_id(1)], o_ref, chunk)


def _gmm_specs(jm, tg, k, tn, j0=0):
    xrow = lambda j, ti, te, nu, nv: (jnp.minimum(ti, nu[0] - 1), 0)
    wsp = pl.BlockSpec((None, None, k, tn), lambda j, ti, te, nu, nv: (jm, te[ti], 0, j0 + j))
    return pl.BlockSpec((tg, k), xrow), wsp, pl.BlockSpec((tg, tn), lambda j, ti, te, nu, nv: (ti, j0 + j))


def _gmm_swiglu(xg, wg, wu, jm, te, n_used, n_valid, tg, tns):
    p, k = xg.shape
    n = wg.shape[-1]
    assert sum(tn * cnt for tn, cnt in tns) == n
    out, col = None, 0
    for tn, cnt in tns:
        assert col % tn == 0
        xs, ws, os_ = _gmm_specs(jm, tg, k, tn, col // tn)
        out = _pcall(
            _gmm_swiglu_kernel, name="gmm_swiglu", grid=(cnt, p // tg), num_scalar_prefetch=3,
            in_specs=[None, None, None, xs, ws, ws], args=(te, n_used, n_valid, xg, wg, wu),
            out_specs=[os_], out_shape=[jax.ShapeDtypeStruct((p, n), BF16)],
            scratch=[pltpu.VMEM((k, tn), BF16), pltpu.VMEM((k, tn), BF16)],
            prev=(out,), sem=("arbitrary", "arbitrary"))[0]
        col += tn * cnt
    return out


def _col_tiling(n, wide, narrow):
    nw = n // wide
    rest = n - nw * wide
    assert rest % narrow == 0 and (nw * wide) % narrow == 0
    return tuple(t for t in ((wide, nw), (narrow, rest // narrow)) if t[1])


def _gmm_down(ug, wd, jm, te, n_used, n_valid, tg, tn):
    p, k = ug.shape
    n = wd.shape[-1]
    xs, ws, _ = _gmm_specs(jm, tg, k, tn)
    return _pcall(
        _gmm_down_kernel, name="gmm_down", grid=(n // tn, p // tg), num_scalar_prefetch=3,
        in_specs=[None, None, None, xs, ws], args=(te, n_used, n_valid, ug, wd),
        out_specs=[pl.BlockSpec((tg, tn // 2), lambda j, ti, te, nu, nv: (ti, j))],
        out_shape=[jax.ShapeDtypeStruct((p, n // 2), jnp.uint32)],
        scratch=[pltpu.VMEM((k, tn), BF16)],
        sem=("arbitrary", "arbitrary"))[0]


def _moe_combine_kernel(pos_ref, nxt_ref, x_ref, rw_ref, g_ref, yg_hbm, *rest, tn, np_tiles):
    outs, (buf, sem) = rest[:-2], rest[-2:]
    tc = x_ref.shape[0]
    i = pl.program_id(0)
    slot = lax.rem(i, 2)

    def gathers(idx_ref, s):
        return [_row_gather(yg_hbm, buf.at[s, k], lambda r, k=k: idx_ref[0, k * tc + r], sem.at[s])
                for k in range(TOP_K)]

    @pl.when(i == 0)
    def _():
        for start, _ in gathers(pos_ref, 0):
            start()

    @pl.when(i + 1 < pl.num_programs(0))
    def _():
        for start, _ in gathers(nxt_ref, 1 - slot):
            start()

    for _, wait in gathers(pos_ref, slot):
        wait()
    rows = buf[slot]

    def expand(p):
        hw = tn // 2
        parts = []
        for jj in range(p.shape[1] // hw):
            parts.extend(_unpack_pair(p[:, jj * hw:(jj + 1) * hw]))
        return jnp.concatenate(parts, axis=-1)

    rw = rw_ref[...]
    y = rw[:, TOP_K:TOP_K + 1] * expand(rows[0])
    for k in range(1, TOP_K):
        y = y + rw[:, TOP_K + k:TOP_K + k + 1] * expand(rows[k])
    xn = x_ref[...] + y
    if np_tiles is None:
        outs[0][...] = xn
    else:
        yn = _rms(xn, g_ref[...])

        @pl.when(i < np_tiles)
        def _():
            outs[0][...] = yn

        @pl.when(i >= np_tiles)
        def _():
            outs[1][...] = yn


def _moe_combine(x, rout, pos, ygp, tn, g3=None, mp=None):
    m, d = x.shape
    tc = _pick(m, (COMBINE_ROWS, 64, 32, 16, 8))
    nt = m // tc
    pos_t = pos.reshape(TOP_K, nt, tc).transpose(1, 0, 2).reshape(nt, 1, TOP_K * tc)
    if g3 is None:
        g3 = jnp.ones((1, 1, d), F32)
        np_tiles = None
        out_specs = [pl.BlockSpec((tc, d), lambda i: (i, 0))]
        out_shape = [jax.ShapeDtypeStruct((m, d), F32)]
    else:
        assert mp % tc == 0
        np_tiles = mp // tc
        out_specs = [pl.BlockSpec((tc, d), lambda i: (jnp.minimum(i, np_tiles - 1), 0)),
                     pl.BlockSpec((tc, d), lambda i: (jnp.maximum(i - np_tiles, 0), 0))]
        out_shape = [jax.ShapeDtypeStruct((mp, d), F32), jax.ShapeDtypeStruct((m - mp, d), F32)]
    return _pcall(
        functools.partial(_moe_combine_kernel, tn=tn, np_tiles=np_tiles), name="moe_combine", grid=(nt,),
        in_specs=[pl.BlockSpec((None, 1, TOP_K * tc), lambda i: (i, 0, 0), memory_space=pltpu.SMEM),
                  pl.BlockSpec((None, 1, TOP_K * tc), lambda i: (jnp.minimum(i + 1, nt - 1), 0, 0),
                               memory_space=pltpu.SMEM),
                  pl.BlockSpec((tc, d), lambda i: (i, 0)),
                  pl.BlockSpec((tc, LANES), lambda i: (i, 0)),
                  _vspec(0, d),
                  pl.BlockSpec(memory_space=pl.ANY)],
        args=(pos_t, pos_t, x, rout, g3, ygp),
        out_specs=out_specs, out_shape=out_shape,
        scratch=[pltpu.VMEM((2, TOP_K, tc, d // 2), jnp.uint32), pltpu.SemaphoreType.DMA((2,))],
        sem=("arbitrary",))


def kernel(x_prompt, x_sample, state_ret, state_lru, state_conv, norm_mix, w_in, ret_gn, w_ret_o, conv_w, conv_b, w_rgate, b_rgate, w_igate, b_igate, lru_lambda, w_lru_o, w_out, norm_ffn, ffn_w_gate, ffn_w_up, ffn_w_down, moe_router, moe_w_gate, moe_w_up, moe_w_down, norm_final):
    bp, tp, d = x_prompt.shape
    bs, ts, _ = x_sample.shape
    depth, _, nh, dk, dv = state_ret.shape
    width = state_lru.shape[-1]
    ncw = conv_w.shape[1]
    ne = moe_router.shape[-1]
    assert dk == dv and nh * dk == width == d and ts == SUBLANES and ncw - 1 <= SUBLANES
    mp, ms = bp * tp, bs * ts
    m = mp + ms
    g_col = 2 * nh * dk + nh * dv
    ux_col = g_col + nh * dv
    ga_col = ux_col + 2 * width

    tm = _pick(m, (1024, 512, 256, 128, 64, 32, 16, 8))
    tm_big = _pick(m, (1536, 1024, 512, 256, 128, 64, 32, 16, 8))
    tm_half = _pick(m, (512, 256, 128, 64, 32, 16, 8))
    tn_of = lambda n: _pick(n, (1024, 512, 256, 128))

    vec3 = lambda a: a.reshape(a.shape[0], 1, a.shape[-1])
    norm_mix3, norm_ffn3, ret_gn3 = vec3(norm_mix), vec3(norm_ffn), vec3(ret_gn)
    norm_final3 = norm_final.reshape(1, 1, d)
    lru_w = (conv_w, vec3(conv_b), w_rgate, vec3(b_rgate), w_igate, vec3(b_igate), vec3(lru_lambda))
    h0_rep = jnp.repeat(state_lru, ts, axis=1)
    buf_fr = jnp.pad(state_conv, ((0, 0), (0, 0), (ts - (ncw - 1), 0), (0, 0))).reshape(depth, ms, width)

    cos_p, sin_p = _rope_tables(tp, dk // 2, 0)
    cos_s, sin_s = _rope_tables(ts, dk // 2, PAST_LEN)
    cos = jnp.concatenate([jnp.tile(cos_p, (bp, 1)), jnp.tile(cos_s, (bs, 1))], axis=0)
    sin = jnp.concatenate([jnp.tile(sin_p, (bp, 1)), jnp.tile(sin_s, (bs, 1))], axis=0)
    tn_in = _pick(nh * dk, (1024, 512, 256))

    x = jnp.concatenate([x_prompt.reshape(mp, d), x_sample.reshape(ms, d)], axis=0)
    ret_p = ret_s = y_p = y_s = None
    lrus_p, lrus_s, convs_p, convs_s = [], [], [], []
    for l in range(depth):
        h = _rmsnorm(x, norm_mix3, l, BF16)
        proj = _mm_inproj(h, w_in, (l,), cos, sin, 2 * nh * dk, dk, (g_col, ux_col),
                          (ux_col + width, ga_col), tm_big, tn_in)

        ya, ret_p = _ret_prompt(proj, ret_gn3, l, depth, bp, tp, nh, dk, dv, ret_p)
        if _fused_seqs_per_step(width, tp, bp, bs, ts) is None:
            ya, ret_s = _ret_sample(proj, ret_gn3, state_ret, l, mp, bs, ts, nh, dk, dv, ya, ret_s)
            yb, hl_p, ul_p = _lru_prompt(proj, ux_col, l, lru_w, bp, tp)
        else:
            yb, hl_p, ul_p, ya, ret_s = _lru_prompt_ret_sample(
                proj, ux_col, l, lru_w, bp, tp, ret_gn3, state_ret, mp, bs, ts, nh, dk, dv, ya, ret_s)
        yb, hl_s, us_s = _lru_sample(proj, ux_col, l, lru_w, mp, h0_rep, buf_fr, yb)
        lrus_p.append(hl_p[:, -1])
        lrus_s.append(hl_s.reshape(bs, ts, width)[:, -1])
        convs_p.append(ul_p[:, SUBLANES - (ncw - 1):])
        convs_s.append(us_s.reshape(bs, ts, width)[:, ts - (ncw - 1):])

        z = _mm_merge(ya, yb, w_ret_o, w_lru_o, (l,), proj, ga_col, ga_col + d, tm, tn_of(d) // 2)
        x = _mm_resid(z, w_out, (l,), x, tm, tn_of(d))
        j = l // 2
        if l % 2 == 0:
            h2 = _rmsnorm(x, norm_ffn3, l, BF16)
            ff = ffn_w_gate.shape[-1]
            u = _mm_swiglu(h2, ffn_w_gate, ffn_w_up, (j,), tm_big, _pick(ff, (512, 256, 128)))
            x = _mm_resid(u, ffn_w_down, (j,), x, tm_half, _pick(d, (512, 256, 128)))
        else:
            fe = moe_w_gate.shape[-1]
            tg = _pick(TOP_K * m, (MOE_TILE, 256, 128, 64, 32, 16))
            tn_d = _pick(d, (1024, 512, 256))
            rout, hp = _router(x, norm_ffn3, l, moe_router[j])
            pos, tok, te, n_used, n_valid, p_rows = _dispatch_tables(rout, ne, tg)
            xg = _moe_gather(hp, tok, p_rows)
            ug = _gmm_swiglu(xg, moe_w_gate, moe_w_up, j, te, n_used, n_valid, tg,
                             _col_tiling(fe, 512, _pick(fe, (256, 128))))
            ygp = _gmm_down(ug, moe_w_down, j, te, n_used, n_valid, tg, tn_d)
            if l == depth - 1:
                y_p, y_s = _moe_combine(x, rout, pos, ygp, tn_d, norm_final3, mp)
            else:
                x = _moe_combine(x, rout, pos, ygp, tn_d)[0]
    if y_p is None:
        y_p = _rmsnorm(x, norm_final3, 0, F32, 0, mp)
        y_s = _rmsnorm(x, norm_final3, 0, F32, mp, ms)
    return (y_p.reshape(bp, tp, d), y_s.reshape(bs, ts, d),
            ret_p, jnp.stack(lrus_p), jnp.stack(convs_p),
            ret_s, jnp.stack(lrus_s), jnp.stack(convs_s))
```

```python
import functools

import jax
import jax.numpy as jnp
from jax import lax
from jax.experimental import pallas as pl
from jax.experimental.pallas import tpu as pltpu

F32 = jnp.float32
BF16 = jnp.bfloat16

ROPE_BASE = 10000.0
LRU_C = 8.0
EPS = 1e-6
RET_CHUNK = 128
PAST_LEN = 16384
TOP_K = 2
SUBLANES = 8
LANES = 128
VMEM_LIMIT = 56 * 1024 * 1024
MOE_TILE = 1024
GATHER_ROWS = 256
COMBINE_ROWS = 256


def _pick(n, prefs):
    for p in prefs:
        if n % p == 0:
            return p
    return n


def _params(*sem):
    return pltpu.CompilerParams(dimension_semantics=sem, vmem_limit_bytes=VMEM_LIMIT)


def _pcall(body, *, name, grid, in_specs, args, out_specs, out_shape, sem, scratch=(), prev=(),
           num_scalar_prefetch=0):
    n_in = len(args)
    prev = tuple(prev) + (None,) * (len(out_shape) - len(prev))
    extra = [(oi, p) for oi, p in enumerate(prev) if p is not None]
    aliases = {n_in + e: oi for e, (oi, _) in enumerate(extra)}
    if extra:
        inner = body

        def body(*refs):
            return inner(*refs[:n_in], *refs[n_in + len(extra):])

    in_specs = list(in_specs) + [pl.BlockSpec(memory_space=pl.ANY)] * len(extra)
    if num_scalar_prefetch:
        grid_spec = pltpu.PrefetchScalarGridSpec(
            num_scalar_prefetch=num_scalar_prefetch, grid=grid, in_specs=in_specs[num_scalar_prefetch:],
            out_specs=tuple(out_specs), scratch_shapes=list(scratch))
        call = pl.pallas_call(body, grid_spec=grid_spec, out_shape=tuple(out_shape),
                              input_output_aliases=aliases, compiler_params=_params(*sem), name=name)
    else:
        call = pl.pallas_call(body, grid=grid, in_specs=in_specs, out_specs=tuple(out_specs),
                              out_shape=tuple(out_shape), scratch_shapes=list(scratch),
                              input_output_aliases=aliases, compiler_params=_params(*sem), name=name)
    return call(*args, *[p for _, p in extra])


def _wspec(pre, k, tn):
    return pl.BlockSpec((None,) * len(pre) + (k, tn), lambda j, i: pre + (0, j))


def _vspec(l, n):
    return pl.BlockSpec((None, 1, n), lambda *_: (l, 0, 0))


def _rms(x, g):
    return (x * lax.rsqrt(jnp.mean(x * x, axis=-1, keepdims=True) + EPS)) * g


def _rmsnorm_kernel(x_ref, g_ref, o_ref):
    o_ref[...] = _rms(x_ref[...], g_ref[...]).astype(o_ref.dtype)


def _rmsnorm(x, g3, l, out_dtype, row0=0, rows=None, out_row0=0, out_rows=None, prev=None):
    d = x.shape[1]
    rows = x.shape[0] if rows is None else rows
    out_rows = rows if out_rows is None else out_rows
    tm = _pick(rows, (512, 256, 128, 64, 32, 16, 8))
    assert row0 % tm == 0 and out_row0 % tm == 0
    rb0, ob0 = row0 // tm, out_row0 // tm
    return _pcall(
        _rmsnorm_kernel, name="rmsnorm", grid=(rows // tm,),
        in_specs=[pl.BlockSpec((tm, d), lambda i: (rb0 + i, 0)), _vspec(l, d)],
        args=(x, g3),
        out_specs=[pl.BlockSpec((tm, d), lambda i: (ob0 + i, 0))],
        out_shape=[jax.ShapeDtypeStruct((out_rows, d), out_dtype)],
        prev=(prev,), sem=("arbitrary",))[0]


def _cast_w(w_ref, wb_ref):
    k = w_ref.shape[0]
    ck = _pick(k, (256, 128, 64, 32, 16))

    def body(c, carry):
        r = pl.multiple_of(c * ck, ck)
        wb_ref[pl.ds(r, ck), :] = w_ref[pl.ds(r, ck), :].astype(BF16)
        return carry

    lax.fori_loop(0, k // ck, body, 0)


def _mm_inproj_kernel(x_ref, w_ref, cos_ref, sin_ref, o_ref, wb_ref, *, nrot, dk, silu_tiles, gelu_tiles):
    j = pl.program_id(0)

    @pl.when(pl.program_id(1) == 0)
    def _():
        _cast_w(w_ref, wb_ref)

    tm, tn = o_ref.shape
    nchunk = 4 if tm % (4 * 16) == 0 else 1

    def row_chunks(epilogue):
        rc = tm // nchunk
        for c in range(nchunk):
            rs = slice(c * rc, (c + 1) * rc)
            epilogue(jnp.dot(x_ref[rs, :], wb_ref[...], preferred_element_type=F32), rs)

    def store(fn):
        def epilogue(acc, rs):
            o_ref[rs, :] = fn(acc).astype(o_ref.dtype)
        return epilogue

    in_tiles = lambda t: jnp.logical_and(j >= t[0], j < t[1])
    is_silu, is_gelu = in_tiles(silu_tiles), in_tiles(gelu_tiles)

    @pl.when(jnp.logical_and(j >= nrot, jnp.logical_not(jnp.logical_or(is_silu, is_gelu))))
    def _():
        row_chunks(store(lambda acc: acc))

    @pl.when(is_silu)
    def _():
        row_chunks(store(jax.nn.silu))

    @pl.when(is_gelu)
    def _():
        row_chunks(store(jax.nn.gelu))

    @pl.when(j < nrot)
    def _():
        scale = jnp.where(j >= nrot // 2, F32(dk ** -0.5), F32(1.0))
        half = dk // 2

        def rotary(acc, rs):
            cos = cos_ref[rs, :] * scale
            sin = sin_ref[rs, :] * scale
            for h in range(tn // dk):
                x1 = acc[:, h * dk:h * dk + half]
                x2 = acc[:, h * dk + half:(h + 1) * dk]
                o_ref[rs, h * dk:h * dk + half] = (x1 * cos - x2 * sin).astype(o_ref.dtype)
                o_ref[rs, h * dk + half:(h + 1) * dk] = (x1 * sin + x2 * cos).astype(o_ref.dtype)

        row_chunks(rotary)


def _mm_inproj(x, w, pre, cos, sin, qk_cols, dk, silu_cols, gelu_cols, tm, tn):
    m, k = x.shape
    n = w.shape[-1]
    half = cos.shape[1]
    assert qk_cols % (2 * tn) == 0 and tn % dk == 0
    assert all(c % tn == 0 for c in silu_cols + gelu_cols)
    tiles = lambda cols: (cols[0] // tn, cols[1] // tn)
    return _pcall(
        functools.partial(_mm_inproj_kernel, nrot=qk_cols // tn, dk=dk, silu_tiles=tiles(silu_cols),
                          gelu_tiles=tiles(gelu_cols)), name="mm_inproj",
        grid=(n // tn, m // tm),
        in_specs=[pl.BlockSpec((tm, k), lambda j, i: (i, 0)), _wspec(pre, k, tn),
                  pl.BlockSpec((tm, half), lambda j, i: (i, 0)),
                  pl.BlockSpec((tm, half), lambda j, i: (i, 0))],
        args=(x, w, cos, sin),
        out_specs=[pl.BlockSpec((tm, tn), lambda j, i: (i, j))],
        out_shape=[jax.ShapeDtypeStruct((m, n), BF16)],
        scratch=[pltpu.VMEM((k, tn), BF16)],
        sem=("arbitrary", "arbitrary"))[0]


def _mm_resid_kernel(x_ref, w_ref, r_ref, o_ref, wb_ref):
    @pl.when(pl.program_id(1) == 0)
    def _():
        _cast_w(w_ref, wb_ref)

    o_ref[...] = r_ref[...] + jnp.dot(x_ref[...], wb_ref[...], preferred_element_type=F32)


def _mm_resid2_kernel(x_ref, w_ref, ra_ref, rb_ref, o_ref, wb_ref, *, na):
    @pl.when(pl.program_id(1) == 0)
    def _():
        _cast_w(w_ref, wb_ref)

    y = jnp.dot(x_ref[...], wb_ref[...], preferred_element_type=F32)
    i = pl.program_id(1)

    @pl.when(i < na)
    def _():
        o_ref[...] = ra_ref[...] + y

    @pl.when(i >= na)
    def _():
        o_ref[...] = rb_ref[...] + y


def _mm_resid(x, w, pre, r, tm, tn):
    m, k = x.shape
    n = w.shape[-1]
    if isinstance(r, tuple):
        ra, rb = r
        assert ra.shape[0] % tm == 0 and rb.shape[0] % tm == 0 and ra.shape[0] + rb.shape[0] == m
        na = ra.shape[0] // tm
        body = functools.partial(_mm_resid2_kernel, na=na)
        r_specs = [pl.BlockSpec((tm, tn), lambda j, i: (jnp.minimum(i, na - 1), j)),
                   pl.BlockSpec((tm, tn), lambda j, i: (jnp.maximum(i - na, 0), j))]
        r_args = (ra, rb)
    else:
        body, r_specs, r_args = _mm_resid_kernel, [pl.BlockSpec((tm, tn), lambda j, i: (i, j))], (r,)
    return _pcall(
        body, name="mm_resid", grid=(n // tn, m // tm),
        in_specs=[pl.BlockSpec((tm, k), lambda j, i: (i, 0)), _wspec(pre, k, tn)] + r_specs,
        args=(x, w) + r_args,
        out_specs=[pl.BlockSpec((tm, tn), lambda j, i: (i, j))],
        out_shape=[jax.ShapeDtypeStruct((m, n), F32)],
        scratch=[pltpu.VMEM((k, tn), BF16)],
        sem=("arbitrary", "arbitrary"))[0]


def _row_chunks(rows, fn, nchunk=4):
    n = nchunk if rows % (nchunk * 16) == 0 else 1
    rc = rows // n
    for c in range(n):
        fn(slice(c * rc, (c + 1) * rc))


def _swiglu_rows(x_ref, wgb_ref, wub_ref, o_ref):
    def chunk(rs):
        x = x_ref[rs, :]
        g = jnp.dot(x, wgb_ref[...], preferred_element_type=F32)
        u = jnp.dot(x, wub_ref[...], preferred_element_type=F32)
        o_ref[rs, :] = (jax.nn.silu(g) * u).astype(o_ref.dtype)

    _row_chunks(o_ref.shape[0], chunk)


def _mm_swiglu_kernel(x_ref, wg_ref, wu_ref, o_ref, wgb_ref, wub_ref):
    @pl.when(pl.program_id(1) == 0)
    def _():
        _cast_w(wg_ref, wgb_ref)
        _cast_w(wu_ref, wub_ref)

    _swiglu_rows(x_ref, wgb_ref, wub_ref, o_ref)


def _mm_swiglu(x, wg, wu, pre, tm, tn):
    m, k = x.shape
    n = wg.shape[-1]
    return _pcall(
        _mm_swiglu_kernel, name="mm_swiglu", grid=(n // tn, m // tm),
        in_specs=[pl.BlockSpec((tm, k), lambda j, i: (i, 0)), _wspec(pre, k, tn), _wspec(pre, k, tn)],
        args=(x, wg, wu),
        out_specs=[pl.BlockSpec((tm, tn), lambda j, i: (i, j))],
        out_shape=[jax.ShapeDtypeStruct((m, n), BF16)],
        scratch=[pltpu.VMEM((k, tn), BF16), pltpu.VMEM((k, tn), BF16)],
        sem=("arbitrary", "arbitrary"))[0]


def _mm_merge_kernel(a_ref, b_ref, wa_ref, wb_ref, ga_ref, gb_ref, o_ref, wab_ref, wbb_ref):
    @pl.when(pl.program_id(1) == 0)
    def _():
        _cast_w(wa_ref, wab_ref)
        _cast_w(wb_ref, wbb_ref)

    def chunk(rs):
        ya = jnp.dot(a_ref[rs, :], wab_ref[...], preferred_element_type=F32)
        yb = jnp.dot(b_ref[rs, :], wbb_ref[...], preferred_element_type=F32)
        ga = ga_ref[rs, :].astype(F32)
        gb = gb_ref[rs, :].astype(F32)
        o_ref[rs, :] = (jax.nn.sigmoid(ga) * ya + jax.nn.sigmoid(gb) * yb).astype(o_ref.dtype)

    _row_chunks(o_ref.shape[0], chunk)


def _mm_merge(a, b, wa, wb, pre, proj, ga_col, gb_col, tm, tn):
    m, k = a.shape
    n = wa.shape[-1]
    ga_blk, gb_blk = ga_col // tn, gb_col // tn
    return _pcall(
        _mm_merge_kernel, name="mm_merge", grid=(n // tn, m // tm),
        in_specs=[pl.BlockSpec((tm, k), lambda j, i: (i, 0)),
                  pl.BlockSpec((tm, k), lambda j, i: (i, 0)),
                  _wspec(pre, k, tn), _wspec(pre, k, tn),
                  pl.BlockSpec((tm, tn), lambda j, i: (i, ga_blk + j)),
                  pl.BlockSpec((tm, tn), lambda j, i: (i, gb_blk + j))],
        args=(a, b, wa, wb, proj, proj),
        out_specs=[pl.BlockSpec((tm, tn), lambda j, i: (i, j))],
        out_shape=[jax.ShapeDtypeStruct((m, n), BF16)],
        scratch=[pltpu.VMEM((k, tn), BF16), pltpu.VMEM((k, tn), BF16)],
        sem=("arbitrary", "arbitrary"))[0]


def _rope_kernel(inv_ref, cos_ref, sin_ref, *, pos0):
    t, half = cos_ref.shape
    pos = lax.broadcasted_iota(jnp.int32, (t, half), 0).astype(F32) + F32(pos0)
    ang = pos * inv_ref[...]
    cos_ref[...] = jnp.cos(ang)
    sin_ref[...] = jnp.sin(ang)


def _rope_tables(t, half, pos0):
    inv = ROPE_BASE ** (-jnp.arange(half, dtype=F32) / half)
    return pl.pallas_call(
        functools.partial(_rope_kernel, pos0=pos0),
        out_shape=(jax.ShapeDtypeStruct((t, half), F32), jax.ShapeDtypeStruct((t, half), F32)),
        name="rope_tables",
    )(inv.reshape(1, half))


def _decay_tables(c, h, dk, dv):
    log_g = jnp.log1p(-jnp.exp2(-5.0 - jnp.arange(h, dtype=F32)))
    idx = jnp.arange(c)
    rel = idx[:, None] - idx[None, :]
    dmask = jnp.where(rel[None] >= 0,
                      jnp.exp(jnp.maximum(rel, 0)[None].astype(F32) * log_g[:, None, None]), 0.0)
    xi = jnp.exp((idx + 1).astype(F32)[None, :] * log_g[:, None])
    zeta = jnp.exp((c - 1 - idx).astype(F32)[None, :] * log_g[:, None])
    g_c = jnp.exp(c * log_g)
    xi_t = jnp.broadcast_to(xi[:, :, None], (h, c, dv))
    zeta_t = jnp.broadcast_to(zeta[:, :, None], (h, c, dk))
    return dmask, xi_t, zeta_t, g_c


def _ret_head(qb, kb, v, g, dm, xi, zt, gn, s, gc):
    scores = lax.dot_general(qb, kb, (((1,), (1,)), ((), ())), preferred_element_type=F32)
    intra = jnp.dot((scores * dm).astype(BF16), v, preferred_element_type=F32)
    cross = jnp.dot(qb, s.astype(BF16), preferred_element_type=F32) * xi
    kz = (kb.astype(F32) * zt).astype(BF16)
    upd = lax.dot_general(kz, v, (((0,), (0,)), ((), ())), preferred_element_type=F32)
    s_new = gc * s + upd
    o = intra + cross
    mu = jnp.mean(o, axis=-1, keepdims=True)
    d = o - mu
    var = jnp.mean(d * d, axis=-1, keepdims=True)
    y = d * lax.rsqrt(var + EPS) * gn
    return (g * y).astype(BF16), s_new


def _ret_prompt_kernel(gc_ref, q_ref, k_ref, v_ref, g_ref, dm_ref, xi_ref, zt_ref,
                       gn_ref, o_ref, s_ref, *, nh, dk, dv):
    @pl.when(pl.program_id(1) == 0)
    def _():
        s_ref[...] = jnp.zeros_like(s_ref)

    c = dm_ref.shape[1]
    for h in range(nh):
        s = s_ref[h]
        for ci in range(q_ref.shape[0] // c):
            rs = slice(ci * c, (ci + 1) * c)
            out, s = _ret_head(
                q_ref[rs, h * dk:(h + 1) * dk], k_ref[rs, h * dk:(h + 1) * dk],
                v_ref[rs, h * dv:(h + 1) * dv], g_ref[rs, h * dv:(h + 1) * dv].astype(F32),
                dm_ref[h], xi_ref[h], zt_ref[h], gn_ref[:, h * dv:(h + 1) * dv], s, gc_ref[h])
            o_ref[rs, h * dv:(h + 1) * dv] = out
        s_ref[h] = s


def _ret_prompt(proj, gn3, l, depth, b, t, nh, dk, dv, s_prev):
    cl = RET_CHUNK if t % RET_CHUNK == 0 else t
    cps = _pick(t // cl, (4, 2, 1))
    c = cl * cps
    nc = t // c
    w = nh * dk
    m = proj.shape[0]
    dmask, xi_t, zeta_t, g_c = _decay_tables(cl, nh, dk, dv)
    row = lambda bi, ci: bi * nc + ci
    full3 = lambda bi, ci: (0, 0, 0)
    return _pcall(
        functools.partial(_ret_prompt_kernel, nh=nh, dk=dk, dv=dv), name="ret_prompt", grid=(b, nc),
        in_specs=[pl.BlockSpec(memory_space=pltpu.SMEM),
                  pl.BlockSpec((c, w), lambda bi, ci: (row(bi, ci), 0)),
                  pl.BlockSpec((c, w), lambda bi, ci: (row(bi, ci), 1)),
                  pl.BlockSpec((c, w), lambda bi, ci: (row(bi, ci), 2)),
                  pl.BlockSpec((c, w), lambda bi, ci: (row(bi, ci), 3)),
                  pl.BlockSpec((nh, cl, cl), full3),
                  pl.BlockSpec((nh, cl, dv), full3),
                  pl.BlockSpec((nh, cl, dk), full3),
                  _vspec(l, nh * dv)],
        args=(g_c, proj, proj, proj, proj, dmask, xi_t, zeta_t, gn3),
        out_specs=[pl.BlockSpec((c, nh * dv), lambda bi, ci: (row(bi, ci), 0)),
                   pl.BlockSpec((None, None, nh, dk, dv), lambda bi, ci: (l, bi, 0, 0, 0))],
        out_shape=[jax.ShapeDtypeStruct((m, nh * dv), BF16),
                   jax.ShapeDtypeStruct((depth, b, nh, dk, dv), F32)],
        prev=(None, s_prev),
        sem=("arbitrary", "arbitrary"))


def _ret_sample_kernel(gc_ref, q_ref, k_ref, v_ref, g_ref, dm_ref, xi_ref, zt_ref,
                       gn_ref, s0_ref, o_ref, s_ref, *, nh, dk, dv, bb, ts):
    q = q_ref[...].astype(F32)
    k = k_ref[...].astype(F32)
    v = v_ref[...].astype(F32)
    g = g_ref[...].astype(F32)
    seqs = []
    for i in range(bb):
        r0, r1 = i * ts, (i + 1) * ts
        outs = []
        for h in range(nh):
            out, s_new = _ret_head(
                q[r0:r1, h * dk:(h + 1) * dk].astype(BF16), k[r0:r1, h * dk:(h + 1) * dk].astype(BF16),
                v[r0:r1, h * dv:(h + 1) * dv].astype(BF16), g[r0:r1, h * dv:(h + 1) * dv],
                dm_ref[h], xi_ref[h], zt_ref[h], gn_ref[:, h * dv:(h + 1) * dv],
                s0_ref[i, h], gc_ref[h])
            s_ref[i, h] = s_new
            outs.append(out.astype(F32))
        seqs.append(jnp.concatenate(outs, axis=-1))
    o_ref[...] = jnp.concatenate(seqs, axis=0).astype(o_ref.dtype)


def _ret_sample(proj, gn3, s0_all, l, row0, bs, ts, nh, dk, dv, ya_prev, s_prev):
    bb = _pick(bs, (4, 2, 1))
    rows = bb * ts
    w = nh * dk
    depth = s0_all.shape[0]
    dmask, xi_t, zeta_t, g_c = _decay_tables(ts, nh, dk, dv)
    assert row0 % rows == 0
    rb0 = row0 // rows
    full3 = lambda i: (0, 0, 0)
    return _pcall(
        functools.partial(_ret_sample_kernel, nh=nh, dk=dk, dv=dv, bb=bb, ts=ts), name="ret_sample",
        grid=(bs // bb,),
        in_specs=[pl.BlockSpec(memory_space=pltpu.SMEM),
                  pl.BlockSpec((rows, w), lambda i: (rb0 + i, 0)),
                  pl.BlockSpec((rows, w), lambda i: (rb0 + i, 1)),
                  pl.BlockSpec((rows, w), lambda i: (rb0 + i, 2)),
                  pl.BlockSpec((rows, w), lambda i: (rb0 + i, 3)),
                  pl.BlockSpec((nh, ts, ts), full3),
                  pl.BlockSpec((nh, ts, dv), full3),
                  pl.BlockSpec((nh, ts, dk), full3),
                  _vspec(l, nh * dv),
                  pl.BlockSpec((None, bb, nh, dk, dv), lambda i: (l, i, 0, 0, 0))],
        args=(g_c, proj, proj, proj, proj, dmask, xi_t, zeta_t, gn3, s0_all),
        out_specs=[pl.BlockSpec((rows, nh * dv), lambda i: (rb0 + i, 0)),
                   pl.BlockSpec((None, bb, nh, dk, dv), lambda i: (l, i, 0, 0, 0))],
        out_shape=[jax.ShapeDtypeStruct(ya_prev.shape, BF16),
                   jax.ShapeDtypeStruct((depth, bs, nh, dk, dv), F32)],
        prev=(ya_prev, s_prev),
        sem=("arbitrary",))


def _lru_gates(uc, wr_ref, br_ref, wi_ref, bi_ref, lam_ref):
    nb, lb, _ = wr_ref.shape
    ucb = uc.astype(BF16)
    rl, il = [], []
    for n in range(nb):
        xb = ucb[:, n * lb:(n + 1) * lb]
        rl.append(jnp.dot(xb, wr_ref[n].astype(BF16), preferred_element_type=F32))
        il.append(jnp.dot(xb, wi_ref[n].astype(BF16), preferred_element_type=F32))
    r = jax.nn.sigmoid(jnp.concatenate(rl, axis=-1) + br_ref[...])
    i = jax.nn.sigmoid(jnp.concatenate(il, axis=-1) + bi_ref[...])
    log_a = -LRU_C * r * jax.nn.softplus(-lam_ref[...])
    a = jnp.exp(log_a)
    z = -jnp.tanh(log_a) * (a * a + 1.0)
    bx = jnp.where(z == 0.0, 0.0, z * lax.rsqrt(z)) * (i * uc)
    return a, bx


def _blocks(x):
    return [x[lo:lo + SUBLANES, :] for lo in range(0, x.shape[0], SUBLANES)]


def _conv_blocks(u_blocks, prev_blocks, cw_ref, cb_ref, rows):
    ncw = cw_ref.shape[0]
    rolled = {}

    def rot(x, s):
        key = (id(x), s)
        if key not in rolled:
            rolled[key] = pltpu.roll(x, s, axis=0)
        return rolled[key]

    out = []
    for ub, pb in zip(u_blocks, prev_blocks):
        uc = cb_ref[...]
        for j in range(ncw):
            s = ncw - 1 - j
            term = ub if s == 0 else jnp.where(rows >= s, rot(ub, s), rot(pb, s))
            uc = uc + term * cw_ref[j:j + 1, :]
        out.append(uc)
    return jnp.concatenate(out, axis=0)


def _scan_block(a, bx, rows):
    d = 1
    while d < SUBLANES:
        m = rows >= d
        bx = bx + a * jnp.where(m, pltpu.roll(bx, d, axis=0), 0.0)
        a = a * jnp.where(m, pltpu.roll(a, d, axis=0), 1.0)
        d *= 2
    return a, bx


def _lru_prompt_kernel(ux_ref, uy_ref, cw_ref, cb_ref, wr_ref, br_ref, wi_ref, bi_ref, lam_ref,
                       o_ref, hl_ref, ul_ref, hc_ref):
    r, cw = ux_ref.shape

    @pl.when(pl.program_id(2) == 0)
    def _():
        hc_ref[...] = jnp.zeros_like(hc_ref)
        ul_ref[...] = jnp.zeros_like(ul_ref)

    rows = lax.broadcasted_iota(jnp.int32, (SUBLANES, cw), 0)
    ub = _blocks(ux_ref[...].astype(F32))
    uc = _conv_blocks(ub, [ul_ref[...]] + ub[:-1], cw_ref, cb_ref, rows)
    ul_ref[...] = ub[-1]
    a, bx = _lru_gates(uc, wr_ref, br_ref, wi_ref, bi_ref, lam_ref)
    carry = hc_ref[...]
    hs = []
    for ab, bb in zip(_blocks(a), _blocks(bx)):
        ab, bb = _scan_block(ab, bb, rows)
        hb = ab * carry + bb
        hs.append(hb)
        carry = jnp.broadcast_to(hb[SUBLANES - 1:SUBLANES, :], (SUBLANES, cw))
    hc_ref[...] = carry
    hl_ref[...] = hs[-1]
    h = jnp.concatenate(hs, axis=0)
    o_ref[...] = (h * uy_ref[...].astype(F32)).astype(o_ref.dtype)


def _lru_sample_kernel(ux_ref, uy_ref, cw_ref, cb_ref, wr_ref, br_ref, wi_ref, bi_ref, lam_ref,
                       h0_ref, buf_ref, o_ref, hl_ref, us_ref):
    r, cw = ux_ref.shape
    rows = lax.broadcasted_iota(jnp.int32, (SUBLANES, cw), 0)
    u = ux_ref[...].astype(F32)
    us_ref[...] = u
    uc = _conv_blocks(_blocks(u), _blocks(buf_ref[...]), cw_ref, cb_ref, rows)
    a, bx = _lru_gates(uc, wr_ref, br_ref, wi_ref, bi_ref, lam_ref)
    hs = []
    for ab, bb, h0 in zip(_blocks(a), _blocks(bx), _blocks(h0_ref[...])):
        ab, bb = _scan_block(ab, bb, rows)
        hs.append(ab * h0 + bb)
    h = jnp.concatenate(hs, axis=0)
    hl_ref[...] = h
    o_ref[...] = (h * uy_ref[...].astype(F32)).astype(o_ref.dtype)


def _lru_specs(l, r, cw, ncw, lb, ux_blk, uy_blk, row_of, cb_of):
    nbc = cw // lb
    vec = pl.BlockSpec((None, 1, cw), lambda *g: (l, 0, cb_of(*g)))
    gate = pl.BlockSpec((None, nbc, lb, lb), lambda *g: (l, cb_of(*g), 0, 0))
    return [pl.BlockSpec((r, cw), lambda *g: (row_of(*g), ux_blk + cb_of(*g))),
            pl.BlockSpec((r, cw), lambda *g: (row_of(*g), uy_blk + cb_of(*g))),
            pl.BlockSpec((None, ncw, cw), lambda *g: (l, 0, cb_of(*g))),
            vec, gate, vec, gate, vec, vec]


def _lru_prompt(proj, ux_col, l, lru_w, b, t):
    conv_w, conv_b3, w_r, b_r3, w_i, b_i3, lam3 = lru_w
    width = lam3.shape[-1]
    lb = w_r.shape[-1]
    ncw = conv_w.shape[1]
    m = proj.shape[0]
    cw = _pick(width, (1024, 512, 256, 128))
    r = _pick(t, (256, 128, 64, 32, 16, 8))
    nt = t // r
    specs = _lru_specs(l, r, cw, ncw, lb, ux_col // cw, (ux_col + width) // cw,
                       lambda bi, cb, ti: bi * nt + ti, lambda bi, cb, ti: cb)
    last8 = pl.BlockSpec((None, SUBLANES, cw), lambda bi, cb, ti: (bi, 0, cb))
    return _pcall(
        _lru_prompt_kernel, name="lru_prompt", grid=(b, width // cw, nt),
        in_specs=specs,
        args=(proj, proj, conv_w, conv_b3, w_r, b_r3, w_i, b_i3, lam3),
        out_specs=[pl.BlockSpec((r, cw), lambda bi, cb, ti: (bi * nt + ti, cb)), last8, last8],
        out_shape=[jax.ShapeDtypeStruct((m, width), BF16),
                   jax.ShapeDtypeStruct((b, SUBLANES, width), F32),
                   jax.ShapeDtypeStruct((b, SUBLANES, width), F32)],
        scratch=[pltpu.VMEM((SUBLANES, cw), F32)],
        sem=("arbitrary", "arbitrary", "arbitrary"))


def _lru_ret_kernel(*refs, nh, dk, dv, bb, ts):
    lru_in, ret_in = refs[:9], refs[9:19]
    yb_ref, hl_ref, ul_ref, ya_ref, s_ref, hc_ref = refs[19:]
    _lru_prompt_kernel(*lru_in, yb_ref, hl_ref, ul_ref, hc_ref)
    _ret_sample_kernel(*ret_in, ya_ref, s_ref, nh=nh, dk=dk, dv=dv, bb=bb, ts=ts)


def _fused_seqs_per_step(width, t, b, bs, ts):
    cw = _pick(width, (1024, 512, 256, 128))
    r = _pick(t, (256, 128, 64, 32, 16, 8))
    steps = b * (width // cw) * (t // r)
    if bs % steps:
        return None
    bb = bs // steps
    return bb if bb <= 4 and (bb * ts) % 16 == 0 else None


def _lru_prompt_ret_sample(proj, ux_col, l, lru_w, b, t, gn3, s0_all, row0, bs, ts, nh, dk, dv,
                           ya_prev, s_prev):
    conv_w, conv_b3, w_r, b_r3, w_i, b_i3, lam3 = lru_w
    width = lam3.shape[-1]
    lb = w_r.shape[-1]
    ncw = conv_w.shape[1]
    m = proj.shape[0]
    depth = s0_all.shape[0]
    cw = _pick(width, (1024, 512, 256, 128))
    r = _pick(t, (256, 128, 64, 32, 16, 8))
    nt, ncb = t // r, width // cw
    bb = _fused_seqs_per_step(width, t, b, bs, ts)
    rows = bb * ts
    w = nh * dk
    assert row0 % rows == 0
    rb0 = row0 // rows
    lin = lambda bi, cb, ti: (bi * ncb + cb) * nt + ti
    dmask, xi_t, zeta_t, g_c = _decay_tables(ts, nh, dk, dv)
    lru_specs = _lru_specs(l, r, cw, ncw, lb, ux_col // cw, (ux_col + width) // cw,
                           lambda bi, cb, ti: bi * nt + ti, lambda bi, cb, ti: cb)
    full3 = lambda *g: (0, 0, 0)
    qkvg = [pl.BlockSpec((rows, w), lambda *g, c=c: (rb0 + lin(*g), c)) for c in range(4)]
    state = pl.BlockSpec((None, bb, nh, dk, dv), lambda *g: (l, lin(*g), 0, 0, 0))
    ret_specs = ([pl.BlockSpec(memory_space=pltpu.SMEM)] + qkvg +
                 [pl.BlockSpec((nh, ts, ts), full3), pl.BlockSpec((nh, ts, dv), full3),
                  pl.BlockSpec((nh, ts, dk), full3), _vspec(l, nh * dv), state])
    last8 = pl.BlockSpec((None, SUBLANES, cw), lambda bi, cb, ti: (bi, 0, cb))
    return _pcall(
        functools.partial(_lru_ret_kernel, nh=nh, dk=dk, dv=dv, bb=bb, ts=ts), name="lru_prompt_ret_sample",
        grid=(b, ncb, nt),
        in_specs=lru_specs + ret_specs,
        args=(proj, proj, conv_w, conv_b3, w_r, b_r3, w_i, b_i3, lam3,
              g_c, proj, proj, proj, proj, dmask, xi_t, zeta_t, gn3, s0_all),
        out_specs=[pl.BlockSpec((r, cw), lambda bi, cb, ti: (bi * nt + ti, cb)), last8, last8,
                   pl.BlockSpec((rows, nh * dv), lambda *g: (rb0 + lin(*g), 0)), state],
        out_shape=[jax.ShapeDtypeStruct((m, width), BF16),
                   jax.ShapeDtypeStruct((b, SUBLANES, width), F32),
                   jax.ShapeDtypeStruct((b, SUBLANES, width), F32),
                   jax.ShapeDtypeStruct(ya_prev.shape, BF16),
                   jax.ShapeDtypeStruct((depth, bs, nh, dk, dv), F32)],
        prev=(None, None, None, ya_prev, s_prev),
        scratch=[pltpu.VMEM((SUBLANES, cw), F32)],
        sem=("arbitrary", "arbitrary", "arbitrary"))


def _lru_sample(proj, ux_col, l, lru_w, row0, h0_rep, buf_fr, yb_prev):
    conv_w, conv_b3, w_r, b_r3, w_i, b_i3, lam3 = lru_w
    width = lam3.shape[-1]
    lb = w_r.shape[-1]
    ncw = conv_w.shape[1]
    ms = h0_rep.shape[1]
    cw = _pick(width, (1024, 512, 256, 128))
    r = _pick(ms, (256, 128, 64, 32, 16, 8))
    assert row0 % r == 0
    rb0 = row0 // r
    specs = _lru_specs(l, r, cw, ncw, lb, ux_col // cw, (ux_col + width) // cw,
                       lambda ri, cb: rb0 + ri, lambda ri, cb: cb)
    st = pl.BlockSpec((None, r, cw), lambda ri, cb: (l, ri, cb))
    f32rows = pl.BlockSpec((r, cw), lambda ri, cb: (ri, cb))
    return _pcall(
        _lru_sample_kernel, name="lru_sample", grid=(ms // r, width // cw),
        in_specs=specs + [st, st],
        args=(proj, proj, conv_w, conv_b3, w_r, b_r3, w_i, b_i3, lam3, h0_rep, buf_fr),
        out_specs=[pl.BlockSpec((r, cw), lambda ri, cb: (rb0 + ri, cb)), f32rows, f32rows],
        out_shape=[jax.ShapeDtypeStruct(yb_prev.shape, BF16),
                   jax.ShapeDtypeStruct((ms, width), F32), jax.ShapeDtypeStruct((ms, width), F32)],
        prev=(yb_prev,),
        sem=("arbitrary", "arbitrary"))


def _pack_pair(hi, lo):
    hb = lax.bitcast_convert_type(hi.astype(jnp.bfloat16).astype(F32), jnp.uint32)
    lb = lax.bitcast_convert_type(lo.astype(jnp.bfloat16).astype(F32), jnp.uint32)
    return hb | (lb >> 16)


def _unpack_pair(p):
    hi = lax.bitcast_convert_type(p & jnp.uint32(0xFFFF0000), F32)
    lo = lax.bitcast_convert_type(p << 16, F32)
    return hi, lo


def _router_kernel(x_ref, g_ref, wr_ref, o_ref, hp_ref, *, ne):
    hf = _rms(x_ref[...], g_ref[...])
    half = hf.shape[1] // 2
    hp_ref[...] = _pack_pair(hf[:, :half], hf[:, half:])
    h = hf.astype(BF16)
    logits = jnp.dot(h, wr_ref[...].astype(BF16), preferred_element_type=F32)
    lane = lax.broadcasted_iota(jnp.int32, logits.shape, 1).astype(F32)
    neg = F32(-jnp.inf)
    big = F32(LANES)
    l1 = jnp.where(lane < ne, logits, neg)
    m1 = jnp.max(l1, axis=-1, keepdims=True)
    i1 = jnp.min(jnp.where(l1 == m1, lane, big), axis=-1, keepdims=True)
    l2 = jnp.where(lane == i1, neg, l1)
    m2 = jnp.max(l2, axis=-1, keepdims=True)
    i2 = jnp.min(jnp.where(l2 == m2, lane, big), axis=-1, keepdims=True)
    e2 = jnp.exp(m2 - m1)
    den = 1.0 + e2
    o_ref[...] = (jnp.where(lane == 0.0, i1, 0.0) + jnp.where(lane == 1.0, i2, 0.0)
                  + jnp.where(lane == 2.0, 1.0 / den, 0.0) + jnp.where(lane == 3.0, e2 / den, 0.0))


def _router(x, g3, l, wr):
    m, d = x.shape
    ne = wr.shape[-1]
    assert TOP_K == 2 and ne <= LANES
    wr_pad = jnp.pad(wr, ((0, 0), (0, LANES - ne)))
    tm = _pick(m, (512, 256, 128, 64, 32, 16, 8))
    return _pcall(
        functools.partial(_router_kernel, ne=ne), name="router", grid=(m // tm,),
        in_specs=[pl.BlockSpec((tm, d), lambda i: (i, 0)), _vspec(l, d),
                  pl.BlockSpec((d, LANES), lambda i: (0, 0))],
        args=(x, g3, wr_pad),
        out_specs=[pl.BlockSpec((tm, LANES), lambda i: (i, 0)),
                   pl.BlockSpec((tm, d // 2), lambda i: (i, 0))],
        out_shape=[jax.ShapeDtypeStruct((m, LANES), F32),
                   jax.ShapeDtypeStruct((m, d // 2), jnp.uint32)],
        sem=("arbitrary",))


def _dispatch_tables(rout, ne, tg):
    m = rout.shape[0]
    na = TOP_K * m
    eid = rout[:, :TOP_K].astype(jnp.int32).T.reshape(na)
    onehot = (eid[:, None] == jnp.arange(ne, dtype=jnp.int32)[None, :]).astype(jnp.int32)
    cnt = jnp.sum(onehot, axis=0)
    rank = jnp.sum((jnp.cumsum(onehot, axis=0) - onehot) * onehot, axis=1)
    padded = ((cnt + tg - 1) // tg) * tg
    gend = jnp.cumsum(padded)
    gstart = gend - padded
    rem = cnt % tg
    first = jnp.where(rem > 0, rem, tg)
    local = jnp.where(rank < first[eid], rank, rank - first[eid] + tg)
    pos = gstart[eid] + local
    p_rows = ((na + tg - 1) // tg + ne) * tg
    n_tiles = p_rows // tg
    tok = jnp.zeros((p_rows,), jnp.int32).at[pos].set(jnp.arange(na, dtype=jnp.int32) % m)
    n_used = (gend[-1] // tg).astype(jnp.int32)
    tstart = jnp.arange(n_tiles, dtype=jnp.int32) * tg
    tstart = jnp.minimum(tstart, (n_used - 1) * tg)
    te = jnp.minimum(jnp.searchsorted(gend, tstart, side="right"), ne - 1).astype(jnp.int32)
    tidx = jnp.arange(n_tiles, dtype=jnp.int32)
    n_valid = jnp.where(tidx * tg == gstart[te], first[te], tg)
    n_valid = jnp.where(tidx < n_used, n_valid, 0).astype(jnp.int32)
    return pos, tok, te, n_used.reshape(1), n_valid, p_rows


DMA_UNROLL = 8


def _row_gather(src_hbm, dst, idx_of, sem):
    rows = dst.shape[0]
    assert rows % DMA_UNROLL == 0

    def row_copy(r, t):
        return pltpu.make_async_copy(src_hbm.at[pl.ds(t, 1), :], dst.at[pl.ds(r, 1), :], sem)

    def issue(c, carry):
        for u in range(DMA_UNROLL):
            r = c * DMA_UNROLL + u
            row_copy(r, idx_of(r)).start()
        return carry

    def drain(c, carry):
        for u in range(DMA_UNROLL):
            row_copy(c * DMA_UNROLL + u, 0).wait()
        return carry

    start = lambda: lax.fori_loop(0, rows // DMA_UNROLL, issue, 0)
    wait = lambda: lax.fori_loop(0, rows // DMA_UNROLL, drain, 0)
    return start, wait


def _moe_gather_kernel(tok_ref, gv_ref, hp_ref, o_ref):
    tr, d = o_ref.shape
    half = d // 2
    gv = gv_ref[0, 0]
    group = 2 * SUBLANES

    @pl.when(gv < tr)
    def _():
        o_ref[...] = jnp.zeros_like(o_ref)

    def body(g, carry):
        r0 = pl.multiple_of(g * group, group)
        rows = [hp_ref[pl.ds(tok_ref[0, r0 + u], 1), :] for u in range(group)]
        hi, lo = _unpack_pair(jnp.concatenate(rows, axis=0))
        o_ref[pl.ds(r0, group), pl.ds(0, half)] = hi.astype(o_ref.dtype)
        o_ref[pl.ds(r0, group), pl.ds(half, half)] = lo.astype(o_ref.dtype)
        return carry

    lax.fori_loop(0, (gv + group - 1) // group, body, 0)


def _moe_gather(hp, tok, n_valid, tg, p_rows):
    m, half = hp.shape
    tr = _pick(tg, (GATHER_ROWS, 128, 64, 32, 16))
    nt = p_rows // tr
    per = tg // tr
    gidx = jnp.arange(nt, dtype=jnp.int32)
    gv = jnp.clip(n_valid[gidx // per] - (gidx % per) * tr, 0, tr).astype(jnp.int32)
    return _pcall(
        _moe_gather_kernel, name="moe_gather", grid=(nt,),
        in_specs=[pl.BlockSpec((None, 1, tr), lambda i: (i, 0, 0), memory_space=pltpu.SMEM),
                  pl.BlockSpec((None, 1, 1), lambda i: (i, 0, 0), memory_space=pltpu.SMEM),
                  pl.BlockSpec((m, half), lambda i: (0, 0), pipeline_mode=pl.Buffered(1))],
        args=(tok.reshape(nt, 1, tr), gv.reshape(nt, 1, 1), hp),
        out_specs=[pl.BlockSpec((tr, 2 * half), lambda i: (i, 0))],
        out_shape=[jax.ShapeDtypeStruct((p_rows, 2 * half), BF16)],
        sem=("arbitrary",))[0]


def _tile_changed(te_ref):
    ti = pl.program_id(1)
    return jnp.logical_or(ti == 0, te_ref[ti] != te_ref[jnp.maximum(ti - 1, 0)])


def _valid_row_chunks(nv, o_ref, chunk):
    tg = o_ref.shape[0]

    @pl.when(nv == tg)
    def _():
        _row_chunks(tg, chunk)

    @pl.when(nv < tg)
    def _():
        def maybe(rs):
            @pl.when(rs.start < nv)
            def _():
                chunk(rs)

            @pl.when(rs.start >= nv)
            def _():
                o_ref[rs, :] = jnp.zeros((rs.stop - rs.start, o_ref.shape[1]), o_ref.dtype)

        _row_chunks(tg, maybe)


def _gmm_swiglu_kernel(te_ref, nu_ref, nv_ref, x_ref, wg_ref, wu_ref, o_ref, wgb_ref, wub_ref):
    @pl.when(_tile_changed(te_ref))
    def _():
        _cast_w(wg_ref, wgb_ref)
        _cast_w(wu_ref, wub_ref)

    def chunk(rs):
        x = x_ref[rs, :]
        g = jnp.dot(x, wgb_ref[...], preferred_element_type=F32)
        u = jnp.dot(x, wub_ref[...], preferred_element_type=F32)
        o_ref[rs, :] = (jax.nn.silu(g) * u).astype(o_ref.dtype)

    _valid_row_chunks(nv_ref[pl.program_id(1)], o_ref, chunk)


def _gmm_down_kernel(te_ref, nu_ref, nv_ref, x_ref, w_ref, o_ref, wb_ref):
    @pl.when(_tile_changed(te_ref))
    def _():
        _cast_w(w_ref, wb_ref)

    hw = o_ref.shape[1]

    def chunk(rs):
        y = jnp.dot(x_ref[rs, :], wb_ref[...], preferred_element_type=F32)
        o_ref[rs, :] = _pack_pair(y[:, :hw], y[:, hw:])

    _valid_row_chunks(nv_ref[pl.program_id(1)], o_ref, chunk)


def _gmm_specs(jm, tg, k, tn, j0=0):
    xrow = lambda j, ti, te, nu, nv: (jnp.minimum(ti, nu[0] - 1), 0)
    wsp = pl.BlockSpec((None, None, k, tn), lambda j, ti, te, nu, nv: (jm, te[ti], 0, j0 + j))
    return pl.BlockSpec((tg, k), xrow), wsp, pl.BlockSpec((tg, tn), lambda j, ti, te, nu, nv: (ti, j0 + j))


def _gmm_swiglu(xg, wg, wu, jm, te, n_used, n_valid, tg, tns):
    p, k = xg.shape
    n = wg.shape[-1]
    assert sum(tn * cnt for tn, cnt in tns) == n
    out, col = None, 0
    for tn, cnt in tns:
        assert col % tn == 0
        xs, ws, os_ = _gmm_specs(jm, tg, k, tn, col // tn)
        out = _pcall(
            _gmm_swiglu_kernel, name="gmm_swiglu", grid=(cnt, p // tg), num_scalar_prefetch=3,
            in_specs=[None, None, None, xs, ws, ws], args=(te, n_used, n_valid, xg, wg, wu),
            out_specs=[os_], out_shape=[jax.ShapeDtypeStruct((p, n), BF16)],
            scratch=[pltpu.VMEM((k, tn), BF16), pltpu.VMEM((k, tn), BF16)],
            prev=(out,), sem=("arbitrary", "arbitrary"))[0]
        col += tn * cnt
    return out


def _col_tiling(n, wide, narrow):
    nw = n // wide
    rest = n - nw * wide
    assert rest % narrow == 0 and (nw * wide) % narrow == 0
    return tuple(t for t in ((wide, nw), (narrow, rest // narrow)) if t[1])


def _gmm_down(ug, wd, jm, te, n_used, n_valid, tg, tn):
    p, k = ug.shape
    n = wd.shape[-1]
    xs, ws, _ = _gmm_specs(jm, tg, k, tn)
    return _pcall(
        _gmm_down_kernel, name="gmm_down", grid=(n // tn, p // tg), num_scalar_prefetch=3,
        in_specs=[None, None, None, xs, ws], args=(te, n_used, n_valid, ug, wd),
        out_specs=[pl.BlockSpec((tg, tn // 2), lambda j, ti, te, nu, nv: (ti, j))],
        out_shape=[jax.ShapeDtypeStruct((p, n // 2), jnp.uint32)],
        scratch=[pltpu.VMEM((k, tn), BF16)],
        sem=("arbitrary", "arbitrary"))[0]


def _moe_combine_kernel(pos_ref, nxt_ref, x_ref, rw_ref, g_ref, yg_hbm, *rest, tn, np_tiles):
    outs, (buf, sem) = rest[:-2], rest[-2:]
    tc = x_ref.shape[0]
    i = pl.program_id(0)
    slot = lax.rem(i, 2)

    def gathers(idx_ref, s):
        return [_row_gather(yg_hbm, buf.at[s, k], lambda r, k=k: idx_ref[0, k * tc + r], sem.at[s])
                for k in range(TOP_K)]

    @pl.when(i == 0)
    def _():
        for start, _ in gathers(pos_ref, 0):
            start()

    @pl.when(i + 1 < pl.num_programs(0))
    def _():
        for start, _ in gathers(nxt_ref, 1 - slot):
            start()

    for _, wait in gathers(pos_ref, slot):
        wait()
    rows = buf[slot]

    def expand(p):
        hw = tn // 2
        parts = []
        for jj in range(p.shape[1] // hw):
            parts.extend(_unpack_pair(p[:, jj * hw:(jj + 1) * hw]))
        return jnp.concatenate(parts, axis=-1)

    rw = rw_ref[...]
    y = rw[:, TOP_K:TOP_K + 1] * expand(rows[0])
    for k in range(1, TOP_K):
        y = y + rw[:, TOP_K + k:TOP_K + k + 1] * expand(rows[k])
    xn = x_ref[...] + y
    if np_tiles is None:
        outs[0][...] = xn
    else:
        yn = _rms(xn, g_ref[...])

        @pl.when(i < np_tiles)
        def _():
            outs[0][...] = yn

        @pl.when(i >= np_tiles)
        def _():
            outs[1][...] = yn


def _moe_combine(x, rout, pos, ygp, tn, g3=None, mp=None):
    m, d = x.shape
    tc = _pick(m, (COMBINE_ROWS, 64, 32, 16, 8))
    nt = m // tc
    pos_t = pos.reshape(TOP_K, nt, tc).transpose(1, 0, 2).reshape(nt, 1, TOP_K * tc)
    if g3 is None:
        g3 = jnp.ones((1, 1, d), F32)
        np_tiles = None
        out_specs = [pl.BlockSpec((tc, d), lambda i: (i, 0))]
        out_shape = [jax.ShapeDtypeStruct((m, d), F32)]
    else:
        assert mp % tc == 0
        np_tiles = mp // tc
        out_specs = [pl.BlockSpec((tc, d), lambda i: (jnp.minimum(i, np_tiles - 1), 0)),
                     pl.BlockSpec((tc, d), lambda i: (jnp.maximum(i - np_tiles, 0), 0))]
        out_shape = [jax.ShapeDtypeStruct((mp, d), F32), jax.ShapeDtypeStruct((m - mp, d), F32)]
    return _pcall(
        functools.partial(_moe_combine_kernel, tn=tn, np_tiles=np_tiles), name="moe_combine", grid=(nt,),
        in_specs=[pl.BlockSpec((None, 1, TOP_K * tc), lambda i: (i, 0, 0), memory_space=pltpu.SMEM),
                  pl.BlockSpec((None, 1, TOP_K * tc), lambda i: (jnp.minimum(i + 1, nt - 1), 0, 0),
                               memory_space=pltpu.SMEM),
                  pl.BlockSpec((tc, d), lambda i: (i, 0)),
                  pl.BlockSpec((tc, LANES), lambda i: (i, 0)),
                  _vspec(0, d),
                  pl.BlockSpec(memory_space=pl.ANY)],
        args=(pos_t, pos_t, x, rout, g3, ygp),
        out_specs=out_specs, out_shape=out_shape,
        scratch=[pltpu.VMEM((2, TOP_K, tc, d // 2), jnp.uint32), pltpu.SemaphoreType.DMA((2,))],
        sem=("arbitrary",))


def kernel(x_prompt, x_sample, state_ret, state_lru, state_conv, norm_mix, w_in, ret_gn, w_ret_o, conv_w, conv_b, w_rgate, b_rgate, w_igate, b_igate, lru_lambda, w_lru_o, w_out, norm_ffn, ffn_w_gate, ffn_w_up, ffn_w_down, moe_router, moe_w_gate, moe_w_up, moe_w_down, norm_final):
    bp, tp, d = x_prompt.shape
    bs, ts, _ = x_sample.shape
    depth, _, nh, dk, dv = state_ret.shape
    width = state_lru.shape[-1]
    ncw = conv_w.shape[1]
    ne = moe_router.shape[-1]
    assert dk == dv and nh * dk == width == d and ts == SUBLANES and ncw - 1 <= SUBLANES
    mp, ms = bp * tp, bs * ts
    m = mp + ms
    g_col = 2 * nh * dk + nh * dv
    ux_col = g_col + nh * dv
    ga_col = ux_col + 2 * width

    tm = _pick(m, (1024, 512, 256, 128, 64, 32, 16, 8))
    tm_big = _pick(m, (1536, 1024, 512, 256, 128, 64, 32, 16, 8))
    tm_half = _pick(m, (512, 256, 128, 64, 32, 16, 8))
    tn_of = lambda n: _pick(n, (1024, 512, 256, 128))

    vec3 = lambda a: a.reshape(a.shape[0], 1, a.shape[-1])
    norm_mix3, norm_ffn3, ret_gn3 = vec3(norm_mix), vec3(norm_ffn), vec3(ret_gn)
    norm_final3 = norm_final.reshape(1, 1, d)
    lru_w = (conv_w, vec3(conv_b), w_rgate, vec3(b_rgate), w_igate, vec3(b_igate), vec3(lru_lambda))
    h0_rep = jnp.repeat(state_lru, ts, axis=1)
    buf_fr = jnp.pad(state_conv, ((0, 0), (0, 0), (ts - (ncw - 1), 0), (0, 0))).reshape(depth, ms, width)

    cos_p, sin_p = _rope_tables(tp, dk // 2, 0)
    cos_s, sin_s = _rope_tables(ts, dk // 2, PAST_LEN)
    cos = jnp.concatenate([jnp.tile(cos_p, (bp, 1)), jnp.tile(cos_s, (bs, 1))], axis=0)
    sin = jnp.concatenate([jnp.tile(sin_p, (bp, 1)), jnp.tile(sin_s, (bs, 1))], axis=0)
    tn_in = _pick(nh * dk, (1024, 512, 256))

    x = (x_prompt.reshape(mp, d), x_sample.reshape(ms, d))
    if mp % tm_half or ms % tm_half:
        x = jnp.concatenate(x, axis=0)
    ret_p = ret_s = y_p = y_s = None
    lrus_p, lrus_s, convs_p, convs_s = [], [], [], []
    for l in range(depth):
        if isinstance(x, tuple):
            h = _rmsnorm(x[0], norm_mix3, l, BF16, out_rows=m)
            h = _rmsnorm(x[1], norm_mix3, l, BF16, out_row0=mp, out_rows=m, prev=h)
        else:
            h = _rmsnorm(x, norm_mix3, l, BF16)
        proj = _mm_inproj(h, w_in, (l,), cos, sin, 2 * nh * dk, dk, (g_col, ux_col),
                          (ux_col + width, ga_col), tm_big, tn_in)

        ya, ret_p = _ret_prompt(proj, ret_gn3, l, depth, bp, tp, nh, dk, dv, ret_p)
        if _fused_seqs_per_step(width, tp, bp, bs, ts) is None:
            ya, ret_s = _ret_sample(proj, ret_gn3, state_ret, l, mp, bs, ts, nh, dk, dv, ya, ret_s)
            yb, hl_p, ul_p = _lru_prompt(proj, ux_col, l, lru_w, bp, tp)
        else:
            yb, hl_p, ul_p, ya, ret_s = _lru_prompt_ret_sample(
                proj, ux_col, l, lru_w, bp, tp, ret_gn3, state_ret, mp, bs, ts, nh, dk, dv, ya, ret_s)
        yb, hl_s, us_s = _lru_sample(proj, ux_col, l, lru_w, mp, h0_rep, buf_fr, yb)
        lrus_p.append(hl_p[:, -1])
        lrus_s.append(hl_s.reshape(bs, ts, width)[:, -1])
        convs_p.append(ul_p[:, SUBLANES - (ncw - 1):])
        convs_s.append(us_s.reshape(bs, ts, width)[:, ts - (ncw - 1):])

        z = _mm_merge(ya, yb, w_ret_o, w_lru_o, (l,), proj, ga_col, ga_col + d, tm, tn_of(d) // 2)
        x = _mm_resid(z, w_out, (l,), x, tm_half if isinstance(x, tuple) else tm, tn_of(d))
        j = l // 2
        if l % 2 == 0:
            h2 = _rmsnorm(x, norm_ffn3, l, BF16)
            ff = ffn_w_gate.shape[-1]
            u = _mm_swiglu(h2, ffn_w_gate, ffn_w_up, (j,), tm_big, _pick(ff, (512, 256, 128)))
            x = _mm_resid(u, ffn_w_down, (j,), x, tm_half, _pick(d, (512, 256, 128)))
        else:
            fe = moe_w_gate.shape[-1]
            tg = _pick(TOP_K * m, (MOE_TILE, 256, 128, 64, 32, 16))
            tn_d = _pick(d, (1024, 512, 256))
            rout, hp = _router(x, norm_ffn3, l, moe_router[j])
            pos, tok, te, n_used, n_valid, p_rows = _dispatch_tables(rout, ne, tg)
            xg = _moe_gather(hp, tok, n_valid, tg, p_rows)
            ug = _gmm_swiglu(xg, moe_w_gate, moe_w_up, j, te, n_used, n_valid, tg,
                             _col_tiling(fe, 512, _pick(fe, (256, 128))))
            ygp = _gmm_down(ug, moe_w_down, j, te, n_used, n_valid, tg, tn_d)
            if l == depth - 1:
                y_p, y_s = _moe_combine(x, rout, pos, ygp, tn_d, norm_final3, mp)
            else:
                x = _moe_combine(x, rout, pos, ygp, tn_d)[0]
    if y_p is None:
        y_p = _rmsnorm(x, norm_final3, 0, F32, 0, mp)
        y_s = _rmsnorm(x, norm_final3, 0, F32, mp, ms)
    return (y_p.reshape(bp, tp, d), y_s.reshape(bs, ts, d),
            ret_p, jnp.stack(lrus_p), jnp.stack(convs_p),
            ret_s, jnp.stack(lrus_s), jnp.stack(convs_s))
```

```python
import functools

import jax
import jax.numpy as jnp
from jax import lax
from jax.experimental import pallas as pl
from jax.experimental.pallas import tpu as pltpu

F32 = jnp.float32
BF16 = jnp.bfloat16

ROPE_BASE = 10000.0
LRU_C = 8.0
EPS = 1e-6
RET_CHUNK = 128
PAST_LEN = 16384
TOP_K = 2
SUBLANES = 8
LANES = 128
VMEM_LIMIT = 56 * 1024 * 1024
MOE_TILE = 1024
GATHER_ROWS = 1024
COMBINE_ROWS = 256


def _pick(n, prefs):
    for p in prefs:
        if n % p == 0:
            return p
    return n


def _params(*sem):
    return pltpu.CompilerParams(dimension_semantics=sem, vmem_limit_bytes=VMEM_LIMIT)


def _pcall(body, *, name, grid, in_specs, args, out_specs, out_shape, sem, scratch=(), prev=(),
           num_scalar_prefetch=0):
    n_in = len(args)
    prev = tuple(prev) + (None,) * (len(out_shape) - len(prev))
    extra = [(oi, p) for oi, p in enumerate(prev) if p is not None]
    aliases = {n_in + e: oi for e, (oi, _) in enumerate(extra)}
    if extra:
        inner = body

        def body(*refs):
            return inner(*refs[:n_in], *refs[n_in + len(extra):])

    in_specs = list(in_specs) + [pl.BlockSpec(memory_space=pl.ANY)] * len(extra)
    if num_scalar_prefetch:
        grid_spec = pltpu.PrefetchScalarGridSpec(
            num_scalar_prefetch=num_scalar_prefetch, grid=grid, in_specs=in_specs[num_scalar_prefetch:],
            out_specs=tuple(out_specs), scratch_shapes=list(scratch))
        call = pl.pallas_call(body, grid_spec=grid_spec, out_shape=tuple(out_shape),
                              input_output_aliases=aliases, compiler_params=_params(*sem), name=name)
    else:
        call = pl.pallas_call(body, grid=grid, in_specs=in_specs, out_specs=tuple(out_specs),
                              out_shape=tuple(out_shape), scratch_shapes=list(scratch),
                              input_output_aliases=aliases, compiler_params=_params(*sem), name=name)
    return call(*args, *[p for _, p in extra])


def _wspec(pre, k, tn):
    return pl.BlockSpec((None,) * len(pre) + (k, tn), lambda j, i: pre + (0, j))


def _vspec(l, n):
    return pl.BlockSpec((None, 1, n), lambda *_: (l, 0, 0))


def _rms(x, g):
    return (x * lax.rsqrt(jnp.mean(x * x, axis=-1, keepdims=True) + EPS)) * g


def _rmsnorm_kernel(x_ref, g_ref, o_ref):
    o_ref[...] = _rms(x_ref[...], g_ref[...]).astype(o_ref.dtype)


def _rmsnorm(x, g3, l, out_dtype, row0=0, rows=None, out_row0=0, out_rows=None, prev=None):
    d = x.shape[1]
    rows = x.shape[0] if rows is None else rows
    out_rows = rows if out_rows is None else out_rows
    tm = _pick(rows, (512, 256, 128, 64, 32, 16, 8))
    assert row0 % tm == 0 and out_row0 % tm == 0
    rb0, ob0 = row0 // tm, out_row0 // tm
    return _pcall(
        _rmsnorm_kernel, name="rmsnorm", grid=(rows // tm,),
        in_specs=[pl.BlockSpec((tm, d), lambda i: (rb0 + i, 0)), _vspec(l, d)],
        args=(x, g3),
        out_specs=[pl.BlockSpec((tm, d), lambda i: (ob0 + i, 0))],
        out_shape=[jax.ShapeDtypeStruct((out_rows, d), out_dtype)],
        prev=(prev,), sem=("arbitrary",))[0]


def _cast_w(w_ref, wb_ref):
    k = w_ref.shape[0]
    ck = _pick(k, (256, 128, 64, 32, 16))

    def body(c, carry):
        r = pl.multiple_of(c * ck, ck)
        wb_ref[pl.ds(r, ck), :] = w_ref[pl.ds(r, ck), :].astype(BF16)
        return carry

    lax.fori_loop(0, k // ck, body, 0)


def _mm_inproj_kernel(x_ref, w_ref, cos_ref, sin_ref, o_ref, wb_ref, *, nrot, dk, silu_tiles, gelu_tiles):
    j = pl.program_id(0)

    @pl.when(pl.program_id(1) == 0)
    def _():
        _cast_w(w_ref, wb_ref)

    tm, tn = o_ref.shape
    nchunk = 4 if tm % (4 * 16) == 0 else 1

    def row_chunks(epilogue):
        rc = tm // nchunk
        for c in range(nchunk):
            rs = slice(c * rc, (c + 1) * rc)
            epilogue(jnp.dot(x_ref[rs, :], wb_ref[...], preferred_element_type=F32), rs)

    def store(fn):
        def epilogue(acc, rs):
            o_ref[rs, :] = fn(acc).astype(o_ref.dtype)
        return epilogue

    in_tiles = lambda t: jnp.logical_and(j >= t[0], j < t[1])
    is_silu, is_gelu = in_tiles(silu_tiles), in_tiles(gelu_tiles)

    @pl.when(jnp.logical_and(j >= nrot, jnp.logical_not(jnp.logical_or(is_silu, is_gelu))))
    def _():
        row_chunks(store(lambda acc: acc))

    @pl.when(is_silu)
    def _():
        row_chunks(store(jax.nn.silu))

    @pl.when(is_gelu)
    def _():
        row_chunks(store(jax.nn.gelu))

    @pl.when(j < nrot)
    def _():
        scale = jnp.where(j >= nrot // 2, F32(dk ** -0.5), F32(1.0))
        half = dk // 2

        def rotary(acc, rs):
            cos = cos_ref[rs, :] * scale
            sin = sin_ref[rs, :] * scale
            for h in range(tn // dk):
                x1 = acc[:, h * dk:h * dk + half]
                x2 = acc[:, h * dk + half:(h + 1) * dk]
                o_ref[rs, h * dk:h * dk + half] = (x1 * cos - x2 * sin).astype(o_ref.dtype)
                o_ref[rs, h * dk + half:(h + 1) * dk] = (x1 * sin + x2 * cos).astype(o_ref.dtype)

        row_chunks(rotary)


def _mm_inproj(x, w, pre, cos, sin, qk_cols, dk, silu_cols, gelu_cols, tm, tn):
    m, k = x.shape
    n = w.shape[-1]
    half = cos.shape[1]
    assert qk_cols % (2 * tn) == 0 and tn % dk == 0
    assert all(c % tn == 0 for c in silu_cols + gelu_cols)
    tiles = lambda cols: (cols[0] // tn, cols[1] // tn)
    return _pcall(
        functools.partial(_mm_inproj_kernel, nrot=qk_cols // tn, dk=dk, silu_tiles=tiles(silu_cols),
                          gelu_tiles=tiles(gelu_cols)), name="mm_inproj",
        grid=(n // tn, m // tm),
        in_specs=[pl.BlockSpec((tm, k), lambda j, i: (i, 0)), _wspec(pre, k, tn),
                  pl.BlockSpec((tm, half), lambda j, i: (i, 0)),
                  pl.BlockSpec((tm, half), lambda j, i: (i, 0))],
        args=(x, w, cos, sin),
        out_specs=[pl.BlockSpec((tm, tn), lambda j, i: (i, j))],
        out_shape=[jax.ShapeDtypeStruct((m, n), BF16)],
        scratch=[pltpu.VMEM((k, tn), BF16)],
        sem=("arbitrary", "arbitrary"))[0]


def _mm_resid_kernel(x_ref, w_ref, r_ref, o_ref, wb_ref):
    @pl.when(pl.program_id(1) == 0)
    def _():
        _cast_w(w_ref, wb_ref)

    o_ref[...] = r_ref[...] + jnp.dot(x_ref[...], wb_ref[...], preferred_element_type=F32)


def _mm_resid2_kernel(x_ref, w_ref, ra_ref, rb_ref, o_ref, wb_ref, *, na):
    @pl.when(pl.program_id(1) == 0)
    def _():
        _cast_w(w_ref, wb_ref)

    y = jnp.dot(x_ref[...], wb_ref[...], preferred_element_type=F32)
    i = pl.program_id(1)

    @pl.when(i < na)
    def _():
        o_ref[...] = ra_ref[...] + y

    @pl.when(i >= na)
    def _():
        o_ref[...] = rb_ref[...] + y


def _mm_resid(x, w, pre, r, tm, tn):
    m, k = x.shape
    n = w.shape[-1]
    if isinstance(r, tuple):
        ra, rb = r
        assert ra.shape[0] % tm == 0 and rb.shape[0] % tm == 0 and ra.shape[0] + rb.shape[0] == m
        na = ra.shape[0] // tm
        body = functools.partial(_mm_resid2_kernel, na=na)
        r_specs = [pl.BlockSpec((tm, tn), lambda j, i: (jnp.minimum(i, na - 1), j)),
                   pl.BlockSpec((tm, tn), lambda j, i: (jnp.maximum(i - na, 0), j))]
        r_args = (ra, rb)
    else:
        body, r_specs, r_args = _mm_resid_kernel, [pl.BlockSpec((tm, tn), lambda j, i: (i, j))], (r,)
    return _pcall(
        body, name="mm_resid", grid=(n // tn, m // tm),
        in_specs=[pl.BlockSpec((tm, k), lambda j, i: (i, 0)), _wspec(pre, k, tn)] + r_specs,
        args=(x, w) + r_args,
        out_specs=[pl.BlockSpec((tm, tn), lambda j, i: (i, j))],
        out_shape=[jax.ShapeDtypeStruct((m, n), F32)],
        scratch=[pltpu.VMEM((k, tn), BF16)],
        sem=("arbitrary", "arbitrary"))[0]


def _row_chunks(rows, fn, nchunk=4):
    n = nchunk if rows % (nchunk * 16) == 0 else 1
    rc = rows // n
    for c in range(n):
        fn(slice(c * rc, (c + 1) * rc))


def _swiglu_rows(x_ref, wgb_ref, wub_ref, o_ref):
    def chunk(rs):
        x = x_ref[rs, :]
        g = jnp.dot(x, wgb_ref[...], preferred_element_type=F32)
        u = jnp.dot(x, wub_ref[...], preferred_element_type=F32)
        o_ref[rs, :] = (jax.nn.silu(g) * u).astype(o_ref.dtype)

    _row_chunks(o_ref.shape[0], chunk)


def _mm_swiglu_kernel(x_ref, wg_ref, wu_ref, o_ref, wgb_ref, wub_ref):
    @pl.when(pl.program_id(1) == 0)
    def _():
        _cast_w(wg_ref, wgb_ref)
        _cast_w(wu_ref, wub_ref)

    _swiglu_rows(x_ref, wgb_ref, wub_ref, o_ref)


def _mm_swiglu(x, wg, wu, pre, tm, tn):
    m, k = x.shape
    n = wg.shape[-1]
    return _pcall(
        _mm_swiglu_kernel, name="mm_swiglu", grid=(n // tn, m // tm),
        in_specs=[pl.BlockSpec((tm, k), lambda j, i: (i, 0)), _wspec(pre, k, tn), _wspec(pre, k, tn)],
        args=(x, wg, wu),
        out_specs=[pl.BlockSpec((tm, tn), lambda j, i: (i, j))],
        out_shape=[jax.ShapeDtypeStruct((m, n), BF16)],
        scratch=[pltpu.VMEM((k, tn), BF16), pltpu.VMEM((k, tn), BF16)],
        sem=("arbitrary", "arbitrary"))[0]


def _mm_merge_kernel(a_ref, b_ref, wa_ref, wb_ref, ga_ref, gb_ref, o_ref, wab_ref, wbb_ref):
    @pl.when(pl.program_id(1) == 0)
    def _():
        _cast_w(wa_ref, wab_ref)
        _cast_w(wb_ref, wbb_ref)

    def chunk(rs):
        ya = jnp.dot(a_ref[rs, :], wab_ref[...], preferred_element_type=F32)
        yb = jnp.dot(b_ref[rs, :], wbb_ref[...], preferred_element_type=F32)
        ga = ga_ref[rs, :].astype(F32)
        gb = gb_ref[rs, :].astype(F32)
        o_ref[rs, :] = (jax.nn.sigmoid(ga) * ya + jax.nn.sigmoid(gb) * yb).astype(o_ref.dtype)

    _row_chunks(o_ref.shape[0], chunk)


def _mm_merge(a, b, wa, wb, pre, proj, ga_col, gb_col, tm, tn):
    m, k = a.shape
    n = wa.shape[-1]
    ga_blk, gb_blk = ga_col // tn, gb_col // tn
    return _pcall(
        _mm_merge_kernel, name="mm_merge", grid=(n // tn, m // tm),
        in_specs=[pl.BlockSpec((tm, k), lambda j, i: (i, 0)),
                  pl.BlockSpec((tm, k), lambda j, i: (i, 0)),
                  _wspec(pre, k, tn), _wspec(pre, k, tn),
                  pl.BlockSpec((tm, tn), lambda j, i: (i, ga_blk + j)),
                  pl.BlockSpec((tm, tn), lambda j, i: (i, gb_blk + j))],
        args=(a, b, wa, wb, proj, proj),
        out_specs=[pl.BlockSpec((tm, tn), lambda j, i: (i, j))],
        out_shape=[jax.ShapeDtypeStruct((m, n), BF16)],
        scratch=[pltpu.VMEM((k, tn), BF16), pltpu.VMEM((k, tn), BF16)],
        sem=("arbitrary", "arbitrary"))[0]


def _rope_kernel(inv_ref, cos_ref, sin_ref, *, pos0):
    t, half = cos_ref.shape
    pos = lax.broadcasted_iota(jnp.int32, (t, half), 0).astype(F32) + F32(pos0)
    ang = pos * inv_ref[...]
    cos_ref[...] = jnp.cos(ang)
    sin_ref[...] = jnp.sin(ang)


def _rope_tables(t, half, pos0):
    inv = ROPE_BASE ** (-jnp.arange(half, dtype=F32) / half)
    return pl.pallas_call(
        functools.partial(_rope_kernel, pos0=pos0),
        out_shape=(jax.ShapeDtypeStruct((t, half), F32), jax.ShapeDtypeStruct((t, half), F32)),
        name="rope_tables",
    )(inv.reshape(1, half))


def _decay_tables(c, h, dk, dv):
    log_g = jnp.log1p(-jnp.exp2(-5.0 - jnp.arange(h, dtype=F32)))
    idx = jnp.arange(c)
    rel = idx[:, None] - idx[None, :]
    dmask = jnp.where(rel[None] >= 0,
                      jnp.exp(jnp.maximum(rel, 0)[None].astype(F32) * log_g[:, None, None]), 0.0)
    xi = jnp.exp((idx + 1).astype(F32)[None, :] * log_g[:, None])
    zeta = jnp.exp((c - 1 - idx).astype(F32)[None, :] * log_g[:, None])
    g_c = jnp.exp(c * log_g)
    xi_t = jnp.broadcast_to(xi[:, :, None], (h, c, dv))
    zeta_t = jnp.broadcast_to(zeta[:, :, None], (h, c, dk))
    return dmask, xi_t, zeta_t, g_c


def _ret_head(qb, kb, v, g, dm, xi, zt, gn, s, gc):
    scores = lax.dot_general(qb, kb, (((1,), (1,)), ((), ())), preferred_element_type=F32)
    intra = jnp.dot((scores * dm).astype(BF16), v, preferred_element_type=F32)
    cross = jnp.dot(qb, s.astype(BF16), preferred_element_type=F32) * xi
    kz = (kb.astype(F32) * zt).astype(BF16)
    upd = lax.dot_general(kz, v, (((0,), (0,)), ((), ())), preferred_element_type=F32)
    s_new = gc * s + upd
    o = intra + cross
    mu = jnp.mean(o, axis=-1, keepdims=True)
    d = o - mu
    var = jnp.mean(d * d, axis=-1, keepdims=True)
    y = d * lax.rsqrt(var + EPS) * gn
    return (g * y).astype(BF16), s_new


def _ret_prompt_kernel(gc_ref, q_ref, k_ref, v_ref, g_ref, dm_ref, xi_ref, zt_ref,
                       gn_ref, o_ref, s_ref, *, nh, dk, dv):
    @pl.when(pl.program_id(1) == 0)
    def _():
        s_ref[...] = jnp.zeros_like(s_ref)

    c = dm_ref.shape[1]
    for h in range(nh):
        s = s_ref[h]
        for ci in range(q_ref.shape[0] // c):
            rs = slice(ci * c, (ci + 1) * c)
            out, s = _ret_head(
                q_ref[rs, h * dk:(h + 1) * dk], k_ref[rs, h * dk:(h + 1) * dk],
                v_ref[rs, h * dv:(h + 1) * dv], g_ref[rs, h * dv:(h + 1) * dv].astype(F32),
                dm_ref[h], xi_ref[h], zt_ref[h], gn_ref[:, h * dv:(h + 1) * dv], s, gc_ref[h])
            o_ref[rs, h * dv:(h + 1) * dv] = out
        s_ref[h] = s


def _ret_prompt(proj, gn3, l, depth, b, t, nh, dk, dv, s_prev):
    cl = RET_CHUNK if t % RET_CHUNK == 0 else t
    cps = _pick(t // cl, (4, 2, 1))
    c = cl * cps
    nc = t // c
    w = nh * dk
    m = proj.shape[0]
    dmask, xi_t, zeta_t, g_c = _decay_tables(cl, nh, dk, dv)
    row = lambda bi, ci: bi * nc + ci
    full3 = lambda bi, ci: (0, 0, 0)
    return _pcall(
        functools.partial(_ret_prompt_kernel, nh=nh, dk=dk, dv=dv), name="ret_prompt", grid=(b, nc),
        in_specs=[pl.BlockSpec(memory_space=pltpu.SMEM),
                  pl.BlockSpec((c, w), lambda bi, ci: (row(bi, ci), 0)),
                  pl.BlockSpec((c, w), lambda bi, ci: (row(bi, ci), 1)),
                  pl.BlockSpec((c, w), lambda bi, ci: (row(bi, ci), 2)),
                  pl.BlockSpec((c, w), lambda bi, ci: (row(bi, ci), 3)),
                  pl.BlockSpec((nh, cl, cl), full3),
                  pl.BlockSpec((nh, cl, dv), full3),
                  pl.BlockSpec((nh, cl, dk), full3),
                  _vspec(l, nh * dv)],
        args=(g_c, proj, proj, proj, proj, dmask, xi_t, zeta_t, gn3),
        out_specs=[pl.BlockSpec((c, nh * dv), lambda bi, ci: (row(bi, ci), 0)),
                   pl.BlockSpec((None, None, nh, dk, dv), lambda bi, ci: (l, bi, 0, 0, 0))],
        out_shape=[jax.ShapeDtypeStruct((m, nh * dv), BF16),
                   jax.ShapeDtypeStruct((depth, b, nh, dk, dv), F32)],
        prev=(None, s_prev),
        sem=("arbitrary", "arbitrary"))


def _ret_sample_kernel(gc_ref, q_ref, k_ref, v_ref, g_ref, dm_ref, xi_ref, zt_ref,
                       gn_ref, s0_ref, o_ref, s_ref, *, nh, dk, dv, bb, ts):
    q = q_ref[...].astype(F32)
    k = k_ref[...].astype(F32)
    v = v_ref[...].astype(F32)
    g = g_ref[...].astype(F32)
    seqs = []
    for i in range(bb):
        r0, r1 = i * ts, (i + 1) * ts
        outs = []
        for h in range(nh):
            out, s_new = _ret_head(
                q[r0:r1, h * dk:(h + 1) * dk].astype(BF16), k[r0:r1, h * dk:(h + 1) * dk].astype(BF16),
                v[r0:r1, h * dv:(h + 1) * dv].astype(BF16), g[r0:r1, h * dv:(h + 1) * dv],
                dm_ref[h], xi_ref[h], zt_ref[h], gn_ref[:, h * dv:(h + 1) * dv],
                s0_ref[i, h], gc_ref[h])
            s_ref[i, h] = s_new
            outs.append(out.astype(F32))
        seqs.append(jnp.concatenate(outs, axis=-1))
    o_ref[...] = jnp.concatenate(seqs, axis=0).astype(o_ref.dtype)


def _ret_sample(proj, gn3, s0_all, l, row0, bs, ts, nh, dk, dv, ya_prev, s_prev):
    bb = _pick(bs, (4, 2, 1))
    rows = bb * ts
    w = nh * dk
    depth = s0_all.shape[0]
    dmask, xi_t, zeta_t, g_c = _decay_tables(ts, nh, dk, dv)
    assert row0 % rows == 0
    rb0 = row0 // rows
    full3 = lambda i: (0, 0, 0)
    return _pcall(
        functools.partial(_ret_sample_kernel, nh=nh, dk=dk, dv=dv, bb=bb, ts=ts), name="ret_sample",
        grid=(bs // bb,),
        in_specs=[pl.BlockSpec(memory_space=pltpu.SMEM),
                  pl.BlockSpec((rows, w), lambda i: (rb0 + i, 0)),
                  pl.BlockSpec((rows, w), lambda i: (rb0 + i, 1)),
                  pl.BlockSpec((rows, w), lambda i: (rb0 + i, 2)),
                  pl.BlockSpec((rows, w), lambda i: (rb0 + i, 3)),
                  pl.BlockSpec((nh, ts, ts), full3),
                  pl.BlockSpec((nh, ts, dv), full3),
                  pl.BlockSpec((nh, ts, dk), full3),
                  _vspec(l, nh * dv),
                  pl.BlockSpec((None, bb, nh, dk, dv), lambda i: (l, i, 0, 0, 0))],
        args=(g_c, proj, proj, proj, proj, dmask, xi_t, zeta_t, gn3, s0_all),
        out_specs=[pl.BlockSpec((rows, nh * dv), lambda i: (rb0 + i, 0)),
                   pl.BlockSpec((None, bb, nh, dk, dv), lambda i: (l, i, 0, 0, 0))],
        out_shape=[jax.ShapeDtypeStruct(ya_prev.shape, BF16),
                   jax.ShapeDtypeStruct((depth, bs, nh, dk, dv), F32)],
        prev=(ya_prev, s_prev),
        sem=("arbitrary",))


def _lru_gates(uc, wr_ref, br_ref, wi_ref, bi_ref, lam_ref):
    nb, lb, _ = wr_ref.shape
    ucb = uc.astype(BF16)
    rl, il = [], []
    for n in range(nb):
        xb = ucb[:, n * lb:(n + 1) * lb]
        rl.append(jnp.dot(xb, wr_ref[n].astype(BF16), preferred_element_type=F32))
        il.append(jnp.dot(xb, wi_ref[n].astype(BF16), preferred_element_type=F32))
    r = jax.nn.sigmoid(jnp.concatenate(rl, axis=-1) + br_ref[...])
    i = jax.nn.sigmoid(jnp.concatenate(il, axis=-1) + bi_ref[...])
    log_a = -LRU_C * r * jax.nn.softplus(-lam_ref[...])
    a = jnp.exp(log_a)
    z = -jnp.tanh(log_a) * (a * a + 1.0)
    bx = jnp.where(z == 0.0, 0.0, z * lax.rsqrt(z)) * (i * uc)
    return a, bx


def _blocks(x):
    return [x[lo:lo + SUBLANES, :] for lo in range(0, x.shape[0], SUBLANES)]


def _conv_blocks(u_blocks, prev_blocks, cw_ref, cb_ref, rows):
    ncw = cw_ref.shape[0]
    rolled = {}

    def rot(x, s):
        key = (id(x), s)
        if key not in rolled:
            rolled[key] = pltpu.roll(x, s, axis=0)
        return rolled[key]

    out = []
    for ub, pb in zip(u_blocks, prev_blocks):
        uc = cb_ref[...]
        for j in range(ncw):
            s = ncw - 1 - j
            term = ub if s == 0 else jnp.where(rows >= s, rot(ub, s), rot(pb, s))
            uc = uc + term * cw_ref[j:j + 1, :]
        out.append(uc)
    return jnp.concatenate(out, axis=0)


def _scan_block(a, bx, rows):
    d = 1
    while d < SUBLANES:
        m = rows >= d
        bx = bx + a * jnp.where(m, pltpu.roll(bx, d, axis=0), 0.0)
        a = a * jnp.where(m, pltpu.roll(a, d, axis=0), 1.0)
        d *= 2
    return a, bx


def _lru_prompt_kernel(ux_ref, uy_ref, cw_ref, cb_ref, wr_ref, br_ref, wi_ref, bi_ref, lam_ref,
                       o_ref, hl_ref, ul_ref, hc_ref):
    r, cw = ux_ref.shape

    @pl.when(pl.program_id(2) == 0)
    def _():
        hc_ref[...] = jnp.zeros_like(hc_ref)
        ul_ref[...] = jnp.zeros_like(ul_ref)

    rows = lax.broadcasted_iota(jnp.int32, (SUBLANES, cw), 0)
    ub = _blocks(ux_ref[...].astype(F32))
    uc = _conv_blocks(ub, [ul_ref[...]] + ub[:-1], cw_ref, cb_ref, rows)
    ul_ref[...] = ub[-1]
    a, bx = _lru_gates(uc, wr_ref, br_ref, wi_ref, bi_ref, lam_ref)
    carry = hc_ref[...]
    hs = []
    for ab, bb in zip(_blocks(a), _blocks(bx)):
        ab, bb = _scan_block(ab, bb, rows)
        hb = ab * carry + bb
        hs.append(hb)
        carry = jnp.broadcast_to(hb[SUBLANES - 1:SUBLANES, :], (SUBLANES, cw))
    hc_ref[...] = carry
    hl_ref[...] = hs[-1]
    h = jnp.concatenate(hs, axis=0)
    o_ref[...] = (h * uy_ref[...].astype(F32)).astype(o_ref.dtype)


def _lru_sample_kernel(ux_ref, uy_ref, cw_ref, cb_ref, wr_ref, br_ref, wi_ref, bi_ref, lam_ref,
                       h0_ref, buf_ref, o_ref, hl_ref, us_ref):
    r, cw = ux_ref.shape
    rows = lax.broadcasted_iota(jnp.int32, (SUBLANES, cw), 0)
    u = ux_ref[...].astype(F32)
    us_ref[...] = u
    uc = _conv_blocks(_blocks(u), _blocks(buf_ref[...]), cw_ref, cb_ref, rows)
    a, bx = _lru_gates(uc, wr_ref, br_ref, wi_ref, bi_ref, lam_ref)
    hs = []
    for ab, bb, h0 in zip(_blocks(a), _blocks(bx), _blocks(h0_ref[...])):
        ab, bb = _scan_block(ab, bb, rows)
        hs.append(ab * h0 + bb)
    h = jnp.concatenate(hs, axis=0)
    hl_ref[...] = h
    o_ref[...] = (h * uy_ref[...].astype(F32)).astype(o_ref.dtype)


def _lru_specs(l, r, cw, ncw, lb, ux_blk, uy_blk, row_of, cb_of):
    nbc = cw // lb
    vec = pl.BlockSpec((None, 1, cw), lambda *g: (l, 0, cb_of(*g)))
    gate = pl.BlockSpec((None, nbc, lb, lb), lambda *g: (l, cb_of(*g), 0, 0))
    return [pl.BlockSpec((r, cw), lambda *g: (row_of(*g), ux_blk + cb_of(*g))),
            pl.BlockSpec((r, cw), lambda *g: (row_of(*g), uy_blk + cb_of(*g))),
            pl.BlockSpec((None, ncw, cw), lambda *g: (l, 0, cb_of(*g))),
            vec, gate, vec, gate, vec, vec]


def _lru_prompt(proj, ux_col, l, lru_w, b, t):
    conv_w, conv_b3, w_r, b_r3, w_i, b_i3, lam3 = lru_w
    width = lam3.shape[-1]
    lb = w_r.shape[-1]
    ncw = conv_w.shape[1]
    m = proj.shape[0]
    cw = _pick(width, (1024, 512, 256, 128))
    r = _pick(t, (256, 128, 64, 32, 16, 8))
    nt = t // r
    specs = _lru_specs(l, r, cw, ncw, lb, ux_col // cw, (ux_col + width) // cw,
                       lambda bi, cb, ti: bi * nt + ti, lambda bi, cb, ti: cb)
    last8 = pl.BlockSpec((None, SUBLANES, cw), lambda bi, cb, ti: (bi, 0, cb))
    return _pcall(
        _lru_prompt_kernel, name="lru_prompt", grid=(b, width // cw, nt),
        in_specs=specs,
        args=(proj, proj, conv_w, conv_b3, w_r, b_r3, w_i, b_i3, lam3),
        out_specs=[pl.BlockSpec((r, cw), lambda bi, cb, ti: (bi * nt + ti, cb)), last8, last8],
        out_shape=[jax.ShapeDtypeStruct((m, width), BF16),
                   jax.ShapeDtypeStruct((b, SUBLANES, width), F32),
                   jax.ShapeDtypeStruct((b, SUBLANES, width), F32)],
        scratch=[pltpu.VMEM((SUBLANES, cw), F32)],
        sem=("arbitrary", "arbitrary", "arbitrary"))


def _lru_ret_kernel(*refs, nh, dk, dv, bb, ts):
    lru_in, ret_in = refs[:9], refs[9:19]
    yb_ref, hl_ref, ul_ref, ya_ref, s_ref, hc_ref = refs[19:]
    _lru_prompt_kernel(*lru_in, yb_ref, hl_ref, ul_ref, hc_ref)
    _ret_sample_kernel(*ret_in, ya_ref, s_ref, nh=nh, dk=dk, dv=dv, bb=bb, ts=ts)


def _fused_seqs_per_step(width, t, b, bs, ts):
    cw = _pick(width, (1024, 512, 256, 128))
    r = _pick(t, (256, 128, 64, 32, 16, 8))
    steps = b * (width // cw) * (t // r)
    if bs % steps:
        return None
    bb = bs // steps
    return bb if bb <= 4 and (bb * ts) % 16 == 0 else None


def _lru_prompt_ret_sample(proj, ux_col, l, lru_w, b, t, gn3, s0_all, row0, bs, ts, nh, dk, dv,
                           ya_prev, s_prev):
    conv_w, conv_b3, w_r, b_r3, w_i, b_i3, lam3 = lru_w
    width = lam3.shape[-1]
    lb = w_r.shape[-1]
    ncw = conv_w.shape[1]
    m = proj.shape[0]
    depth = s0_all.shape[0]
    cw = _pick(width, (1024, 512, 256, 128))
    r = _pick(t, (256, 128, 64, 32, 16, 8))
    nt, ncb = t // r, width // cw
    bb = _fused_seqs_per_step(width, t, b, bs, ts)
    rows = bb * ts
    w = nh * dk
    assert row0 % rows == 0
    rb0 = row0 // rows
    lin = lambda bi, cb, ti: (bi * ncb + cb) * nt + ti
    dmask, xi_t, zeta_t, g_c = _decay_tables(ts, nh, dk, dv)
    lru_specs = _lru_specs(l, r, cw, ncw, lb, ux_col // cw, (ux_col + width) // cw,
                           lambda bi, cb, ti: bi * nt + ti, lambda bi, cb, ti: cb)
    full3 = lambda *g: (0, 0, 0)
    qkvg = [pl.BlockSpec((rows, w), lambda *g, c=c: (rb0 + lin(*g), c)) for c in range(4)]
    state = pl.BlockSpec((None, bb, nh, dk, dv), lambda *g: (l, lin(*g), 0, 0, 0))
    ret_specs = ([pl.BlockSpec(memory_space=pltpu.SMEM)] + qkvg +
                 [pl.BlockSpec((nh, ts, ts), full3), pl.BlockSpec((nh, ts, dv), full3),
                  pl.BlockSpec((nh, ts, dk), full3), _vspec(l, nh * dv), state])
    last8 = pl.BlockSpec((None, SUBLANES, cw), lambda bi, cb, ti: (bi, 0, cb))
    return _pcall(
        functools.partial(_lru_ret_kernel, nh=nh, dk=dk, dv=dv, bb=bb, ts=ts), name="lru_prompt_ret_sample",
        grid=(b, ncb, nt),
        in_specs=lru_specs + ret_specs,
        args=(proj, proj, conv_w, conv_b3, w_r, b_r3, w_i, b_i3, lam3,
              g_c, proj, proj, proj, proj, dmask, xi_t, zeta_t, gn3, s0_all),
        out_specs=[pl.BlockSpec((r, cw), lambda bi, cb, ti: (bi * nt + ti, cb)), last8, last8,
                   pl.BlockSpec((rows, nh * dv), lambda *g: (rb0 + lin(*g), 0)), state],
        out_shape=[jax.ShapeDtypeStruct((m, width), BF16),
                   jax.ShapeDtypeStruct((b, SUBLANES, width), F32),
                   jax.ShapeDtypeStruct((b, SUBLANES, width), F32),
                   jax.ShapeDtypeStruct(ya_prev.shape, BF16),
                   jax.ShapeDtypeStruct((depth, bs, nh, dk, dv), F32)],
        prev=(None, None, None, ya_prev, s_prev),
        scratch=[pltpu.VMEM((SUBLANES, cw), F32)],
        sem=("arbitrary", "arbitrary", "arbitrary"))


def _lru_sample(proj, ux_col, l, lru_w, row0, h0_rep, buf_fr, yb_prev):
    conv_w, conv_b3, w_r, b_r3, w_i, b_i3, lam3 = lru_w
    width = lam3.shape[-1]
    lb = w_r.shape[-1]
    ncw = conv_w.shape[1]
    ms = h0_rep.shape[1]
    cw = _pick(width, (1024, 512, 256, 128))
    r = _pick(ms, (256, 128, 64, 32, 16, 8))
    assert row0 % r == 0
    rb0 = row0 // r
    specs = _lru_specs(l, r, cw, ncw, lb, ux_col // cw, (ux_col + width) // cw,
                       lambda ri, cb: rb0 + ri, lambda ri, cb: cb)
    st = pl.BlockSpec((None, r, cw), lambda ri, cb: (l, ri, cb))
    f32rows = pl.BlockSpec((r, cw), lambda ri, cb: (ri, cb))
    return _pcall(
        _lru_sample_kernel, name="lru_sample", grid=(ms // r, width // cw),
        in_specs=specs + [st, st],
        args=(proj, proj, conv_w, conv_b3, w_r, b_r3, w_i, b_i3, lam3, h0_rep, buf_fr),
        out_specs=[pl.BlockSpec((r, cw), lambda ri, cb: (rb0 + ri, cb)), f32rows, f32rows],
        out_shape=[jax.ShapeDtypeStruct(yb_prev.shape, BF16),
                   jax.ShapeDtypeStruct((ms, width), F32), jax.ShapeDtypeStruct((ms, width), F32)],
        prev=(yb_prev,),
        sem=("arbitrary", "arbitrary"))


def _pack_pair(hi, lo):
    hb = lax.bitcast_convert_type(hi.astype(jnp.bfloat16).astype(F32), jnp.uint32)
    lb = lax.bitcast_convert_type(lo.astype(jnp.bfloat16).astype(F32), jnp.uint32)
    return hb | (lb >> 16)


def _unpack_pair(p):
    hi = lax.bitcast_convert_type(p & jnp.uint32(0xFFFF0000), F32)
    lo = lax.bitcast_convert_type(p << 16, F32)
    return hi, lo


def _router_kernel(x_ref, g_ref, wr_ref, o_ref, hp_ref, *, ne):
    hf = _rms(x_ref[...], g_ref[...])
    half = hf.shape[1] // 2
    hp_ref[...] = _pack_pair(hf[:, :half], hf[:, half:])
    h = hf.astype(BF16)
    logits = jnp.dot(h, wr_ref[...].astype(BF16), preferred_element_type=F32)
    lane = lax.broadcasted_iota(jnp.int32, logits.shape, 1).astype(F32)
    neg = F32(-jnp.inf)
    big = F32(LANES)
    l1 = jnp.where(lane < ne, logits, neg)
    m1 = jnp.max(l1, axis=-1, keepdims=True)
    i1 = jnp.min(jnp.where(l1 == m1, lane, big), axis=-1, keepdims=True)
    l2 = jnp.where(lane == i1, neg, l1)
    m2 = jnp.max(l2, axis=-1, keepdims=True)
    i2 = jnp.min(jnp.where(l2 == m2, lane, big), axis=-1, keepdims=True)
    e2 = jnp.exp(m2 - m1)
    den = 1.0 + e2
    o_ref[...] = (jnp.where(lane == 0.0, i1, 0.0) + jnp.where(lane == 1.0, i2, 0.0)
                  + jnp.where(lane == 2.0, 1.0 / den, 0.0) + jnp.where(lane == 3.0, e2 / den, 0.0))


def _router(x, g3, l, wr):
    m, d = x.shape
    ne = wr.shape[-1]
    assert TOP_K == 2 and ne <= LANES
    wr_pad = jnp.pad(wr, ((0, 0), (0, LANES - ne)))
    tm = _pick(m, (512, 256, 128, 64, 32, 16, 8))
    return _pcall(
        functools.partial(_router_kernel, ne=ne), name="router", grid=(m // tm,),
        in_specs=[pl.BlockSpec((tm, d), lambda i: (i, 0)), _vspec(l, d),
                  pl.BlockSpec((d, LANES), lambda i: (0, 0))],
        args=(x, g3, wr_pad),
        out_specs=[pl.BlockSpec((tm, LANES), lambda i: (i, 0)),
                   pl.BlockSpec((tm, d // 2), lambda i: (i, 0))],
        out_shape=[jax.ShapeDtypeStruct((m, LANES), F32),
                   jax.ShapeDtypeStruct((m, d // 2), jnp.uint32)],
        sem=("arbitrary",))


def _dispatch_tables(rout, ne, tg):
    m = rout.shape[0]
    na = TOP_K * m
    eid = rout[:, :TOP_K].astype(jnp.int32).T.reshape(na)
    onehot = (eid[:, None] == jnp.arange(ne, dtype=jnp.int32)[None, :]).astype(jnp.int32)
    cnt = jnp.sum(onehot, axis=0)
    rank = jnp.sum((jnp.cumsum(onehot, axis=0) - onehot) * onehot, axis=1)
    padded = ((cnt + tg - 1) // tg) * tg
    gend = jnp.cumsum(padded)
    gstart = gend - padded
    rem = cnt % tg
    first = jnp.where(rem > 0, rem, tg)
    local = jnp.where(rank < first[eid], rank, rank - first[eid] + tg)
    pos = gstart[eid] + local
    p_rows = ((na + tg - 1) // tg + ne) * tg
    n_tiles = p_rows // tg
    tok = jnp.zeros((p_rows,), jnp.int32).at[pos].set(jnp.arange(na, dtype=jnp.int32) % m)
    n_used = (gend[-1] // tg).astype(jnp.int32)
    tstart = jnp.arange(n_tiles, dtype=jnp.int32) * tg
    tstart = jnp.minimum(tstart, (n_used - 1) * tg)
    te = jnp.minimum(jnp.searchsorted(gend, tstart, side="right"), ne - 1).astype(jnp.int32)
    tidx = jnp.arange(n_tiles, dtype=jnp.int32)
    n_valid = jnp.where(tidx * tg == gstart[te], first[te], tg)
    n_valid = jnp.where(tidx < n_used, n_valid, 0).astype(jnp.int32)
    return pos, tok, te, n_used.reshape(1), n_valid, p_rows


DMA_UNROLL = 8


def _row_gather(src_hbm, dst, idx_of, sem):
    rows = dst.shape[0]
    assert rows % DMA_UNROLL == 0

    def row_copy(r, t):
        return pltpu.make_async_copy(src_hbm.at[pl.ds(t, 1), :], dst.at[pl.ds(r, 1), :], sem)

    def issue(c, carry):
        for u in range(DMA_UNROLL):
            r = c * DMA_UNROLL + u
            row_copy(r, idx_of(r)).start()
        return carry

    def drain(c, carry):
        for u in range(DMA_UNROLL):
            row_copy(c * DMA_UNROLL + u, 0).wait()
        return carry

    start = lambda: lax.fori_loop(0, rows // DMA_UNROLL, issue, 0)
    wait = lambda: lax.fori_loop(0, rows // DMA_UNROLL, drain, 0)
    return start, wait


def _moe_gather_kernel(tok_ref, gv_ref, hp_ref, o_ref):
    tr, d = o_ref.shape
    half = d // 2
    gv = gv_ref[0, 0]
    group = 2 * SUBLANES

    @pl.when(gv < tr)
    def _():
        o_ref[...] = jnp.zeros_like(o_ref)

    def body(g, carry):
        r0 = pl.multiple_of(g * group, group)
        rows = [hp_ref[pl.ds(tok_ref[0, r0 + u], 1), :] for u in range(group)]
        hi, lo = _unpack_pair(jnp.concatenate(rows, axis=0))
        o_ref[pl.ds(r0, group), pl.ds(0, half)] = hi.astype(o_ref.dtype)
        o_ref[pl.ds(r0, group), pl.ds(half, half)] = lo.astype(o_ref.dtype)
        return carry

    lax.fori_loop(0, (gv + group - 1) // group, body, 0)


def _moe_gather(hp, tok, n_valid, tg, p_rows):
    m, half = hp.shape
    tr = _pick(tg, (GATHER_ROWS, 128, 64, 32, 16))
    nt = p_rows // tr
    per = tg // tr
    gidx = jnp.arange(nt, dtype=jnp.int32)
    gv = jnp.clip(n_valid[gidx // per] - (gidx % per) * tr, 0, tr).astype(jnp.int32)
    return _pcall(
        _moe_gather_kernel, name="moe_gather", grid=(nt,),
        in_specs=[pl.BlockSpec((None, 1, tr), lambda i: (i, 0, 0), memory_space=pltpu.SMEM),
                  pl.BlockSpec((None, 1, 1), lambda i: (i, 0, 0), memory_space=pltpu.SMEM),
                  pl.BlockSpec((m, half), lambda i: (0, 0), pipeline_mode=pl.Buffered(1))],
        args=(tok.reshape(nt, 1, tr), gv.reshape(nt, 1, 1), hp),
        out_specs=[pl.BlockSpec((tr, 2 * half), lambda i: (i, 0))],
        out_shape=[jax.ShapeDtypeStruct((p_rows, 2 * half), BF16)],
        sem=("arbitrary",))[0]


def _tile_changed(te_ref):
    ti = pl.program_id(1)
    return jnp.logical_or(ti == 0, te_ref[ti] != te_ref[jnp.maximum(ti - 1, 0)])


def _valid_row_chunks(nv, o_ref, chunk):
    tg = o_ref.shape[0]

    @pl.when(nv == tg)
    def _():
        _row_chunks(tg, chunk)

    @pl.when(nv < tg)
    def _():
        def maybe(rs):
            @pl.when(rs.start < nv)
            def _():
                chunk(rs)

            @pl.when(rs.start >= nv)
            def _():
                o_ref[rs, :] = jnp.zeros((rs.stop - rs.start, o_ref.shape[1]), o_ref.dtype)

        _row_chunks(tg, maybe)


def _gmm_swiglu_kernel(te_ref, nu_ref, nv_ref, x_ref, wg_ref, wu_ref, o_ref, wgb_ref, wub_ref):
    @pl.when(_tile_changed(te_ref))
    def _():
        _cast_w(wg_ref, wgb_ref)
        _cast_w(wu_ref, wub_ref)

    def chunk(rs):
        x = x_ref[rs, :]
        g = jnp.dot(x, wgb_ref[...], preferred_element_type=F32)
        u = jnp.dot(x, wub_ref[...], preferred_element_type=F32)
        o_ref[rs, :] = (jax.nn.silu(g) * u).astype(o_ref.dtype)

    _valid_row_chunks(nv_ref[pl.program_id(1)], o_ref, chunk)


def _gmm_down_kernel(te_ref, nu_ref, nv_ref, x_ref, w_ref, o_ref, wb_ref):
    @pl.when(_tile_changed(te_ref))
    def _():
        _cast_w(w_ref, wb_ref)

    hw = o_ref.shape[1]

    def chunk(rs):
        y = jnp.dot(x_ref[rs, :], wb_ref[...], preferred_element_type=F32)
        o_ref[rs, :] = _pack_pair(y[:, :hw], y[:, hw:])

    _valid_row_chunks(nv_ref[pl.program_id(1)], o_ref, chunk)


def _gmm_specs(jm, tg, k, tn, j0=0):
    xrow = lambda j, ti, te, nu, nv: (jnp.minimum(ti, nu[0] - 1), 0)
    wsp = pl.BlockSpec((None, None, k, tn), lambda j, ti, te, nu, nv: (jm, te[ti], 0, j0 + j))
    return pl.BlockSpec((tg, k), xrow), wsp, pl.BlockSpec((tg, tn), lambda j, ti, te, nu, nv: (ti, j0 + j))


def _gmm_swiglu(xg, wg, wu, jm, te, n_used, n_valid, tg, tns):
    p, k = xg.shape
    n = wg.shape[-1]
    assert sum(tn * cnt for tn, cnt in tns) == n
    out, col = None, 0
    for tn, cnt in tns:
        assert col % tn == 0
        xs, ws, os_ = _gmm_specs(jm, tg, k, tn, col // tn)
        out = _pcall(
            _gmm_swiglu_kernel, name="gmm_swiglu", grid=(cnt, p // tg), num_scalar_prefetch=3,
            in_specs=[None, None, None, xs, ws, ws], args=(te, n_used, n_valid, xg, wg, wu),
            out_specs=[os_], out_shape=[jax.ShapeDtypeStruct((p, n), BF16)],
            scratch=[pltpu.VMEM((k, tn), BF16), pltpu.VMEM((k, tn), BF16)],
            prev=(out,), sem=("arbitrary", "arbitrary"))[0]
        col += tn * cnt
    return out


def _col_tiling(n, widths):
    out, col = [], 0
    for wdt in widths:
        cnt = (n - col) // wdt
        if cnt:
            out.append((wdt, cnt))
            col += cnt * wdt
    assert col == n
    return tuple(out)


def _gmm_down(ug, wd, jm, te, n_used, n_valid, tg, tn):
    p, k = ug.shape
    n = wd.shape[-1]
    xs, ws, _ = _gmm_specs(jm, tg, k, tn)
    return _pcall(
        _gmm_down_kernel, name="gmm_down", grid=(n // tn, p // tg), num_scalar_prefetch=3,
        in_specs=[None, None, None, xs, ws], args=(te, n_used, n_valid, ug, wd),
        out_specs=[pl.BlockSpec((tg, tn // 2), lambda j, ti, te, nu, nv: (ti, j))],
        out_shape=[jax.ShapeDtypeStruct((p, n // 2), jnp.uint32)],
        scratch=[pltpu.VMEM((k, tn), BF16)],
        sem=("arbitrary", "arbitrary"))[0]


def _moe_combine_kernel(pos_ref, nxt_ref, x_ref, rw_ref, g_ref, yg_hbm, *rest, tn, np_tiles):
    outs, (buf, sem) = rest[:-2], rest[-2:]
    tc = x_ref.shape[0]
    i = pl.program_id(0)
    slot = lax.rem(i, 2)

    def gathers(idx_ref, s):
        return [_row_gather(yg_hbm, buf.at[s, k], lambda r, k=k: idx_ref[0, k * tc + r], sem.at[s])
                for k in range(TOP_K)]

    @pl.when(i == 0)
    def _():
        for start, _ in gathers(pos_ref, 0):
            start()

    @pl.when(i + 1 < pl.num_programs(0))
    def _():
        for start, _ in gathers(nxt_ref, 1 - slot):
            start()

    for _, wait in gathers(pos_ref, slot):
        wait()
    rows = buf[slot]

    def expand(p):
        hw = tn // 2
        parts = []
        for jj in range(p.shape[1] // hw):
            parts.extend(_unpack_pair(p[:, jj * hw:(jj + 1) * hw]))
        return jnp.concatenate(parts, axis=-1)

    rw = rw_ref[...]
    y = rw[:, TOP_K:TOP_K + 1] * expand(rows[0])
    for k in range(1, TOP_K):
        y = y + rw[:, TOP_K + k:TOP_K + k + 1] * expand(rows[k])
    xn = x_ref[...] + y
    if np_tiles is None:
        outs[0][...] = xn
    else:
        yn = _rms(xn, g_ref[...])

        @pl.when(i < np_tiles)
        def _():
            outs[0][...] = yn

        @pl.when(i >= np_tiles)
        def _():
            outs[1][...] = yn


def _moe_combine(x, rout, pos, ygp, tn, g3=None, mp=None):
    m, d = x.shape
    tc = _pick(m, (COMBINE_ROWS, 64, 32, 16, 8))
    nt = m // tc
    pos_t = pos.reshape(TOP_K, nt, tc).transpose(1, 0, 2).reshape(nt, 1, TOP_K * tc)
    if g3 is None:
        g3 = jnp.ones((1, 1, d), F32)
        np_tiles = None
        out_specs = [pl.BlockSpec((tc, d), lambda i: (i, 0))]
        out_shape = [jax.ShapeDtypeStruct((m, d), F32)]
    else:
        assert mp % tc == 0
        np_tiles = mp // tc
        out_specs = [pl.BlockSpec((tc, d), lambda i: (jnp.minimum(i, np_tiles - 1), 0)),
                     pl.BlockSpec((tc, d), lambda i: (jnp.maximum(i - np_tiles, 0), 0))]
        out_shape = [jax.ShapeDtypeStruct((mp, d), F32), jax.ShapeDtypeStruct((m - mp, d), F32)]
    return _pcall(
        functools.partial(_moe_combine_kernel, tn=tn, np_tiles=np_tiles), name="moe_combine", grid=(nt,),
        in_specs=[pl.BlockSpec((None, 1, TOP_K * tc), lambda i: (i, 0, 0), memory_space=pltpu.SMEM),
                  pl.BlockSpec((None, 1, TOP_K * tc), lambda i: (jnp.minimum(i + 1, nt - 1), 0, 0),
                               memory_space=pltpu.SMEM),
                  pl.BlockSpec((tc, d), lambda i: (i, 0)),
                  pl.BlockSpec((tc, LANES), lambda i: (i, 0)),
                  _vspec(0, d),
                  pl.BlockSpec(memory_space=pl.ANY)],
        args=(pos_t, pos_t, x, rout, g3, ygp),
        out_specs=out_specs, out_shape=out_shape,
        scratch=[pltpu.VMEM((2, TOP_K, tc, d // 2), jnp.uint32), pltpu.SemaphoreType.DMA((2,))],
        sem=("arbitrary",))


def kernel(x_prompt, x_sample, state_ret, state_lru, state_conv, norm_mix, w_in, ret_gn, w_ret_o, conv_w, conv_b, w_rgate, b_rgate, w_igate, b_igate, lru_lambda, w_lru_o, w_out, norm_ffn, ffn_w_gate, ffn_w_up, ffn_w_down, moe_router, moe_w_gate, moe_w_up, moe_w_down, norm_final):
    bp, tp, d = x_prompt.shape
    bs, ts, _ = x_sample.shape
    depth, _, nh, dk, dv = state_ret.shape
    width = state_lru.shape[-1]
    ncw = conv_w.shape[1]
    ne = moe_router.shape[-1]
    assert dk == dv and nh * dk == width == d and ts == SUBLANES and ncw - 1 <= SUBLANES
    mp, ms = bp * tp, bs * ts
    m = mp + ms
    g_col = 2 * nh * dk + nh * dv
    ux_col = g_col + nh * dv
    ga_col = ux_col + 2 * width

    tm = _pick(m, (1024, 512, 256, 128, 64, 32, 16, 8))
    tm_big = _pick(m, (1536, 1024, 512, 256, 128, 64, 32, 16, 8))
    tm_half = _pick(m, (512, 256, 128, 64, 32, 16, 8))
    tn_of = lambda n: _pick(n, (1024, 512, 256, 128))

    vec3 = lambda a: a.reshape(a.shape[0], 1, a.shape[-1])
    norm_mix3, norm_ffn3, ret_gn3 = vec3(norm_mix), vec3(norm_ffn), vec3(ret_gn)
    norm_final3 = norm_final.reshape(1, 1, d)
    lru_w = (conv_w, vec3(conv_b), w_rgate, vec3(b_rgate), w_igate, vec3(b_igate), vec3(lru_lambda))
    h0_rep = jnp.repeat(state_lru, ts, axis=1)
    buf_fr = jnp.pad(state_conv, ((0, 0), (0, 0), (ts - (ncw - 1), 0), (0, 0))).reshape(depth, ms, width)

    cos_p, sin_p = _rope_tables(tp, dk // 2, 0)
    cos_s, sin_s = _rope_tables(ts, dk // 2, PAST_LEN)
    cos = jnp.concatenate([jnp.tile(cos_p, (bp, 1)), jnp.tile(cos_s, (bs, 1))], axis=0)
    sin = jnp.concatenate([jnp.tile(sin_p, (bp, 1)), jnp.tile(sin_s, (bs, 1))], axis=0)
    tn_in = _pick(nh * dk, (1024, 512, 256))

    x = (x_prompt.reshape(mp, d), x_sample.reshape(ms, d))
    if mp % tm_half or ms % tm_half:
        x = jnp.concatenate(x, axis=0)
    ret_p = ret_s = y_p = y_s = None
    lrus_p, lrus_s, convs_p, convs_s = [], [], [], []
    for l in range(depth):
        if isinstance(x, tuple):
            h = _rmsnorm(x[0], norm_mix3, l, BF16, out_rows=m)
            h = _rmsnorm(x[1], norm_mix3, l, BF16, out_row0=mp, out_rows=m, prev=h)
        else:
            h = _rmsnorm(x, norm_mix3, l, BF16)
        proj = _mm_inproj(h, w_in, (l,), cos, sin, 2 * nh * dk, dk, (g_col, ux_col),
                          (ux_col + width, ga_col), tm_big, tn_in)

        ya, ret_p = _ret_prompt(proj, ret_gn3, l, depth, bp, tp, nh, dk, dv, ret_p)
        if _fused_seqs_per_step(width, tp, bp, bs, ts) is None:
            ya, ret_s = _ret_sample(proj, ret_gn3, state_ret, l, mp, bs, ts, nh, dk, dv, ya, ret_s)
            yb, hl_p, ul_p = _lru_prompt(proj, ux_col, l, lru_w, bp, tp)
        else:
            yb, hl_p, ul_p, ya, ret_s = _lru_prompt_ret_sample(
                proj, ux_col, l, lru_w, bp, tp, ret_gn3, state_ret, mp, bs, ts, nh, dk, dv, ya, ret_s)
        yb, hl_s, us_s = _lru_sample(proj, ux_col, l, lru_w, mp, h0_rep, buf_fr, yb)
        lrus_p.append(hl_p[:, -1])
        lrus_s.append(hl_s.reshape(bs, ts, width)[:, -1])
        convs_p.append(ul_p[:, SUBLANES - (ncw - 1):])
        convs_s.append(us_s.reshape(bs, ts, width)[:, ts - (ncw - 1):])

        z = _mm_merge(ya, yb, w_ret_o, w_lru_o, (l,), proj, ga_col, ga_col + d, tm, tn_of(d) // 2)
        x = _mm_resid(z, w_out, (l,), x, tm_half if isinstance(x, tuple) else tm, tn_of(d))
        j = l // 2
        if l % 2 == 0:
            h2 = _rmsnorm(x, norm_ffn3, l, BF16)
            ff = ffn_w_gate.shape[-1]
            u = _mm_swiglu(h2, ffn_w_gate, ffn_w_up, (j,), tm_big, _pick(ff, (512, 256, 128)))
            x = _mm_resid(u, ffn_w_down, (j,), x, tm_half, _pick(d, (512, 256, 128)))
        else:
            fe = moe_w_gate.shape[-1]
            tg = _pick(TOP_K * m, (MOE_TILE, 256, 128, 64, 32, 16))
            tn_d = _pick(d, (1024, 512, 256))
            rout, hp = _router(x, norm_ffn3, l, moe_router[j])
            pos, tok, te, n_used, n_valid, p_rows = _dispatch_tables(rout, ne, tg)
            xg = _moe_gather(hp, tok, n_valid, tg, p_rows)
            ug = _gmm_swiglu(xg, moe_w_gate, moe_w_up, j, te, n_used, n_valid, tg,
                             _col_tiling(fe, (1024, 512, 256, 128)))
            ygp = _gmm_down(ug, moe_w_down, j, te, n_used, n_valid, tg, tn_d)
            if l == depth - 1:
                y_p, y_s = _moe_combine(x, rout, pos, ygp, tn_d, norm_final3, mp)
            else:
                x = _moe_combine(x, rout, pos, ygp, tn_d)[0]
    if y_p is None:
        y_p = _rmsnorm(x, norm_final3, 0, F32, 0, mp)
        y_s = _rmsnorm(x, norm_final3, 0, F32, mp, ms)
    return (y_p.reshape(bp, tp, d), y_s.reshape(bs, ts, d),
            ret_p, jnp.stack(lrus_p), jnp.stack(convs_p),
            ret_s, jnp.stack(lrus_s), jnp.stack(convs_s))
```

```python
import functools

import jax
import jax.numpy as jnp
from jax import lax
from jax.experimental import pallas as pl
from jax.experimental.pallas import tpu as pltpu

F32 = jnp.float32
BF16 = jnp.bfloat16

ROPE_BASE = 10000.0
LRU_C = 8.0
EPS = 1e-6
RET_CHUNK = 128
PAST_LEN = 16384
TOP_K = 2
SUBLANES = 8
LANES = 128
VMEM_LIMIT = 56 * 1024 * 1024
MOE_TILE = 1024
GATHER_ROWS = 1024
COMBINE_ROWS = 512


def _pick(n, prefs):
    for p in prefs:
        if n % p == 0:
            return p
    return n


def _params(*sem):
    return pltpu.CompilerParams(dimension_semantics=sem, vmem_limit_bytes=VMEM_LIMIT)


def _pcall(body, *, name, grid, in_specs, args, out_specs, out_shape, sem, scratch=(), prev=(),
           num_scalar_prefetch=0):
    n_in = len(args)
    prev = tuple(prev) + (None,) * (len(out_shape) - len(prev))
    extra = [(oi, p) for oi, p in enumerate(prev) if p is not None]
    aliases = {n_in + e: oi for e, (oi, _) in enumerate(extra)}
    if extra:
        inner = body

        def body(*refs):
            return inner(*refs[:n_in], *refs[n_in + len(extra):])

    in_specs = list(in_specs) + [pl.BlockSpec(memory_space=pl.ANY)] * len(extra)
    if num_scalar_prefetch:
        grid_spec = pltpu.PrefetchScalarGridSpec(
            num_scalar_prefetch=num_scalar_prefetch, grid=grid, in_specs=in_specs[num_scalar_prefetch:],
            out_specs=tuple(out_specs), scratch_shapes=list(scratch))
        call = pl.pallas_call(body, grid_spec=grid_spec, out_shape=tuple(out_shape),
                              input_output_aliases=aliases, compiler_params=_params(*sem), name=name)
    else:
        call = pl.pallas_call(body, grid=grid, in_specs=in_specs, out_specs=tuple(out_specs),
                              out_shape=tuple(out_shape), scratch_shapes=list(scratch),
                              input_output_aliases=aliases, compiler_params=_params(*sem), name=name)
    return call(*args, *[p for _, p in extra])


def _wspec(pre, k, tn):
    return pl.BlockSpec((None,) * len(pre) + (k, tn), lambda j, i: pre + (0, j))


def _vspec(l, n):
    return pl.BlockSpec((None, 1, n), lambda *_: (l, 0, 0))


def _rms(x, g):
    return (x * lax.rsqrt(jnp.mean(x * x, axis=-1, keepdims=True) + EPS)) * g


def _rmsnorm_kernel(x_ref, g_ref, o_ref):
    o_ref[...] = _rms(x_ref[...], g_ref[...]).astype(o_ref.dtype)


def _rmsnorm(x, g3, l, out_dtype, row0=0, rows=None, out_row0=0, out_rows=None, prev=None):
    d = x.shape[1]
    rows = x.shape[0] if rows is None else rows
    out_rows = rows if out_rows is None else out_rows
    tm = _pick(rows, (512, 256, 128, 64, 32, 16, 8))
    assert row0 % tm == 0 and out_row0 % tm == 0
    rb0, ob0 = row0 // tm, out_row0 // tm
    return _pcall(
        _rmsnorm_kernel, name="rmsnorm", grid=(rows // tm,),
        in_specs=[pl.BlockSpec((tm, d), lambda i: (rb0 + i, 0)), _vspec(l, d)],
        args=(x, g3),
        out_specs=[pl.BlockSpec((tm, d), lambda i: (ob0 + i, 0))],
        out_shape=[jax.ShapeDtypeStruct((out_rows, d), out_dtype)],
        prev=(prev,), sem=("arbitrary",))[0]


def _cast_w(w_ref, wb_ref):
    k = w_ref.shape[0]
    ck = _pick(k, (256, 128, 64, 32, 16))

    def body(c, carry):
        r = pl.multiple_of(c * ck, ck)
        wb_ref[pl.ds(r, ck), :] = w_ref[pl.ds(r, ck), :].astype(BF16)
        return carry

    lax.fori_loop(0, k // ck, body, 0)


def _mm_inproj_kernel(x_ref, w_ref, cos_ref, sin_ref, o_ref, wb_ref, *, nrot, dk, silu_tiles, gelu_tiles):
    j = pl.program_id(0)

    @pl.when(pl.program_id(1) == 0)
    def _():
        _cast_w(w_ref, wb_ref)

    tm, tn = o_ref.shape
    nchunk = 4 if tm % (4 * 16) == 0 else 1

    def row_chunks(epilogue):
        rc = tm // nchunk
        for c in range(nchunk):
            rs = slice(c * rc, (c + 1) * rc)
            epilogue(jnp.dot(x_ref[rs, :], wb_ref[...], preferred_element_type=F32), rs)

    def store(fn):
        def epilogue(acc, rs):
            o_ref[rs, :] = fn(acc).astype(o_ref.dtype)
        return epilogue

    in_tiles = lambda t: jnp.logical_and(j >= t[0], j < t[1])
    is_silu, is_gelu = in_tiles(silu_tiles), in_tiles(gelu_tiles)

    @pl.when(jnp.logical_and(j >= nrot, jnp.logical_not(jnp.logical_or(is_silu, is_gelu))))
    def _():
        row_chunks(store(lambda acc: acc))

    @pl.when(is_silu)
    def _():
        row_chunks(store(jax.nn.silu))

    @pl.when(is_gelu)
    def _():
        row_chunks(store(jax.nn.gelu))

    @pl.when(j < nrot)
    def _():
        scale = jnp.where(j >= nrot // 2, F32(dk ** -0.5), F32(1.0))
        half = dk // 2

        def rotary(acc, rs):
            cos = cos_ref[rs, :] * scale
            sin = sin_ref[rs, :] * scale
            for h in range(tn // dk):
                x1 = acc[:, h * dk:h * dk + half]
                x2 = acc[:, h * dk + half:(h + 1) * dk]
                o_ref[rs, h * dk:h * dk + half] = (x1 * cos - x2 * sin).astype(o_ref.dtype)
                o_ref[rs, h * dk + half:(h + 1) * dk] = (x1 * sin + x2 * cos).astype(o_ref.dtype)

        row_chunks(rotary)


def _mm_inproj(x, w, pre, cos, sin, qk_cols, dk, silu_cols, gelu_cols, tm, tn):
    m, k = x.shape
    n = w.shape[-1]
    half = cos.shape[1]
    assert qk_cols % (2 * tn) == 0 and tn % dk == 0
    assert all(c % tn == 0 for c in silu_cols + gelu_cols)
    tiles = lambda cols: (cols[0] // tn, cols[1] // tn)
    return _pcall(
        functools.partial(_mm_inproj_kernel, nrot=qk_cols // tn, dk=dk, silu_tiles=tiles(silu_cols),
                          gelu_tiles=tiles(gelu_cols)), name="mm_inproj",
        grid=(n // tn, m // tm),
        in_specs=[pl.BlockSpec((tm, k), lambda j, i: (i, 0)), _wspec(pre, k, tn),
                  pl.BlockSpec((tm, half), lambda j, i: (i, 0)),
                  pl.BlockSpec((tm, half), lambda j, i: (i, 0))],
        args=(x, w, cos, sin),
        out_specs=[pl.BlockSpec((tm, tn), lambda j, i: (i, j))],
        out_shape=[jax.ShapeDtypeStruct((m, n), BF16)],
        scratch=[pltpu.VMEM((k, tn), BF16)],
        sem=("arbitrary", "arbitrary"))[0]


def _mm_resid_kernel(x_ref, w_ref, r_ref, o_ref, wb_ref):
    @pl.when(pl.program_id(1) == 0)
    def _():
        _cast_w(w_ref, wb_ref)

    o_ref[...] = r_ref[...] + jnp.dot(x_ref[...], wb_ref[...], preferred_element_type=F32)


def _mm_resid2_kernel(x_ref, w_ref, ra_ref, rb_ref, o_ref, wb_ref, *, na):
    @pl.when(pl.program_id(1) == 0)
    def _():
        _cast_w(w_ref, wb_ref)

    y = jnp.dot(x_ref[...], wb_ref[...], preferred_element_type=F32)
    i = pl.program_id(1)

    @pl.when(i < na)
    def _():
        o_ref[...] = ra_ref[...] + y

    @pl.when(i >= na)
    def _():
        o_ref[...] = rb_ref[...] + y


def _mm_resid(x, w, pre, r, tm, tn):
    m, k = x.shape
    n = w.shape[-1]
    if isinstance(r, tuple):
        ra, rb = r
        assert ra.shape[0] % tm == 0 and rb.shape[0] % tm == 0 and ra.shape[0] + rb.shape[0] == m
        na = ra.shape[0] // tm
        body = functools.partial(_mm_resid2_kernel, na=na)
        r_specs = [pl.BlockSpec((tm, tn), lambda j, i: (jnp.minimum(i, na - 1), j)),
                   pl.BlockSpec((tm, tn), lambda j, i: (jnp.maximum(i - na, 0), j))]
        r_args = (ra, rb)
    else:
        body, r_specs, r_args = _mm_resid_kernel, [pl.BlockSpec((tm, tn), lambda j, i: (i, j))], (r,)
    return _pcall(
        body, name="mm_resid", grid=(n // tn, m // tm),
        in_specs=[pl.BlockSpec((tm, k), lambda j, i: (i, 0)), _wspec(pre, k, tn)] + r_specs,
        args=(x, w) + r_args,
        out_specs=[pl.BlockSpec((tm, tn), lambda j, i: (i, j))],
        out_shape=[jax.ShapeDtypeStruct((m, n), F32)],
        scratch=[pltpu.VMEM((k, tn), BF16)],
        sem=("arbitrary", "arbitrary"))[0]


def _row_chunks(rows, fn, nchunk=4):
    n = nchunk if rows % (nchunk * 16) == 0 else 1
    rc = rows // n
    for c in range(n):
        fn(slice(c * rc, (c + 1) * rc))


def _swiglu_rows(x_ref, wgb_ref, wub_ref, o_ref):
    def chunk(rs):
        x = x_ref[rs, :]
        g = jnp.dot(x, wgb_ref[...], preferred_element_type=F32)
        u = jnp.dot(x, wub_ref[...], preferred_element_type=F32)
        o_ref[rs, :] = (jax.nn.silu(g) * u).astype(o_ref.dtype)

    _row_chunks(o_ref.shape[0], chunk)


def _mm_swiglu_kernel(x_ref, wg_ref, wu_ref, o_ref, wgb_ref, wub_ref):
    @pl.when(pl.program_id(1) == 0)
    def _():
        _cast_w(wg_ref, wgb_ref)
        _cast_w(wu_ref, wub_ref)

    _swiglu_rows(x_ref, wgb_ref, wub_ref, o_ref)


def _mm_swiglu(x, wg, wu, pre, tm, tn):
    m, k = x.shape
    n = wg.shape[-1]
    return _pcall(
        _mm_swiglu_kernel, name="mm_swiglu", grid=(n // tn, m // tm),
        in_specs=[pl.BlockSpec((tm, k), lambda j, i: (i, 0)), _wspec(pre, k, tn), _wspec(pre, k, tn)],
        args=(x, wg, wu),
        out_specs=[pl.BlockSpec((tm, tn), lambda j, i: (i, j))],
        out_shape=[jax.ShapeDtypeStruct((m, n), BF16)],
        scratch=[pltpu.VMEM((k, tn), BF16), pltpu.VMEM((k, tn), BF16)],
        sem=("arbitrary", "arbitrary"))[0]


def _mm_merge_kernel(a_ref, b_ref, wa_ref, wb_ref, ga_ref, gb_ref, o_ref, wab_ref, wbb_ref):
    @pl.when(pl.program_id(1) == 0)
    def _():
        _cast_w(wa_ref, wab_ref)
        _cast_w(wb_ref, wbb_ref)

    def chunk(rs):
        ya = jnp.dot(a_ref[rs, :], wab_ref[...], preferred_element_type=F32)
        yb = jnp.dot(b_ref[rs, :], wbb_ref[...], preferred_element_type=F32)
        ga = ga_ref[rs, :].astype(F32)
        gb = gb_ref[rs, :].astype(F32)
        o_ref[rs, :] = (jax.nn.sigmoid(ga) * ya + jax.nn.sigmoid(gb) * yb).astype(o_ref.dtype)

    _row_chunks(o_ref.shape[0], chunk)


def _mm_merge(a, b, wa, wb, pre, proj, ga_col, gb_col, tm, tn):
    m, k = a.shape
    n = wa.shape[-1]
    ga_blk, gb_blk = ga_col // tn, gb_col // tn
    return _pcall(
        _mm_merge_kernel, name="mm_merge", grid=(n // tn, m // tm),
        in_specs=[pl.BlockSpec((tm, k), lambda j, i: (i, 0)),
                  pl.BlockSpec((tm, k), lambda j, i: (i, 0)),
                  _wspec(pre, k, tn), _wspec(pre, k, tn),
                  pl.BlockSpec((tm, tn), lambda j, i: (i, ga_blk + j)),
                  pl.BlockSpec((tm, tn), lambda j, i: (i, gb_blk + j))],
        args=(a, b, wa, wb, proj, proj),
        out_specs=[pl.BlockSpec((tm, tn), lambda j, i: (i, j))],
        out_shape=[jax.ShapeDtypeStruct((m, n), BF16)],
        scratch=[pltpu.VMEM((k, tn), BF16), pltpu.VMEM((k, tn), BF16)],
        sem=("arbitrary", "arbitrary"))[0]


def _rope_kernel(inv_ref, cos_ref, sin_ref, *, pos0):
    t, half = cos_ref.shape
    pos = lax.broadcasted_iota(jnp.int32, (t, half), 0).astype(F32) + F32(pos0)
    ang = pos * inv_ref[...]
    cos_ref[...] = jnp.cos(ang)
    sin_ref[...] = jnp.sin(ang)


def _rope_tables(t, half, pos0):
    inv = ROPE_BASE ** (-jnp.arange(half, dtype=F32) / half)
    return pl.pallas_call(
        functools.partial(_rope_kernel, pos0=pos0),
        out_shape=(jax.ShapeDtypeStruct((t, half), F32), jax.ShapeDtypeStruct((t, half), F32)),
        name="rope_tables",
    )(inv.reshape(1, half))


def _decay_tables(c, h, dk, dv):
    log_g = jnp.log1p(-jnp.exp2(-5.0 - jnp.arange(h, dtype=F32)))
    idx = jnp.arange(c)
    rel = idx[:, None] - idx[None, :]
    dmask = jnp.where(rel[None] >= 0,
                      jnp.exp(jnp.maximum(rel, 0)[None].astype(F32) * log_g[:, None, None]), 0.0)
    xi = jnp.exp((idx + 1).astype(F32)[None, :] * log_g[:, None])
    zeta = jnp.exp((c - 1 - idx).astype(F32)[None, :] * log_g[:, None])
    g_c = jnp.exp(c * log_g)
    xi_t = jnp.broadcast_to(xi[:, :, None], (h, c, dv))
    zeta_t = jnp.broadcast_to(zeta[:, :, None], (h, c, dk))
    return dmask, xi_t, zeta_t, g_c


def _ret_head(qb, kb, v, g, dm, xi, zt, gn, s, gc):
    scores = lax.dot_general(qb, kb, (((1,), (1,)), ((), ())), preferred_element_type=F32)
    intra = jnp.dot((scores * dm).astype(BF16), v, preferred_element_type=F32)
    cross = jnp.dot(qb, s.astype(BF16), preferred_element_type=F32) * xi
    kz = (kb.astype(F32) * zt).astype(BF16)
    upd = lax.dot_general(kz, v, (((0,), (0,)), ((), ())), preferred_element_type=F32)
    s_new = gc * s + upd
    o = intra + cross
    mu = jnp.mean(o, axis=-1, keepdims=True)
    d = o - mu
    var = jnp.mean(d * d, axis=-1, keepdims=True)
    y = d * lax.rsqrt(var + EPS) * gn
    return (g * y).astype(BF16), s_new


def _ret_prompt_kernel(gc_ref, q_ref, k_ref, v_ref, g_ref, dm_ref, xi_ref, zt_ref,
                       gn_ref, o_ref, s_ref, *, nh, dk, dv):
    @pl.when(pl.program_id(1) == 0)
    def _():
        s_ref[...] = jnp.zeros_like(s_ref)

    c = dm_ref.shape[1]
    for h in range(nh):
        s = s_ref[h]
        for ci in range(q_ref.shape[0] // c):
            rs = slice(ci * c, (ci + 1) * c)
            out, s = _ret_head(
                q_ref[rs, h * dk:(h + 1) * dk], k_ref[rs, h * dk:(h + 1) * dk],
                v_ref[rs, h * dv:(h + 1) * dv], g_ref[rs, h * dv:(h + 1) * dv].astype(F32),
                dm_ref[h], xi_ref[h], zt_ref[h], gn_ref[:, h * dv:(h + 1) * dv], s, gc_ref[h])
            o_ref[rs, h * dv:(h + 1) * dv] = out
        s_ref[h] = s


def _ret_prompt(proj, gn3, l, depth, b, t, nh, dk, dv, s_prev):
    cl = RET_CHUNK if t % RET_CHUNK == 0 else t
    cps = _pick(t // cl, (8, 4, 2, 1))
    c = cl * cps
    nc = t // c
    w = nh * dk
    m = proj.shape[0]
    dmask, xi_t, zeta_t, g_c = _decay_tables(cl, nh, dk, dv)
    row = lambda bi, ci: bi * nc + ci
    full3 = lambda bi, ci: (0, 0, 0)
    return _pcall(
        functools.partial(_ret_prompt_kernel, nh=nh, dk=dk, dv=dv), name="ret_prompt", grid=(b, nc),
        in_specs=[pl.BlockSpec(memory_space=pltpu.SMEM),
                  pl.BlockSpec((c, w), lambda bi, ci: (row(bi, ci), 0)),
                  pl.BlockSpec((c, w), lambda bi, ci: (row(bi, ci), 1)),
                  pl.BlockSpec((c, w), lambda bi, ci: (row(bi, ci), 2)),
                  pl.BlockSpec((c, w), lambda bi, ci: (row(bi, ci), 3)),
                  pl.BlockSpec((nh, cl, cl), full3),
                  pl.BlockSpec((nh, cl, dv), full3),
                  pl.BlockSpec((nh, cl, dk), full3),
                  _vspec(l, nh * dv)],
        args=(g_c, proj, proj, proj, proj, dmask, xi_t, zeta_t, gn3),
        out_specs=[pl.BlockSpec((c, nh * dv), lambda bi, ci: (row(bi, ci), 0)),
                   pl.BlockSpec((None, None, nh, dk, dv), lambda bi, ci: (l, bi, 0, 0, 0))],
        out_shape=[jax.ShapeDtypeStruct((m, nh * dv), BF16),
                   jax.ShapeDtypeStruct((depth, b, nh, dk, dv), F32)],
        prev=(None, s_prev),
        sem=("arbitrary", "arbitrary"))


def _ret_sample_kernel(gc_ref, q_ref, k_ref, v_ref, g_ref, dm_ref, xi_ref, zt_ref,
                       gn_ref, s0_ref, o_ref, s_ref, *, nh, dk, dv, bb, ts):
    q = q_ref[...].astype(F32)
    k = k_ref[...].astype(F32)
    v = v_ref[...].astype(F32)
    g = g_ref[...].astype(F32)
    seqs = []
    for i in range(bb):
        r0, r1 = i * ts, (i + 1) * ts
        outs = []
        for h in range(nh):
            out, s_new = _ret_head(
                q[r0:r1, h * dk:(h + 1) * dk].astype(BF16), k[r0:r1, h * dk:(h + 1) * dk].astype(BF16),
                v[r0:r1, h * dv:(h + 1) * dv].astype(BF16), g[r0:r1, h * dv:(h + 1) * dv],
                dm_ref[h], xi_ref[h], zt_ref[h], gn_ref[:, h * dv:(h + 1) * dv],
                s0_ref[i, h], gc_ref[h])
            s_ref[i, h] = s_new
            outs.append(out.astype(F32))
        seqs.append(jnp.concatenate(outs, axis=-1))
    o_ref[...] = jnp.concatenate(seqs, axis=0).astype(o_ref.dtype)


def _ret_sample(proj, gn3, s0_all, l, row0, bs, ts, nh, dk, dv, ya_prev, s_prev):
    bb = _pick(bs, (4, 2, 1))
    rows = bb * ts
    w = nh * dk
    depth = s0_all.shape[0]
    dmask, xi_t, zeta_t, g_c = _decay_tables(ts, nh, dk, dv)
    assert row0 % rows == 0
    rb0 = row0 // rows
    full3 = lambda i: (0, 0, 0)
    return _pcall(
        functools.partial(_ret_sample_kernel, nh=nh, dk=dk, dv=dv, bb=bb, ts=ts), name="ret_sample",
        grid=(bs // bb,),
        in_specs=[pl.BlockSpec(memory_space=pltpu.SMEM),
                  pl.BlockSpec((rows, w), lambda i: (rb0 + i, 0)),
                  pl.BlockSpec((rows, w), lambda i: (rb0 + i, 1)),
                  pl.BlockSpec((rows, w), lambda i: (rb0 + i, 2)),
                  pl.BlockSpec((rows, w), lambda i: (rb0 + i, 3)),
                  pl.BlockSpec((nh, ts, ts), full3),
                  pl.BlockSpec((nh, ts, dv), full3),
                  pl.BlockSpec((nh, ts, dk), full3),
                  _vspec(l, nh * dv),
                  pl.BlockSpec((None, bb, nh, dk, dv), lambda i: (l, i, 0, 0, 0))],
        args=(g_c, proj, proj, proj, proj, dmask, xi_t, zeta_t, gn3, s0_all),
        out_specs=[pl.BlockSpec((rows, nh * dv), lambda i: (rb0 + i, 0)),
                   pl.BlockSpec((None, bb, nh, dk, dv), lambda i: (l, i, 0, 0, 0))],
        out_shape=[jax.ShapeDtypeStruct(ya_prev.shape, BF16),
                   jax.ShapeDtypeStruct((depth, bs, nh, dk, dv), F32)],
        prev=(ya_prev, s_prev),
        sem=("arbitrary",))


def _lru_gates(uc, wr_ref, br_ref, wi_ref, bi_ref, lam_ref):
    nb, lb, _ = wr_ref.shape
    ucb = uc.astype(BF16)
    rl, il = [], []
    for n in range(nb):
        xb = ucb[:, n * lb:(n + 1) * lb]
        rl.append(jnp.dot(xb, wr_ref[n].astype(BF16), preferred_element_type=F32))
        il.append(jnp.dot(xb, wi_ref[n].astype(BF16), preferred_element_type=F32))
    r = jax.nn.sigmoid(jnp.concatenate(rl, axis=-1) + br_ref[...])
    i = jax.nn.sigmoid(jnp.concatenate(il, axis=-1) + bi_ref[...])
    log_a = -LRU_C * r * jax.nn.softplus(-lam_ref[...])
    a = jnp.exp(log_a)
    z = -jnp.tanh(log_a) * (a * a + 1.0)
    bx = jnp.where(z == 0.0, 0.0, z * lax.rsqrt(z)) * (i * uc)
    return a, bx


def _blocks(x):
    return [x[lo:lo + SUBLANES, :] for lo in range(0, x.shape[0], SUBLANES)]


def _conv_blocks(u_blocks, prev_blocks, cw_ref, cb_ref, rows):
    ncw = cw_ref.shape[0]
    rolled = {}

    def rot(x, s):
        key = (id(x), s)
        if key not in rolled:
            rolled[key] = pltpu.roll(x, s, axis=0)
        return rolled[key]

    out = []
    for ub, pb in zip(u_blocks, prev_blocks):
        uc = cb_ref[...]
        for j in range(ncw):
            s = ncw - 1 - j
            term = ub if s == 0 else jnp.where(rows >= s, rot(ub, s), rot(pb, s))
            uc = uc + term * cw_ref[j:j + 1, :]
        out.append(uc)
    return jnp.concatenate(out, axis=0)


def _scan_block(a, bx, rows):
    d = 1
    while d < SUBLANES:
        m = rows >= d
        bx = bx + a * jnp.where(m, pltpu.roll(bx, d, axis=0), 0.0)
        a = a * jnp.where(m, pltpu.roll(a, d, axis=0), 1.0)
        d *= 2
    return a, bx


def _lru_prompt_kernel(ux_ref, uy_ref, cw_ref, cb_ref, wr_ref, br_ref, wi_ref, bi_ref, lam_ref,
                       o_ref, hl_ref, ul_ref, hc_ref):
    _lru_prompt_init(ul_ref, hc_ref)
    _lru_prompt_main(ux_ref, uy_ref, cw_ref, cb_ref, wr_ref, br_ref, wi_ref, bi_ref, lam_ref,
                     o_ref, hl_ref, ul_ref, hc_ref)


def _lru_prompt_init(ul_ref, hc_ref):
    @pl.when(pl.program_id(2) == 0)
    def _():
        hc_ref[...] = jnp.zeros_like(hc_ref)
        ul_ref[...] = jnp.zeros_like(ul_ref)


def _lru_prompt_main(ux_ref, uy_ref, cw_ref, cb_ref, wr_ref, br_ref, wi_ref, bi_ref, lam_ref,
                     o_ref, hl_ref, ul_ref, hc_ref):
    r, cw = ux_ref.shape
    rows = lax.broadcasted_iota(jnp.int32, (SUBLANES, cw), 0)
    ub = _blocks(ux_ref[...].astype(F32))
    uc = _conv_blocks(ub, [ul_ref[...]] + ub[:-1], cw_ref, cb_ref, rows)
    ul_ref[...] = ub[-1]
    a, bx = _lru_gates(uc, wr_ref, br_ref, wi_ref, bi_ref, lam_ref)
    carry = hc_ref[...]
    hs = []
    for ab, bb in zip(_blocks(a), _blocks(bx)):
        ab, bb = _scan_block(ab, bb, rows)
        hb = ab * carry + bb
        hs.append(hb)
        carry = jnp.broadcast_to(hb[SUBLANES - 1:SUBLANES, :], (SUBLANES, cw))
    hc_ref[...] = carry
    hl_ref[...] = hs[-1]
    h = jnp.concatenate(hs, axis=0)
    o_ref[...] = (h * uy_ref[...].astype(F32)).astype(o_ref.dtype)


def _lru_sample_kernel(ux_ref, uy_ref, cw_ref, cb_ref, wr_ref, br_ref, wi_ref, bi_ref, lam_ref,
                       h0_ref, buf_ref, o_ref, hl_ref, us_ref):
    r, cw = ux_ref.shape
    rows = lax.broadcasted_iota(jnp.int32, (SUBLANES, cw), 0)
    u = ux_ref[...].astype(F32)
    us_ref[...] = u
    uc = _conv_blocks(_blocks(u), _blocks(buf_ref[...]), cw_ref, cb_ref, rows)
    a, bx = _lru_gates(uc, wr_ref, br_ref, wi_ref, bi_ref, lam_ref)
    hs = []
    for ab, bb, h0 in zip(_blocks(a), _blocks(bx), _blocks(h0_ref[...])):
        ab, bb = _scan_block(ab, bb, rows)
        hs.append(ab * h0 + bb)
    h = jnp.concatenate(hs, axis=0)
    hl_ref[...] = h
    o_ref[...] = (h * uy_ref[...].astype(F32)).astype(o_ref.dtype)


def _lru_specs(l, r, cw, ncw, lb, ux_blk, uy_blk, row_of, cb_of):
    nbc = cw // lb
    vec = pl.BlockSpec((None, 1, cw), lambda *g: (l, 0, cb_of(*g)))
    gate = pl.BlockSpec((None, nbc, lb, lb), lambda *g: (l, cb_of(*g), 0, 0))
    return [pl.BlockSpec((r, cw), lambda *g: (row_of(*g), ux_blk + cb_of(*g))),
            pl.BlockSpec((r, cw), lambda *g: (row_of(*g), uy_blk + cb_of(*g))),
            pl.BlockSpec((None, ncw, cw), lambda *g: (l, 0, cb_of(*g))),
            vec, gate, vec, gate, vec, vec]


def _lru_prompt(proj, ux_col, l, lru_w, b, t):
    conv_w, conv_b3, w_r, b_r3, w_i, b_i3, lam3 = lru_w
    width = lam3.shape[-1]
    lb = w_r.shape[-1]
    ncw = conv_w.shape[1]
    m = proj.shape[0]
    cw = _pick(width, (1024, 512, 256, 128))
    r = _pick(t, (256, 128, 64, 32, 16, 8))
    nt = t // r
    specs = _lru_specs(l, r, cw, ncw, lb, ux_col // cw, (ux_col + width) // cw,
                       lambda bi, cb, ti: bi * nt + ti, lambda bi, cb, ti: cb)
    last8 = pl.BlockSpec((None, SUBLANES, cw), lambda bi, cb, ti: (bi, 0, cb))
    return _pcall(
        _lru_prompt_kernel, name="lru_prompt", grid=(b, width // cw, nt),
        in_specs=specs,
        args=(proj, proj, conv_w, conv_b3, w_r, b_r3, w_i, b_i3, lam3),
        out_specs=[pl.BlockSpec((r, cw), lambda bi, cb, ti: (bi * nt + ti, cb)), last8, last8],
        out_shape=[jax.ShapeDtypeStruct((m, width), BF16),
                   jax.ShapeDtypeStruct((b, SUBLANES, width), F32),
                   jax.ShapeDtypeStruct((b, SUBLANES, width), F32)],
        scratch=[pltpu.VMEM((SUBLANES, cw), F32)],
        sem=("arbitrary", "arbitrary", "arbitrary"))


def _lru_ret_kernel(*refs, nh, dk, dv, bb, ts):
    lru_in, ret_in = refs[:9], refs[9:19]
    yb_ref, hl_ref, ul_ref, ya_ref, s_ref, hc_ref = refs[19:]
    _lru_prompt_init(ul_ref, hc_ref)
    _ret_sample_kernel(*ret_in, ya_ref, s_ref, nh=nh, dk=dk, dv=dv, bb=bb, ts=ts)
    _lru_prompt_main(*lru_in, yb_ref, hl_ref, ul_ref, hc_ref)


def _fused_seqs_per_step(width, t, b, bs, ts):
    cw = _pick(width, (1024, 512, 256, 128))
    r = _pick(t, (256, 128, 64, 32, 16, 8))
    steps = b * (width // cw) * (t // r)
    if bs % steps:
        return None
    bb = bs // steps
    return bb if bb <= 4 and (bb * ts) % 16 == 0 else None


def _lru_prompt_ret_sample(proj, ux_col, l, lru_w, b, t, gn3, s0_all, row0, bs, ts, nh, dk, dv,
                           ya_prev, s_prev):
    conv_w, conv_b3, w_r, b_r3, w_i, b_i3, lam3 = lru_w
    width = lam3.shape[-1]
    lb = w_r.shape[-1]
    ncw = conv_w.shape[1]
    m = proj.shape[0]
    depth = s0_all.shape[0]
    cw = _pick(width, (1024, 512, 256, 128))
    r = _pick(t, (256, 128, 64, 32, 16, 8))
    nt, ncb = t // r, width // cw
    bb = _fused_seqs_per_step(width, t, b, bs, ts)
    rows = bb * ts
    w = nh * dk
    assert row0 % rows == 0
    rb0 = row0 // rows
    lin = lambda bi, cb, ti: (bi * ncb + cb) * nt + ti
    dmask, xi_t, zeta_t, g_c = _decay_tables(ts, nh, dk, dv)
    lru_specs = _lru_specs(l, r, cw, ncw, lb, ux_col // cw, (ux_col + width) // cw,
                           lambda bi, cb, ti: bi * nt + ti, lambda bi, cb, ti: cb)
    full3 = lambda *g: (0, 0, 0)
    qkvg = [pl.BlockSpec((rows, w), lambda *g, c=c: (rb0 + lin(*g), c)) for c in range(4)]
    state = pl.BlockSpec((None, bb, nh, dk, dv), lambda *g: (l, lin(*g), 0, 0, 0))
    ret_specs = ([pl.BlockSpec(memory_space=pltpu.SMEM)] + qkvg +
                 [pl.BlockSpec((nh, ts, ts), full3), pl.BlockSpec((nh, ts, dv), full3),
                  pl.BlockSpec((nh, ts, dk), full3), _vspec(l, nh * dv), state])
    last8 = pl.BlockSpec((None, SUBLANES, cw), lambda bi, cb, ti: (bi, 0, cb))
    return _pcall(
        functools.partial(_lru_ret_kernel, nh=nh, dk=dk, dv=dv, bb=bb, ts=ts), name="lru_prompt_ret_sample",
        grid=(b, ncb, nt),
        in_specs=lru_specs + ret_specs,
        args=(proj, proj, conv_w, conv_b3, w_r, b_r3, w_i, b_i3, lam3,
              g_c, proj, proj, proj, proj, dmask, xi_t, zeta_t, gn3, s0_all),
        out_specs=[pl.BlockSpec((r, cw), lambda bi, cb, ti: (bi * nt + ti, cb)), last8, last8,
                   pl.BlockSpec((rows, nh * dv), lambda *g: (rb0 + lin(*g), 0)), state],
        out_shape=[jax.ShapeDtypeStruct((m, width), BF16),
                   jax.ShapeDtypeStruct((b, SUBLANES, width), F32),
                   jax.ShapeDtypeStruct((b, SUBLANES, width), F32),
                   jax.ShapeDtypeStruct(ya_prev.shape, BF16),
                   jax.ShapeDtypeStruct((depth, bs, nh, dk, dv), F32)],
        prev=(None, None, None, ya_prev, s_prev),
        scratch=[pltpu.VMEM((SUBLANES, cw), F32)],
        sem=("arbitrary", "arbitrary", "arbitrary"))


def _lru_sample(proj, ux_col, l, lru_w, row0, h0_rep, buf_fr, yb_prev):
    conv_w, conv_b3, w_r, b_r3, w_i, b_i3, lam3 = lru_w
    width = lam3.shape[-1]
    lb = w_r.shape[-1]
    ncw = conv_w.shape[1]
    ms = h0_rep.shape[1]
    cw = _pick(width, (1024, 512, 256, 128))
    r = _pick(ms, (256, 128, 64, 32, 16, 8))
    assert row0 % r == 0
    rb0 = row0 // r
    specs = _lru_specs(l, r, cw, ncw, lb, ux_col // cw, (ux_col + width) // cw,
                       lambda ri, cb: rb0 + ri, lambda ri, cb: cb)
    st = pl.BlockSpec((None, r, cw), lambda ri, cb: (l, ri, cb))
    f32rows = pl.BlockSpec((r, cw), lambda ri, cb: (ri, cb))
    return _pcall(
        _lru_sample_kernel, name="lru_sample", grid=(ms // r, width // cw),
        in_specs=specs + [st, st],
        args=(proj, proj, conv_w, conv_b3, w_r, b_r3, w_i, b_i3, lam3, h0_rep, buf_fr),
        out_specs=[pl.BlockSpec((r, cw), lambda ri, cb: (rb0 + ri, cb)), f32rows, f32rows],
        out_shape=[jax.ShapeDtypeStruct(yb_prev.shape, BF16),
                   jax.ShapeDtypeStruct((ms, width), F32), jax.ShapeDtypeStruct((ms, width), F32)],
        prev=(yb_prev,),
        sem=("arbitrary", "arbitrary"))


def _pack_pair(hi, lo):
    hb = lax.bitcast_convert_type(hi.astype(jnp.bfloat16).astype(F32), jnp.uint32)
    lb = lax.bitcast_convert_type(lo.astype(jnp.bfloat16).astype(F32), jnp.uint32)
    return hb | (lb >> 16)


def _unpack_pair(p):
    hi = lax.bitcast_convert_type(p & jnp.uint32(0xFFFF0000), F32)
    lo = lax.bitcast_convert_type(p << 16, F32)
    return hi, lo


def _router_kernel(x_ref, g_ref, wr_ref, o_ref, hp_ref, *, ne):
    hf = _rms(x_ref[...], g_ref[...])
    half = hf.shape[1] // 2
    hp_ref[...] = _pack_pair(hf[:, :half], hf[:, half:])
    h = hf.astype(BF16)
    logits = jnp.dot(h, wr_ref[...].astype(BF16), preferred_element_type=F32)
    lane = lax.broadcasted_iota(jnp.int32, logits.shape, 1).astype(F32)
    neg = F32(-jnp.inf)
    big = F32(LANES)
    l1 = jnp.where(lane < ne, logits, neg)
    m1 = jnp.max(l1, axis=-1, keepdims=True)
    i1 = jnp.min(jnp.where(l1 == m1, lane, big), axis=-1, keepdims=True)
    l2 = jnp.where(lane == i1, neg, l1)
    m2 = jnp.max(l2, axis=-1, keepdims=True)
    i2 = jnp.min(jnp.where(l2 == m2, lane, big), axis=-1, keepdims=True)
    e2 = jnp.exp(m2 - m1)
    den = 1.0 + e2
    o_ref[...] = (jnp.where(lane == 0.0, i1, 0.0) + jnp.where(lane == 1.0, i2, 0.0)
                  + jnp.where(lane == 2.0, 1.0 / den, 0.0) + jnp.where(lane == 3.0, e2 / den, 0.0))


def _router(x, g3, l, wr):
    m, d = x.shape
    ne = wr.shape[-1]
    assert TOP_K == 2 and ne <= LANES
    wr_pad = jnp.pad(wr, ((0, 0), (0, LANES - ne)))
    tm = _pick(m, (512, 256, 128, 64, 32, 16, 8))
    return _pcall(
        functools.partial(_router_kernel, ne=ne), name="router", grid=(m // tm,),
        in_specs=[pl.BlockSpec((tm, d), lambda i: (i, 0)), _vspec(l, d),
                  pl.BlockSpec((d, LANES), lambda i: (0, 0))],
        args=(x, g3, wr_pad),
        out_specs=[pl.BlockSpec((tm, LANES), lambda i: (i, 0)),
                   pl.BlockSpec((tm, d // 2), lambda i: (i, 0))],
        out_shape=[jax.ShapeDtypeStruct((m, LANES), F32),
                   jax.ShapeDtypeStruct((m, d // 2), jnp.uint32)],
        sem=("arbitrary",))


def _dispatch_tables(rout, ne, tg):
    m = rout.shape[0]
    na = TOP_K * m
    eid = rout[:, :TOP_K].astype(jnp.int32).T.reshape(na)
    onehot = (eid[:, None] == jnp.arange(ne, dtype=jnp.int32)[None, :]).astype(jnp.int32)
    cnt = jnp.sum(onehot, axis=0)
    rank = jnp.sum((jnp.cumsum(onehot, axis=0) - onehot) * onehot, axis=1)
    padded = ((cnt + tg - 1) // tg) * tg
    gend = jnp.cumsum(padded)
    gstart = gend - padded
    rem = cnt % tg
    first = jnp.where(rem > 0, rem, tg)
    local = jnp.where(rank < first[eid], rank, rank - first[eid] + tg)
    pos = gstart[eid] + local
    p_rows = ((na + tg - 1) // tg + ne) * tg
    n_tiles = p_rows // tg
    tok = jnp.zeros((p_rows,), jnp.int32).at[pos].set(jnp.arange(na, dtype=jnp.int32) % m)
    n_used = (gend[-1] // tg).astype(jnp.int32)
    tstart = jnp.arange(n_tiles, dtype=jnp.int32) * tg
    tstart = jnp.minimum(tstart, (n_used - 1) * tg)
    te = jnp.minimum(jnp.searchsorted(gend, tstart, side="right"), ne - 1).astype(jnp.int32)
    tidx = jnp.arange(n_tiles, dtype=jnp.int32)
    n_valid = jnp.where(tidx * tg == gstart[te], first[te], tg)
    n_valid = jnp.where(tidx < n_used, n_valid, 0).astype(jnp.int32)
    return pos, tok, te, n_used.reshape(1), n_valid, p_rows


DMA_UNROLL = 8


def _row_gather(src_hbm, dst, idx_of, sem):
    rows = dst.shape[0]
    assert rows % DMA_UNROLL == 0

    def row_copy(r, t):
        return pltpu.make_async_copy(src_hbm.at[pl.ds(t, 1), :], dst.at[pl.ds(r, 1), :], sem)

    def issue(c, carry):
        for u in range(DMA_UNROLL):
            r = c * DMA_UNROLL + u
            row_copy(r, idx_of(r)).start()
        return carry

    def drain(c, carry):
        for u in range(DMA_UNROLL):
            row_copy(c * DMA_UNROLL + u, 0).wait()
        return carry

    start = lambda: lax.fori_loop(0, rows // DMA_UNROLL, issue, 0)
    wait = lambda: lax.fori_loop(0, rows // DMA_UNROLL, drain, 0)
    return start, wait


def _moe_gather_kernel(tok_ref, gv_ref, hp_ref, o_ref):
    tr, d = o_ref.shape
    half = d // 2
    gv = gv_ref[0, 0]
    group = 2 * SUBLANES

    @pl.when(gv < tr)
    def _():
        o_ref[...] = jnp.zeros_like(o_ref)

    def body(g, carry):
        r0 = pl.multiple_of(g * group, group)
        rows = [hp_ref[pl.ds(tok_ref[0, r0 + u], 1), :] for u in range(group)]
        hi, lo = _unpack_pair(jnp.concatenate(rows, axis=0))
        o_ref[pl.ds(r0, group), pl.ds(0, half)] = hi.astype(o_ref.dtype)
        o_ref[pl.ds(r0, group), pl.ds(half, half)] = lo.astype(o_ref.dtype)
        return carry

    lax.fori_loop(0, (gv + group - 1) // group, body, 0)


def _moe_gather(hp, tok, n_valid, tg, p_rows):
    m, half = hp.shape
    tr = _pick(tg, (GATHER_ROWS, 128, 64, 32, 16))
    nt = p_rows // tr
    per = tg // tr
    gidx = jnp.arange(nt, dtype=jnp.int32)
    gv = jnp.clip(n_valid[gidx // per] - (gidx % per) * tr, 0, tr).astype(jnp.int32)
    return _pcall(
        _moe_gather_kernel, name="moe_gather", grid=(nt,),
        in_specs=[pl.BlockSpec((None, 1, tr), lambda i: (i, 0, 0), memory_space=pltpu.SMEM),
                  pl.BlockSpec((None, 1, 1), lambda i: (i, 0, 0), memory_space=pltpu.SMEM),
                  pl.BlockSpec((m, half), lambda i: (0, 0), pipeline_mode=pl.Buffered(1))],
        args=(tok.reshape(nt, 1, tr), gv.reshape(nt, 1, 1), hp),
        out_specs=[pl.BlockSpec((tr, 2 * half), lambda i: (i, 0))],
        out_shape=[jax.ShapeDtypeStruct((p_rows, 2 * half), BF16)],
        sem=("arbitrary",))[0]


def _tile_changed(te_ref):
    ti = pl.program_id(1)
    return jnp.logical_or(ti == 0, te_ref[ti] != te_ref[jnp.maximum(ti - 1, 0)])


def _valid_row_chunks(nv, o_ref, chunk):
    tg = o_ref.shape[0]

    @pl.when(nv == tg)
    def _():
        _row_chunks(tg, chunk)

    @pl.when(nv < tg)
    def _():
        def maybe(rs):
            @pl.when(rs.start < nv)
            def _():
                chunk(rs)

            @pl.when(rs.start >= nv)
            def _():
                o_ref[rs, :] = jnp.zeros((rs.stop - rs.start, o_ref.shape[1]), o_ref.dtype)

        _row_chunks(tg, maybe)


def _gmm_swiglu_kernel(te_ref, nu_ref, nv_ref, x_ref, wg_ref, wu_ref, o_ref, wgb_ref, wub_ref):
    @pl.when(_tile_changed(te_ref))
    def _():
        _cast_w(wg_ref, wgb_ref)
        _cast_w(wu_ref, wub_ref)

    def chunk(rs):
        x = x_ref[rs, :]
        g = jnp.dot(x, wgb_ref[...], preferred_element_type=F32)
        u = jnp.dot(x, wub_ref[...], preferred_element_type=F32)
        o_ref[rs, :] = (jax.nn.silu(g) * u).astype(o_ref.dtype)

    _valid_row_chunks(nv_ref[pl.program_id(1)], o_ref, chunk)


def _gmm_down_kernel(te_ref, nu_ref, nv_ref, x_ref, w_ref, o_ref, wb_ref):
    @pl.when(_tile_changed(te_ref))
    def _():
        _cast_w(w_ref, wb_ref)

    hw = o_ref.shape[1]

    def chunk(rs):
        y = jnp.dot(x_ref[rs, :], wb_ref[...], preferred_element_type=F32)
        o_ref[rs, :] = _pack_pair(y[:, :hw], y[:, hw:])

    _valid_row_chunks(nv_ref[pl.program_id(1)], o_ref, chunk)


def _gmm_specs(jm, tg, k, tn, j0=0):
    xrow = lambda j, ti, te, nu, nv: (jnp.minimum(ti, nu[0] - 1), 0)
    wsp = pl.BlockSpec((None, None, k, tn), lambda j, ti, te, nu, nv: (jm, te[ti], 0, j0 + j))
    return pl.BlockSpec((tg, k), xrow), wsp, pl.BlockSpec((tg, tn), lambda j, ti, te, nu, nv: (ti, j0 + j))


def _gmm_swiglu(xg, wg, wu, jm, te, n_used, n_valid, tg, tns):
    p, k = xg.shape
    n = wg.shape[-1]
    assert sum(tn * cnt for tn, cnt in tns) == n
    out, col = None, 0
    for tn, cnt in tns:
        assert col % tn == 0
        xs, ws, os_ = _gmm_specs(jm, tg, k, tn, col // tn)
        out = _pcall(
            _gmm_swiglu_kernel, name="gmm_swiglu", grid=(cnt, p // tg), num_scalar_prefetch=3,
            in_specs=[None, None, None, xs, ws, ws], args=(te, n_used, n_valid, xg, wg, wu),
            out_specs=[os_], out_shape=[jax.ShapeDtypeStruct((p, n), BF16)],
            scratch=[pltpu.VMEM((k, tn), BF16), pltpu.VMEM((k, tn), BF16)],
            prev=(out,), sem=("arbitrary", "arbitrary"))[0]
        col += tn * cnt
    return out


def _col_tiling(n, widths):
    out, col = [], 0
    for wdt in widths:
        cnt = (n - col) // wdt
        if cnt:
            out.append((wdt, cnt))
            col += cnt * wdt
    assert col == n
    return tuple(out)


def _gmm_down(ug, wd, jm, te, n_used, n_valid, tg, tn):
    p, k = ug.shape
    n = wd.shape[-1]
    xs, ws, _ = _gmm_specs(jm, tg, k, tn)
    return _pcall(
        _gmm_down_kernel, name="gmm_down", grid=(n // tn, p // tg), num_scalar_prefetch=3,
        in_specs=[None, None, None, xs, ws], args=(te, n_used, n_valid, ug, wd),
        out_specs=[pl.BlockSpec((tg, tn // 2), lambda j, ti, te, nu, nv: (ti, j))],
        out_shape=[jax.ShapeDtypeStruct((p, n // 2), jnp.uint32)],
        scratch=[pltpu.VMEM((k, tn), BF16)],
        sem=("arbitrary", "arbitrary"))[0]


def _moe_combine_kernel(pos_ref, nxt_ref, x_ref, rw_ref, g_ref, yg_hbm, *rest, tn, np_tiles):
    outs, (buf, sem) = rest[:-2], rest[-2:]
    tc = x_ref.shape[0]
    i = pl.program_id(0)
    slot = lax.rem(i, 2)

    def gathers(idx_ref, s):
        return [_row_gather(yg_hbm, buf.at[s, k], lambda r, k=k: idx_ref[0, k * tc + r], sem.at[s])
                for k in range(TOP_K)]

    @pl.when(i == 0)
    def _():
        for start, _ in gathers(pos_ref, 0):
            start()

    @pl.when(i + 1 < pl.num_programs(0))
    def _():
        for start, _ in gathers(nxt_ref, 1 - slot):
            start()

    for _, wait in gathers(pos_ref, slot):
        wait()
    rows = buf[slot]

    def expand(p):
        hw = tn // 2
        parts = []
        for jj in range(p.shape[1] // hw):
            parts.extend(_unpack_pair(p[:, jj * hw:(jj + 1) * hw]))
        return jnp.concatenate(parts, axis=-1)

    rw = rw_ref[...]
    y = rw[:, TOP_K:TOP_K + 1] * expand(rows[0])
    for k in range(1, TOP_K):
        y = y + rw[:, TOP_K + k:TOP_K + k + 1] * expand(rows[k])
    xn = x_ref[...] + y
    if np_tiles is None:
        outs[0][...] = xn
    else:
        yn = _rms(xn, g_ref[...])

        @pl.when(i < np_tiles)
        def _():
            outs[0][...] = yn

        @pl.when(i >= np_tiles)
        def _():
            outs[1][...] = yn


def _moe_combine(x, rout, pos, ygp, tn, g3=None, mp=None):
    m, d = x.shape
    tc = _pick(m, (COMBINE_ROWS, 64, 32, 16, 8))
    nt = m // tc
    pos_t = pos.reshape(TOP_K, nt, tc).transpose(1, 0, 2).reshape(nt, 1, TOP_K * tc)
    if g3 is None:
        g3 = jnp.ones((1, 1, d), F32)
        np_tiles = None
        out_specs = [pl.BlockSpec((tc, d), lambda i: (i, 0))]
        out_shape = [jax.ShapeDtypeStruct((m, d), F32)]
    else:
        assert mp % tc == 0
        np_tiles = mp // tc
        out_specs = [pl.BlockSpec((tc, d), lambda i: (jnp.minimum(i, np_tiles - 1), 0)),
                     pl.BlockSpec((tc, d), lambda i: (jnp.maximum(i - np_tiles, 0), 0))]
        out_shape = [jax.ShapeDtypeStruct((mp, d), F32), jax.ShapeDtypeStruct((m - mp, d), F32)]
    return _pcall(
        functools.partial(_moe_combine_kernel, tn=tn, np_tiles=np_tiles), name="moe_combine", grid=(nt,),
        in_specs=[pl.BlockSpec((None, 1, TOP_K * tc), lambda i: (i, 0, 0), memory_space=pltpu.SMEM),
                  pl.BlockSpec((None, 1, TOP_K * tc), lambda i: (jnp.minimum(i + 1, nt - 1), 0, 0),
                               memory_space=pltpu.SMEM),
                  pl.BlockSpec((tc, d), lambda i: (i, 0)),
                  pl.BlockSpec((tc, LANES), lambda i: (i, 0)),
                  _vspec(0, d),
                  pl.BlockSpec(memory_space=pl.ANY)],
        args=(pos_t, pos_t, x, rout, g3, ygp),
        out_specs=out_specs, out_shape=out_shape,
        scratch=[pltpu.VMEM((2, TOP_K, tc, d // 2), jnp.uint32), pltpu.SemaphoreType.DMA((2,))],
        sem=("arbitrary",))


def kernel(x_prompt, x_sample, state_ret, state_lru, state_conv, norm_mix, w_in, ret_gn, w_ret_o, conv_w, conv_b, w_rgate, b_rgate, w_igate, b_igate, lru_lambda, w_lru_o, w_out, norm_ffn, ffn_w_gate, ffn_w_up, ffn_w_down, moe_router, moe_w_gate, moe_w_up, moe_w_down, norm_final):
    bp, tp, d = x_prompt.shape
    bs, ts, _ = x_sample.shape
    depth, _, nh, dk, dv = state_ret.shape
    width = state_lru.shape[-1]
    ncw = conv_w.shape[1]
    ne = moe_router.shape[-1]
    assert dk == dv and nh * dk == width == d and ts == SUBLANES and ncw - 1 <= SUBLANES
    mp, ms = bp * tp, bs * ts
    m = mp + ms
    g_col = 2 * nh * dk + nh * dv
    ux_col = g_col + nh * dv
    ga_col = ux_col + 2 * width

    tm = _pick(m, (1024, 512, 256, 128, 64, 32, 16, 8))
    tm_big = _pick(m, (1536, 1024, 512, 256, 128, 64, 32, 16, 8))
    tm_half = _pick(m, (512, 256, 128, 64, 32, 16, 8))
    tn_of = lambda n: _pick(n, (1024, 512, 256, 128))

    vec3 = lambda a: a.reshape(a.shape[0], 1, a.shape[-1])
    norm_mix3, norm_ffn3, ret_gn3 = vec3(norm_mix), vec3(norm_ffn), vec3(ret_gn)
    norm_final3 = norm_final.reshape(1, 1, d)
    lru_w = (conv_w, vec3(conv_b), w_rgate, vec3(b_rgate), w_igate, vec3(b_igate), vec3(lru_lambda))
    h0_rep = jnp.repeat(state_lru, ts, axis=1)
    buf_fr = jnp.pad(state_conv, ((0, 0), (0, 0), (ts - (ncw - 1), 0), (0, 0))).reshape(depth, ms, width)

    cos_p, sin_p = _rope_tables(tp, dk // 2, 0)
    cos_s, sin_s = _rope_tables(ts, dk // 2, PAST_LEN)
    cos = jnp.concatenate([jnp.tile(cos_p, (bp, 1)), jnp.tile(cos_s, (bs, 1))], axis=0)
    sin = jnp.concatenate([jnp.tile(sin_p, (bp, 1)), jnp.tile(sin_s, (bs, 1))], axis=0)
    tn_in = _pick(nh * dk, (1024, 512, 256))

    x = (x_prompt.reshape(mp, d), x_sample.reshape(ms, d))
    if mp % tm_half or ms % tm_half:
        x = jnp.concatenate(x, axis=0)
    ret_p = ret_s = y_p = y_s = None
    lrus_p, lrus_s, convs_p, convs_s = [], [], [], []
    for l in range(depth):
        if isinstance(x, tuple):
            h = _rmsnorm(x[0], norm_mix3, l, BF16, out_rows=m)
            h = _rmsnorm(x[1], norm_mix3, l, BF16, out_row0=mp, out_rows=m, prev=h)
        else:
            h = _rmsnorm(x, norm_mix3, l, BF16)
        proj = _mm_inproj(h, w_in, (l,), cos, sin, 2 * nh * dk, dk, (g_col, ux_col),
                          (ux_col + width, ga_col), tm_big, tn_in)

        ya, ret_p = _ret_prompt(proj, ret_gn3, l, depth, bp, tp, nh, dk, dv, ret_p)
        if _fused_seqs_per_step(width, tp, bp, bs, ts) is None:
            ya, ret_s = _ret_sample(proj, ret_gn3, state_ret, l, mp, bs, ts, nh, dk, dv, ya, ret_s)
            yb, hl_p, ul_p = _lru_prompt(proj, ux_col, l, lru_w, bp, tp)
        else:
            yb, hl_p, ul_p, ya, ret_s = _lru_prompt_ret_sample(
                proj, ux_col, l, lru_w, bp, tp, ret_gn3, state_ret, mp, bs, ts, nh, dk, dv, ya, ret_s)
        yb, hl_s, us_s = _lru_sample(proj, ux_col, l, lru_w, mp, h0_rep, buf_fr, yb)
        lrus_p.append(hl_p[:, -1])
        lrus_s.append(hl_s.reshape(bs, ts, width)[:, -1])
        convs_p.append(ul_p[:, SUBLANES - (ncw - 1):])
        convs_s.append(us_s.reshape(bs, ts, width)[:, ts - (ncw - 1):])

        z = _mm_merge(ya, yb, w_ret_o, w_lru_o, (l,), proj, ga_col, ga_col + d, tm, tn_of(d) // 2)
        x = _mm_resid(z, w_out, (l,), x, tm_half if isinstance(x, tuple) else tm, tn_of(d))
        j = l // 2
        if l % 2 == 0:
            h2 = _rmsnorm(x, norm_ffn3, l, BF16)
            ff = ffn_w_gate.shape[-1]
            u = _mm_swiglu(h2, ffn_w_gate, ffn_w_up, (j,), tm_big, _pick(ff, (512, 256, 128)))
            x = _mm_resid(u, ffn_w_down, (j,), x, tm_half, _pick(d, (512, 256, 128)))
        else:
            fe = moe_w_gate.shape[-1]
            tg = _pick(TOP_K * m, (MOE_TILE, 256, 128, 64, 32, 16))
            tn_d = _pick(d, (1024, 512, 256))
            rout, hp = _router(x, norm_ffn3, l, moe_router[j])
            pos, tok, te, n_used, n_valid, p_rows = _dispatch_tables(rout, ne, tg)
            xg = _moe_gather(hp, tok, n_valid, tg, p_rows)
            ug = _gmm_swiglu(xg, moe_w_gate, moe_w_up, j, te, n_used, n_valid, tg,
                             _col_tiling(fe, (1024, 512, 256, 128)))
            ygp = _gmm_down(ug, moe_w_down, j, te, n_used, n_valid, tg, tn_d)
            if l == depth - 1:
                y_p, y_s = _moe_combine(x, rout, pos, ygp, tn_d, norm_final3, mp)
            else:
                x = _moe_combine(x, rout, pos, ygp, tn_d)[0]
    if y_p is None:
        y_p = _rmsnorm(x, norm_final3, 0, F32, 0, mp)
        y_s = _rmsnorm(x, norm_final3, 0, F32, mp, ms)
    return (y_p.reshape(bp, tp, d), y_s.reshape(bs, ts, d),
            ret_p, jnp.stack(lrus_p), jnp.stack(convs_p),
            ret_s, jnp.stack(lrus_s), jnp.stack(convs_s))
```

```python
import functools

import jax
import jax.numpy as jnp
from jax import lax
from jax.experimental import pallas as pl
from jax.experimental.pallas import tpu as pltpu

F32 = jnp.float32
BF16 = jnp.bfloat16

ROPE_BASE = 10000.0
LRU_C = 8.0
EPS = 1e-6
RET_CHUNK = 128
PAST_LEN = 16384
TOP_K = 2
SUBLANES = 8
LANES = 128
VMEM_LIMIT = 56 * 1024 * 1024
MOE_TILE = 1024
GATHER_ROWS = 1024
COMBINE_ROWS = 512


def _pick(n, prefs):
    for p in prefs:
        if n % p == 0:
            return p
    return n


def _params(*sem):
    return pltpu.CompilerParams(dimension_semantics=sem, vmem_limit_bytes=VMEM_LIMIT)


def _pcall(body, *, name, grid, in_specs, args, out_specs, out_shape, sem, scratch=(), prev=(),
           num_scalar_prefetch=0):
    n_in = len(args)
    prev = tuple(prev) + (None,) * (len(out_shape) - len(prev))
    extra = [(oi, p) for oi, p in enumerate(prev) if p is not None]
    aliases = {n_in + e: oi for e, (oi, _) in enumerate(extra)}
    if extra:
        inner = body

        def body(*refs):
            return inner(*refs[:n_in], *refs[n_in + len(extra):])

    in_specs = list(in_specs) + [pl.BlockSpec(memory_space=pl.ANY)] * len(extra)
    if num_scalar_prefetch:
        grid_spec = pltpu.PrefetchScalarGridSpec(
            num_scalar_prefetch=num_scalar_prefetch, grid=grid, in_specs=in_specs[num_scalar_prefetch:],
            out_specs=tuple(out_specs), scratch_shapes=list(scratch))
        call = pl.pallas_call(body, grid_spec=grid_spec, out_shape=tuple(out_shape),
                              input_output_aliases=aliases, compiler_params=_params(*sem), name=name)
    else:
        call = pl.pallas_call(body, grid=grid, in_specs=in_specs, out_specs=tuple(out_specs),
                              out_shape=tuple(out_shape), scratch_shapes=list(scratch),
                              input_output_aliases=aliases, compiler_params=_params(*sem), name=name)
    return call(*args, *[p for _, p in extra])


def _wspec(pre, k, tn):
    return pl.BlockSpec((None,) * len(pre) + (k, tn), lambda j, i: pre + (0, j))


def _vspec(l, n):
    return pl.BlockSpec((None, 1, n), lambda *_: (l, 0, 0))


def _rms(x, g):
    return (x * lax.rsqrt(jnp.mean(x * x, axis=-1, keepdims=True) + EPS)) * g


def _rmsnorm_kernel(x_ref, g_ref, o_ref):
    o_ref[...] = _rms(x_ref[...], g_ref[...]).astype(o_ref.dtype)


def _rmsnorm(x, g3, l, out_dtype, row0=0, rows=None, out_row0=0, out_rows=None, prev=None):
    d = x.shape[1]
    rows = x.shape[0] if rows is None else rows
    out_rows = rows if out_rows is None else out_rows
    tm = _pick(rows, (512, 256, 128, 64, 32, 16, 8))
    assert row0 % tm == 0 and out_row0 % tm == 0
    rb0, ob0 = row0 // tm, out_row0 // tm
    return _pcall(
        _rmsnorm_kernel, name="rmsnorm", grid=(rows // tm,),
        in_specs=[pl.BlockSpec((tm, d), lambda i: (rb0 + i, 0)), _vspec(l, d)],
        args=(x, g3),
        out_specs=[pl.BlockSpec((tm, d), lambda i: (ob0 + i, 0))],
        out_shape=[jax.ShapeDtypeStruct((out_rows, d), out_dtype)],
        prev=(prev,), sem=("arbitrary",))[0]


def _cast_w(w_ref, wb_ref):
    k = w_ref.shape[0]
    ck = _pick(k, (256, 128, 64, 32, 16))

    def body(c, carry):
        r = pl.multiple_of(c * ck, ck)
        wb_ref[pl.ds(r, ck), :] = w_ref[pl.ds(r, ck), :].astype(BF16)
        return carry

    lax.fori_loop(0, k // ck, body, 0)


def _mm_inproj_kernel(x_ref, w_ref, *rest, j0, nrot, dk, silu_tiles, gelu_tiles):
    has_rot = len(rest) == 4
    cos_ref, sin_ref = rest[:2] if has_rot else (None, None)
    o_ref, wb_ref = rest[-2:]
    j = j0 + pl.program_id(0)

    @pl.when(pl.program_id(1) == 0)
    def _():
        _cast_w(w_ref, wb_ref)

    tm, tn = o_ref.shape
    nchunk = 4 if tm % (4 * 16) == 0 else 1

    def row_chunks(epilogue):
        rc = tm // nchunk
        for c in range(nchunk):
            rs = slice(c * rc, (c + 1) * rc)
            epilogue(jnp.dot(x_ref[rs, :], wb_ref[...], preferred_element_type=F32), rs)

    def store(fn):
        def epilogue(acc, rs):
            o_ref[rs, :] = fn(acc).astype(o_ref.dtype)
        return epilogue

    in_tiles = lambda t: jnp.logical_and(j >= t[0], j < t[1])
    is_silu, is_gelu = in_tiles(silu_tiles), in_tiles(gelu_tiles)

    @pl.when(jnp.logical_and(j >= nrot, jnp.logical_not(jnp.logical_or(is_silu, is_gelu))))
    def _():
        row_chunks(store(lambda acc: acc))

    @pl.when(is_silu)
    def _():
        row_chunks(store(jax.nn.silu))

    @pl.when(is_gelu)
    def _():
        row_chunks(store(jax.nn.gelu))

    if has_rot:
        @pl.when(j < nrot)
        def _():
            scale = jnp.where(j >= nrot // 2, F32(dk ** -0.5), F32(1.0))
            half = dk // 2

            def rotary(acc, rs):
                cos = cos_ref[rs, :] * scale
                sin = sin_ref[rs, :] * scale
                for h in range(tn // dk):
                    x1 = acc[:, h * dk:h * dk + half]
                    x2 = acc[:, h * dk + half:(h + 1) * dk]
                    o_ref[rs, h * dk:h * dk + half] = (x1 * cos - x2 * sin).astype(o_ref.dtype)
                    o_ref[rs, h * dk + half:(h + 1) * dk] = (x1 * sin + x2 * cos).astype(o_ref.dtype)

            row_chunks(rotary)


def _mm_inproj(x, w, pre, cos, sin, qk_cols, dk, silu_cols, gelu_cols, tm, tn, cols=None, prev=None):
    m, k = x.shape
    n = w.shape[-1]
    half = cos.shape[1]
    c0, c1 = (0, n) if cols is None else cols
    assert qk_cols % (2 * tn) == 0 and tn % dk == 0 and c0 % tn == 0 and c1 % tn == 0
    assert all(c % tn == 0 for c in silu_cols + gelu_cols)
    tiles = lambda cc: (cc[0] // tn, cc[1] // tn)
    j0 = c0 // tn
    has_rot = c0 < qk_cols
    rot_specs = [pl.BlockSpec((tm, half), lambda j, i: (i, 0))] * 2 if has_rot else []
    wspec = pl.BlockSpec((None,) * len(pre) + (k, tn), lambda j, i: pre + (0, j0 + j))
    return _pcall(
        functools.partial(_mm_inproj_kernel, j0=j0, nrot=qk_cols // tn, dk=dk,
                          silu_tiles=tiles(silu_cols), gelu_tiles=tiles(gelu_cols)), name="mm_inproj",
        grid=((c1 - c0) // tn, m // tm),
        in_specs=[pl.BlockSpec((tm, k), lambda j, i: (i, 0)), wspec] + rot_specs,
        args=(x, w) + ((cos, sin) if has_rot else ()),
        out_specs=[pl.BlockSpec((tm, tn), lambda j, i: (i, j0 + j))],
        out_shape=[jax.ShapeDtypeStruct((m, n), BF16)],
        scratch=[pltpu.VMEM((k, tn), BF16)],
        prev=(prev,), sem=("arbitrary", "arbitrary"))[0]


def _mm_resid_kernel(x_ref, w_ref, r_ref, o_ref, wb_ref):
    @pl.when(pl.program_id(1) == 0)
    def _():
        _cast_w(w_ref, wb_ref)

    o_ref[...] = r_ref[...] + jnp.dot(x_ref[...], wb_ref[...], preferred_element_type=F32)


def _mm_resid2_kernel(x_ref, w_ref, ra_ref, rb_ref, o_ref, wb_ref, *, na):
    @pl.when(pl.program_id(1) == 0)
    def _():
        _cast_w(w_ref, wb_ref)

    y = jnp.dot(x_ref[...], wb_ref[...], preferred_element_type=F32)
    i = pl.program_id(1)

    @pl.when(i < na)
    def _():
        o_ref[...] = ra_ref[...] + y

    @pl.when(i >= na)
    def _():
        o_ref[...] = rb_ref[...] + y


def _mm_resid(x, w, pre, r, tm, tn):
    m, k = x.shape
    n = w.shape[-1]
    if isinstance(r, tuple):
        ra, rb = r
        assert ra.shape[0] % tm == 0 and rb.shape[0] % tm == 0 and ra.shape[0] + rb.shape[0] == m
        na = ra.shape[0] // tm
        body = functools.partial(_mm_resid2_kernel, na=na)
        r_specs = [pl.BlockSpec((tm, tn), lambda j, i: (jnp.minimum(i, na - 1), j)),
                   pl.BlockSpec((tm, tn), lambda j, i: (jnp.maximum(i - na, 0), j))]
        r_args = (ra, rb)
    else:
        body, r_specs, r_args = _mm_resid_kernel, [pl.BlockSpec((tm, tn), lambda j, i: (i, j))], (r,)
    return _pcall(
        body, name="mm_resid", grid=(n // tn, m // tm),
        in_specs=[pl.BlockSpec((tm, k), lambda j, i: (i, 0)), _wspec(pre, k, tn)] + r_specs,
        args=(x, w) + r_args,
        out_specs=[pl.BlockSpec((tm, tn), lambda j, i: (i, j))],
        out_shape=[jax.ShapeDtypeStruct((m, n), F32)],
        scratch=[pltpu.VMEM((k, tn), BF16)],
        sem=("arbitrary", "arbitrary"))[0]


def _row_chunks(rows, fn, nchunk=4):
    n = nchunk if rows % (nchunk * 16) == 0 else 1
    rc = rows // n
    for c in range(n):
        fn(slice(c * rc, (c + 1) * rc))


def _swiglu_rows(x_ref, wgb_ref, wub_ref, o_ref):
    def chunk(rs):
        x = x_ref[rs, :]
        g = jnp.dot(x, wgb_ref[...], preferred_element_type=F32)
        u = jnp.dot(x, wub_ref[...], preferred_element_type=F32)
        o_ref[rs, :] = (jax.nn.silu(g) * u).astype(o_ref.dtype)

    _row_chunks(o_ref.shape[0], chunk)


def _mm_swiglu_kernel(x_ref, wg_ref, wu_ref, o_ref, wgb_ref, wub_ref):
    @pl.when(pl.program_id(1) == 0)
    def _():
        _cast_w(wg_ref, wgb_ref)
        _cast_w(wu_ref, wub_ref)

    _swiglu_rows(x_ref, wgb_ref, wub_ref, o_ref)


def _mm_swiglu(x, wg, wu, pre, tm, tn):
    m, k = x.shape
    n = wg.shape[-1]
    return _pcall(
        _mm_swiglu_kernel, name="mm_swiglu", grid=(n // tn, m // tm),
        in_specs=[pl.BlockSpec((tm, k), lambda j, i: (i, 0)), _wspec(pre, k, tn), _wspec(pre, k, tn)],
        args=(x, wg, wu),
        out_specs=[pl.BlockSpec((tm, tn), lambda j, i: (i, j))],
        out_shape=[jax.ShapeDtypeStruct((m, n), BF16)],
        scratch=[pltpu.VMEM((k, tn), BF16), pltpu.VMEM((k, tn), BF16)],
        sem=("arbitrary", "arbitrary"))[0]


def _mm_merge_kernel(a_ref, b_ref, wa_ref, wb_ref, ga_ref, gb_ref, o_ref, wab_ref, wbb_ref):
    @pl.when(pl.program_id(1) == 0)
    def _():
        _cast_w(wa_ref, wab_ref)
        _cast_w(wb_ref, wbb_ref)

    def chunk(rs):
        ya = jnp.dot(a_ref[rs, :], wab_ref[...], preferred_element_type=F32)
        yb = jnp.dot(b_ref[rs, :], wbb_ref[...], preferred_element_type=F32)
        ga = ga_ref[rs, :].astype(F32)
        gb = gb_ref[rs, :].astype(F32)
        o_ref[rs, :] = (jax.nn.sigmoid(ga) * ya + jax.nn.sigmoid(gb) * yb).astype(o_ref.dtype)

    _row_chunks(o_ref.shape[0], chunk)


def _mm_merge(a, b, wa, wb, pre, proj, ga_col, gb_col, tm, tn):
    m, k = a.shape
    n = wa.shape[-1]
    ga_blk, gb_blk = ga_col // tn, gb_col // tn
    return _pcall(
        _mm_merge_kernel, name="mm_merge", grid=(n // tn, m // tm),
        in_specs=[pl.BlockSpec((tm, k), lambda j, i: (i, 0)),
                  pl.BlockSpec((tm, k), lambda j, i: (i, 0)),
                  _wspec(pre, k, tn), _wspec(pre, k, tn),
                  pl.BlockSpec((tm, tn), lambda j, i: (i, ga_blk + j)),
                  pl.BlockSpec((tm, tn), lambda j, i: (i, gb_blk + j))],
        args=(a, b, wa, wb, proj, proj),
        out_specs=[pl.BlockSpec((tm, tn), lambda j, i: (i, j))],
        out_shape=[jax.ShapeDtypeStruct((m, n), BF16)],
        scratch=[pltpu.VMEM((k, tn), BF16), pltpu.VMEM((k, tn), BF16)],
        sem=("arbitrary", "arbitrary"))[0]


def _rope_kernel(inv_ref, cos_ref, sin_ref, *, pos0):
    t, half = cos_ref.shape
    pos = lax.broadcasted_iota(jnp.int32, (t, half), 0).astype(F32) + F32(pos0)
    ang = pos * inv_ref[...]
    cos_ref[...] = jnp.cos(ang)
    sin_ref[...] = jnp.sin(ang)


def _rope_tables(t, half, pos0):
    inv = ROPE_BASE ** (-jnp.arange(half, dtype=F32) / half)
    return pl.pallas_call(
        functools.partial(_rope_kernel, pos0=pos0),
        out_shape=(jax.ShapeDtypeStruct((t, half), F32), jax.ShapeDtypeStruct((t, half), F32)),
        name="rope_tables",
    )(inv.reshape(1, half))


def _decay_tables(c, h, dk, dv):
    log_g = jnp.log1p(-jnp.exp2(-5.0 - jnp.arange(h, dtype=F32)))
    idx = jnp.arange(c)
    rel = idx[:, None] - idx[None, :]
    dmask = jnp.where(rel[None] >= 0,
                      jnp.exp(jnp.maximum(rel, 0)[None].astype(F32) * log_g[:, None, None]), 0.0)
    xi = jnp.exp((idx + 1).astype(F32)[None, :] * log_g[:, None])
    zeta = jnp.exp((c - 1 - idx).astype(F32)[None, :] * log_g[:, None])
    g_c = jnp.exp(c * log_g)
    xi_t = jnp.broadcast_to(xi[:, :, None], (h, c, dv))
    zeta_t = jnp.broadcast_to(zeta[:, :, None], (h, c, dk))
    return dmask, xi_t, zeta_t, g_c


def _ret_head(qb, kb, v, g, dm, xi, zt, gn, s, gc):
    scores = lax.dot_general(qb, kb, (((1,), (1,)), ((), ())), preferred_element_type=F32)
    intra = jnp.dot((scores * dm).astype(BF16), v, preferred_element_type=F32)
    cross = jnp.dot(qb, s.astype(BF16), preferred_element_type=F32) * xi
    kz = (kb.astype(F32) * zt).astype(BF16)
    upd = lax.dot_general(kz, v, (((0,), (0,)), ((), ())), preferred_element_type=F32)
    s_new = gc * s + upd
    o = intra + cross
    mu = jnp.mean(o, axis=-1, keepdims=True)
    d = o - mu
    var = jnp.mean(d * d, axis=-1, keepdims=True)
    y = d * lax.rsqrt(var + EPS) * gn
    return (g * y).astype(BF16), s_new


def _ret_prompt_kernel(gc_ref, q_ref, k_ref, v_ref, g_ref, dm_ref, xi_ref, zt_ref,
                       gn_ref, o_ref, s_ref, *, nh, dk, dv):
    @pl.when(pl.program_id(1) == 0)
    def _():
        s_ref[...] = jnp.zeros_like(s_ref)

    c = dm_ref.shape[1]
    for h in range(nh):
        s = s_ref[h]
        for ci in range(q_ref.shape[0] // c):
            rs = slice(ci * c, (ci + 1) * c)
            out, s = _ret_head(
                q_ref[rs, h * dk:(h + 1) * dk], k_ref[rs, h * dk:(h + 1) * dk],
                v_ref[rs, h * dv:(h + 1) * dv], g_ref[rs, h * dv:(h + 1) * dv].astype(F32),
                dm_ref[h], xi_ref[h], zt_ref[h], gn_ref[:, h * dv:(h + 1) * dv], s, gc_ref[h])
            o_ref[rs, h * dv:(h + 1) * dv] = out
        s_ref[h] = s


def _ret_prompt(proj, gn3, l, depth, b, t, nh, dk, dv, s_prev):
    cl = RET_CHUNK if t % RET_CHUNK == 0 else t
    cps = _pick(t // cl, (8, 4, 2, 1))
    c = cl * cps
    nc = t // c
    w = nh * dk
    m = proj.shape[0]
    dmask, xi_t, zeta_t, g_c = _decay_tables(cl, nh, dk, dv)
    row = lambda bi, ci: bi * nc + ci
    full3 = lambda bi, ci: (0, 0, 0)
    return _pcall(
        functools.partial(_ret_prompt_kernel, nh=nh, dk=dk, dv=dv), name="ret_prompt", grid=(b, nc),
        in_specs=[pl.BlockSpec(memory_space=pltpu.SMEM),
                  pl.BlockSpec((c, w), lambda bi, ci: (row(bi, ci), 0)),
                  pl.BlockSpec((c, w), lambda bi, ci: (row(bi, ci), 1)),
                  pl.BlockSpec((c, w), lambda bi, ci: (row(bi, ci), 2)),
                  pl.BlockSpec((c, w), lambda bi, ci: (row(bi, ci), 3)),
                  pl.BlockSpec((nh, cl, cl), full3),
                  pl.BlockSpec((nh, cl, dv), full3),
                  pl.BlockSpec((nh, cl, dk), full3),
                  _vspec(l, nh * dv)],
        args=(g_c, proj, proj, proj, proj, dmask, xi_t, zeta_t, gn3),
        out_specs=[pl.BlockSpec((c, nh * dv), lambda bi, ci: (row(bi, ci), 0)),
                   pl.BlockSpec((None, None, nh, dk, dv), lambda bi, ci: (l, bi, 0, 0, 0))],
        out_shape=[jax.ShapeDtypeStruct((m, nh * dv), BF16),
                   jax.ShapeDtypeStruct((depth, b, nh, dk, dv), F32)],
        prev=(None, s_prev),
        sem=("arbitrary", "arbitrary"))


def _ret_sample_kernel(gc_ref, q_ref, k_ref, v_ref, g_ref, dm_ref, xi_ref, zt_ref,
                       gn_ref, s0_ref, o_ref, s_ref, *, nh, dk, dv, bb, ts):
    q = q_ref[...].astype(F32)
    k = k_ref[...].astype(F32)
    v = v_ref[...].astype(F32)
    g = g_ref[...].astype(F32)
    seqs = []
    for i in range(bb):
        r0, r1 = i * ts, (i + 1) * ts
        outs = []
        for h in range(nh):
            out, s_new = _ret_head(
                q[r0:r1, h * dk:(h + 1) * dk].astype(BF16), k[r0:r1, h * dk:(h + 1) * dk].astype(BF16),
                v[r0:r1, h * dv:(h + 1) * dv].astype(BF16), g[r0:r1, h * dv:(h + 1) * dv],
                dm_ref[h], xi_ref[h], zt_ref[h], gn_ref[:, h * dv:(h + 1) * dv],
                s0_ref[i, h], gc_ref[h])
            s_ref[i, h] = s_new
            outs.append(out.astype(F32))
        seqs.append(jnp.concatenate(outs, axis=-1))
    o_ref[...] = jnp.concatenate(seqs, axis=0).astype(o_ref.dtype)


def _ret_sample(proj, gn3, s0_all, l, row0, bs, ts, nh, dk, dv, ya_prev, s_prev):
    bb = _pick(bs, (4, 2, 1))
    rows = bb * ts
    w = nh * dk
    depth = s0_all.shape[0]
    dmask, xi_t, zeta_t, g_c = _decay_tables(ts, nh, dk, dv)
    assert row0 % rows == 0
    rb0 = row0 // rows
    full3 = lambda i: (0, 0, 0)
    return _pcall(
        functools.partial(_ret_sample_kernel, nh=nh, dk=dk, dv=dv, bb=bb, ts=ts), name="ret_sample",
        grid=(bs // bb,),
        in_specs=[pl.BlockSpec(memory_space=pltpu.SMEM),
                  pl.BlockSpec((rows, w), lambda i: (rb0 + i, 0)),
                  pl.BlockSpec((rows, w), lambda i: (rb0 + i, 1)),
                  pl.BlockSpec((rows, w), lambda i: (rb0 + i, 2)),
                  pl.BlockSpec((rows, w), lambda i: (rb0 + i, 3)),
                  pl.BlockSpec((nh, ts, ts), full3),
                  pl.BlockSpec((nh, ts, dv), full3),
                  pl.BlockSpec((nh, ts, dk), full3),
                  _vspec(l, nh * dv),
                  pl.BlockSpec((None, bb, nh, dk, dv), lambda i: (l, i, 0, 0, 0))],
        args=(g_c, proj, proj, proj, proj, dmask, xi_t, zeta_t, gn3, s0_all),
        out_specs=[pl.BlockSpec((rows, nh * dv), lambda i: (rb0 + i, 0)),
                   pl.BlockSpec((None, bb, nh, dk, dv), lambda i: (l, i, 0, 0, 0))],
        out_shape=[jax.ShapeDtypeStruct(ya_prev.shape, BF16),
                   jax.ShapeDtypeStruct((depth, bs, nh, dk, dv), F32)],
        prev=(ya_prev, s_prev),
        sem=("arbitrary",))


def _lru_gates(uc, wr_ref, br_ref, wi_ref, bi_ref, lam_ref):
    nb, lb, _ = wr_ref.shape
    ucb = uc.astype(BF16)
    rl, il = [], []
    for n in range(nb):
        xb = ucb[:, n * lb:(n + 1) * lb]
        rl.append(jnp.dot(xb, wr_ref[n].astype(BF16), preferred_element_type=F32))
        il.append(jnp.dot(xb, wi_ref[n].astype(BF16), preferred_element_type=F32))
    r = jax.nn.sigmoid(jnp.concatenate(rl, axis=-1) + br_ref[...])
    i = jax.nn.sigmoid(jnp.concatenate(il, axis=-1) + bi_ref[...])
    log_a = -LRU_C * r * jax.nn.softplus(-lam_ref[...])
    a = jnp.exp(log_a)
    z = -jnp.tanh(log_a) * (a * a + 1.0)
    bx = jnp.where(z == 0.0, 0.0, z * lax.rsqrt(z)) * (i * uc)
    return a, bx


def _blocks(x):
    return [x[lo:lo + SUBLANES, :] for lo in range(0, x.shape[0], SUBLANES)]


def _conv_blocks(u_blocks, prev_blocks, cw_ref, cb_ref, rows):
    ncw = cw_ref.shape[0]
    rolled = {}

    def rot(x, s):
        key = (id(x), s)
        if key not in rolled:
            rolled[key] = pltpu.roll(x, s, axis=0)
        return rolled[key]

    out = []
    for ub, pb in zip(u_blocks, prev_blocks):
        uc = cb_ref[...]
        for j in range(ncw):
            s = ncw - 1 - j
            term = ub if s == 0 else jnp.where(rows >= s, rot(ub, s), rot(pb, s))
            uc = uc + term * cw_ref[j:j + 1, :]
        out.append(uc)
    return jnp.concatenate(out, axis=0)


def _scan_block(a, bx, rows):
    d = 1
    while d < SUBLANES:
        m = rows >= d
        bx = bx + a * jnp.where(m, pltpu.roll(bx, d, axis=0), 0.0)
        a = a * jnp.where(m, pltpu.roll(a, d, axis=0), 1.0)
        d *= 2
    return a, bx


def _lru_prompt_kernel(ux_ref, uy_ref, cw_ref, cb_ref, wr_ref, br_ref, wi_ref, bi_ref, lam_ref,
                       o_ref, hl_ref, ul_ref, hc_ref):
    _lru_prompt_init(ul_ref, hc_ref)
    _lru_prompt_main(ux_ref, uy_ref, cw_ref, cb_ref, wr_ref, br_ref, wi_ref, bi_ref, lam_ref,
                     o_ref, hl_ref, ul_ref, hc_ref)


def _lru_prompt_init(ul_ref, hc_ref):
    @pl.when(pl.program_id(2) == 0)
    def _():
        hc_ref[...] = jnp.zeros_like(hc_ref)
        ul_ref[...] = jnp.zeros_like(ul_ref)


def _lru_prompt_main(ux_ref, uy_ref, cw_ref, cb_ref, wr_ref, br_ref, wi_ref, bi_ref, lam_ref,
                     o_ref, hl_ref, ul_ref, hc_ref):
    r, cw = ux_ref.shape
    rows = lax.broadcasted_iota(jnp.int32, (SUBLANES, cw), 0)
    ub = _blocks(ux_ref[...].astype(F32))
    uc = _conv_blocks(ub, [ul_ref[...]] + ub[:-1], cw_ref, cb_ref, rows)
    ul_ref[...] = ub[-1]
    a, bx = _lru_gates(uc, wr_ref, br_ref, wi_ref, bi_ref, lam_ref)
    carry = hc_ref[...]
    hs = []
    for ab, bb in zip(_blocks(a), _blocks(bx)):
        ab, bb = _scan_block(ab, bb, rows)
        hb = ab * carry + bb
        hs.append(hb)
        carry = jnp.broadcast_to(hb[SUBLANES - 1:SUBLANES, :], (SUBLANES, cw))
    hc_ref[...] = carry
    hl_ref[...] = hs[-1]
    h = jnp.concatenate(hs, axis=0)
    o_ref[...] = (h * uy_ref[...].astype(F32)).astype(o_ref.dtype)


def _lru_sample_kernel(ux_ref, uy_ref, cw_ref, cb_ref, wr_ref, br_ref, wi_ref, bi_ref, lam_ref,
                       h0_ref, buf_ref, o_ref, hl_ref, us_ref):
    r, cw = ux_ref.shape
    rows = lax.broadcasted_iota(jnp.int32, (SUBLANES, cw), 0)
    u = ux_ref[...].astype(F32)
    us_ref[...] = u
    uc = _conv_blocks(_blocks(u), _blocks(buf_ref[...]), cw_ref, cb_ref, rows)
    a, bx = _lru_gates(uc, wr_ref, br_ref, wi_ref, bi_ref, lam_ref)
    hs = []
    for ab, bb, h0 in zip(_blocks(a), _blocks(bx), _blocks(h0_ref[...])):
        ab, bb = _scan_block(ab, bb, rows)
        hs.append(ab * h0 + bb)
    h = jnp.concatenate(hs, axis=0)
    hl_ref[...] = h
    o_ref[...] = (h * uy_ref[...].astype(F32)).astype(o_ref.dtype)


def _lru_specs(l, r, cw, ncw, lb, ux_blk, uy_blk, row_of, cb_of):
    nbc = cw // lb
    vec = pl.BlockSpec((None, 1, cw), lambda *g: (l, 0, cb_of(*g)))
    gate = pl.BlockSpec((None, nbc, lb, lb), lambda *g: (l, cb_of(*g), 0, 0))
    return [pl.BlockSpec((r, cw), lambda *g: (row_of(*g), ux_blk + cb_of(*g))),
            pl.BlockSpec((r, cw), lambda *g: (row_of(*g), uy_blk + cb_of(*g))),
            pl.BlockSpec((None, ncw, cw), lambda *g: (l, 0, cb_of(*g))),
            vec, gate, vec, gate, vec, vec]


def _lru_prompt(proj, ux_col, l, lru_w, b, t):
    conv_w, conv_b3, w_r, b_r3, w_i, b_i3, lam3 = lru_w
    width = lam3.shape[-1]
    lb = w_r.shape[-1]
    ncw = conv_w.shape[1]
    m = proj.shape[0]
    cw = _pick(width, (1024, 512, 256, 128))
    r = _pick(t, (256, 128, 64, 32, 16, 8))
    nt = t // r
    specs = _lru_specs(l, r, cw, ncw, lb, ux_col // cw, (ux_col + width) // cw,
                       lambda bi, cb, ti: bi * nt + ti, lambda bi, cb, ti: cb)
    last8 = pl.BlockSpec((None, SUBLANES, cw), lambda bi, cb, ti: (bi, 0, cb))
    return _pcall(
        _lru_prompt_kernel, name="lru_prompt", grid=(b, width // cw, nt),
        in_specs=specs,
        args=(proj, proj, conv_w, conv_b3, w_r, b_r3, w_i, b_i3, lam3),
        out_specs=[pl.BlockSpec((r, cw), lambda bi, cb, ti: (bi * nt + ti, cb)), last8, last8],
        out_shape=[jax.ShapeDtypeStruct((m, width), BF16),
                   jax.ShapeDtypeStruct((b, SUBLANES, width), F32),
                   jax.ShapeDtypeStruct((b, SUBLANES, width), F32)],
        scratch=[pltpu.VMEM((SUBLANES, cw), F32)],
        sem=("arbitrary", "arbitrary", "arbitrary"))


def _lru_ret_kernel(*refs, nh, dk, dv, bb, ts):
    lru_in, ret_in = refs[:9], refs[9:19]
    yb_ref, hl_ref, ul_ref, ya_ref, s_ref, hc_ref = refs[19:]
    _lru_prompt_init(ul_ref, hc_ref)
    _ret_sample_kernel(*ret_in, ya_ref, s_ref, nh=nh, dk=dk, dv=dv, bb=bb, ts=ts)
    _lru_prompt_main(*lru_in, yb_ref, hl_ref, ul_ref, hc_ref)


def _fused_seqs_per_step(width, t, b, bs, ts):
    cw = _pick(width, (1024, 512, 256, 128))
    r = _pick(t, (256, 128, 64, 32, 16, 8))
    steps = b * (width // cw) * (t // r)
    if bs % steps:
        return None
    bb = bs // steps
    return bb if bb <= 4 and (bb * ts) % 16 == 0 else None


def _lru_prompt_ret_sample(proj, ux_col, l, lru_w, b, t, gn3, s0_all, row0, bs, ts, nh, dk, dv,
                           ya_prev, s_prev):
    conv_w, conv_b3, w_r, b_r3, w_i, b_i3, lam3 = lru_w
    width = lam3.shape[-1]
    lb = w_r.shape[-1]
    ncw = conv_w.shape[1]
    m = proj.shape[0]
    depth = s0_all.shape[0]
    cw = _pick(width, (1024, 512, 256, 128))
    r = _pick(t, (256, 128, 64, 32, 16, 8))
    nt, ncb = t // r, width // cw
    bb = _fused_seqs_per_step(width, t, b, bs, ts)
    rows = bb * ts
    w = nh * dk
    assert row0 % rows == 0
    rb0 = row0 // rows
    lin = lambda bi, cb, ti: (bi * ncb + cb) * nt + ti
    dmask, xi_t, zeta_t, g_c = _decay_tables(ts, nh, dk, dv)
    lru_specs = _lru_specs(l, r, cw, ncw, lb, ux_col // cw, (ux_col + width) // cw,
                           lambda bi, cb, ti: bi * nt + ti, lambda bi, cb, ti: cb)
    full3 = lambda *g: (0, 0, 0)
    qkvg = [pl.BlockSpec((rows, w), lambda *g, c=c: (rb0 + lin(*g), c)) for c in range(4)]
    state = pl.BlockSpec((None, bb, nh, dk, dv), lambda *g: (l, lin(*g), 0, 0, 0))
    ret_specs = ([pl.BlockSpec(memory_space=pltpu.SMEM)] + qkvg +
                 [pl.BlockSpec((nh, ts, ts), full3), pl.BlockSpec((nh, ts, dv), full3),
                  pl.BlockSpec((nh, ts, dk), full3), _vspec(l, nh * dv), state])
    last8 = pl.BlockSpec((None, SUBLANES, cw), lambda bi, cb, ti: (bi, 0, cb))
    return _pcall(
        functools.partial(_lru_ret_kernel, nh=nh, dk=dk, dv=dv, bb=bb, ts=ts), name="lru_prompt_ret_sample",
        grid=(b, ncb, nt),
        in_specs=lru_specs + ret_specs,
        args=(proj, proj, conv_w, conv_b3, w_r, b_r3, w_i, b_i3, lam3,
              g_c, proj, proj, proj, proj, dmask, xi_t, zeta_t, gn3, s0_all),
        out_specs=[pl.BlockSpec((r, cw), lambda bi, cb, ti: (bi * nt + ti, cb)), last8, last8,
                   pl.BlockSpec((rows, nh * dv), lambda *g: (rb0 + lin(*g), 0)), state],
        out_shape=[jax.ShapeDtypeStruct((m, width), BF16),
                   jax.ShapeDtypeStruct((b, SUBLANES, width), F32),
                   jax.ShapeDtypeStruct((b, SUBLANES, width), F32),
                   jax.ShapeDtypeStruct(ya_prev.shape, BF16),
                   jax.ShapeDtypeStruct((depth, bs, nh, dk, dv), F32)],
        prev=(None, None, None, ya_prev, s_prev),
        scratch=[pltpu.VMEM((SUBLANES, cw), F32)],
        sem=("arbitrary", "arbitrary", "arbitrary"))


def _lru_sample(proj, ux_col, l, lru_w, row0, h0_rep, buf_fr, yb_prev):
    conv_w, conv_b3, w_r, b_r3, w_i, b_i3, lam3 = lru_w
    width = lam3.shape[-1]
    lb = w_r.shape[-1]
    ncw = conv_w.shape[1]
    ms = h0_rep.shape[1]
    cw = _pick(width, (1024, 512, 256, 128))
    r = _pick(ms, (256, 128, 64, 32, 16, 8))
    assert row0 % r == 0
    rb0 = row0 // r
    specs = _lru_specs(l, r, cw, ncw, lb, ux_col // cw, (ux_col + width) // cw,
                       lambda ri, cb: rb0 + ri, lambda ri, cb: cb)
    st = pl.BlockSpec((None, r, cw), lambda ri, cb: (l, ri, cb))
    f32rows = pl.BlockSpec((r, cw), lambda ri, cb: (ri, cb))
    return _pcall(
        _lru_sample_kernel, name="lru_sample", grid=(ms // r, width // cw),
        in_specs=specs + [st, st],
        args=(proj, proj, conv_w, conv_b3, w_r, b_r3, w_i, b_i3, lam3, h0_rep, buf_fr),
        out_specs=[pl.BlockSpec((r, cw), lambda ri, cb: (rb0 + ri, cb)), f32rows, f32rows],
        out_shape=[jax.ShapeDtypeStruct(yb_prev.shape, BF16),
                   jax.ShapeDtypeStruct((ms, width), F32), jax.ShapeDtypeStruct((ms, width), F32)],
        prev=(yb_prev,),
        sem=("arbitrary", "arbitrary"))


def _pack_pair(hi, lo):
    hb = lax.bitcast_convert_type(hi.astype(jnp.bfloat16).astype(F32), jnp.uint32)
    lb = lax.bitcast_convert_type(lo.astype(jnp.bfloat16).astype(F32), jnp.uint32)
    return hb | (lb >> 16)


def _unpack_pair(p):
    hi = lax.bitcast_convert_type(p & jnp.uint32(0xFFFF0000), F32)
    lo = lax.bitcast_convert_type(p << 16, F32)
    return hi, lo


def _router_kernel(x_ref, g_ref, wr_ref, o_ref, hp_ref, *, ne):
    hf = _rms(x_ref[...], g_ref[...])
    half = hf.shape[1] // 2
    hp_ref[...] = _pack_pair(hf[:, :half], hf[:, half:])
    h = hf.astype(BF16)
    logits = jnp.dot(h, wr_ref[...].astype(BF16), preferred_element_type=F32)
    lane = lax.broadcasted_iota(jnp.int32, logits.shape, 1).astype(F32)
    neg = F32(-jnp.inf)
    big = F32(LANES)
    l1 = jnp.where(lane < ne, logits, neg)
    m1 = jnp.max(l1, axis=-1, keepdims=True)
    i1 = jnp.min(jnp.where(l1 == m1, lane, big), axis=-1, keepdims=True)
    l2 = jnp.where(lane == i1, neg, l1)
    m2 = jnp.max(l2, axis=-1, keepdims=True)
    i2 = jnp.min(jnp.where(l2 == m2, lane, big), axis=-1, keepdims=True)
    e2 = jnp.exp(m2 - m1)
    den = 1.0 + e2
    o_ref[...] = (jnp.where(lane == 0.0, i1, 0.0) + jnp.where(lane == 1.0, i2, 0.0)
                  + jnp.where(lane == 2.0, 1.0 / den, 0.0) + jnp.where(lane == 3.0, e2 / den, 0.0))


def _router(x, g3, l, wr):
    m, d = x.shape
    ne = wr.shape[-1]
    assert TOP_K == 2 and ne <= LANES
    wr_pad = jnp.pad(wr, ((0, 0), (0, LANES - ne)))
    tm = _pick(m, (512, 256, 128, 64, 32, 16, 8))
    return _pcall(
        functools.partial(_router_kernel, ne=ne), name="router", grid=(m // tm,),
        in_specs=[pl.BlockSpec((tm, d), lambda i: (i, 0)), _vspec(l, d),
                  pl.BlockSpec((d, LANES), lambda i: (0, 0))],
        args=(x, g3, wr_pad),
        out_specs=[pl.BlockSpec((tm, LANES), lambda i: (i, 0)),
                   pl.BlockSpec((tm, d // 2), lambda i: (i, 0))],
        out_shape=[jax.ShapeDtypeStruct((m, LANES), F32),
                   jax.ShapeDtypeStruct((m, d // 2), jnp.uint32)],
        sem=("arbitrary",))


def _dispatch_tables(rout, ne, tg):
    m = rout.shape[0]
    na = TOP_K * m
    eid = rout[:, :TOP_K].astype(jnp.int32).T.reshape(na)
    onehot = (eid[:, None] == jnp.arange(ne, dtype=jnp.int32)[None, :]).astype(jnp.int32)
    cnt = jnp.sum(onehot, axis=0)
    rank = jnp.sum((jnp.cumsum(onehot, axis=0) - onehot) * onehot, axis=1)
    padded = ((cnt + tg - 1) // tg) * tg
    gend = jnp.cumsum(padded)
    gstart = gend - padded
    rem = cnt % tg
    first = jnp.where(rem > 0, rem, tg)
    local = jnp.where(rank < first[eid], rank, rank - first[eid] + tg)
    pos = gstart[eid] + local
    p_rows = ((na + tg - 1) // tg + ne) * tg
    n_tiles = p_rows // tg
    tok = jnp.zeros((p_rows,), jnp.int32).at[pos].set(jnp.arange(na, dtype=jnp.int32) % m)
    n_used = (gend[-1] // tg).astype(jnp.int32)
    tstart = jnp.arange(n_tiles, dtype=jnp.int32) * tg
    tstart = jnp.minimum(tstart, (n_used - 1) * tg)
    te = jnp.minimum(jnp.searchsorted(gend, tstart, side="right"), ne - 1).astype(jnp.int32)
    tidx = jnp.arange(n_tiles, dtype=jnp.int32)
    n_valid = jnp.where(tidx * tg == gstart[te], first[te], tg)
    n_valid = jnp.where(tidx < n_used, n_valid, 0).astype(jnp.int32)
    return pos, tok, te, n_used.reshape(1), n_valid, p_rows


DMA_UNROLL = 8


def _row_gather(src_hbm, dst, idx_of, sem):
    rows = dst.shape[0]
    assert rows % DMA_UNROLL == 0

    def row_copy(r, t):
        return pltpu.make_async_copy(src_hbm.at[pl.ds(t, 1), :], dst.at[pl.ds(r, 1), :], sem)

    def issue(c, carry):
        for u in range(DMA_UNROLL):
            r = c * DMA_UNROLL + u
            row_copy(r, idx_of(r)).start()
        return carry

    def drain(c, carry):
        for u in range(DMA_UNROLL):
            row_copy(c * DMA_UNROLL + u, 0).wait()
        return carry

    start = lambda: lax.fori_loop(0, rows // DMA_UNROLL, issue, 0)
    wait = lambda: lax.fori_loop(0, rows // DMA_UNROLL, drain, 0)
    return start, wait


def _moe_gather_kernel(tok_ref, gv_ref, hp_ref, o_ref):
    tr, d = o_ref.shape
    half = d // 2
    gv = gv_ref[0, 0]
    group = 2 * SUBLANES

    @pl.when(gv < tr)
    def _():
        o_ref[...] = jnp.zeros_like(o_ref)

    def body(g, carry):
        r0 = pl.multiple_of(g * group, group)
        rows = [hp_ref[pl.ds(tok_ref[0, r0 + u], 1), :] for u in range(group)]
        hi, lo = _unpack_pair(jnp.concatenate(rows, axis=0))
        o_ref[pl.ds(r0, group), pl.ds(0, half)] = hi.astype(o_ref.dtype)
        o_ref[pl.ds(r0, group), pl.ds(half, half)] = lo.astype(o_ref.dtype)
        return carry

    lax.fori_loop(0, (gv + group - 1) // group, body, 0)


def _moe_gather(hp, tok, n_valid, tg, p_rows):
    m, half = hp.shape
    tr = _pick(tg, (GATHER_ROWS, 128, 64, 32, 16))
    nt = p_rows // tr
    per = tg // tr
    gidx = jnp.arange(nt, dtype=jnp.int32)
    gv = jnp.clip(n_valid[gidx // per] - (gidx % per) * tr, 0, tr).astype(jnp.int32)
    return _pcall(
        _moe_gather_kernel, name="moe_gather", grid=(nt,),
        in_specs=[pl.BlockSpec((None, 1, tr), lambda i: (i, 0, 0), memory_space=pltpu.SMEM),
                  pl.BlockSpec((None, 1, 1), lambda i: (i, 0, 0), memory_space=pltpu.SMEM),
                  pl.BlockSpec((m, half), lambda i: (0, 0), pipeline_mode=pl.Buffered(1))],
        args=(tok.reshape(nt, 1, tr), gv.reshape(nt, 1, 1), hp),
        out_specs=[pl.BlockSpec((tr, 2 * half), lambda i: (i, 0))],
        out_shape=[jax.ShapeDtypeStruct((p_rows, 2 * half), BF16)],
        sem=("arbitrary",))[0]


def _tile_changed(te_ref):
    ti = pl.program_id(1)
    return jnp.logical_or(ti == 0, te_ref[ti] != te_ref[jnp.maximum(ti - 1, 0)])


def _valid_row_chunks(nv, o_ref, chunk):
    tg = o_ref.shape[0]

    @pl.when(nv == tg)
    def _():
        _row_chunks(tg, chunk)

    @pl.when(nv < tg)
    def _():
        def maybe(rs):
            @pl.when(rs.start < nv)
            def _():
                chunk(rs)

            @pl.when(rs.start >= nv)
            def _():
                o_ref[rs, :] = jnp.zeros((rs.stop - rs.start, o_ref.shape[1]), o_ref.dtype)

        _row_chunks(tg, maybe)


def _gmm_swiglu_kernel(te_ref, nu_ref, nv_ref, x_ref, wg_ref, wu_ref, o_ref, wgb_ref, wub_ref):
    @pl.when(_tile_changed(te_ref))
    def _():
        _cast_w(wg_ref, wgb_ref)
        _cast_w(wu_ref, wub_ref)

    def chunk(rs):
        x = x_ref[rs, :]
        g = jnp.dot(x, wgb_ref[...], preferred_element_type=F32)
        u = jnp.dot(x, wub_ref[...], preferred_element_type=F32)
        o_ref[rs, :] = (jax.nn.silu(g) * u).astype(o_ref.dtype)

    _valid_row_chunks(nv_ref[pl.program_id(1)], o_ref, chunk)


def _gmm_down_kernel(te_ref, nu_ref, nv_ref, x_ref, w_ref, o_ref, wb_ref):
    @pl.when(_tile_changed(te_ref))
    def _():
        _cast_w(w_ref, wb_ref)

    hw = o_ref.shape[1]

    def chunk(rs):
        y = jnp.dot(x_ref[rs, :], wb_ref[...], preferred_element_type=F32)
        o_ref[rs, :] = _pack_pair(y[:, :hw], y[:, hw:])

    _valid_row_chunks(nv_ref[pl.program_id(1)], o_ref, chunk)


def _gmm_specs(jm, tg, k, tn, j0=0):
    xrow = lambda j, ti, te, nu, nv: (jnp.minimum(ti, nu[0] - 1), 0)
    wsp = pl.BlockSpec((None, None, k, tn), lambda j, ti, te, nu, nv: (jm, te[ti], 0, j0 + j))
    return pl.BlockSpec((tg, k), xrow), wsp, pl.BlockSpec((tg, tn), lambda j, ti, te, nu, nv: (ti, j0 + j))


def _gmm_swiglu(xg, wg, wu, jm, te, n_used, n_valid, tg, tns):
    p, k = xg.shape
    n = wg.shape[-1]
    assert sum(tn * cnt for tn, cnt in tns) == n
    out, col = None, 0
    for tn, cnt in tns:
        assert col % tn == 0
        xs, ws, os_ = _gmm_specs(jm, tg, k, tn, col // tn)
        out = _pcall(
            _gmm_swiglu_kernel, name="gmm_swiglu", grid=(cnt, p // tg), num_scalar_prefetch=3,
            in_specs=[None, None, None, xs, ws, ws], args=(te, n_used, n_valid, xg, wg, wu),
            out_specs=[os_], out_shape=[jax.ShapeDtypeStruct((p, n), BF16)],
            scratch=[pltpu.VMEM((k, tn), BF16), pltpu.VMEM((k, tn), BF16)],
            prev=(out,), sem=("arbitrary", "arbitrary"))[0]
        col += tn * cnt
    return out


def _col_tiling(n, widths):
    out, col = [], 0
    for wdt in widths:
        cnt = (n - col) // wdt
        if cnt:
            out.append((wdt, cnt))
            col += cnt * wdt
    assert col == n
    return tuple(out)


def _gmm_down(ug, wd, jm, te, n_used, n_valid, tg, tn):
    p, k = ug.shape
    n = wd.shape[-1]
    xs, ws, _ = _gmm_specs(jm, tg, k, tn)
    return _pcall(
        _gmm_down_kernel, name="gmm_down", grid=(n // tn, p // tg), num_scalar_prefetch=3,
        in_specs=[None, None, None, xs, ws], args=(te, n_used, n_valid, ug, wd),
        out_specs=[pl.BlockSpec((tg, tn // 2), lambda j, ti, te, nu, nv: (ti, j))],
        out_shape=[jax.ShapeDtypeStruct((p, n // 2), jnp.uint32)],
        scratch=[pltpu.VMEM((k, tn), BF16)],
        sem=("arbitrary", "arbitrary"))[0]


def _moe_combine_kernel(pos_ref, nxt_ref, x_ref, rw_ref, g_ref, yg_hbm, *rest, tn, np_tiles):
    outs, (buf, sem) = rest[:-2], rest[-2:]
    tc = x_ref.shape[0]
    i = pl.program_id(0)
    slot = lax.rem(i, 2)

    def gathers(idx_ref, s):
        return [_row_gather(yg_hbm, buf.at[s, k], lambda r, k=k: idx_ref[0, k * tc + r], sem.at[s])
                for k in range(TOP_K)]

    @pl.when(i == 0)
    def _():
        for start, _ in gathers(pos_ref, 0):
            start()

    @pl.when(i + 1 < pl.num_programs(0))
    def _():
        for start, _ in gathers(nxt_ref, 1 - slot):
            start()

    for _, wait in gathers(pos_ref, slot):
        wait()
    rows = buf[slot]

    def expand(p):
        hw = tn // 2
        parts = []
        for jj in range(p.shape[1] // hw):
            parts.extend(_unpack_pair(p[:, jj * hw:(jj + 1) * hw]))
        return jnp.concatenate(parts, axis=-1)

    rw = rw_ref[...]
    y = rw[:, TOP_K:TOP_K + 1] * expand(rows[0])
    for k in range(1, TOP_K):
        y = y + rw[:, TOP_K + k:TOP_K + k + 1] * expand(rows[k])
    xn = x_ref[...] + y
    if np_tiles is None:
        outs[0][...] = xn
    else:
        yn = _rms(xn, g_ref[...])

        @pl.when(i < np_tiles)
        def _():
            outs[0][...] = yn

        @pl.when(i >= np_tiles)
        def _():
            outs[1][...] = yn


def _moe_combine(x, rout, pos, ygp, tn, g3=None, mp=None):
    m, d = x.shape
    tc = _pick(m, (COMBINE_ROWS, 64, 32, 16, 8))
    nt = m // tc
    pos_t = pos.reshape(TOP_K, nt, tc).transpose(1, 0, 2).reshape(nt, 1, TOP_K * tc)
    if g3 is None:
        g3 = jnp.ones((1, 1, d), F32)
        np_tiles = None
        out_specs = [pl.BlockSpec((tc, d), lambda i: (i, 0))]
        out_shape = [jax.ShapeDtypeStruct((m, d), F32)]
    else:
        assert mp % tc == 0
        np_tiles = mp // tc
        out_specs = [pl.BlockSpec((tc, d), lambda i: (jnp.minimum(i, np_tiles - 1), 0)),
                     pl.BlockSpec((tc, d), lambda i: (jnp.maximum(i - np_tiles, 0), 0))]
        out_shape = [jax.ShapeDtypeStruct((mp, d), F32), jax.ShapeDtypeStruct((m - mp, d), F32)]
    return _pcall(
        functools.partial(_moe_combine_kernel, tn=tn, np_tiles=np_tiles), name="moe_combine", grid=(nt,),
        in_specs=[pl.BlockSpec((None, 1, TOP_K * tc), lambda i: (i, 0, 0), memory_space=pltpu.SMEM),
                  pl.BlockSpec((None, 1, TOP_K * tc), lambda i: (jnp.minimum(i + 1, nt - 1), 0, 0),
                               memory_space=pltpu.SMEM),
                  pl.BlockSpec((tc, d), lambda i: (i, 0)),
                  pl.BlockSpec((tc, LANES), lambda i: (i, 0)),
                  _vspec(0, d),
                  pl.BlockSpec(memory_space=pl.ANY)],
        args=(pos_t, pos_t, x, rout, g3, ygp),
        out_specs=out_specs, out_shape=out_shape,
        scratch=[pltpu.VMEM((2, TOP_K, tc, d // 2), jnp.uint32), pltpu.SemaphoreType.DMA((2,))],
        sem=("arbitrary",))


def kernel(x_prompt, x_sample, state_ret, state_lru, state_conv, norm_mix, w_in, ret_gn, w_ret_o, conv_w, conv_b, w_rgate, b_rgate, w_igate, b_igate, lru_lambda, w_lru_o, w_out, norm_ffn, ffn_w_gate, ffn_w_up, ffn_w_down, moe_router, moe_w_gate, moe_w_up, moe_w_down, norm_final):
    bp, tp, d = x_prompt.shape
    bs, ts, _ = x_sample.shape
    depth, _, nh, dk, dv = state_ret.shape
    width = state_lru.shape[-1]
    ncw = conv_w.shape[1]
    ne = moe_router.shape[-1]
    assert dk == dv and nh * dk == width == d and ts == SUBLANES and ncw - 1 <= SUBLANES
    mp, ms = bp * tp, bs * ts
    m = mp + ms
    g_col = 2 * nh * dk + nh * dv
    ux_col = g_col + nh * dv
    ga_col = ux_col + 2 * width

    tm = _pick(m, (1024, 512, 256, 128, 64, 32, 16, 8))
    tm_big = _pick(m, (1536, 1024, 512, 256, 128, 64, 32, 16, 8))
    tm_xl = _pick(m, (2304, 1536, 1024, 512, 256, 128, 64, 32, 16, 8))
    tm_half = _pick(m, (512, 256, 128, 64, 32, 16, 8))
    tn_of = lambda n: _pick(n, (1024, 512, 256, 128))

    vec3 = lambda a: a.reshape(a.shape[0], 1, a.shape[-1])
    norm_mix3, norm_ffn3, ret_gn3 = vec3(norm_mix), vec3(norm_ffn), vec3(ret_gn)
    norm_final3 = norm_final.reshape(1, 1, d)
    lru_w = (conv_w, vec3(conv_b), w_rgate, vec3(b_rgate), w_igate, vec3(b_igate), vec3(lru_lambda))
    h0_rep = jnp.repeat(state_lru, ts, axis=1)
    buf_fr = jnp.pad(state_conv, ((0, 0), (0, 0), (ts - (ncw - 1), 0), (0, 0))).reshape(depth, ms, width)

    cos_p, sin_p = _rope_tables(tp, dk // 2, 0)
    cos_s, sin_s = _rope_tables(ts, dk // 2, PAST_LEN)
    cos = jnp.concatenate([jnp.tile(cos_p, (bp, 1)), jnp.tile(cos_s, (bs, 1))], axis=0)
    sin = jnp.concatenate([jnp.tile(sin_p, (bp, 1)), jnp.tile(sin_s, (bs, 1))], axis=0)
    tn_in = _pick(nh * dk, (1024, 512, 256))

    x = (x_prompt.reshape(mp, d), x_sample.reshape(ms, d))
    if mp % tm_half or ms % tm_half:
        x = jnp.concatenate(x, axis=0)
    ret_p = ret_s = y_p = y_s = None
    lrus_p, lrus_s, convs_p, convs_s = [], [], [], []
    for l in range(depth):
        if isinstance(x, tuple):
            h = _rmsnorm(x[0], norm_mix3, l, BF16, out_rows=m)
            h = _rmsnorm(x[1], norm_mix3, l, BF16, out_row0=mp, out_rows=m, prev=h)
        else:
            h = _rmsnorm(x, norm_mix3, l, BF16)
        acts = (2 * nh * dk, dk, (g_col, ux_col), (ux_col + width, ga_col))
        proj = _mm_inproj(h, w_in, (l,), cos, sin, *acts, tm_big, tn_in, cols=(0, 2 * nh * dk))
        proj = _mm_inproj(h, w_in, (l,), cos, sin, *acts, tm_xl, tn_in,
                          cols=(2 * nh * dk, w_in.shape[-1]), prev=proj)

        ya, ret_p = _ret_prompt(proj, ret_gn3, l, depth, bp, tp, nh, dk, dv, ret_p)
        if _fused_seqs_per_step(width, tp, bp, bs, ts) is None:
            ya, ret_s = _ret_sample(proj, ret_gn3, state_ret, l, mp, bs, ts, nh, dk, dv, ya, ret_s)
            yb, hl_p, ul_p = _lru_prompt(proj, ux_col, l, lru_w, bp, tp)
        else:
            yb, hl_p, ul_p, ya, ret_s = _lru_prompt_ret_sample(
                proj, ux_col, l, lru_w, bp, tp, ret_gn3, state_ret, mp, bs, ts, nh, dk, dv, ya, ret_s)
        yb, hl_s, us_s = _lru_sample(proj, ux_col, l, lru_w, mp, h0_rep, buf_fr, yb)
        lrus_p.append(hl_p[:, -1])
        lrus_s.append(hl_s.reshape(bs, ts, width)[:, -1])
        convs_p.append(ul_p[:, SUBLANES - (ncw - 1):])
        convs_s.append(us_s.reshape(bs, ts, width)[:, ts - (ncw - 1):])

        z = _mm_merge(ya, yb, w_ret_o, w_lru_o, (l,), proj, ga_col, ga_col + d, tm, tn_of(d) // 2)
        x = _mm_resid(z, w_out, (l,), x, tm_half if isinstance(x, tuple) else tm, tn_of(d))
        j = l // 2
        if l % 2 == 0:
            h2 = _rmsnorm(x, norm_ffn3, l, BF16)
            ff = ffn_w_gate.shape[-1]
            u = _mm_swiglu(h2, ffn_w_gate, ffn_w_up, (j,), tm_big, _pick(ff, (512, 256, 128)))
            x = _mm_resid(u, ffn_w_down, (j,), x, tm_half, _pick(d, (512, 256, 128)))
        else:
            fe = moe_w_gate.shape[-1]
            tg = _pick(TOP_K * m, (MOE_TILE, 256, 128, 64, 32, 16))
            tn_d = _pick(d, (1024, 512, 256))
            rout, hp = _router(x, norm_ffn3, l, moe_router[j])
            pos, tok, te, n_used, n_valid, p_rows = _dispatch_tables(rout, ne, tg)
            xg = _moe_gather(hp, tok, n_valid, tg, p_rows)
            ug = _gmm_swiglu(xg, moe_w_gate, moe_w_up, j, te, n_used, n_valid, tg,
                             _col_tiling(fe, (1024, 512, 256, 128)))
            ygp = _gmm_down(ug, moe_w_down, j, te, n_used, n_valid, tg, tn_d)
            if l == depth - 1:
                y_p, y_s = _moe_combine(x, rout, pos, ygp, tn_d, norm_final3, mp)
            else:
                x = _moe_combine(x, rout, pos, ygp, tn_d)[0]
    if y_p is None:
        y_p = _rmsnorm(x, norm_final3, 0, F32, 0, mp)
        y_s = _rmsnorm(x, norm_final3, 0, F32, mp, ms)
    return (y_p.reshape(bp, tp, d), y_s.reshape(bs, ts, d),
            ret_p, jnp.stack(lrus_p), jnp.stack(convs_p),
            ret_s, jnp.stack(lrus_s), jnp.stack(convs_s))
```

```python
import functools

import jax
import jax.numpy as jnp
from jax import lax
from jax.experimental import pallas as pl
from jax.experimental.pallas import tpu as pltpu

F32 = jnp.float32
BF16 = jnp.bfloat16

ROPE_BASE = 10000.0
LRU_C = 8.0
EPS = 1e-6
RET_CHUNK = 128
PAST_LEN = 16384
TOP_K = 2
SUBLANES = 8
LANES = 128
VMEM_LIMIT = 56 * 1024 * 1024
MOE_TILE = 1024
GATHER_ROWS = 1024
COMBINE_ROWS = 512


def _pick(n, prefs):
    for p in prefs:
        if n % p == 0:
            return p
    return n


def _params(*sem):
    return pltpu.CompilerParams(dimension_semantics=sem, vmem_limit_bytes=VMEM_LIMIT)


def _pcall(body, *, name, grid, in_specs, args, out_specs, out_shape, sem, scratch=(), prev=(),
           num_scalar_prefetch=0):
    n_in = len(args)
    prev = tuple(prev) + (None,) * (len(out_shape) - len(prev))
    extra = [(oi, p) for oi, p in enumerate(prev) if p is not None]
    aliases = {n_in + e: oi for e, (oi, _) in enumerate(extra)}
    if extra:
        inner = body

        def body(*refs):
            return inner(*refs[:n_in], *refs[n_in + len(extra):])

    in_specs = list(in_specs) + [pl.BlockSpec(memory_space=pl.ANY)] * len(extra)
    if num_scalar_prefetch:
        grid_spec = pltpu.PrefetchScalarGridSpec(
            num_scalar_prefetch=num_scalar_prefetch, grid=grid, in_specs=in_specs[num_scalar_prefetch:],
            out_specs=tuple(out_specs), scratch_shapes=list(scratch))
        call = pl.pallas_call(body, grid_spec=grid_spec, out_shape=tuple(out_shape),
                              input_output_aliases=aliases, compiler_params=_params(*sem), name=name)
    else:
        call = pl.pallas_call(body, grid=grid, in_specs=in_specs, out_specs=tuple(out_specs),
                              out_shape=tuple(out_shape), scratch_shapes=list(scratch),
                              input_output_aliases=aliases, compiler_params=_params(*sem), name=name)
    return call(*args, *[p for _, p in extra])


def _wspec(pre, k, tn):
    return pl.BlockSpec((None,) * len(pre) + (k, tn), lambda j, i: pre + (0, j))


def _vspec(l, n):
    return pl.BlockSpec((None, 1, n), lambda *_: (l, 0, 0))


def _rms(x, g):
    return (x * lax.rsqrt(jnp.mean(x * x, axis=-1, keepdims=True) + EPS)) * g


def _rmsnorm_kernel(x_ref, g_ref, o_ref):
    o_ref[...] = _rms(x_ref[...], g_ref[...]).astype(o_ref.dtype)


def _rmsnorm(x, g3, l, out_dtype, row0=0, rows=None, out_row0=0, out_rows=None, prev=None):
    d = x.shape[1]
    rows = x.shape[0] if rows is None else rows
    out_rows = rows if out_rows is None else out_rows
    tm = _pick(rows, (512, 256, 128, 64, 32, 16, 8))
    assert row0 % tm == 0 and out_row0 % tm == 0
    rb0, ob0 = row0 // tm, out_row0 // tm
    return _pcall(
        _rmsnorm_kernel, name="rmsnorm", grid=(rows // tm,),
        in_specs=[pl.BlockSpec((tm, d), lambda i: (rb0 + i, 0)), _vspec(l, d)],
        args=(x, g3),
        out_specs=[pl.BlockSpec((tm, d), lambda i: (ob0 + i, 0))],
        out_shape=[jax.ShapeDtypeStruct((out_rows, d), out_dtype)],
        prev=(prev,), sem=("arbitrary",))[0]


def _cast_w(w_ref, wb_ref):
    k = w_ref.shape[0]
    ck = _pick(k, (256, 128, 64, 32, 16))

    def body(c, carry):
        r = pl.multiple_of(c * ck, ck)
        wb_ref[pl.ds(r, ck), :] = w_ref[pl.ds(r, ck), :].astype(BF16)
        return carry

    lax.fori_loop(0, k // ck, body, 0)


def _mm_inproj_kernel(x_ref, w_ref, cos_ref, sin_ref, o_ref, wb_ref, *, nrot, dk, silu_tiles, gelu_tiles):
    j = pl.program_id(0)

    @pl.when(pl.program_id(1) == 0)
    def _():
        _cast_w(w_ref, wb_ref)

    tm, tn = o_ref.shape
    nchunk = 4 if tm % (4 * 16) == 0 else 1

    def row_chunks(epilogue):
        rc = tm // nchunk
        for c in range(nchunk):
            rs = slice(c * rc, (c + 1) * rc)
            epilogue(jnp.dot(x_ref[rs, :], wb_ref[...], preferred_element_type=F32), rs)

    def store(fn):
        def epilogue(acc, rs):
            o_ref[rs, :] = fn(acc).astype(o_ref.dtype)
        return epilogue

    in_tiles = lambda t: jnp.logical_and(j >= t[0], j < t[1])
    is_silu, is_gelu = in_tiles(silu_tiles), in_tiles(gelu_tiles)

    @pl.when(jnp.logical_and(j >= nrot, jnp.logical_not(jnp.logical_or(is_silu, is_gelu))))
    def _():
        row_chunks(store(lambda acc: acc))

    @pl.when(is_silu)
    def _():
        row_chunks(store(jax.nn.silu))

    @pl.when(is_gelu)
    def _():
        row_chunks(store(jax.nn.gelu))

    @pl.when(j < nrot)
    def _():
        scale = jnp.where(j >= nrot // 2, F32(dk ** -0.5), F32(1.0))
        half = dk // 2

        def rotary(acc, rs):
            cos = cos_ref[rs, :] * scale
            sin = sin_ref[rs, :] * scale
            for h in range(tn // dk):
                x1 = acc[:, h * dk:h * dk + half]
                x2 = acc[:, h * dk + half:(h + 1) * dk]
                o_ref[rs, h * dk:h * dk + half] = (x1 * cos - x2 * sin).astype(o_ref.dtype)
                o_ref[rs, h * dk + half:(h + 1) * dk] = (x1 * sin + x2 * cos).astype(o_ref.dtype)

        row_chunks(rotary)


def _mm_inproj(x, w, pre, cos, sin, qk_cols, dk, silu_cols, gelu_cols, tm, tn):
    m, k = x.shape
    n = w.shape[-1]
    half = cos.shape[1]
    assert qk_cols % (2 * tn) == 0 and tn % dk == 0
    assert all(c % tn == 0 for c in silu_cols + gelu_cols)
    tiles = lambda cols: (cols[0] // tn, cols[1] // tn)
    return _pcall(
        functools.partial(_mm_inproj_kernel, nrot=qk_cols // tn, dk=dk, silu_tiles=tiles(silu_cols),
                          gelu_tiles=tiles(gelu_cols)), name="mm_inproj",
        grid=(n // tn, m // tm),
        in_specs=[pl.BlockSpec((tm, k), lambda j, i: (i, 0)), _wspec(pre, k, tn),
                  pl.BlockSpec((tm, half), lambda j, i: (i, 0)),
                  pl.BlockSpec((tm, half), lambda j, i: (i, 0))],
        args=(x, w, cos, sin),
        out_specs=[pl.BlockSpec((tm, tn), lambda j, i: (i, j))],
        out_shape=[jax.ShapeDtypeStruct((m, n), BF16)],
        scratch=[pltpu.VMEM((k, tn), BF16)],
        sem=("arbitrary", "arbitrary"))[0]


def _mm_resid_kernel(x_ref, w_ref, r_ref, o_ref, wb_ref):
    @pl.when(pl.program_id(1) == 0)
    def _():
        _cast_w(w_ref, wb_ref)

    o_ref[...] = r_ref[...] + jnp.dot(x_ref[...], wb_ref[...], preferred_element_type=F32)


def _mm_resid2_kernel(x_ref, w_ref, ra_ref, rb_ref, o_ref, wb_ref, *, na):
    @pl.when(pl.program_id(1) == 0)
    def _():
        _cast_w(w_ref, wb_ref)

    y = jnp.dot(x_ref[...], wb_ref[...], preferred_element_type=F32)
    i = pl.program_id(1)

    @pl.when(i < na)
    def _():
        o_ref[...] = ra_ref[...] + y

    @pl.when(i >= na)
    def _():
        o_ref[...] = rb_ref[...] + y


def _mm_resid(x, w, pre, r, tm, tn):
    m, k = x.shape
    n = w.shape[-1]
    if isinstance(r, tuple):
        ra, rb = r
        assert ra.shape[0] % tm == 0 and rb.shape[0] % tm == 0 and ra.shape[0] + rb.shape[0] == m
        na = ra.shape[0] // tm
        body = functools.partial(_mm_resid2_kernel, na=na)
        r_specs = [pl.BlockSpec((tm, tn), lambda j, i: (jnp.minimum(i, na - 1), j)),
                   pl.BlockSpec((tm, tn), lambda j, i: (jnp.maximum(i - na, 0), j))]
        r_args = (ra, rb)
    else:
        body, r_specs, r_args = _mm_resid_kernel, [pl.BlockSpec((tm, tn), lambda j, i: (i, j))], (r,)
    return _pcall(
        body, name="mm_resid", grid=(n // tn, m // tm),
        in_specs=[pl.BlockSpec((tm, k), lambda j, i: (i, 0)), _wspec(pre, k, tn)] + r_specs,
        args=(x, w) + r_args,
        out_specs=[pl.BlockSpec((tm, tn), lambda j, i: (i, j))],
        out_shape=[jax.ShapeDtypeStruct((m, n), F32)],
        scratch=[pltpu.VMEM((k, tn), BF16)],
        sem=("arbitrary", "arbitrary"))[0]


def _row_chunks(rows, fn, nchunk=4):
    n = nchunk if rows % (nchunk * 16) == 0 else 1
    rc = rows // n
    for c in range(n):
        fn(slice(c * rc, (c + 1) * rc))


def _swiglu_rows(x_ref, wgb_ref, wub_ref, o_ref):
    def chunk(rs):
        x = x_ref[rs, :]
        g = jnp.dot(x, wgb_ref[...], preferred_element_type=F32)
        u = jnp.dot(x, wub_ref[...], preferred_element_type=F32)
        o_ref[rs, :] = (jax.nn.silu(g) * u).astype(o_ref.dtype)

    _row_chunks(o_ref.shape[0], chunk)


def _mm_swiglu_kernel(x_ref, wg_ref, wu_ref, o_ref, wgb_ref, wub_ref):
    @pl.when(pl.program_id(1) == 0)
    def _():
        _cast_w(wg_ref, wgb_ref)
        _cast_w(wu_ref, wub_ref)

    _swiglu_rows(x_ref, wgb_ref, wub_ref, o_ref)


def _mm_swiglu(x, wg, wu, pre, tm, tns):
    m, k = x.shape
    n = wg.shape[-1]
    assert sum(tn * cnt for tn, cnt in tns) == n
    out, col = None, 0
    for tn, cnt in tns:
        assert col % tn == 0
        j0 = col // tn
        wspec = pl.BlockSpec((None,) * len(pre) + (k, tn), lambda j, i, j0=j0: pre + (0, j0 + j))
        out = _pcall(
            _mm_swiglu_kernel, name="mm_swiglu", grid=(cnt, m // tm),
            in_specs=[pl.BlockSpec((tm, k), lambda j, i: (i, 0)), wspec, wspec],
            args=(x, wg, wu),
            out_specs=[pl.BlockSpec((tm, tn), lambda j, i, j0=j0: (i, j0 + j))],
            out_shape=[jax.ShapeDtypeStruct((m, n), BF16)],
            scratch=[pltpu.VMEM((k, tn), BF16), pltpu.VMEM((k, tn), BF16)],
            prev=(out,), sem=("arbitrary", "arbitrary"))[0]
        col += tn * cnt
    return out


def _mm_merge_kernel(a_ref, b_ref, wa_ref, wb_ref, ga_ref, gb_ref, o_ref, wab_ref, wbb_ref):
    @pl.when(pl.program_id(1) == 0)
    def _():
        _cast_w(wa_ref, wab_ref)
        _cast_w(wb_ref, wbb_ref)

    def chunk(rs):
        ya = jnp.dot(a_ref[rs, :], wab_ref[...], preferred_element_type=F32)
        yb = jnp.dot(b_ref[rs, :], wbb_ref[...], preferred_element_type=F32)
        ga = ga_ref[rs, :].astype(F32)
        gb = gb_ref[rs, :].astype(F32)
        o_ref[rs, :] = (jax.nn.sigmoid(ga) * ya + jax.nn.sigmoid(gb) * yb).astype(o_ref.dtype)

    _row_chunks(o_ref.shape[0], chunk)


def _mm_merge(a, b, wa, wb, pre, proj, ga_col, gb_col, tm, tn):
    m, k = a.shape
    n = wa.shape[-1]
    ga_blk, gb_blk = ga_col // tn, gb_col // tn
    return _pcall(
        _mm_merge_kernel, name="mm_merge", grid=(n // tn, m // tm),
        in_specs=[pl.BlockSpec((tm, k), lambda j, i: (i, 0)),
                  pl.BlockSpec((tm, k), lambda j, i: (i, 0)),
                  _wspec(pre, k, tn), _wspec(pre, k, tn),
                  pl.BlockSpec((tm, tn), lambda j, i: (i, ga_blk + j)),
                  pl.BlockSpec((tm, tn), lambda j, i: (i, gb_blk + j))],
        args=(a, b, wa, wb, proj, proj),
        out_specs=[pl.BlockSpec((tm, tn), lambda j, i: (i, j))],
        out_shape=[jax.ShapeDtypeStruct((m, n), BF16)],
        scratch=[pltpu.VMEM((k, tn), BF16), pltpu.VMEM((k, tn), BF16)],
        sem=("arbitrary", "arbitrary"))[0]


def _rope_kernel(inv_ref, cos_ref, sin_ref, *, pos0):
    t, half = cos_ref.shape
    pos = lax.broadcasted_iota(jnp.int32, (t, half), 0).astype(F32) + F32(pos0)
    ang = pos * inv_ref[...]
    cos_ref[...] = jnp.cos(ang)
    sin_ref[...] = jnp.sin(ang)


def _rope_tables(t, half, pos0):
    inv = ROPE_BASE ** (-jnp.arange(half, dtype=F32) / half)
    return pl.pallas_call(
        functools.partial(_rope_kernel, pos0=pos0),
        out_shape=(jax.ShapeDtypeStruct((t, half), F32), jax.ShapeDtypeStruct((t, half), F32)),
        name="rope_tables",
    )(inv.reshape(1, half))


def _decay_tables(c, h, dk, dv):
    log_g = jnp.log1p(-jnp.exp2(-5.0 - jnp.arange(h, dtype=F32)))
    idx = jnp.arange(c)
    rel = idx[:, None] - idx[None, :]
    dmask = jnp.where(rel[None] >= 0,
                      jnp.exp(jnp.maximum(rel, 0)[None].astype(F32) * log_g[:, None, None]), 0.0)
    xi = jnp.exp((idx + 1).astype(F32)[None, :] * log_g[:, None])
    zeta = jnp.exp((c - 1 - idx).astype(F32)[None, :] * log_g[:, None])
    g_c = jnp.exp(c * log_g)
    xi_t = jnp.broadcast_to(xi[:, :, None], (h, c, dv))
    zeta_t = jnp.broadcast_to(zeta[:, :, None], (h, c, dk))
    return dmask, xi_t, zeta_t, g_c


def _ret_head(qb, kb, v, g, dm, xi, zt, gn, s, gc):
    scores = lax.dot_general(qb, kb, (((1,), (1,)), ((), ())), preferred_element_type=F32)
    intra = jnp.dot((scores * dm).astype(BF16), v, preferred_element_type=F32)
    cross = jnp.dot(qb, s.astype(BF16), preferred_element_type=F32) * xi
    kz = (kb.astype(F32) * zt).astype(BF16)
    upd = lax.dot_general(kz, v, (((0,), (0,)), ((), ())), preferred_element_type=F32)
    s_new = gc * s + upd
    o = intra + cross
    mu = jnp.mean(o, axis=-1, keepdims=True)
    d = o - mu
    var = jnp.mean(d * d, axis=-1, keepdims=True)
    y = d * lax.rsqrt(var + EPS) * gn
    return (g * y).astype(BF16), s_new


def _ret_prompt_kernel(gc_ref, q_ref, k_ref, v_ref, g_ref, dm_ref, xi_ref, zt_ref,
                       gn_ref, o_ref, s_ref, *, nh, dk, dv):
    @pl.when(pl.program_id(1) == 0)
    def _():
        s_ref[...] = jnp.zeros_like(s_ref)

    c = dm_ref.shape[1]
    for h in range(nh):
        s = s_ref[h]
        for ci in range(q_ref.shape[0] // c):
            rs = slice(ci * c, (ci + 1) * c)
            out, s = _ret_head(
                q_ref[rs, h * dk:(h + 1) * dk], k_ref[rs, h * dk:(h + 1) * dk],
                v_ref[rs, h * dv:(h + 1) * dv], g_ref[rs, h * dv:(h + 1) * dv].astype(F32),
                dm_ref[h], xi_ref[h], zt_ref[h], gn_ref[:, h * dv:(h + 1) * dv], s, gc_ref[h])
            o_ref[rs, h * dv:(h + 1) * dv] = out
        s_ref[h] = s


def _ret_prompt(proj, gn3, l, depth, b, t, nh, dk, dv, s_prev):
    cl = RET_CHUNK if t % RET_CHUNK == 0 else t
    cps = _pick(t // cl, (8, 4, 2, 1))
    c = cl * cps
    nc = t // c
    w = nh * dk
    m = proj.shape[0]
    dmask, xi_t, zeta_t, g_c = _decay_tables(cl, nh, dk, dv)
    row = lambda bi, ci: bi * nc + ci
    full3 = lambda bi, ci: (0, 0, 0)
    return _pcall(
        functools.partial(_ret_prompt_kernel, nh=nh, dk=dk, dv=dv), name="ret_prompt", grid=(b, nc),
        in_specs=[pl.BlockSpec(memory_space=pltpu.SMEM),
                  pl.BlockSpec((c, w), lambda bi, ci: (row(bi, ci), 0)),
                  pl.BlockSpec((c, w), lambda bi, ci: (row(bi, ci), 1)),
                  pl.BlockSpec((c, w), lambda bi, ci: (row(bi, ci), 2)),
                  pl.BlockSpec((c, w), lambda bi, ci: (row(bi, ci), 3)),
                  pl.BlockSpec((nh, cl, cl), full3),
                  pl.BlockSpec((nh, cl, dv), full3),
                  pl.BlockSpec((nh, cl, dk), full3),
                  _vspec(l, nh * dv)],
        args=(g_c, proj, proj, proj, proj, dmask, xi_t, zeta_t, gn3),
        out_specs=[pl.BlockSpec((c, nh * dv), lambda bi, ci: (row(bi, ci), 0)),
                   pl.BlockSpec((None, None, nh, dk, dv), lambda bi, ci: (l, bi, 0, 0, 0))],
        out_shape=[jax.ShapeDtypeStruct((m, nh * dv), BF16),
                   jax.ShapeDtypeStruct((depth, b, nh, dk, dv), F32)],
        prev=(None, s_prev),
        sem=("arbitrary", "arbitrary"))


def _ret_sample_kernel(gc_ref, q_ref, k_ref, v_ref, g_ref, dm_ref, xi_ref, zt_ref,
                       gn_ref, s0_ref, o_ref, s_ref, *, nh, dk, dv, bb, ts):
    q = q_ref[...].astype(F32)
    k = k_ref[...].astype(F32)
    v = v_ref[...].astype(F32)
    g = g_ref[...].astype(F32)
    seqs = []
    for i in range(bb):
        r0, r1 = i * ts, (i + 1) * ts
        outs = []
        for h in range(nh):
            out, s_new = _ret_head(
                q[r0:r1, h * dk:(h + 1) * dk].astype(BF16), k[r0:r1, h * dk:(h + 1) * dk].astype(BF16),
                v[r0:r1, h * dv:(h + 1) * dv].astype(BF16), g[r0:r1, h * dv:(h + 1) * dv],
                dm_ref[h], xi_ref[h], zt_ref[h], gn_ref[:, h * dv:(h + 1) * dv],
                s0_ref[i, h], gc_ref[h])
            s_ref[i, h] = s_new
            outs.append(out.astype(F32))
        seqs.append(jnp.concatenate(outs, axis=-1))
    o_ref[...] = jnp.concatenate(seqs, axis=0).astype(o_ref.dtype)


def _ret_sample(proj, gn3, s0_all, l, row0, bs, ts, nh, dk, dv, ya_prev, s_prev):
    bb = _pick(bs, (4, 2, 1))
    rows = bb * ts
    w = nh * dk
    depth = s0_all.shape[0]
    dmask, xi_t, zeta_t, g_c = _decay_tables(ts, nh, dk, dv)
    assert row0 % rows == 0
    rb0 = row0 // rows
    full3 = lambda i: (0, 0, 0)
    return _pcall(
        functools.partial(_ret_sample_kernel, nh=nh, dk=dk, dv=dv, bb=bb, ts=ts), name="ret_sample",
        grid=(bs // bb,),
        in_specs=[pl.BlockSpec(memory_space=pltpu.SMEM),
                  pl.BlockSpec((rows, w), lambda i: (rb0 + i, 0)),
                  pl.BlockSpec((rows, w), lambda i: (rb0 + i, 1)),
                  pl.BlockSpec((rows, w), lambda i: (rb0 + i, 2)),
                  pl.BlockSpec((rows, w), lambda i: (rb0 + i, 3)),
                  pl.BlockSpec((nh, ts, ts), full3),
                  pl.BlockSpec((nh, ts, dv), full3),
                  pl.BlockSpec((nh, ts, dk), full3),
                  _vspec(l, nh * dv),
                  pl.BlockSpec((None, bb, nh, dk, dv), lambda i: (l, i, 0, 0, 0))],
        args=(g_c, proj, proj, proj, proj, dmask, xi_t, zeta_t, gn3, s0_all),
        out_specs=[pl.BlockSpec((rows, nh * dv), lambda i: (rb0 + i, 0)),
                   pl.BlockSpec((None, bb, nh, dk, dv), lambda i: (l, i, 0, 0, 0))],
        out_shape=[jax.ShapeDtypeStruct(ya_prev.shape, BF16),
                   jax.ShapeDtypeStruct((depth, bs, nh, dk, dv), F32)],
        prev=(ya_prev, s_prev),
        sem=("arbitrary",))


def _lru_gates(uc, wr_ref, br_ref, wi_ref, bi_ref, lam_ref):
    nb, lb, _ = wr_ref.shape
    ucb = uc.astype(BF16)
    rl, il = [], []
    for n in range(nb):
        xb = ucb[:, n * lb:(n + 1) * lb]
        rl.append(jnp.dot(xb, wr_ref[n].astype(BF16), preferred_element_type=F32))
        il.append(jnp.dot(xb, wi_ref[n].astype(BF16), preferred_element_type=F32))
    r = jax.nn.sigmoid(jnp.concatenate(rl, axis=-1) + br_ref[...])
    i = jax.nn.sigmoid(jnp.concatenate(il, axis=-1) + bi_ref[...])
    log_a = -LRU_C * r * jax.nn.softplus(-lam_ref[...])
    a = jnp.exp(log_a)
    z = -jnp.tanh(log_a) * (a * a + 1.0)
    bx = jnp.where(z == 0.0, 0.0, z * lax.rsqrt(z)) * (i * uc)
    return a, bx


def _blocks(x):
    return [x[lo:lo + SUBLANES, :] for lo in range(0, x.shape[0], SUBLANES)]


def _conv_blocks(u_blocks, prev_blocks, cw_ref, cb_ref, rows):
    ncw = cw_ref.shape[0]
    rolled = {}

    def rot(x, s):
        key = (id(x), s)
        if key not in rolled:
            rolled[key] = pltpu.roll(x, s, axis=0)
        return rolled[key]

    out = []
    for ub, pb in zip(u_blocks, prev_blocks):
        uc = cb_ref[...]
        for j in range(ncw):
            s = ncw - 1 - j
            term = ub if s == 0 else jnp.where(rows >= s, rot(ub, s), rot(pb, s))
            uc = uc + term * cw_ref[j:j + 1, :]
        out.append(uc)
    return jnp.concatenate(out, axis=0)


def _scan_block(a, bx, rows):
    d = 1
    while d < SUBLANES:
        m = rows >= d
        bx = bx + a * jnp.where(m, pltpu.roll(bx, d, axis=0), 0.0)
        a = a * jnp.where(m, pltpu.roll(a, d, axis=0), 1.0)
        d *= 2
    return a, bx


def _lru_prompt_kernel(ux_ref, uy_ref, cw_ref, cb_ref, wr_ref, br_ref, wi_ref, bi_ref, lam_ref,
                       o_ref, hl_ref, ul_ref, hc_ref):
    _lru_prompt_init(ul_ref, hc_ref)
    _lru_prompt_main(ux_ref, uy_ref, cw_ref, cb_ref, wr_ref, br_ref, wi_ref, bi_ref, lam_ref,
                     o_ref, hl_ref, ul_ref, hc_ref)


def _lru_prompt_init(ul_ref, hc_ref):
    @pl.when(pl.program_id(2) == 0)
    def _():
        hc_ref[...] = jnp.zeros_like(hc_ref)
        ul_ref[...] = jnp.zeros_like(ul_ref)


def _lru_prompt_main(ux_ref, uy_ref, cw_ref, cb_ref, wr_ref, br_ref, wi_ref, bi_ref, lam_ref,
                     o_ref, hl_ref, ul_ref, hc_ref):
    r, cw = ux_ref.shape
    rows = lax.broadcasted_iota(jnp.int32, (SUBLANES, cw), 0)
    ub = _blocks(ux_ref[...].astype(F32))
    uc = _conv_blocks(ub, [ul_ref[...]] + ub[:-1], cw_ref, cb_ref, rows)
    ul_ref[...] = ub[-1]
    a, bx = _lru_gates(uc, wr_ref, br_ref, wi_ref, bi_ref, lam_ref)
    carry = hc_ref[...]
    hs = []
    for ab, bb in zip(_blocks(a), _blocks(bx)):
        ab, bb = _scan_block(ab, bb, rows)
        hb = ab * carry + bb
        hs.append(hb)
        carry = jnp.broadcast_to(hb[SUBLANES - 1:SUBLANES, :], (SUBLANES, cw))
    hc_ref[...] = carry
    hl_ref[...] = hs[-1]
    h = jnp.concatenate(hs, axis=0)
    o_ref[...] = (h * uy_ref[...].astype(F32)).astype(o_ref.dtype)


def _lru_sample_kernel(ux_ref, uy_ref, cw_ref, cb_ref, wr_ref, br_ref, wi_ref, bi_ref, lam_ref,
                       h0_ref, buf_ref, o_ref, hl_ref, us_ref):
    r, cw = ux_ref.shape
    rows = lax.broadcasted_iota(jnp.int32, (SUBLANES, cw), 0)
    u = ux_ref[...].astype(F32)
    us_ref[...] = u
    uc = _conv_blocks(_blocks(u), _blocks(buf_ref[...]), cw_ref, cb_ref, rows)
    a, bx = _lru_gates(uc, wr_ref, br_ref, wi_ref, bi_ref, lam_ref)
    hs = []
    for ab, bb, h0 in zip(_blocks(a), _blocks(bx), _blocks(h0_ref[...])):
        ab, bb = _scan_block(ab, bb, rows)
        hs.append(ab * h0 + bb)
    h = jnp.concatenate(hs, axis=0)
    hl_ref[...] = h
    o_ref[...] = (h * uy_ref[...].astype(F32)).astype(o_ref.dtype)


def _lru_specs(l, r, cw, ncw, lb, ux_blk, uy_blk, row_of, cb_of):
    nbc = cw // lb
    vec = pl.BlockSpec((None, 1, cw), lambda *g: (l, 0, cb_of(*g)))
    gate = pl.BlockSpec((None, nbc, lb, lb), lambda *g: (l, cb_of(*g), 0, 0))
    return [pl.BlockSpec((r, cw), lambda *g: (row_of(*g), ux_blk + cb_of(*g))),
            pl.BlockSpec((r, cw), lambda *g: (row_of(*g), uy_blk + cb_of(*g))),
            pl.BlockSpec((None, ncw, cw), lambda *g: (l, 0, cb_of(*g))),
            vec, gate, vec, gate, vec, vec]


def _lru_prompt(proj, ux_col, l, lru_w, b, t):
    conv_w, conv_b3, w_r, b_r3, w_i, b_i3, lam3 = lru_w
    width = lam3.shape[-1]
    lb = w_r.shape[-1]
    ncw = conv_w.shape[1]
    m = proj.shape[0]
    cw = _pick(width, (1024, 512, 256, 128))
    r = _pick(t, (256, 128, 64, 32, 16, 8))
    nt = t // r
    specs = _lru_specs(l, r, cw, ncw, lb, ux_col // cw, (ux_col + width) // cw,
                       lambda bi, cb, ti: bi * nt + ti, lambda bi, cb, ti: cb)
    last8 = pl.BlockSpec((None, SUBLANES, cw), lambda bi, cb, ti: (bi, 0, cb))
    return _pcall(
        _lru_prompt_kernel, name="lru_prompt", grid=(b, width // cw, nt),
        in_specs=specs,
        args=(proj, proj, conv_w, conv_b3, w_r, b_r3, w_i, b_i3, lam3),
        out_specs=[pl.BlockSpec((r, cw), lambda bi, cb, ti: (bi * nt + ti, cb)), last8, last8],
        out_shape=[jax.ShapeDtypeStruct((m, width), BF16),
                   jax.ShapeDtypeStruct((b, SUBLANES, width), F32),
                   jax.ShapeDtypeStruct((b, SUBLANES, width), F32)],
        scratch=[pltpu.VMEM((SUBLANES, cw), F32)],
        sem=("arbitrary", "arbitrary", "arbitrary"))


def _lru_ret_kernel(*refs, nh, dk, dv, bb, ts):
    lru_in, ret_in = refs[:9], refs[9:19]
    yb_ref, hl_ref, ul_ref, ya_ref, s_ref, hc_ref = refs[19:]
    _lru_prompt_init(ul_ref, hc_ref)
    _ret_sample_kernel(*ret_in, ya_ref, s_ref, nh=nh, dk=dk, dv=dv, bb=bb, ts=ts)
    _lru_prompt_main(*lru_in, yb_ref, hl_ref, ul_ref, hc_ref)


def _fused_seqs_per_step(width, t, b, bs, ts):
    cw = _pick(width, (1024, 512, 256, 128))
    r = _pick(t, (256, 128, 64, 32, 16, 8))
    steps = b * (width // cw) * (t // r)
    if bs % steps:
        return None
    bb = bs // steps
    return bb if bb <= 4 and (bb * ts) % 16 == 0 else None


def _lru_prompt_ret_sample(proj, ux_col, l, lru_w, b, t, gn3, s0_all, row0, bs, ts, nh, dk, dv,
                           ya_prev, s_prev):
    conv_w, conv_b3, w_r, b_r3, w_i, b_i3, lam3 = lru_w
    width = lam3.shape[-1]
    lb = w_r.shape[-1]
    ncw = conv_w.shape[1]
    m = proj.shape[0]
    depth = s0_all.shape[0]
    cw = _pick(width, (1024, 512, 256, 128))
    r = _pick(t, (256, 128, 64, 32, 16, 8))
    nt, ncb = t // r, width // cw
    bb = _fused_seqs_per_step(width, t, b, bs, ts)
    rows = bb * ts
    w = nh * dk
    assert row0 % rows == 0
    rb0 = row0 // rows
    lin = lambda bi, cb, ti: (bi * ncb + cb) * nt + ti
    dmask, xi_t, zeta_t, g_c = _decay_tables(ts, nh, dk, dv)
    lru_specs = _lru_specs(l, r, cw, ncw, lb, ux_col // cw, (ux_col + width) // cw,
                           lambda bi, cb, ti: bi * nt + ti, lambda bi, cb, ti: cb)
    full3 = lambda *g: (0, 0, 0)
    qkvg = [pl.BlockSpec((rows, w), lambda *g, c=c: (rb0 + lin(*g), c)) for c in range(4)]
    state = pl.BlockSpec((None, bb, nh, dk, dv), lambda *g: (l, lin(*g), 0, 0, 0))
    ret_specs = ([pl.BlockSpec(memory_space=pltpu.SMEM)] + qkvg +
                 [pl.BlockSpec((nh, ts, ts), full3), pl.BlockSpec((nh, ts, dv), full3),
                  pl.BlockSpec((nh, ts, dk), full3), _vspec(l, nh * dv), state])
    last8 = pl.BlockSpec((None, SUBLANES, cw), lambda bi, cb, ti: (bi, 0, cb))
    return _pcall(
        functools.partial(_lru_ret_kernel, nh=nh, dk=dk, dv=dv, bb=bb, ts=ts), name="lru_prompt_ret_sample",
        grid=(b, ncb, nt),
        in_specs=lru_specs + ret_specs,
        args=(proj, proj, conv_w, conv_b3, w_r, b_r3, w_i, b_i3, lam3,
              g_c, proj, proj, proj, proj, dmask, xi_t, zeta_t, gn3, s0_all),
        out_specs=[pl.BlockSpec((r, cw), lambda bi, cb, ti: (bi * nt + ti, cb)), last8, last8,
                   pl.BlockSpec((rows, nh * dv), lambda *g: (rb0 + lin(*g), 0)), state],
        out_shape=[jax.ShapeDtypeStruct((m, width), BF16),
                   jax.ShapeDtypeStruct((b, SUBLANES, width), F32),
                   jax.ShapeDtypeStruct((b, SUBLANES, width), F32),
                   jax.ShapeDtypeStruct(ya_prev.shape, BF16),
                   jax.ShapeDtypeStruct((depth, bs, nh, dk, dv), F32)],
        prev=(None, None, None, ya_prev, s_prev),
        scratch=[pltpu.VMEM((SUBLANES, cw), F32)],
        sem=("arbitrary", "arbitrary", "arbitrary"))


def _lru_sample(proj, ux_col, l, lru_w, row0, h0_rep, buf_fr, yb_prev):
    conv_w, conv_b3, w_r, b_r3, w_i, b_i3, lam3 = lru_w
    width = lam3.shape[-1]
    lb = w_r.shape[-1]
    ncw = conv_w.shape[1]
    ms = h0_rep.shape[1]
    cw = _pick(width, (1024, 512, 256, 128))
    r = _pick(ms, (256, 128, 64, 32, 16, 8))
    assert row0 % r == 0
    rb0 = row0 // r
    specs = _lru_specs(l, r, cw, ncw, lb, ux_col // cw, (ux_col + width) // cw,
                       lambda ri, cb: rb0 + ri, lambda ri, cb: cb)
    st = pl.BlockSpec((None, r, cw), lambda ri, cb: (l, ri, cb))
    f32rows = pl.BlockSpec((r, cw), lambda ri, cb: (ri, cb))
    return _pcall(
        _lru_sample_kernel, name="lru_sample", grid=(ms // r, width // cw),
        in_specs=specs + [st, st],
        args=(proj, proj, conv_w, conv_b3, w_r, b_r3, w_i, b_i3, lam3, h0_rep, buf_fr),
        out_specs=[pl.BlockSpec((r, cw), lambda ri, cb: (rb0 + ri, cb)), f32rows, f32rows],
        out_shape=[jax.ShapeDtypeStruct(yb_prev.shape, BF16),
                   jax.ShapeDtypeStruct((ms, width), F32), jax.ShapeDtypeStruct((ms, width), F32)],
        prev=(yb_prev,),
        sem=("arbitrary", "arbitrary"))


def _pack_pair(hi, lo):
    hb = lax.bitcast_convert_type(hi.astype(jnp.bfloat16).astype(F32), jnp.uint32)
    lb = lax.bitcast_convert_type(lo.astype(jnp.bfloat16).astype(F32), jnp.uint32)
    return hb | (lb >> 16)


def _unpack_pair(p):
    hi = lax.bitcast_convert_type(p & jnp.uint32(0xFFFF0000), F32)
    lo = lax.bitcast_convert_type(p << 16, F32)
    return hi, lo


def _router_kernel(x_ref, g_ref, wr_ref, o_ref, hp_ref, *, ne):
    hf = _rms(x_ref[...], g_ref[...])
    half = hf.shape[1] // 2
    hp_ref[...] = _pack_pair(hf[:, :half], hf[:, half:])
    h = hf.astype(BF16)
    logits = jnp.dot(h, wr_ref[...].astype(BF16), preferred_element_type=F32)
    lane = lax.broadcasted_iota(jnp.int32, logits.shape, 1).astype(F32)
    neg = F32(-jnp.inf)
    big = F32(LANES)
    l1 = jnp.where(lane < ne, logits, neg)
    m1 = jnp.max(l1, axis=-1, keepdims=True)
    i1 = jnp.min(jnp.where(l1 == m1, lane, big), axis=-1, keepdims=True)
    l2 = jnp.where(lane == i1, neg, l1)
    m2 = jnp.max(l2, axis=-1, keepdims=True)
    i2 = jnp.min(jnp.where(l2 == m2, lane, big), axis=-1, keepdims=True)
    e2 = jnp.exp(m2 - m1)
    den = 1.0 + e2
    o_ref[...] = (jnp.where(lane == 0.0, i1, 0.0) + jnp.where(lane == 1.0, i2, 0.0)
                  + jnp.where(lane == 2.0, 1.0 / den, 0.0) + jnp.where(lane == 3.0, e2 / den, 0.0))


def _router(x, g3, l, wr):
    m, d = x.shape
    ne = wr.shape[-1]
    assert TOP_K == 2 and ne <= LANES
    wr_pad = jnp.pad(wr, ((0, 0), (0, LANES - ne)))
    tm = _pick(m, (512, 256, 128, 64, 32, 16, 8))
    return _pcall(
        functools.partial(_router_kernel, ne=ne), name="router", grid=(m // tm,),
        in_specs=[pl.BlockSpec((tm, d), lambda i: (i, 0)), _vspec(l, d),
                  pl.BlockSpec((d, LANES), lambda i: (0, 0))],
        args=(x, g3, wr_pad),
        out_specs=[pl.BlockSpec((tm, LANES), lambda i: (i, 0)),
                   pl.BlockSpec((tm, d // 2), lambda i: (i, 0))],
        out_shape=[jax.ShapeDtypeStruct((m, LANES), F32),
                   jax.ShapeDtypeStruct((m, d // 2), jnp.uint32)],
        sem=("arbitrary",))


def _dispatch_tables(rout, ne, tg):
    m = rout.shape[0]
    na = TOP_K * m
    eid = rout[:, :TOP_K].astype(jnp.int32).T.reshape(na)
    onehot = (eid[:, None] == jnp.arange(ne, dtype=jnp.int32)[None, :]).astype(jnp.int32)
    cnt = jnp.sum(onehot, axis=0)
    rank = jnp.sum((jnp.cumsum(onehot, axis=0) - onehot) * onehot, axis=1)
    padded = ((cnt + tg - 1) // tg) * tg
    gend = jnp.cumsum(padded)
    gstart = gend - padded
    rem = cnt % tg
    first = jnp.where(rem > 0, rem, tg)
    local = jnp.where(rank < first[eid], rank, rank - first[eid] + tg)
    pos = gstart[eid] + local
    p_rows = ((na + tg - 1) // tg + ne) * tg
    n_tiles = p_rows // tg
    tok = jnp.zeros((p_rows,), jnp.int32).at[pos].set(jnp.arange(na, dtype=jnp.int32) % m)
    n_used = (gend[-1] // tg).astype(jnp.int32)
    tstart = jnp.arange(n_tiles, dtype=jnp.int32) * tg
    tstart = jnp.minimum(tstart, (n_used - 1) * tg)
    te = jnp.minimum(jnp.searchsorted(gend, tstart, side="right"), ne - 1).astype(jnp.int32)
    tidx = jnp.arange(n_tiles, dtype=jnp.int32)
    n_valid = jnp.where(tidx * tg == gstart[te], first[te], tg)
    n_valid = jnp.where(tidx < n_used, n_valid, 0).astype(jnp.int32)
    return pos, tok, te, n_used.reshape(1), n_valid, p_rows


DMA_UNROLL = 8


def _row_gather(src_hbm, dst, idx_of, sem):
    rows = dst.shape[0]
    assert rows % DMA_UNROLL == 0

    def row_copy(r, t):
        return pltpu.make_async_copy(src_hbm.at[pl.ds(t, 1), :], dst.at[pl.ds(r, 1), :], sem)

    def issue(c, carry):
        for u in range(DMA_UNROLL):
            r = c * DMA_UNROLL + u
            row_copy(r, idx_of(r)).start()
        return carry

    def drain(c, carry):
        for u in range(DMA_UNROLL):
            row_copy(c * DMA_UNROLL + u, 0).wait()
        return carry

    start = lambda: lax.fori_loop(0, rows // DMA_UNROLL, issue, 0)
    wait = lambda: lax.fori_loop(0, rows // DMA_UNROLL, drain, 0)
    return start, wait


def _moe_gather_kernel(tok_ref, gv_ref, hp_ref, o_ref):
    tr, d = o_ref.shape
    half = d // 2
    gv = gv_ref[0, 0]
    group = 2 * SUBLANES

    @pl.when(gv < tr)
    def _():
        o_ref[...] = jnp.zeros_like(o_ref)

    def body(g, carry):
        r0 = pl.multiple_of(g * group, group)
        rows = [hp_ref[pl.ds(tok_ref[0, r0 + u], 1), :] for u in range(group)]
        hi, lo = _unpack_pair(jnp.concatenate(rows, axis=0))
        o_ref[pl.ds(r0, group), pl.ds(0, half)] = hi.astype(o_ref.dtype)
        o_ref[pl.ds(r0, group), pl.ds(half, half)] = lo.astype(o_ref.dtype)
        return carry

    lax.fori_loop(0, (gv + group - 1) // group, body, 0)


def _moe_gather(hp, tok, n_valid, tg, p_rows):
    m, half = hp.shape
    tr = _pick(tg, (GATHER_ROWS, 128, 64, 32, 16))
    nt = p_rows // tr
    per = tg // tr
    gidx = jnp.arange(nt, dtype=jnp.int32)
    gv = jnp.clip(n_valid[gidx // per] - (gidx % per) * tr, 0, tr).astype(jnp.int32)
    return _pcall(
        _moe_gather_kernel, name="moe_gather", grid=(nt,),
        in_specs=[pl.BlockSpec((None, 1, tr), lambda i: (i, 0, 0), memory_space=pltpu.SMEM),
                  pl.BlockSpec((None, 1, 1), lambda i: (i, 0, 0), memory_space=pltpu.SMEM),
                  pl.BlockSpec((m, half), lambda i: (0, 0), pipeline_mode=pl.Buffered(1))],
        args=(tok.reshape(nt, 1, tr), gv.reshape(nt, 1, 1), hp),
        out_specs=[pl.BlockSpec((tr, 2 * half), lambda i: (i, 0))],
        out_shape=[jax.ShapeDtypeStruct((p_rows, 2 * half), BF16)],
        sem=("arbitrary",))[0]


def _tile_changed(te_ref):
    ti = pl.program_id(1)
    return jnp.logical_or(ti == 0, te_ref[ti] != te_ref[jnp.maximum(ti - 1, 0)])


def _valid_row_chunks(nv, o_ref, chunk):
    tg = o_ref.shape[0]

    @pl.when(nv == tg)
    def _():
        _row_chunks(tg, chunk)

    @pl.when(nv < tg)
    def _():
        def maybe(rs):
            @pl.when(rs.start < nv)
            def _():
                chunk(rs)

            @pl.when(rs.start >= nv)
            def _():
                o_ref[rs, :] = jnp.zeros((rs.stop - rs.start, o_ref.shape[1]), o_ref.dtype)

        _row_chunks(tg, maybe)


def _gmm_swiglu_kernel(te_ref, nu_ref, nv_ref, x_ref, wg_ref, wu_ref, o_ref, wgb_ref, wub_ref):
    @pl.when(_tile_changed(te_ref))
    def _():
        _cast_w(wg_ref, wgb_ref)
        _cast_w(wu_ref, wub_ref)

    def chunk(rs):
        x = x_ref[rs, :]
        g = jnp.dot(x, wgb_ref[...], preferred_element_type=F32)
        u = jnp.dot(x, wub_ref[...], preferred_element_type=F32)
        o_ref[rs, :] = (jax.nn.silu(g) * u).astype(o_ref.dtype)

    _valid_row_chunks(nv_ref[pl.program_id(1)], o_ref, chunk)


def _gmm_down_kernel(te_ref, nu_ref, nv_ref, x_ref, w_ref, o_ref, wb_ref):
    @pl.when(_tile_changed(te_ref))
    def _():
        _cast_w(w_ref, wb_ref)

    hw = o_ref.shape[1]

    def chunk(rs):
        y = jnp.dot(x_ref[rs, :], wb_ref[...], preferred_element_type=F32)
        o_ref[rs, :] = _pack_pair(y[:, :hw], y[:, hw:])

    _valid_row_chunks(nv_ref[pl.program_id(1)], o_ref, chunk)


def _gmm_specs(jm, tg, k, tn, j0=0):
    xrow = lambda j, ti, te, nu, nv: (jnp.minimum(ti, nu[0] - 1), 0)
    wsp = pl.BlockSpec((None, None, k, tn), lambda j, ti, te, nu, nv: (jm, te[ti], 0, j0 + j))
    return pl.BlockSpec((tg, k), xrow), wsp, pl.BlockSpec((tg, tn), lambda j, ti, te, nu, nv: (ti, j0 + j))


def _gmm_swiglu(xg, wg, wu, jm, te, n_used, n_valid, tg, tns):
    p, k = xg.shape
    n = wg.shape[-1]
    assert sum(tn * cnt for tn, cnt in tns) == n
    out, col = None, 0
    for tn, cnt in tns:
        assert col % tn == 0
        xs, ws, os_ = _gmm_specs(jm, tg, k, tn, col // tn)
        out = _pcall(
            _gmm_swiglu_kernel, name="gmm_swiglu", grid=(cnt, p // tg), num_scalar_prefetch=3,
            in_specs=[None, None, None, xs, ws, ws], args=(te, n_used, n_valid, xg, wg, wu),
            out_specs=[os_], out_shape=[jax.ShapeDtypeStruct((p, n), BF16)],
            scratch=[pltpu.VMEM((k, tn), BF16), pltpu.VMEM((k, tn), BF16)],
            prev=(out,), sem=("arbitrary", "arbitrary"))[0]
        col += tn * cnt
    return out


def _col_tiling(n, widths):
    out, col = [], 0
    for wdt in widths:
        cnt = (n - col) // wdt
        if cnt:
            out.append((wdt, cnt))
            col += cnt * wdt
    assert col == n
    return tuple(out)


def _gmm_down(ug, wd, jm, te, n_used, n_valid, tg, tn):
    p, k = ug.shape
    n = wd.shape[-1]
    xs, ws, _ = _gmm_specs(jm, tg, k, tn)
    return _pcall(
        _gmm_down_kernel, name="gmm_down", grid=(n // tn, p // tg), num_scalar_prefetch=3,
        in_specs=[None, None, None, xs, ws], args=(te, n_used, n_valid, ug, wd),
        out_specs=[pl.BlockSpec((tg, tn // 2), lambda j, ti, te, nu, nv: (ti, j))],
        out_shape=[jax.ShapeDtypeStruct((p, n // 2), jnp.uint32)],
        scratch=[pltpu.VMEM((k, tn), BF16)],
        sem=("arbitrary", "arbitrary"))[0]


def _moe_combine_kernel(pos_ref, nxt_ref, x_ref, rw_ref, g_ref, yg_hbm, *rest, tn, np_tiles):
    outs, (buf, sem) = rest[:-2], rest[-2:]
    tc = x_ref.shape[0]
    i = pl.program_id(0)
    slot = lax.rem(i, 2)

    def gathers(idx_ref, s):
        return [_row_gather(yg_hbm, buf.at[s, k], lambda r, k=k: idx_ref[0, k * tc + r], sem.at[s])
                for k in range(TOP_K)]

    @pl.when(i == 0)
    def _():
        for start, _ in gathers(pos_ref, 0):
            start()

    @pl.when(i + 1 < pl.num_programs(0))
    def _():
        for start, _ in gathers(nxt_ref, 1 - slot):
            start()

    for _, wait in gathers(pos_ref, slot):
        wait()
    rows = buf[slot]

    def expand(p):
        hw = tn // 2
        parts = []
        for jj in range(p.shape[1] // hw):
            parts.extend(_unpack_pair(p[:, jj * hw:(jj + 1) * hw]))
        return jnp.concatenate(parts, axis=-1)

    rw = rw_ref[...]
    y = rw[:, TOP_K:TOP_K + 1] * expand(rows[0])
    for k in range(1, TOP_K):
        y = y + rw[:, TOP_K + k:TOP_K + k + 1] * expand(rows[k])
    xn = x_ref[...] + y
    if np_tiles is None:
        outs[0][...] = xn
    else:
        yn = _rms(xn, g_ref[...])

        @pl.when(i < np_tiles)
        def _():
            outs[0][...] = yn

        @pl.when(i >= np_tiles)
        def _():
            outs[1][...] = yn


def _moe_combine(x, rout, pos, ygp, tn, g3=None, mp=None):
    m, d = x.shape
    tc = _pick(m, (COMBINE_ROWS, 64, 32, 16, 8))
    nt = m // tc
    pos_t = pos.reshape(TOP_K, nt, tc).transpose(1, 0, 2).reshape(nt, 1, TOP_K * tc)
    if g3 is None:
        g3 = jnp.ones((1, 1, d), F32)
        np_tiles = None
        out_specs = [pl.BlockSpec((tc, d), lambda i: (i, 0))]
        out_shape = [jax.ShapeDtypeStruct((m, d), F32)]
    else:
        assert mp % tc == 0
        np_tiles = mp // tc
        out_specs = [pl.BlockSpec((tc, d), lambda i: (jnp.minimum(i, np_tiles - 1), 0)),
                     pl.BlockSpec((tc, d), lambda i: (jnp.maximum(i - np_tiles, 0), 0))]
        out_shape = [jax.ShapeDtypeStruct((mp, d), F32), jax.ShapeDtypeStruct((m - mp, d), F32)]
    return _pcall(
        functools.partial(_moe_combine_kernel, tn=tn, np_tiles=np_tiles), name="moe_combine", grid=(nt,),
        in_specs=[pl.BlockSpec((None, 1, TOP_K * tc), lambda i: (i, 0, 0), memory_space=pltpu.SMEM),
                  pl.BlockSpec((None, 1, TOP_K * tc), lambda i: (jnp.minimum(i + 1, nt - 1), 0, 0),
                               memory_space=pltpu.SMEM),
                  pl.BlockSpec((tc, d), lambda i: (i, 0)),
                  pl.BlockSpec((tc, LANES), lambda i: (i, 0)),
                  _vspec(0, d),
                  pl.BlockSpec(memory_space=pl.ANY)],
        args=(pos_t, pos_t, x, rout, g3, ygp),
        out_specs=out_specs, out_shape=out_shape,
        scratch=[pltpu.VMEM((2, TOP_K, tc, d // 2), jnp.uint32), pltpu.SemaphoreType.DMA((2,))],
        sem=("arbitrary",))


def kernel(x_prompt, x_sample, state_ret, state_lru, state_conv, norm_mix, w_in, ret_gn, w_ret_o, conv_w, conv_b, w_rgate, b_rgate, w_igate, b_igate, lru_lambda, w_lru_o, w_out, norm_ffn, ffn_w_gate, ffn_w_up, ffn_w_down, moe_router, moe_w_gate, moe_w_up, moe_w_down, norm_final):
    bp, tp, d = x_prompt.shape
    bs, ts, _ = x_sample.shape
    depth, _, nh, dk, dv = state_ret.shape
    width = state_lru.shape[-1]
    ncw = conv_w.shape[1]
    ne = moe_router.shape[-1]
    assert dk == dv and nh * dk == width == d and ts == SUBLANES and ncw - 1 <= SUBLANES
    mp, ms = bp * tp, bs * ts
    m = mp + ms
    g_col = 2 * nh * dk + nh * dv
    ux_col = g_col + nh * dv
    ga_col = ux_col + 2 * width

    tm = _pick(m, (1024, 512, 256, 128, 64, 32, 16, 8))
    tm_big = _pick(m, (1536, 1024, 512, 256, 128, 64, 32, 16, 8))
    tm_half = _pick(m, (512, 256, 128, 64, 32, 16, 8))
    tn_of = lambda n: _pick(n, (1024, 512, 256, 128))

    vec3 = lambda a: a.reshape(a.shape[0], 1, a.shape[-1])
    norm_mix3, norm_ffn3, ret_gn3 = vec3(norm_mix), vec3(norm_ffn), vec3(ret_gn)
    norm_final3 = norm_final.reshape(1, 1, d)
    lru_w = (conv_w, vec3(conv_b), w_rgate, vec3(b_rgate), w_igate, vec3(b_igate), vec3(lru_lambda))
    h0_rep = jnp.repeat(state_lru, ts, axis=1)
    buf_fr = jnp.pad(state_conv, ((0, 0), (0, 0), (ts - (ncw - 1), 0), (0, 0))).reshape(depth, ms, width)

    cos_p, sin_p = _rope_tables(tp, dk // 2, 0)
    cos_s, sin_s = _rope_tables(ts, dk // 2, PAST_LEN)
    cos = jnp.concatenate([jnp.tile(cos_p, (bp, 1)), jnp.tile(cos_s, (bs, 1))], axis=0)
    sin = jnp.concatenate([jnp.tile(sin_p, (bp, 1)), jnp.tile(sin_s, (bs, 1))], axis=0)
    tn_in = _pick(nh * dk, (1024, 512, 256))

    x = (x_prompt.reshape(mp, d), x_sample.reshape(ms, d))
    if mp % tm_half or ms % tm_half:
        x = jnp.concatenate(x, axis=0)
    ret_p = ret_s = y_p = y_s = None
    lrus_p, lrus_s, convs_p, convs_s = [], [], [], []
    for l in range(depth):
        if isinstance(x, tuple):
            h = _rmsnorm(x[0], norm_mix3, l, BF16, out_rows=m)
            h = _rmsnorm(x[1], norm_mix3, l, BF16, out_row0=mp, out_rows=m, prev=h)
        else:
            h = _rmsnorm(x, norm_mix3, l, BF16)
        proj = _mm_inproj(h, w_in, (l,), cos, sin, 2 * nh * dk, dk, (g_col, ux_col),
                          (ux_col + width, ga_col), tm_big, tn_in)

        ya, ret_p = _ret_prompt(proj, ret_gn3, l, depth, bp, tp, nh, dk, dv, ret_p)
        if _fused_seqs_per_step(width, tp, bp, bs, ts) is None:
            ya, ret_s = _ret_sample(proj, ret_gn3, state_ret, l, mp, bs, ts, nh, dk, dv, ya, ret_s)
            yb, hl_p, ul_p = _lru_prompt(proj, ux_col, l, lru_w, bp, tp)
        else:
            yb, hl_p, ul_p, ya, ret_s = _lru_prompt_ret_sample(
                proj, ux_col, l, lru_w, bp, tp, ret_gn3, state_ret, mp, bs, ts, nh, dk, dv, ya, ret_s)
        yb, hl_s, us_s = _lru_sample(proj, ux_col, l, lru_w, mp, h0_rep, buf_fr, yb)
        lrus_p.append(hl_p[:, -1])
        lrus_s.append(hl_s.reshape(bs, ts, width)[:, -1])
        convs_p.append(ul_p[:, SUBLANES - (ncw - 1):])
        convs_s.append(us_s.reshape(bs, ts, width)[:, ts - (ncw - 1):])

        z = _mm_merge(ya, yb, w_ret_o, w_lru_o, (l,), proj, ga_col, ga_col + d, tm, tn_of(d) // 2)
        x = _mm_resid(z, w_out, (l,), x, tm_half if isinstance(x, tuple) else tm, tn_of(d))
        j = l // 2
        if l % 2 == 0:
            h2 = _rmsnorm(x, norm_ffn3, l, BF16)
            ff = ffn_w_gate.shape[-1]
            u = _mm_swiglu(h2, ffn_w_gate, ffn_w_up, (j,), tm, _col_tiling(ff, (1024, 512, 256, 128)))
            x = _mm_resid(u, ffn_w_down, (j,), x, tm_half, _pick(d, (512, 256, 128)))
        else:
            fe = moe_w_gate.shape[-1]
            tg = _pick(TOP_K * m, (MOE_TILE, 256, 128, 64, 32, 16))
            tn_d = _pick(d, (1024, 512, 256))
            rout, hp = _router(x, norm_ffn3, l, moe_router[j])
            pos, tok, te, n_used, n_valid, p_rows = _dispatch_tables(rout, ne, tg)
            xg = _moe_gather(hp, tok, n_valid, tg, p_rows)
            ug = _gmm_swiglu(xg, moe_w_gate, moe_w_up, j, te, n_used, n_valid, tg,
                             _col_tiling(fe, (1024, 512, 256, 128)))
            ygp = _gmm_down(ug, moe_w_down, j, te, n_used, n_valid, tg, tn_d)
            if l == depth - 1:
                y_p, y_s = _moe_combine(x, rout, pos, ygp, tn_d, norm_final3, mp)
            else:
                x = _moe_combine(x, rout, pos, ygp, tn_d)[0]
    if y_p is None:
        y_p = _rmsnorm(x, norm_final3, 0, F32, 0, mp)
        y_s = _rmsnorm(x, norm_final3, 0, F32, mp, ms)
    return (y_p.reshape(bp, tp, d), y_s.reshape(bs, ts, d),
            ret_p, jnp.stack(lrus_p), jnp.stack(convs_p),
            ret_s, jnp.stack(lrus_s), jnp.stack(convs_s))
```

```python
import functools

import jax
import jax.numpy as jnp
from jax import lax
from jax.experimental import pallas as pl
from jax.experimental.pallas import tpu as pltpu

F32 = jnp.float32
BF16 = jnp.bfloat16

ROPE_BASE = 10000.0
LRU_C = 8.0
EPS = 1e-6
RET_CHUNK = 128
PAST_LEN = 16384
TOP_K = 2
SUBLANES = 8
LANES = 128
VMEM_LIMIT = 56 * 1024 * 1024
MOE_TILE = 1024
GATHER_ROWS = 1024
COMBINE_ROWS = 512


def _pick(n, prefs):
    for p in prefs:
        if n % p == 0:
            return p
    return n


def _params(*sem):
    return pltpu.CompilerParams(dimension_semantics=sem, vmem_limit_bytes=VMEM_LIMIT)


def _pcall(body, *, name, grid, in_specs, args, out_specs, out_shape, sem, scratch=(), prev=(),
           num_scalar_prefetch=0):
    n_in = len(args)
    prev = tuple(prev) + (None,) * (len(out_shape) - len(prev))
    extra = [(oi, p) for oi, p in enumerate(prev) if p is not None]
    aliases = {n_in + e: oi for e, (oi, _) in enumerate(extra)}
    if extra:
        inner = body

        def body(*refs):
            return inner(*refs[:n_in], *refs[n_in + len(extra):])

    in_specs = list(in_specs) + [pl.BlockSpec(memory_space=pl.ANY)] * len(extra)
    if num_scalar_prefetch:
        grid_spec = pltpu.PrefetchScalarGridSpec(
            num_scalar_prefetch=num_scalar_prefetch, grid=grid, in_specs=in_specs[num_scalar_prefetch:],
            out_specs=tuple(out_specs), scratch_shapes=list(scratch))
        call = pl.pallas_call(body, grid_spec=grid_spec, out_shape=tuple(out_shape),
                              input_output_aliases=aliases, compiler_params=_params(*sem), name=name)
    else:
        call = pl.pallas_call(body, grid=grid, in_specs=in_specs, out_specs=tuple(out_specs),
                              out_shape=tuple(out_shape), scratch_shapes=list(scratch),
                              input_output_aliases=aliases, compiler_params=_params(*sem), name=name)
    return call(*args, *[p for _, p in extra])


def _wspec(pre, k, tn):
    return pl.BlockSpec((None,) * len(pre) + (k, tn), lambda j, i: pre + (0, j))


def _vspec(l, n):
    return pl.BlockSpec((None, 1, n), lambda *_: (l, 0, 0))


def _rms(x, g):
    return (x * lax.rsqrt(jnp.mean(x * x, axis=-1, keepdims=True) + EPS)) * g


def _rmsnorm_kernel(x_ref, g_ref, o_ref):
    o_ref[...] = _rms(x_ref[...], g_ref[...]).astype(o_ref.dtype)


def _rmsnorm(x, g3, l, out_dtype, row0=0, rows=None, out_row0=0, out_rows=None, prev=None):
    d = x.shape[1]
    rows = x.shape[0] if rows is None else rows
    out_rows = rows if out_rows is None else out_rows
    tm = _pick(rows, (512, 256, 128, 64, 32, 16, 8))
    assert row0 % tm == 0 and out_row0 % tm == 0
    rb0, ob0 = row0 // tm, out_row0 // tm
    return _pcall(
        _rmsnorm_kernel, name="rmsnorm", grid=(rows // tm,),
        in_specs=[pl.BlockSpec((tm, d), lambda i: (rb0 + i, 0)), _vspec(l, d)],
        args=(x, g3),
        out_specs=[pl.BlockSpec((tm, d), lambda i: (ob0 + i, 0))],
        out_shape=[jax.ShapeDtypeStruct((out_rows, d), out_dtype)],
        prev=(prev,), sem=("arbitrary",))[0]


def _cast_w(w_ref, wb_ref):
    k = w_ref.shape[0]
    ck = _pick(k, (256, 128, 64, 32, 16))

    def body(c, carry):
        r = pl.multiple_of(c * ck, ck)
        wb_ref[pl.ds(r, ck), :] = w_ref[pl.ds(r, ck), :].astype(BF16)
        return carry

    lax.fori_loop(0, k // ck, body, 0)


def _mm_inproj_kernel(x_ref, w_ref, cos_ref, sin_ref, o_ref, wb_ref, *, nrot, dk, silu_tiles, gelu_tiles):
    j = pl.program_id(0)

    @pl.when(pl.program_id(1) == 0)
    def _():
        _cast_w(w_ref, wb_ref)

    tm, tn = o_ref.shape
    nchunk = 4 if tm % (4 * 16) == 0 else 1

    def row_chunks(epilogue):
        rc = tm // nchunk
        for c in range(nchunk):
            rs = slice(c * rc, (c + 1) * rc)
            epilogue(jnp.dot(x_ref[rs, :], wb_ref[...], preferred_element_type=F32), rs)

    def store(fn):
        def epilogue(acc, rs):
            o_ref[rs, :] = fn(acc).astype(o_ref.dtype)
        return epilogue

    in_tiles = lambda t: jnp.logical_and(j >= t[0], j < t[1])
    is_silu, is_gelu = in_tiles(silu_tiles), in_tiles(gelu_tiles)

    @pl.when(jnp.logical_and(j >= nrot, jnp.logical_not(jnp.logical_or(is_silu, is_gelu))))
    def _():
        row_chunks(store(lambda acc: acc))

    @pl.when(is_silu)
    def _():
        row_chunks(store(jax.nn.silu))

    @pl.when(is_gelu)
    def _():
        row_chunks(store(jax.nn.gelu))

    @pl.when(j < nrot)
    def _():
        scale = jnp.where(j >= nrot // 2, F32(dk ** -0.5), F32(1.0))
        half = dk // 2

        def rotary(acc, rs):
            cos = cos_ref[rs, :] * scale
            sin = sin_ref[rs, :] * scale
            for h in range(tn // dk):
                x1 = acc[:, h * dk:h * dk + half]
                x2 = acc[:, h * dk + half:(h + 1) * dk]
                o_ref[rs, h * dk:h * dk + half] = (x1 * cos - x2 * sin).astype(o_ref.dtype)
                o_ref[rs, h * dk + half:(h + 1) * dk] = (x1 * sin + x2 * cos).astype(o_ref.dtype)

        row_chunks(rotary)


def _mm_inproj(x, w, pre, cos, sin, qk_cols, dk, silu_cols, gelu_cols, tm, tn):
    m, k = x.shape
    n = w.shape[-1]
    half = cos.shape[1]
    assert qk_cols % (2 * tn) == 0 and tn % dk == 0
    assert all(c % tn == 0 for c in silu_cols + gelu_cols)
    tiles = lambda cols: (cols[0] // tn, cols[1] // tn)
    return _pcall(
        functools.partial(_mm_inproj_kernel, nrot=qk_cols // tn, dk=dk, silu_tiles=tiles(silu_cols),
                          gelu_tiles=tiles(gelu_cols)), name="mm_inproj",
        grid=(n // tn, m // tm),
        in_specs=[pl.BlockSpec((tm, k), lambda j, i: (i, 0)), _wspec(pre, k, tn),
                  pl.BlockSpec((tm, half), lambda j, i: (i, 0)),
                  pl.BlockSpec((tm, half), lambda j, i: (i, 0))],
        args=(x, w, cos, sin),
        out_specs=[pl.BlockSpec((tm, tn), lambda j, i: (i, j))],
        out_shape=[jax.ShapeDtypeStruct((m, n), BF16)],
        scratch=[pltpu.VMEM((k, tn), BF16)],
        sem=("arbitrary", "arbitrary"))[0]


def _mm_resid_kernel(x_ref, w_ref, r_ref, o_ref, wb_ref):
    @pl.when(pl.program_id(1) == 0)
    def _():
        _cast_w(w_ref, wb_ref)

    o_ref[...] = r_ref[...] + jnp.dot(x_ref[...], wb_ref[...], preferred_element_type=F32)


def _mm_resid2_kernel(x_ref, w_ref, ra_ref, rb_ref, o_ref, wb_ref, *, na):
    @pl.when(pl.program_id(1) == 0)
    def _():
        _cast_w(w_ref, wb_ref)

    y = jnp.dot(x_ref[...], wb_ref[...], preferred_element_type=F32)
    i = pl.program_id(1)

    @pl.when(i < na)
    def _():
        o_ref[...] = ra_ref[...] + y

    @pl.when(i >= na)
    def _():
        o_ref[...] = rb_ref[...] + y


def _mm_resid(x, w, pre, r, tm, tn):
    m, k = x.shape
    n = w.shape[-1]
    if isinstance(r, tuple):
        ra, rb = r
        assert ra.shape[0] % tm == 0 and rb.shape[0] % tm == 0 and ra.shape[0] + rb.shape[0] == m
        na = ra.shape[0] // tm
        body = functools.partial(_mm_resid2_kernel, na=na)
        r_specs = [pl.BlockSpec((tm, tn), lambda j, i: (jnp.minimum(i, na - 1), j)),
                   pl.BlockSpec((tm, tn), lambda j, i: (jnp.maximum(i - na, 0), j))]
        r_args = (ra, rb)
    else:
        body, r_specs, r_args = _mm_resid_kernel, [pl.BlockSpec((tm, tn), lambda j, i: (i, j))], (r,)
    return _pcall(
        body, name="mm_resid", grid=(n // tn, m // tm),
        in_specs=[pl.BlockSpec((tm, k), lambda j, i: (i, 0)), _wspec(pre, k, tn)] + r_specs,
        args=(x, w) + r_args,
        out_specs=[pl.BlockSpec((tm, tn), lambda j, i: (i, j))],
        out_shape=[jax.ShapeDtypeStruct((m, n), F32)],
        scratch=[pltpu.VMEM((k, tn), BF16)],
        sem=("arbitrary", "arbitrary"))[0]


def _row_chunks(rows, fn, nchunk=4):
    n = nchunk if rows % (nchunk * 16) == 0 else 1
    rc = rows // n
    for c in range(n):
        fn(slice(c * rc, (c + 1) * rc))


def _swiglu_rows(x_ref, wgb_ref, wub_ref, o_ref):
    def chunk(rs):
        x = x_ref[rs, :]
        g = jnp.dot(x, wgb_ref[...], preferred_element_type=F32)
        u = jnp.dot(x, wub_ref[...], preferred_element_type=F32)
        o_ref[rs, :] = (jax.nn.silu(g) * u).astype(o_ref.dtype)

    _row_chunks(o_ref.shape[0], chunk)


def _mm_swiglu_kernel(x_ref, wg_ref, wu_ref, o_ref, wgb_ref, wub_ref):
    @pl.when(pl.program_id(1) == 0)
    def _():
        _cast_w(wg_ref, wgb_ref)
        _cast_w(wu_ref, wub_ref)

    _swiglu_rows(x_ref, wgb_ref, wub_ref, o_ref)


def _mm_swiglu(x, wg, wu, pre, tm, tn):
    m, k = x.shape
    n = wg.shape[-1]
    return _pcall(
        _mm_swiglu_kernel, name="mm_swiglu", grid=(n // tn, m // tm),
        in_specs=[pl.BlockSpec((tm, k), lambda j, i: (i, 0)), _wspec(pre, k, tn), _wspec(pre, k, tn)],
        args=(x, wg, wu),
        out_specs=[pl.BlockSpec((tm, tn), lambda j, i: (i, j))],
        out_shape=[jax.ShapeDtypeStruct((m, n), BF16)],
        scratch=[pltpu.VMEM((k, tn), BF16), pltpu.VMEM((k, tn), BF16)],
        sem=("arbitrary", "arbitrary"))[0]


def _mm_merge_kernel(a_ref, b_ref, wa_ref, wb_ref, ga_ref, gb_ref, o_ref, wab_ref, wbb_ref):
    @pl.when(pl.program_id(1) == 0)
    def _():
        _cast_w(wa_ref, wab_ref)
        _cast_w(wb_ref, wbb_ref)

    def chunk(rs):
        ya = jnp.dot(a_ref[rs, :], wab_ref[...], preferred_element_type=F32)
        yb = jnp.dot(b_ref[rs, :], wbb_ref[...], preferred_element_type=F32)
        ga = ga_ref[rs, :].astype(F32)
        gb = gb_ref[rs, :].astype(F32)
        o_ref[rs, :] = (jax.nn.sigmoid(ga) * ya + jax.nn.sigmoid(gb) * yb).astype(o_ref.dtype)

    _row_chunks(o_ref.shape[0], chunk)


def _mm_merge(a, b, wa, wb, pre, proj, ga_col, gb_col, tm, tn):
    m, k = a.shape
    n = wa.shape[-1]
    ga_blk, gb_blk = ga_col // tn, gb_col // tn
    return _pcall(
        _mm_merge_kernel, name="mm_merge", grid=(n // tn, m // tm),
        in_specs=[pl.BlockSpec((tm, k), lambda j, i: (i, 0)),
                  pl.BlockSpec((tm, k), lambda j, i: (i, 0)),
                  _wspec(pre, k, tn), _wspec(pre, k, tn),
                  pl.BlockSpec((tm, tn), lambda j, i: (i, ga_blk + j)),
                  pl.BlockSpec((tm, tn), lambda j, i: (i, gb_blk + j))],
        args=(a, b, wa, wb, proj, proj),
        out_specs=[pl.BlockSpec((tm, tn), lambda j, i: (i, j))],
        out_shape=[jax.ShapeDtypeStruct((m, n), BF16)],
        scratch=[pltpu.VMEM((k, tn), BF16), pltpu.VMEM((k, tn), BF16)],
        sem=("arbitrary", "arbitrary"))[0]


def _rope_kernel(inv_ref, cos_ref, sin_ref, *, pos0):
    t, half = cos_ref.shape
    pos = lax.broadcasted_iota(jnp.int32, (t, half), 0).astype(F32) + F32(pos0)
    ang = pos * inv_ref[...]
    cos_ref[...] = jnp.cos(ang)
    sin_ref[...] = jnp.sin(ang)


def _rope_tables(t, half, pos0):
    inv = ROPE_BASE ** (-jnp.arange(half, dtype=F32) / half)
    return pl.pallas_call(
        functools.partial(_rope_kernel, pos0=pos0),
        out_shape=(jax.ShapeDtypeStruct((t, half), F32), jax.ShapeDtypeStruct((t, half), F32)),
        name="rope_tables",
    )(inv.reshape(1, half))


def _decay_tables(c, h, dk, dv):
    log_g = jnp.log1p(-jnp.exp2(-5.0 - jnp.arange(h, dtype=F32)))
    idx = jnp.arange(c)
    rel = idx[:, None] - idx[None, :]
    dmask = jnp.where(rel[None] >= 0,
                      jnp.exp(jnp.maximum(rel, 0)[None].astype(F32) * log_g[:, None, None]), 0.0)
    xi = jnp.exp((idx + 1).astype(F32)[None, :] * log_g[:, None])
    zeta = jnp.exp((c - 1 - idx).astype(F32)[None, :] * log_g[:, None])
    g_c = jnp.exp(c * log_g)
    xi_t = jnp.broadcast_to(xi[:, :, None], (h, c, dv))
    zeta_t = jnp.broadcast_to(zeta[:, :, None], (h, c, dk))
    return dmask, xi_t, zeta_t, g_c


def _ret_head(qb, kb, v, g, dm, xi, zt, gn, s, gc):
    scores = lax.dot_general(qb, kb, (((1,), (1,)), ((), ())), preferred_element_type=F32)
    intra = jnp.dot((scores * dm).astype(BF16), v, preferred_element_type=F32)
    cross = jnp.dot(qb, s.astype(BF16), preferred_element_type=F32) * xi
    kz = (kb.astype(F32) * zt).astype(BF16)
    upd = lax.dot_general(kz, v, (((0,), (0,)), ((), ())), preferred_element_type=F32)
    s_new = gc * s + upd
    o = intra + cross
    mu = jnp.mean(o, axis=-1, keepdims=True)
    d = o - mu
    var = jnp.mean(d * d, axis=-1, keepdims=True)
    y = d * lax.rsqrt(var + EPS) * gn
    return (g * y).astype(BF16), s_new


def _ret_prompt_kernel(gc_ref, q_ref, k_ref, v_ref, g_ref, dm_ref, xi_ref, zt_ref,
                       gn_ref, o_ref, s_ref, *, nh, dk, dv):
    @pl.when(pl.program_id(1) == 0)
    def _():
        s_ref[...] = jnp.zeros_like(s_ref)

    c = dm_ref.shape[1]
    for h in range(nh):
        s = s_ref[h]
        for ci in range(q_ref.shape[0] // c):
            rs = slice(ci * c, (ci + 1) * c)
            out, s = _ret_head(
                q_ref[rs, h * dk:(h + 1) * dk], k_ref[rs, h * dk:(h + 1) * dk],
                v_ref[rs, h * dv:(h + 1) * dv], g_ref[rs, h * dv:(h + 1) * dv].astype(F32),
                dm_ref[h], xi_ref[h], zt_ref[h], gn_ref[:, h * dv:(h + 1) * dv], s, gc_ref[h])
            o_ref[rs, h * dv:(h + 1) * dv] = out
        s_ref[h] = s


def _ret_prompt(proj, gn3, l, depth, b, t, nh, dk, dv, s_prev):
    cl = RET_CHUNK if t % RET_CHUNK == 0 else t
    cps = _pick(t // cl, (8, 4, 2, 1))
    c = cl * cps
    nc = t // c
    w = nh * dk
    m = proj.shape[0]
    dmask, xi_t, zeta_t, g_c = _decay_tables(cl, nh, dk, dv)
    row = lambda bi, ci: bi * nc + ci
    full3 = lambda bi, ci: (0, 0, 0)
    return _pcall(
        functools.partial(_ret_prompt_kernel, nh=nh, dk=dk, dv=dv), name="ret_prompt", grid=(b, nc),
        in_specs=[pl.BlockSpec(memory_space=pltpu.SMEM),
                  pl.BlockSpec((c, w), lambda bi, ci: (row(bi, ci), 0)),
                  pl.BlockSpec((c, w), lambda bi, ci: (row(bi, ci), 1)),
                  pl.BlockSpec((c, w), lambda bi, ci: (row(bi, ci), 2)),
                  pl.BlockSpec((c, w), lambda bi, ci: (row(bi, ci), 3)),
                  pl.BlockSpec((nh, cl, cl), full3),
                  pl.BlockSpec((nh, cl, dv), full3),
                  pl.BlockSpec((nh, cl, dk), full3),
                  _vspec(l, nh * dv)],
        args=(g_c, proj, proj, proj, proj, dmask, xi_t, zeta_t, gn3),
        out_specs=[pl.BlockSpec((c, nh * dv), lambda bi, ci: (row(bi, ci), 0)),
                   pl.BlockSpec((None, None, nh, dk, dv), lambda bi, ci: (l, bi, 0, 0, 0))],
        out_shape=[jax.ShapeDtypeStruct((m, nh * dv), BF16),
                   jax.ShapeDtypeStruct((depth, b, nh, dk, dv), F32)],
        prev=(None, s_prev),
        sem=("arbitrary", "arbitrary"))


def _ret_sample_kernel(gc_ref, q_ref, k_ref, v_ref, g_ref, dm_ref, xi_ref, zt_ref,
                       gn_ref, s0_ref, o_ref, s_ref, *, nh, dk, dv, bb, ts):
    q = q_ref[...].astype(F32)
    k = k_ref[...].astype(F32)
    v = v_ref[...].astype(F32)
    g = g_ref[...].astype(F32)
    seqs = []
    for i in range(bb):
        r0, r1 = i * ts, (i + 1) * ts
        outs = []
        for h in range(nh):
            out, s_new = _ret_head(
                q[r0:r1, h * dk:(h + 1) * dk].astype(BF16), k[r0:r1, h * dk:(h + 1) * dk].astype(BF16),
                v[r0:r1, h * dv:(h + 1) * dv].astype(BF16), g[r0:r1, h * dv:(h + 1) * dv],
                dm_ref[h], xi_ref[h], zt_ref[h], gn_ref[:, h * dv:(h + 1) * dv],
                s0_ref[i, h], gc_ref[h])
            s_ref[i, h] = s_new
            outs.append(out.astype(F32))
        seqs.append(jnp.concatenate(outs, axis=-1))
    o_ref[...] = jnp.concatenate(seqs, axis=0).astype(o_ref.dtype)


def _ret_sample(proj, gn3, s0_all, l, row0, bs, ts, nh, dk, dv, ya_prev, s_prev):
    bb = _pick(bs, (4, 2, 1))
    rows = bb * ts
    w = nh * dk
    depth = s0_all.shape[0]
    dmask, xi_t, zeta_t, g_c = _decay_tables(ts, nh, dk, dv)
    assert row0 % rows == 0
    rb0 = row0 // rows
    full3 = lambda i: (0, 0, 0)
    return _pcall(
        functools.partial(_ret_sample_kernel, nh=nh, dk=dk, dv=dv, bb=bb, ts=ts), name="ret_sample",
        grid=(bs // bb,),
        in_specs=[pl.BlockSpec(memory_space=pltpu.SMEM),
                  pl.BlockSpec((rows, w), lambda i: (rb0 + i, 0)),
                  pl.BlockSpec((rows, w), lambda i: (rb0 + i, 1)),
                  pl.BlockSpec((rows, w), lambda i: (rb0 + i, 2)),
                  pl.BlockSpec((rows, w), lambda i: (rb0 + i, 3)),
                  pl.BlockSpec((nh, ts, ts), full3),
                  pl.BlockSpec((nh, ts, dv), full3),
                  pl.BlockSpec((nh, ts, dk), full3),
                  _vspec(l, nh * dv),
                  pl.BlockSpec((None, bb, nh, dk, dv), lambda i: (l, i, 0, 0, 0))],
        args=(g_c, proj, proj, proj, proj, dmask, xi_t, zeta_t, gn3, s0_all),
        out_specs=[pl.BlockSpec((rows, nh * dv), lambda i: (rb0 + i, 0)),
                   pl.BlockSpec((None, bb, nh, dk, dv), lambda i: (l, i, 0, 0, 0))],
        out_shape=[jax.ShapeDtypeStruct(ya_prev.shape, BF16),
                   jax.ShapeDtypeStruct((depth, bs, nh, dk, dv), F32)],
        prev=(ya_prev, s_prev),
        sem=("arbitrary",))


def _lru_gates(uc, wr_ref, br_ref, wi_ref, bi_ref, lam_ref):
    nb, lb, _ = wr_ref.shape
    ucb = uc.astype(BF16)
    rl, il = [], []
    for n in range(nb):
        xb = ucb[:, n * lb:(n + 1) * lb]
        rl.append(jnp.dot(xb, wr_ref[n].astype(BF16), preferred_element_type=F32))
        il.append(jnp.dot(xb, wi_ref[n].astype(BF16), preferred_element_type=F32))
    r = jax.nn.sigmoid(jnp.concatenate(rl, axis=-1) + br_ref[...])
    i = jax.nn.sigmoid(jnp.concatenate(il, axis=-1) + bi_ref[...])
    log_a = -LRU_C * r * jax.nn.softplus(-lam_ref[...])
    a = jnp.exp(log_a)
    z = -jnp.tanh(log_a) * (a * a + 1.0)
    bx = jnp.where(z == 0.0, 0.0, z * lax.rsqrt(z)) * (i * uc)
    return a, bx


def _blocks(x):
    return [x[lo:lo + SUBLANES, :] for lo in range(0, x.shape[0], SUBLANES)]


def _conv_blocks(u_blocks, prev_blocks, cw_ref, cb_ref, rows):
    ncw = cw_ref.shape[0]
    rolled = {}

    def rot(x, s):
        key = (id(x), s)
        if key not in rolled:
            rolled[key] = pltpu.roll(x, s, axis=0)
        return rolled[key]

    out = []
    for ub, pb in zip(u_blocks, prev_blocks):
        uc = cb_ref[...]
        for j in range(ncw):
            s = ncw - 1 - j
            term = ub if s == 0 else jnp.where(rows >= s, rot(ub, s), rot(pb, s))
            uc = uc + term * cw_ref[j:j + 1, :]
        out.append(uc)
    return jnp.concatenate(out, axis=0)


def _scan_block(a, bx, rows):
    d = 1
    while d < SUBLANES:
        m = rows >= d
        bx = bx + a * jnp.where(m, pltpu.roll(bx, d, axis=0), 0.0)
        a = a * jnp.where(m, pltpu.roll(a, d, axis=0), 1.0)
        d *= 2
    return a, bx


def _lru_prompt_kernel(ux_ref, uy_ref, cw_ref, cb_ref, wr_ref, br_ref, wi_ref, bi_ref, lam_ref,
                       o_ref, hl_ref, ul_ref, hc_ref):
    _lru_prompt_init(ul_ref, hc_ref)
    _lru_prompt_main(ux_ref, uy_ref, cw_ref, cb_ref, wr_ref, br_ref, wi_ref, bi_ref, lam_ref,
                     o_ref, hl_ref, ul_ref, hc_ref)


def _lru_prompt_init(ul_ref, hc_ref):
    @pl.when(pl.program_id(2) == 0)
    def _():
        hc_ref[...] = jnp.zeros_like(hc_ref)
        ul_ref[...] = jnp.zeros_like(ul_ref)


def _lru_prompt_main(ux_ref, uy_ref, cw_ref, cb_ref, wr_ref, br_ref, wi_ref, bi_ref, lam_ref,
                     o_ref, hl_ref, ul_ref, hc_ref):
    r, cw = ux_ref.shape
    rows = lax.broadcasted_iota(jnp.int32, (SUBLANES, cw), 0)
    ub = _blocks(ux_ref[...].astype(F32))
    uc = _conv_blocks(ub, [ul_ref[...]] + ub[:-1], cw_ref, cb_ref, rows)
    ul_ref[...] = ub[-1]
    a, bx = _lru_gates(uc, wr_ref, br_ref, wi_ref, bi_ref, lam_ref)
    carry = hc_ref[...]
    hs = []
    for ab, bb in zip(_blocks(a), _blocks(bx)):
        ab, bb = _scan_block(ab, bb, rows)
        hb = ab * carry + bb
        hs.append(hb)
        carry = jnp.broadcast_to(hb[SUBLANES - 1:SUBLANES, :], (SUBLANES, cw))
    hc_ref[...] = carry
    hl_ref[...] = hs[-1]
    h = jnp.concatenate(hs, axis=0)
    o_ref[...] = (h * uy_ref[...].astype(F32)).astype(o_ref.dtype)


def _lru_sample_kernel(ux_ref, uy_ref, cw_ref, cb_ref, wr_ref, br_ref, wi_ref, bi_ref, lam_ref,
                       h0_ref, buf_ref, o_ref, hl_ref, us_ref):
    r, cw = ux_ref.shape
    rows = lax.broadcasted_iota(jnp.int32, (SUBLANES, cw), 0)
    u = ux_ref[...].astype(F32)
    us_ref[...] = u
    uc = _conv_blocks(_blocks(u), _blocks(buf_ref[...]), cw_ref, cb_ref, rows)
    a, bx = _lru_gates(uc, wr_ref, br_ref, wi_ref, bi_ref, lam_ref)
    hs = []
    for ab, bb, h0 in zip(_blocks(a), _blocks(bx), _blocks(h0_ref[...])):
        ab, bb = _scan_block(ab, bb, rows)
        hs.append(ab * h0 + bb)
    h = jnp.concatenate(hs, axis=0)
    hl_ref[...] = h
    o_ref[...] = (h * uy_ref[...].astype(F32)).astype(o_ref.dtype)


def _lru_specs(l, r, cw, ncw, lb, ux_blk, uy_blk, row_of, cb_of):
    nbc = cw // lb
    vec = pl.BlockSpec((None, 1, cw), lambda *g: (l, 0, cb_of(*g)))
    gate = pl.BlockSpec((None, nbc, lb, lb), lambda *g: (l, cb_of(*g), 0, 0))
    return [pl.BlockSpec((r, cw), lambda *g: (row_of(*g), ux_blk + cb_of(*g))),
            pl.BlockSpec((r, cw), lambda *g: (row_of(*g), uy_blk + cb_of(*g))),
            pl.BlockSpec((None, ncw, cw), lambda *g: (l, 0, cb_of(*g))),
            vec, gate, vec, gate, vec, vec]


def _lru_prompt(proj, ux_col, l, lru_w, b, t):
    conv_w, conv_b3, w_r, b_r3, w_i, b_i3, lam3 = lru_w
    width = lam3.shape[-1]
    lb = w_r.shape[-1]
    ncw = conv_w.shape[1]
    m = proj.shape[0]
    cw = _pick(width, (1024, 512, 256, 128))
    r = _pick(t, (512, 256, 128, 64, 32, 16, 8))
    nt = t // r
    specs = _lru_specs(l, r, cw, ncw, lb, ux_col // cw, (ux_col + width) // cw,
                       lambda bi, cb, ti: bi * nt + ti, lambda bi, cb, ti: cb)
    last8 = pl.BlockSpec((None, SUBLANES, cw), lambda bi, cb, ti: (bi, 0, cb))
    return _pcall(
        _lru_prompt_kernel, name="lru_prompt", grid=(b, width // cw, nt),
        in_specs=specs,
        args=(proj, proj, conv_w, conv_b3, w_r, b_r3, w_i, b_i3, lam3),
        out_specs=[pl.BlockSpec((r, cw), lambda bi, cb, ti: (bi * nt + ti, cb)), last8, last8],
        out_shape=[jax.ShapeDtypeStruct((m, width), BF16),
                   jax.ShapeDtypeStruct((b, SUBLANES, width), F32),
                   jax.ShapeDtypeStruct((b, SUBLANES, width), F32)],
        scratch=[pltpu.VMEM((SUBLANES, cw), F32)],
        sem=("arbitrary", "arbitrary", "arbitrary"))


def _lru_ret_kernel(*refs, nh, dk, dv, bb, ts):
    lru_in, ret_in = refs[:9], refs[9:19]
    yb_ref, hl_ref, ul_ref, ya_ref, s_ref, hc_ref = refs[19:]
    _lru_prompt_init(ul_ref, hc_ref)
    _ret_sample_kernel(*ret_in, ya_ref, s_ref, nh=nh, dk=dk, dv=dv, bb=bb, ts=ts)
    _lru_prompt_main(*lru_in, yb_ref, hl_ref, ul_ref, hc_ref)


def _fused_seqs_per_step(width, t, b, bs, ts):
    cw = _pick(width, (1024, 512, 256, 128))
    r = _pick(t, (512, 256, 128, 64, 32, 16, 8))
    steps = b * (width // cw) * (t // r)
    if bs % steps:
        return None
    bb = bs // steps
    return bb if bb <= 4 and (bb * ts) % 16 == 0 else None


def _lru_prompt_ret_sample(proj, ux_col, l, lru_w, b, t, gn3, s0_all, row0, bs, ts, nh, dk, dv,
                           ya_prev, s_prev):
    conv_w, conv_b3, w_r, b_r3, w_i, b_i3, lam3 = lru_w
    width = lam3.shape[-1]
    lb = w_r.shape[-1]
    ncw = conv_w.shape[1]
    m = proj.shape[0]
    depth = s0_all.shape[0]
    cw = _pick(width, (1024, 512, 256, 128))
    r = _pick(t, (512, 256, 128, 64, 32, 16, 8))
    nt, ncb = t // r, width // cw
    bb = _fused_seqs_per_step(width, t, b, bs, ts)
    rows = bb * ts
    w = nh * dk
    assert row0 % rows == 0
    rb0 = row0 // rows
    lin = lambda bi, cb, ti: (bi * ncb + cb) * nt + ti
    dmask, xi_t, zeta_t, g_c = _decay_tables(ts, nh, dk, dv)
    lru_specs = _lru_specs(l, r, cw, ncw, lb, ux_col // cw, (ux_col + width) // cw,
                           lambda bi, cb, ti: bi * nt + ti, lambda bi, cb, ti: cb)
    full3 = lambda *g: (0, 0, 0)
    qkvg = [pl.BlockSpec((rows, w), lambda *g, c=c: (rb0 + lin(*g), c)) for c in range(4)]
    state = pl.BlockSpec((None, bb, nh, dk, dv), lambda *g: (l, lin(*g), 0, 0, 0))
    ret_specs = ([pl.BlockSpec(memory_space=pltpu.SMEM)] + qkvg +
                 [pl.BlockSpec((nh, ts, ts), full3), pl.BlockSpec((nh, ts, dv), full3),
                  pl.BlockSpec((nh, ts, dk), full3), _vspec(l, nh * dv), state])
    last8 = pl.BlockSpec((None, SUBLANES, cw), lambda bi, cb, ti: (bi, 0, cb))
    return _pcall(
        functools.partial(_lru_ret_kernel, nh=nh, dk=dk, dv=dv, bb=bb, ts=ts), name="lru_prompt_ret_sample",
        grid=(b, ncb, nt),
        in_specs=lru_specs + ret_specs,
        args=(proj, proj, conv_w, conv_b3, w_r, b_r3, w_i, b_i3, lam3,
              g_c, proj, proj, proj, proj, dmask, xi_t, zeta_t, gn3, s0_all),
        out_specs=[pl.BlockSpec((r, cw), lambda bi, cb, ti: (bi * nt + ti, cb)), last8, last8,
                   pl.BlockSpec((rows, nh * dv), lambda *g: (rb0 + lin(*g), 0)), state],
        out_shape=[jax.ShapeDtypeStruct((m, width), BF16),
                   jax.ShapeDtypeStruct((b, SUBLANES, width), F32),
                   jax.ShapeDtypeStruct((b, SUBLANES, width), F32),
                   jax.ShapeDtypeStruct(ya_prev.shape, BF16),
                   jax.ShapeDtypeStruct((depth, bs, nh, dk, dv), F32)],
        prev=(None, None, None, ya_prev, s_prev),
        scratch=[pltpu.VMEM((SUBLANES, cw), F32)],
        sem=("arbitrary", "arbitrary", "arbitrary"))


def _lru_sample(proj, ux_col, l, lru_w, row0, h0_rep, buf_fr, yb_prev):
    conv_w, conv_b3, w_r, b_r3, w_i, b_i3, lam3 = lru_w
    width = lam3.shape[-1]
    lb = w_r.shape[-1]
    ncw = conv_w.shape[1]
    ms = h0_rep.shape[1]
    cw = _pick(width, (1024, 512, 256, 128))
    r = _pick(ms, (256, 128, 64, 32, 16, 8))
    assert row0 % r == 0
    rb0 = row0 // r
    specs = _lru_specs(l, r, cw, ncw, lb, ux_col // cw, (ux_col + width) // cw,
                       lambda ri, cb: rb0 + ri, lambda ri, cb: cb)
    st = pl.BlockSpec((None, r, cw), lambda ri, cb: (l, ri, cb))
    f32rows = pl.BlockSpec((r, cw), lambda ri, cb: (ri, cb))
    return _pcall(
        _lru_sample_kernel, name="lru_sample", grid=(ms // r, width // cw),
        in_specs=specs + [st, st],
        args=(proj, proj, conv_w, conv_b3, w_r, b_r3, w_i, b_i3, lam3, h0_rep, buf_fr),
        out_specs=[pl.BlockSpec((r, cw), lambda ri, cb: (rb0 + ri, cb)), f32rows, f32rows],
        out_shape=[jax.ShapeDtypeStruct(yb_prev.shape, BF16),
                   jax.ShapeDtypeStruct((ms, width), F32), jax.ShapeDtypeStruct((ms, width), F32)],
        prev=(yb_prev,),
        sem=("arbitrary", "arbitrary"))


def _pack_pair(hi, lo):
    hb = lax.bitcast_convert_type(hi.astype(jnp.bfloat16).astype(F32), jnp.uint32)
    lb = lax.bitcast_convert_type(lo.astype(jnp.bfloat16).astype(F32), jnp.uint32)
    return hb | (lb >> 16)


def _unpack_pair(p):
    hi = lax.bitcast_convert_type(p & jnp.uint32(0xFFFF0000), F32)
    lo = lax.bitcast_convert_type(p << 16, F32)
    return hi, lo


def _router_kernel(x_ref, g_ref, wr_ref, o_ref, hp_ref, *, ne):
    hf = _rms(x_ref[...], g_ref[...])
    half = hf.shape[1] // 2
    hp_ref[...] = _pack_pair(hf[:, :half], hf[:, half:])
    h = hf.astype(BF16)
    logits = jnp.dot(h, wr_ref[...].astype(BF16), preferred_element_type=F32)
    lane = lax.broadcasted_iota(jnp.int32, logits.shape, 1).astype(F32)
    neg = F32(-jnp.inf)
    big = F32(LANES)
    l1 = jnp.where(lane < ne, logits, neg)
    m1 = jnp.max(l1, axis=-1, keepdims=True)
    i1 = jnp.min(jnp.where(l1 == m1, lane, big), axis=-1, keepdims=True)
    l2 = jnp.where(lane == i1, neg, l1)
    m2 = jnp.max(l2, axis=-1, keepdims=True)
    i2 = jnp.min(jnp.where(l2 == m2, lane, big), axis=-1, keepdims=True)
    e2 = jnp.exp(m2 - m1)
    den = 1.0 + e2
    o_ref[...] = (jnp.where(lane == 0.0, i1, 0.0) + jnp.where(lane == 1.0, i2, 0.0)
                  + jnp.where(lane == 2.0, 1.0 / den, 0.0) + jnp.where(lane == 3.0, e2 / den, 0.0))


def _router(x, g3, l, wr):
    m, d = x.shape
    ne = wr.shape[-1]
    assert TOP_K == 2 and ne <= LANES
    wr_pad = jnp.pad(wr, ((0, 0), (0, LANES - ne)))
    tm = _pick(m, (512, 256, 128, 64, 32, 16, 8))
    return _pcall(
        functools.partial(_router_kernel, ne=ne), name="router", grid=(m // tm,),
        in_specs=[pl.BlockSpec((tm, d), lambda i: (i, 0)), _vspec(l, d),
                  pl.BlockSpec((d, LANES), lambda i: (0, 0))],
        args=(x, g3, wr_pad),
        out_specs=[pl.BlockSpec((tm, LANES), lambda i: (i, 0)),
                   pl.BlockSpec((tm, d // 2), lambda i: (i, 0))],
        out_shape=[jax.ShapeDtypeStruct((m, LANES), F32),
                   jax.ShapeDtypeStruct((m, d // 2), jnp.uint32)],
        sem=("arbitrary",))


def _dispatch_tables(rout, ne, tg):
    m = rout.shape[0]
    na = TOP_K * m
    eid = rout[:, :TOP_K].astype(jnp.int32).T.reshape(na)
    onehot = (eid[:, None] == jnp.arange(ne, dtype=jnp.int32)[None, :]).astype(jnp.int32)
    cnt = jnp.sum(onehot, axis=0)
    rank = jnp.sum((jnp.cumsum(onehot, axis=0) - onehot) * onehot, axis=1)
    padded = ((cnt + tg - 1) // tg) * tg
    gend = jnp.cumsum(padded)
    gstart = gend - padded
    rem = cnt % tg
    first = jnp.where(rem > 0, rem, tg)
    local = jnp.where(rank < first[eid], rank, rank - first[eid] + tg)
    pos = gstart[eid] + local
    p_rows = ((na + tg - 1) // tg + ne) * tg
    n_tiles = p_rows // tg
    tok = jnp.zeros((p_rows,), jnp.int32).at[pos].set(jnp.arange(na, dtype=jnp.int32) % m)
    n_used = (gend[-1] // tg).astype(jnp.int32)
    tstart = jnp.arange(n_tiles, dtype=jnp.int32) * tg
    tstart = jnp.minimum(tstart, (n_used - 1) * tg)
    te = jnp.minimum(jnp.searchsorted(gend, tstart, side="right"), ne - 1).astype(jnp.int32)
    tidx = jnp.arange(n_tiles, dtype=jnp.int32)
    n_valid = jnp.where(tidx * tg == gstart[te], first[te], tg)
    n_valid = jnp.where(tidx < n_used, n_valid, 0).astype(jnp.int32)
    return pos, tok, te, n_used.reshape(1), n_valid, p_rows


DMA_UNROLL = 8


def _row_gather(src_hbm, dst, idx_of, sem):
    rows = dst.shape[0]
    assert rows % DMA_UNROLL == 0

    def row_copy(r, t):
        return pltpu.make_async_copy(src_hbm.at[pl.ds(t, 1), :], dst.at[pl.ds(r, 1), :], sem)

    def issue(c, carry):
        for u in range(DMA_UNROLL):
            r = c * DMA_UNROLL + u
            row_copy(r, idx_of(r)).start()
        return carry

    def drain(c, carry):
        for u in range(DMA_UNROLL):
            row_copy(c * DMA_UNROLL + u, 0).wait()
        return carry

    start = lambda: lax.fori_loop(0, rows // DMA_UNROLL, issue, 0)
    wait = lambda: lax.fori_loop(0, rows // DMA_UNROLL, drain, 0)
    return start, wait


def _moe_gather_kernel(tok_ref, gv_ref, hp_ref, o_ref):
    tr, d = o_ref.shape
    half = d // 2
    gv = gv_ref[0, 0]
    group = 2 * SUBLANES

    @pl.when(gv < tr)
    def _():
        o_ref[...] = jnp.zeros_like(o_ref)

    def body(g, carry):
        r0 = pl.multiple_of(g * group, group)
        rows = [hp_ref[pl.ds(tok_ref[0, r0 + u], 1), :] for u in range(group)]
        hi, lo = _unpack_pair(jnp.concatenate(rows, axis=0))
        o_ref[pl.ds(r0, group), pl.ds(0, half)] = hi.astype(o_ref.dtype)
        o_ref[pl.ds(r0, group), pl.ds(half, half)] = lo.astype(o_ref.dtype)
        return carry

    lax.fori_loop(0, (gv + group - 1) // group, body, 0)


def _moe_gather(hp, tok, n_valid, tg, p_rows):
    m, half = hp.shape
    tr = _pick(tg, (GATHER_ROWS, 128, 64, 32, 16))
    nt = p_rows // tr
    per = tg // tr
    gidx = jnp.arange(nt, dtype=jnp.int32)
    gv = jnp.clip(n_valid[gidx // per] - (gidx % per) * tr, 0, tr).astype(jnp.int32)
    return _pcall(
        _moe_gather_kernel, name="moe_gather", grid=(nt,),
        in_specs=[pl.BlockSpec((None, 1, tr), lambda i: (i, 0, 0), memory_space=pltpu.SMEM),
                  pl.BlockSpec((None, 1, 1), lambda i: (i, 0, 0), memory_space=pltpu.SMEM),
                  pl.BlockSpec((m, half), lambda i: (0, 0), pipeline_mode=pl.Buffered(1))],
        args=(tok.reshape(nt, 1, tr), gv.reshape(nt, 1, 1), hp),
        out_specs=[pl.BlockSpec((tr, 2 * half), lambda i: (i, 0))],
        out_shape=[jax.ShapeDtypeStruct((p_rows, 2 * half), BF16)],
        sem=("arbitrary",))[0]


def _tile_changed(te_ref):
    ti = pl.program_id(1)
    return jnp.logical_or(ti == 0, te_ref[ti] != te_ref[jnp.maximum(ti - 1, 0)])


def _valid_row_chunks(nv, o_ref, chunk):
    tg = o_ref.shape[0]

    @pl.when(nv == tg)
    def _():
        _row_chunks(tg, chunk)

    @pl.when(nv < tg)
    def _():
        def maybe(rs):
            @pl.when(rs.start < nv)
            def _():
                chunk(rs)

            @pl.when(rs.start >= nv)
            def _():
                o_ref[rs, :] = jnp.zeros((rs.stop - rs.start, o_ref.shape[1]), o_ref.dtype)

        _row_chunks(tg, maybe)


def _gmm_swiglu_kernel(te_ref, nu_ref, nv_ref, x_ref, wg_ref, wu_ref, o_ref, wgb_ref, wub_ref):
    @pl.when(_tile_changed(te_ref))
    def _():
        _cast_w(wg_ref, wgb_ref)
        _cast_w(wu_ref, wub_ref)

    def chunk(rs):
        x = x_ref[rs, :]
        g = jnp.dot(x, wgb_ref[...], preferred_element_type=F32)
        u = jnp.dot(x, wub_ref[...], preferred_element_type=F32)
        o_ref[rs, :] = (jax.nn.silu(g) * u).astype(o_ref.dtype)

    _valid_row_chunks(nv_ref[pl.program_id(1)], o_ref, chunk)


def _gmm_down_kernel(te_ref, nu_ref, nv_ref, x_ref, w_ref, o_ref, wb_ref):
    @pl.when(_tile_changed(te_ref))
    def _():
        _cast_w(w_ref, wb_ref)

    hw = o_ref.shape[1]

    def chunk(rs):
        y = jnp.dot(x_ref[rs, :], wb_ref[...], preferred_element_type=F32)
        o_ref[rs, :] = _pack_pair(y[:, :hw], y[:, hw:])

    _valid_row_chunks(nv_ref[pl.program_id(1)], o_ref, chunk)


def _gmm_specs(jm, tg, k, tn, j0=0):
    xrow = lambda j, ti, te, nu, nv: (jnp.minimum(ti, nu[0] - 1), 0)
    wsp = pl.BlockSpec((None, None, k, tn), lambda j, ti, te, nu, nv: (jm, te[ti], 0, j0 + j))
    return pl.BlockSpec((tg, k), xrow), wsp, pl.BlockSpec((tg, tn), lambda j, ti, te, nu, nv: (ti, j0 + j))


def _gmm_swiglu(xg, wg, wu, jm, te, n_used, n_valid, tg, tns):
    p, k = xg.shape
    n = wg.shape[-1]
    assert sum(tn * cnt for tn, cnt in tns) == n
    out, col = None, 0
    for tn, cnt in tns:
        assert col % tn == 0
        xs, ws, os_ = _gmm_specs(jm, tg, k, tn, col // tn)
        out = _pcall(
            _gmm_swiglu_kernel, name="gmm_swiglu", grid=(cnt, p // tg), num_scalar_prefetch=3,
            in_specs=[None, None, None, xs, ws, ws], args=(te, n_used, n_valid, xg, wg, wu),
            out_specs=[os_], out_shape=[jax.ShapeDtypeStruct((p, n), BF16)],
            scratch=[pltpu.VMEM((k, tn), BF16), pltpu.VMEM((k, tn), BF16)],
            prev=(out,), sem=("arbitrary", "arbitrary"))[0]
        col += tn * cnt
    return out


def _col_tiling(n, widths):
    out, col = [], 0
    for wdt in widths:
        cnt = (n - col) // wdt
        if cnt:
            out.append((wdt, cnt))
            col += cnt * wdt
    assert col == n
    return tuple(out)


def _gmm_down(ug, wd, jm, te, n_used, n_valid, tg, tn):
    p, k = ug.shape
    n = wd.shape[-1]
    xs, ws, _ = _gmm_specs(jm, tg, k, tn)
    return _pcall(
        _gmm_down_kernel, name="gmm_down", grid=(n // tn, p // tg), num_scalar_prefetch=3,
        in_specs=[None, None, None, xs, ws], args=(te, n_used, n_valid, ug, wd),
        out_specs=[pl.BlockSpec((tg, tn // 2), lambda j, ti, te, nu, nv: (ti, j))],
        out_shape=[jax.ShapeDtypeStruct((p, n // 2), jnp.uint32)],
        scratch=[pltpu.VMEM((k, tn), BF16)],
        sem=("arbitrary", "arbitrary"))[0]


def _moe_combine_kernel(pos_ref, nxt_ref, x_ref, rw_ref, g_ref, yg_hbm, *rest, tn, np_tiles):
    outs, (buf, sem) = rest[:-2], rest[-2:]
    tc = x_ref.shape[0]
    i = pl.program_id(0)
    slot = lax.rem(i, 2)

    def gathers(idx_ref, s):
        return [_row_gather(yg_hbm, buf.at[s, k], lambda r, k=k: idx_ref[0, k * tc + r], sem.at[s])
                for k in range(TOP_K)]

    @pl.when(i == 0)
    def _():
        for start, _ in gathers(pos_ref, 0):
            start()

    @pl.when(i + 1 < pl.num_programs(0))
    def _():
        for start, _ in gathers(nxt_ref, 1 - slot):
            start()

    for _, wait in gathers(pos_ref, slot):
        wait()
    rows = buf[slot]

    def expand(p):
        hw = tn // 2
        parts = []
        for jj in range(p.shape[1] // hw):
            parts.extend(_unpack_pair(p[:, jj * hw:(jj + 1) * hw]))
        return jnp.concatenate(parts, axis=-1)

    rw = rw_ref[...]
    y = rw[:, TOP_K:TOP_K + 1] * expand(rows[0])
    for k in range(1, TOP_K):
        y = y + rw[:, TOP_K + k:TOP_K + k + 1] * expand(rows[k])
    xn = x_ref[...] + y
    if np_tiles is None:
        outs[0][...] = xn
    else:
        yn = _rms(xn, g_ref[...])

        @pl.when(i < np_tiles)
        def _():
            outs[0][...] = yn

        @pl.when(i >= np_tiles)
        def _():
            outs[1][...] = yn


def _moe_combine(x, rout, pos, ygp, tn, g3=None, mp=None):
    m, d = x.shape
    tc = _pick(m, (COMBINE_ROWS, 64, 32, 16, 8))
    nt = m // tc
    pos_t = pos.reshape(TOP_K, nt, tc).transpose(1, 0, 2).reshape(nt, 1, TOP_K * tc)
    if g3 is None:
        g3 = jnp.ones((1, 1, d), F32)
        np_tiles = None
        out_specs = [pl.BlockSpec((tc, d), lambda i: (i, 0))]
        out_shape = [jax.ShapeDtypeStruct((m, d), F32)]
    else:
        assert mp % tc == 0
        np_tiles = mp // tc
        out_specs = [pl.BlockSpec((tc, d), lambda i: (jnp.minimum(i, np_tiles - 1), 0)),
                     pl.BlockSpec((tc, d), lambda i: (jnp.maximum(i - np_tiles, 0), 0))]
        out_shape = [jax.ShapeDtypeStruct((mp, d), F32), jax.ShapeDtypeStruct((m - mp, d), F32)]
    return _pcall(
        functools.partial(_moe_combine_kernel, tn=tn, np_tiles=np_tiles), name="moe_combine", grid=(nt,),
        in_specs=[pl.BlockSpec((None, 1, TOP_K * tc), lambda i: (i, 0, 0), memory_space=pltpu.SMEM),
                  pl.BlockSpec((None, 1, TOP_K * tc), lambda i: (jnp.minimum(i + 1, nt - 1), 0, 0),
                               memory_space=pltpu.SMEM),
                  pl.BlockSpec((tc, d), lambda i: (i, 0)),
                  pl.BlockSpec((tc, LANES), lambda i: (i, 0)),
                  _vspec(0, d),
                  pl.BlockSpec(memory_space=pl.ANY)],
        args=(pos_t, pos_t, x, rout, g3, ygp),
        out_specs=out_specs, out_shape=out_shape,
        scratch=[pltpu.VMEM((2, TOP_K, tc, d // 2), jnp.uint32), pltpu.SemaphoreType.DMA((2,))],
        sem=("arbitrary",))


def kernel(x_prompt, x_sample, state_ret, state_lru, state_conv, norm_mix, w_in, ret_gn, w_ret_o, conv_w, conv_b, w_rgate, b_rgate, w_igate, b_igate, lru_lambda, w_lru_o, w_out, norm_ffn, ffn_w_gate, ffn_w_up, ffn_w_down, moe_router, moe_w_gate, moe_w_up, moe_w_down, norm_final):
    bp, tp, d = x_prompt.shape
    bs, ts, _ = x_sample.shape
    depth, _, nh, dk, dv = state_ret.shape
    width = state_lru.shape[-1]
    ncw = conv_w.shape[1]
    ne = moe_router.shape[-1]
    assert dk == dv and nh * dk == width == d and ts == SUBLANES and ncw - 1 <= SUBLANES
    mp, ms = bp * tp, bs * ts
    m = mp + ms
    g_col = 2 * nh * dk + nh * dv
    ux_col = g_col + nh * dv
    ga_col = ux_col + 2 * width

    tm = _pick(m, (1024, 512, 256, 128, 64, 32, 16, 8))
    tm_big = _pick(m, (1536, 1024, 512, 256, 128, 64, 32, 16, 8))
    tm_half = _pick(m, (512, 256, 128, 64, 32, 16, 8))
    tn_of = lambda n: _pick(n, (1024, 512, 256, 128))

    vec3 = lambda a: a.reshape(a.shape[0], 1, a.shape[-1])
    norm_mix3, norm_ffn3, ret_gn3 = vec3(norm_mix), vec3(norm_ffn), vec3(ret_gn)
    norm_final3 = norm_final.reshape(1, 1, d)
    lru_w = (conv_w, vec3(conv_b), w_rgate, vec3(b_rgate), w_igate, vec3(b_igate), vec3(lru_lambda))
    h0_rep = jnp.repeat(state_lru, ts, axis=1)
    buf_fr = jnp.pad(state_conv, ((0, 0), (0, 0), (ts - (ncw - 1), 0), (0, 0))).reshape(depth, ms, width)

    cos_p, sin_p = _rope_tables(tp, dk // 2, 0)
    cos_s, sin_s = _rope_tables(ts, dk // 2, PAST_LEN)
    cos = jnp.concatenate([jnp.tile(cos_p, (bp, 1)), jnp.tile(cos_s, (bs, 1))], axis=0)
    sin = jnp.concatenate([jnp.tile(sin_p, (bp, 1)), jnp.tile(sin_s, (bs, 1))], axis=0)
    tn_in = _pick(nh * dk, (1024, 512, 256))

    x = (x_prompt.reshape(mp, d), x_sample.reshape(ms, d))
    if mp % tm_half or ms % tm_half:
        x = jnp.concatenate(x, axis=0)
    ret_p = ret_s = y_p = y_s = None
    lrus_p, lrus_s, convs_p, convs_s = [], [], [], []
    for l in range(depth):
        if isinstance(x, tuple):
            h = _rmsnorm(x[0], norm_mix3, l, BF16, out_rows=m)
            h = _rmsnorm(x[1], norm_mix3, l, BF16, out_row0=mp, out_rows=m, prev=h)
        else:
            h = _rmsnorm(x, norm_mix3, l, BF16)
        proj = _mm_inproj(h, w_in, (l,), cos, sin, 2 * nh * dk, dk, (g_col, ux_col),
                          (ux_col + width, ga_col), tm_big, tn_in)

        ya, ret_p = _ret_prompt(proj, ret_gn3, l, depth, bp, tp, nh, dk, dv, ret_p)
        if _fused_seqs_per_step(width, tp, bp, bs, ts) is None:
            ya, ret_s = _ret_sample(proj, ret_gn3, state_ret, l, mp, bs, ts, nh, dk, dv, ya, ret_s)
            yb, hl_p, ul_p = _lru_prompt(proj, ux_col, l, lru_w, bp, tp)
        else:
            yb, hl_p, ul_p, ya, ret_s = _lru_prompt_ret_sample(
                proj, ux_col, l, lru_w, bp, tp, ret_gn3, state_ret, mp, bs, ts, nh, dk, dv, ya, ret_s)
        yb, hl_s, us_s = _lru_sample(proj, ux_col, l, lru_w, mp, h0_rep, buf_fr, yb)
        lrus_p.append(hl_p[:, -1])
        lrus_s.append(hl_s.reshape(bs, ts, width)[:, -1])
        convs_p.append(ul_p[:, SUBLANES - (ncw - 1):])
        convs_s.append(us_s.reshape(bs, ts, width)[:, ts - (ncw - 1):])

        z = _mm_merge(ya, yb, w_ret_o, w_lru_o, (l,), proj, ga_col, ga_col + d, tm, tn_of(d) // 2)
        x = _mm_resid(z, w_out, (l,), x, tm_half if isinstance(x, tuple) else tm, tn_of(d))
        j = l // 2
        if l % 2 == 0:
            h2 = _rmsnorm(x, norm_ffn3, l, BF16)
            ff = ffn_w_gate.shape[-1]
            u = _mm_swiglu(h2, ffn_w_gate, ffn_w_up, (j,), tm_big, _pick(ff, (512, 256, 128)))
            x = _mm_resid(u, ffn_w_down, (j,), x, tm_half, _pick(d, (512, 256, 128)))
        else:
            fe = moe_w_gate.shape[-1]
            tg = _pick(TOP_K * m, (MOE_TILE, 256, 128, 64, 32, 16))
            tn_d = _pick(d, (1024, 512, 256))
            rout, hp = _router(x, norm_ffn3, l, moe_router[j])
            pos, tok, te, n_used, n_valid, p_rows = _dispatch_tables(rout, ne, tg)
            xg = _moe_gather(hp, tok, n_valid, tg, p_rows)
            ug = _gmm_swiglu(xg, moe_w_gate, moe_w_up, j, te, n_used, n_valid, tg,
                             _col_tiling(fe, (1024, 512, 256, 128)))
            ygp = _gmm_down(ug, moe_w_down, j, te, n_used, n_valid, tg, tn_d)
            if l == depth - 1:
                y_p, y_s = _moe_combine(x, rout, pos, ygp, tn_d, norm_final3, mp)
            else:
                x = _moe_combine(x, rout, pos, ygp, tn_d)[0]
    if y_p is None:
        y_p = _rmsnorm(x, norm_final3, 0, F32, 0, mp)
        y_s = _rmsnorm(x, norm_final3, 0, F32, mp, ms)
    return (y_p.reshape(bp, tp, d), y_s.reshape(bs, ts, d),
            ret_p, jnp.stack(lrus_p), jnp.stack(convs_p),
            ret_s, jnp.stack(lrus_s), jnp.stack(convs_s))
```

```python
import functools

import jax
import jax.numpy as jnp
from jax import lax
from jax.experimental import pallas as pl
from jax.experimental.pallas import tpu as pltpu

F32 = jnp.float32
BF16 = jnp.bfloat16

ROPE_BASE = 10000.0
LRU_C = 8.0
EPS = 1e-6
RET_CHUNK = 128
PAST_LEN = 16384
TOP_K = 2
SUBLANES = 8
LANES = 128
VMEM_LIMIT = 56 * 1024 * 1024
MOE_TILE = 1024
GATHER_ROWS = 1024
COMBINE_ROWS = 512


def _pick(n, prefs):
    for p in prefs:
        if n % p == 0:
            return p
    return n


def _params(*sem):
    return pltpu.CompilerParams(dimension_semantics=sem, vmem_limit_bytes=VMEM_LIMIT)


def _pcall(body, *, name, grid, in_specs, args, out_specs, out_shape, sem, scratch=(), prev=(),
           num_scalar_prefetch=0):
    n_in = len(args)
    prev = tuple(prev) + (None,) * (len(out_shape) - len(prev))
    extra = [(oi, p) for oi, p in enumerate(prev) if p is not None]
    aliases = {n_in + e: oi for e, (oi, _) in enumerate(extra)}
    if extra:
        inner = body

        def body(*refs):
            return inner(*refs[:n_in], *refs[n_in + len(extra):])

    in_specs = list(in_specs) + [pl.BlockSpec(memory_space=pl.ANY)] * len(extra)
    if num_scalar_prefetch:
        grid_spec = pltpu.PrefetchScalarGridSpec(
            num_scalar_prefetch=num_scalar_prefetch, grid=grid, in_specs=in_specs[num_scalar_prefetch:],
            out_specs=tuple(out_specs), scratch_shapes=list(scratch))
        call = pl.pallas_call(body, grid_spec=grid_spec, out_shape=tuple(out_shape),
                              input_output_aliases=aliases, compiler_params=_params(*sem), name=name)
    else:
        call = pl.pallas_call(body, grid=grid, in_specs=in_specs, out_specs=tuple(out_specs),
                              out_shape=tuple(out_shape), scratch_shapes=list(scratch),
                              input_output_aliases=aliases, compiler_params=_params(*sem), name=name)
    return call(*args, *[p for _, p in extra])


def _wspec(pre, k, tn):
    return pl.BlockSpec((None,) * len(pre) + (k, tn), lambda j, i: pre + (0, j))


def _vspec(l, n):
    return pl.BlockSpec((None, 1, n), lambda *_: (l, 0, 0))


def _rms(x, g):
    return (x * lax.rsqrt(jnp.mean(x * x, axis=-1, keepdims=True) + EPS)) * g


def _rmsnorm_kernel(x_ref, g_ref, o_ref):
    o_ref[...] = _rms(x_ref[...], g_ref[...]).astype(o_ref.dtype)


def _rmsnorm(x, g3, l, out_dtype, row0=0, rows=None, out_row0=0, out_rows=None, prev=None):
    d = x.shape[1]
    rows = x.shape[0] if rows is None else rows
    out_rows = rows if out_rows is None else out_rows
    tm = _pick(rows, (512, 256, 128, 64, 32, 16, 8))
    assert row0 % tm == 0 and out_row0 % tm == 0
    rb0, ob0 = row0 // tm, out_row0 // tm
    return _pcall(
        _rmsnorm_kernel, name="rmsnorm", grid=(rows // tm,),
        in_specs=[pl.BlockSpec((tm, d), lambda i: (rb0 + i, 0)), _vspec(l, d)],
        args=(x, g3),
        out_specs=[pl.BlockSpec((tm, d), lambda i: (ob0 + i, 0))],
        out_shape=[jax.ShapeDtypeStruct((out_rows, d), out_dtype)],
        prev=(prev,), sem=("arbitrary",))[0]


def _cast_w(w_ref, wb_ref):
    k = w_ref.shape[0]
    ck = _pick(k, (256, 128, 64, 32, 16))

    def body(c, carry):
        r = pl.multiple_of(c * ck, ck)
        wb_ref[pl.ds(r, ck), :] = w_ref[pl.ds(r, ck), :].astype(BF16)
        return carry

    lax.fori_loop(0, k // ck, body, 0)


def _mm_inproj_kernel(x_ref, w_ref, cos_ref, sin_ref, o_ref, wb_ref, *, nrot, dk, silu_tiles, gelu_tiles):
    j = pl.program_id(0)

    @pl.when(pl.program_id(1) == 0)
    def _():
        _cast_w(w_ref, wb_ref)

    tm, tn = o_ref.shape
    nchunk = 4 if tm % (4 * 16) == 0 else 1

    def row_chunks(epilogue):
        rc = tm // nchunk
        for c in range(nchunk):
            rs = slice(c * rc, (c + 1) * rc)
            epilogue(jnp.dot(x_ref[rs, :], wb_ref[...], preferred_element_type=F32), rs)

    def store(fn):
        def epilogue(acc, rs):
            o_ref[rs, :] = fn(acc).astype(o_ref.dtype)
        return epilogue

    in_tiles = lambda t: jnp.logical_and(j >= t[0], j < t[1])
    is_silu, is_gelu = in_tiles(silu_tiles), in_tiles(gelu_tiles)

    @pl.when(jnp.logical_and(j >= nrot, jnp.logical_not(jnp.logical_or(is_silu, is_gelu))))
    def _():
        row_chunks(store(lambda acc: acc))

    @pl.when(is_silu)
    def _():
        row_chunks(store(jax.nn.silu))

    @pl.when(is_gelu)
    def _():
        row_chunks(store(jax.nn.gelu))

    @pl.when(j < nrot)
    def _():
        scale = jnp.where(j >= nrot // 2, F32(dk ** -0.5), F32(1.0))
        half = dk // 2

        def rotary(acc, rs):
            cos = cos_ref[rs, :] * scale
            sin = sin_ref[rs, :] * scale
            for h in range(tn // dk):
                x1 = acc[:, h * dk:h * dk + half]
                x2 = acc[:, h * dk + half:(h + 1) * dk]
                o_ref[rs, h * dk:h * dk + half] = (x1 * cos - x2 * sin).astype(o_ref.dtype)
                o_ref[rs, h * dk + half:(h + 1) * dk] = (x1 * sin + x2 * cos).astype(o_ref.dtype)

        row_chunks(rotary)


def _mm_inproj(x, w, pre, cos, sin, qk_cols, dk, silu_cols, gelu_cols, tm, tn):
    m, k = x.shape
    n = w.shape[-1]
    half = cos.shape[1]
    assert qk_cols % (2 * tn) == 0 and tn % dk == 0
    assert all(c % tn == 0 for c in silu_cols + gelu_cols)
    tiles = lambda cols: (cols[0] // tn, cols[1] // tn)
    return _pcall(
        functools.partial(_mm_inproj_kernel, nrot=qk_cols // tn, dk=dk, silu_tiles=tiles(silu_cols),
                          gelu_tiles=tiles(gelu_cols)), name="mm_inproj",
        grid=(n // tn, m // tm),
        in_specs=[pl.BlockSpec((tm, k), lambda j, i: (i, 0)), _wspec(pre, k, tn),
                  pl.BlockSpec((tm, half), lambda j, i: (i, 0)),
                  pl.BlockSpec((tm, half), lambda j, i: (i, 0))],
        args=(x, w, cos, sin),
        out_specs=[pl.BlockSpec((tm, tn), lambda j, i: (i, j))],
        out_shape=[jax.ShapeDtypeStruct((m, n), BF16)],
        scratch=[pltpu.VMEM((k, tn), BF16)],
        sem=("arbitrary", "arbitrary"))[0]


def _mm_resid_kernel(x_ref, w_ref, r_ref, o_ref, wb_ref):
    @pl.when(pl.program_id(1) == 0)
    def _():
        _cast_w(w_ref, wb_ref)

    o_ref[...] = r_ref[...] + jnp.dot(x_ref[...], wb_ref[...], preferred_element_type=F32)


def _mm_resid2_kernel(x_ref, w_ref, ra_ref, rb_ref, o_ref, wb_ref, *, na):
    @pl.when(pl.program_id(1) == 0)
    def _():
        _cast_w(w_ref, wb_ref)

    y = jnp.dot(x_ref[...], wb_ref[...], preferred_element_type=F32)
    i = pl.program_id(1)

    @pl.when(i < na)
    def _():
        o_ref[...] = ra_ref[...] + y

    @pl.when(i >= na)
    def _():
        o_ref[...] = rb_ref[...] + y


def _mm_resid(x, w, pre, r, tm, tn):
    m, k = x.shape
    n = w.shape[-1]
    if isinstance(r, tuple):
        ra, rb = r
        assert ra.shape[0] % tm == 0 and rb.shape[0] % tm == 0 and ra.shape[0] + rb.shape[0] == m
        na = ra.shape[0] // tm
        body = functools.partial(_mm_resid2_kernel, na=na)
        r_specs = [pl.BlockSpec((tm, tn), lambda j, i: (jnp.minimum(i, na - 1), j)),
                   pl.BlockSpec((tm, tn), lambda j, i: (jnp.maximum(i - na, 0), j))]
        r_args = (ra, rb)
    else:
        body, r_specs, r_args = _mm_resid_kernel, [pl.BlockSpec((tm, tn), lambda j, i: (i, j))], (r,)
    return _pcall(
        body, name="mm_resid", grid=(n // tn, m // tm),
        in_specs=[pl.BlockSpec((tm, k), lambda j, i: (i, 0)), _wspec(pre, k, tn)] + r_specs,
        args=(x, w) + r_args,
        out_specs=[pl.BlockSpec((tm, tn), lambda j, i: (i, j))],
        out_shape=[jax.ShapeDtypeStruct((m, n), F32)],
        scratch=[pltpu.VMEM((k, tn), BF16)],
        sem=("arbitrary", "arbitrary"))[0]


def _row_chunks(rows, fn, nchunk=4):
    n = nchunk if rows % (nchunk * 16) == 0 else 1
    rc = rows // n
    for c in range(n):
        fn(slice(c * rc, (c + 1) * rc))


def _swiglu_rows(x_ref, wgb_ref, wub_ref, o_ref):
    def chunk(rs):
        x = x_ref[rs, :]
        g = jnp.dot(x, wgb_ref[...], preferred_element_type=F32)
        u = jnp.dot(x, wub_ref[...], preferred_element_type=F32)
        o_ref[rs, :] = (jax.nn.silu(g) * u).astype(o_ref.dtype)

    _row_chunks(o_ref.shape[0], chunk)


def _mm_swiglu_kernel(x_ref, wg_ref, wu_ref, o_ref, wgb_ref, wub_ref):
    @pl.when(pl.program_id(1) == 0)
    def _():
        _cast_w(wg_ref, wgb_ref)
        _cast_w(wu_ref, wub_ref)

    _swiglu_rows(x_ref, wgb_ref, wub_ref, o_ref)


def _mm_swiglu(x, wg, wu, pre, tm, tn):
    m, k = x.shape
    n = wg.shape[-1]
    return _pcall(
        _mm_swiglu_kernel, name="mm_swiglu", grid=(n // tn, m // tm),
        in_specs=[pl.BlockSpec((tm, k), lambda j, i: (i, 0)), _wspec(pre, k, tn), _wspec(pre, k, tn)],
        args=(x, wg, wu),
        out_specs=[pl.BlockSpec((tm, tn), lambda j, i: (i, j))],
        out_shape=[jax.ShapeDtypeStruct((m, n), BF16)],
        scratch=[pltpu.VMEM((k, tn), BF16), pltpu.VMEM((k, tn), BF16)],
        sem=("arbitrary", "arbitrary"))[0]


def _mm_merge_kernel(a_ref, b_ref, wa_ref, wb_ref, ga_ref, gb_ref, o_ref, wab_ref, wbb_ref):
    @pl.when(pl.program_id(1) == 0)
    def _():
        _cast_w(wa_ref, wab_ref)
        _cast_w(wb_ref, wbb_ref)

    def chunk(rs):
        ya = jnp.dot(a_ref[rs, :], wab_ref[...], preferred_element_type=F32)
        yb = jnp.dot(b_ref[rs, :], wbb_ref[...], preferred_element_type=F32)
        ga = ga_ref[rs, :].astype(F32)
        gb = gb_ref[rs, :].astype(F32)
        o_ref[rs, :] = (jax.nn.sigmoid(ga) * ya + jax.nn.sigmoid(gb) * yb).astype(o_ref.dtype)

    _row_chunks(o_ref.shape[0], chunk)


def _mm_merge(a, b, wa, wb, pre, proj, ga_col, gb_col, tm, tn):
    m, k = a.shape
    n = wa.shape[-1]
    ga_blk, gb_blk = ga_col // tn, gb_col // tn
    return _pcall(
        _mm_merge_kernel, name="mm_merge", grid=(n // tn, m // tm),
        in_specs=[pl.BlockSpec((tm, k), lambda j, i: (i, 0)),
                  pl.BlockSpec((tm, k), lambda j, i: (i, 0)),
                  _wspec(pre, k, tn), _wspec(pre, k, tn),
                  pl.BlockSpec((tm, tn), lambda j, i: (i, ga_blk + j)),
                  pl.BlockSpec((tm, tn), lambda j, i: (i, gb_blk + j))],
        args=(a, b, wa, wb, proj, proj),
        out_specs=[pl.BlockSpec((tm, tn), lambda j, i: (i, j))],
        out_shape=[jax.ShapeDtypeStruct((m, n), BF16)],
        scratch=[pltpu.VMEM((k, tn), BF16), pltpu.VMEM((k, tn), BF16)],
        sem=("arbitrary", "arbitrary"))[0]


def _rope_kernel(inv_ref, cos_ref, sin_ref, *, pos0):
    t, half = cos_ref.shape
    pos = lax.broadcasted_iota(jnp.int32, (t, half), 0).astype(F32) + F32(pos0)
    ang = pos * inv_ref[...]
    cos_ref[...] = jnp.cos(ang)
    sin_ref[...] = jnp.sin(ang)


def _rope_tables(t, half, pos0):
    inv = ROPE_BASE ** (-jnp.arange(half, dtype=F32) / half)
    return pl.pallas_call(
        functools.partial(_rope_kernel, pos0=pos0),
        out_shape=(jax.ShapeDtypeStruct((t, half), F32), jax.ShapeDtypeStruct((t, half), F32)),
        name="rope_tables",
    )(inv.reshape(1, half))


def _decay_tables(c, h, dk, dv):
    log_g = jnp.log1p(-jnp.exp2(-5.0 - jnp.arange(h, dtype=F32)))
    idx = jnp.arange(c)
    rel = idx[:, None] - idx[None, :]
    dmask = jnp.where(rel[None] >= 0,
                      jnp.exp(jnp.maximum(rel, 0)[None].astype(F32) * log_g[:, None, None]), 0.0)
    xi = jnp.exp((idx + 1).astype(F32)[None, :] * log_g[:, None])
    zeta = jnp.exp((c - 1 - idx).astype(F32)[None, :] * log_g[:, None])
    g_c = jnp.exp(c * log_g)
    xi_t = jnp.broadcast_to(xi[:, :, None], (h, c, dv))
    zeta_t = jnp.broadcast_to(zeta[:, :, None], (h, c, dk))
    return dmask, xi_t, zeta_t, g_c


def _ret_head(qb, kb, v, g, dm, xi, zt, gn, s, gc):
    scores = lax.dot_general(qb, kb, (((1,), (1,)), ((), ())), preferred_element_type=F32)
    intra = jnp.dot((scores * dm).astype(BF16), v, preferred_element_type=F32)
    cross = jnp.dot(qb, s.astype(BF16), preferred_element_type=F32) * xi
    kz = (kb.astype(F32) * zt).astype(BF16)
    upd = lax.dot_general(kz, v, (((0,), (0,)), ((), ())), preferred_element_type=F32)
    s_new = gc * s + upd
    o = intra + cross
    mu = jnp.mean(o, axis=-1, keepdims=True)
    d = o - mu
    var = jnp.mean(d * d, axis=-1, keepdims=True)
    y = d * lax.rsqrt(var + EPS) * gn
    return (g * y).astype(BF16), s_new


def _ret_prompt_kernel(gc_ref, q_ref, k_ref, v_ref, g_ref, dm_ref, xi_ref, zt_ref,
                       gn_ref, o_ref, s_ref, *, nh, dk, dv):
    @pl.when(pl.program_id(1) == 0)
    def _():
        s_ref[...] = jnp.zeros_like(s_ref)

    c = dm_ref.shape[1]
    for h in range(nh):
        s = s_ref[h]
        for ci in range(q_ref.shape[0] // c):
            rs = slice(ci * c, (ci + 1) * c)
            out, s = _ret_head(
                q_ref[rs, h * dk:(h + 1) * dk], k_ref[rs, h * dk:(h + 1) * dk],
                v_ref[rs, h * dv:(h + 1) * dv], g_ref[rs, h * dv:(h + 1) * dv].astype(F32),
                dm_ref[h], xi_ref[h], zt_ref[h], gn_ref[:, h * dv:(h + 1) * dv], s, gc_ref[h])
            o_ref[rs, h * dv:(h + 1) * dv] = out
        s_ref[h] = s


def _ret_prompt(proj, gn3, l, depth, b, t, nh, dk, dv, s_prev):
    cl = RET_CHUNK if t % RET_CHUNK == 0 else t
    cps = _pick(t // cl, (8, 4, 2, 1))
    c = cl * cps
    nc = t // c
    w = nh * dk
    m = proj.shape[0]
    dmask, xi_t, zeta_t, g_c = _decay_tables(cl, nh, dk, dv)
    row = lambda bi, ci: bi * nc + ci
    full3 = lambda bi, ci: (0, 0, 0)
    return _pcall(
        functools.partial(_ret_prompt_kernel, nh=nh, dk=dk, dv=dv), name="ret_prompt", grid=(b, nc),
        in_specs=[pl.BlockSpec(memory_space=pltpu.SMEM),
                  pl.BlockSpec((c, w), lambda bi, ci: (row(bi, ci), 0)),
                  pl.BlockSpec((c, w), lambda bi, ci: (row(bi, ci), 1)),
                  pl.BlockSpec((c, w), lambda bi, ci: (row(bi, ci), 2)),
                  pl.BlockSpec((c, w), lambda bi, ci: (row(bi, ci), 3)),
                  pl.BlockSpec((nh, cl, cl), full3),
                  pl.BlockSpec((nh, cl, dv), full3),
                  pl.BlockSpec((nh, cl, dk), full3),
                  _vspec(l, nh * dv)],
        args=(g_c, proj, proj, proj, proj, dmask, xi_t, zeta_t, gn3),
        out_specs=[pl.BlockSpec((c, nh * dv), lambda bi, ci: (row(bi, ci), 0)),
                   pl.BlockSpec((None, None, nh, dk, dv), lambda bi, ci: (l, bi, 0, 0, 0))],
        out_shape=[jax.ShapeDtypeStruct((m, nh * dv), BF16),
                   jax.ShapeDtypeStruct((depth, b, nh, dk, dv), F32)],
        prev=(None, s_prev),
        sem=("arbitrary", "arbitrary"))


def _ret_sample_kernel(gc_ref, q_ref, k_ref, v_ref, g_ref, dm_ref, xi_ref, zt_ref,
                       gn_ref, s0_ref, o_ref, s_ref, *, nh, dk, dv, bb, ts):
    q = q_ref[...].astype(F32)
    k = k_ref[...].astype(F32)
    v = v_ref[...].astype(F32)
    g = g_ref[...].astype(F32)
    seqs = []
    for i in range(bb):
        r0, r1 = i * ts, (i + 1) * ts
        outs = []
        for h in range(nh):
            out, s_new = _ret_head(
                q[r0:r1, h * dk:(h + 1) * dk].astype(BF16), k[r0:r1, h * dk:(h + 1) * dk].astype(BF16),
                v[r0:r1, h * dv:(h + 1) * dv].astype(BF16), g[r0:r1, h * dv:(h + 1) * dv],
                dm_ref[h], xi_ref[h], zt_ref[h], gn_ref[:, h * dv:(h + 1) * dv],
                s0_ref[i, h], gc_ref[h])
            s_ref[i, h] = s_new
            outs.append(out.astype(F32))
        seqs.append(jnp.concatenate(outs, axis=-1))
    o_ref[...] = jnp.concatenate(seqs, axis=0).astype(o_ref.dtype)


def _ret_sample(proj, gn3, s0_all, l, row0, bs, ts, nh, dk, dv, ya_prev, s_prev):
    bb = _pick(bs, (4, 2, 1))
    rows = bb * ts
    w = nh * dk
    depth = s0_all.shape[0]
    dmask, xi_t, zeta_t, g_c = _decay_tables(ts, nh, dk, dv)
    assert row0 % rows == 0
    rb0 = row0 // rows
    full3 = lambda i: (0, 0, 0)
    return _pcall(
        functools.partial(_ret_sample_kernel, nh=nh, dk=dk, dv=dv, bb=bb, ts=ts), name="ret_sample",
        grid=(bs // bb,),
        in_specs=[pl.BlockSpec(memory_space=pltpu.SMEM),
                  pl.BlockSpec((rows, w), lambda i: (rb0 + i, 0)),
                  pl.BlockSpec((rows, w), lambda i: (rb0 + i, 1)),
                  pl.BlockSpec((rows, w), lambda i: (rb0 + i, 2)),
                  pl.BlockSpec((rows, w), lambda i: (rb0 + i, 3)),
                  pl.BlockSpec((nh, ts, ts), full3),
                  pl.BlockSpec((nh, ts, dv), full3),
                  pl.BlockSpec((nh, ts, dk), full3),
                  _vspec(l, nh * dv),
                  pl.BlockSpec((None, bb, nh, dk, dv), lambda i: (l, i, 0, 0, 0))],
        args=(g_c, proj, proj, proj, proj, dmask, xi_t, zeta_t, gn3, s0_all),
        out_specs=[pl.BlockSpec((rows, nh * dv), lambda i: (rb0 + i, 0)),
                   pl.BlockSpec((None, bb, nh, dk, dv), lambda i: (l, i, 0, 0, 0))],
        out_shape=[jax.ShapeDtypeStruct(ya_prev.shape, BF16),
                   jax.ShapeDtypeStruct((depth, bs, nh, dk, dv), F32)],
        prev=(ya_prev, s_prev),
        sem=("arbitrary",))


def _lru_gates(uc, wr_ref, br_ref, wi_ref, bi_ref, lam_ref):
    nb, lb, _ = wr_ref.shape
    ucb = uc.astype(BF16)
    rl, il = [], []
    for n in range(nb):
        xb = ucb[:, n * lb:(n + 1) * lb]
        rl.append(jnp.dot(xb, wr_ref[n].astype(BF16), preferred_element_type=F32))
        il.append(jnp.dot(xb, wi_ref[n].astype(BF16), preferred_element_type=F32))
    r = jax.nn.sigmoid(jnp.concatenate(rl, axis=-1) + br_ref[...])
    i = jax.nn.sigmoid(jnp.concatenate(il, axis=-1) + bi_ref[...])
    log_a = -LRU_C * r * jax.nn.softplus(-lam_ref[...])
    a = jnp.exp(log_a)
    z = -jnp.tanh(log_a) * (a * a + 1.0)
    bx = jnp.where(z == 0.0, 0.0, z * lax.rsqrt(z)) * (i * uc)
    return a, bx


def _blocks(x):
    return [x[lo:lo + SUBLANES, :] for lo in range(0, x.shape[0], SUBLANES)]


def _conv_blocks(u_blocks, prev_blocks, cw_ref, cb_ref, rows):
    ncw = cw_ref.shape[0]
    rolled = {}

    def rot(x, s):
        key = (id(x), s)
        if key not in rolled:
            rolled[key] = pltpu.roll(x, s, axis=0)
        return rolled[key]

    out = []
    for ub, pb in zip(u_blocks, prev_blocks):
        uc = cb_ref[...]
        for j in range(ncw):
            s = ncw - 1 - j
            term = ub if s == 0 else jnp.where(rows >= s, rot(ub, s), rot(pb, s))
            uc = uc + term * cw_ref[j:j + 1, :]
        out.append(uc)
    return jnp.concatenate(out, axis=0)


def _scan_block(a, bx, rows):
    d = 1
    while d < SUBLANES:
        m = rows >= d
        bx = bx + a * jnp.where(m, pltpu.roll(bx, d, axis=0), 0.0)
        a = a * jnp.where(m, pltpu.roll(a, d, axis=0), 1.0)
        d *= 2
    return a, bx


def _lru_prompt_kernel(ux_ref, uy_ref, cw_ref, cb_ref, wr_ref, br_ref, wi_ref, bi_ref, lam_ref,
                       o_ref, hl_ref, ul_ref, hc_ref):
    _lru_prompt_init(ul_ref, hc_ref)
    _lru_prompt_main(ux_ref, uy_ref, cw_ref, cb_ref, wr_ref, br_ref, wi_ref, bi_ref, lam_ref,
                     o_ref, hl_ref, ul_ref, hc_ref)


def _lru_prompt_init(ul_ref, hc_ref):
    @pl.when(pl.program_id(2) == 0)
    def _():
        hc_ref[...] = jnp.zeros_like(hc_ref)
        ul_ref[...] = jnp.zeros_like(ul_ref)


def _lru_prompt_main(ux_ref, uy_ref, cw_ref, cb_ref, wr_ref, br_ref, wi_ref, bi_ref, lam_ref,
                     o_ref, hl_ref, ul_ref, hc_ref):
    r, cw = ux_ref.shape
    rows = lax.broadcasted_iota(jnp.int32, (SUBLANES, cw), 0)
    ub = _blocks(ux_ref[...].astype(F32))
    uc = _conv_blocks(ub, [ul_ref[...]] + ub[:-1], cw_ref, cb_ref, rows)
    ul_ref[...] = ub[-1]
    a, bx = _lru_gates(uc, wr_ref, br_ref, wi_ref, bi_ref, lam_ref)
    carry = hc_ref[...]
    hs = []
    for ab, bb in zip(_blocks(a), _blocks(bx)):
        ab, bb = _scan_block(ab, bb, rows)
        hb = ab * carry + bb
        hs.append(hb)
        carry = jnp.broadcast_to(hb[SUBLANES - 1:SUBLANES, :], (SUBLANES, cw))
    hc_ref[...] = carry
    hl_ref[...] = hs[-1]
    h = jnp.concatenate(hs, axis=0)
    o_ref[...] = (h * uy_ref[...].astype(F32)).astype(o_ref.dtype)


def _lru_sample_kernel(ux_ref, uy_ref, cw_ref, cb_ref, wr_ref, br_ref, wi_ref, bi_ref, lam_ref,
                       h0_ref, buf_ref, o_ref, hl_ref, us_ref):
    r, cw = ux_ref.shape
    rows = lax.broadcasted_iota(jnp.int32, (SUBLANES, cw), 0)
    u = ux_ref[...].astype(F32)
    us_ref[...] = u
    uc = _conv_blocks(_blocks(u), _blocks(buf_ref[...]), cw_ref, cb_ref, rows)
    a, bx = _lru_gates(uc, wr_ref, br_ref, wi_ref, bi_ref, lam_ref)
    hs = []
    for ab, bb, h0 in zip(_blocks(a), _blocks(bx), _blocks(h0_ref[...])):
        ab, bb = _scan_block(ab, bb, rows)
        hs.append(ab * h0 + bb)
    h = jnp.concatenate(hs, axis=0)
    hl_ref[...] = h
    o_ref[...] = (h * uy_ref[...].astype(F32)).astype(o_ref.dtype)


def _lru_specs(l, r, cw, ncw, lb, ux_blk, uy_blk, row_of, cb_of):
    nbc = cw // lb
    vec = pl.BlockSpec((None, 1, cw), lambda *g: (l, 0, cb_of(*g)))
    gate = pl.BlockSpec((None, nbc, lb, lb), lambda *g: (l, cb_of(*g), 0, 0))
    return [pl.BlockSpec((r, cw), lambda *g: (row_of(*g), ux_blk + cb_of(*g))),
            pl.BlockSpec((r, cw), lambda *g: (row_of(*g), uy_blk + cb_of(*g))),
            pl.BlockSpec((None, ncw, cw), lambda *g: (l, 0, cb_of(*g))),
            vec, gate, vec, gate, vec, vec]


def _lru_prompt(proj, ux_col, l, lru_w, b, t):
    conv_w, conv_b3, w_r, b_r3, w_i, b_i3, lam3 = lru_w
    width = lam3.shape[-1]
    lb = w_r.shape[-1]
    ncw = conv_w.shape[1]
    m = proj.shape[0]
    cw = _pick(width, (1024, 512, 256, 128))
    r = _pick(t, (256, 128, 64, 32, 16, 8))
    nt = t // r
    specs = _lru_specs(l, r, cw, ncw, lb, ux_col // cw, (ux_col + width) // cw,
                       lambda bi, cb, ti: bi * nt + ti, lambda bi, cb, ti: cb)
    last8 = pl.BlockSpec((None, SUBLANES, cw), lambda bi, cb, ti: (bi, 0, cb))
    return _pcall(
        _lru_prompt_kernel, name="lru_prompt", grid=(b, width // cw, nt),
        in_specs=specs,
        args=(proj, proj, conv_w, conv_b3, w_r, b_r3, w_i, b_i3, lam3),
        out_specs=[pl.BlockSpec((r, cw), lambda bi, cb, ti: (bi * nt + ti, cb)), last8, last8],
        out_shape=[jax.ShapeDtypeStruct((m, width), BF16),
                   jax.ShapeDtypeStruct((b, SUBLANES, width), F32),
                   jax.ShapeDtypeStruct((b, SUBLANES, width), F32)],
        scratch=[pltpu.VMEM((SUBLANES, cw), F32)],
        sem=("arbitrary", "arbitrary", "arbitrary"))


def _lru_ret_kernel(*refs, nh, dk, dv, bb, ts):
    lru_in, ret_in = refs[:9], refs[9:19]
    yb_ref, hl_ref, ul_ref, ya_ref, s_ref, hc_ref = refs[19:]
    _lru_prompt_init(ul_ref, hc_ref)
    _ret_sample_kernel(*ret_in, ya_ref, s_ref, nh=nh, dk=dk, dv=dv, bb=bb, ts=ts)
    _lru_prompt_main(*lru_in, yb_ref, hl_ref, ul_ref, hc_ref)


def _fused_seqs_per_step(width, t, b, bs, ts):
    cw = _pick(width, (1024, 512, 256, 128))
    r = _pick(t, (256, 128, 64, 32, 16, 8))
    steps = b * (width // cw) * (t // r)
    if bs % steps:
        return None
    bb = bs // steps
    return bb if bb <= 4 and (bb * ts) % 16 == 0 else None


def _lru_prompt_ret_sample(proj, ux_col, l, lru_w, b, t, gn3, s0_all, row0, bs, ts, nh, dk, dv,
                           ya_prev, s_prev):
    conv_w, conv_b3, w_r, b_r3, w_i, b_i3, lam3 = lru_w
    width = lam3.shape[-1]
    lb = w_r.shape[-1]
    ncw = conv_w.shape[1]
    m = proj.shape[0]
    depth = s0_all.shape[0]
    cw = _pick(width, (1024, 512, 256, 128))
    r = _pick(t, (256, 128, 64, 32, 16, 8))
    nt, ncb = t // r, width // cw
    bb = _fused_seqs_per_step(width, t, b, bs, ts)
    rows = bb * ts
    w = nh * dk
    assert row0 % rows == 0
    rb0 = row0 // rows
    lin = lambda bi, cb, ti: (bi * ncb + cb) * nt + ti
    dmask, xi_t, zeta_t, g_c = _decay_tables(ts, nh, dk, dv)
    lru_specs = _lru_specs(l, r, cw, ncw, lb, ux_col // cw, (ux_col + width) // cw,
                           lambda bi, cb, ti: bi * nt + ti, lambda bi, cb, ti: cb)
    full3 = lambda *g: (0, 0, 0)
    qkvg = [pl.BlockSpec((rows, w), lambda *g, c=c: (rb0 + lin(*g), c)) for c in range(4)]
    state = pl.BlockSpec((None, bb, nh, dk, dv), lambda *g: (l, lin(*g), 0, 0, 0))
    ret_specs = ([pl.BlockSpec(memory_space=pltpu.SMEM)] + qkvg +
                 [pl.BlockSpec((nh, ts, ts), full3), pl.BlockSpec((nh, ts, dv), full3),
                  pl.BlockSpec((nh, ts, dk), full3), _vspec(l, nh * dv), state])
    last8 = pl.BlockSpec((None, SUBLANES, cw), lambda bi, cb, ti: (bi, 0, cb))
    return _pcall(
        functools.partial(_lru_ret_kernel, nh=nh, dk=dk, dv=dv, bb=bb, ts=ts), name="lru_prompt_ret_sample",
        grid=(b, ncb, nt),
        in_specs=lru_specs + ret_specs,
        args=(proj, proj, conv_w, conv_b3, w_r, b_r3, w_i, b_i3, lam3,
              g_c, proj, proj, proj, proj, dmask, xi_t, zeta_t, gn3, s0_all),
        out_specs=[pl.BlockSpec((r, cw), lambda bi, cb, ti: (bi * nt + ti, cb)), last8, last8,
                   pl.BlockSpec((rows, nh * dv), lambda *g: (rb0 + lin(*g), 0)), state],
        out_shape=[jax.ShapeDtypeStruct((m, width), BF16),
                   jax.ShapeDtypeStruct((b, SUBLANES, width), F32),
                   jax.ShapeDtypeStruct((b, SUBLANES, width), F32),
                   jax.ShapeDtypeStruct(ya_prev.shape, BF16),
                   jax.ShapeDtypeStruct((depth, bs, nh, dk, dv), F32)],
        prev=(None, None, None, ya_prev, s_prev),
        scratch=[pltpu.VMEM((SUBLANES, cw), F32)],
        sem=("arbitrary", "arbitrary", "arbitrary"))


def _lru_sample(proj, ux_col, l, lru_w, row0, h0_rep, buf_fr, yb_prev):
    conv_w, conv_b3, w_r, b_r3, w_i, b_i3, lam3 = lru_w
    width = lam3.shape[-1]
    lb = w_r.shape[-1]
    ncw = conv_w.shape[1]
    ms = h0_rep.shape[1]
    cw = _pick(width, (1024, 512, 256, 128))
    r = _pick(ms, (256, 128, 64, 32, 16, 8))
    assert row0 % r == 0
    rb0 = row0 // r
    specs = _lru_specs(l, r, cw, ncw, lb, ux_col // cw, (ux_col + width) // cw,
                       lambda ri, cb: rb0 + ri, lambda ri, cb: cb)
    st = pl.BlockSpec((None, r, cw), lambda ri, cb: (l, ri, cb))
    f32rows = pl.BlockSpec((r, cw), lambda ri, cb: (ri, cb))
    return _pcall(
        _lru_sample_kernel, name="lru_sample", grid=(ms // r, width // cw),
        in_specs=specs + [st, st],
        args=(proj, proj, conv_w, conv_b3, w_r, b_r3, w_i, b_i3, lam3, h0_rep, buf_fr),
        out_specs=[pl.BlockSpec((r, cw), lambda ri, cb: (rb0 + ri, cb)), f32rows, f32rows],
        out_shape=[jax.ShapeDtypeStruct(yb_prev.shape, BF16),
                   jax.ShapeDtypeStruct((ms, width), F32), jax.ShapeDtypeStruct((ms, width), F32)],
        prev=(yb_prev,),
        sem=("arbitrary", "arbitrary"))


def _pack_pair(hi, lo):
    hb = lax.bitcast_convert_type(hi.astype(jnp.bfloat16).astype(F32), jnp.uint32)
    lb = lax.bitcast_convert_type(lo.astype(jnp.bfloat16).astype(F32), jnp.uint32)
    return hb | (lb >> 16)


def _unpack_pair(p):
    hi = lax.bitcast_convert_type(p & jnp.uint32(0xFFFF0000), F32)
    lo = lax.bitcast_convert_type(p << 16, F32)
    return hi, lo


def _router_kernel(x_ref, g_ref, wr_ref, o_ref, hp_ref, *, ne):
    hf = _rms(x_ref[...], g_ref[...])
    half = hf.shape[1] // 2
    hp_ref[...] = _pack_pair(hf[:, :half], hf[:, half:])
    h = hf.astype(BF16)
    logits = jnp.dot(h, wr_ref[...].astype(BF16), preferred_element_type=F32)
    lane = lax.broadcasted_iota(jnp.int32, logits.shape, 1).astype(F32)
    neg = F32(-jnp.inf)
    big = F32(LANES)
    l1 = jnp.where(lane < ne, logits, neg)
    m1 = jnp.max(l1, axis=-1, keepdims=True)
    i1 = jnp.min(jnp.where(l1 == m1, lane, big), axis=-1, keepdims=True)
    l2 = jnp.where(lane == i1, neg, l1)
    m2 = jnp.max(l2, axis=-1, keepdims=True)
    i2 = jnp.min(jnp.where(l2 == m2, lane, big), axis=-1, keepdims=True)
    e2 = jnp.exp(m2 - m1)
    den = 1.0 + e2
    o_ref[...] = (jnp.where(lane == 0.0, i1, 0.0) + jnp.where(lane == 1.0, i2, 0.0)
                  + jnp.where(lane == 2.0, 1.0 / den, 0.0) + jnp.where(lane == 3.0, e2 / den, 0.0))


def _router(x, g3, l, wr):
    m, d = x.shape
    ne = wr.shape[-1]
    assert TOP_K == 2 and ne <= LANES
    wr_pad = jnp.pad(wr, ((0, 0), (0, LANES - ne)))
    tm = _pick(m, (512, 256, 128, 64, 32, 16, 8))
    return _pcall(
        functools.partial(_router_kernel, ne=ne), name="router", grid=(m // tm,),
        in_specs=[pl.BlockSpec((tm, d), lambda i: (i, 0)), _vspec(l, d),
                  pl.BlockSpec((d, LANES), lambda i: (0, 0))],
        args=(x, g3, wr_pad),
        out_specs=[pl.BlockSpec((tm, LANES), lambda i: (i, 0)),
                   pl.BlockSpec((tm, d // 2), lambda i: (i, 0))],
        out_shape=[jax.ShapeDtypeStruct((m, LANES), F32),
                   jax.ShapeDtypeStruct((m, d // 2), jnp.uint32)],
        sem=("arbitrary",))


def _dispatch_tables(rout, ne, tg):
    m = rout.shape[0]
    na = TOP_K * m
    eid = rout[:, :TOP_K].astype(jnp.int32).T.reshape(na)
    onehot = (eid[:, None] == jnp.arange(ne, dtype=jnp.int32)[None, :]).astype(jnp.int32)
    cnt = jnp.sum(onehot, axis=0)
    rank = jnp.sum((jnp.cumsum(onehot, axis=0) - onehot) * onehot, axis=1)
    padded = ((cnt + tg - 1) // tg) * tg
    gend = jnp.cumsum(padded)
    gstart = gend - padded
    rem = cnt % tg
    first = jnp.where(rem > 0, rem, tg)
    local = jnp.where(rank < first[eid], rank, rank - first[eid] + tg)
    pos = gstart[eid] + local
    p_rows = ((na + tg - 1) // tg + ne) * tg
    n_tiles = p_rows // tg
    tok = jnp.zeros((p_rows,), jnp.int32).at[pos].set(jnp.arange(na, dtype=jnp.int32) % m)
    n_used = (gend[-1] // tg).astype(jnp.int32)
    tstart = jnp.arange(n_tiles, dtype=jnp.int32) * tg
    tstart = jnp.minimum(tstart, (n_used - 1) * tg)
    te = jnp.minimum(jnp.searchsorted(gend, tstart, side="right"), ne - 1).astype(jnp.int32)
    tidx = jnp.arange(n_tiles, dtype=jnp.int32)
    n_valid = jnp.where(tidx * tg == gstart[te], first[te], tg)
    n_valid = jnp.where(tidx < n_used, n_valid, 0).astype(jnp.int32)
    return pos, tok, te, n_used.reshape(1), n_valid, p_rows


DMA_UNROLL = 8


def _row_gather(src_hbm, dst, idx_of, sem):
    rows = dst.shape[0]
    assert rows % DMA_UNROLL == 0

    def row_copy(r, t):
        return pltpu.make_async_copy(src_hbm.at[pl.ds(t, 1), :], dst.at[pl.ds(r, 1), :], sem)

    def issue(c, carry):
        for u in range(DMA_UNROLL):
            r = c * DMA_UNROLL + u
            row_copy(r, idx_of(r)).start()
        return carry

    def drain(c, carry):
        for u in range(DMA_UNROLL):
            row_copy(c * DMA_UNROLL + u, 0).wait()
        return carry

    start = lambda: lax.fori_loop(0, rows // DMA_UNROLL, issue, 0)
    wait = lambda: lax.fori_loop(0, rows // DMA_UNROLL, drain, 0)
    return start, wait


def _moe_gather_kernel(tok_ref, gv_ref, hp_ref, o_ref):
    tr, d = o_ref.shape
    half = d // 2
    gv = gv_ref[0, 0]
    group = 2 * SUBLANES

    @pl.when(gv < tr)
    def _():
        o_ref[...] = jnp.zeros_like(o_ref)

    def body(g, carry):
        r0 = pl.multiple_of(g * group, group)
        rows = [hp_ref[pl.ds(tok_ref[0, r0 + u], 1), :] for u in range(group)]
        hi, lo = _unpack_pair(jnp.concatenate(rows, axis=0))
        o_ref[pl.ds(r0, group), pl.ds(0, half)] = hi.astype(o_ref.dtype)
        o_ref[pl.ds(r0, group), pl.ds(half, half)] = lo.astype(o_ref.dtype)
        return carry

    lax.fori_loop(0, (gv + group - 1) // group, body, 0)


def _moe_gather(hp, tok, n_valid, tg, p_rows):
    m, half = hp.shape
    tr = _pick(tg, (GATHER_ROWS, 128, 64, 32, 16))
    nt = p_rows // tr
    per = tg // tr
    gidx = jnp.arange(nt, dtype=jnp.int32)
    gv = jnp.clip(n_valid[gidx // per] - (gidx % per) * tr, 0, tr).astype(jnp.int32)
    return _pcall(
        _moe_gather_kernel, name="moe_gather", grid=(nt,),
        in_specs=[pl.BlockSpec((None, 1, tr), lambda i: (i, 0, 0), memory_space=pltpu.SMEM),
                  pl.BlockSpec((None, 1, 1), lambda i: (i, 0, 0), memory_space=pltpu.SMEM),
                  pl.BlockSpec((m, half), lambda i: (0, 0), pipeline_mode=pl.Buffered(1))],
        args=(tok.reshape(nt, 1, tr), gv.reshape(nt, 1, 1), hp),
        out_specs=[pl.BlockSpec((tr, 2 * half), lambda i: (i, 0))],
        out_shape=[jax.ShapeDtypeStruct((p_rows, 2 * half), BF16)],
        sem=("arbitrary",))[0]


def _tile_changed(te_ref):
    ti = pl.program_id(1)
    return jnp.logical_or(ti == 0, te_ref[ti] != te_ref[jnp.maximum(ti - 1, 0)])


def _valid_row_chunks(nv, o_ref, chunk):
    tg = o_ref.shape[0]

    @pl.when(nv == tg)
    def _():
        _row_chunks(tg, chunk)

    @pl.when(nv < tg)
    def _():
        def maybe(rs):
            @pl.when(rs.start < nv)
            def _():
                chunk(rs)

            @pl.when(rs.start >= nv)
            def _():
                o_ref[rs, :] = jnp.zeros((rs.stop - rs.start, o_ref.shape[1]), o_ref.dtype)

        _row_chunks(tg, maybe)


def _gmm_swiglu_kernel(te_ref, nu_ref, nv_ref, x_ref, wg_ref, wu_ref, o_ref, wgb_ref, wub_ref):
    @pl.when(_tile_changed(te_ref))
    def _():
        _cast_w(wg_ref, wgb_ref)
        _cast_w(wu_ref, wub_ref)

    def chunk(rs):
        x = x_ref[rs, :]
        g = jnp.dot(x, wgb_ref[...], preferred_element_type=F32)
        u = jnp.dot(x, wub_ref[...], preferred_element_type=F32)
        o_ref[rs, :] = (jax.nn.silu(g) * u).astype(o_ref.dtype)

    _valid_row_chunks(nv_ref[pl.program_id(1)], o_ref, chunk)


def _gmm_down_kernel(te_ref, nu_ref, nv_ref, x_ref, w_ref, o_ref, wb_ref):
    @pl.when(_tile_changed(te_ref))
    def _():
        _cast_w(w_ref, wb_ref)

    hw = o_ref.shape[1]

    def chunk(rs):
        y = jnp.dot(x_ref[rs, :], wb_ref[...], preferred_element_type=F32)
        o_ref[rs, :] = _pack_pair(y[:, :hw], y[:, hw:])

    _valid_row_chunks(nv_ref[pl.program_id(1)], o_ref, chunk)


def _gmm_specs(jm, tg, k, tn, j0=0):
    xrow = lambda j, ti, te, nu, nv: (jnp.minimum(ti, nu[0] - 1), 0)
    wsp = pl.BlockSpec((None, None, k, tn), lambda j, ti, te, nu, nv: (jm, te[ti], 0, j0 + j))
    return pl.BlockSpec((tg, k), xrow), wsp, pl.BlockSpec((tg, tn), lambda j, ti, te, nu, nv: (ti, j0 + j))


def _gmm_swiglu(xg, wg, wu, jm, te, n_used, n_valid, tg, tns):
    p, k = xg.shape
    n = wg.shape[-1]
    assert sum(tn * cnt for tn, cnt in tns) == n
    out, col = None, 0
    for tn, cnt in tns:
        assert col % tn == 0
        xs, ws, os_ = _gmm_specs(jm, tg, k, tn, col // tn)
        out = _pcall(
            _gmm_swiglu_kernel, name="gmm_swiglu", grid=(cnt, p // tg), num_scalar_prefetch=3,
            in_specs=[None, None, None, xs, ws, ws], args=(te, n_used, n_valid, xg, wg, wu),
            out_specs=[os_], out_shape=[jax.ShapeDtypeStruct((p, n), BF16)],
            scratch=[pltpu.VMEM((k, tn), BF16), pltpu.VMEM((k, tn), BF16)],
            prev=(out,), sem=("arbitrary", "arbitrary"))[0]
        col += tn * cnt
    return out


def _col_tiling(n, widths):
    out, col = [], 0
    for wdt in widths:
        cnt = (n - col) // wdt
        if cnt:
            out.append((wdt, cnt))
            col += cnt * wdt
    assert col == n
    return tuple(out)


def _gmm_down(ug, wd, jm, te, n_used, n_valid, tg, tn):
    p, k = ug.shape
    n = wd.shape[-1]
    xs, ws, _ = _gmm_specs(jm, tg, k, tn)
    return _pcall(
        _gmm_down_kernel, name="gmm_down", grid=(n // tn, p // tg), num_scalar_prefetch=3,
        in_specs=[None, None, None, xs, ws], args=(te, n_used, n_valid, ug, wd),
        out_specs=[pl.BlockSpec((tg, tn // 2), lambda j, ti, te, nu, nv: (ti, j))],
        out_shape=[jax.ShapeDtypeStruct((p, n // 2), jnp.uint32)],
        scratch=[pltpu.VMEM((k, tn), BF16)],
        sem=("arbitrary", "arbitrary"))[0]


def _moe_combine_kernel(pos_ref, nxt_ref, x_ref, rw_ref, g_ref, yg_hbm, *rest, tn, np_tiles):
    outs, (buf, sem) = rest[:-2], rest[-2:]
    tc = x_ref.shape[0]
    i = pl.program_id(0)
    slot = lax.rem(i, 2)

    def gathers(idx_ref, s):
        return [_row_gather(yg_hbm, buf.at[s, k], lambda r, k=k: idx_ref[0, k * tc + r], sem.at[s])
                for k in range(TOP_K)]

    @pl.when(i == 0)
    def _():
        for start, _ in gathers(pos_ref, 0):
            start()

    @pl.when(i + 1 < pl.num_programs(0))
    def _():
        for start, _ in gathers(nxt_ref, 1 - slot):
            start()

    for _, wait in gathers(pos_ref, slot):
        wait()
    rows = buf[slot]

    def expand(p):
        hw = tn // 2
        parts = []
        for jj in range(p.shape[1] // hw):
            parts.extend(_unpack_pair(p[:, jj * hw:(jj + 1) * hw]))
        return jnp.concatenate(parts, axis=-1)

    rw = rw_ref[...]
    y = rw[:, TOP_K:TOP_K + 1] * expand(rows[0])
    for k in range(1, TOP_K):
        y = y + rw[:, TOP_K + k:TOP_K + k + 1] * expand(rows[k])
    xn = x_ref[...] + y
    if np_tiles is None:
        outs[0][...] = xn
    else:
        yn = _rms(xn, g_ref[...])

        @pl.when(i < np_tiles)
        def _():
            outs[0][...] = yn

        @pl.when(i >= np_tiles)
        def _():
            outs[1][...] = yn


def _moe_combine(x, rout, pos, ygp, tn, g3=None, mp=None):
    m, d = x.shape
    tc = _pick(m, (COMBINE_ROWS, 64, 32, 16, 8))
    nt = m // tc
    pos_t = pos.reshape(TOP_K, nt, tc).transpose(1, 0, 2).reshape(nt, 1, TOP_K * tc)
    if g3 is None:
        g3 = jnp.ones((1, 1, d), F32)
        np_tiles = None
        out_specs = [pl.BlockSpec((tc, d), lambda i: (i, 0))]
        out_shape = [jax.ShapeDtypeStruct((m, d), F32)]
    else:
        assert mp % tc == 0
        np_tiles = mp // tc
        out_specs = [pl.BlockSpec((tc, d), lambda i: (jnp.minimum(i, np_tiles - 1), 0)),
                     pl.BlockSpec((tc, d), lambda i: (jnp.maximum(i - np_tiles, 0), 0))]
        out_shape = [jax.ShapeDtypeStruct((mp, d), F32), jax.ShapeDtypeStruct((m - mp, d), F32)]
    return _pcall(
        functools.partial(_moe_combine_kernel, tn=tn, np_tiles=np_tiles), name="moe_combine", grid=(nt,),
        in_specs=[pl.BlockSpec((None, 1, TOP_K * tc), lambda i: (i, 0, 0), memory_space=pltpu.SMEM),
                  pl.BlockSpec((None, 1, TOP_K * tc), lambda i: (jnp.minimum(i + 1, nt - 1), 0, 0),
                               memory_space=pltpu.SMEM),
                  pl.BlockSpec((tc, d), lambda i: (i, 0)),
                  pl.BlockSpec((tc, LANES), lambda i: (i, 0)),
                  _vspec(0, d),
                  pl.BlockSpec(memory_space=pl.ANY)],
        args=(pos_t, pos_t, x, rout, g3, ygp),
        out_specs=out_specs, out_shape=out_shape,
        scratch=[pltpu.VMEM((2, TOP_K, tc, d // 2), jnp.uint32), pltpu.SemaphoreType.DMA((2,))],
        sem=("arbitrary",))


def kernel(x_prompt, x_sample, state_ret, state_lru, state_conv, norm_mix, w_in, ret_gn, w_ret_o, conv_w, conv_b, w_rgate, b_rgate, w_igate, b_igate, lru_lambda, w_lru_o, w_out, norm_ffn, ffn_w_gate, ffn_w_up, ffn_w_down, moe_router, moe_w_gate, moe_w_up, moe_w_down, norm_final):
    bp, tp, d = x_prompt.shape
    bs, ts, _ = x_sample.shape
    depth, _, nh, dk, dv = state_ret.shape
    width = state_lru.shape[-1]
    ncw = conv_w.shape[1]
    ne = moe_router.shape[-1]
    assert dk == dv and nh * dk == width == d and ts == SUBLANES and ncw - 1 <= SUBLANES
    mp, ms = bp * tp, bs * ts
    m = mp + ms
    g_col = 2 * nh * dk + nh * dv
    ux_col = g_col + nh * dv
    ga_col = ux_col + 2 * width

    tm = _pick(m, (1024, 512, 256, 128, 64, 32, 16, 8))
    tm_big = _pick(m, (1536, 1024, 512, 256, 128, 64, 32, 16, 8))
    tm_half = _pick(m, (512, 256, 128, 64, 32, 16, 8))
    tn_of = lambda n: _pick(n, (1024, 512, 256, 128))

    vec3 = lambda a: a.reshape(a.shape[0], 1, a.shape[-1])
    norm_mix3, norm_ffn3, ret_gn3 = vec3(norm_mix), vec3(norm_ffn), vec3(ret_gn)
    norm_final3 = norm_final.reshape(1, 1, d)
    lru_w = (conv_w, vec3(conv_b), w_rgate, vec3(b_rgate), w_igate, vec3(b_igate), vec3(lru_lambda))
    h0_rep = jnp.repeat(state_lru, ts, axis=1)
    buf_fr = jnp.pad(state_conv, ((0, 0), (0, 0), (ts - (ncw - 1), 0), (0, 0))).reshape(depth, ms, width)

    cos_p, sin_p = _rope_tables(tp, dk // 2, 0)
    cos_s, sin_s = _rope_tables(ts, dk // 2, PAST_LEN)
    cos = jnp.concatenate([jnp.tile(cos_p, (bp, 1)), jnp.tile(cos_s, (bs, 1))], axis=0)
    sin = jnp.concatenate([jnp.tile(sin_p, (bp, 1)), jnp.tile(sin_s, (bs, 1))], axis=0)
    tn_in = _pick(nh * dk, (1024, 512, 256))

    x = (x_prompt.reshape(mp, d), x_sample.reshape(ms, d))
    if mp % tm_half or ms % tm_half:
        x = jnp.concatenate(x, axis=0)
    ret_p = ret_s = y_p = y_s = None
    lrus_p, lrus_s, convs_p, convs_s = [], [], [], []
    for l in range(depth):
        if isinstance(x, tuple):
            h = _rmsnorm(x[0], norm_mix3, l, BF16, out_rows=m)
            h = _rmsnorm(x[1], norm_mix3, l, BF16, out_row0=mp, out_rows=m, prev=h)
        else:
            h = _rmsnorm(x, norm_mix3, l, BF16)
        proj = _mm_inproj(h, w_in, (l,), cos, sin, 2 * nh * dk, dk, (g_col, ux_col),
                          (ux_col + width, ga_col), tm_big, tn_in)

        ya, ret_p = _ret_prompt(proj, ret_gn3, l, depth, bp, tp, nh, dk, dv, ret_p)
        if _fused_seqs_per_step(width, tp, bp, bs, ts) is None:
            ya, ret_s = _ret_sample(proj, ret_gn3, state_ret, l, mp, bs, ts, nh, dk, dv, ya, ret_s)
            yb, hl_p, ul_p = _lru_prompt(proj, ux_col, l, lru_w, bp, tp)
        else:
            yb, hl_p, ul_p, ya, ret_s = _lru_prompt_ret_sample(
                proj, ux_col, l, lru_w, bp, tp, ret_gn3, state_ret, mp, bs, ts, nh, dk, dv, ya, ret_s)
        yb, hl_s, us_s = _lru_sample(proj, ux_col, l, lru_w, mp, h0_rep, buf_fr, yb)
        lrus_p.append(hl_p[:, -1])
        lrus_s.append(hl_s.reshape(bs, ts, width)[:, -1])
        convs_p.append(ul_p[:, SUBLANES - (ncw - 1):])
        convs_s.append(us_s.reshape(bs, ts, width)[:, ts - (ncw - 1):])

        z = _mm_merge(ya, yb, w_ret_o, w_lru_o, (l,), proj, ga_col, ga_col + d, tm, tn_of(d) // 2)
        x = _mm_resid(z, w_out, (l,), x, tm_half if isinstance(x, tuple) else tm, tn_of(d))
        j = l // 2
        if l % 2 == 0:
            h2 = _rmsnorm(x, norm_ffn3, l, BF16)
            ff = ffn_w_gate.shape[-1]
            u = _mm_swiglu(h2, ffn_w_gate, ffn_w_up, (j,), tm_big, _pick(ff, (512, 256, 128)))
            x = _mm_resid(u, ffn_w_down, (j,), x, _pick(m, (768, 512, 256, 128, 64, 32, 16, 8)),
                          _pick(d, (512, 256, 128)))
        else:
            fe = moe_w_gate.shape[-1]
            tg = _pick(TOP_K * m, (MOE_TILE, 256, 128, 64, 32, 16))
            tn_d = _pick(d, (1024, 512, 256))
            rout, hp = _router(x, norm_ffn3, l, moe_router[j])
            pos, tok, te, n_used, n_valid, p_rows = _dispatch_tables(rout, ne, tg)
            xg = _moe_gather(hp, tok, n_valid, tg, p_rows)
            ug = _gmm_swiglu(xg, moe_w_gate, moe_w_up, j, te, n_used, n_valid, tg,
                             _col_tiling(fe, (1024, 512, 256, 128)))
            ygp = _gmm_down(ug, moe_w_down, j, te, n_used, n_valid, tg, tn_d)
            if l == depth - 1:
                y_p, y_s = _moe_combine(x, rout, pos, ygp, tn_d, norm_final3, mp)
            else:
                x = _moe_combine(x, rout, pos, ygp, tn_d)[0]
    if y_p is None:
        y_p = _rmsnorm(x, norm_final3, 0, F32, 0, mp)
        y_s = _rmsnorm(x, norm_final3, 0, F32, mp, ms)
    return (y_p.reshape(bp, tp, d), y_s.reshape(bs, ts, d),
            ret_p, jnp.stack(lrus_p), jnp.stack(convs_p),
            ret_s, jnp.stack(lrus_s), jnp.stack(convs_s))
```
